```python
import math
import jax
import jax.numpy as jnp
from jax import lax
import numpy as np

D_MODEL = 1024
BATCH = 16
SEQ = 2048
DEPTH = 2

HEAD_DIM = 64
SB_HEADS = 4
MLA_HEADS = 6
MLA_Q_RANK = 256
MLA_KV_RANK = 128
MLA_NOPE = 64
MLA_ROPE = 32
MLA_V = 64
MLA_QK = MLA_NOPE + MLA_ROPE
ROPE_THETA = 10000.0
SW_HEADS = 6
SW_KV_HEADS = 2
WINDOW = 128
REL_BUCKETS = 32
REL_MAX_DIST = 128
BLOCK = 128
D_FF = 2816
CONV_W = 3
EPS = 1e-6
NEG = -1e30

D_MIX = SB_HEADS * HEAD_DIM + MLA_HEADS * MLA_V + SW_HEADS * HEAD_DIM
IN_SPLITS = (SB_HEADS * HEAD_DIM, SB_HEADS * HEAD_DIM, SB_HEADS * HEAD_DIM,
             MLA_Q_RANK, MLA_KV_RANK, MLA_ROPE,
             SW_HEADS * HEAD_DIM, SW_KV_HEADS * HEAD_DIM, SW_KV_HEADS * HEAD_DIM)
D_IN = 3 * SB_HEADS * HEAD_DIM + MLA_Q_RANK + MLA_KV_RANK + MLA_ROPE + (SW_HEADS + 2 * SW_KV_HEADS) * HEAD_DIM

kernel_name = 'hybrid_sb_mla_swa_convffn_block'


def rms_norm(x, g):
    xf = x.astype(jnp.float32)
    y = xf * lax.rsqrt(jnp.mean(xf * xf, axis=-1, keepdims=True) + EPS)
    return (y * g.astype(jnp.float32)).astype(x.dtype)


def split_cols(t, sizes):
    out = []
    start = 0
    for n in sizes:
        out.append(t[..., start:start + n])
        start += n
    return out


def apply_rope(x, positions):
    half = x.shape[-1] // 2
    inv_freq = jnp.power(ROPE_THETA, -jnp.arange(half, dtype=jnp.float32) / half)
    ang = positions.astype(jnp.float32)[..., None] * inv_freq
    cos = jnp.cos(ang)[:, :, None, :]
    sin = jnp.sin(ang)[:, :, None, :]
    x1 = x[..., :half].astype(jnp.float32)
    x2 = x[..., half:].astype(jnp.float32)
    out = jnp.concatenate([x1 * cos - x2 * sin, x1 * sin + x2 * cos], axis=-1)
    return out.astype(x.dtype)


def t5_causal_bucket(dist):
    max_exact = REL_BUCKETS // 2
    n = jnp.maximum(dist, 0)
    nf = jnp.maximum(n, 1).astype(jnp.float32)
    large = max_exact + (jnp.log(nf / max_exact) / math.log(REL_MAX_DIST / max_exact)
                         * (REL_BUCKETS - max_exact)).astype(jnp.int32)
    large = jnp.minimum(large, REL_BUCKETS - 1)
    return jnp.where(n < max_exact, n, large)


def window_rel_bias(rel_table):
    a = jnp.arange(BLOCK)[:, None]
    b = jnp.arange(2 * BLOCK)[None, :]
    bucket = t5_causal_bucket(BLOCK + a - b)
    return jnp.transpose(rel_table[bucket], (2, 0, 1))


def stick_breaking_attention(q, k, v):
    B, S, H, D = q.shape
    scale = D ** -0.5
    outs = []
    for i in range(S // BLOCK):
        t0 = i * BLOCK
        end = t0 + BLOCK
        z = jnp.einsum('bqhd,bkhd->bhqk', q[:, t0:end], k[:, :end]).astype(jnp.float32) * scale
        strict = jnp.arange(end)[None, :] < (t0 + jnp.arange(BLOCK))[:, None]
        log_keep = jnp.where(strict, -jax.nn.softplus(z), 0.0)
        suffix = lax.cumsum(log_keep, axis=log_keep.ndim - 1, reverse=True) - log_keep
        weights = jnp.where(strict, jnp.exp(jax.nn.log_sigmoid(z) + suffix), 0.0)
        outs.append(jnp.einsum('bhqk,bkhd->bqhd', weights.astype(v.dtype), v[:, :end]))
    return jnp.concatenate(outs, axis=1)


def causal_softmax_attention(q, k, v):
    B, S, H, Dk = q.shape
    scale = Dk ** -0.5
    outs = []
    for i in range(S // BLOCK):
        t0 = i * BLOCK
        end = t0 + BLOCK
        s = jnp.einsum('bqhd,bkhd->bhqk', q[:, t0:end], k[:, :end]).astype(jnp.float32) * scale
        causal = jnp.arange(end)[None, :] <= (t0 + jnp.arange(BLOCK))[:, None]
        p = jax.nn.softmax(jnp.where(causal, s, NEG), axis=-1)
        outs.append(jnp.einsum('bhqk,bkhd->bqhd', p.astype(v.dtype), v[:, :end]))
    return jnp.concatenate(outs, axis=1)


def sliding_window_sink_attention(q, k, v, sinks, rel_bias):
    B, S, H, D = q.shape
    G = k.shape[2]
    R = H // G
    nb = S // BLOCK
    qb = q.reshape(B, nb, BLOCK, G, R, D)

    def band(t):
        tp = jnp.concatenate([jnp.zeros((B, BLOCK, G, D), t.dtype), t], axis=1)
        tp = tp.reshape(B, nb + 1, BLOCK, G, D)
        return jnp.concatenate([tp[:, :-1], tp[:, 1:]], axis=2)

    kb = band(k)
    vb = band(v)
    s = jnp.einsum('bnqgrd,bnkgd->bngrqk', qb, kb).astype(jnp.float32) * (D ** -0.5)
    s = s + rel_bias.astype(jnp.float32).reshape(G, R, BLOCK, 2 * BLOCK)[None, None]
    dist = BLOCK + jnp.arange(BLOCK)[:, None] - jnp.arange(2 * BLOCK)[None, :]
    in_window = (dist >= 0) & (dist < WINDOW)
    key_pos = (jnp.arange(nb)[:, None] - 1) * BLOCK + jnp.arange(2 * BLOCK)[None, :]
    valid = in_window[None] & (key_pos >= 0)[:, None, :]
    s = jnp.where(valid[None, :, None, None], s, NEG)
    sink = jnp.broadcast_to(sinks.astype(jnp.float32).reshape(1, 1, G, R, 1, 1), s.shape[:-1] + (1,))
    p = jax.nn.softmax(jnp.concatenate([s, sink], axis=-1), axis=-1)[..., :-1]
    o = jnp.einsum('bngrqk,bnkgd->bnqgrd', p.astype(v.dtype), vb)
    return o.reshape(B, S, H, D)


def causal_depthwise_conv(u, w, b):
    C = u.shape[-1]
    y = lax.conv_general_dilated(u, w[:, None, :].astype(u.dtype), window_strides=(1,),
                                 padding=[(CONV_W - 1, 0)],
                                 dimension_numbers=('NWC', 'WIO', 'NWC'),
                                 feature_group_count=C)
    return y + b


def hybrid_layer(x, cond, positions, rel_bias, norm1_g, norm2_g, w_ada, b_ada, w_in,
                 mla_cq_g, w_uq, mla_ckv_g, w_ukv, mla_qn_g, mla_kn_g, sw_qn_g, sw_kn_g,
                 sw_sinks, w_out, w_up, conv_w, conv_b, w_down):
    B, S, _ = x.shape
    mods = jnp.einsum('bd,de->be', jax.nn.silu(cond), w_ada) + b_ada
    shift1, scale1, gate1, shift2, scale2, gate2 = jnp.split(mods[:, None, :], 6, axis=-1)

    h = rms_norm(x, norm1_g) * (1.0 + scale1) + shift1
    proj = jnp.einsum('bsd,de->bse', h, w_in)
    sb_q, sb_k, sb_v, cq, ckv, k_rope, sw_q, sw_k, sw_v = split_cols(proj, IN_SPLITS)

    sb_shape = (B, S, SB_HEADS, HEAD_DIM)
    o_a = stick_breaking_attention(sb_q.reshape(sb_shape), sb_k.reshape(sb_shape), sb_v.reshape(sb_shape))

    q_b = jnp.einsum('bsr,re->bse', rms_norm(cq, mla_cq_g), w_uq).reshape(B, S, MLA_HEADS, MLA_QK)
    kv_b = jnp.einsum('bsr,re->bse', rms_norm(ckv, mla_ckv_g), w_ukv).reshape(B, S, MLA_HEADS, MLA_NOPE + MLA_V)
    k_nope = kv_b[..., :MLA_NOPE]
    v_b = kv_b[..., MLA_NOPE:]
    k_rope_h = jnp.broadcast_to(k_rope[:, :, None, :], (B, S, MLA_HEADS, MLA_ROPE))
    k_b = jnp.concatenate([k_nope, k_rope_h], axis=-1)
    q_b = rms_norm(q_b, mla_qn_g)
    k_b = rms_norm(k_b, mla_kn_g)
    q_b = jnp.concatenate([q_b[..., :MLA_NOPE], apply_rope(q_b[..., MLA_NOPE:], positions)], axis=-1)
    k_b = jnp.concatenate([k_b[..., :MLA_NOPE], apply_rope(k_b[..., MLA_NOPE:], positions)], axis=-1)
    o_b = causal_softmax_attention(q_b, k_b, v_b)

    q_c = rms_norm(sw_q.reshape(B, S, SW_HEADS, HEAD_DIM), sw_qn_g)
    k_c = rms_norm(sw_k.reshape(B, S, SW_KV_HEADS, HEAD_DIM), sw_kn_g)
    v_c = sw_v.reshape(B, S, SW_KV_HEADS, HEAD_DIM)
    o_c = sliding_window_sink_attention(q_c, k_c, v_c, sw_sinks, rel_bias)

    mix = jnp.concatenate([o_a.reshape(B, S, -1), o_b.reshape(B, S, -1), o_c.reshape(B, S, -1)], axis=-1)
    x = x + gate1 * jnp.einsum('bse,ed->bsd', mix, w_out)

    h2 = rms_norm(x, norm2_g) * (1.0 + scale2) + shift2
    u = causal_depthwise_conv(jnp.einsum('bsd,df->bsf', h2, w_up), conv_w, conv_b)
    g = u[..., :D_FF]
    val = u[..., D_FF:]
    y = jnp.einsum('bsf,fd->bsd', jax.nn.silu(g) * val, w_down)
    return x + gate2 * y


def _fwd_setup_inputs(seed: int = 0) -> dict:
    key = jax.random.key(seed)
    ks = jax.random.split(key, 24)
    f32 = jnp.float32
    L = DEPTH
    D = D_MODEL

    def nrm(k, shape, scale):
        return jax.random.normal(k, shape, f32) * scale

    def gain(k, shape):
        return 1.0 + 0.02 * jax.random.normal(k, shape, f32)

    x = nrm(ks[0], (BATCH, SEQ, D), 1.0)
    c = nrm(ks[1], (BATCH, D), 1.0)
    offsets = jax.random.randint(ks[2], (BATCH, 1), 0, SEQ, dtype=jnp.int32)
    positions = offsets + jnp.arange(SEQ, dtype=jnp.int32)[None, :]
    rel_table = nrm(ks[3], (REL_BUCKETS, SW_HEADS), 0.5)
    norm1_g = gain(ks[4], (L, D))
    norm2_g = gain(ks[5], (L, D))
    w_ada = nrm(ks[6], (L, D, 6 * D), 0.5 * D ** -0.5)
    b_ada = nrm(ks[7], (L, 6 * D), 0.02)
    w_in = nrm(ks[8], (L, D, D_IN), D ** -0.5)
    mla_cq_g = gain(ks[9], (L, MLA_Q_RANK))
    w_uq = nrm(ks[10], (L, MLA_Q_RANK, MLA_HEADS * MLA_QK), MLA_Q_RANK ** -0.5)
    mla_ckv_g = gain(ks[11], (L, MLA_KV_RANK))
    w_ukv = nrm(ks[12], (L, MLA_KV_RANK, MLA_HEADS * (MLA_NOPE + MLA_V)), MLA_KV_RANK ** -0.5)
    mla_qn_g = gain(ks[13], (L, MLA_QK))
    mla_kn_g = gain(ks[14], (L, MLA_QK))
    sw_qn_g = gain(ks[15], (L, HEAD_DIM))
    sw_kn_g = gain(ks[16], (L, HEAD_DIM))
    sw_sinks = nrm(ks[17], (L, SW_HEADS), 1.0)
    w_out = nrm(ks[18], (L, D_MIX, D), D_MIX ** -0.5)
    w_up = nrm(ks[19], (L, D, 2 * D_FF), D ** -0.5)
    conv_w = nrm(ks[20], (L, CONV_W, 2 * D_FF), CONV_W ** -0.5)
    conv_b = nrm(ks[21], (L, 2 * D_FF), 0.02)
    w_down = nrm(ks[22], (L, D_FF, D), D_FF ** -0.5)
    return {'x': x, 'c': c, 'positions': positions, 'rel_table': rel_table,
            'norm1_g': norm1_g, 'norm2_g': norm2_g, 'w_ada': w_ada, 'b_ada': b_ada,
            'w_in': w_in, 'mla_cq_g': mla_cq_g, 'w_uq': w_uq, 'mla_ckv_g': mla_ckv_g,
            'w_ukv': w_ukv, 'mla_qn_g': mla_qn_g, 'mla_kn_g': mla_kn_g,
            'sw_qn_g': sw_qn_g, 'sw_kn_g': sw_kn_g, 'sw_sinks': sw_sinks,
            'w_out': w_out, 'w_up': w_up, 'conv_w': conv_w, 'conv_b': conv_b,
            'w_down': w_down}


def _fwd_reference(x, c, positions, rel_table, norm1_g, norm2_g, w_ada, b_ada, w_in,
              mla_cq_g, w_uq, mla_ckv_g, w_ukv, mla_qn_g, mla_kn_g, sw_qn_g, sw_kn_g,
              sw_sinks, w_out, w_up, conv_w, conv_b, w_down):
    rel_bias = window_rel_bias(rel_table)
    for l in range(DEPTH):
        x = hybrid_layer(x, c, positions, rel_bias, norm1_g[l], norm2_g[l], w_ada[l], b_ada[l],
                         w_in[l], mla_cq_g[l], w_uq[l], mla_ckv_g[l], w_ukv[l], mla_qn_g[l],
                         mla_kn_g[l], sw_qn_g[l], sw_kn_g[l], sw_sinks[l], w_out[l], w_up[l],
                         conv_w[l], conv_b[l], w_down[l])
    return x


import jax as _jax
import jax.numpy as _jnp

TWIN_FORMAT = 'train_step'
FWD_PARAMS = ['x', 'c', 'positions', 'rel_table', 'norm1_g', 'norm2_g', 'w_ada', 'b_ada', 'w_in', 'mla_cq_g', 'w_uq', 'mla_ckv_g', 'w_ukv', 'mla_qn_g', 'mla_kn_g', 'sw_qn_g', 'sw_kn_g', 'sw_sinks', 'w_out', 'w_up', 'conv_w', 'conv_b', 'w_down']
TWIN_WEIGHTS = ['rel_table', 'norm1_g', 'norm2_g', 'w_ada', 'b_ada', 'w_in', 'mla_cq_g', 'w_uq', 'mla_ckv_g', 'w_ukv', 'mla_qn_g', 'mla_kn_g', 'sw_qn_g', 'sw_kn_g', 'sw_sinks', 'w_out', 'w_up', 'conv_w', 'conv_b', 'w_down']
TWIN_DIFF_INPUT = 'x'
TWIN_INPUTS = ['x', 'c', 'positions', 'rel_table', 'norm1_g', 'norm2_g', 'w_ada', 'b_ada', 'w_in', 'mla_cq_g', 'w_uq', 'mla_ckv_g', 'w_ukv', 'mla_qn_g', 'mla_kn_g', 'sw_qn_g', 'sw_kn_g', 'sw_sinks', 'w_out', 'w_up', 'conv_w', 'conv_b', 'w_down', 'loss_target', 'm_rel_table', 'm_norm1_g', 'm_norm2_g', 'm_w_ada', 'm_b_ada', 'm_w_in', 'm_mla_cq_g', 'm_w_uq', 'm_mla_ckv_g', 'm_w_ukv', 'm_mla_qn_g', 'm_mla_kn_g', 'm_sw_qn_g', 'm_sw_kn_g', 'm_sw_sinks', 'm_w_out', 'm_w_up', 'm_conv_w', 'm_conv_b', 'm_w_down', 'v_rel_table', 'v_norm1_g', 'v_norm2_g', 'v_w_ada', 'v_b_ada', 'v_w_in', 'v_mla_cq_g', 'v_w_uq', 'v_mla_ckv_g', 'v_w_ukv', 'v_mla_qn_g', 'v_mla_kn_g', 'v_sw_qn_g', 'v_sw_kn_g', 'v_sw_sinks', 'v_w_out', 'v_w_up', 'v_conv_w', 'v_conv_b', 'v_w_down']
TWIN_OUTPUTS = ['loss', 'grad_x', 'grad_rel_table', 'grad_norm1_g', 'grad_norm2_g', 'grad_w_ada', 'grad_b_ada', 'grad_w_in', 'grad_mla_cq_g', 'grad_w_uq', 'grad_mla_ckv_g', 'grad_w_ukv', 'grad_mla_qn_g', 'grad_mla_kn_g', 'grad_sw_qn_g', 'grad_sw_kn_g', 'grad_sw_sinks', 'grad_w_out', 'grad_w_up', 'grad_conv_w', 'grad_conv_b', 'grad_w_down', 'delta_rel_table', 'delta_norm1_g', 'delta_norm2_g', 'delta_w_ada', 'delta_b_ada', 'delta_w_in', 'delta_mla_cq_g', 'delta_w_uq', 'delta_mla_ckv_g', 'delta_w_ukv', 'delta_mla_qn_g', 'delta_mla_kn_g', 'delta_sw_qn_g', 'delta_sw_kn_g', 'delta_sw_sinks', 'delta_w_out', 'delta_w_up', 'delta_conv_w', 'delta_conv_b', 'delta_w_down', 'new_m_rel_table', 'new_m_norm1_g', 'new_m_norm2_g', 'new_m_w_ada', 'new_m_b_ada', 'new_m_w_in', 'new_m_mla_cq_g', 'new_m_w_uq', 'new_m_mla_ckv_g', 'new_m_w_ukv', 'new_m_mla_qn_g', 'new_m_mla_kn_g', 'new_m_sw_qn_g', 'new_m_sw_kn_g', 'new_m_sw_sinks', 'new_m_w_out', 'new_m_w_up', 'new_m_conv_w', 'new_m_conv_b', 'new_m_w_down', 'new_v_rel_table', 'new_v_norm1_g', 'new_v_norm2_g', 'new_v_w_ada', 'new_v_b_ada', 'new_v_w_in', 'new_v_mla_cq_g', 'new_v_w_uq', 'new_v_mla_ckv_g', 'new_v_w_ukv', 'new_v_mla_qn_g', 'new_v_mla_kn_g', 'new_v_sw_qn_g', 'new_v_sw_kn_g', 'new_v_sw_sinks', 'new_v_w_out', 'new_v_w_up', 'new_v_conv_w', 'new_v_conv_b', 'new_v_w_down']
TWIN_LEAF_KINDS = {'loss': 'loss', 'grad_x': 'grad_x', 'grad_rel_table': 'grad_w', 'grad_norm1_g': 'grad_w', 'grad_norm2_g': 'grad_w', 'grad_w_ada': 'grad_w', 'grad_b_ada': 'grad_w', 'grad_w_in': 'grad_w', 'grad_mla_cq_g': 'grad_w', 'grad_w_uq': 'grad_w', 'grad_mla_ckv_g': 'grad_w', 'grad_w_ukv': 'grad_w', 'grad_mla_qn_g': 'grad_w', 'grad_mla_kn_g': 'grad_w', 'grad_sw_qn_g': 'grad_w', 'grad_sw_kn_g': 'grad_w', 'grad_sw_sinks': 'grad_w', 'grad_w_out': 'grad_w', 'grad_w_up': 'grad_w', 'grad_conv_w': 'grad_w', 'grad_conv_b': 'grad_w', 'grad_w_down': 'grad_w', 'delta_rel_table': 'delta_w', 'delta_norm1_g': 'delta_w', 'delta_norm2_g': 'delta_w', 'delta_w_ada': 'delta_w', 'delta_b_ada': 'delta_w', 'delta_w_in': 'delta_w', 'delta_mla_cq_g': 'delta_w', 'delta_w_uq': 'delta_w', 'delta_mla_ckv_g': 'delta_w', 'delta_w_ukv': 'delta_w', 'delta_mla_qn_g': 'delta_w', 'delta_mla_kn_g': 'delta_w', 'delta_sw_qn_g': 'delta_w', 'delta_sw_kn_g': 'delta_w', 'delta_sw_sinks': 'delta_w', 'delta_w_out': 'delta_w', 'delta_w_up': 'delta_w', 'delta_conv_w': 'delta_w', 'delta_conv_b': 'delta_w', 'delta_w_down': 'delta_w', 'new_m_rel_table': 'new_m', 'new_m_norm1_g': 'new_m', 'new_m_norm2_g': 'new_m', 'new_m_w_ada': 'new_m', 'new_m_b_ada': 'new_m', 'new_m_w_in': 'new_m', 'new_m_mla_cq_g': 'new_m', 'new_m_w_uq': 'new_m', 'new_m_mla_ckv_g': 'new_m', 'new_m_w_ukv': 'new_m', 'new_m_mla_qn_g': 'new_m', 'new_m_mla_kn_g': 'new_m', 'new_m_sw_qn_g': 'new_m', 'new_m_sw_kn_g': 'new_m', 'new_m_sw_sinks': 'new_m', 'new_m_w_out': 'new_m', 'new_m_w_up': 'new_m', 'new_m_conv_w': 'new_m', 'new_m_conv_b': 'new_m', 'new_m_w_down': 'new_m', 'new_v_rel_table': 'new_v', 'new_v_norm1_g': 'new_v', 'new_v_norm2_g': 'new_v', 'new_v_w_ada': 'new_v', 'new_v_b_ada': 'new_v', 'new_v_w_in': 'new_v', 'new_v_mla_cq_g': 'new_v', 'new_v_w_uq': 'new_v', 'new_v_mla_ckv_g': 'new_v', 'new_v_w_ukv': 'new_v', 'new_v_mla_qn_g': 'new_v', 'new_v_mla_kn_g': 'new_v', 'new_v_sw_qn_g': 'new_v', 'new_v_sw_kn_g': 'new_v', 'new_v_sw_sinks': 'new_v', 'new_v_w_out': 'new_v', 'new_v_w_up': 'new_v', 'new_v_conv_w': 'new_v', 'new_v_conv_b': 'new_v', 'new_v_w_down': 'new_v'}


def _forward(args):
    return _fwd_reference(*[args[k] for k in FWD_PARAMS])


def _output_shape():
    out = _jax.eval_shape(lambda: _forward(_fwd_setup_inputs(0)))
    return out.shape, out.dtype

N_MICROBATCH = 1
ADAM_LR = 0.001
ADAM_B1 = 0.9
ADAM_B2 = 0.999
ADAM_EPS = 1e-08
ADAM_WD = 0.01
ADAM_STEP = 10
PER_EXAMPLE_BATCH_AXIS = {'x': 0, 'c': 0, 'positions': 0, 'loss_target': 0}
SHARED_INPUTS = []
_WEIGHT_DTYPES = {'rel_table': _jnp.float32, 'norm1_g': _jnp.float32, 'norm2_g': _jnp.float32, 'w_ada': _jnp.float32, 'b_ada': _jnp.float32, 'w_in': _jnp.float32, 'mla_cq_g': _jnp.float32, 'w_uq': _jnp.float32, 'mla_ckv_g': _jnp.float32, 'w_ukv': _jnp.float32, 'mla_qn_g': _jnp.float32, 'mla_kn_g': _jnp.float32, 'sw_qn_g': _jnp.float32, 'sw_kn_g': _jnp.float32, 'sw_sinks': _jnp.float32, 'w_out': _jnp.float32, 'w_up': _jnp.float32, 'conv_w': _jnp.float32, 'conv_b': _jnp.float32, 'w_down': _jnp.float32}
MOMENT_SCALE = {'rel_table': 7.238685e-02, 'norm1_g': 3.780905e-01, 'norm2_g': 3.604513e+00, 'w_ada': 9.961432e-01, 'b_ada': 2.124496e+00, 'w_in': 2.619752e-01, 'mla_cq_g': 2.084689e-02, 'w_uq': 1.386595e-02, 'mla_ckv_g': 1.097873e+00, 'w_ukv': 2.247195e-01, 'mla_qn_g': 6.144140e-02, 'mla_kn_g': 6.150689e-02, 'sw_qn_g': 2.222110e-01, 'sw_kn_g': 2.214988e-01, 'sw_sinks': 1.378860e-01, 'w_out': 2.862037e-01, 'w_up': 1.217979e-01, 'conv_w': 5.338452e-01, 'conv_b': 4.508579e-01, 'w_down': 9.809632e-02}


def _to_microbatches(a, axis):
    t = _jnp.moveaxis(a, axis, 0)
    t = t.reshape((N_MICROBATCH, t.shape[0] // N_MICROBATCH) + t.shape[1:])
    return _jnp.moveaxis(t, 1, axis + 1)


def setup_inputs(seed: int = 0) -> dict:
    inp = _fwd_setup_inputs(seed)
    key = _jax.random.fold_in(_jax.random.key(seed), 7919)
    shape, _ = _output_shape()
    out = dict(inp)
    out["loss_target"] = _jax.random.normal(_jax.random.fold_in(key, 0), shape, _jnp.float32)
    for i, name in enumerate(TWIN_WEIGHTS):
        w = inp[name].astype(_jnp.float32)
        if MOMENT_SCALE is None:
            s = _jnp.sqrt(_jnp.mean(_jnp.square(w)) + 1e-30)
        else:
            s = MOMENT_SCALE[name]
        km, kv = _jax.random.split(_jax.random.fold_in(key, i + 1))
        out[name] = w
        out["m_" + name] = s * _jax.random.normal(km, w.shape, _jnp.float32)
        out["v_" + name] = (s * s) * _jax.random.uniform(kv, w.shape, _jnp.float32, 0.5, 1.5)
    if N_MICROBATCH > 1:
        for name, axis in PER_EXAMPLE_BATCH_AXIS.items():
            out[name] = _to_microbatches(out[name], axis)
    return {'x': out['x'], 'c': out['c'], 'positions': out['positions'], 'rel_table': out['rel_table'], 'norm1_g': out['norm1_g'], 'norm2_g': out['norm2_g'], 'w_ada': out['w_ada'], 'b_ada': out['b_ada'], 'w_in': out['w_in'], 'mla_cq_g': out['mla_cq_g'], 'w_uq': out['w_uq'], 'mla_ckv_g': out['mla_ckv_g'], 'w_ukv': out['w_ukv'], 'mla_qn_g': out['mla_qn_g'], 'mla_kn_g': out['mla_kn_g'], 'sw_qn_g': out['sw_qn_g'], 'sw_kn_g': out['sw_kn_g'], 'sw_sinks': out['sw_sinks'], 'w_out': out['w_out'], 'w_up': out['w_up'], 'conv_w': out['conv_w'], 'conv_b': out['conv_b'], 'w_down': out['w_down'], 'loss_target': out['loss_target'], 'm_rel_table': out['m_rel_table'], 'm_norm1_g': out['m_norm1_g'], 'm_norm2_g': out['m_norm2_g'], 'm_w_ada': out['m_w_ada'], 'm_b_ada': out['m_b_ada'], 'm_w_in': out['m_w_in'], 'm_mla_cq_g': out['m_mla_cq_g'], 'm_w_uq': out['m_w_uq'], 'm_mla_ckv_g': out['m_mla_ckv_g'], 'm_w_ukv': out['m_w_ukv'], 'm_mla_qn_g': out['m_mla_qn_g'], 'm_mla_kn_g': out['m_mla_kn_g'], 'm_sw_qn_g': out['m_sw_qn_g'], 'm_sw_kn_g': out['m_sw_kn_g'], 'm_sw_sinks': out['m_sw_sinks'], 'm_w_out': out['m_w_out'], 'm_w_up': out['m_w_up'], 'm_conv_w': out['m_conv_w'], 'm_conv_b': out['m_conv_b'], 'm_w_down': out['m_w_down'], 'v_rel_table': out['v_rel_table'], 'v_norm1_g': out['v_norm1_g'], 'v_norm2_g': out['v_norm2_g'], 'v_w_ada': out['v_w_ada'], 'v_b_ada': out['v_b_ada'], 'v_w_in': out['v_w_in'], 'v_mla_cq_g': out['v_mla_cq_g'], 'v_w_uq': out['v_w_uq'], 'v_mla_ckv_g': out['v_mla_ckv_g'], 'v_w_ukv': out['v_w_ukv'], 'v_mla_qn_g': out['v_mla_qn_g'], 'v_mla_kn_g': out['v_mla_kn_g'], 'v_sw_qn_g': out['v_sw_qn_g'], 'v_sw_kn_g': out['v_sw_kn_g'], 'v_sw_sinks': out['v_sw_sinks'], 'v_w_out': out['v_w_out'], 'v_w_up': out['v_w_up'], 'v_conv_w': out['v_conv_w'], 'v_conv_b': out['v_conv_b'], 'v_w_down': out['v_w_down']}


def _loss(weights, diff, rest, loss_target):
    with _jax.named_scope("forward"):
        args = {**rest, TWIN_DIFF_INPUT: diff, **{k: w.astype(_WEIGHT_DTYPES[k]) for k, w in weights.items()}}
        y = _forward(args)
    with _jax.named_scope("loss_head"):
        err = _jnp.square(y.astype(_jnp.float32) - loss_target)
        return 0.5 * _jnp.sum(_jnp.mean(err, axis=-1)) if err.ndim else 0.5 * err


def _adamw(w, g, m, v):
    m = ADAM_B1 * m + (1.0 - ADAM_B1) * g
    v = ADAM_B2 * v + (1.0 - ADAM_B2) * _jnp.square(g)
    m_hat = m / (1.0 - ADAM_B1 ** ADAM_STEP)
    v_hat = v / (1.0 - ADAM_B2 ** ADAM_STEP)
    delta = -ADAM_LR * (m_hat / (_jnp.sqrt(v_hat) + ADAM_EPS) + ADAM_WD * w)
    return delta, m, v


def reference(x, c, positions, rel_table, norm1_g, norm2_g, w_ada, b_ada, w_in, mla_cq_g, w_uq, mla_ckv_g, w_ukv, mla_qn_g, mla_kn_g, sw_qn_g, sw_kn_g, sw_sinks, w_out, w_up, conv_w, conv_b, w_down, loss_target, m_rel_table, m_norm1_g, m_norm2_g, m_w_ada, m_b_ada, m_w_in, m_mla_cq_g, m_w_uq, m_mla_ckv_g, m_w_ukv, m_mla_qn_g, m_mla_kn_g, m_sw_qn_g, m_sw_kn_g, m_sw_sinks, m_w_out, m_w_up, m_conv_w, m_conv_b, m_w_down, v_rel_table, v_norm1_g, v_norm2_g, v_w_ada, v_b_ada, v_w_in, v_mla_cq_g, v_w_uq, v_mla_ckv_g, v_w_ukv, v_mla_qn_g, v_mla_kn_g, v_sw_qn_g, v_sw_kn_g, v_sw_sinks, v_w_out, v_w_up, v_conv_w, v_conv_b, v_w_down):
    given = dict(x=x, c=c, positions=positions, rel_table=rel_table, norm1_g=norm1_g, norm2_g=norm2_g, w_ada=w_ada, b_ada=b_ada, w_in=w_in, mla_cq_g=mla_cq_g, w_uq=w_uq, mla_ckv_g=mla_ckv_g, w_ukv=w_ukv, mla_qn_g=mla_qn_g, mla_kn_g=mla_kn_g, sw_qn_g=sw_qn_g, sw_kn_g=sw_kn_g, sw_sinks=sw_sinks, w_out=w_out, w_up=w_up, conv_w=conv_w, conv_b=conv_b, w_down=w_down, loss_target=loss_target, m_rel_table=m_rel_table, m_norm1_g=m_norm1_g, m_norm2_g=m_norm2_g, m_w_ada=m_w_ada, m_b_ada=m_b_ada, m_w_in=m_w_in, m_mla_cq_g=m_mla_cq_g, m_w_uq=m_w_uq, m_mla_ckv_g=m_mla_ckv_g, m_w_ukv=m_w_ukv, m_mla_qn_g=m_mla_qn_g, m_mla_kn_g=m_mla_kn_g, m_sw_qn_g=m_sw_qn_g, m_sw_kn_g=m_sw_kn_g, m_sw_sinks=m_sw_sinks, m_w_out=m_w_out, m_w_up=m_w_up, m_conv_w=m_conv_w, m_conv_b=m_conv_b, m_w_down=m_w_down, v_rel_table=v_rel_table, v_norm1_g=v_norm1_g, v_norm2_g=v_norm2_g, v_w_ada=v_w_ada, v_b_ada=v_b_ada, v_w_in=v_w_in, v_mla_cq_g=v_mla_cq_g, v_w_uq=v_w_uq, v_mla_ckv_g=v_mla_ckv_g, v_w_ukv=v_w_ukv, v_mla_qn_g=v_mla_qn_g, v_mla_kn_g=v_mla_kn_g, v_sw_qn_g=v_sw_qn_g, v_sw_kn_g=v_sw_kn_g, v_sw_sinks=v_sw_sinks, v_w_out=v_w_out, v_w_up=v_w_up, v_conv_w=v_conv_w, v_conv_b=v_conv_b, v_w_down=v_w_down)
    weights = {n: given[n] for n in TWIN_WEIGHTS}
    shared = {n: given[n] for n in SHARED_INPUTS}
    per_example = {n: given[n] for n in ['x', 'c', 'positions']}
    grad_fn = _jax.value_and_grad(_loss, argnums=(0, 1))

    def one_microbatch(ex, loss_target):
        ex = dict(ex)
        diff = ex.pop(TWIN_DIFF_INPUT)
        return grad_fn(weights, diff, {**shared, **ex}, loss_target)

    if N_MICROBATCH == 1:
        loss, (grad_w, grad_x) = one_microbatch(per_example, given["loss_target"])
    else:
        def body(carry, xs):
            loss_sum, grad_sum = carry
            l_k, (gw_k, gx_k) = one_microbatch(xs[0], xs[1])
            with _jax.named_scope("update"):
                return (loss_sum + l_k, _jax.tree.map(_jnp.add, grad_sum, gw_k)), gx_k

        init = (_jnp.zeros((), _jnp.float32), _jax.tree.map(_jnp.zeros_like, weights))
        (loss, grad_w), grad_x = _jax.lax.scan(body, init, (per_example, given["loss_target"]))
    with _jax.named_scope("update"):
        delta_w, new_m, new_v = {}, {}, {}
        for n in TWIN_WEIGHTS:
            delta_w[n], new_m[n], new_v[n] = _adamw(weights[n], grad_w[n], given["m_" + n], given["v_" + n])
    return (loss, grad_x, *[grad_w[n] for n in TWIN_WEIGHTS], *[delta_w[n] for n in TWIN_WEIGHTS],
            *[new_m[n] for n in TWIN_WEIGHTS], *[new_v[n] for n in TWIN_WEIGHTS])
```

```python
import math

import jax
import jax.numpy as jnp
from jax import lax
from jax.experimental import pallas as pl
from jax.experimental.pallas import tpu as pltpu

F32 = jnp.float32
MXU = jnp.bfloat16
EPS = 1e-6
NEG = -1e30
VMEM_LIMIT_BYTES = 56 * 1024 * 1024
N_DEV = 8
MESH = pl.DeviceIdType.MESH

D_MODEL = 1024
D_FF = 2816
HEAD = 64
LANES = 128
MLA_HEADS = 6
MLA_QK = 96
SW_HEADS = 6
REL_BUCKETS = 32
BLOCK = 128
SB_SCALE = HEAD ** -0.5
SW_SCALE = HEAD ** -0.5
MLA_SCALE = MLA_QK ** -0.5
ROPE_THETA = 10000.0
D_IN_PAD = 2048
COL_SBQ, COL_SBK, COL_SBV, COL_CQ, COL_CKV, COL_SWQ, COL_SWK, COL_SWV, COL_KR = 0, 256, 512, 768, 1024, 1152, 1536, 1664, 1792

ADAM_LR, ADAM_B1, ADAM_B2, ADAM_EPS, ADAM_WD, ADAM_STEP = 0.001, 0.9, 0.999, 1e-08, 0.01, 10


def _cp(*sem):
    return pltpu.CompilerParams(dimension_semantics=sem, vmem_limit_bytes=VMEM_LIMIT_BYTES)


def _iota(shape, dim):
    return lax.broadcasted_iota(jnp.int32, shape, dim)


def _dot(a, b):
    return jnp.dot(a, b, preferred_element_type=F32)


def _dot_nt(a, b):
    return lax.dot_general(a, b, (((1,), (1,)), ((), ())), preferred_element_type=F32)


def _dot_tn(a, b):
    return lax.dot_general(a, b, (((0,), (0,)), ((), ())), preferred_element_type=F32)


def _cumdot(x, u):
    hi = x.astype(MXU)
    r = x - hi.astype(F32)
    mid = r.astype(MXU)
    lo = (r - mid.astype(F32)).astype(MXU)
    return _dot(hi, u) + _dot(mid, u) + _dot(lo, u)


def _sigmoid(x):
    return 1.0 / (1.0 + jnp.exp(-x))


def _rsum(x):
    return jnp.sum(x, axis=-1, keepdims=True)


def _csum(x):
    return jnp.sum(x, axis=0, keepdims=True)


def _all_gather(xs, name):
    m, n = xs.shape

    def body(x_ref, out_ref, send_sems, recv_sems, local_sem):
        x, y, c = lax.axis_index("x"), lax.axis_index("y"), lax.axis_index("c")
        me, sibling = (x, y, c), (x, y, 1 - c)
        chips = [(1 - x, y), (x, 1 - y), (1 - x, 1 - y)]

        def slot(px, py, pc):
            return out_ref.at[4 * px + 2 * py + pc]

        def copy(k, block, to, src=None):
            return pltpu.make_async_remote_copy(
                src_ref=slot(*block) if src is None else src, dst_ref=slot(*block),
                send_sem=send_sems.at[k], recv_sem=recv_sems.at[k], device_id=to, device_id_type=MESH)

        mine = pltpu.make_async_copy(x_ref, slot(*me), local_sem)
        mine.start()
        first = [copy(0, me, sibling, src=x_ref)]
        first += [copy(1 + j, me, (*chip, c), src=x_ref) for j, chip in enumerate(chips)]
        for cp in first:
            cp.start()
        passed = [copy(4 + j, (*chip, c), sibling) for j, chip in enumerate(chips)]
        for j, chip in enumerate(chips):
            copy(1 + j, (*chip, c), me).wait_recv()
            passed[j].start()
        copy(0, sibling, me).wait_recv()
        for j, chip in enumerate(chips):
            copy(4 + j, (*chip, 1 - c), me).wait_recv()
        for cp in first + passed:
            cp.wait_send()
        mine.wait()

    return pl.pallas_call(
        body, name=name, out_shape=jax.ShapeDtypeStruct((N_DEV, m, n), xs.dtype),
        in_specs=[pl.BlockSpec(memory_space=pl.ANY)], out_specs=pl.BlockSpec(memory_space=pl.ANY),
        scratch_shapes=[pltpu.SemaphoreType.DMA((7,)), pltpu.SemaphoreType.DMA((7,)), pltpu.SemaphoreType.DMA],
    )(xs)


def _all_to_all(xs, name):
    _, m, n = xs.shape
    flips = [(dx, dy, dc) for dx in (0, 1) for dy in (0, 1) for dc in (0, 1) if dx or dy or dc]

    def body(x_ref, out_ref, send_sems, recv_sems, local_sem):
        x, y, c = lax.axis_index("x"), lax.axis_index("y"), lax.axis_index("c")
        me = 4 * x + 2 * y + c
        mine = pltpu.make_async_copy(x_ref.at[me], out_ref.at[me], local_sem)
        mine.start()
        copies = []
        for k, (dx, dy, dc) in enumerate(flips):
            px = 1 - x if dx else x
            py = 1 - y if dy else y
            pc = 1 - c if dc else c
            cp = pltpu.make_async_remote_copy(
                src_ref=x_ref.at[4 * px + 2 * py + pc], dst_ref=out_ref.at[me],
                send_sem=send_sems.at[k], recv_sem=recv_sems.at[k], device_id=(px, py, pc), device_id_type=MESH)
            cp.start()
            copies.append(cp)
        for cp in copies:
            cp.wait()
        mine.wait()

    return pl.pallas_call(
        body, name=name, out_shape=jax.ShapeDtypeStruct(xs.shape, xs.dtype),
        in_specs=[pl.BlockSpec(memory_space=pl.ANY)], out_specs=pl.BlockSpec(memory_space=pl.ANY),
        scratch_shapes=[pltpu.SemaphoreType.DMA((7,)), pltpu.SemaphoreType.DMA((7,)), pltpu.SemaphoreType.DMA],
    )(xs)


def _ada_fwd(c_all, w_ada, b_my, name):
    nl, d, n = w_ada.shape
    nb = c_all.shape[0]

    def body(c_ref, w_ref, b_ref, o_ref):
        cv = c_ref[...]
        sc = (cv * _sigmoid(cv)).astype(MXU)
        o_ref[...] = _dot(sc, w_ref[...].astype(MXU)) + b_ref[...]

    return pl.pallas_call(
        body, name=name, grid=(nl,),
        in_specs=[pl.BlockSpec((nb, d), lambda l: (0, 0)),
                  pl.BlockSpec((None, d, n), lambda l: (l, 0, 0)),
                  pl.BlockSpec((None, 1, n), lambda l: (l, 0, 0))],
        out_specs=pl.BlockSpec((None, nb, n), lambda l: (l, 0, 0)),
        out_shape=jax.ShapeDtypeStruct((nl, nb, n), F32),
        compiler_params=_cp("parallel"))(c_all, w_ada, b_my)


def _ada_bwd(c_all, dmods_my, dmods_all, name):
    nl, nb, n = dmods_my.shape
    d = c_all.shape[1]
    nfull = dmods_all.shape[2]

    def body(c_ref, dm_ref, da_ref, dw_ref, db_ref):
        cv = c_ref[...]
        sc = (cv * _sigmoid(cv)).astype(MXU)
        dw_ref[...] = _dot_tn(sc, dm_ref[...].astype(MXU))
        db_ref[...] = _csum(da_ref[...])

    return pl.pallas_call(
        body, name=name, grid=(nl,),
        in_specs=[pl.BlockSpec((nb, d), lambda l: (0, 0)),
                  pl.BlockSpec((None, nb, n), lambda l: (l, 0, 0)),
                  pl.BlockSpec((None, nb, nfull), lambda l: (l, 0, 0))],
        out_specs=[pl.BlockSpec((None, d, n), lambda l: (l, 0, 0)),
                   pl.BlockSpec((None, 1, nfull), lambda l: (l, 0, 0))],
        out_shape=[jax.ShapeDtypeStruct((nl, d, n), F32), jax.ShapeDtypeStruct((nl, 1, nfull), F32)],
        compiler_params=_cp("parallel"))(c_all, dmods_my, dmods_all)


def _ln_mod_matmul(x, g, scale, shift, w, name):
    nb, s, d = x.shape
    n = w.shape[1]
    tm, tn = min(512, s), 512

    def body(x_ref, g_ref, sc_ref, sh_ref, w_ref, y_ref, h_ref, h_s):
        @pl.when(pl.program_id(2) == 0)
        def _():
            xf = x_ref[...]
            rstd = lax.rsqrt(jnp.mean(xf * xf, axis=-1, keepdims=True) + EPS)
            hv = (xf * rstd * g_ref[...]) * (1.0 + sc_ref[...]) + sh_ref[...]
            h_s[...] = hv.astype(MXU)
            h_ref[...] = h_s[...]

        y_ref[...] = _dot(h_s[...], w_ref[...])

    return pl.pallas_call(
        body, name=name, grid=(nb, s // tm, n // tn),
        in_specs=[pl.BlockSpec((None, tm, d), lambda b, i, j: (b, i, 0)),
                  pl.BlockSpec((1, d), lambda b, i, j: (0, 0)),
                  pl.BlockSpec((None, 1, d), lambda b, i, j: (b, 0, 0)),
                  pl.BlockSpec((None, 1, d), lambda b, i, j: (b, 0, 0)),
                  pl.BlockSpec((d, tn), lambda b, i, j: (0, j))],
        out_specs=[pl.BlockSpec((None, tm, tn), lambda b, i, j: (b, i, j)),
                   pl.BlockSpec((None, tm, d), lambda b, i, j: (b, i, 0))],
        out_shape=[jax.ShapeDtypeStruct((nb, s, n), F32), jax.ShapeDtypeStruct((nb, s, d), MXU)],
        scratch_shapes=[pltpu.VMEM((tm, d), MXU)],
        compiler_params=_cp("parallel", "parallel", "arbitrary"))(x, g, scale, shift, w)


def _ln_mod_matmul_bwd(dy, w, x, g, scale, dres, conv_w, name):
    nb, s, n = dy.shape
    d = x.shape[-1]
    tm, tn = min(256, s), 512
    ni, nj = s // tm, n // tn
    hb = tm // 8
    conv = conv_w is not None

    def body(*refs):
        if conv:
            dy_ref, nx_ref, cw_ref = refs[:3]
            refs = refs[3:]
        else:
            dy_ref = refs[0]
            refs = refs[1:]
        w_ref, x_ref, g_ref, sc_ref, dr_ref, dx_ref, dyp_ref, dsh_ref, dsc_ref, dg_ref, acc = refs
        b, i, j = pl.program_id(0), pl.program_id(1), pl.program_id(2)

        @pl.when(j == 0)
        def _():
            acc[...] = jnp.zeros_like(acc)

        @pl.when((j == 0) & (i == 0))
        def _():
            dsh_ref[...] = jnp.zeros_like(dsh_ref)
            dsc_ref[...] = jnp.zeros_like(dsc_ref)

        @pl.when((j == 0) & (i == 0) & (b == 0))
        def _():
            dg_ref[...] = jnp.zeros_like(dg_ref)

        dv = dy_ref[...]
        if conv:
            rows = _iota((tm, 1), 0)
            nx = jnp.where(i == ni - 1, 0.0, nx_ref[...])
            n1 = jnp.where(rows == tm - 1, nx[0:1, :], pltpu.roll(dv, tm - 1, 0))
            n2 = jnp.where(rows == tm - 2, nx[0:1, :], jnp.where(rows == tm - 1, nx[1:2, :], pltpu.roll(dv, tm - 2, 0)))
            cw = cw_ref[...]
            dv = cw[2:3, :] * dv + cw[1:2, :] * n1 + cw[0:1, :] * n2
        dp = dv.astype(MXU)
        dyp_ref[...] = dp
        acc[...] += _dot_nt(dp, w_ref[...])

        @pl.when(j == nj - 1)
        def _():
            dh = acc[...]
            xf = x_ref[...]
            rstd = lax.rsqrt(jnp.mean(xf * xf, axis=-1, keepdims=True) + EPS)
            xn = xf * rstd
            gg = g_ref[...]
            sc1 = 1.0 + sc_ref[...]
            dsh_ref[...] += _csum(dh)
            dsc_ref[...] += _csum(dh * xn * gg)
            dg_ref[...] += _csum(dh * xn * sc1)
            dn = dh * gg * sc1
            dx_ref[...] = dr_ref[...] + rstd * (dn - xn * jnp.mean(dn * xn, axis=-1, keepdims=True))

    in_specs = [pl.BlockSpec((None, tm, tn), lambda b, i, j: (b, i, j))]
    args = [dy]
    if conv:
        in_specs += [pl.BlockSpec((None, 8, tn), lambda b, i, j: (b, jnp.minimum((i + 1) * hb, s // 8 - 1), j)),
                     pl.BlockSpec((3, tn), lambda b, i, j: (0, j))]
        args += [dy, conv_w]
    in_specs += [pl.BlockSpec((d, tn), lambda b, i, j: (0, j)),
                 pl.BlockSpec((None, tm, d), lambda b, i, j: (b, i, 0)),
                 pl.BlockSpec((1, d), lambda b, i, j: (0, 0)),
                 pl.BlockSpec((None, 1, d), lambda b, i, j: (b, 0, 0)),
                 pl.BlockSpec((None, tm, d), lambda b, i, j: (b, i, 0))]
    args += [w, x, g, scale, dres]
    return pl.pallas_call(
        body, name=name, grid=(nb, ni, nj), in_specs=in_specs,
        out_specs=[pl.BlockSpec((None, tm, d), lambda b, i, j: (b, i, 0)),
                   pl.BlockSpec((None, tm, tn), lambda b, i, j: (b, i, j)),
                   pl.BlockSpec((None, 1, d), lambda b, i, j: (b, 0, 0)),
                   pl.BlockSpec((None, 1, d), lambda b, i, j: (b, 0, 0)),
                   pl.BlockSpec((1, d), lambda b, i, j: (0, 0))],
        out_shape=[jax.ShapeDtypeStruct((nb, s, d), F32), jax.ShapeDtypeStruct((nb, s, n), MXU),
                   jax.ShapeDtypeStruct((nb, 1, d), F32), jax.ShapeDtypeStruct((nb, 1, d), F32),
                   jax.ShapeDtypeStruct((1, d), F32)],
        scratch_shapes=[pltpu.VMEM((tm, d), F32)],
        compiler_params=_cp("arbitrary", "arbitrary", "arbitrary"))(*args)


def _wgrad(xm, dym, name):
    t, k = xm.shape
    n = dym.shape[1]
    tk = 512 if k % 512 == 0 else 256
    tn = 512
    tt = min(512, t)

    def body(x_ref, dy_ref, o_ref):
        @pl.when(pl.program_id(2) == 0)
        def _():
            o_ref[...] = jnp.zeros_like(o_ref)

        o_ref[...] += _dot_tn(x_ref[...], dy_ref[...])

    return pl.pallas_call(
        body, name=name, grid=(k // tk, n // tn, t // tt),
        in_specs=[pl.BlockSpec((tt, tk), lambda a, b, c: (c, a)),
                  pl.BlockSpec((tt, tn), lambda a, b, c: (c, b))],
        out_specs=pl.BlockSpec((tk, tn), lambda a, b, c: (a, b)),
        out_shape=jax.ShapeDtypeStruct((k, n), F32),
        compiler_params=_cp("parallel", "parallel", "arbitrary"))(xm, dym)


def _out_proj(parts, ws, gate, res, name):
    nb, s, d = res.shape
    tm = min(512, s)
    npart = len(parts)

    def body(*refs):
        p_refs, w_refs = refs[:npart], refs[npart:2 * npart]
        gt_ref, res_ref, xo_ref, y_ref = refs[2 * npart:]
        y = _dot(p_refs[0][...].astype(MXU), w_refs[0][...])
        for p_ref, w_ref in zip(p_refs[1:], w_refs[1:]):
            y = y + _dot(p_ref[...].astype(MXU), w_ref[...])
        y_ref[...] = y
        xo_ref[...] = res_ref[...] + gt_ref[...] * y

    in_specs = [pl.BlockSpec((None, tm, p.shape[-1]), lambda b, i: (b, i, 0)) for p in parts]
    in_specs += [pl.BlockSpec(w.shape, lambda b, i: (0, 0)) for w in ws]
    in_specs += [pl.BlockSpec((None, 1, d), lambda b, i: (b, 0, 0)),
                 pl.BlockSpec((None, tm, d), lambda b, i: (b, i, 0))]
    return pl.pallas_call(
        body, name=name, grid=(nb, s // tm), in_specs=in_specs,
        out_specs=[pl.BlockSpec((None, tm, d), lambda b, i: (b, i, 0))] * 2,
        out_shape=[jax.ShapeDtypeStruct((nb, s, d), F32)] * 2,
        compiler_params=_cp("parallel", "parallel"))(*parts, *ws, gate, res)


def _gate_bwd_nt(dx, y, gate, ws, name):
    nb, s, d = dx.shape
    tm = min(256, s)
    npart = len(ws)

    def body(*refs):
        dx_ref, y_ref, gt_ref = refs[:3]
        w_refs = refs[3:3 + npart]
        da_refs = refs[3 + npart:3 + 2 * npart]
        dy_ref, dgt_ref = refs[3 + 2 * npart:]

        @pl.when(pl.program_id(1) == 0)
        def _():
            dgt_ref[...] = jnp.zeros_like(dgt_ref)

        dxv = dx_ref[...]
        dyv = (dxv * gt_ref[...]).astype(MXU)
        dy_ref[...] = dyv
        dgt_ref[...] += _csum(dxv * y_ref[...])
        for w_ref, da_ref in zip(w_refs, da_refs):
            da_ref[...] = _dot_nt(dyv, w_ref[...])

    tile = pl.BlockSpec((None, tm, d), lambda b, i: (b, i, 0))
    row = pl.BlockSpec((None, 1, d), lambda b, i: (b, 0, 0))
    outs = pl.pallas_call(
        body, name=name, grid=(nb, s // tm),
        in_specs=[tile, tile, row] + [pl.BlockSpec(w.shape, lambda b, i: (0, 0)) for w in ws],
        out_specs=[pl.BlockSpec((None, tm, w.shape[0]), lambda b, i: (b, i, 0)) for w in ws] + [tile, row],
        out_shape=[jax.ShapeDtypeStruct((nb, s, w.shape[0]), F32) for w in ws]
        + [jax.ShapeDtypeStruct((nb, s, d), MXU), jax.ShapeDtypeStruct((nb, 1, d), F32)],
        compiler_params=_cp("arbitrary", "arbitrary"))(dx, y, gate, *ws)
    return outs[:npart], outs[npart], outs[npart + 1]


def _conv_shifts(xv, halo, rows):
    p1 = jnp.where(rows == 0, halo[7:8, :], pltpu.roll(xv, 1, 0))
    p2 = jnp.where(rows == 0, halo[6:7, :], jnp.where(rows == 1, halo[7:8, :], pltpu.roll(xv, 2, 0)))
    return p1, p2


def _conv_gate_matmul(u, cw, cb, wd, gate, res, name):
    nb, s, f2 = u.shape
    f = f2 // 2
    d = wd.shape[1]
    tm = min(256, s)
    tk = f // 2
    nk = f // tk
    hb = tm // 8

    def body(ug_ref, uv_ref, hg_ref, hv_ref, cwg_ref, cwv_ref, cbg_ref, cbv_ref, wd_ref, gt_ref, res_ref,
             xo_ref, y_ref, acc):
        i, k = pl.program_id(1), pl.program_id(2)

        @pl.when(k == 0)
        def _():
            acc[...] = jnp.zeros_like(acc)

        rows = _iota((tm, 1), 0)

        def conv(x_ref, h_ref, w_ref, b_ref):
            xv = x_ref[...]
            halo = jnp.where(i == 0, 0.0, h_ref[...])
            p1, p2 = _conv_shifts(xv, halo, rows)
            wv = w_ref[...]
            return wv[2:3, :] * xv + wv[1:2, :] * p1 + wv[0:1, :] * p2 + b_ref[...]

        gv = conv(ug_ref, hg_ref, cwg_ref, cbg_ref)
        vv = conv(uv_ref, hv_ref, cwv_ref, cbv_ref)
        av = gv * _sigmoid(gv) * vv
        acc[...] += _dot(av.astype(MXU), wd_ref[...])

        @pl.when(k == nk - 1)
        def _():
            y = acc[...]
            y_ref[...] = y
            xo_ref[...] = res_ref[...] + gt_ref[...] * y

    def halo_idx(off):
        return lambda b, i, k: (b, jnp.maximum(i * hb - 1, 0), k + off)

    tile = pl.BlockSpec((None, tm, d), lambda b, i, k: (b, i, 0))
    return pl.pallas_call(
        body, name=name, grid=(nb, s // tm, nk),
        in_specs=[pl.BlockSpec((None, tm, tk), lambda b, i, k: (b, i, k)),
                  pl.BlockSpec((None, tm, tk), lambda b, i, k: (b, i, k + nk)),
                  pl.BlockSpec((None, 8, tk), halo_idx(0)),
                  pl.BlockSpec((None, 8, tk), halo_idx(nk)),
                  pl.BlockSpec((3, tk), lambda b, i, k: (0, k)),
                  pl.BlockSpec((3, tk), lambda b, i, k: (0, k + nk)),
                  pl.BlockSpec((1, tk), lambda b, i, k: (0, k)),
                  pl.BlockSpec((1, tk), lambda b, i, k: (0, k + nk)),
                  pl.BlockSpec((tk, d), lambda b, i, k: (k, 0)),
                  pl.BlockSpec((None, 1, d), lambda b, i, k: (b, 0, 0)),
                  tile],
        out_specs=[tile, tile],
        out_shape=[jax.ShapeDtypeStruct((nb, s, d), F32)] * 2,
        scratch_shapes=[pltpu.VMEM((tm, d), F32)],
        compiler_params=_cp("parallel", "parallel", "arbitrary"))(u, u, u, u, cw, cw, cb, cb, wd, gate, res)


def _conv_gate_bwd(da, u, cw, cb, name):
    nb, s, f2 = u.shape
    f = f2 // 2
    tm = min(128, s)
    hb = tm // 8

    def body(da_ref, u_ref, h_ref, cw_ref, cb_ref, du_ref, a_ref, st_ref):
        b, i = pl.program_id(0), pl.program_id(1)

        @pl.when((b == 0) & (i == 0))
        def _():
            st_ref[...] = jnp.zeros_like(st_ref)

        rows = _iota((tm, 1), 0)
        xv = u_ref[...]
        halo = jnp.where(i == 0, 0.0, h_ref[...])
        p1, p2 = _conv_shifts(xv, halo, rows)
        wv = cw_ref[...]
        uc = wv[2:3, :] * xv + wv[1:2, :] * p1 + wv[0:1, :] * p2 + cb_ref[...]
        gv, vv = uc[:, :f], uc[:, f:]
        sg = _sigmoid(gv)
        sl = gv * sg
        a_ref[...] = (sl * vv).astype(MXU)
        dav = da_ref[...]
        du_ref[:, :f] = dav * vv * (sg * (1.0 + gv * (1.0 - sg)))
        du_ref[:, f:] = dav * sl
        du = du_ref[...]
        st_ref[0:1, :] += _csum(du)
        st_ref[1:2, :] += _csum(du * p2)
        st_ref[2:3, :] += _csum(du * p1)
        st_ref[3:4, :] += _csum(du * xv)

    return pl.pallas_call(
        body, name=name, grid=(nb, s // tm),
        in_specs=[pl.BlockSpec((None, tm, f), lambda b, i: (b, i, 0)),
                  pl.BlockSpec((None, tm, f2), lambda b, i: (b, i, 0)),
                  pl.BlockSpec((None, 8, f2), lambda b, i: (b, jnp.maximum(i * hb - 1, 0), 0)),
                  pl.BlockSpec((3, f2), lambda b, i: (0, 0)),
                  pl.BlockSpec((1, f2), lambda b, i: (0, 0))],
        out_specs=[pl.BlockSpec((None, tm, f2), lambda b, i: (b, i, 0)),
                   pl.BlockSpec((None, tm, f), lambda b, i: (b, i, 0)),
                   pl.BlockSpec((8, f2), lambda b, i: (0, 0))],
        out_shape=[jax.ShapeDtypeStruct((nb, s, f2), F32), jax.ShapeDtypeStruct((nb, s, f), MXU),
                   jax.ShapeDtypeStruct((8, f2), F32)],
        compiler_params=_cp("arbitrary", "arbitrary"))(da, u, u, cw, cb)


def _rot(xv, lane):
    return jnp.where((lane >= 64) & (lane < 80), -pltpu.roll(xv, 112, 1),
                     jnp.where((lane >= 80) & (lane < 96), pltpu.roll(xv, 16, 1), 0.0))


def _rot_t(dv, lane):
    return jnp.where((lane >= 80) & (lane < 96), -pltpu.roll(dv, 16, 1),
                     jnp.where((lane >= 64) & (lane < 80), pltpu.roll(dv, 112, 1), 0.0))


def _mla_prep_specs(s, tm):
    def blk(width, col):
        return pl.BlockSpec((None, tm, width), lambda b, i: (b, i, col // width))

    full = lambda shape: pl.BlockSpec(shape, lambda b, i: (0, 0))
    return [blk(256, COL_CQ), blk(128, COL_CKV), blk(128, COL_KR),
            pl.BlockSpec((None, tm, LANES), lambda b, i: (b, i, 0)),
            pl.BlockSpec((None, tm, LANES), lambda b, i: (b, i, 0)),
            full((1, 256)), full((1, 128)), full((1, 128)), full((1, 128)),
            full((256, 768)), full((128, 768))]


def _mla_prep(proj, cs, sn, gcq, gckv, gqn, gkn, wuq, wukv, name):
    nb, s, _ = proj.shape
    tm = min(256, s)

    def body(cq_ref, ckv_ref, kr_ref, c_ref, s_ref, gcq_ref, gckv_ref, gqn_ref, gkn_ref, wuq_ref, wukv_ref,
             q_ref, k_ref, v_ref):
        lane = _iota((tm, LANES), 1)
        cv, sv = c_ref[...], s_ref[...]
        cq = cq_ref[...]
        cqn = cq * lax.rsqrt(jnp.mean(cq * cq, axis=-1, keepdims=True) + EPS) * gcq_ref[...]
        qb = _dot(cqn.astype(MXU), wuq_ref[...])
        ckv = ckv_ref[...]
        ckvn = ckv * lax.rsqrt(jnp.mean(ckv * ckv, axis=-1, keepdims=True) + EPS) * gckv_ref[...]
        kvb = _dot(ckvn.astype(MXU), wukv_ref[...])
        kr = kr_ref[...]
        for h in range(MLA_HEADS):
            hs = slice(h * LANES, (h + 1) * LANES)
            qh = qb[:, hs]
            qn = qh * lax.rsqrt(_rsum(qh * qh) / MLA_QK + EPS) * gqn_ref[...]
            q_ref[:, hs] = (qn * cv + _rot(qn, lane) * sv).astype(MXU)
            kc = jnp.where(lane < HEAD, kvb[:, hs], kr)
            kn = kc * lax.rsqrt(_rsum(kc * kc) / MLA_QK + EPS) * gkn_ref[...]
            k_ref[:, hs] = (kn * cv + _rot(kn, lane) * sv).astype(MXU)
        for j in range(MLA_HEADS // 2):
            va = kvb[:, (2 * j) * LANES:(2 * j + 1) * LANES]
            vb = kvb[:, (2 * j + 1) * LANES:(2 * j + 2) * LANES]
            v_ref[:, j * LANES:(j + 1) * LANES] = jnp.where(lane < HEAD, pltpu.roll(va, HEAD, 1), vb).astype(MXU)

    return pl.pallas_call(
        body, name=name, grid=(nb, s // tm), in_specs=_mla_prep_specs(s, tm),
        out_specs=[pl.BlockSpec((None, tm, 768), lambda b, i: (b, i, 0)),
                   pl.BlockSpec((None, tm, 768), lambda b, i: (b, i, 0)),
                   pl.BlockSpec((None, tm, 384), lambda b, i: (b, i, 0))],
        out_shape=[jax.ShapeDtypeStruct((nb, s, 768), MXU), jax.ShapeDtypeStruct((nb, s, 768), MXU),
                   jax.ShapeDtypeStruct((nb, s, 384), MXU)],
        compiler_params=_cp("parallel", "parallel"))(proj, proj, proj, cs, sn, gcq, gckv, gqn, gkn, wuq, wukv)


def _mla_prep_bwd(proj, cs, sn, gcq, gckv, gqn, gkn, wuq, wukv, dq, dk, dv, name):
    nb, s, _ = proj.shape
    tm = min(256, s)

    def body(cq_ref, ckv_ref, kr_ref, c_ref, s_ref, gcq_ref, gckv_ref, gqn_ref, gkn_ref, wuq_ref, wukv_ref,
             dq_ref, dk_ref, dv_ref,
             dcq_ref, dckv_ref, dkr_ref, dwuq_ref, dwukv_ref, dgcq_ref, dgckv_ref, dgqn_ref, dgkn_ref,
             dqb_s, dkvb_s):
        @pl.when((pl.program_id(0) == 0) & (pl.program_id(1) == 0))
        def _():
            for r in (dwuq_ref, dwukv_ref, dgcq_ref, dgckv_ref, dgqn_ref, dgkn_ref):
                r[...] = jnp.zeros_like(r)

        lane = _iota((tm, LANES), 1)
        cv, sv = c_ref[...], s_ref[...]
        gqn, gkn = gqn_ref[...], gkn_ref[...]
        cq = cq_ref[...]
        rc = lax.rsqrt(jnp.mean(cq * cq, axis=-1, keepdims=True) + EPS)
        chat = cq * rc
        cqn = (chat * gcq_ref[...]).astype(MXU)
        qb = _dot(cqn, wuq_ref[...])
        ckv = ckv_ref[...]
        rkv = lax.rsqrt(jnp.mean(ckv * ckv, axis=-1, keepdims=True) + EPS)
        kvhat = ckv * rkv
        ckvn = (kvhat * gckv_ref[...]).astype(MXU)
        kvb = _dot(ckvn, wukv_ref[...])
        kr = kr_ref[...]
        dgq = jnp.zeros((1, LANES), F32)
        dgk = jnp.zeros((1, LANES), F32)
        dkr = jnp.zeros((tm, LANES), F32)
        for h in range(MLA_HEADS):
            hs = slice(h * LANES, (h + 1) * LANES)
            qh = qb[:, hs]
            rq = lax.rsqrt(_rsum(qh * qh) / MLA_QK + EPS)
            qhat = qh * rq
            dqr = dq_ref[:, hs]
            dqn = dqr * cv + _rot_t(dqr * sv, lane)
            dgq = dgq + _csum(dqn * qhat)
            dyq = dqn * gqn
            dqb_s[:, hs] = (rq * (dyq - qhat * (_rsum(dyq * qhat) / MLA_QK))).astype(MXU)

            kc = jnp.where(lane < HEAD, kvb[:, hs], kr)
            rk = lax.rsqrt(_rsum(kc * kc) / MLA_QK + EPS)
            khat = kc * rk
            dkr_h = dk_ref[:, hs]
            dkn = dkr_h * cv + _rot_t(dkr_h * sv, lane)
            dgk = dgk + _csum(dkn * khat)
            dyk = dkn * gkn
            dkc = rk * (dyk - khat * (_rsum(dyk * khat) / MLA_QK))
            dkr = dkr + jnp.where(lane >= HEAD, dkc, 0.0)
            dvb = dv_ref[:, (h // 2) * LANES:(h // 2 + 1) * LANES]
            dvp = dvb if h % 2 == 1 else pltpu.roll(dvb, HEAD, 1)
            dkvb_s[:, hs] = jnp.where(lane < HEAD, dkc, dvp).astype(MXU)
        dgqn_ref[...] += dgq
        dgkn_ref[...] += dgk
        dkr_ref[...] = dkr

        dqb = dqb_s[...]
        dwuq_ref[...] += _dot_tn(cqn, dqb)
        dcqn = _dot_nt(dqb, wuq_ref[...])
        dgcq_ref[...] += _csum(dcqn * chat)
        dyc = dcqn * gcq_ref[...]
        dcq_ref[...] = rc * (dyc - chat * jnp.mean(dyc * chat, axis=-1, keepdims=True))

        dkvb = dkvb_s[...]
        dwukv_ref[...] += _dot_tn(ckvn, dkvb)
        dckvn = _dot_nt(dkvb, wukv_ref[...])
        dgckv_ref[...] += _csum(dckvn * kvhat)
        dykv = dckvn * gckv_ref[...]
        dckv_ref[...] = rkv * (dykv - kvhat * jnp.mean(dykv * kvhat, axis=-1, keepdims=True))

    full = lambda shape: pl.BlockSpec(shape, lambda b, i: (0, 0))
    tile = lambda width: pl.BlockSpec((None, tm, width), lambda b, i: (b, i, 0))
    return pl.pallas_call(
        body, name=name, grid=(nb, s // tm),
        in_specs=_mla_prep_specs(s, tm) + [tile(768), tile(768), tile(384)],
        out_specs=[tile(256), tile(128), tile(128), full((256, 768)), full((128, 768)),
                   full((1, 256)), full((1, 128)), full((1, 128)), full((1, 128))],
        out_shape=[jax.ShapeDtypeStruct((nb, s, 256), F32), jax.ShapeDtypeStruct((nb, s, 128), F32),
                   jax.ShapeDtypeStruct((nb, s, 128), F32),
                   jax.ShapeDtypeStruct((256, 768), F32), jax.ShapeDtypeStruct((128, 768), F32),
                   jax.ShapeDtypeStruct((1, 256), F32), jax.ShapeDtypeStruct((1, 128), F32),
                   jax.ShapeDtypeStruct((1, 128), F32), jax.ShapeDtypeStruct((1, 128), F32)],
        scratch_shapes=[pltpu.VMEM((tm, 768), MXU), pltpu.VMEM((tm, 768), MXU)],
        compiler_params=_cp("arbitrary", "arbitrary"))(
            proj, proj, proj, cs, sn, gcq, gckv, gqn, gkn, wuq, wukv, dq, dk, dv)


def _softplus(z):
    return jnp.maximum(z, 0.0) + jnp.log(1.0 + jnp.exp(-jnp.abs(z)))


def _sb_fwd(proj, name):
    nb, s, _ = proj.shape
    tq, tk = min(256, s), 128
    ratio = tq // tk

    def body(q_ref, k_ref, v_ref, o_ref, ct_ref):
        i = pl.program_id(2)
        lo = _iota((tq, LANES), 1) < HEAD
        qv = q_ref[...]
        q0 = jnp.where(lo, qv, 0.0).astype(MXU)
        q1 = jnp.where(lo, 0.0, qv).astype(MXU)
        usuf = (_iota((tk, tk), 0) > _iota((tk, tk), 1)).astype(MXU)
        tpos = i * tq + _iota((tq, tk), 0)
        scol = _iota((tq, tk), 1)
        nch = (i + 1) * ratio

        def step(t, carry):
            c0, a0, c1, a1 = carry
            j = nch - 1 - t
            off = pl.multiple_of(j * tk, tk)
            kc = k_ref[pl.ds(off, tk), :].astype(MXU)
            vc = v_ref[pl.ds(off, tk), :].astype(MXU)
            msk = (scol + j * tk) < tpos

            def head(qm, c, a):
                z = _dot_nt(qm, kc) * SB_SCALE
                sp = _softplus(z)
                lk = jnp.where(msk, -sp, 0.0)
                w = jnp.where(msk, jnp.exp(z - sp + _cumdot(lk, usuf) + c), 0.0)
                return c + _rsum(lk), a + _dot(w.astype(MXU), vc)

            c0, a0 = head(q0, c0, a0)
            c1, a1 = head(q1, c1, a1)
            return c0, a0, c1, a1

        z1 = jnp.zeros((tq, 1), F32)
        za = jnp.zeros((tq, LANES), F32)
        c0, a0, c1, a1 = lax.fori_loop(0, nch, step, (z1, za, z1, za))
        o_ref[...] = jnp.where(lo, a0, a1)
        ct_ref[...] = jnp.where(lo, c0, c1)

    kv = lambda col: pl.BlockSpec((None, s, LANES), lambda b, p, i: (b, 0, col // LANES + p))
    tile = pl.BlockSpec((None, tq, LANES), lambda b, p, i: (b, i, p))
    return pl.pallas_call(
        body, name=name, grid=(nb, 2, s // tq),
        in_specs=[pl.BlockSpec((None, tq, LANES), lambda b, p, i: (b, i, COL_SBQ // LANES + p)),
                  kv(COL_SBK), kv(COL_SBV)],
        out_specs=[tile, tile],
        out_shape=[jax.ShapeDtypeStruct((nb, s, 256), F32)] * 2,
        compiler_params=_cp("parallel", "parallel", "arbitrary"))(proj, proj, proj)


def _sb_bwd(proj, ct, do, name):
    nb, s, _ = proj.shape
    tq, tk = min(256, s), 128
    ratio = tq // tk

    def body(q_ref, k_ref, v_ref, ct_ref, do_ref, dq_ref, dk_ref, dv_ref):
        i = pl.program_id(2)

        @pl.when(i == 0)
        def _():
            dk_ref[...] = jnp.zeros_like(dk_ref)
            dv_ref[...] = jnp.zeros_like(dv_ref)

        lane = _iota((tq, LANES), 1)
        lo = lane < HEAD
        lok = _iota((tk, LANES), 1) < HEAD
        qv, dov = q_ref[...], do_ref[...]
        qb, dob = qv.astype(MXU), dov.astype(MXU)
        q0 = jnp.where(lo, qv, 0.0).astype(MXU)
        q1 = jnp.where(lo, 0.0, qv).astype(MXU)
        do0 = jnp.where(lo, dov, 0.0).astype(MXU)
        do1 = jnp.where(lo, 0.0, dov).astype(MXU)
        ctv = ct_ref[...]
        ct0 = _rsum(jnp.where(lane == 0, ctv, 0.0))
        ct1 = _rsum(jnp.where(lane == LANES - 1, ctv, 0.0))
        uincl = (_iota((tk, tk), 0) <= _iota((tk, tk), 1)).astype(MXU)
        ustrict = (_iota((tk, tk), 0) < _iota((tk, tk), 1)).astype(MXU)
        tpos = i * tq + _iota((tq, tk), 0)
        scol = _iota((tq, tk), 1)
        nch = (i + 1) * ratio

        def step(j, carry):
            p0, g0, dq0, p1, g1, dq1 = carry
            off = pl.multiple_of(j * tk, tk)
            kc = k_ref[pl.ds(off, tk), :].astype(MXU)
            vc = v_ref[pl.ds(off, tk), :].astype(MXU)
            msk = (scol + j * tk) < tpos

            def head(qm, dom, ctot, pc, gc, dqa):
                z = _dot_nt(qm, kc) * SB_SCALE
                sp = _softplus(z)
                lk = jnp.where(msk, -sp, 0.0)
                lsig = z - sp
                w = jnp.where(msk, jnp.exp(lsig + (ctot - pc - _cumdot(lk, uincl))), 0.0)
                g = w * _dot_nt(dom, vc)
                gpre = gc + _cumdot(g, ustrict)
                sig = jnp.exp(lsig)
                dz = (jnp.where(msk, g * (1.0 - sig) - sig * gpre, 0.0) * SB_SCALE).astype(MXU)
                return (pc + _rsum(lk), gc + _rsum(g), dqa + _dot(dz, kc),
                        _dot_tn(dz, qb), _dot_tn(w.astype(MXU), dob))

            p0, g0, dq0, dk0, dv0 = head(q0, do0, ct0, p0, g0, dq0)
            p1, g1, dq1, dk1, dv1 = head(q1, do1, ct1, p1, g1, dq1)
            dk_ref[pl.ds(off, tk), :] += jnp.where(lok, dk0, dk1)
            dv_ref[pl.ds(off, tk), :] += jnp.where(lok, dv0, dv1)
            return p0, g0, dq0, p1, g1, dq1

        z1 = jnp.zeros((tq, 1), F32)
        za = jnp.zeros((tq, LANES), F32)
        _, _, dq0, _, _, dq1 = lax.fori_loop(0, nch, step, (z1, z1, za, z1, z1, za))
        dq_ref[...] = jnp.where(lo, dq0, dq1)

    kv = lambda col: pl.BlockSpec((None, s, LANES), lambda b, p, i: (b, 0, col // LANES + p))
    tile = pl.BlockSpec((None, tq, LANES), lambda b, p, i: (b, i, p))
    acc = pl.BlockSpec((None, s, LANES), lambda b, p, i: (b, 0, p))
    return pl.pallas_call(
        body, name=name, grid=(nb, 2, s // tq),
        in_specs=[pl.BlockSpec((None, tq, LANES), lambda b, p, i: (b, i, COL_SBQ // LANES + p)),
                  kv(COL_SBK), kv(COL_SBV), tile, tile],
        out_specs=[tile, acc, acc],
        out_shape=[jax.ShapeDtypeStruct((nb, s, 256), F32)] * 3,
        compiler_params=_cp("parallel", "parallel", "arbitrary"))(proj, proj, proj, ct, do)


def _mla_fwd(q, k, v, name):
    nb, s, _ = q.shape
    tq = tk = min(256, s)

    def body(q_ref, k_ref, v_ref, o_ref, lse_ref):
        i = pl.program_id(2)
        lo = _iota((tq, LANES), 1) < HEAD
        q0, q1 = q_ref[:, :LANES], q_ref[:, LANES:]
        tpos = i * tq + _iota((tq, tk), 0)
        scol = _iota((tq, tk), 1)

        def step(j, carry):
            m0, l0, a0, m1, l1, a1 = carry
            off = pl.multiple_of(j * tk, tk)
            vc = v_ref[pl.ds(off, tk), :]
            msk = (scol + j * tk) <= tpos

            def head(qh, kh, m, l, a):
                sc = jnp.where(msk, _dot_nt(qh, kh) * MLA_SCALE, NEG)
                mn = jnp.maximum(m, jnp.max(sc, axis=-1, keepdims=True))
                al = jnp.exp(m - mn)
                p = jnp.exp(sc - mn)
                return mn, al * l + _rsum(p), al * a + _dot(p.astype(MXU), vc)

            m0, l0, a0 = head(q0, k_ref[pl.ds(off, tk), :LANES], m0, l0, a0)
            m1, l1, a1 = head(q1, k_ref[pl.ds(off, tk), LANES:], m1, l1, a1)
            return m0, l0, a0, m1, l1, a1

        mi = jnp.full((tq, 1), NEG, F32)
        z1 = jnp.zeros((tq, 1), F32)
        za = jnp.zeros((tq, LANES), F32)
        m0, l0, a0, m1, l1, a1 = lax.fori_loop(0, i + 1, step, (mi, z1, za, mi, z1, za))
        o_ref[...] = jnp.where(lo, a0 / l0, a1 / l1)
        lse_ref[...] = jnp.where(lo, m0 + jnp.log(l0), m1 + jnp.log(l1))

    tile = pl.BlockSpec((None, tq, LANES), lambda b, p, i: (b, i, p))
    return pl.pallas_call(
        body, name=name, grid=(nb, MLA_HEADS // 2, s // tq),
        in_specs=[pl.BlockSpec((None, tq, 2 * LANES), lambda b, p, i: (b, i, p)),
                  pl.BlockSpec((None, s, 2 * LANES), lambda b, p, i: (b, 0, p)),
                  pl.BlockSpec((None, s, LANES), lambda b, p, i: (b, 0, p))],
        out_specs=[tile, tile],
        out_shape=[jax.ShapeDtypeStruct((nb, s, 384), F32)] * 2,
        compiler_params=_cp("parallel", "parallel", "arbitrary"))(q, k, v)


def _mla_bwd(q, k, v, o, lse, do, name):
    nb, s, _ = q.shape
    tq = tk = min(256, s)

    def body(q_ref, k_ref, v_ref, o_ref, lse_ref, do_ref, dq_ref, dk_ref, dv_ref):
        i = pl.program_id(2)

        @pl.when(i == 0)
        def _():
            dk_ref[...] = jnp.zeros_like(dk_ref)
            dv_ref[...] = jnp.zeros_like(dv_ref)

        lane = _iota((tq, LANES), 1)
        lo = lane < HEAD
        lok = _iota((tk, LANES), 1) < HEAD
        q0, q1 = q_ref[:, :LANES], q_ref[:, LANES:]
        dov = do_ref[...]
        dob = dov.astype(MXU)
        do0 = jnp.where(lo, dov, 0.0).astype(MXU)
        do1 = jnp.where(lo, 0.0, dov).astype(MXU)
        dd = dov * o_ref[...]
        dl0 = _rsum(jnp.where(lo, dd, 0.0))
        dl1 = _rsum(jnp.where(lo, 0.0, dd))
        lsev = lse_ref[...]
        ls0 = _rsum(jnp.where(lane == 0, lsev, 0.0))
        ls1 = _rsum(jnp.where(lane == LANES - 1, lsev, 0.0))
        tpos = i * tq + _iota((tq, tk), 0)
        scol = _iota((tq, tk), 1)

        def step(j, carry):
            dq0, dq1 = carry
            off = pl.multiple_of(j * tk, tk)
            vc = v_ref[pl.ds(off, tk), :]
            msk = (scol + j * tk) <= tpos

            def head(qh, kh, dom, ls, dl, dqa):
                sc = jnp.where(msk, _dot_nt(qh, kh) * MLA_SCALE, NEG)
                p = jnp.exp(sc - ls)
                ds = (p * (_dot_nt(dom, vc) - dl) * MLA_SCALE).astype(MXU)
                return dqa + _dot(ds, kh), _dot_tn(ds, qh), _dot_tn(p.astype(MXU), dob)

            dq0, dk0, dv0 = head(q0, k_ref[pl.ds(off, tk), :LANES], do0, ls0, dl0, dq0)
            dq1, dk1, dv1 = head(q1, k_ref[pl.ds(off, tk), LANES:], do1, ls1, dl1, dq1)
            dk_ref[pl.ds(off, tk), :LANES] += dk0
            dk_ref[pl.ds(off, tk), LANES:] += dk1
            dv_ref[pl.ds(off, tk), :] += jnp.where(lok, dv0, dv1)
            return dq0, dq1

        za = jnp.zeros((tq, LANES), F32)
        dq0, dq1 = lax.fori_loop(0, i + 1, step, (za, za))
        dq_ref[:, :LANES] = dq0
        dq_ref[:, LANES:] = dq1

    tile = pl.BlockSpec((None, tq, LANES), lambda b, p, i: (b, i, p))
    tile2 = pl.BlockSpec((None, tq, 2 * LANES), lambda b, p, i: (b, i, p))
    return pl.pallas_call(
        body, name=name, grid=(nb, MLA_HEADS // 2, s // tq),
        in_specs=[tile2,
                  pl.BlockSpec((None, s, 2 * LANES), lambda b, p, i: (b, 0, p)),
                  pl.BlockSpec((None, s, LANES), lambda b, p, i: (b, 0, p)),
                  tile, tile, tile],
        out_specs=[tile2,
                   pl.BlockSpec((None, s, 2 * LANES), lambda b, p, i: (b, 0, p)),
                   pl.BlockSpec((None, s, LANES), lambda b, p, i: (b, 0, p))],
        out_shape=[jax.ShapeDtypeStruct((nb, s, 768), F32), jax.ShapeDtypeStruct((nb, s, 768), F32),
                   jax.ShapeDtypeStruct((nb, s, 384), F32)],
        compiler_params=_cp("parallel", "parallel", "arbitrary"))(q, k, v, o, lse, do)


def _half_stats(xv, lo):
    x2 = xv * xv
    s0 = _rsum(jnp.where(lo, x2, 0.0))
    s1 = _rsum(jnp.where(lo, 0.0, x2))
    return jnp.where(lo, lax.rsqrt(s0 / HEAD + EPS), lax.rsqrt(s1 / HEAD + EPS))


def _half_mean(xv, lo):
    s0 = _rsum(jnp.where(lo, xv, 0.0))
    s1 = _rsum(jnp.where(lo, 0.0, xv))
    return jnp.where(lo, s0, s1) / HEAD


def _swa_in_specs():
    def band(col, prev):
        if prev:
            return pl.BlockSpec((None, BLOCK, LANES), lambda b, n: (b, jnp.maximum(n - 1, 0), col // LANES))
        return pl.BlockSpec((None, BLOCK, LANES), lambda b, n: (b, n, col // LANES))

    full = lambda shape: pl.BlockSpec(shape, lambda b, n: tuple(0 for _ in shape))
    return [pl.BlockSpec((None, BLOCK, 384), lambda b, n: (b, n, COL_SWQ // 384)),
            band(COL_SWK, False), band(COL_SWK, True), band(COL_SWV, False), band(COL_SWV, True),
            full((1, LANES)), full((1, LANES)), full((8, LANES)), full((SW_HEADS, BLOCK, 2 * BLOCK))]


def _swa_valid(n):
    a = _iota((BLOCK, 2 * BLOCK), 0)
    bcol = _iota((BLOCK, 2 * BLOCK), 1)
    dist = BLOCK + a - bcol
    return (dist >= 0) & (dist < BLOCK) & ((n > 0) | (bcol >= BLOCK))


def _swa_fwd(proj, gq, gk, sinks, bias, name):
    nb, s, _ = proj.shape

    def body(q_ref, kc_ref, kp_ref, vc_ref, vp_ref, gq_ref, gk_ref, sk_ref, bias_ref, o_ref):
        n = pl.program_id(1)
        lo = _iota((BLOCK, LANES), 1) < HEAD
        lo2 = _iota((2 * BLOCK, LANES), 1) < HEAD
        kband = jnp.concatenate([kp_ref[...], kc_ref[...]], axis=0)
        vband = jnp.concatenate([vp_ref[...], vc_ref[...]], axis=0)
        kn = kband * _half_stats(kband, lo2) * gk_ref[...]
        ks = (kn.astype(MXU), pltpu.roll(kn, HEAD, 1).astype(MXU))
        vs = (vband.astype(MXU), pltpu.roll(vband, HEAD, 1).astype(MXU))
        valid = _swa_valid(n)
        for blk in range(SW_HEADS // 2):
            qv = q_ref[:, blk * LANES:(blk + 1) * LANES]
            qn = qv * _half_stats(qv, lo) * gq_ref[...]
            outs = []
            for half in range(2):
                h = 2 * blk + half
                swap = 0 if half == h // 3 else 1
                qm = jnp.where(lo if half == 0 else ~lo, qn, 0.0).astype(MXU)
                sc = jnp.where(valid, _dot_nt(qm, ks[swap]) * SW_SCALE + bias_ref[h], NEG)
                sk = jnp.max(sk_ref[h:h + 1, :], axis=-1, keepdims=True)
                m = jnp.maximum(jnp.max(sc, axis=-1, keepdims=True), sk)
                p = jnp.exp(sc - m)
                l = _rsum(p) + jnp.exp(sk - m)
                outs.append(_dot((p / l).astype(MXU), vs[swap]))
            o_ref[:, blk * LANES:(blk + 1) * LANES] = jnp.where(lo, outs[0], outs[1])

    return pl.pallas_call(
        body, name=name, grid=(nb, s // BLOCK), in_specs=_swa_in_specs(),
        out_specs=pl.BlockSpec((None, BLOCK, 384), lambda b, n: (b, n, 0)),
        out_shape=jax.ShapeDtypeStruct((nb, s, 384), F32),
        compiler_params=_cp("parallel", "parallel"))(proj, proj, proj, proj, proj, gq, gk, sinks, bias)


def _swa_bwd(proj, gq, gk, sinks, bias, do, name):
    nb, s, _ = proj.shape

    def body(q_ref, kc_ref, kp_ref, vc_ref, vp_ref, gq_ref, gk_ref, sk_ref, bias_ref, do_ref,
             dq_ref, dkc_ref, dkp_ref, dvc_ref, dvp_ref, dbias_ref, dsk_ref, dgq_ref, dgk_ref):
        n = pl.program_id(1)

        @pl.when((pl.program_id(0) == 0) & (n == 0))
        def _():
            for r in (dbias_ref, dsk_ref, dgq_ref, dgk_ref):
                r[...] = jnp.zeros_like(r)

        lo = _iota((BLOCK, LANES), 1) < HEAD
        lo2 = _iota((2 * BLOCK, LANES), 1) < HEAD
        kband = jnp.concatenate([kp_ref[...], kc_ref[...]], axis=0)
        vband = jnp.concatenate([vp_ref[...], vc_ref[...]], axis=0)
        rk = _half_stats(kband, lo2)
        khat = kband * rk
        gkv = gk_ref[...]
        kn = khat * gkv
        ks = (kn.astype(MXU), pltpu.roll(kn, HEAD, 1).astype(MXU))
        vs = (vband.astype(MXU), pltpu.roll(vband, HEAD, 1).astype(MXU))
        valid = _swa_valid(n)
        dkn = jnp.zeros((2 * BLOCK, LANES), F32)
        dvb = jnp.zeros((2 * BLOCK, LANES), F32)
        gqv = gq_ref[...]
        dgq = jnp.zeros((1, LANES), F32)
        for blk in range(SW_HEADS // 2):
            bs = slice(blk * LANES, (blk + 1) * LANES)
            qv = q_ref[:, bs]
            rq = _half_stats(qv, lo)
            qhat = qv * rq
            qn = qhat * gqv
            dov = do_ref[:, bs]
            dqn = jnp.zeros((BLOCK, LANES), F32)
            for half in range(2):
                h = 2 * blk + half
                swap = 0 if half == h // 3 else 1
                hm = lo if half == 0 else ~lo
                qm = jnp.where(hm, qn, 0.0).astype(MXU)
                dom = jnp.where(hm, dov, 0.0).astype(MXU)
                sc = jnp.where(valid, _dot_nt(qm, ks[swap]) * SW_SCALE + bias_ref[h], NEG)
                sk = jnp.max(sk_ref[h:h + 1, :], axis=-1, keepdims=True)
                m = jnp.maximum(jnp.max(sc, axis=-1, keepdims=True), sk)
                e = jnp.exp(sc - m)
                es = jnp.exp(sk - m)
                l = _rsum(e) + es
                p = e / l
                dp = _dot_nt(dom, vs[swap])
                delta = _rsum(p * dp)
                ds = p * (dp - delta)
                dsk_ref[h:h + 1, :] += jnp.broadcast_to(_csum(-(es / l) * delta), (1, LANES))
                dbias_ref[h] += ds
                dsb = (ds * SW_SCALE).astype(MXU)
                dqn = dqn + jnp.where(hm, _dot(dsb, ks[swap]), 0.0)
                rk_ = _dot_tn(dsb, qm)
                rv_ = _dot_tn(p.astype(MXU), dom)
                if swap:
                    rk_ = pltpu.roll(rk_, HEAD, 1)
                    rv_ = pltpu.roll(rv_, HEAD, 1)
                dkn = dkn + rk_
                dvb = dvb + rv_
            dgq = dgq + _csum(dqn * qhat)
            dyq = dqn * gqv
            dq_ref[:, bs] = rq * (dyq - qhat * _half_mean(dyq * qhat, lo))
        dgq_ref[...] += dgq
        dgk_ref[...] += _csum(dkn * khat)
        dyk = dkn * gkv
        dkb = rk * (dyk - khat * _half_mean(dyk * khat, lo2))
        dkp_ref[...] = dkb[:BLOCK]
        dkc_ref[...] = dkb[BLOCK:]
        dvp_ref[...] = dvb[:BLOCK]
        dvc_ref[...] = dvb[BLOCK:]

    full = lambda shape: pl.BlockSpec(shape, lambda b, n: tuple(0 for _ in shape))
    tile = pl.BlockSpec((None, BLOCK, LANES), lambda b, n: (b, n, 0))
    tile3 = pl.BlockSpec((None, BLOCK, 384), lambda b, n: (b, n, 0))
    kvs = jax.ShapeDtypeStruct((nb, s, LANES), F32)
    return pl.pallas_call(
        body, name=name, grid=(nb, s // BLOCK), in_specs=_swa_in_specs() + [tile3],
        out_specs=[tile3, tile, tile, tile, tile, full((SW_HEADS, BLOCK, 2 * BLOCK)), full((8, LANES)),
                   full((1, LANES)), full((1, LANES))],
        out_shape=[jax.ShapeDtypeStruct((nb, s, 384), F32), kvs, kvs, kvs, kvs,
                   jax.ShapeDtypeStruct((SW_HEADS, BLOCK, 2 * BLOCK), F32), jax.ShapeDtypeStruct((8, LANES), F32),
                   jax.ShapeDtypeStruct((1, LANES), F32), jax.ShapeDtypeStruct((1, LANES), F32)],
        compiler_params=_cp("arbitrary", "arbitrary"))(proj, proj, proj, proj, proj, gq, gk, sinks, bias, do)


def _bias_grad(dbias, bucket, name):
    def body(db_ref, bk_ref, o_ref):
        bk = bk_ref[...]
        row = _iota((8, LANES), 0)
        col = _iota((8, LANES), 1)
        res = jnp.zeros((8, LANES), F32)
        for h in range(SW_HEADS):
            dbh = db_ref[h]
            for t in range(REL_BUCKETS):
                val = jnp.sum(jnp.where(bk == t, dbh, 0.0), keepdims=True)
                res = jnp.where((row == h) & (col == t), val, res)
        o_ref[...] = res

    return pl.pallas_call(body, name=name, out_shape=jax.ShapeDtypeStruct((8, LANES), F32))(dbias, bucket)


def _loss_grad(y, target, name):
    nb, s, d = y.shape
    tm = min(512, s)

    def body(y_ref, t_ref, loss_ref, dy_ref):
        @pl.when((pl.program_id(0) == 0) & (pl.program_id(1) == 0))
        def _():
            loss_ref[...] = jnp.zeros_like(loss_ref)

        e = y_ref[...] - t_ref[...]
        dy_ref[...] = e / d
        loss_ref[...] += 0.5 * jnp.sum(_rsum(e * e) / d, keepdims=True)

    tile = pl.BlockSpec((None, tm, d), lambda b, i: (b, i, 0))
    return pl.pallas_call(
        body, name=name, grid=(nb, s // tm), in_specs=[tile, tile],
        out_specs=[pl.BlockSpec((8, LANES), lambda b, i: (0, 0)), tile],
        out_shape=[jax.ShapeDtypeStruct((8, LANES), F32), jax.ShapeDtypeStruct((nb, s, d), F32)],
        compiler_params=_cp("arbitrary", "arbitrary"))(y, target)


def _adamw(parts, w, m, v, name):
    npart, r, _ = parts.shape
    tr = 512 if r % 512 == 0 else r
    bc1 = 1.0 - ADAM_B1 ** ADAM_STEP
    bc2 = 1.0 - ADAM_B2 ** ADAM_STEP

    def body(p_ref, w_ref, m_ref, v_ref, g_ref, d_ref, nm_ref, nv_ref):
        g = p_ref[0]
        for k in range(1, npart):
            g = g + p_ref[k]
        mn = ADAM_B1 * m_ref[...] + (1.0 - ADAM_B1) * g
        vn = ADAM_B2 * v_ref[...] + (1.0 - ADAM_B2) * (g * g)
        g_ref[...] = g
        nm_ref[...] = mn
        nv_ref[...] = vn
        d_ref[...] = -ADAM_LR * ((mn / bc1) / (jnp.sqrt(vn / bc2) + ADAM_EPS) + ADAM_WD * w_ref[...])

    tile = pl.BlockSpec((tr, LANES), lambda i: (i, 0))
    return pl.pallas_call(
        body, name=name, grid=(r // tr,),
        in_specs=[pl.BlockSpec((npart, tr, LANES), lambda i: (0, i, 0)), tile, tile, tile],
        out_specs=[tile] * 4, out_shape=[jax.ShapeDtypeStruct((r, LANES), F32)] * 4,
        compiler_params=_cp("parallel"))(parts, w, m, v)


def _pack_rows(vecs, dtype, row_mult):
    flat = jnp.concatenate([a.reshape(-1).astype(dtype) for a in vecs])
    rows = -(-flat.shape[0] // LANES)
    rows = -(-rows // row_mult) * row_mult
    return jnp.pad(flat, (0, rows * LANES - flat.shape[0])).reshape(rows, LANES)


def _unpack(flat, shapes, lead=()):
    out, off = [], 0
    for shp in shapes:
        size = 1
        for dim in shp:
            size *= dim
        out.append(flat[..., off:off + size].reshape(lead + tuple(shp)))
        off += size
    return out


def _t5_bucket():
    a = jnp.arange(BLOCK)[:, None]
    b = jnp.arange(2 * BLOCK)[None, :]
    dist = BLOCK + a - b
    max_exact = REL_BUCKETS // 2
    nn = jnp.maximum(dist, 0)
    nf = jnp.maximum(nn, 1).astype(F32)
    large = max_exact + (jnp.log(nf / max_exact) / math.log(BLOCK / max_exact)
                         * (REL_BUCKETS - max_exact)).astype(jnp.int32)
    large = jnp.minimum(large, REL_BUCKETS - 1)
    return jnp.where(nn < max_exact, nn, large).astype(jnp.int32)


def _pad_lanes(g, n):
    return jnp.pad(g, (0, n - g.shape[0])).reshape(1, n)


def kernel(x, c, positions, rel_table, norm1_g, norm2_g, w_ada, b_ada, w_in, mla_cq_g, w_uq, mla_ckv_g, w_ukv, mla_qn_g, mla_kn_g, sw_qn_g, sw_kn_g, sw_sinks, w_out, w_up, conv_w, conv_b, w_down, loss_target, m_rel_table, m_norm1_g, m_norm2_g, m_w_ada, m_b_ada, m_w_in, m_mla_cq_g, m_w_uq, m_mla_ckv_g, m_w_ukv, m_mla_qn_g, m_mla_kn_g, m_sw_qn_g, m_sw_kn_g, m_sw_sinks, m_w_out, m_w_up, m_conv_w, m_conv_b, m_w_down, v_rel_table, v_norm1_g, v_norm2_g, v_w_ada, v_b_ada, v_w_in, v_mla_cq_g, v_w_uq, v_mla_ckv_g, v_w_ukv, v_mla_qn_g, v_mla_kn_g, v_sw_qn_g, v_sw_kn_g, v_sw_sinks, v_w_out, v_w_up, v_conv_w, v_conv_b, v_w_down):
    nb, s, d = x.shape
    nl = norm1_g.shape[0]
    me = 4 * lax.axis_index("x") + 2 * lax.axis_index("y") + lax.axis_index("c")
    n_ada = w_ada.shape[2]

    small = _all_gather(_pack_rows([c, conv_w], F32, 8), "gather_cond_conv")
    small = small.reshape(N_DEV, -1)
    c_all = small[:, :nb * d].reshape(N_DEV * nb, d)
    ncw = nl * 3 * conv_w.shape[2]
    conv_full = small[:, nb * d:nb * d + ncw].reshape(N_DEV, nl, 3, -1).transpose(1, 2, 0, 3).reshape(nl, 3, -1)

    col_sharded = [w_in, w_uq, w_ukv, w_up]
    row_sharded = [w_out, w_down]
    wg = _all_gather(_pack_rows(col_sharded + row_sharded, MXU, 16), "gather_weights").reshape(N_DEV, -1)
    parts = _unpack(wg, [a.shape for a in col_sharded + row_sharded], (N_DEV,))
    w_in_f, w_uq_f, w_ukv_f, w_up_f = [p.transpose(1, 2, 0, 3).reshape(p.shape[1], p.shape[2], -1) for p in parts[:4]]
    w_out_f, w_down_f = [p.transpose(1, 0, 2, 3).reshape(p.shape[1], -1, p.shape[4 - 1]) for p in parts[4:]]
    zpad = lambda n: jnp.zeros((nl, d, n), MXU)
    w_in_p = jnp.concatenate([w_in_f[:, :, :1152], w_in_f[:, :, 1184:1824], zpad(64), w_in_f[:, :, 1152:1184], zpad(160)], axis=-1)
    w_uq_p = jnp.pad(w_uq_f.reshape(nl, 256, MLA_HEADS, MLA_QK), ((0, 0), (0, 0), (0, 0), (0, LANES - MLA_QK))).reshape(nl, 256, 768)

    b_my = lax.dynamic_slice_in_dim(b_ada, me * n_ada, n_ada, axis=1).reshape(nl, 1, n_ada)
    mods_my = _ada_fwd(c_all, w_ada, b_my, "ada_fwd")
    mods = _all_gather(mods_my.reshape(nl * N_DEV * nb, n_ada), "gather_mods")
    mods = mods.reshape(N_DEV, nl, N_DEV * nb, n_ada).transpose(1, 2, 0, 3).reshape(nl, N_DEV * nb, N_DEV * n_ada)
    mods = lax.dynamic_slice_in_dim(mods, me * nb, nb, axis=1)
    shift1, scale1, gate1, shift2, scale2, gate2 = [mods[:, :, k * d:(k + 1) * d].reshape(nl, nb, 1, d) for k in range(6)]

    half = 16
    inv_freq = jnp.power(ROPE_THETA, -jnp.arange(half, dtype=F32) / half)
    ang = positions.astype(F32)[..., None] * inv_freq
    ones = lambda n: jnp.ones((nb, s, n), F32)
    zeros = lambda n: jnp.zeros((nb, s, n), F32)
    rope_c = jnp.concatenate([ones(64), jnp.cos(ang), jnp.cos(ang), ones(32)], axis=-1)
    rope_s = jnp.concatenate([zeros(64), jnp.sin(ang), jnp.sin(ang), zeros(32)], axis=-1)
    bucket = _t5_bucket()
    bias = jnp.transpose(rel_table[bucket], (2, 0, 1))

    row = lambda g: g.reshape(1, -1)
    twice = lambda g: jnp.concatenate([g, g]).reshape(1, LANES)

    saved = []
    xl = x
    for l in range(nl):
        proj, h1 = _ln_mod_matmul(xl, row(norm1_g[l]), scale1[l], shift1[l], w_in_p[l], f"l{l}_in_proj")
        prep_args = (proj, rope_c, rope_s, row(mla_cq_g[l]), row(mla_ckv_g[l]), _pad_lanes(mla_qn_g[l], LANES),
                     _pad_lanes(mla_kn_g[l], LANES), w_uq_p[l], w_ukv_f[l])
        qm, km, vm = _mla_prep(*prep_args, f"l{l}_mla_prep")
        o_a, ct_a = _sb_fwd(proj, f"l{l}_sb_fwd")
        o_b, lse_b = _mla_fwd(qm, km, vm, f"l{l}_mla_fwd")
        sinks = jnp.broadcast_to(jnp.pad(sw_sinks[l], (0, 2))[:, None], (8, LANES))
        swa_args = (proj, twice(sw_qn_g[l]), twice(sw_kn_g[l]), sinks, bias)
        o_c = _swa_fwd(*swa_args, f"l{l}_swa_fwd")
        wo = [w_out_f[l, :256], w_out_f[l, 256:640], w_out_f[l, 640:]]
        x_mid, y1 = _out_proj([o_a, o_b, o_c], wo, gate1[l], xl, f"l{l}_out_proj")
        u_pre, h2 = _ln_mod_matmul(x_mid, row(norm2_g[l]), scale2[l], shift2[l], w_up_f[l], f"l{l}_up_proj")
        x_out, y2 = _conv_gate_matmul(u_pre, conv_full[l], row(conv_b[l]), w_down_f[l], gate2[l], x_mid, f"l{l}_ffn_down")
        saved.append(dict(x=xl, proj=proj, h1=h1, prep=prep_args, qkv=(qm, km, vm), o_a=o_a, ct_a=ct_a, o_b=o_b, lse_b=lse_b,
                          swa=swa_args, o_c=o_c, wo=wo, y1=y1, x_mid=x_mid, u_pre=u_pre, h2=h2, y2=y2))
        xl = x_out

    loss_blk, dx = _loss_grad(xl, loss_target, "loss")
    loss = lax.psum(loss_blk[0, 0], ("x", "y", "c"))

    t = nb * s
    flat = lambda a: a.reshape(t, a.shape[-1])
    grads = [None] * nl
    dmods = [None] * nl
    dbias = jnp.zeros((SW_HEADS, BLOCK, 2 * BLOCK), F32)
    for l in reversed(range(nl)):
        sv = saved[l]
        (da,), dy2, dgate2 = _gate_bwd_nt(dx, sv["y2"], gate2[l], [w_down_f[l]], f"l{l}_ffn_down_bwd")
        du, a_act, cstats = _conv_gate_bwd(da, sv["u_pre"], conv_full[l], row(conv_b[l]), f"l{l}_conv_gate_bwd")
        dx_mid, du_pre, dshift2, dscale2, dg2 = _ln_mod_matmul_bwd(
            du, w_up_f[l], sv["x_mid"], row(norm2_g[l]), scale2[l], dx, conv_full[l], f"l{l}_up_proj_bwd")
        g_w_down = _wgrad(flat(a_act), flat(dy2), f"l{l}_w_down_grad")
        g_w_up = _wgrad(flat(sv["h2"]), flat(du_pre), f"l{l}_w_up_grad")

        (do_a, do_b, do_c), dy1, dgate1 = _gate_bwd_nt(dx_mid, sv["y1"], gate1[l], sv["wo"], f"l{l}_out_proj_bwd")
        mix = jnp.concatenate([sv["o_a"], sv["o_b"], sv["o_c"]], axis=-1).astype(MXU)
        g_w_out = _wgrad(flat(mix), flat(dy1), f"l{l}_w_out_grad")

        dsb_q, dsb_k, dsb_v = _sb_bwd(sv["proj"], sv["ct_a"], do_a, f"l{l}_sb_bwd")
        qm, km, vm = sv["qkv"]
        dqm, dkm, dvm = _mla_bwd(qm, km, vm, sv["o_b"], sv["lse_b"], do_b, f"l{l}_mla_bwd")
        dsw_q, dkc, dkp, dvc, dvp, dbias_l, dsinks, dg_swq, dg_swk = _swa_bwd(*sv["swa"], do_c, f"l{l}_swa_bwd")
        dbias = dbias + dbias_l
        shift_up = lambda a: jnp.concatenate([a[:, BLOCK:], jnp.zeros((nb, BLOCK, LANES), F32)], axis=1)
        dsw_k = dkc + shift_up(dkp)
        dsw_v = dvc + shift_up(dvp)
        dcq, dckv, dkr, g_w_uq_p, g_w_ukv, dg_cq, dg_ckv, dg_qn, dg_kn = _mla_prep_bwd(
            *sv["prep"], dqm, dkm, dvm, f"l{l}_mla_prep_bwd")
        dproj = jnp.concatenate([dsb_q, dsb_k, dsb_v, dcq, dckv, dsw_q, dsw_k, dsw_v, dkr, zeros(128)], axis=-1)
        dx, dproj_m, dshift1, dscale1, dg1 = _ln_mod_matmul_bwd(
            dproj, w_in_p[l], sv["x"], row(norm1_g[l]), scale1[l], dx_mid, None, f"l{l}_in_proj_bwd")
        g_w_in_p = _wgrad(flat(sv["h1"]), flat(dproj_m), f"l{l}_w_in_grad")

        g_w_in = jnp.concatenate([g_w_in_p[:, :1152], g_w_in_p[:, 1856:1888], g_w_in_p[:, 1152:1792]], axis=-1)
        g_w_uq = g_w_uq_p.reshape(256, MLA_HEADS, LANES)[:, :, :MLA_QK].reshape(256, MLA_HEADS * MLA_QK)
        dmods[l] = jnp.concatenate([dshift1, dscale1, dgate1, dshift2, dscale2, dgate2], axis=-1).reshape(nb, 6 * d)
        grads[l] = dict(
            w_in=g_w_in, w_uq=g_w_uq, w_ukv=g_w_ukv, w_out=g_w_out, w_up=g_w_up, conv_w=cstats[1:4], w_down=g_w_down,
            norm1_g=dg1[0], norm2_g=dg2[0], mla_cq_g=dg_cq[0], mla_ckv_g=dg_ckv[0], mla_qn_g=dg_qn[0, :MLA_QK],
            mla_kn_g=dg_kn[0, :MLA_QK], sw_qn_g=dg_swq[0, :HEAD] + dg_swq[0, HEAD:], sw_kn_g=dg_swk[0, :HEAD] + dg_swk[0, HEAD:],
            sw_sinks=dsinks[:SW_HEADS, 0], conv_b=cstats[0])
    grad_x = dx
    g_rel = _bias_grad(dbias, bucket, "rel_table_grad")[:SW_HEADS, :REL_BUCKETS].T
    stack = lambda k: jnp.stack([grads[l][k] for l in range(nl)])

    dm_all = _all_gather(jnp.stack(dmods).reshape(nl * nb, 6 * d), "gather_dmods")
    dm_all = dm_all.reshape(N_DEV, nl, nb, 6 * d).transpose(1, 0, 2, 3).reshape(nl, N_DEV * nb, 6 * d)
    dm_my = lax.dynamic_slice_in_dim(dm_all, me * n_ada, n_ada, axis=2)
    g_w_ada, g_b_ada = _ada_bwd(c_all, dm_my, dm_all, "ada_bwd")
    g_b_ada = g_b_ada.reshape(nl, 6 * d)

    big_names = ["w_in", "w_uq", "w_ukv", "w_out", "w_up", "conv_w", "w_down"]
    big_w = dict(w_in=w_in, w_uq=w_uq, w_ukv=w_ukv, w_out=w_out, w_up=w_up, conv_w=conv_w, w_down=w_down)
    big_m = dict(w_in=m_w_in, w_uq=m_w_uq, w_ukv=m_w_ukv, w_out=m_w_out, w_up=m_w_up, conv_w=m_conv_w, w_down=m_w_down)
    big_v = dict(w_in=v_w_in, w_uq=v_w_uq, w_ukv=v_w_ukv, w_out=v_w_out, w_up=v_w_up, conv_w=v_conv_w, w_down=v_w_down)
    chunks = []
    for k in big_names:
        g = stack(k)
        if k in ("w_out", "w_down"):
            g = g.reshape(nl, N_DEV, -1, g.shape[2]).transpose(1, 0, 2, 3)
        else:
            g = g.reshape(nl, g.shape[1], N_DEV, -1).transpose(2, 0, 1, 3)
        chunks.append(g.reshape(N_DEV, -1))
    send = jnp.concatenate(chunks, axis=1)
    n_big = send.shape[1]
    rows_big = -(-n_big // (512 * LANES)) * 512
    send = jnp.pad(send, ((0, 0), (0, rows_big * LANES - n_big))).reshape(N_DEV, rows_big, LANES)
    recv = _all_to_all(send, "exchange_grads")
    packf = lambda dct, names, rows: jnp.pad(jnp.concatenate([dct[k].reshape(-1) for k in names]),
                                             (0, rows * LANES - sum(dct[k].size for k in names))).reshape(rows, LANES)
    big_out = _adamw(recv, packf(big_w, big_names, rows_big), packf(big_m, big_names, rows_big),
                     packf(big_v, big_names, rows_big), "adamw_sharded")
    big_out = [dict(zip(big_names, _unpack(o.reshape(-1), [big_w[k].shape for k in big_names]))) for o in big_out]

    small_names = ["rel_table", "norm1_g", "norm2_g", "mla_cq_g", "mla_ckv_g", "mla_qn_g", "mla_kn_g",
                   "sw_qn_g", "sw_kn_g", "sw_sinks", "conv_b"]
    small_w = dict(rel_table=rel_table, norm1_g=norm1_g, norm2_g=norm2_g, mla_cq_g=mla_cq_g, mla_ckv_g=mla_ckv_g,
                   mla_qn_g=mla_qn_g, mla_kn_g=mla_kn_g, sw_qn_g=sw_qn_g, sw_kn_g=sw_kn_g, sw_sinks=sw_sinks, conv_b=conv_b)
    small_m = dict(rel_table=m_rel_table, norm1_g=m_norm1_g, norm2_g=m_norm2_g, mla_cq_g=m_mla_cq_g, mla_ckv_g=m_mla_ckv_g,
                   mla_qn_g=m_mla_qn_g, mla_kn_g=m_mla_kn_g, sw_qn_g=m_sw_qn_g, sw_kn_g=m_sw_kn_g, sw_sinks=m_sw_sinks, conv_b=m_conv_b)
    small_v = dict(rel_table=v_rel_table, norm1_g=v_norm1_g, norm2_g=v_norm2_g, mla_cq_g=v_mla_cq_g, mla_ckv_g=v_mla_ckv_g,
                   mla_qn_g=v_mla_qn_g, mla_kn_g=v_mla_kn_g, sw_qn_g=v_sw_qn_g, sw_kn_g=v_sw_kn_g, sw_sinks=v_sw_sinks, conv_b=v_conv_b)
    small_g = {k: (g_rel if k == "rel_table" else stack(k)) for k in small_names}
    n_small = sum(small_w[k].size for k in small_names)
    rows_small = -(-n_small // (8 * LANES)) * 8
    small_parts = _all_gather(packf(small_g, small_names, rows_small), "gather_small_grads")
    small_out = _adamw(small_parts, packf(small_w, small_names, rows_small), packf(small_m, small_names, rows_small),
                       packf(small_v, small_names, rows_small), "adamw_replicated")
    small_out = [dict(zip(small_names, _unpack(o.reshape(-1), [small_w[k].shape for k in small_names]))) for o in small_out]

    ada_names = ["w_ada", "b_ada"]
    ada_w, ada_m, ada_v = dict(w_ada=w_ada, b_ada=b_ada), dict(w_ada=m_w_ada, b_ada=m_b_ada), dict(w_ada=v_w_ada, b_ada=v_b_ada)
    ada_g = dict(w_ada=g_w_ada, b_ada=g_b_ada)
    n_adaw = w_ada.size + b_ada.size
    rows_ada = -(-n_adaw // (512 * LANES)) * 512
    ada_out = _adamw(packf(ada_g, ada_names, rows_ada)[None], packf(ada_w, ada_names, rows_ada),
                     packf(ada_m, ada_names, rows_ada), packf(ada_v, ada_names, rows_ada), "adamw_ada")
    ada_out = [dict(zip(ada_names, _unpack(o.reshape(-1), [ada_w[k].shape for k in ada_names]))) for o in ada_out]

    order = ["rel_table", "norm1_g", "norm2_g", "w_ada", "b_ada", "w_in", "mla_cq_g", "w_uq", "mla_ckv_g", "w_ukv",
             "mla_qn_g", "mla_kn_g", "sw_qn_g", "sw_kn_g", "sw_sinks", "w_out", "w_up", "conv_w", "conv_b", "w_down"]
    outs = [{**big_out[k], **small_out[k], **ada_out[k]} for k in range(4)]
    return (loss, grad_x, *[outs[0][n] for n in order], *[outs[1][n] for n in order],
            *[outs[2][n] for n in order], *[outs[3][n] for n in order])
```

```python
import math

import jax
import jax.numpy as jnp
from jax import lax
from jax.experimental import pallas as pl
from jax.experimental.pallas import tpu as pltpu

F32 = jnp.float32
MXU = jnp.bfloat16
EPS = 1e-6
NEG = -1e30
VMEM_LIMIT_BYTES = 56 * 1024 * 1024
N_DEV = 8
MESH = pl.DeviceIdType.MESH

D_MODEL = 1024
D_FF = 2816
HEAD = 64
LANES = 128
MLA_HEADS = 6
MLA_QK = 96
SW_HEADS = 6
REL_BUCKETS = 32
BLOCK = 128
SB_SCALE = HEAD ** -0.5
SB_DEAD = -105.0
SW_SCALE = HEAD ** -0.5
MLA_SCALE = MLA_QK ** -0.5
ROPE_THETA = 10000.0
D_IN_PAD = 2048
COL_SBQ, COL_SBK, COL_SBV, COL_CQ, COL_CKV, COL_SWQ, COL_SWK, COL_SWV, COL_KR = 0, 256, 512, 768, 1024, 1152, 1536, 1664, 1792

ADAMW_WHOLE_BYTES = 1 << 20
ADAMW_TILE_BYTES = 1 << 19
ADAM_LR, ADAM_B1, ADAM_B2, ADAM_EPS, ADAM_WD, ADAM_STEP = 0.001, 0.9, 0.999, 1e-08, 0.01, 10


def _cp(*sem):
    return pltpu.CompilerParams(dimension_semantics=sem, vmem_limit_bytes=VMEM_LIMIT_BYTES)


def _iota(shape, dim):
    return lax.broadcasted_iota(jnp.int32, shape, dim)


def _dot(a, b):
    return jnp.dot(a, b, preferred_element_type=F32)


def _dot_nt(a, b):
    return lax.dot_general(a, b, (((1,), (1,)), ((), ())), preferred_element_type=F32)


def _dot_tn(a, b):
    return lax.dot_general(a, b, (((0,), (0,)), ((), ())), preferred_element_type=F32)


def _cumdot(x, u):
    hi = x.astype(MXU)
    r = x - hi.astype(F32)
    mid = r.astype(MXU)
    lo = (r - mid.astype(F32)).astype(MXU)
    return _dot(hi, u) + _dot(mid, u) + _dot(lo, u)


def _sigmoid(x):
    return 1.0 / (1.0 + jnp.exp(-x))


def _rsum(x):
    return jnp.sum(x, axis=-1, keepdims=True)


def _csum(x):
    return jnp.sum(x, axis=0, keepdims=True)


def _all_gather(xs, name):
    na = len(xs)

    def body(*refs):
        x_refs, out_refs = refs[:na], refs[na:2 * na]
        send_sems, recv_sems, local_sems = refs[2 * na:]
        x, y, c = lax.axis_index("x"), lax.axis_index("y"), lax.axis_index("c")
        me, sibling = (x, y, c), (x, y, 1 - c)
        chips = [(1 - x, y), (x, 1 - y), (1 - x, 1 - y)]

        def slot(a, px, py, pc):
            return out_refs[a].at[4 * px + 2 * py + pc]

        def copy(a, k, block, to, src=None):
            return pltpu.make_async_remote_copy(
                src_ref=slot(a, *block) if src is None else src, dst_ref=slot(a, *block),
                send_sem=send_sems.at[7 * a + k], recv_sem=recv_sems.at[7 * a + k], device_id=to, device_id_type=MESH)

        mines, sends = [], []
        for a in range(na):
            mines.append(pltpu.make_async_copy(x_refs[a], slot(a, *me), local_sems.at[a]))
            mines[-1].start()
            first = [copy(a, 0, me, sibling, src=x_refs[a])]
            first += [copy(a, 1 + j, me, (*chip, c), src=x_refs[a]) for j, chip in enumerate(chips)]
            for cp in first:
                cp.start()
            sends += first
        for j, chip in enumerate(chips):
            for a in range(na):
                copy(a, 1 + j, (*chip, c), me).wait_recv()
                sends.append(copy(a, 4 + j, (*chip, c), sibling))
                sends[-1].start()
        for a in range(na):
            copy(a, 0, sibling, me).wait_recv()
            for j, chip in enumerate(chips):
                copy(a, 4 + j, (*chip, 1 - c), me).wait_recv()
        for cp in sends:
            cp.wait_send()
        for mine in mines:
            mine.wait()

    hbm = pl.BlockSpec(memory_space=pl.ANY)
    return pl.pallas_call(
        body, name=name, out_shape=[jax.ShapeDtypeStruct((N_DEV,) + a.shape, a.dtype) for a in xs],
        in_specs=[hbm] * na, out_specs=[hbm] * na,
        scratch_shapes=[pltpu.SemaphoreType.DMA((7 * na,)), pltpu.SemaphoreType.DMA((7 * na,)),
                        pltpu.SemaphoreType.DMA((na,))],
    )(*xs)


def _all_to_all(xs, name):
    na = len(xs)
    flips = [(dx, dy, dc) for dx in (0, 1) for dy in (0, 1) for dc in (0, 1) if dx or dy or dc]

    def body(*refs):
        x_refs, out_refs = refs[:na], refs[na:2 * na]
        send_sems, recv_sems, local_sems = refs[2 * na:]
        x, y, c = lax.axis_index("x"), lax.axis_index("y"), lax.axis_index("c")
        me = 4 * x + 2 * y + c
        mines, copies = [], []
        for a in range(na):
            mines.append(pltpu.make_async_copy(x_refs[a].at[me], out_refs[a].at[me], local_sems.at[a]))
            mines[-1].start()
        for k, (dx, dy, dc) in enumerate(flips):
            px = 1 - x if dx else x
            py = 1 - y if dy else y
            pc = 1 - c if dc else c
            for a in range(na):
                copies.append(pltpu.make_async_remote_copy(
                    src_ref=x_refs[a].at[4 * px + 2 * py + pc], dst_ref=out_refs[a].at[me],
                    send_sem=send_sems.at[7 * a + k], recv_sem=recv_sems.at[7 * a + k],
                    device_id=(px, py, pc), device_id_type=MESH))
                copies[-1].start()
        for cp in copies:
            cp.wait()
        for mine in mines:
            mine.wait()

    hbm = pl.BlockSpec(memory_space=pl.ANY)
    return pl.pallas_call(
        body, name=name, out_shape=[jax.ShapeDtypeStruct(a.shape, a.dtype) for a in xs],
        in_specs=[hbm] * na, out_specs=[hbm] * na,
        scratch_shapes=[pltpu.SemaphoreType.DMA((7 * na,)), pltpu.SemaphoreType.DMA((7 * na,)),
                        pltpu.SemaphoreType.DMA((na,))],
    )(*xs)


def _ada_fwd(c_all, w_ada, b_my, name):
    nl, d, n = w_ada.shape
    nb = c_all.shape[0]

    def body(c_ref, w_ref, b_ref, o_ref):
        cv = c_ref[...]
        sc = (cv * _sigmoid(cv)).astype(MXU)
        o_ref[...] = _dot(sc, w_ref[...].astype(MXU)) + b_ref[...]

    return pl.pallas_call(
        body, name=name, grid=(nl,),
        in_specs=[pl.BlockSpec((nb, d), lambda l: (0, 0)),
                  pl.BlockSpec((None, d, n), lambda l: (l, 0, 0)),
                  pl.BlockSpec((None, 1, n), lambda l: (l, 0, 0))],
        out_specs=pl.BlockSpec((None, nb, n), lambda l: (l, 0, 0)),
        out_shape=jax.ShapeDtypeStruct((nl, nb, n), F32),
        compiler_params=_cp("parallel"))(c_all, w_ada, b_my)


def _ada_bwd(c_all, dmods_my, dmods_all, name):
    nl, nb, n = dmods_my.shape
    d = c_all.shape[1]
    nfull = dmods_all.shape[2]

    def body(c_ref, dm_ref, da_ref, dw_ref, db_ref):
        cv = c_ref[...]
        sc = (cv * _sigmoid(cv)).astype(MXU)
        dw_ref[...] = _dot_tn(sc, dm_ref[...].astype(MXU))
        db_ref[...] = _csum(da_ref[...])

    return pl.pallas_call(
        body, name=name, grid=(nl,),
        in_specs=[pl.BlockSpec((nb, d), lambda l: (0, 0)),
                  pl.BlockSpec((None, nb, n), lambda l: (l, 0, 0)),
                  pl.BlockSpec((None, nb, nfull), lambda l: (l, 0, 0))],
        out_specs=[pl.BlockSpec((None, d, n), lambda l: (l, 0, 0)),
                   pl.BlockSpec((None, 1, nfull), lambda l: (l, 0, 0))],
        out_shape=[jax.ShapeDtypeStruct((nl, d, n), F32), jax.ShapeDtypeStruct((nl, 1, nfull), F32)],
        compiler_params=_cp("parallel"))(c_all, dmods_my, dmods_all)


def _ln_mod_matmul(x, g, scale, shift, w, name):
    nb, s, d = x.shape
    n = w.shape[0]
    tm, tn = min(512, s), 512

    def body(x_ref, g_ref, sc_ref, sh_ref, w_ref, y_ref, h_ref, h_s):
        @pl.when(pl.program_id(2) == 0)
        def _():
            xf = x_ref[...]
            rstd = lax.rsqrt(jnp.mean(xf * xf, axis=-1, keepdims=True) + EPS)
            hv = (xf * rstd * g_ref[...]) * (1.0 + sc_ref[...]) + sh_ref[...]
            h_s[...] = hv.astype(MXU)
            h_ref[...] = h_s[...]

        y_ref[...] = _dot_nt(h_s[...], w_ref[...])

    return pl.pallas_call(
        body, name=name, grid=(nb, s // tm, n // tn),
        in_specs=[pl.BlockSpec((None, tm, d), lambda b, i, j: (b, i, 0)),
                  pl.BlockSpec((1, d), lambda b, i, j: (0, 0)),
                  pl.BlockSpec((None, 1, d), lambda b, i, j: (b, 0, 0)),
                  pl.BlockSpec((None, 1, d), lambda b, i, j: (b, 0, 0)),
                  pl.BlockSpec((tn, d), lambda b, i, j: (j, 0))],
        out_specs=[pl.BlockSpec((None, tm, tn), lambda b, i, j: (b, i, j)),
                   pl.BlockSpec((None, tm, d), lambda b, i, j: (b, i, 0))],
        out_shape=[jax.ShapeDtypeStruct((nb, s, n), F32), jax.ShapeDtypeStruct((nb, s, d), MXU)],
        scratch_shapes=[pltpu.VMEM((tm, d), MXU)],
        compiler_params=_cp("parallel", "parallel", "arbitrary"))(x, g, scale, shift, w)


def _ln_mod_matmul_bwd(dy, w, x, g, scale, dres, conv_w, name):
    nb, s, n = dy.shape
    d = x.shape[-1]
    tm, tn = min(256, s), 512
    ni, nj = s // tm, n // tn
    hb = tm // 8
    conv = conv_w is not None

    def body(*refs):
        if conv:
            dy_ref, nx_ref, cw_ref = refs[:3]
            refs = refs[3:]
        else:
            dy_ref = refs[0]
            refs = refs[1:]
        w_ref, x_ref, g_ref, sc_ref, dr_ref, dx_ref, dyp_ref, dsh_ref, dsc_ref, dg_ref, acc = refs
        b, i, j = pl.program_id(0), pl.program_id(1), pl.program_id(2)

        @pl.when(j == 0)
        def _():
            acc[...] = jnp.zeros_like(acc)

        @pl.when((j == 0) & (i == 0))
        def _():
            dsh_ref[...] = jnp.zeros_like(dsh_ref)
            dsc_ref[...] = jnp.zeros_like(dsc_ref)

        @pl.when((j == 0) & (i == 0) & (b == 0))
        def _():
            dg_ref[...] = jnp.zeros_like(dg_ref)

        dv = dy_ref[...]
        if conv:
            rows = _iota((tm, 1), 0)
            nx = jnp.where(i == ni - 1, 0.0, nx_ref[...])
            n1 = jnp.where(rows == tm - 1, nx[0:1, :], pltpu.roll(dv, tm - 1, 0))
            n2 = jnp.where(rows == tm - 2, nx[0:1, :], jnp.where(rows == tm - 1, nx[1:2, :], pltpu.roll(dv, tm - 2, 0)))
            cw = cw_ref[...]
            dv = cw[2:3, :] * dv + cw[1:2, :] * n1 + cw[0:1, :] * n2
        dp = dv.astype(MXU)
        dyp_ref[...] = dp
        acc[...] += _dot(dp, w_ref[...])

        @pl.when(j == nj - 1)
        def _():
            dh = acc[...]
            xf = x_ref[...]
            rstd = lax.rsqrt(jnp.mean(xf * xf, axis=-1, keepdims=True) + EPS)
            xn = xf * rstd
            gg = g_ref[...]
            sc1 = 1.0 + sc_ref[...]
            dsh_ref[...] += _csum(dh)
            dsc_ref[...] += _csum(dh * xn * gg)
            dg_ref[...] += _csum(dh * xn * sc1)
            dn = dh * gg * sc1
            dx_ref[...] = dr_ref[...] + rstd * (dn - xn * jnp.mean(dn * xn, axis=-1, keepdims=True))

    in_specs = [pl.BlockSpec((None, tm, tn), lambda b, i, j: (b, i, j))]
    args = [dy]
    if conv:
        in_specs += [pl.BlockSpec((None, 8, tn), lambda b, i, j: (b, jnp.minimum((i + 1) * hb, s // 8 - 1), j)),
                     pl.BlockSpec((3, tn), lambda b, i, j: (0, j))]
        args += [dy, conv_w]
    in_specs += [pl.BlockSpec((tn, d), lambda b, i, j: (j, 0)),
                 pl.BlockSpec((None, tm, d), lambda b, i, j: (b, i, 0)),
                 pl.BlockSpec((1, d), lambda b, i, j: (0, 0)),
                 pl.BlockSpec((None, 1, d), lambda b, i, j: (b, 0, 0)),
                 pl.BlockSpec((None, tm, d), lambda b, i, j: (b, i, 0))]
    args += [w, x, g, scale, dres]
    return pl.pallas_call(
        body, name=name, grid=(nb, ni, nj), in_specs=in_specs,
        out_specs=[pl.BlockSpec((None, tm, d), lambda b, i, j: (b, i, 0)),
                   pl.BlockSpec((None, tm, tn), lambda b, i, j: (b, i, j)),
                   pl.BlockSpec((None, 1, d), lambda b, i, j: (b, 0, 0)),
                   pl.BlockSpec((None, 1, d), lambda b, i, j: (b, 0, 0)),
                   pl.BlockSpec((1, d), lambda b, i, j: (0, 0))],
        out_shape=[jax.ShapeDtypeStruct((nb, s, d), F32), jax.ShapeDtypeStruct((nb, s, n), MXU),
                   jax.ShapeDtypeStruct((nb, 1, d), F32), jax.ShapeDtypeStruct((nb, 1, d), F32),
                   jax.ShapeDtypeStruct((1, d), F32)],
        scratch_shapes=[pltpu.VMEM((tm, d), F32)],
        compiler_params=_cp("arbitrary", "arbitrary", "arbitrary"))(*args)


def _wgrad(xm, dym, name):
    t, k = xm.shape
    n = dym.shape[1]
    tk = 512 if k % 512 == 0 else 256
    tn = 512
    tt = min(512, t)

    def body(x_ref, dy_ref, o_ref):
        @pl.when(pl.program_id(2) == 0)
        def _():
            o_ref[...] = jnp.zeros_like(o_ref)

        o_ref[...] += _dot_tn(x_ref[...], dy_ref[...])

    return pl.pallas_call(
        body, name=name, grid=(k // tk, n // tn, t // tt),
        in_specs=[pl.BlockSpec((tt, tk), lambda a, b, c: (c, a)),
                  pl.BlockSpec((tt, tn), lambda a, b, c: (c, b))],
        out_specs=pl.BlockSpec((tk, tn), lambda a, b, c: (a, b)),
        out_shape=jax.ShapeDtypeStruct((k, n), F32),
        compiler_params=_cp("parallel", "parallel", "arbitrary"))(xm, dym)


def _out_proj(parts, ws, gate, res, name):
    nb, s, d = res.shape
    tm = min(512, s)
    npart = len(parts)

    def body(*refs):
        p_refs, w_refs = refs[:npart], refs[npart:2 * npart]
        gt_ref, res_ref, xo_ref, y_ref = refs[2 * npart:]
        y = _dot(p_refs[0][...].astype(MXU), w_refs[0][...])
        for p_ref, w_ref in zip(p_refs[1:], w_refs[1:]):
            y = y + _dot(p_ref[...].astype(MXU), w_ref[...])
        y_ref[...] = y
        xo_ref[...] = res_ref[...] + gt_ref[...] * y

    in_specs = [pl.BlockSpec((None, tm, p.shape[-1]), lambda b, i: (b, i, 0)) for p in parts]
    in_specs += [pl.BlockSpec(w.shape, lambda b, i: (0, 0)) for w in ws]
    in_specs += [pl.BlockSpec((None, 1, d), lambda b, i: (b, 0, 0)),
                 pl.BlockSpec((None, tm, d), lambda b, i: (b, i, 0))]
    return pl.pallas_call(
        body, name=name, grid=(nb, s // tm), in_specs=in_specs,
        out_specs=[pl.BlockSpec((None, tm, d), lambda b, i: (b, i, 0))] * 2,
        out_shape=[jax.ShapeDtypeStruct((nb, s, d), F32)] * 2,
        compiler_params=_cp("parallel", "parallel"))(*parts, *ws, gate, res)


def _gate_bwd_nt(dx, y, gate, ws, name):
    nb, s, d = dx.shape
    tm = min(256, s)
    npart = len(ws)

    def body(*refs):
        dx_ref, y_ref, gt_ref = refs[:3]
        w_refs = refs[3:3 + npart]
        da_refs = refs[3 + npart:3 + 2 * npart]
        dy_ref, dgt_ref = refs[3 + 2 * npart:]

        @pl.when(pl.program_id(1) == 0)
        def _():
            dgt_ref[...] = jnp.zeros_like(dgt_ref)

        dxv = dx_ref[...]
        dyv = (dxv * gt_ref[...]).astype(MXU)
        dy_ref[...] = dyv
        dgt_ref[...] += _csum(dxv * y_ref[...])
        for w_ref, da_ref in zip(w_refs, da_refs):
            da_ref[...] = _dot_nt(dyv, w_ref[...])

    tile = pl.BlockSpec((None, tm, d), lambda b, i: (b, i, 0))
    row = pl.BlockSpec((None, 1, d), lambda b, i: (b, 0, 0))
    outs = pl.pallas_call(
        body, name=name, grid=(nb, s // tm),
        in_specs=[tile, tile, row] + [pl.BlockSpec(w.shape, lambda b, i: (0, 0)) for w in ws],
        out_specs=[pl.BlockSpec((None, tm, w.shape[0]), lambda b, i: (b, i, 0)) for w in ws] + [tile, row],
        out_shape=[jax.ShapeDtypeStruct((nb, s, w.shape[0]), F32) for w in ws]
        + [jax.ShapeDtypeStruct((nb, s, d), MXU), jax.ShapeDtypeStruct((nb, 1, d), F32)],
        compiler_params=_cp("arbitrary", "arbitrary"))(dx, y, gate, *ws)
    return outs[:npart], outs[npart], outs[npart + 1]


def _conv_shifts(xv, halo, rows):
    p1 = jnp.where(rows == 0, halo[7:8, :], pltpu.roll(xv, 1, 0))
    p2 = jnp.where(rows == 0, halo[6:7, :], jnp.where(rows == 1, halo[7:8, :], pltpu.roll(xv, 2, 0)))
    return p1, p2


def _conv_gate_matmul(u, cw, cb, wd, gate, res, name):
    nb, s, f2 = u.shape
    f = f2 // 2
    d = wd.shape[1]
    tm = min(256, s)
    tk = f // 2
    nk = f // tk
    hb = tm // 8

    def body(ug_ref, uv_ref, hg_ref, hv_ref, cwg_ref, cwv_ref, cbg_ref, cbv_ref, wd_ref, gt_ref, res_ref,
             xo_ref, y_ref, acc):
        i, k = pl.program_id(1), pl.program_id(2)

        @pl.when(k == 0)
        def _():
            acc[...] = jnp.zeros_like(acc)

        rows = _iota((tm, 1), 0)

        def conv(x_ref, h_ref, w_ref, b_ref):
            xv = x_ref[...]
            halo = jnp.where(i == 0, 0.0, h_ref[...])
            p1, p2 = _conv_shifts(xv, halo, rows)
            wv = w_ref[...]
            return wv[2:3, :] * xv + wv[1:2, :] * p1 + wv[0:1, :] * p2 + b_ref[...]

        gv = conv(ug_ref, hg_ref, cwg_ref, cbg_ref)
        vv = conv(uv_ref, hv_ref, cwv_ref, cbv_ref)
        av = gv * _sigmoid(gv) * vv
        acc[...] += _dot(av.astype(MXU), wd_ref[...])

        @pl.when(k == nk - 1)
        def _():
            y = acc[...]
            y_ref[...] = y
            xo_ref[...] = res_ref[...] + gt_ref[...] * y

    def halo_idx(off):
        return lambda b, i, k: (b, jnp.maximum(i * hb - 1, 0), k + off)

    tile = pl.BlockSpec((None, tm, d), lambda b, i, k: (b, i, 0))
    return pl.pallas_call(
        body, name=name, grid=(nb, s // tm, nk),
        in_specs=[pl.BlockSpec((None, tm, tk), lambda b, i, k: (b, i, k)),
                  pl.BlockSpec((None, tm, tk), lambda b, i, k: (b, i, k + nk)),
                  pl.BlockSpec((None, 8, tk), halo_idx(0)),
                  pl.BlockSpec((None, 8, tk), halo_idx(nk)),
                  pl.BlockSpec((3, tk), lambda b, i, k: (0, k)),
                  pl.BlockSpec((3, tk), lambda b, i, k: (0, k + nk)),
                  pl.BlockSpec((1, tk), lambda b, i, k: (0, k)),
                  pl.BlockSpec((1, tk), lambda b, i, k: (0, k + nk)),
                  pl.BlockSpec((tk, d), lambda b, i, k: (k, 0)),
                  pl.BlockSpec((None, 1, d), lambda b, i, k: (b, 0, 0)),
                  tile],
        out_specs=[tile, tile],
        out_shape=[jax.ShapeDtypeStruct((nb, s, d), F32)] * 2,
        scratch_shapes=[pltpu.VMEM((tm, d), F32)],
        compiler_params=_cp("parallel", "parallel", "arbitrary"))(u, u, u, u, cw, cw, cb, cb, wd, gate, res)


def _conv_gate_bwd(da, u, cw, cb, name):
    nb, s, f2 = u.shape
    f = f2 // 2
    tm = min(128, s)
    hb = tm // 8

    def body(da_ref, u_ref, h_ref, cw_ref, cb_ref, du_ref, a_ref, st_ref):
        b, i = pl.program_id(0), pl.program_id(1)

        @pl.when((b == 0) & (i == 0))
        def _():
            st_ref[...] = jnp.zeros_like(st_ref)

        rows = _iota((tm, 1), 0)
        xv = u_ref[...]
        halo = jnp.where(i == 0, 0.0, h_ref[...])
        p1, p2 = _conv_shifts(xv, halo, rows)
        wv = cw_ref[...]
        uc = wv[2:3, :] * xv + wv[1:2, :] * p1 + wv[0:1, :] * p2 + cb_ref[...]
        gv, vv = uc[:, :f], uc[:, f:]
        sg = _sigmoid(gv)
        sl = gv * sg
        a_ref[...] = (sl * vv).astype(MXU)
        dav = da_ref[...]
        du_ref[:, :f] = dav * vv * (sg * (1.0 + gv * (1.0 - sg)))
        du_ref[:, f:] = dav * sl
        du = du_ref[...]
        st_ref[0:1, :] += _csum(du)
        st_ref[1:2, :] += _csum(du * p2)
        st_ref[2:3, :] += _csum(du * p1)
        st_ref[3:4, :] += _csum(du * xv)

    return pl.pallas_call(
        body, name=name, grid=(nb, s // tm),
        in_specs=[pl.BlockSpec((None, tm, f), lambda b, i: (b, i, 0)),
                  pl.BlockSpec((None, tm, f2), lambda b, i: (b, i, 0)),
                  pl.BlockSpec((None, 8, f2), lambda b, i: (b, jnp.maximum(i * hb - 1, 0), 0)),
                  pl.BlockSpec((3, f2), lambda b, i: (0, 0)),
                  pl.BlockSpec((1, f2), lambda b, i: (0, 0))],
        out_specs=[pl.BlockSpec((None, tm, f2), lambda b, i: (b, i, 0)),
                   pl.BlockSpec((None, tm, f), lambda b, i: (b, i, 0)),
                   pl.BlockSpec((8, f2), lambda b, i: (0, 0))],
        out_shape=[jax.ShapeDtypeStruct((nb, s, f2), F32), jax.ShapeDtypeStruct((nb, s, f), MXU),
                   jax.ShapeDtypeStruct((8, f2), F32)],
        compiler_params=_cp("arbitrary", "arbitrary"))(da, u, u, cw, cb)


def _rot(xv, lane):
    return jnp.where((lane >= 64) & (lane < 80), -pltpu.roll(xv, 112, 1),
                     jnp.where((lane >= 80) & (lane < 96), pltpu.roll(xv, 16, 1), 0.0))


def _rot_t(dv, lane):
    return jnp.where((lane >= 80) & (lane < 96), -pltpu.roll(dv, 16, 1),
                     jnp.where((lane >= 64) & (lane < 80), pltpu.roll(dv, 112, 1), 0.0))


def _mla_prep_specs(s, tm):
    def blk(width, col):
        return pl.BlockSpec((None, tm, width), lambda b, i: (b, i, col // width))

    full = lambda shape: pl.BlockSpec(shape, lambda b, i: (0, 0))
    return [blk(256, COL_CQ), blk(128, COL_CKV), blk(128, COL_KR),
            pl.BlockSpec((None, tm, LANES), lambda b, i: (b, i, 0)),
            pl.BlockSpec((None, tm, LANES), lambda b, i: (b, i, 0)),
            full((1, 256)), full((1, 128)), full((1, 128)), full((1, 128)),
            full((768, 256)), full((768, 128))]


def _mla_prep(proj, cs, sn, gcq, gckv, gqn, gkn, wuq, wukv, name):
    nb, s, _ = proj.shape
    tm = min(256, s)

    def body(cq_ref, ckv_ref, kr_ref, c_ref, s_ref, gcq_ref, gckv_ref, gqn_ref, gkn_ref, wuq_ref, wukv_ref,
             q_ref, k_ref, v_ref):
        lane = _iota((tm, LANES), 1)
        cv, sv = c_ref[...], s_ref[...]
        cq = cq_ref[...]
        cqn = cq * lax.rsqrt(jnp.mean(cq * cq, axis=-1, keepdims=True) + EPS) * gcq_ref[...]
        qb = _dot_nt(cqn.astype(MXU), wuq_ref[...])
        ckv = ckv_ref[...]
        ckvn = ckv * lax.rsqrt(jnp.mean(ckv * ckv, axis=-1, keepdims=True) + EPS) * gckv_ref[...]
        kvb = _dot_nt(ckvn.astype(MXU), wukv_ref[...])
        kr = kr_ref[...]
        for h in range(MLA_HEADS):
            hs = slice(h * LANES, (h + 1) * LANES)
            qh = qb[:, hs]
            qn = qh * lax.rsqrt(_rsum(qh * qh) / MLA_QK + EPS) * gqn_ref[...]
            q_ref[:, hs] = (qn * cv + _rot(qn, lane) * sv).astype(MXU)
            kc = jnp.where(lane < HEAD, kvb[:, hs], kr)
            kn = kc * lax.rsqrt(_rsum(kc * kc) / MLA_QK + EPS) * gkn_ref[...]
            k_ref[:, hs] = (kn * cv + _rot(kn, lane) * sv).astype(MXU)
        for j in range(MLA_HEADS // 2):
            va = kvb[:, (2 * j) * LANES:(2 * j + 1) * LANES]
            vb = kvb[:, (2 * j + 1) * LANES:(2 * j + 2) * LANES]
            v_ref[:, j * LANES:(j + 1) * LANES] = jnp.where(lane < HEAD, pltpu.roll(va, HEAD, 1), vb).astype(MXU)

    return pl.pallas_call(
        body, name=name, grid=(nb, s // tm), in_specs=_mla_prep_specs(s, tm),
        out_specs=[pl.BlockSpec((None, tm, 768), lambda b, i: (b, i, 0)),
                   pl.BlockSpec((None, tm, 768), lambda b, i: (b, i, 0)),
                   pl.BlockSpec((None, tm, 384), lambda b, i: (b, i, 0))],
        out_shape=[jax.ShapeDtypeStruct((nb, s, 768), MXU), jax.ShapeDtypeStruct((nb, s, 768), MXU),
                   jax.ShapeDtypeStruct((nb, s, 384), MXU)],
        compiler_params=_cp("parallel", "parallel"))(proj, proj, proj, cs, sn, gcq, gckv, gqn, gkn, wuq, wukv)


def _mla_prep_bwd(proj, cs, sn, gcq, gckv, gqn, gkn, wuq, wukv, dq, dk, dv, name):
    nb, s, _ = proj.shape
    tm = min(256, s)

    def body(cq_ref, ckv_ref, kr_ref, c_ref, s_ref, gcq_ref, gckv_ref, gqn_ref, gkn_ref, wuq_ref, wukv_ref,
             dq_ref, dk_ref, dv_ref,
             dcq_ref, dckv_ref, dkr_ref, dwuq_ref, dwukv_ref, dgcq_ref, dgckv_ref, dgqn_ref, dgkn_ref,
             dqb_s, dkvb_s):
        @pl.when((pl.program_id(0) == 0) & (pl.program_id(1) == 0))
        def _():
            for r in (dwuq_ref, dwukv_ref, dgcq_ref, dgckv_ref, dgqn_ref, dgkn_ref):
                r[...] = jnp.zeros_like(r)

        lane = _iota((tm, LANES), 1)
        cv, sv = c_ref[...], s_ref[...]
        gqn, gkn = gqn_ref[...], gkn_ref[...]
        cq = cq_ref[...]
        rc = lax.rsqrt(jnp.mean(cq * cq, axis=-1, keepdims=True) + EPS)
        chat = cq * rc
        cqn = (chat * gcq_ref[...]).astype(MXU)
        qb = _dot_nt(cqn, wuq_ref[...])
        ckv = ckv_ref[...]
        rkv = lax.rsqrt(jnp.mean(ckv * ckv, axis=-1, keepdims=True) + EPS)
        kvhat = ckv * rkv
        ckvn = (kvhat * gckv_ref[...]).astype(MXU)
        kvb = _dot_nt(ckvn, wukv_ref[...])
        kr = kr_ref[...]
        dgq = jnp.zeros((1, LANES), F32)
        dgk = jnp.zeros((1, LANES), F32)
        dkr = jnp.zeros((tm, LANES), F32)
        for h in range(MLA_HEADS):
            hs = slice(h * LANES, (h + 1) * LANES)
            qh = qb[:, hs]
            rq = lax.rsqrt(_rsum(qh * qh) / MLA_QK + EPS)
            qhat = qh * rq
            dqr = dq_ref[:, hs]
            dqn = dqr * cv + _rot_t(dqr * sv, lane)
            dgq = dgq + _csum(dqn * qhat)
            dyq = dqn * gqn
            dqb_s[:, hs] = (rq * (dyq - qhat * (_rsum(dyq * qhat) / MLA_QK))).astype(MXU)

            kc = jnp.where(lane < HEAD, kvb[:, hs], kr)
            rk = lax.rsqrt(_rsum(kc * kc) / MLA_QK + EPS)
            khat = kc * rk
            dkr_h = dk_ref[:, hs]
            dkn = dkr_h * cv + _rot_t(dkr_h * sv, lane)
            dgk = dgk + _csum(dkn * khat)
            dyk = dkn * gkn
            dkc = rk * (dyk - khat * (_rsum(dyk * khat) / MLA_QK))
            dkr = dkr + jnp.where(lane >= HEAD, dkc, 0.0)
            dvb = dv_ref[:, (h // 2) * LANES:(h // 2 + 1) * LANES]
            dvp = dvb if h % 2 == 1 else pltpu.roll(dvb, HEAD, 1)
            dkvb_s[:, hs] = jnp.where(lane < HEAD, dkc, dvp).astype(MXU)
        dgqn_ref[...] += dgq
        dgkn_ref[...] += dgk
        dkr_ref[...] = dkr

        dqb = dqb_s[...]
        dwuq_ref[...] += _dot_tn(dqb, cqn)
        dcqn = _dot(dqb, wuq_ref[...])
        dgcq_ref[...] += _csum(dcqn * chat)
        dyc = dcqn * gcq_ref[...]
        dcq_ref[...] = rc * (dyc - chat * jnp.mean(dyc * chat, axis=-1, keepdims=True))

        dkvb = dkvb_s[...]
        dwukv_ref[...] += _dot_tn(dkvb, ckvn)
        dckvn = _dot(dkvb, wukv_ref[...])
        dgckv_ref[...] += _csum(dckvn * kvhat)
        dykv = dckvn * gckv_ref[...]
        dckv_ref[...] = rkv * (dykv - kvhat * jnp.mean(dykv * kvhat, axis=-1, keepdims=True))

    full = lambda shape: pl.BlockSpec(shape, lambda b, i: (0, 0))
    tile = lambda width: pl.BlockSpec((None, tm, width), lambda b, i: (b, i, 0))
    return pl.pallas_call(
        body, name=name, grid=(nb, s // tm),
        in_specs=_mla_prep_specs(s, tm) + [tile(768), tile(768), tile(384)],
        out_specs=[tile(256), tile(128), tile(128), full((768, 256)), full((768, 128)),
                   full((1, 256)), full((1, 128)), full((1, 128)), full((1, 128))],
        out_shape=[jax.ShapeDtypeStruct((nb, s, 256), F32), jax.ShapeDtypeStruct((nb, s, 128), F32),
                   jax.ShapeDtypeStruct((nb, s, 128), F32),
                   jax.ShapeDtypeStruct((768, 256), F32), jax.ShapeDtypeStruct((768, 128), F32),
                   jax.ShapeDtypeStruct((1, 256), F32), jax.ShapeDtypeStruct((1, 128), F32),
                   jax.ShapeDtypeStruct((1, 128), F32), jax.ShapeDtypeStruct((1, 128), F32)],
        scratch_shapes=[pltpu.VMEM((tm, 768), MXU), pltpu.VMEM((tm, 768), MXU)],
        compiler_params=_cp("arbitrary", "arbitrary"))(
            proj, proj, proj, cs, sn, gcq, gckv, gqn, gkn, wuq, wukv, dq, dk, dv)


def _softplus(z):
    return jnp.maximum(z, 0.0) + jnp.log(1.0 + jnp.exp(-jnp.abs(z)))


def _sb_fwd(proj, name):
    nb, s, _ = proj.shape
    tq, tk = min(256, s), 128
    ratio = tq // tk

    def body(q_ref, k_ref, v_ref, o_ref, ct_ref, cnt_ref):
        i = pl.program_id(2)
        lo = _iota((tq, LANES), 1) < HEAD
        qv = q_ref[...]
        q0 = jnp.where(lo, qv, 0.0).astype(MXU)
        q1 = jnp.where(lo, 0.0, qv).astype(MXU)
        usuf = (_iota((tk, tk), 0) > _iota((tk, tk), 1)).astype(MXU)
        tpos = i * tq + _iota((tq, tk), 0)
        scol = _iota((tq, tk), 1)
        nch = (i + 1) * ratio

        def alive(st):
            return (st[0] < nch) & (st[5] > SB_DEAD)

        def step(st):
            t, c0, a0, c1, a1, _ = st
            j = nch - 1 - t
            off = pl.multiple_of(j * tk, tk)
            kc = k_ref[pl.ds(off, tk), :].astype(MXU)
            vc = v_ref[pl.ds(off, tk), :].astype(MXU)
            msk = (scol + j * tk) < tpos

            def head(qm, c, a):
                z = _dot_nt(qm, kc) * SB_SCALE
                sp = _softplus(z)
                lk = jnp.where(msk, -sp, 0.0)
                w = jnp.where(msk, jnp.exp(z - sp + _cumdot(lk, usuf) + c), 0.0)
                return c + _rsum(lk), a + _dot(w.astype(MXU), vc)

            c0, a0 = head(q0, c0, a0)
            c1, a1 = head(q1, c1, a1)
            return t + 1, c0, a0, c1, a1, jnp.maximum(jnp.max(c0), jnp.max(c1))

        z1 = jnp.zeros((tq, 1), F32)
        za = jnp.zeros((tq, LANES), F32)
        t, c0, a0, c1, a1, _ = lax.while_loop(alive, step, (jnp.int32(0), z1, za, z1, za, jnp.float32(0.0)))
        o_ref[...] = jnp.where(lo, a0, a1)
        ct_ref[...] = jnp.where(lo, c0, c1)
        cnt_ref[...] = jnp.zeros((8, LANES), F32) + t.astype(F32)

    kv = lambda col: pl.BlockSpec((None, s, LANES), lambda b, p, i: (b, 0, col // LANES + p))
    tile = pl.BlockSpec((None, tq, LANES), lambda b, p, i: (b, i, p))
    return pl.pallas_call(
        body, name=name, grid=(nb, 2, s // tq),
        in_specs=[pl.BlockSpec((None, tq, LANES), lambda b, p, i: (b, i, COL_SBQ // LANES + p)),
                  kv(COL_SBK), kv(COL_SBV)],
        out_specs=[tile, tile, pl.BlockSpec((None, None, None, 8, LANES), lambda b, p, i: (b, p, i, 0, 0))],
        out_shape=[jax.ShapeDtypeStruct((nb, s, 256), F32)] * 2
        + [jax.ShapeDtypeStruct((nb, 2, s // tq, 8, LANES), F32)],
        compiler_params=_cp("parallel", "parallel", "arbitrary"))(proj, proj, proj)


def _sb_bwd(proj, ct, cnt, do, name):
    nb, s, _ = proj.shape
    tq, tk = min(256, s), 128
    ratio = tq // tk

    def body(q_ref, k_ref, v_ref, ct_ref, cnt_ref, do_ref, dq_ref, dk_ref, dv_ref):
        i = pl.program_id(2)

        @pl.when(i == 0)
        def _():
            dk_ref[...] = jnp.zeros_like(dk_ref)
            dv_ref[...] = jnp.zeros_like(dv_ref)

        lane = _iota((tq, LANES), 1)
        lo = lane < HEAD
        lok = _iota((tk, LANES), 1) < HEAD
        qv, dov = q_ref[...], do_ref[...]
        qb, dob = qv.astype(MXU), dov.astype(MXU)
        q0 = jnp.where(lo, qv, 0.0).astype(MXU)
        q1 = jnp.where(lo, 0.0, qv).astype(MXU)
        do0 = jnp.where(lo, dov, 0.0).astype(MXU)
        do1 = jnp.where(lo, 0.0, dov).astype(MXU)
        ctv = ct_ref[...]
        ct0 = _rsum(jnp.where(lane == 0, ctv, 0.0))
        ct1 = _rsum(jnp.where(lane == LANES - 1, ctv, 0.0))
        uincl = (_iota((tk, tk), 0) <= _iota((tk, tk), 1)).astype(MXU)
        ustrict = (_iota((tk, tk), 0) < _iota((tk, tk), 1)).astype(MXU)
        tpos = i * tq + _iota((tq, tk), 0)
        scol = _iota((tq, tk), 1)
        nch = (i + 1) * ratio

        def step(j, carry):
            p0, g0, dq0, p1, g1, dq1 = carry
            off = pl.multiple_of(j * tk, tk)
            kc = k_ref[pl.ds(off, tk), :].astype(MXU)
            vc = v_ref[pl.ds(off, tk), :].astype(MXU)
            msk = (scol + j * tk) < tpos

            def head(qm, dom, ctot, pc, gc, dqa):
                z = _dot_nt(qm, kc) * SB_SCALE
                sp = _softplus(z)
                lk = jnp.where(msk, -sp, 0.0)
                lsig = z - sp
                w = jnp.where(msk, jnp.exp(lsig + (ctot - pc - _cumdot(lk, uincl))), 0.0)
                g = w * _dot_nt(dom, vc)
                gpre = gc + _cumdot(g, ustrict)
                sig = jnp.exp(lsig)
                dz = (jnp.where(msk, g * (1.0 - sig) - sig * gpre, 0.0) * SB_SCALE).astype(MXU)
                return (pc + _rsum(lk), gc + _rsum(g), dqa + _dot(dz, kc),
                        _dot_tn(dz, qb), _dot_tn(w.astype(MXU), dob))

            p0, g0, dq0, dk0, dv0 = head(q0, do0, ct0, p0, g0, dq0)
            p1, g1, dq1, dk1, dv1 = head(q1, do1, ct1, p1, g1, dq1)
            dk_ref[pl.ds(off, tk), :] += jnp.where(lok, dk0, dk1)
            dv_ref[pl.ds(off, tk), :] += jnp.where(lok, dv0, dv1)
            return p0, g0, dq0, p1, g1, dq1

        z1 = jnp.zeros((tq, 1), F32)
        za = jnp.zeros((tq, LANES), F32)
        first = nch - jnp.max(cnt_ref[...]).astype(jnp.int32)
        _, _, dq0, _, _, dq1 = lax.fori_loop(first, nch, step, (z1, z1, za, z1, z1, za))
        dq_ref[...] = jnp.where(lo, dq0, dq1)

    kv = lambda col: pl.BlockSpec((None, s, LANES), lambda b, p, i: (b, 0, col // LANES + p))
    tile = pl.BlockSpec((None, tq, LANES), lambda b, p, i: (b, i, p))
    acc = pl.BlockSpec((None, s, LANES), lambda b, p, i: (b, 0, p))
    return pl.pallas_call(
        body, name=name, grid=(nb, 2, s // tq),
        in_specs=[pl.BlockSpec((None, tq, LANES), lambda b, p, i: (b, i, COL_SBQ // LANES + p)),
                  kv(COL_SBK), kv(COL_SBV), tile,
                  pl.BlockSpec((None, None, None, 8, LANES), lambda b, p, i: (b, p, i, 0, 0)), tile],
        out_specs=[tile, acc, acc],
        out_shape=[jax.ShapeDtypeStruct((nb, s, 256), F32)] * 3,
        compiler_params=_cp("parallel", "parallel", "arbitrary"))(proj, proj, proj, ct, cnt, do)


def _mla_fwd(q, k, v, name):
    nb, s, _ = q.shape
    tq = tk = min(256, s)

    def body(q_ref, k_ref, v_ref, o_ref, lse_ref):
        i = pl.program_id(2)
        lo = _iota((tq, LANES), 1) < HEAD
        q0, q1 = q_ref[:, :LANES], q_ref[:, LANES:]
        tpos = i * tq + _iota((tq, tk), 0)
        scol = _iota((tq, tk), 1)

        def step(j, carry):
            m0, l0, a0, m1, l1, a1 = carry
            off = pl.multiple_of(j * tk, tk)
            vc = v_ref[pl.ds(off, tk), :]
            msk = (scol + j * tk) <= tpos

            def head(qh, kh, m, l, a):
                sc = jnp.where(msk, _dot_nt(qh, kh) * MLA_SCALE, NEG)
                mn = jnp.maximum(m, jnp.max(sc, axis=-1, keepdims=True))
                al = jnp.exp(m - mn)
                p = jnp.exp(sc - mn)
                return mn, al * l + _rsum(p), al * a + _dot(p.astype(MXU), vc)

            m0, l0, a0 = head(q0, k_ref[pl.ds(off, tk), :LANES], m0, l0, a0)
            m1, l1, a1 = head(q1, k_ref[pl.ds(off, tk), LANES:], m1, l1, a1)
            return m0, l0, a0, m1, l1, a1

        mi = jnp.full((tq, 1), NEG, F32)
        z1 = jnp.zeros((tq, 1), F32)
        za = jnp.zeros((tq, LANES), F32)
        m0, l0, a0, m1, l1, a1 = lax.fori_loop(0, i + 1, step, (mi, z1, za, mi, z1, za))
        o_ref[...] = jnp.where(lo, a0 / l0, a1 / l1)
        lse_ref[...] = jnp.where(lo, m0 + jnp.log(l0), m1 + jnp.log(l1))

    tile = pl.BlockSpec((None, tq, LANES), lambda b, p, i: (b, i, p))
    return pl.pallas_call(
        body, name=name, grid=(nb, MLA_HEADS // 2, s // tq),
        in_specs=[pl.BlockSpec((None, tq, 2 * LANES), lambda b, p, i: (b, i, p)),
                  pl.BlockSpec((None, s, 2 * LANES), lambda b, p, i: (b, 0, p)),
                  pl.BlockSpec((None, s, LANES), lambda b, p, i: (b, 0, p))],
        out_specs=[tile, tile],
        out_shape=[jax.ShapeDtypeStruct((nb, s, 384), F32)] * 2,
        compiler_params=_cp("parallel", "parallel", "arbitrary"))(q, k, v)


def _mla_bwd(q, k, v, o, lse, do, name):
    nb, s, _ = q.shape
    tq = tk = min(256, s)

    def body(q_ref, k_ref, v_ref, o_ref, lse_ref, do_ref, dq_ref, dk_ref, dv_ref):
        i = pl.program_id(2)

        @pl.when(i == 0)
        def _():
            dk_ref[...] = jnp.zeros_like(dk_ref)
            dv_ref[...] = jnp.zeros_like(dv_ref)

        lane = _iota((tq, LANES), 1)
        lo = lane < HEAD
        lok = _iota((tk, LANES), 1) < HEAD
        q0, q1 = q_ref[:, :LANES], q_ref[:, LANES:]
        dov = do_ref[...]
        dob = dov.astype(MXU)
        do0 = jnp.where(lo, dov, 0.0).astype(MXU)
        do1 = jnp.where(lo, 0.0, dov).astype(MXU)
        dd = dov * o_ref[...]
        dl0 = _rsum(jnp.where(lo, dd, 0.0))
        dl1 = _rsum(jnp.where(lo, 0.0, dd))
        lsev = lse_ref[...]
        ls0 = _rsum(jnp.where(lane == 0, lsev, 0.0))
        ls1 = _rsum(jnp.where(lane == LANES - 1, lsev, 0.0))
        tpos = i * tq + _iota((tq, tk), 0)
        scol = _iota((tq, tk), 1)

        def step(j, carry):
            dq0, dq1 = carry
            off = pl.multiple_of(j * tk, tk)
            vc = v_ref[pl.ds(off, tk), :]
            msk = (scol + j * tk) <= tpos

            def head(qh, kh, dom, ls, dl, dqa):
                sc = jnp.where(msk, _dot_nt(qh, kh) * MLA_SCALE, NEG)
                p = jnp.exp(sc - ls)
                ds = (p * (_dot_nt(dom, vc) - dl) * MLA_SCALE).astype(MXU)
                return dqa + _dot(ds, kh), _dot_tn(ds, qh), _dot_tn(p.astype(MXU), dob)

            dq0, dk0, dv0 = head(q0, k_ref[pl.ds(off, tk), :LANES], do0, ls0, dl0, dq0)
            dq1, dk1, dv1 = head(q1, k_ref[pl.ds(off, tk), LANES:], do1, ls1, dl1, dq1)
            dk_ref[pl.ds(off, tk), :LANES] += dk0
            dk_ref[pl.ds(off, tk), LANES:] += dk1
            dv_ref[pl.ds(off, tk), :] += jnp.where(lok, dv0, dv1)
            return dq0, dq1

        za = jnp.zeros((tq, LANES), F32)
        dq0, dq1 = lax.fori_loop(0, i + 1, step, (za, za))
        dq_ref[:, :LANES] = dq0
        dq_ref[:, LANES:] = dq1

    tile = pl.BlockSpec((None, tq, LANES), lambda b, p, i: (b, i, p))
    tile2 = pl.BlockSpec((None, tq, 2 * LANES), lambda b, p, i: (b, i, p))
    return pl.pallas_call(
        body, name=name, grid=(nb, MLA_HEADS // 2, s // tq),
        in_specs=[tile2,
                  pl.BlockSpec((None, s, 2 * LANES), lambda b, p, i: (b, 0, p)),
                  pl.BlockSpec((None, s, LANES), lambda b, p, i: (b, 0, p)),
                  tile, tile, tile],
        out_specs=[tile2,
                   pl.BlockSpec((None, s, 2 * LANES), lambda b, p, i: (b, 0, p)),
                   pl.BlockSpec((None, s, LANES), lambda b, p, i: (b, 0, p))],
        out_shape=[jax.ShapeDtypeStruct((nb, s, 768), F32), jax.ShapeDtypeStruct((nb, s, 768), F32),
                   jax.ShapeDtypeStruct((nb, s, 384), F32)],
        compiler_params=_cp("parallel", "parallel", "arbitrary"))(q, k, v, o, lse, do)


def _half_stats(xv, lo):
    x2 = xv * xv
    s0 = _rsum(jnp.where(lo, x2, 0.0))
    s1 = _rsum(jnp.where(lo, 0.0, x2))
    return jnp.where(lo, lax.rsqrt(s0 / HEAD + EPS), lax.rsqrt(s1 / HEAD + EPS))


def _half_mean(xv, lo):
    s0 = _rsum(jnp.where(lo, xv, 0.0))
    s1 = _rsum(jnp.where(lo, 0.0, xv))
    return jnp.where(lo, s0, s1) / HEAD


def _swa_in_specs():
    def band(col, prev):
        if prev:
            return pl.BlockSpec((None, BLOCK, LANES), lambda b, n: (b, jnp.maximum(n - 1, 0), col // LANES))
        return pl.BlockSpec((None, BLOCK, LANES), lambda b, n: (b, n, col // LANES))

    full = lambda shape: pl.BlockSpec(shape, lambda b, n: tuple(0 for _ in shape))
    return [pl.BlockSpec((None, BLOCK, 384), lambda b, n: (b, n, COL_SWQ // 384)),
            band(COL_SWK, False), band(COL_SWK, True), band(COL_SWV, False), band(COL_SWV, True),
            full((1, LANES)), full((1, LANES)), full((8, LANES)), full((SW_HEADS, BLOCK, 2 * BLOCK))]


def _swa_valid(n):
    a = _iota((BLOCK, 2 * BLOCK), 0)
    bcol = _iota((BLOCK, 2 * BLOCK), 1)
    dist = BLOCK + a - bcol
    return (dist >= 0) & (dist < BLOCK) & ((n > 0) | (bcol >= BLOCK))


def _swa_fwd(proj, gq, gk, sinks, bias, name):
    nb, s, _ = proj.shape

    def body(q_ref, kc_ref, kp_ref, vc_ref, vp_ref, gq_ref, gk_ref, sk_ref, bias_ref, o_ref):
        n = pl.program_id(1)
        lo = _iota((BLOCK, LANES), 1) < HEAD
        lo2 = _iota((2 * BLOCK, LANES), 1) < HEAD
        kband = jnp.concatenate([kp_ref[...], kc_ref[...]], axis=0)
        vband = jnp.concatenate([vp_ref[...], vc_ref[...]], axis=0)
        kn = kband * _half_stats(kband, lo2) * gk_ref[...]
        ks = (kn.astype(MXU), pltpu.roll(kn, HEAD, 1).astype(MXU))
        vs = (vband.astype(MXU), pltpu.roll(vband, HEAD, 1).astype(MXU))
        valid = _swa_valid(n)
        for blk in range(SW_HEADS // 2):
            qv = q_ref[:, blk * LANES:(blk + 1) * LANES]
            qn = qv * _half_stats(qv, lo) * gq_ref[...]
            outs = []
            for half in range(2):
                h = 2 * blk + half
                swap = 0 if half == h // 3 else 1
                qm = jnp.where(lo if half == 0 else ~lo, qn, 0.0).astype(MXU)
                sc = jnp.where(valid, _dot_nt(qm, ks[swap]) * SW_SCALE + bias_ref[h], NEG)
                sk = jnp.max(sk_ref[h:h + 1, :], axis=-1, keepdims=True)
                m = jnp.maximum(jnp.max(sc, axis=-1, keepdims=True), sk)
                p = jnp.exp(sc - m)
                l = _rsum(p) + jnp.exp(sk - m)
                outs.append(_dot((p / l).astype(MXU), vs[swap]))
            o_ref[:, blk * LANES:(blk + 1) * LANES] = jnp.where(lo, outs[0], outs[1])

    return pl.pallas_call(
        body, name=name, grid=(nb, s // BLOCK), in_specs=_swa_in_specs(),
        out_specs=pl.BlockSpec((None, BLOCK, 384), lambda b, n: (b, n, 0)),
        out_shape=jax.ShapeDtypeStruct((nb, s, 384), F32),
        compiler_params=_cp("parallel", "parallel"))(proj, proj, proj, proj, proj, gq, gk, sinks, bias)


def _swa_bwd(proj, gq, gk, sinks, bias, do, name):
    nb, s, _ = proj.shape

    def body(q_ref, kc_ref, kp_ref, vc_ref, vp_ref, gq_ref, gk_ref, sk_ref, bias_ref, do_ref,
             dq_ref, dkc_ref, dkp_ref, dvc_ref, dvp_ref, dbias_ref, dsk_ref, dgq_ref, dgk_ref):
        n = pl.program_id(1)

        @pl.when((pl.program_id(0) == 0) & (n == 0))
        def _():
            for r in (dbias_ref, dsk_ref, dgq_ref, dgk_ref):
                r[...] = jnp.zeros_like(r)

        lo = _iota((BLOCK, LANES), 1) < HEAD
        lo2 = _iota((2 * BLOCK, LANES), 1) < HEAD
        kband = jnp.concatenate([kp_ref[...], kc_ref[...]], axis=0)
        vband = jnp.concatenate([vp_ref[...], vc_ref[...]], axis=0)
        rk = _half_stats(kband, lo2)
        khat = kband * rk
        gkv = gk_ref[...]
        kn = khat * gkv
        ks = (kn.astype(MXU), pltpu.roll(kn, HEAD, 1).astype(MXU))
        vs = (vband.astype(MXU), pltpu.roll(vband, HEAD, 1).astype(MXU))
        valid = _swa_valid(n)
        dkn = jnp.zeros((2 * BLOCK, LANES), F32)
        dvb = jnp.zeros((2 * BLOCK, LANES), F32)
        gqv = gq_ref[...]
        dgq = jnp.zeros((1, LANES), F32)
        for blk in range(SW_HEADS // 2):
            bs = slice(blk * LANES, (blk + 1) * LANES)
            qv = q_ref[:, bs]
            rq = _half_stats(qv, lo)
            qhat = qv * rq
            qn = qhat * gqv
            dov = do_ref[:, bs]
            dqn = jnp.zeros((BLOCK, LANES), F32)
            for half in range(2):
                h = 2 * blk + half
                swap = 0 if half == h // 3 else 1
                hm = lo if half == 0 else ~lo
                qm = jnp.where(hm, qn, 0.0).astype(MXU)
                dom = jnp.where(hm, dov, 0.0).astype(MXU)
                sc = jnp.where(valid, _dot_nt(qm, ks[swap]) * SW_SCALE + bias_ref[h], NEG)
                sk = jnp.max(sk_ref[h:h + 1, :], axis=-1, keepdims=True)
                m = jnp.maximum(jnp.max(sc, axis=-1, keepdims=True), sk)
                e = jnp.exp(sc - m)
                es = jnp.exp(sk - m)
                l = _rsum(e) + es
                p = e / l
                dp = _dot_nt(dom, vs[swap])
                delta = _rsum(p * dp)
                ds = p * (dp - delta)
                dsk_ref[h:h + 1, :] += jnp.broadcast_to(_csum(-(es / l) * delta), (1, LANES))
                dbias_ref[h] += ds
                dsb = (ds * SW_SCALE).astype(MXU)
                dqn = dqn + jnp.where(hm, _dot(dsb, ks[swap]), 0.0)
                rk_ = _dot_tn(dsb, qm)
                rv_ = _dot_tn(p.astype(MXU), dom)
                if swap:
                    rk_ = pltpu.roll(rk_, HEAD, 1)
                    rv_ = pltpu.roll(rv_, HEAD, 1)
                dkn = dkn + rk_
                dvb = dvb + rv_
            dgq = dgq + _csum(dqn * qhat)
            dyq = dqn * gqv
            dq_ref[:, bs] = rq * (dyq - qhat * _half_mean(dyq * qhat, lo))
        dgq_ref[...] += dgq
        dgk_ref[...] += _csum(dkn * khat)
        dyk = dkn * gkv
        dkb = rk * (dyk - khat * _half_mean(dyk * khat, lo2))
        dkp_ref[...] = dkb[:BLOCK]
        dkc_ref[...] = dkb[BLOCK:]
        dvp_ref[...] = dvb[:BLOCK]
        dvc_ref[...] = dvb[BLOCK:]

    full = lambda shape: pl.BlockSpec(shape, lambda b, n: tuple(0 for _ in shape))
    tile = pl.BlockSpec((None, BLOCK, LANES), lambda b, n: (b, n, 0))
    tile3 = pl.BlockSpec((None, BLOCK, 384), lambda b, n: (b, n, 0))
    kvs = jax.ShapeDtypeStruct((nb, s, LANES), F32)
    return pl.pallas_call(
        body, name=name, grid=(nb, s // BLOCK), in_specs=_swa_in_specs() + [tile3],
        out_specs=[tile3, tile, tile, tile, tile, full((SW_HEADS, BLOCK, 2 * BLOCK)), full((8, LANES)),
                   full((1, LANES)), full((1, LANES))],
        out_shape=[jax.ShapeDtypeStruct((nb, s, 384), F32), kvs, kvs, kvs, kvs,
                   jax.ShapeDtypeStruct((SW_HEADS, BLOCK, 2 * BLOCK), F32), jax.ShapeDtypeStruct((8, LANES), F32),
                   jax.ShapeDtypeStruct((1, LANES), F32), jax.ShapeDtypeStruct((1, LANES), F32)],
        compiler_params=_cp("arbitrary", "arbitrary"))(proj, proj, proj, proj, proj, gq, gk, sinks, bias, do)


def _bias_grad(dbias, bucket, name):
    def body(db_ref, bk_ref, o_ref):
        bk = bk_ref[...]
        row = _iota((8, LANES), 0)
        col = _iota((8, LANES), 1)
        res = jnp.zeros((8, LANES), F32)
        for h in range(SW_HEADS):
            dbh = db_ref[h]
            for t in range(REL_BUCKETS):
                val = jnp.sum(jnp.where(bk == t, dbh, 0.0), keepdims=True)
                res = jnp.where((row == h) & (col == t), val, res)
        o_ref[...] = res

    return pl.pallas_call(body, name=name, out_shape=jax.ShapeDtypeStruct((8, LANES), F32))(dbias, bucket)


def _loss_grad(y, target, name):
    nb, s, d = y.shape
    tm = min(512, s)

    def body(y_ref, t_ref, loss_ref, dy_ref):
        @pl.when((pl.program_id(0) == 0) & (pl.program_id(1) == 0))
        def _():
            loss_ref[...] = jnp.zeros_like(loss_ref)

        e = y_ref[...] - t_ref[...]
        dy_ref[...] = e / d
        loss_ref[...] += 0.5 * jnp.sum(_rsum(e * e) / d, keepdims=True)

    tile = pl.BlockSpec((None, tm, d), lambda b, i: (b, i, 0))
    return pl.pallas_call(
        body, name=name, grid=(nb, s // tm), in_specs=[tile, tile],
        out_specs=[pl.BlockSpec((8, LANES), lambda b, i: (0, 0)), tile],
        out_shape=[jax.ShapeDtypeStruct((8, LANES), F32), jax.ShapeDtypeStruct((nb, s, d), F32)],
        compiler_params=_cp("arbitrary", "arbitrary"))(y, target)


def _adamw(parts, w, m, v, name):
    npart, r, ncol = parts.shape
    tr = r
    if r * ncol * 4 > ADAMW_WHOLE_BYTES:
        tr = max(t for t in range(8, r, 8) if r % t == 0 and t * ncol * 4 <= ADAMW_TILE_BYTES)
    bc1 = 1.0 - ADAM_B1 ** ADAM_STEP
    bc2 = 1.0 - ADAM_B2 ** ADAM_STEP

    def body(p_ref, w_ref, m_ref, v_ref, g_ref, d_ref, nm_ref, nv_ref):
        g = p_ref[0]
        for k in range(1, npart):
            g = g + p_ref[k]
        mn = ADAM_B1 * m_ref[...] + (1.0 - ADAM_B1) * g
        vn = ADAM_B2 * v_ref[...] + (1.0 - ADAM_B2) * (g * g)
        g_ref[...] = g
        nm_ref[...] = mn
        nv_ref[...] = vn
        d_ref[...] = -ADAM_LR * ((mn / bc1) / (jnp.sqrt(vn / bc2) + ADAM_EPS) + ADAM_WD * w_ref[...])

    tile = pl.BlockSpec((tr, ncol), lambda i: (i, 0))
    return pl.pallas_call(
        body, name=name, grid=(r // tr,),
        in_specs=[pl.BlockSpec((npart, tr, ncol), lambda i: (0, i, 0)), tile, tile, tile],
        out_specs=[tile] * 4, out_shape=[jax.ShapeDtypeStruct((r, ncol), F32)] * 4,
        compiler_params=_cp("parallel"))(parts, w, m, v)


def _unpack(flat, shapes, lead=()):
    out, off = [], 0
    for shp in shapes:
        size = 1
        for dim in shp:
            size *= dim
        out.append(flat[..., off:off + size].reshape(lead + tuple(shp)))
        off += size
    return out


def _t5_bucket():
    a = jnp.arange(BLOCK)[:, None]
    b = jnp.arange(2 * BLOCK)[None, :]
    dist = BLOCK + a - b
    max_exact = REL_BUCKETS // 2
    nn = jnp.maximum(dist, 0)
    nf = jnp.maximum(nn, 1).astype(F32)
    large = max_exact + (jnp.log(nf / max_exact) / math.log(BLOCK / max_exact)
                         * (REL_BUCKETS - max_exact)).astype(jnp.int32)
    large = jnp.minimum(large, REL_BUCKETS - 1)
    return jnp.where(nn < max_exact, nn, large).astype(jnp.int32)


def _pad_lanes(g, n):
    return jnp.pad(g, (0, n - g.shape[0])).reshape(1, n)


def kernel(x, c, positions, rel_table, norm1_g, norm2_g, w_ada, b_ada, w_in, mla_cq_g, w_uq, mla_ckv_g, w_ukv, mla_qn_g, mla_kn_g, sw_qn_g, sw_kn_g, sw_sinks, w_out, w_up, conv_w, conv_b, w_down, loss_target, m_rel_table, m_norm1_g, m_norm2_g, m_w_ada, m_b_ada, m_w_in, m_mla_cq_g, m_w_uq, m_mla_ckv_g, m_w_ukv, m_mla_qn_g, m_mla_kn_g, m_sw_qn_g, m_sw_kn_g, m_sw_sinks, m_w_out, m_w_up, m_conv_w, m_conv_b, m_w_down, v_rel_table, v_norm1_g, v_norm2_g, v_w_ada, v_b_ada, v_w_in, v_mla_cq_g, v_w_uq, v_mla_ckv_g, v_w_ukv, v_mla_qn_g, v_mla_kn_g, v_sw_qn_g, v_sw_kn_g, v_sw_sinks, v_w_out, v_w_up, v_conv_w, v_conv_b, v_w_down):
    nb, s, d = x.shape
    nl = norm1_g.shape[0]
    me = 4 * lax.axis_index("x") + 2 * lax.axis_index("y") + lax.axis_index("c")
    n_ada = w_ada.shape[2]

    rows2d = lambda a: a.reshape(-1, a.shape[-1])
    tr = lambda a: jnp.swapaxes(a, -1, -2)
    local = [rows2d(tr(w).astype(MXU)) for w in (w_in, w_uq, w_ukv, w_up)]
    local += [rows2d(w.astype(MXU)) for w in (w_out, w_down)] + [rows2d(conv_w), c]
    got = _all_gather(local, "gather_inputs")
    stack_rows = lambda a: a.reshape(N_DEV, nl, -1, a.shape[-1]).transpose(1, 0, 2, 3).reshape(nl, -1, a.shape[-1])
    w_in_t, w_uq_t, w_ukv_t, w_up_t, w_out_f, w_down_f = [stack_rows(a) for a in got[:6]]
    conv_full = got[6].reshape(N_DEV, nl, 3, -1).transpose(1, 2, 0, 3).reshape(nl, 3, -1)
    c_all = got[7].reshape(N_DEV * nb, d)
    zrows = lambda n: jnp.zeros((nl, n, d), MXU)
    w_in_pt = jnp.concatenate([w_in_t[:, :1152], w_in_t[:, 1184:1824], zrows(64), w_in_t[:, 1152:1184], zrows(160)], axis=1)
    w_uq_pt = jnp.pad(w_uq_t.reshape(nl, MLA_HEADS, MLA_QK, 256), ((0, 0), (0, 0), (0, LANES - MLA_QK), (0, 0))).reshape(nl, 768, 256)

    b_my = lax.dynamic_slice_in_dim(b_ada, me * n_ada, n_ada, axis=1).reshape(nl, 1, n_ada)
    mods_my = _ada_fwd(c_all, w_ada, b_my, "ada_fwd")
    mods, = _all_gather([mods_my.reshape(nl * N_DEV * nb, n_ada)], "gather_mods")
    mods = mods.reshape(N_DEV, nl, N_DEV * nb, n_ada).transpose(1, 2, 0, 3).reshape(nl, N_DEV * nb, N_DEV * n_ada)
    mods = lax.dynamic_slice_in_dim(mods, me * nb, nb, axis=1)
    shift1, scale1, gate1, shift2, scale2, gate2 = [mods[:, :, k * d:(k + 1) * d].reshape(nl, nb, 1, d) for k in range(6)]

    half = 16
    inv_freq = jnp.power(ROPE_THETA, -jnp.arange(half, dtype=F32) / half)
    ang = positions.astype(F32)[..., None] * inv_freq
    ones = lambda n: jnp.ones((nb, s, n), F32)
    zeros = lambda n: jnp.zeros((nb, s, n), F32)
    rope_c = jnp.concatenate([ones(64), jnp.cos(ang), jnp.cos(ang), ones(32)], axis=-1)
    rope_s = jnp.concatenate([zeros(64), jnp.sin(ang), jnp.sin(ang), zeros(32)], axis=-1)
    bucket = _t5_bucket()
    bias = jnp.transpose(rel_table[bucket], (2, 0, 1))

    row = lambda g: g.reshape(1, -1)
    twice = lambda g: jnp.concatenate([g, g]).reshape(1, LANES)

    saved = []
    xl = x
    for l in range(nl):
        proj, h1 = _ln_mod_matmul(xl, row(norm1_g[l]), scale1[l], shift1[l], w_in_pt[l], f"l{l}_in_proj")
        prep_args = (proj, rope_c, rope_s, row(mla_cq_g[l]), row(mla_ckv_g[l]), _pad_lanes(mla_qn_g[l], LANES),
                     _pad_lanes(mla_kn_g[l], LANES), w_uq_pt[l], w_ukv_t[l])
        qm, km, vm = _mla_prep(*prep_args, f"l{l}_mla_prep")
        o_a, ct_a, cnt_a = _sb_fwd(proj, f"l{l}_sb_fwd")
        o_b, lse_b = _mla_fwd(qm, km, vm, f"l{l}_mla_fwd")
        sinks = jnp.broadcast_to(jnp.pad(sw_sinks[l], (0, 2))[:, None], (8, LANES))
        swa_args = (proj, twice(sw_qn_g[l]), twice(sw_kn_g[l]), sinks, bias)
        o_c = _swa_fwd(*swa_args, f"l{l}_swa_fwd")
        wo = [w_out_f[l, :256], w_out_f[l, 256:640], w_out_f[l, 640:]]
        x_mid, y1 = _out_proj([o_a, o_b, o_c], wo, gate1[l], xl, f"l{l}_out_proj")
        u_pre, h2 = _ln_mod_matmul(x_mid, row(norm2_g[l]), scale2[l], shift2[l], w_up_t[l], f"l{l}_up_proj")
        x_out, y2 = _conv_gate_matmul(u_pre, conv_full[l], row(conv_b[l]), w_down_f[l], gate2[l], x_mid, f"l{l}_ffn_down")
        saved.append(dict(x=xl, proj=proj, h1=h1, prep=prep_args, qkv=(qm, km, vm), o_a=o_a, ct_a=ct_a, cnt_a=cnt_a, o_b=o_b, lse_b=lse_b,
                          swa=swa_args, o_c=o_c, wo=wo, y1=y1, x_mid=x_mid, u_pre=u_pre, h2=h2, y2=y2))
        xl = x_out

    loss_blk, dx = _loss_grad(xl, loss_target, "loss")
    loss = lax.psum(loss_blk[0, 0], ("x", "y", "c"))

    t = nb * s
    flat = lambda a: a.reshape(t, a.shape[-1])
    grads = [None] * nl
    dmods = [None] * nl
    sharded_out = [None] * nl
    sharded_names = ["w_in", "w_uq", "w_ukv", "w_up", "w_out", "w_down", "conv_w"]
    sharded_wmv = dict(w_in=(w_in, m_w_in, v_w_in), w_uq=(w_uq, m_w_uq, v_w_uq), w_ukv=(w_ukv, m_w_ukv, v_w_ukv),
                       w_up=(w_up, m_w_up, v_w_up), w_out=(w_out, m_w_out, v_w_out), w_down=(w_down, m_w_down, v_w_down),
                       conv_w=(conv_w, m_conv_w, v_conv_w))
    transposed = ("w_in", "w_uq", "w_ukv", "w_up")
    dbias = jnp.zeros((SW_HEADS, BLOCK, 2 * BLOCK), F32)
    for l in reversed(range(nl)):
        sv = saved[l]
        (da,), dy2, dgate2 = _gate_bwd_nt(dx, sv["y2"], gate2[l], [w_down_f[l]], f"l{l}_ffn_down_bwd")
        du, a_act, cstats = _conv_gate_bwd(da, sv["u_pre"], conv_full[l], row(conv_b[l]), f"l{l}_conv_gate_bwd")
        dx_mid, du_pre, dshift2, dscale2, dg2 = _ln_mod_matmul_bwd(
            du, w_up_t[l], sv["x_mid"], row(norm2_g[l]), scale2[l], dx, conv_full[l], f"l{l}_up_proj_bwd")
        g_w_down = _wgrad(flat(a_act), flat(dy2), f"l{l}_w_down_grad")
        g_w_up_t = _wgrad(flat(du_pre), flat(sv["h2"]), f"l{l}_w_up_grad")

        (do_a, do_b, do_c), dy1, dgate1 = _gate_bwd_nt(dx_mid, sv["y1"], gate1[l], sv["wo"], f"l{l}_out_proj_bwd")
        mix = jnp.concatenate([sv["o_a"], sv["o_b"], sv["o_c"]], axis=-1).astype(MXU)
        g_w_out = _wgrad(flat(mix), flat(dy1), f"l{l}_w_out_grad")

        dsb_q, dsb_k, dsb_v = _sb_bwd(sv["proj"], sv["ct_a"], sv["cnt_a"], do_a, f"l{l}_sb_bwd")
        qm, km, vm = sv["qkv"]
        dqm, dkm, dvm = _mla_bwd(qm, km, vm, sv["o_b"], sv["lse_b"], do_b, f"l{l}_mla_bwd")
        dsw_q, dkc, dkp, dvc, dvp, dbias_l, dsinks, dg_swq, dg_swk = _swa_bwd(*sv["swa"], do_c, f"l{l}_swa_bwd")
        dbias = dbias + dbias_l
        shift_up = lambda a: jnp.concatenate([a[:, BLOCK:], jnp.zeros((nb, BLOCK, LANES), F32)], axis=1)
        dsw_k = dkc + shift_up(dkp)
        dsw_v = dvc + shift_up(dvp)
        dcq, dckv, dkr, g_w_uq_pt, g_w_ukv_t, dg_cq, dg_ckv, dg_qn, dg_kn = _mla_prep_bwd(
            *sv["prep"], dqm, dkm, dvm, f"l{l}_mla_prep_bwd")
        dproj = jnp.concatenate([dsb_q, dsb_k, dsb_v, dcq, dckv, dsw_q, dsw_k, dsw_v, dkr, zeros(128)], axis=-1)
        dx, dproj_m, dshift1, dscale1, dg1 = _ln_mod_matmul_bwd(
            dproj, w_in_pt[l], sv["x"], row(norm1_g[l]), scale1[l], dx_mid, None, f"l{l}_in_proj_bwd")
        g_w_in_pt = _wgrad(flat(dproj_m), flat(sv["h1"]), f"l{l}_w_in_grad")

        g_w_in_t = jnp.concatenate([g_w_in_pt[:1152], g_w_in_pt[1856:1888], g_w_in_pt[1152:1792]], axis=0)
        g_w_uq_t = g_w_uq_pt.reshape(MLA_HEADS, LANES, 256)[:, :MLA_QK].reshape(MLA_HEADS * MLA_QK, 256)
        dmods[l] = jnp.concatenate([dshift1, dscale1, dgate1, dshift2, dscale2, dgate2], axis=-1).reshape(nb, 6 * d)

        per_dev = lambda g: g.reshape(N_DEV, -1, g.shape[-1])
        send = [per_dev(g) for g in (g_w_in_t, g_w_uq_t, g_w_ukv_t, g_w_up_t, g_w_out, g_w_down)]
        send.append(cstats[1:4].reshape(3, N_DEV, -1).transpose(1, 0, 2))
        recv = _all_to_all(send, f"l{l}_exchange_grads")
        sharded_out[l] = {}
        for k, parts in zip(sharded_names, recv):
            wmv = [a[l].T if k in transposed else a[l] for a in sharded_wmv[k]]
            res = _adamw(parts, *wmv, f"l{l}_adamw_{k}")
            sharded_out[l][k] = [r.T if k in transposed else r for r in res]
        grads[l] = dict(
            norm1_g=dg1[0], norm2_g=dg2[0], mla_cq_g=dg_cq[0], mla_ckv_g=dg_ckv[0], mla_qn_g=dg_qn[0, :MLA_QK],
            mla_kn_g=dg_kn[0, :MLA_QK], sw_qn_g=dg_swq[0, :HEAD] + dg_swq[0, HEAD:], sw_kn_g=dg_swk[0, :HEAD] + dg_swk[0, HEAD:],
            sw_sinks=dsinks[:SW_HEADS, 0], conv_b=cstats[0])
    grad_x = dx
    g_rel = _bias_grad(dbias, bucket, "rel_table_grad")[:SW_HEADS, :REL_BUCKETS].T
    stack = lambda k: jnp.stack([grads[l][k] for l in range(nl)])

    dm_all, = _all_gather([jnp.stack(dmods).reshape(nl * nb, 6 * d)], "gather_dmods")
    dm_all = dm_all.reshape(N_DEV, nl, nb, 6 * d).transpose(1, 0, 2, 3).reshape(nl, N_DEV * nb, 6 * d)
    dm_my = lax.dynamic_slice_in_dim(dm_all, me * n_ada, n_ada, axis=2)
    g_w_ada, g_b_ada = _ada_bwd(c_all, dm_my, dm_all, "ada_bwd")
    g_b_ada = g_b_ada.reshape(nl, 6 * d)

    big_out = [{k: jnp.stack([sharded_out[l][k][o] for l in range(nl)]) for k in sharded_names} for o in range(4)]
    packf = lambda dct, names, rows: jnp.pad(jnp.concatenate([dct[k].reshape(-1) for k in names]),
                                             (0, rows * LANES - sum(dct[k].size for k in names))).reshape(rows, LANES)

    small_names = ["rel_table", "norm1_g", "norm2_g", "mla_cq_g", "mla_ckv_g", "mla_qn_g", "mla_kn_g",
                   "sw_qn_g", "sw_kn_g", "sw_sinks", "conv_b"]
    small_w = dict(rel_table=rel_table, norm1_g=norm1_g, norm2_g=norm2_g, mla_cq_g=mla_cq_g, mla_ckv_g=mla_ckv_g,
                   mla_qn_g=mla_qn_g, mla_kn_g=mla_kn_g, sw_qn_g=sw_qn_g, sw_kn_g=sw_kn_g, sw_sinks=sw_sinks, conv_b=conv_b)
    small_m = dict(rel_table=m_rel_table, norm1_g=m_norm1_g, norm2_g=m_norm2_g, mla_cq_g=m_mla_cq_g, mla_ckv_g=m_mla_ckv_g,
                   mla_qn_g=m_mla_qn_g, mla_kn_g=m_mla_kn_g, sw_qn_g=m_sw_qn_g, sw_kn_g=m_sw_kn_g, sw_sinks=m_sw_sinks, conv_b=m_conv_b)
    small_v = dict(rel_table=v_rel_table, norm1_g=v_norm1_g, norm2_g=v_norm2_g, mla_cq_g=v_mla_cq_g, mla_ckv_g=v_mla_ckv_g,
                   mla_qn_g=v_mla_qn_g, mla_kn_g=v_mla_kn_g, sw_qn_g=v_sw_qn_g, sw_kn_g=v_sw_kn_g, sw_sinks=v_sw_sinks, conv_b=v_conv_b)
    small_g = {k: (g_rel if k == "rel_table" else stack(k)) for k in small_names}
    n_small = sum(small_w[k].size for k in small_names)
    rows_small = -(-n_small // (8 * LANES)) * 8
    small_parts, = _all_gather([packf(small_g, small_names, rows_small)], "gather_small_grads")
    small_out = _adamw(small_parts, packf(small_w, small_names, rows_small), packf(small_m, small_names, rows_small),
                       packf(small_v, small_names, rows_small), "adamw_replicated")
    small_out = [dict(zip(small_names, _unpack(o.reshape(-1), [small_w[k].shape for k in small_names]))) for o in small_out]

    two_d = lambda a: a.reshape(-1, a.shape[-1])
    res_w = _adamw(two_d(g_w_ada)[None], two_d(w_ada), two_d(m_w_ada), two_d(v_w_ada), "adamw_w_ada")
    res_b = _adamw(g_b_ada[None], b_ada, m_b_ada, v_b_ada, "adamw_b_ada")
    ada_out = [dict(w_ada=rw.reshape(w_ada.shape), b_ada=rb) for rw, rb in zip(res_w, res_b)]

    order = ["rel_table", "norm1_g", "norm2_g", "w_ada", "b_ada", "w_in", "mla_cq_g", "w_uq", "mla_ckv_g", "w_ukv",
             "mla_qn_g", "mla_kn_g", "sw_qn_g", "sw_kn_g", "sw_sinks", "w_out", "w_up", "conv_w", "conv_b", "w_down"]
    outs = [{**big_out[k], **small_out[k], **ada_out[k]} for k in range(4)]
    return (loss, grad_x, *[outs[0][n] for n in order], *[outs[1][n] for n in order],
            *[outs[2][n] for n in order], *[outs[3][n] for n in order])
```

```python
import math

import jax
import jax.numpy as jnp
from jax import lax
from jax.experimental import pallas as pl
from jax.experimental.pallas import tpu as pltpu

F32 = jnp.float32
MXU = jnp.bfloat16
EPS = 1e-6
NEG = -1e30
VMEM_LIMIT_BYTES = 56 * 1024 * 1024
N_DEV = 8
MESH = pl.DeviceIdType.MESH

D_MODEL = 1024
D_FF = 2816
HEAD = 64
LANES = 128
MLA_HEADS = 6
MLA_QK = 96
SW_HEADS = 6
REL_BUCKETS = 32
BLOCK = 128
SB_SCALE = HEAD ** -0.5
SB_DEAD = -105.0
SW_SCALE = HEAD ** -0.5
MLA_SCALE = MLA_QK ** -0.5
ROPE_THETA = 10000.0
D_IN_PAD = 2048
COL_SBQ, COL_SBK, COL_SBV, COL_CQ, COL_CKV, COL_SWQ, COL_SWK, COL_SWV, COL_KR = 0, 256, 512, 768, 1024, 1152, 1536, 1664, 1792

GRAD_ROW_TILE = 208
ADAMW_WHOLE_BYTES = 1 << 20
ADAMW_TILE_BYTES = 1 << 19
ADAM_LR, ADAM_B1, ADAM_B2, ADAM_EPS, ADAM_WD, ADAM_STEP = 0.001, 0.9, 0.999, 1e-08, 0.01, 10


def _cp(*sem):
    return pltpu.CompilerParams(dimension_semantics=sem, vmem_limit_bytes=VMEM_LIMIT_BYTES)


def _iota(shape, dim):
    return lax.broadcasted_iota(jnp.int32, shape, dim)


def _dot(a, b):
    return jnp.dot(a, b, preferred_element_type=F32)


def _dot_nt(a, b):
    return lax.dot_general(a, b, (((1,), (1,)), ((), ())), preferred_element_type=F32)


def _dot_tn(a, b):
    return lax.dot_general(a, b, (((0,), (0,)), ((), ())), preferred_element_type=F32)


def _cumdot(x, u):
    hi = x.astype(MXU)
    r = x - hi.astype(F32)
    mid = r.astype(MXU)
    lo = (r - mid.astype(F32)).astype(MXU)
    return _dot(hi, u) + _dot(mid, u) + _dot(lo, u)


def _sigmoid(x):
    return 1.0 / (1.0 + jnp.exp(-x))


def _rsum(x):
    return jnp.sum(x, axis=-1, keepdims=True)


def _csum(x):
    return jnp.sum(x, axis=0, keepdims=True)


def _all_gather(xs, name):
    na = len(xs)

    def body(*refs):
        x_refs, out_refs = refs[:na], refs[na:2 * na]
        send_sems, recv_sems, local_sems = refs[2 * na:]
        x, y, c = lax.axis_index("x"), lax.axis_index("y"), lax.axis_index("c")
        me, sibling = (x, y, c), (x, y, 1 - c)
        chips = [(1 - x, y), (x, 1 - y), (1 - x, 1 - y)]

        def slot(a, px, py, pc):
            return out_refs[a].at[4 * px + 2 * py + pc]

        def copy(a, k, block, to, src=None):
            return pltpu.make_async_remote_copy(
                src_ref=slot(a, *block) if src is None else src, dst_ref=slot(a, *block),
                send_sem=send_sems.at[7 * a + k], recv_sem=recv_sems.at[7 * a + k], device_id=to, device_id_type=MESH)

        mines, sends = [], []
        for a in range(na):
            mines.append(pltpu.make_async_copy(x_refs[a], slot(a, *me), local_sems.at[a]))
            mines[-1].start()
            first = [copy(a, 0, me, sibling, src=x_refs[a])]
            first += [copy(a, 1 + j, me, (*chip, c), src=x_refs[a]) for j, chip in enumerate(chips)]
            for cp in first:
                cp.start()
            sends += first
        for j, chip in enumerate(chips):
            for a in range(na):
                copy(a, 1 + j, (*chip, c), me).wait_recv()
                sends.append(copy(a, 4 + j, (*chip, c), sibling))
                sends[-1].start()
        for a in range(na):
            copy(a, 0, sibling, me).wait_recv()
            for j, chip in enumerate(chips):
                copy(a, 4 + j, (*chip, 1 - c), me).wait_recv()
        for cp in sends:
            cp.wait_send()
        for mine in mines:
            mine.wait()

    hbm = pl.BlockSpec(memory_space=pl.ANY)
    return pl.pallas_call(
        body, name=name, out_shape=[jax.ShapeDtypeStruct((N_DEV,) + a.shape, a.dtype) for a in xs],
        in_specs=[hbm] * na, out_specs=[hbm] * na,
        scratch_shapes=[pltpu.SemaphoreType.DMA((7 * na,)), pltpu.SemaphoreType.DMA((7 * na,)),
                        pltpu.SemaphoreType.DMA((na,))],
    )(*xs)


def _pair_exchange(xs, name):
    _, r, ncol = xs.shape

    def body(x_ref, out_ref, send_sems, recv_sems):
        x, y, c = lax.axis_index("x"), lax.axis_index("y"), lax.axis_index("c")
        copies = []
        for q in range(4):
            copies.append(pltpu.make_async_remote_copy(
                src_ref=x_ref.at[2 * q + 1 - c], dst_ref=out_ref.at[q],
                send_sem=send_sems.at[q], recv_sem=recv_sems.at[q], device_id=(x, y, 1 - c), device_id_type=MESH))
            copies[-1].start()
        for cp in copies:
            cp.wait()

    hbm = pl.BlockSpec(memory_space=pl.ANY)
    return pl.pallas_call(
        body, name=name, out_shape=jax.ShapeDtypeStruct((4, r, ncol), xs.dtype), in_specs=[hbm], out_specs=hbm,
        scratch_shapes=[pltpu.SemaphoreType.DMA((4,)), pltpu.SemaphoreType.DMA((4,))])(xs)


def _pair_add(core, xs, sib, name):
    _, r, ncol = xs.shape
    tr = GRAD_ROW_TILE

    def body(c_ref, x_ref, s_ref, o_ref):
        o_ref[...] = (x_ref[...] + s_ref[...]).astype(MXU)

    return pl.pallas_call(
        body, name=name,
        grid_spec=pltpu.PrefetchScalarGridSpec(
            num_scalar_prefetch=1, grid=(4, r // tr),
            in_specs=[pl.BlockSpec((None, tr, ncol), lambda q, i, c_ref: (2 * q + c_ref[0], i, 0)),
                      pl.BlockSpec((None, tr, ncol), lambda q, i, c_ref: (q, i, 0))],
            out_specs=pl.BlockSpec((None, tr, ncol), lambda q, i, c_ref: (q, i, 0))),
        out_shape=jax.ShapeDtypeStruct((4, r, ncol), MXU),
        compiler_params=_cp("parallel", "parallel"))(core, xs, sib)


def _chip_exchange(xs, name):
    _, r, ncol = xs.shape
    flips = [(1, 0), (0, 1), (1, 1)]

    def body(x_ref, out_ref, send_sems, recv_sems, local_sem):
        x, y, c = lax.axis_index("x"), lax.axis_index("y"), lax.axis_index("c")
        me = 2 * x + y
        mine = pltpu.make_async_copy(x_ref.at[me], out_ref.at[me], local_sem)
        mine.start()
        copies = []
        for k, (dx, dy) in enumerate(flips):
            px = 1 - x if dx else x
            py = 1 - y if dy else y
            copies.append(pltpu.make_async_remote_copy(
                src_ref=x_ref.at[2 * px + py], dst_ref=out_ref.at[me],
                send_sem=send_sems.at[k], recv_sem=recv_sems.at[k], device_id=(px, py, c), device_id_type=MESH))
            copies[-1].start()
        for cp in copies:
            cp.wait()
        mine.wait()

    hbm = pl.BlockSpec(memory_space=pl.ANY)
    return pl.pallas_call(
        body, name=name, out_shape=jax.ShapeDtypeStruct(xs.shape, xs.dtype), in_specs=[hbm], out_specs=hbm,
        scratch_shapes=[pltpu.SemaphoreType.DMA((3,)), pltpu.SemaphoreType.DMA((3,)), pltpu.SemaphoreType.DMA])(xs)


def _ada_fwd(c_all, w_ada, b_my, name):
    nl, d, n = w_ada.shape
    nb = c_all.shape[0]

    def body(c_ref, w_ref, b_ref, o_ref):
        cv = c_ref[...]
        sc = (cv * _sigmoid(cv)).astype(MXU)
        o_ref[...] = _dot(sc, w_ref[...].astype(MXU)) + b_ref[...]

    return pl.pallas_call(
        body, name=name, grid=(nl,),
        in_specs=[pl.BlockSpec((nb, d), lambda l: (0, 0)),
                  pl.BlockSpec((None, d, n), lambda l: (l, 0, 0)),
                  pl.BlockSpec((None, 1, n), lambda l: (l, 0, 0))],
        out_specs=pl.BlockSpec((None, nb, n), lambda l: (l, 0, 0)),
        out_shape=jax.ShapeDtypeStruct((nl, nb, n), F32),
        compiler_params=_cp("parallel"))(c_all, w_ada, b_my)


def _ada_bwd(c_all, dmods_my, dmods_all, name):
    nl, nb, n = dmods_my.shape
    d = c_all.shape[1]
    nfull = dmods_all.shape[2]

    def body(c_ref, dm_ref, da_ref, dw_ref, db_ref):
        cv = c_ref[...]
        sc = (cv * _sigmoid(cv)).astype(MXU)
        dw_ref[...] = _dot_tn(sc, dm_ref[...].astype(MXU))
        db_ref[...] = _csum(da_ref[...])

    return pl.pallas_call(
        body, name=name, grid=(nl,),
        in_specs=[pl.BlockSpec((nb, d), lambda l: (0, 0)),
                  pl.BlockSpec((None, nb, n), lambda l: (l, 0, 0)),
                  pl.BlockSpec((None, nb, nfull), lambda l: (l, 0, 0))],
        out_specs=[pl.BlockSpec((None, d, n), lambda l: (l, 0, 0)),
                   pl.BlockSpec((None, 1, nfull), lambda l: (l, 0, 0))],
        out_shape=[jax.ShapeDtypeStruct((nl, d, n), F32), jax.ShapeDtypeStruct((nl, 1, nfull), F32)],
        compiler_params=_cp("parallel"))(c_all, dmods_my, dmods_all)


def _ln_mod_matmul(x, g, scale, shift, w, name):
    nb, s, d = x.shape
    n = w.shape[0]
    tm, tn = min(1024, s), 512

    def body(x_ref, g_ref, sc_ref, sh_ref, w_ref, y_ref, h_ref, h_s):
        @pl.when(pl.program_id(2) == 0)
        def _():
            xf = x_ref[...]
            rstd = lax.rsqrt(jnp.mean(xf * xf, axis=-1, keepdims=True) + EPS)
            hv = (xf * rstd * g_ref[...]) * (1.0 + sc_ref[...]) + sh_ref[...]
            h_s[...] = hv.astype(MXU)
            h_ref[...] = h_s[...]

        y_ref[...] = _dot_nt(h_s[...], w_ref[...])

    return pl.pallas_call(
        body, name=name, grid=(nb, s // tm, n // tn),
        in_specs=[pl.BlockSpec((None, tm, d), lambda b, i, j: (b, i, 0)),
                  pl.BlockSpec((1, d), lambda b, i, j: (0, 0)),
                  pl.BlockSpec((None, 1, d), lambda b, i, j: (b, 0, 0)),
                  pl.BlockSpec((None, 1, d), lambda b, i, j: (b, 0, 0)),
                  pl.BlockSpec((tn, d), lambda b, i, j: (j, 0))],
        out_specs=[pl.BlockSpec((None, tm, tn), lambda b, i, j: (b, i, j)),
                   pl.BlockSpec((None, tm, d), lambda b, i, j: (b, i, 0))],
        out_shape=[jax.ShapeDtypeStruct((nb, s, n), F32), jax.ShapeDtypeStruct((nb, s, d), MXU)],
        scratch_shapes=[pltpu.VMEM((tm, d), MXU)],
        compiler_params=_cp("parallel", "parallel", "arbitrary"))(x, g, scale, shift, w)


def _ln_mod_matmul_bwd(dy, w, x, g, scale, dres, conv_w, name):
    nb, s, n = dy.shape
    d = x.shape[-1]
    tm, tn = min(512, s), 512
    ni, nj = s // tm, n // tn
    hb = tm // 8
    conv = conv_w is not None

    def body(*refs):
        if conv:
            dy_ref, nx_ref, cw_ref = refs[:3]
            refs = refs[3:]
        else:
            dy_ref = refs[0]
            refs = refs[1:]
        w_ref, x_ref, g_ref, sc_ref, dr_ref, dx_ref, dyp_ref, dsh_ref, dsc_ref, dg_ref, acc = refs
        b, i, j = pl.program_id(0), pl.program_id(1), pl.program_id(2)

        @pl.when(j == 0)
        def _():
            acc[...] = jnp.zeros_like(acc)

        @pl.when((j == 0) & (i == 0))
        def _():
            dsh_ref[...] = jnp.zeros_like(dsh_ref)
            dsc_ref[...] = jnp.zeros_like(dsc_ref)

        @pl.when((j == 0) & (i == 0) & (b == 0))
        def _():
            dg_ref[...] = jnp.zeros_like(dg_ref)

        dv = dy_ref[...]
        if conv:
            rows = _iota((tm, 1), 0)
            nx = jnp.where(i == ni - 1, 0.0, nx_ref[...])
            n1 = jnp.where(rows == tm - 1, nx[0:1, :], pltpu.roll(dv, tm - 1, 0))
            n2 = jnp.where(rows == tm - 2, nx[0:1, :], jnp.where(rows == tm - 1, nx[1:2, :], pltpu.roll(dv, tm - 2, 0)))
            cw = cw_ref[...]
            dv = cw[2:3, :] * dv + cw[1:2, :] * n1 + cw[0:1, :] * n2
        dp = dv.astype(MXU)
        dyp_ref[...] = dp
        acc[...] += _dot(dp, w_ref[...])

        @pl.when(j == nj - 1)
        def _():
            dh = acc[...]
            xf = x_ref[...]
            rstd = lax.rsqrt(jnp.mean(xf * xf, axis=-1, keepdims=True) + EPS)
            xn = xf * rstd
            gg = g_ref[...]
            sc1 = 1.0 + sc_ref[...]
            dsh_ref[...] += _csum(dh)
            dsc_ref[...] += _csum(dh * xn * gg)
            dg_ref[...] += _csum(dh * xn * sc1)
            dn = dh * gg * sc1
            dx_ref[...] = dr_ref[...] + rstd * (dn - xn * jnp.mean(dn * xn, axis=-1, keepdims=True))

    in_specs = [pl.BlockSpec((None, tm, tn), lambda b, i, j: (b, i, j))]
    args = [dy]
    if conv:
        in_specs += [pl.BlockSpec((None, 8, tn), lambda b, i, j: (b, jnp.minimum((i + 1) * hb, s // 8 - 1), j)),
                     pl.BlockSpec((3, tn), lambda b, i, j: (0, j))]
        args += [dy, conv_w]
    in_specs += [pl.BlockSpec((tn, d), lambda b, i, j: (j, 0)),
                 pl.BlockSpec((None, tm, d), lambda b, i, j: (b, i, 0)),
                 pl.BlockSpec((1, d), lambda b, i, j: (0, 0)),
                 pl.BlockSpec((None, 1, d), lambda b, i, j: (b, 0, 0)),
                 pl.BlockSpec((None, tm, d), lambda b, i, j: (b, i, 0))]
    args += [w, x, g, scale, dres]
    return pl.pallas_call(
        body, name=name, grid=(nb, ni, nj), in_specs=in_specs,
        out_specs=[pl.BlockSpec((None, tm, d), lambda b, i, j: (b, i, 0)),
                   pl.BlockSpec((None, tm, tn), lambda b, i, j: (b, i, j)),
                   pl.BlockSpec((None, 1, d), lambda b, i, j: (b, 0, 0)),
                   pl.BlockSpec((None, 1, d), lambda b, i, j: (b, 0, 0)),
                   pl.BlockSpec((1, d), lambda b, i, j: (0, 0))],
        out_shape=[jax.ShapeDtypeStruct((nb, s, d), F32), jax.ShapeDtypeStruct((nb, s, n), MXU),
                   jax.ShapeDtypeStruct((nb, 1, d), F32), jax.ShapeDtypeStruct((nb, 1, d), F32),
                   jax.ShapeDtypeStruct((1, d), F32)],
        scratch_shapes=[pltpu.VMEM((tm, d), F32)],
        compiler_params=_cp("arbitrary", "arbitrary", "arbitrary"))(*args)


def _wgrad(xm, dym, name):
    t, k = xm.shape
    n = dym.shape[1]
    tk = 1408 if k % 1408 == 0 else 1024
    tt = min(512, t)

    def body(x_ref, dy_ref, o_ref):
        @pl.when(pl.program_id(1) == 0)
        def _():
            o_ref[...] = jnp.zeros_like(o_ref)

        o_ref[...] += _dot_tn(x_ref[...], dy_ref[...])

    return pl.pallas_call(
        body, name=name, grid=(k // tk, t // tt),
        in_specs=[pl.BlockSpec((tt, tk), lambda a, c: (c, a)),
                  pl.BlockSpec((tt, n), lambda a, c: (c, 0))],
        out_specs=pl.BlockSpec((tk, n), lambda a, c: (a, 0)),
        out_shape=jax.ShapeDtypeStruct((k, n), F32),
        compiler_params=_cp("parallel", "arbitrary"))(xm, dym)


def _out_proj(parts, ws, gate, res, name):
    nb, s, d = res.shape
    tm = min(512, s)
    npart = len(parts)

    def body(*refs):
        p_refs, w_refs = refs[:npart], refs[npart:2 * npart]
        gt_ref, res_ref, xo_ref, y_ref = refs[2 * npart:]
        y = _dot(p_refs[0][...].astype(MXU), w_refs[0][...])
        for p_ref, w_ref in zip(p_refs[1:], w_refs[1:]):
            y = y + _dot(p_ref[...].astype(MXU), w_ref[...])
        y_ref[...] = y
        xo_ref[...] = res_ref[...] + gt_ref[...] * y

    in_specs = [pl.BlockSpec((None, tm, p.shape[-1]), lambda b, i: (b, i, 0)) for p in parts]
    in_specs += [pl.BlockSpec(w.shape, lambda b, i: (0, 0)) for w in ws]
    in_specs += [pl.BlockSpec((None, 1, d), lambda b, i: (b, 0, 0)),
                 pl.BlockSpec((None, tm, d), lambda b, i: (b, i, 0))]
    return pl.pallas_call(
        body, name=name, grid=(nb, s // tm), in_specs=in_specs,
        out_specs=[pl.BlockSpec((None, tm, d), lambda b, i: (b, i, 0))] * 2,
        out_shape=[jax.ShapeDtypeStruct((nb, s, d), F32)] * 2,
        compiler_params=_cp("parallel", "parallel"))(*parts, *ws, gate, res)


def _gate_bwd_nt(dx, y, gate, ws, name):
    nb, s, d = dx.shape
    tm = min(256, s)
    npart = len(ws)

    def body(*refs):
        dx_ref, y_ref, gt_ref = refs[:3]
        w_refs = refs[3:3 + npart]
        da_refs = refs[3 + npart:3 + 2 * npart]
        dy_ref, dgt_ref = refs[3 + 2 * npart:]

        @pl.when(pl.program_id(1) == 0)
        def _():
            dgt_ref[...] = jnp.zeros_like(dgt_ref)

        dxv = dx_ref[...]
        dyv = (dxv * gt_ref[...]).astype(MXU)
        dy_ref[...] = dyv
        dgt_ref[...] += _csum(dxv * y_ref[...])
        for w_ref, da_ref in zip(w_refs, da_refs):
            da_ref[...] = _dot_nt(dyv, w_ref[...])

    tile = pl.BlockSpec((None, tm, d), lambda b, i: (b, i, 0))
    row = pl.BlockSpec((None, 1, d), lambda b, i: (b, 0, 0))
    outs = pl.pallas_call(
        body, name=name, grid=(nb, s // tm),
        in_specs=[tile, tile, row] + [pl.BlockSpec(w.shape, lambda b, i: (0, 0)) for w in ws],
        out_specs=[pl.BlockSpec((None, tm, w.shape[0]), lambda b, i: (b, i, 0)) for w in ws] + [tile, row],
        out_shape=[jax.ShapeDtypeStruct((nb, s, w.shape[0]), F32) for w in ws]
        + [jax.ShapeDtypeStruct((nb, s, d), MXU), jax.ShapeDtypeStruct((nb, 1, d), F32)],
        compiler_params=_cp("arbitrary", "arbitrary"))(dx, y, gate, *ws)
    return outs[:npart], outs[npart], outs[npart + 1]


def _conv_shifts(xv, halo, rows):
    p1 = jnp.where(rows == 0, halo[7:8, :], pltpu.roll(xv, 1, 0))
    p2 = jnp.where(rows == 0, halo[6:7, :], jnp.where(rows == 1, halo[7:8, :], pltpu.roll(xv, 2, 0)))
    return p1, p2


def _conv_gate_matmul(u, cw, cb, wd, gate, res, name):
    nb, s, f2 = u.shape
    f = f2 // 2
    d = wd.shape[1]
    tm = min(512, s)
    tk = f // 2
    nk = f // tk
    hb = tm // 8

    def body(ug_ref, uv_ref, hg_ref, hv_ref, cwg_ref, cwv_ref, cbg_ref, cbv_ref, wd_ref, gt_ref, res_ref,
             xo_ref, y_ref, acc):
        i, k = pl.program_id(1), pl.program_id(2)

        @pl.when(k == 0)
        def _():
            acc[...] = jnp.zeros_like(acc)

        rows = _iota((tm, 1), 0)

        def conv(x_ref, h_ref, w_ref, b_ref):
            xv = x_ref[...]
            halo = jnp.where(i == 0, 0.0, h_ref[...])
            p1, p2 = _conv_shifts(xv, halo, rows)
            wv = w_ref[...]
            return wv[2:3, :] * xv + wv[1:2, :] * p1 + wv[0:1, :] * p2 + b_ref[...]

        gv = conv(ug_ref, hg_ref, cwg_ref, cbg_ref)
        vv = conv(uv_ref, hv_ref, cwv_ref, cbv_ref)
        av = gv * _sigmoid(gv) * vv
        acc[...] += _dot(av.astype(MXU), wd_ref[...])

        @pl.when(k == nk - 1)
        def _():
            y = acc[...]
            y_ref[...] = y
            xo_ref[...] = res_ref[...] + gt_ref[...] * y

    def halo_idx(off):
        return lambda b, i, k: (b, jnp.maximum(i * hb - 1, 0), k + off)

    tile = pl.BlockSpec((None, tm, d), lambda b, i, k: (b, i, 0))
    return pl.pallas_call(
        body, name=name, grid=(nb, s // tm, nk),
        in_specs=[pl.BlockSpec((None, tm, tk), lambda b, i, k: (b, i, k)),
                  pl.BlockSpec((None, tm, tk), lambda b, i, k: (b, i, k + nk)),
                  pl.BlockSpec((None, 8, tk), halo_idx(0)),
                  pl.BlockSpec((None, 8, tk), halo_idx(nk)),
                  pl.BlockSpec((3, tk), lambda b, i, k: (0, k)),
                  pl.BlockSpec((3, tk), lambda b, i, k: (0, k + nk)),
                  pl.BlockSpec((1, tk), lambda b, i, k: (0, k)),
                  pl.BlockSpec((1, tk), lambda b, i, k: (0, k + nk)),
                  pl.BlockSpec((tk, d), lambda b, i, k: (k, 0)),
                  pl.BlockSpec((None, 1, d), lambda b, i, k: (b, 0, 0)),
                  tile],
        out_specs=[tile, tile],
        out_shape=[jax.ShapeDtypeStruct((nb, s, d), F32)] * 2,
        scratch_shapes=[pltpu.VMEM((tm, d), F32)],
        compiler_params=_cp("parallel", "parallel", "arbitrary"))(u, u, u, u, cw, cw, cb, cb, wd, gate, res)


def _conv_gate_bwd(da, u, cw, cb, name):
    nb, s, f2 = u.shape
    f = f2 // 2
    tm = min(128, s)
    hb = tm // 8

    def body(da_ref, u_ref, h_ref, cw_ref, cb_ref, du_ref, a_ref, st_ref):
        b, i = pl.program_id(0), pl.program_id(1)

        @pl.when((b == 0) & (i == 0))
        def _():
            st_ref[...] = jnp.zeros_like(st_ref)

        rows = _iota((tm, 1), 0)
        xv = u_ref[...]
        halo = jnp.where(i == 0, 0.0, h_ref[...])
        p1, p2 = _conv_shifts(xv, halo, rows)
        wv = cw_ref[...]
        uc = wv[2:3, :] * xv + wv[1:2, :] * p1 + wv[0:1, :] * p2 + cb_ref[...]
        gv, vv = uc[:, :f], uc[:, f:]
        sg = _sigmoid(gv)
        sl = gv * sg
        a_ref[...] = (sl * vv).astype(MXU)
        dav = da_ref[...]
        du_ref[:, :f] = dav * vv * (sg * (1.0 + gv * (1.0 - sg)))
        du_ref[:, f:] = dav * sl
        du = du_ref[...]
        st_ref[0:1, :] += _csum(du)
        st_ref[1:2, :] += _csum(du * p2)
        st_ref[2:3, :] += _csum(du * p1)
        st_ref[3:4, :] += _csum(du * xv)

    return pl.pallas_call(
        body, name=name, grid=(nb, s // tm),
        in_specs=[pl.BlockSpec((None, tm, f), lambda b, i: (b, i, 0)),
                  pl.BlockSpec((None, tm, f2), lambda b, i: (b, i, 0)),
                  pl.BlockSpec((None, 8, f2), lambda b, i: (b, jnp.maximum(i * hb - 1, 0), 0)),
                  pl.BlockSpec((3, f2), lambda b, i: (0, 0)),
                  pl.BlockSpec((1, f2), lambda b, i: (0, 0))],
        out_specs=[pl.BlockSpec((None, tm, f2), lambda b, i: (b, i, 0)),
                   pl.BlockSpec((None, tm, f), lambda b, i: (b, i, 0)),
                   pl.BlockSpec((8, f2), lambda b, i: (0, 0))],
        out_shape=[jax.ShapeDtypeStruct((nb, s, f2), F32), jax.ShapeDtypeStruct((nb, s, f), MXU),
                   jax.ShapeDtypeStruct((8, f2), F32)],
        compiler_params=_cp("arbitrary", "arbitrary"))(da, u, u, cw, cb)


def _rot(xv, lane):
    return jnp.where((lane >= 64) & (lane < 80), -pltpu.roll(xv, 112, 1),
                     jnp.where((lane >= 80) & (lane < 96), pltpu.roll(xv, 16, 1), 0.0))


def _rot_t(dv, lane):
    return jnp.where((lane >= 80) & (lane < 96), -pltpu.roll(dv, 16, 1),
                     jnp.where((lane >= 64) & (lane < 80), pltpu.roll(dv, 112, 1), 0.0))


def _mla_prep_specs(s, tm):
    def blk(width, col):
        return pl.BlockSpec((None, tm, width), lambda b, i: (b, i, col // width))

    full = lambda shape: pl.BlockSpec(shape, lambda b, i: (0, 0))
    return [blk(256, COL_CQ), blk(128, COL_CKV), blk(128, COL_KR),
            pl.BlockSpec((None, tm, LANES), lambda b, i: (b, i, 0)),
            pl.BlockSpec((None, tm, LANES), lambda b, i: (b, i, 0)),
            full((1, 256)), full((1, 128)), full((1, 128)), full((1, 128)),
            full((768, 256)), full((768, 128))]


def _mla_prep(proj, cs, sn, gcq, gckv, gqn, gkn, wuq, wukv, name):
    nb, s, _ = proj.shape
    tm = min(256, s)

    def body(cq_ref, ckv_ref, kr_ref, c_ref, s_ref, gcq_ref, gckv_ref, gqn_ref, gkn_ref, wuq_ref, wukv_ref,
             q_ref, k_ref, v_ref):
        lane = _iota((tm, LANES), 1)
        cv, sv = c_ref[...], s_ref[...]
        cq = cq_ref[...]
        cqn = cq * lax.rsqrt(jnp.mean(cq * cq, axis=-1, keepdims=True) + EPS) * gcq_ref[...]
        qb = _dot_nt(cqn.astype(MXU), wuq_ref[...])
        ckv = ckv_ref[...]
        ckvn = ckv * lax.rsqrt(jnp.mean(ckv * ckv, axis=-1, keepdims=True) + EPS) * gckv_ref[...]
        kvb = _dot_nt(ckvn.astype(MXU), wukv_ref[...])
        kr = kr_ref[...]
        for h in range(MLA_HEADS):
            hs = slice(h * LANES, (h + 1) * LANES)
            qh = qb[:, hs]
            qn = qh * lax.rsqrt(_rsum(qh * qh) / MLA_QK + EPS) * gqn_ref[...]
            q_ref[:, hs] = (qn * cv + _rot(qn, lane) * sv).astype(MXU)
            kc = jnp.where(lane < HEAD, kvb[:, hs], kr)
            kn = kc * lax.rsqrt(_rsum(kc * kc) / MLA_QK + EPS) * gkn_ref[...]
            k_ref[:, hs] = (kn * cv + _rot(kn, lane) * sv).astype(MXU)
        for j in range(MLA_HEADS // 2):
            va = kvb[:, (2 * j) * LANES:(2 * j + 1) * LANES]
            vb = kvb[:, (2 * j + 1) * LANES:(2 * j + 2) * LANES]
            v_ref[:, j * LANES:(j + 1) * LANES] = jnp.where(lane < HEAD, pltpu.roll(va, HEAD, 1), vb).astype(MXU)

    return pl.pallas_call(
        body, name=name, grid=(nb, s // tm), in_specs=_mla_prep_specs(s, tm),
        out_specs=[pl.BlockSpec((None, tm, 768), lambda b, i: (b, i, 0)),
                   pl.BlockSpec((None, tm, 768), lambda b, i: (b, i, 0)),
                   pl.BlockSpec((None, tm, 384), lambda b, i: (b, i, 0))],
        out_shape=[jax.ShapeDtypeStruct((nb, s, 768), MXU), jax.ShapeDtypeStruct((nb, s, 768), MXU),
                   jax.ShapeDtypeStruct((nb, s, 384), MXU)],
        compiler_params=_cp("parallel", "parallel"))(proj, proj, proj, cs, sn, gcq, gckv, gqn, gkn, wuq, wukv)


def _mla_prep_bwd(proj, cs, sn, gcq, gckv, gqn, gkn, wuq, wukv, dq, dk, dv, name):
    nb, s, _ = proj.shape
    tm = min(256, s)

    def body(cq_ref, ckv_ref, kr_ref, c_ref, s_ref, gcq_ref, gckv_ref, gqn_ref, gkn_ref, wuq_ref, wukv_ref,
             dq_ref, dk_ref, dv_ref,
             dcq_ref, dckv_ref, dkr_ref, dwuq_ref, dwukv_ref, dgcq_ref, dgckv_ref, dgqn_ref, dgkn_ref,
             dqb_s, dkvb_s):
        @pl.when((pl.program_id(0) == 0) & (pl.program_id(1) == 0))
        def _():
            for r in (dwuq_ref, dwukv_ref, dgcq_ref, dgckv_ref, dgqn_ref, dgkn_ref):
                r[...] = jnp.zeros_like(r)

        lane = _iota((tm, LANES), 1)
        cv, sv = c_ref[...], s_ref[...]
        gqn, gkn = gqn_ref[...], gkn_ref[...]
        cq = cq_ref[...]
        rc = lax.rsqrt(jnp.mean(cq * cq, axis=-1, keepdims=True) + EPS)
        chat = cq * rc
        cqn = (chat * gcq_ref[...]).astype(MXU)
        qb = _dot_nt(cqn, wuq_ref[...])
        ckv = ckv_ref[...]
        rkv = lax.rsqrt(jnp.mean(ckv * ckv, axis=-1, keepdims=True) + EPS)
        kvhat = ckv * rkv
        ckvn = (kvhat * gckv_ref[...]).astype(MXU)
        kvb = _dot_nt(ckvn, wukv_ref[...])
        kr = kr_ref[...]
        dgq = jnp.zeros((1, LANES), F32)
        dgk = jnp.zeros((1, LANES), F32)
        dkr = jnp.zeros((tm, LANES), F32)
        for h in range(MLA_HEADS):
            hs = slice(h * LANES, (h + 1) * LANES)
            qh = qb[:, hs]
            rq = lax.rsqrt(_rsum(qh * qh) / MLA_QK + EPS)
            qhat = qh * rq
            dqr = dq_ref[:, hs]
            dqn = dqr * cv + _rot_t(dqr * sv, lane)
            dgq = dgq + _csum(dqn * qhat)
            dyq = dqn * gqn
            dqb_s[:, hs] = (rq * (dyq - qhat * (_rsum(dyq * qhat) / MLA_QK))).astype(MXU)

            kc = jnp.where(lane < HEAD, kvb[:, hs], kr)
            rk = lax.rsqrt(_rsum(kc * kc) / MLA_QK + EPS)
            khat = kc * rk
            dkr_h = dk_ref[:, hs]
            dkn = dkr_h * cv + _rot_t(dkr_h * sv, lane)
            dgk = dgk + _csum(dkn * khat)
            dyk = dkn * gkn
            dkc = rk * (dyk - khat * (_rsum(dyk * khat) / MLA_QK))
            dkr = dkr + jnp.where(lane >= HEAD, dkc, 0.0)
            dvb = dv_ref[:, (h // 2) * LANES:(h // 2 + 1) * LANES]
            dvp = dvb if h % 2 == 1 else pltpu.roll(dvb, HEAD, 1)
            dkvb_s[:, hs] = jnp.where(lane < HEAD, dkc, dvp).astype(MXU)
        dgqn_ref[...] += dgq
        dgkn_ref[...] += dgk
        dkr_ref[...] = dkr

        dqb = dqb_s[...]
        dwuq_ref[...] += _dot_tn(dqb, cqn)
        dcqn = _dot(dqb, wuq_ref[...])
        dgcq_ref[...] += _csum(dcqn * chat)
        dyc = dcqn * gcq_ref[...]
        dcq_ref[...] = rc * (dyc - chat * jnp.mean(dyc * chat, axis=-1, keepdims=True))

        dkvb = dkvb_s[...]
        dwukv_ref[...] += _dot_tn(dkvb, ckvn)
        dckvn = _dot(dkvb, wukv_ref[...])
        dgckv_ref[...] += _csum(dckvn * kvhat)
        dykv = dckvn * gckv_ref[...]
        dckv_ref[...] = rkv * (dykv - kvhat * jnp.mean(dykv * kvhat, axis=-1, keepdims=True))

    full = lambda shape: pl.BlockSpec(shape, lambda b, i: (0, 0))
    tile = lambda width: pl.BlockSpec((None, tm, width), lambda b, i: (b, i, 0))
    return pl.pallas_call(
        body, name=name, grid=(nb, s // tm),
        in_specs=_mla_prep_specs(s, tm) + [tile(768), tile(768), tile(384)],
        out_specs=[tile(256), tile(128), tile(128), full((768, 256)), full((768, 128)),
                   full((1, 256)), full((1, 128)), full((1, 128)), full((1, 128))],
        out_shape=[jax.ShapeDtypeStruct((nb, s, 256), F32), jax.ShapeDtypeStruct((nb, s, 128), F32),
                   jax.ShapeDtypeStruct((nb, s, 128), F32),
                   jax.ShapeDtypeStruct((768, 256), F32), jax.ShapeDtypeStruct((768, 128), F32),
                   jax.ShapeDtypeStruct((1, 256), F32), jax.ShapeDtypeStruct((1, 128), F32),
                   jax.ShapeDtypeStruct((1, 128), F32), jax.ShapeDtypeStruct((1, 128), F32)],
        scratch_shapes=[pltpu.VMEM((tm, 768), MXU), pltpu.VMEM((tm, 768), MXU)],
        compiler_params=_cp("arbitrary", "arbitrary"))(
            proj, proj, proj, cs, sn, gcq, gckv, gqn, gkn, wuq, wukv, dq, dk, dv)


def _softplus(z):
    return jnp.maximum(z, 0.0) + jnp.log(1.0 + jnp.exp(-jnp.abs(z)))


def _sb_fwd(proj, name):
    nb, s, _ = proj.shape
    tq, tk = min(256, s), 128
    ratio = tq // tk

    def body(q_ref, k_ref, v_ref, o_ref, ct_ref, cnt_ref):
        i = pl.program_id(2)
        lo = _iota((tq, LANES), 1) < HEAD
        qv = q_ref[...]
        q0 = jnp.where(lo, qv, 0.0).astype(MXU)
        q1 = jnp.where(lo, 0.0, qv).astype(MXU)
        usuf = (_iota((tk, tk), 0) > _iota((tk, tk), 1)).astype(MXU)
        tpos = i * tq + _iota((tq, tk), 0)
        scol = _iota((tq, tk), 1)
        nch = (i + 1) * ratio

        def alive(st):
            return (st[0] < nch) & (st[5] > SB_DEAD)

        def step(st):
            t, c0, a0, c1, a1, _ = st
            j = nch - 1 - t
            off = pl.multiple_of(j * tk, tk)
            kc = k_ref[pl.ds(off, tk), :].astype(MXU)
            vc = v_ref[pl.ds(off, tk), :].astype(MXU)
            msk = (scol + j * tk) < tpos

            def head(qm, c, a):
                z = _dot_nt(qm, kc) * SB_SCALE
                sp = _softplus(z)
                lk = jnp.where(msk, -sp, 0.0)
                w = jnp.where(msk, jnp.exp(z - sp + _cumdot(lk, usuf) + c), 0.0)
                return c + _rsum(lk), a + _dot(w.astype(MXU), vc)

            c0, a0 = head(q0, c0, a0)
            c1, a1 = head(q1, c1, a1)
            return t + 1, c0, a0, c1, a1, jnp.maximum(jnp.max(c0), jnp.max(c1))

        z1 = jnp.zeros((tq, 1), F32)
        za = jnp.zeros((tq, LANES), F32)
        t, c0, a0, c1, a1, _ = lax.while_loop(alive, step, (jnp.int32(0), z1, za, z1, za, jnp.float32(0.0)))
        o_ref[...] = jnp.where(lo, a0, a1)
        ct_ref[...] = jnp.where(lo, c0, c1)
        cnt_ref[...] = jnp.zeros((8, LANES), F32) + t.astype(F32)

    kv = lambda col: pl.BlockSpec((None, s, LANES), lambda b, p, i: (b, 0, col // LANES + p))
    tile = pl.BlockSpec((None, tq, LANES), lambda b, p, i: (b, i, p))
    return pl.pallas_call(
        body, name=name, grid=(nb, 2, s // tq),
        in_specs=[pl.BlockSpec((None, tq, LANES), lambda b, p, i: (b, i, COL_SBQ // LANES + p)),
                  kv(COL_SBK), kv(COL_SBV)],
        out_specs=[tile, tile, pl.BlockSpec((None, None, None, 8, LANES), lambda b, p, i: (b, p, i, 0, 0))],
        out_shape=[jax.ShapeDtypeStruct((nb, s, 256), F32)] * 2
        + [jax.ShapeDtypeStruct((nb, 2, s // tq, 8, LANES), F32)],
        compiler_params=_cp("parallel", "parallel", "arbitrary"))(proj, proj, proj)


def _sb_bwd(proj, ct, cnt, do, name):
    nb, s, _ = proj.shape
    tq, tk = min(256, s), 128
    ratio = tq // tk

    def body(q_ref, k_ref, v_ref, ct_ref, cnt_ref, do_ref, dq_ref, dk_ref, dv_ref):
        i = pl.program_id(2)

        @pl.when(i == 0)
        def _():
            dk_ref[...] = jnp.zeros_like(dk_ref)
            dv_ref[...] = jnp.zeros_like(dv_ref)

        lane = _iota((tq, LANES), 1)
        lo = lane < HEAD
        lok = _iota((tk, LANES), 1) < HEAD
        qv, dov = q_ref[...], do_ref[...]
        qb, dob = qv.astype(MXU), dov.astype(MXU)
        q0 = jnp.where(lo, qv, 0.0).astype(MXU)
        q1 = jnp.where(lo, 0.0, qv).astype(MXU)
        do0 = jnp.where(lo, dov, 0.0).astype(MXU)
        do1 = jnp.where(lo, 0.0, dov).astype(MXU)
        ctv = ct_ref[...]
        ct0 = _rsum(jnp.where(lane == 0, ctv, 0.0))
        ct1 = _rsum(jnp.where(lane == LANES - 1, ctv, 0.0))
        uincl = (_iota((tk, tk), 0) <= _iota((tk, tk), 1)).astype(MXU)
        ustrict = (_iota((tk, tk), 0) < _iota((tk, tk), 1)).astype(MXU)
        tpos = i * tq + _iota((tq, tk), 0)
        scol = _iota((tq, tk), 1)
        nch = (i + 1) * ratio

        def step(j, carry):
            p0, g0, dq0, p1, g1, dq1 = carry
            off = pl.multiple_of(j * tk, tk)
            kc = k_ref[pl.ds(off, tk), :].astype(MXU)
            vc = v_ref[pl.ds(off, tk), :].astype(MXU)
            msk = (scol + j * tk) < tpos

            def head(qm, dom, ctot, pc, gc, dqa):
                z = _dot_nt(qm, kc) * SB_SCALE
                sp = _softplus(z)
                lk = jnp.where(msk, -sp, 0.0)
                lsig = z - sp
                w = jnp.where(msk, jnp.exp(lsig + (ctot - pc - _cumdot(lk, uincl))), 0.0)
                g = w * _dot_nt(dom, vc)
                gpre = gc + _cumdot(g, ustrict)
                sig = jnp.exp(lsig)
                dz = (jnp.where(msk, g * (1.0 - sig) - sig * gpre, 0.0) * SB_SCALE).astype(MXU)
                return (pc + _rsum(lk), gc + _rsum(g), dqa + _dot(dz, kc),
                        _dot_tn(dz, qb), _dot_tn(w.astype(MXU), dob))

            p0, g0, dq0, dk0, dv0 = head(q0, do0, ct0, p0, g0, dq0)
            p1, g1, dq1, dk1, dv1 = head(q1, do1, ct1, p1, g1, dq1)
            dk_ref[pl.ds(off, tk), :] += jnp.where(lok, dk0, dk1)
            dv_ref[pl.ds(off, tk), :] += jnp.where(lok, dv0, dv1)
            return p0, g0, dq0, p1, g1, dq1

        z1 = jnp.zeros((tq, 1), F32)
        za = jnp.zeros((tq, LANES), F32)
        first = nch - jnp.max(cnt_ref[...]).astype(jnp.int32)
        _, _, dq0, _, _, dq1 = lax.fori_loop(first, nch, step, (z1, z1, za, z1, z1, za))
        dq_ref[...] = jnp.where(lo, dq0, dq1)

    kv = lambda col: pl.BlockSpec((None, s, LANES), lambda b, p, i: (b, 0, col // LANES + p))
    tile = pl.BlockSpec((None, tq, LANES), lambda b, p, i: (b, i, p))
    acc = pl.BlockSpec((None, s, LANES), lambda b, p, i: (b, 0, p))
    return pl.pallas_call(
        body, name=name, grid=(nb, 2, s // tq),
        in_specs=[pl.BlockSpec((None, tq, LANES), lambda b, p, i: (b, i, COL_SBQ // LANES + p)),
                  kv(COL_SBK), kv(COL_SBV), tile,
                  pl.BlockSpec((None, None, None, 8, LANES), lambda b, p, i: (b, p, i, 0, 0)), tile],
        out_specs=[tile, acc, acc],
        out_shape=[jax.ShapeDtypeStruct((nb, s, 256), F32)] * 3,
        compiler_params=_cp("parallel", "parallel", "arbitrary"))(proj, proj, proj, ct, cnt, do)


def _mla_fwd(q, k, v, name):
    nb, s, _ = q.shape
    tq = tk = min(256, s)

    def body(q_ref, k_ref, v_ref, o_ref, lse_ref):
        i = pl.program_id(2)
        lo = _iota((tq, LANES), 1) < HEAD
        q0, q1 = q_ref[:, :LANES], q_ref[:, LANES:]
        tpos = i * tq + _iota((tq, tk), 0)
        scol = _iota((tq, tk), 1)

        def step(j, carry):
            m0, l0, a0, m1, l1, a1 = carry
            off = pl.multiple_of(j * tk, tk)
            vc = v_ref[pl.ds(off, tk), :]
            msk = (scol + j * tk) <= tpos

            def head(qh, kh, m, l, a):
                sc = jnp.where(msk, _dot_nt(qh, kh) * MLA_SCALE, NEG)
                mn = jnp.maximum(m, jnp.max(sc, axis=-1, keepdims=True))
                al = jnp.exp(m - mn)
                p = jnp.exp(sc - mn)
                return mn, al * l + _rsum(p), al * a + _dot(p.astype(MXU), vc)

            m0, l0, a0 = head(q0, k_ref[pl.ds(off, tk), :LANES], m0, l0, a0)
            m1, l1, a1 = head(q1, k_ref[pl.ds(off, tk), LANES:], m1, l1, a1)
            return m0, l0, a0, m1, l1, a1

        mi = jnp.full((tq, 1), NEG, F32)
        z1 = jnp.zeros((tq, 1), F32)
        za = jnp.zeros((tq, LANES), F32)
        m0, l0, a0, m1, l1, a1 = lax.fori_loop(0, i + 1, step, (mi, z1, za, mi, z1, za))
        o_ref[...] = jnp.where(lo, a0 / l0, a1 / l1)
        lse_ref[...] = jnp.where(lo, m0 + jnp.log(l0), m1 + jnp.log(l1))

    tile = pl.BlockSpec((None, tq, LANES), lambda b, p, i: (b, i, p))
    return pl.pallas_call(
        body, name=name, grid=(nb, MLA_HEADS // 2, s // tq),
        in_specs=[pl.BlockSpec((None, tq, 2 * LANES), lambda b, p, i: (b, i, p)),
                  pl.BlockSpec((None, s, 2 * LANES), lambda b, p, i: (b, 0, p)),
                  pl.BlockSpec((None, s, LANES), lambda b, p, i: (b, 0, p))],
        out_specs=[tile, tile],
        out_shape=[jax.ShapeDtypeStruct((nb, s, 384), F32)] * 2,
        compiler_params=_cp("parallel", "parallel", "arbitrary"))(q, k, v)


def _mla_bwd(q, k, v, o, lse, do, name):
    nb, s, _ = q.shape
    tq = tk = min(256, s)

    def body(q_ref, k_ref, v_ref, o_ref, lse_ref, do_ref, dq_ref, dk_ref, dv_ref):
        i = pl.program_id(2)

        @pl.when(i == 0)
        def _():
            dk_ref[...] = jnp.zeros_like(dk_ref)
            dv_ref[...] = jnp.zeros_like(dv_ref)

        lane = _iota((tq, LANES), 1)
        lo = lane < HEAD
        lok = _iota((tk, LANES), 1) < HEAD
        q0, q1 = q_ref[:, :LANES], q_ref[:, LANES:]
        dov = do_ref[...]
        dob = dov.astype(MXU)
        do0 = jnp.where(lo, dov, 0.0).astype(MXU)
        do1 = jnp.where(lo, 0.0, dov).astype(MXU)
        dd = dov * o_ref[...]
        dl0 = _rsum(jnp.where(lo, dd, 0.0))
        dl1 = _rsum(jnp.where(lo, 0.0, dd))
        lsev = lse_ref[...]
        ls0 = _rsum(jnp.where(lane == 0, lsev, 0.0))
        ls1 = _rsum(jnp.where(lane == LANES - 1, lsev, 0.0))
        tpos = i * tq + _iota((tq, tk), 0)
        scol = _iota((tq, tk), 1)

        def step(j, carry):
            dq0, dq1 = carry
            off = pl.multiple_of(j * tk, tk)
            vc = v_ref[pl.ds(off, tk), :]
            msk = (scol + j * tk) <= tpos

            def head(qh, kh, dom, ls, dl, dqa):
                sc = jnp.where(msk, _dot_nt(qh, kh) * MLA_SCALE, NEG)
                p = jnp.exp(sc - ls)
                ds = (p * (_dot_nt(dom, vc) - dl) * MLA_SCALE).astype(MXU)
                return dqa + _dot(ds, kh), _dot_tn(ds, qh), _dot_tn(p.astype(MXU), dob)

            dq0, dk0, dv0 = head(q0, k_ref[pl.ds(off, tk), :LANES], do0, ls0, dl0, dq0)
            dq1, dk1, dv1 = head(q1, k_ref[pl.ds(off, tk), LANES:], do1, ls1, dl1, dq1)
            dk_ref[pl.ds(off, tk), :LANES] += dk0
            dk_ref[pl.ds(off, tk), LANES:] += dk1
            dv_ref[pl.ds(off, tk), :] += jnp.where(lok, dv0, dv1)
            return dq0, dq1

        za = jnp.zeros((tq, LANES), F32)
        dq0, dq1 = lax.fori_loop(0, i + 1, step, (za, za))
        dq_ref[:, :LANES] = dq0
        dq_ref[:, LANES:] = dq1

    tile = pl.BlockSpec((None, tq, LANES), lambda b, p, i: (b, i, p))
    tile2 = pl.BlockSpec((None, tq, 2 * LANES), lambda b, p, i: (b, i, p))
    return pl.pallas_call(
        body, name=name, grid=(nb, MLA_HEADS // 2, s // tq),
        in_specs=[tile2,
                  pl.BlockSpec((None, s, 2 * LANES), lambda b, p, i: (b, 0, p)),
                  pl.BlockSpec((None, s, LANES), lambda b, p, i: (b, 0, p)),
                  tile, tile, tile],
        out_specs=[tile2,
                   pl.BlockSpec((None, s, 2 * LANES), lambda b, p, i: (b, 0, p)),
                   pl.BlockSpec((None, s, LANES), lambda b, p, i: (b, 0, p))],
        out_shape=[jax.ShapeDtypeStruct((nb, s, 768), F32), jax.ShapeDtypeStruct((nb, s, 768), F32),
                   jax.ShapeDtypeStruct((nb, s, 384), F32)],
        compiler_params=_cp("parallel", "parallel", "arbitrary"))(q, k, v, o, lse, do)


def _half_stats(xv, lo):
    x2 = xv * xv
    s0 = _rsum(jnp.where(lo, x2, 0.0))
    s1 = _rsum(jnp.where(lo, 0.0, x2))
    return jnp.where(lo, lax.rsqrt(s0 / HEAD + EPS), lax.rsqrt(s1 / HEAD + EPS))


def _half_mean(xv, lo):
    s0 = _rsum(jnp.where(lo, xv, 0.0))
    s1 = _rsum(jnp.where(lo, 0.0, xv))
    return jnp.where(lo, s0, s1) / HEAD


def _swa_in_specs():
    def band(col, prev):
        if prev:
            return pl.BlockSpec((None, BLOCK, LANES), lambda b, n: (b, jnp.maximum(n - 1, 0), col // LANES))
        return pl.BlockSpec((None, BLOCK, LANES), lambda b, n: (b, n, col // LANES))

    full = lambda shape: pl.BlockSpec(shape, lambda b, n: tuple(0 for _ in shape))
    return [pl.BlockSpec((None, BLOCK, 384), lambda b, n: (b, n, COL_SWQ // 384)),
            band(COL_SWK, False), band(COL_SWK, True), band(COL_SWV, False), band(COL_SWV, True),
            full((1, LANES)), full((1, LANES)), full((8, LANES)), full((SW_HEADS, BLOCK, 2 * BLOCK))]


def _swa_valid(n):
    a = _iota((BLOCK, 2 * BLOCK), 0)
    bcol = _iota((BLOCK, 2 * BLOCK), 1)
    dist = BLOCK + a - bcol
    return (dist >= 0) & (dist < BLOCK) & ((n > 0) | (bcol >= BLOCK))


def _swa_fwd(proj, gq, gk, sinks, bias, name):
    nb, s, _ = proj.shape

    def body(q_ref, kc_ref, kp_ref, vc_ref, vp_ref, gq_ref, gk_ref, sk_ref, bias_ref, o_ref):
        n = pl.program_id(1)
        lo = _iota((BLOCK, LANES), 1) < HEAD
        lo2 = _iota((2 * BLOCK, LANES), 1) < HEAD
        kband = jnp.concatenate([kp_ref[...], kc_ref[...]], axis=0)
        vband = jnp.concatenate([vp_ref[...], vc_ref[...]], axis=0)
        kn = kband * _half_stats(kband, lo2) * gk_ref[...]
        ks = (kn.astype(MXU), pltpu.roll(kn, HEAD, 1).astype(MXU))
        vs = (vband.astype(MXU), pltpu.roll(vband, HEAD, 1).astype(MXU))
        valid = _swa_valid(n)
        for blk in range(SW_HEADS // 2):
            qv = q_ref[:, blk * LANES:(blk + 1) * LANES]
            qn = qv * _half_stats(qv, lo) * gq_ref[...]
            outs = []
            for half in range(2):
                h = 2 * blk + half
                swap = 0 if half == h // 3 else 1
                qm = jnp.where(lo if half == 0 else ~lo, qn, 0.0).astype(MXU)
                sc = jnp.where(valid, _dot_nt(qm, ks[swap]) * SW_SCALE + bias_ref[h], NEG)
                sk = jnp.max(sk_ref[h:h + 1, :], axis=-1, keepdims=True)
                m = jnp.maximum(jnp.max(sc, axis=-1, keepdims=True), sk)
                p = jnp.exp(sc - m)
                l = _rsum(p) + jnp.exp(sk - m)
                outs.append(_dot((p / l).astype(MXU), vs[swap]))
            o_ref[:, blk * LANES:(blk + 1) * LANES] = jnp.where(lo, outs[0], outs[1])

    return pl.pallas_call(
        body, name=name, grid=(nb, s // BLOCK), in_specs=_swa_in_specs(),
        out_specs=pl.BlockSpec((None, BLOCK, 384), lambda b, n: (b, n, 0)),
        out_shape=jax.ShapeDtypeStruct((nb, s, 384), F32),
        compiler_params=_cp("parallel", "parallel"))(proj, proj, proj, proj, proj, gq, gk, sinks, bias)


def _swa_bwd(proj, gq, gk, sinks, bias, do, name):
    nb, s, _ = proj.shape

    def body(q_ref, kc_ref, kp_ref, vc_ref, vp_ref, gq_ref, gk_ref, sk_ref, bias_ref, do_ref,
             dq_ref, dkc_ref, dkp_ref, dvc_ref, dvp_ref, dbias_ref, dsk_ref, dgq_ref, dgk_ref):
        n = pl.program_id(1)

        @pl.when((pl.program_id(0) == 0) & (n == 0))
        def _():
            for r in (dbias_ref, dsk_ref, dgq_ref, dgk_ref):
                r[...] = jnp.zeros_like(r)

        lo = _iota((BLOCK, LANES), 1) < HEAD
        lo2 = _iota((2 * BLOCK, LANES), 1) < HEAD
        kband = jnp.concatenate([kp_ref[...], kc_ref[...]], axis=0)
        vband = jnp.concatenate([vp_ref[...], vc_ref[...]], axis=0)
        rk = _half_stats(kband, lo2)
        khat = kband * rk
        gkv = gk_ref[...]
        kn = khat * gkv
        ks = (kn.astype(MXU), pltpu.roll(kn, HEAD, 1).astype(MXU))
        vs = (vband.astype(MXU), pltpu.roll(vband, HEAD, 1).astype(MXU))
        valid = _swa_valid(n)
        dkn = jnp.zeros((2 * BLOCK, LANES), F32)
        dvb = jnp.zeros((2 * BLOCK, LANES), F32)
        gqv = gq_ref[...]
        dgq = jnp.zeros((1, LANES), F32)
        for blk in range(SW_HEADS // 2):
            bs = slice(blk * LANES, (blk + 1) * LANES)
            qv = q_ref[:, bs]
            rq = _half_stats(qv, lo)
            qhat = qv * rq
            qn = qhat * gqv
            dov = do_ref[:, bs]
            dqn = jnp.zeros((BLOCK, LANES), F32)
            for half in range(2):
                h = 2 * blk + half
                swap = 0 if half == h // 3 else 1
                hm = lo if half == 0 else ~lo
                qm = jnp.where(hm, qn, 0.0).astype(MXU)
                dom = jnp.where(hm, dov, 0.0).astype(MXU)
                sc = jnp.where(valid, _dot_nt(qm, ks[swap]) * SW_SCALE + bias_ref[h], NEG)
                sk = jnp.max(sk_ref[h:h + 1, :], axis=-1, keepdims=True)
                m = jnp.maximum(jnp.max(sc, axis=-1, keepdims=True), sk)
                e = jnp.exp(sc - m)
                es = jnp.exp(sk - m)
                l = _rsum(e) + es
                p = e / l
                dp = _dot_nt(dom, vs[swap])
                delta = _rsum(p * dp)
                ds = p * (dp - delta)
                dsk_ref[h:h + 1, :] += jnp.broadcast_to(_csum(-(es / l) * delta), (1, LANES))
                dbias_ref[h] += ds
                dsb = (ds * SW_SCALE).astype(MXU)
                dqn = dqn + jnp.where(hm, _dot(dsb, ks[swap]), 0.0)
                rk_ = _dot_tn(dsb, qm)
                rv_ = _dot_tn(p.astype(MXU), dom)
                if swap:
                    rk_ = pltpu.roll(rk_, HEAD, 1)
                    rv_ = pltpu.roll(rv_, HEAD, 1)
                dkn = dkn + rk_
                dvb = dvb + rv_
            dgq = dgq + _csum(dqn * qhat)
            dyq = dqn * gqv
            dq_ref[:, bs] = rq * (dyq - qhat * _half_mean(dyq * qhat, lo))
        dgq_ref[...] += dgq
        dgk_ref[...] += _csum(dkn * khat)
        dyk = dkn * gkv
        dkb = rk * (dyk - khat * _half_mean(dyk * khat, lo2))
        dkp_ref[...] = dkb[:BLOCK]
        dkc_ref[...] = dkb[BLOCK:]
        dvp_ref[...] = dvb[:BLOCK]
        dvc_ref[...] = dvb[BLOCK:]

    full = lambda shape: pl.BlockSpec(shape, lambda b, n: tuple(0 for _ in shape))
    tile = pl.BlockSpec((None, BLOCK, LANES), lambda b, n: (b, n, 0))
    tile3 = pl.BlockSpec((None, BLOCK, 384), lambda b, n: (b, n, 0))
    kvs = jax.ShapeDtypeStruct((nb, s, LANES), F32)
    return pl.pallas_call(
        body, name=name, grid=(nb, s // BLOCK), in_specs=_swa_in_specs() + [tile3],
        out_specs=[tile3, tile, tile, tile, tile, full((SW_HEADS, BLOCK, 2 * BLOCK)), full((8, LANES)),
                   full((1, LANES)), full((1, LANES))],
        out_shape=[jax.ShapeDtypeStruct((nb, s, 384), F32), kvs, kvs, kvs, kvs,
                   jax.ShapeDtypeStruct((SW_HEADS, BLOCK, 2 * BLOCK), F32), jax.ShapeDtypeStruct((8, LANES), F32),
                   jax.ShapeDtypeStruct((1, LANES), F32), jax.ShapeDtypeStruct((1, LANES), F32)],
        compiler_params=_cp("arbitrary", "arbitrary"))(proj, proj, proj, proj, proj, gq, gk, sinks, bias, do)


def _bias_build(table, bucket, name):
    def body(tb_ref, bk_ref, o_ref):
        bk = bk_ref[...]
        tb = tb_ref[...]
        row = _iota((8, LANES), 0)
        col = _iota((8, LANES), 1)
        for h in range(SW_HEADS):
            acc = jnp.zeros((BLOCK, 2 * BLOCK), F32)
            for t in range(REL_BUCKETS):
                val = jnp.sum(jnp.where((row == h) & (col == t), tb, 0.0), keepdims=True)
                acc = jnp.where(bk == t, val, acc)
            o_ref[h] = acc

    return pl.pallas_call(
        body, name=name, out_shape=jax.ShapeDtypeStruct((SW_HEADS, BLOCK, 2 * BLOCK), F32))(table, bucket)


def _bias_grad(dbias, bucket, name):
    def body(db_ref, bk_ref, o_ref):
        bk = bk_ref[...]
        row = _iota((8, LANES), 0)
        col = _iota((8, LANES), 1)
        res = jnp.zeros((8, LANES), F32)
        for h in range(SW_HEADS):
            dbh = db_ref[h]
            for t in range(REL_BUCKETS):
                val = jnp.sum(jnp.where(bk == t, dbh, 0.0), keepdims=True)
                res = jnp.where((row == h) & (col == t), val, res)
        o_ref[...] = res

    return pl.pallas_call(body, name=name, out_shape=jax.ShapeDtypeStruct((8, LANES), F32))(dbias, bucket)


def _loss_grad(y, target, name):
    nb, s, d = y.shape
    tm = min(512, s)

    def body(y_ref, t_ref, loss_ref, dy_ref):
        @pl.when((pl.program_id(0) == 0) & (pl.program_id(1) == 0))
        def _():
            loss_ref[...] = jnp.zeros_like(loss_ref)

        e = y_ref[...] - t_ref[...]
        dy_ref[...] = e / d
        loss_ref[...] += 0.5 * jnp.sum(_rsum(e * e) / d, keepdims=True)

    tile = pl.BlockSpec((None, tm, d), lambda b, i: (b, i, 0))
    return pl.pallas_call(
        body, name=name, grid=(nb, s // tm), in_specs=[tile, tile],
        out_specs=[pl.BlockSpec((8, LANES), lambda b, i: (0, 0)), tile],
        out_shape=[jax.ShapeDtypeStruct((8, LANES), F32), jax.ShapeDtypeStruct((nb, s, d), F32)],
        compiler_params=_cp("arbitrary", "arbitrary"))(y, target)


def _adamw(parts, w, m, v, name):
    npart, r, ncol = parts.shape
    tr = r
    if r * ncol * 4 > ADAMW_WHOLE_BYTES:
        tr = max(t for t in range(8, r, 8) if r % t == 0 and t * ncol * 4 <= ADAMW_TILE_BYTES)
    bc1 = 1.0 - ADAM_B1 ** ADAM_STEP
    bc2 = 1.0 - ADAM_B2 ** ADAM_STEP

    def body(p_ref, w_ref, m_ref, v_ref, g_ref, d_ref, nm_ref, nv_ref):
        g = p_ref[0].astype(F32)
        for k in range(1, npart):
            g = g + p_ref[k].astype(F32)
        mn = ADAM_B1 * m_ref[...] + (1.0 - ADAM_B1) * g
        vn = ADAM_B2 * v_ref[...] + (1.0 - ADAM_B2) * (g * g)
        g_ref[...] = g
        nm_ref[...] = mn
        nv_ref[...] = vn
        d_ref[...] = -ADAM_LR * ((mn / bc1) / (jnp.sqrt(vn / bc2) + ADAM_EPS) + ADAM_WD * w_ref[...])

    tile = pl.BlockSpec((tr, ncol), lambda i: (i, 0))
    return pl.pallas_call(
        body, name=name, grid=(r // tr,),
        in_specs=[pl.BlockSpec((npart, tr, ncol), lambda i: (0, i, 0)), tile, tile, tile],
        out_specs=[tile] * 4, out_shape=[jax.ShapeDtypeStruct((r, ncol), F32)] * 4,
        compiler_params=_cp("parallel"))(parts, w, m, v)


def _unpack(flat, shapes, lead=()):
    out, off = [], 0
    for shp in shapes:
        size = 1
        for dim in shp:
            size *= dim
        out.append(flat[..., off:off + size].reshape(lead + tuple(shp)))
        off += size
    return out


def _t5_bucket():
    a = jnp.arange(BLOCK)[:, None]
    b = jnp.arange(2 * BLOCK)[None, :]
    dist = BLOCK + a - b
    max_exact = REL_BUCKETS // 2
    nn = jnp.maximum(dist, 0)
    nf = jnp.maximum(nn, 1).astype(F32)
    large = max_exact + (jnp.log(nf / max_exact) / math.log(BLOCK / max_exact)
                         * (REL_BUCKETS - max_exact)).astype(jnp.int32)
    large = jnp.minimum(large, REL_BUCKETS - 1)
    return jnp.where(nn < max_exact, nn, large).astype(jnp.int32)


def _pad_lanes(g, n):
    return jnp.pad(g, (0, n - g.shape[0])).reshape(1, n)


def kernel(x, c, positions, rel_table, norm1_g, norm2_g, w_ada, b_ada, w_in, mla_cq_g, w_uq, mla_ckv_g, w_ukv, mla_qn_g, mla_kn_g, sw_qn_g, sw_kn_g, sw_sinks, w_out, w_up, conv_w, conv_b, w_down, loss_target, m_rel_table, m_norm1_g, m_norm2_g, m_w_ada, m_b_ada, m_w_in, m_mla_cq_g, m_w_uq, m_mla_ckv_g, m_w_ukv, m_mla_qn_g, m_mla_kn_g, m_sw_qn_g, m_sw_kn_g, m_sw_sinks, m_w_out, m_w_up, m_conv_w, m_conv_b, m_w_down, v_rel_table, v_norm1_g, v_norm2_g, v_w_ada, v_b_ada, v_w_in, v_mla_cq_g, v_w_uq, v_mla_ckv_g, v_w_ukv, v_mla_qn_g, v_mla_kn_g, v_sw_qn_g, v_sw_kn_g, v_sw_sinks, v_w_out, v_w_up, v_conv_w, v_conv_b, v_w_down):
    nb, s, d = x.shape
    nl = norm1_g.shape[0]
    me = 4 * lax.axis_index("x") + 2 * lax.axis_index("y") + lax.axis_index("c")
    n_ada = w_ada.shape[2]

    rows2d = lambda a: a.reshape(-1, a.shape[-1])
    tr = lambda a: jnp.swapaxes(a, -1, -2)
    local = [rows2d(tr(w).astype(MXU)) for w in (w_in, w_uq, w_ukv, w_up)]
    local += [rows2d(w.astype(MXU)) for w in (w_out, w_down)] + [rows2d(conv_w), c]
    got = _all_gather(local, "gather_inputs")
    stack_rows = lambda a: a.reshape(N_DEV, nl, -1, a.shape[-1]).transpose(1, 0, 2, 3).reshape(nl, -1, a.shape[-1])
    w_in_t, w_uq_t, w_ukv_t, w_up_t, w_out_f, w_down_f = [stack_rows(a) for a in got[:6]]
    conv_full = got[6].reshape(N_DEV, nl, 3, -1).transpose(1, 2, 0, 3).reshape(nl, 3, -1)
    c_all = got[7].reshape(N_DEV * nb, d)
    zrows = lambda n: jnp.zeros((nl, n, d), MXU)
    w_in_pt = jnp.concatenate([w_in_t[:, :1152], w_in_t[:, 1184:1824], zrows(64), w_in_t[:, 1152:1184], zrows(160)], axis=1)
    w_uq_pt = jnp.pad(w_uq_t.reshape(nl, MLA_HEADS, MLA_QK, 256), ((0, 0), (0, 0), (0, LANES - MLA_QK), (0, 0))).reshape(nl, 768, 256)

    b_my = lax.dynamic_slice_in_dim(b_ada, me * n_ada, n_ada, axis=1).reshape(nl, 1, n_ada)
    mods_my = _ada_fwd(c_all, w_ada, b_my, "ada_fwd")
    mods, = _all_gather([mods_my.reshape(nl * N_DEV * nb, n_ada)], "gather_mods")
    mods = mods.reshape(N_DEV, nl, N_DEV * nb, n_ada).transpose(1, 2, 0, 3).reshape(nl, N_DEV * nb, N_DEV * n_ada)
    mods = lax.dynamic_slice_in_dim(mods, me * nb, nb, axis=1)
    shift1, scale1, gate1, shift2, scale2, gate2 = [mods[:, :, k * d:(k + 1) * d].reshape(nl, nb, 1, d) for k in range(6)]

    half = 16
    inv_freq = jnp.power(ROPE_THETA, -jnp.arange(half, dtype=F32) / half)
    ang = positions.astype(F32)[..., None] * inv_freq
    ones = lambda n: jnp.ones((nb, s, n), F32)
    zeros = lambda n: jnp.zeros((nb, s, n), F32)
    rope_c = jnp.concatenate([ones(64), jnp.cos(ang), jnp.cos(ang), ones(32)], axis=-1)
    rope_s = jnp.concatenate([zeros(64), jnp.sin(ang), jnp.sin(ang), zeros(32)], axis=-1)
    bucket = _t5_bucket()
    bias = _bias_build(jnp.pad(rel_table.T, ((0, 8 - SW_HEADS), (0, LANES - REL_BUCKETS))), bucket, "rel_bias")

    row = lambda g: g.reshape(1, -1)
    twice = lambda g: jnp.concatenate([g, g]).reshape(1, LANES)

    saved = []
    xl = x
    for l in range(nl):
        proj, h1 = _ln_mod_matmul(xl, row(norm1_g[l]), scale1[l], shift1[l], w_in_pt[l], f"l{l}_in_proj")
        prep_args = (proj, rope_c, rope_s, row(mla_cq_g[l]), row(mla_ckv_g[l]), _pad_lanes(mla_qn_g[l], LANES),
                     _pad_lanes(mla_kn_g[l], LANES), w_uq_pt[l], w_ukv_t[l])
        qm, km, vm = _mla_prep(*prep_args, f"l{l}_mla_prep")
        o_a, ct_a, cnt_a = _sb_fwd(proj, f"l{l}_sb_fwd")
        o_b, lse_b = _mla_fwd(qm, km, vm, f"l{l}_mla_fwd")
        sinks = jnp.broadcast_to(jnp.pad(sw_sinks[l], (0, 2))[:, None], (8, LANES))
        swa_args = (proj, twice(sw_qn_g[l]), twice(sw_kn_g[l]), sinks, bias)
        o_c = _swa_fwd(*swa_args, f"l{l}_swa_fwd")
        wo = [w_out_f[l, :256], w_out_f[l, 256:640], w_out_f[l, 640:]]
        x_mid, y1 = _out_proj([o_a, o_b, o_c], wo, gate1[l], xl, f"l{l}_out_proj")
        u_pre, h2 = _ln_mod_matmul(x_mid, row(norm2_g[l]), scale2[l], shift2[l], w_up_t[l], f"l{l}_up_proj")
        x_out, y2 = _conv_gate_matmul(u_pre, conv_full[l], row(conv_b[l]), w_down_f[l], gate2[l], x_mid, f"l{l}_ffn_down")
        saved.append(dict(x=xl, proj=proj, h1=h1, prep=prep_args, qkv=(qm, km, vm), o_a=o_a, ct_a=ct_a, cnt_a=cnt_a, o_b=o_b, lse_b=lse_b,
                          swa=swa_args, o_c=o_c, wo=wo, y1=y1, x_mid=x_mid, u_pre=u_pre, h2=h2, y2=y2))
        xl = x_out

    loss_blk, dx = _loss_grad(xl, loss_target, "loss")
    loss = lax.psum(loss_blk[0, 0], ("x", "y", "c"))

    t = nb * s
    flat = lambda a: a.reshape(t, a.shape[-1])
    grads = [None] * nl
    dmods = [None] * nl
    sharded_out = [None] * nl
    sharded_names = ["w_in", "w_uq", "w_ukv", "w_up", "w_out", "w_down", "conv_w"]
    sharded_wmv = dict(w_in=(w_in, m_w_in, v_w_in), w_uq=(w_uq, m_w_uq, v_w_uq), w_ukv=(w_ukv, m_w_ukv, v_w_ukv),
                       w_up=(w_up, m_w_up, v_w_up), w_out=(w_out, m_w_out, v_w_out), w_down=(w_down, m_w_down, v_w_down),
                       conv_w=(conv_w, m_conv_w, v_conv_w))
    n_in, n_up, n_out, n_dn = w_in.shape[2], w_up.shape[2], w_out.shape[1], w_down.shape[1]
    small_sizes = [w_uq[0].size, w_ukv[0].size, conv_w[0].size]
    n_small_rows = -(-sum(small_sizes) // d)
    rows_used = n_in + n_up + n_out + n_dn + n_small_rows
    rows_grad = -(-rows_used // GRAD_ROW_TILE) * GRAD_ROW_TILE

    def pack_rows(mats, vecs):
        lead = mats[0].shape[:-2]
        flat_part = jnp.concatenate(vecs, axis=-1)
        flat_part = jnp.pad(flat_part, [(0, 0)] * len(lead) + [(0, n_small_rows * d - flat_part.shape[-1])])
        tail = jnp.zeros(lead + (rows_grad - rows_used, d), F32)
        return jnp.concatenate(list(mats) + [flat_part.reshape(lead + (n_small_rows, d)), tail], axis=-2)

    def unpack_rows(a):
        o1, o2, o3, o4 = n_in, n_in + n_up, n_in + n_up + n_out, n_in + n_up + n_out + n_dn
        flat_part = a[o4:o4 + n_small_rows].reshape(-1)
        s1, s2, s3 = small_sizes[0], small_sizes[0] + small_sizes[1], sum(small_sizes)
        return dict(w_in=a[:o1].T, w_up=a[o1:o2].T, w_out=a[o2:o3], w_down=a[o3:o4],
                    w_uq=flat_part[:s1].reshape(w_uq.shape[2], -1).T, w_ukv=flat_part[s1:s2].reshape(w_ukv.shape[2], -1).T,
                    conv_w=flat_part[s2:s3].reshape(conv_w.shape[1:]))

    dbias = jnp.zeros((SW_HEADS, BLOCK, 2 * BLOCK), F32)
    for l in reversed(range(nl)):
        sv = saved[l]
        (da,), dy2, dgate2 = _gate_bwd_nt(dx, sv["y2"], gate2[l], [w_down_f[l]], f"l{l}_ffn_down_bwd")
        du, a_act, cstats = _conv_gate_bwd(da, sv["u_pre"], conv_full[l], row(conv_b[l]), f"l{l}_conv_gate_bwd")
        dx_mid, du_pre, dshift2, dscale2, dg2 = _ln_mod_matmul_bwd(
            du, w_up_t[l], sv["x_mid"], row(norm2_g[l]), scale2[l], dx, conv_full[l], f"l{l}_up_proj_bwd")
        g_w_down = _wgrad(flat(a_act), flat(dy2), f"l{l}_w_down_grad")
        g_w_up_t = _wgrad(flat(du_pre), flat(sv["h2"]), f"l{l}_w_up_grad")

        (do_a, do_b, do_c), dy1, dgate1 = _gate_bwd_nt(dx_mid, sv["y1"], gate1[l], sv["wo"], f"l{l}_out_proj_bwd")
        mix = jnp.concatenate([sv["o_a"], sv["o_b"], sv["o_c"]], axis=-1).astype(MXU)
        g_w_out = _wgrad(flat(mix), flat(dy1), f"l{l}_w_out_grad")

        dsb_q, dsb_k, dsb_v = _sb_bwd(sv["proj"], sv["ct_a"], sv["cnt_a"], do_a, f"l{l}_sb_bwd")
        qm, km, vm = sv["qkv"]
        dqm, dkm, dvm = _mla_bwd(qm, km, vm, sv["o_b"], sv["lse_b"], do_b, f"l{l}_mla_bwd")
        dsw_q, dkc, dkp, dvc, dvp, dbias_l, dsinks, dg_swq, dg_swk = _swa_bwd(*sv["swa"], do_c, f"l{l}_swa_bwd")
        dbias = dbias + dbias_l
        shift_up = lambda a: jnp.concatenate([a[:, BLOCK:], jnp.zeros((nb, BLOCK, LANES), F32)], axis=1)
        dsw_k = dkc + shift_up(dkp)
        dsw_v = dvc + shift_up(dvp)
        dcq, dckv, dkr, g_w_uq_pt, g_w_ukv_t, dg_cq, dg_ckv, dg_qn, dg_kn = _mla_prep_bwd(
            *sv["prep"], dqm, dkm, dvm, f"l{l}_mla_prep_bwd")
        dproj = jnp.concatenate([dsb_q, dsb_k, dsb_v, dcq, dckv, dsw_q, dsw_k, dsw_v, dkr, zeros(128)], axis=-1)
        dx, dproj_m, dshift1, dscale1, dg1 = _ln_mod_matmul_bwd(
            dproj, w_in_pt[l], sv["x"], row(norm1_g[l]), scale1[l], dx_mid, None, f"l{l}_in_proj_bwd")
        g_w_in_pt = _wgrad(flat(dproj_m), flat(sv["h1"]), f"l{l}_w_in_grad")

        g_w_in_t = jnp.concatenate([g_w_in_pt[:1152], g_w_in_pt[1856:1888], g_w_in_pt[1152:1792]], axis=0)
        g_w_uq_t = g_w_uq_pt.reshape(MLA_HEADS, LANES, 256)[:, :MLA_QK].reshape(MLA_HEADS * MLA_QK, 256)
        dmods[l] = jnp.concatenate([dshift1, dscale1, dgate1, dshift2, dscale2, dgate2], axis=-1).reshape(nb, 6 * d)

        per_dev = lambda g: g.reshape(N_DEV, -1, d)
        conv_dev = cstats[1:4].reshape(3, N_DEV, -1).transpose(1, 0, 2)
        send = pack_rows([per_dev(g_w_in_t), per_dev(g_w_up_t), per_dev(g_w_out), per_dev(g_w_down)],
                         [g_w_uq_t.reshape(N_DEV, -1), g_w_ukv_t.reshape(N_DEV, -1), conv_dev.reshape(N_DEV, -1)])
        sib = _pair_exchange(send, f"l{l}_pair_exchange")
        pair = _pair_add(lax.axis_index("c").reshape(1).astype(jnp.int32), send, sib, f"l{l}_pair_add")
        recv = _chip_exchange(pair, f"l{l}_chip_exchange")
        wmv = [pack_rows([a["w_in"][l].T, a["w_up"][l].T, a["w_out"][l], a["w_down"][l]],
                         [a["w_uq"][l].T.reshape(-1), a["w_ukv"][l].T.reshape(-1), a["conv_w"][l].reshape(-1)])
               for a in ({k: v[o] for k, v in sharded_wmv.items()} for o in range(3))]
        res = _adamw(recv, *wmv, f"l{l}_adamw_sharded")
        sharded_out[l] = [unpack_rows(r) for r in res]
        grads[l] = dict(
            norm1_g=dg1[0], norm2_g=dg2[0], mla_cq_g=dg_cq[0], mla_ckv_g=dg_ckv[0], mla_qn_g=dg_qn[0, :MLA_QK],
            mla_kn_g=dg_kn[0, :MLA_QK], sw_qn_g=dg_swq[0, :HEAD] + dg_swq[0, HEAD:], sw_kn_g=dg_swk[0, :HEAD] + dg_swk[0, HEAD:],
            sw_sinks=dsinks[:SW_HEADS, 0], conv_b=cstats[0])
    grad_x = dx
    g_rel = _bias_grad(dbias, bucket, "rel_table_grad")[:SW_HEADS, :REL_BUCKETS].T
    stack = lambda k: jnp.stack([grads[l][k] for l in range(nl)])

    dm_all, = _all_gather([jnp.stack(dmods).reshape(nl * nb, 6 * d)], "gather_dmods")
    dm_all = dm_all.reshape(N_DEV, nl, nb, 6 * d).transpose(1, 0, 2, 3).reshape(nl, N_DEV * nb, 6 * d)
    dm_my = lax.dynamic_slice_in_dim(dm_all, me * n_ada, n_ada, axis=2)
    g_w_ada, g_b_ada = _ada_bwd(c_all, dm_my, dm_all, "ada_bwd")
    g_b_ada = g_b_ada.reshape(nl, 6 * d)

    big_out = [{k: jnp.stack([sharded_out[l][o][k] for l in range(nl)]) for k in sharded_names} for o in range(4)]
    packf = lambda dct, names, rows: jnp.pad(jnp.concatenate([dct[k].reshape(-1) for k in names]),
                                             (0, rows * LANES - sum(dct[k].size for k in names))).reshape(rows, LANES)

    small_names = ["rel_table", "norm1_g", "norm2_g", "mla_cq_g", "mla_ckv_g", "mla_qn_g", "mla_kn_g",
                   "sw_qn_g", "sw_kn_g", "sw_sinks", "conv_b"]
    small_w = dict(rel_table=rel_table, norm1_g=norm1_g, norm2_g=norm2_g, mla_cq_g=mla_cq_g, mla_ckv_g=mla_ckv_g,
                   mla_qn_g=mla_qn_g, mla_kn_g=mla_kn_g, sw_qn_g=sw_qn_g, sw_kn_g=sw_kn_g, sw_sinks=sw_sinks, conv_b=conv_b)
    small_m = dict(rel_table=m_rel_table, norm1_g=m_norm1_g, norm2_g=m_norm2_g, mla_cq_g=m_mla_cq_g, mla_ckv_g=m_mla_ckv_g,
                   mla_qn_g=m_mla_qn_g, mla_kn_g=m_mla_kn_g, sw_qn_g=m_sw_qn_g, sw_kn_g=m_sw_kn_g, sw_sinks=m_sw_sinks, conv_b=m_conv_b)
    small_v = dict(rel_table=v_rel_table, norm1_g=v_norm1_g, norm2_g=v_norm2_g, mla_cq_g=v_mla_cq_g, mla_ckv_g=v_mla_ckv_g,
                   mla_qn_g=v_mla_qn_g, mla_kn_g=v_mla_kn_g, sw_qn_g=v_sw_qn_g, sw_kn_g=v_sw_kn_g, sw_sinks=v_sw_sinks, conv_b=v_conv_b)
    small_g = {k: (g_rel if k == "rel_table" else stack(k)) for k in small_names}
    n_small = sum(small_w[k].size for k in small_names)
    rows_small = -(-n_small // (8 * LANES)) * 8
    small_parts, = _all_gather([packf(small_g, small_names, rows_small)], "gather_small_grads")
    small_out = _adamw(small_parts, packf(small_w, small_names, rows_small), packf(small_m, small_names, rows_small),
                       packf(small_v, small_names, rows_small), "adamw_replicated")
    small_out = [dict(zip(small_names, _unpack(o.reshape(-1), [small_w[k].shape for k in small_names]))) for o in small_out]

    two_d = lambda a: a.reshape(-1, a.shape[-1])
    res_w = _adamw(two_d(g_w_ada)[None], two_d(w_ada), two_d(m_w_ada), two_d(v_w_ada), "adamw_w_ada")
    res_b = _adamw(g_b_ada[None], b_ada, m_b_ada, v_b_ada, "adamw_b_ada")
    ada_out = [dict(w_ada=rw.reshape(w_ada.shape), b_ada=rb) for rw, rb in zip(res_w, res_b)]

    order = ["rel_table", "norm1_g", "norm2_g", "w_ada", "b_ada", "w_in", "mla_cq_g", "w_uq", "mla_ckv_g", "w_ukv",
             "mla_qn_g", "mla_kn_g", "sw_qn_g", "sw_kn_g", "sw_sinks", "w_out", "w_up", "conv_w", "conv_b", "w_down"]
    outs = [{**big_out[k], **small_out[k], **ada_out[k]} for k in range(4)]
    return (loss, grad_x, *[outs[0][n] for n in order], *[outs[1][n] for n in order],
            *[outs[2][n] for n in order], *[outs[3][n] for n in order])
```

```python
import math

import jax
import jax.numpy as jnp
from jax import lax
from jax.experimental import pallas as pl
from jax.experimental.pallas import tpu as pltpu

F32 = jnp.float32
MXU = jnp.bfloat16
EPS = 1e-6
NEG = -1e30
VMEM_LIMIT_BYTES = 56 * 1024 * 1024
N_DEV = 8
MESH = pl.DeviceIdType.MESH

D_MODEL = 1024
D_FF = 2816
HEAD = 64
LANES = 128
MLA_HEADS = 6
MLA_QK = 96
SW_HEADS = 6
REL_BUCKETS = 32
BLOCK = 128
SB_SCALE = HEAD ** -0.5
SB_DEAD = -105.0
SW_SCALE = HEAD ** -0.5
MLA_SCALE = MLA_QK ** -0.5
ROPE_THETA = 10000.0
D_IN_PAD = 2048
COL_SBQ, COL_SBK, COL_SBV, COL_CQ, COL_CKV, COL_SWQ, COL_SWK, COL_SWV, COL_KR = 0, 256, 512, 768, 1024, 1152, 1536, 1664, 1792

ROW_TILE_BYTES = 1 << 20
ADAM_LR, ADAM_B1, ADAM_B2, ADAM_EPS, ADAM_WD, ADAM_STEP = 0.001, 0.9, 0.999, 1e-08, 0.01, 10


def _cp(*sem):
    return pltpu.CompilerParams(dimension_semantics=sem, vmem_limit_bytes=VMEM_LIMIT_BYTES)


def _iota(shape, dim):
    return lax.broadcasted_iota(jnp.int32, shape, dim)


def _dot(a, b):
    return jnp.dot(a, b, preferred_element_type=F32)


def _dot_nt(a, b):
    return lax.dot_general(a, b, (((1,), (1,)), ((), ())), preferred_element_type=F32)


def _dot_tn(a, b):
    return lax.dot_general(a, b, (((0,), (0,)), ((), ())), preferred_element_type=F32)


def _cumdot(x, u):
    hi = x.astype(MXU)
    r = x - hi.astype(F32)
    mid = r.astype(MXU)
    lo = (r - mid.astype(F32)).astype(MXU)
    return _dot(hi, u) + _dot(mid, u) + _dot(lo, u)


def _sigmoid(x):
    return 1.0 / (1.0 + jnp.exp(-x))


def _rsum(x):
    return jnp.sum(x, axis=-1, keepdims=True)


def _csum(x):
    return jnp.sum(x, axis=0, keepdims=True)


def _all_gather(xs, name):
    na = len(xs)

    def body(*refs):
        x_refs, out_refs = refs[:na], refs[na:2 * na]
        send_sems, recv_sems, local_sems = refs[2 * na:]
        x, y, c = lax.axis_index("x"), lax.axis_index("y"), lax.axis_index("c")
        me, sibling = (x, y, c), (x, y, 1 - c)
        chips = [(1 - x, y), (x, 1 - y), (1 - x, 1 - y)]

        def slot(a, px, py, pc):
            return out_refs[a].at[4 * px + 2 * py + pc]

        def copy(a, k, block, to, src=None):
            return pltpu.make_async_remote_copy(
                src_ref=slot(a, *block) if src is None else src, dst_ref=slot(a, *block),
                send_sem=send_sems.at[7 * a + k], recv_sem=recv_sems.at[7 * a + k], device_id=to, device_id_type=MESH)

        mines, sends = [], []
        for a in range(na):
            mines.append(pltpu.make_async_copy(x_refs[a], slot(a, *me), local_sems.at[a]))
            mines[-1].start()
            first = [copy(a, 0, me, sibling, src=x_refs[a])]
            first += [copy(a, 1 + j, me, (*chip, c), src=x_refs[a]) for j, chip in enumerate(chips)]
            for cp in first:
                cp.start()
            sends += first
        for j, chip in enumerate(chips):
            for a in range(na):
                copy(a, 1 + j, (*chip, c), me).wait_recv()
                sends.append(copy(a, 4 + j, (*chip, c), sibling))
                sends[-1].start()
        for a in range(na):
            copy(a, 0, sibling, me).wait_recv()
            for j, chip in enumerate(chips):
                copy(a, 4 + j, (*chip, 1 - c), me).wait_recv()
        for cp in sends:
            cp.wait_send()
        for mine in mines:
            mine.wait()

    hbm = pl.BlockSpec(memory_space=pl.ANY)
    return pl.pallas_call(
        body, name=name, out_shape=[jax.ShapeDtypeStruct((N_DEV,) + a.shape, a.dtype) for a in xs],
        in_specs=[hbm] * na, out_specs=[hbm] * na,
        scratch_shapes=[pltpu.SemaphoreType.DMA((7 * na,)), pltpu.SemaphoreType.DMA((7 * na,)),
                        pltpu.SemaphoreType.DMA((na,))],
    )(*xs)


def _pair_exchange(xs, name):
    na = len(xs)

    def body(*refs):
        x_refs, out_refs = refs[:na], refs[na:2 * na]
        send_sems, recv_sems = refs[2 * na:]
        x, y, c = lax.axis_index("x"), lax.axis_index("y"), lax.axis_index("c")
        copies = []
        for a in range(na):
            for q in range(4):
                copies.append(pltpu.make_async_remote_copy(
                    src_ref=x_refs[a].at[2 * q + 1 - c], dst_ref=out_refs[a].at[q],
                    send_sem=send_sems.at[4 * a + q], recv_sem=recv_sems.at[4 * a + q],
                    device_id=(x, y, 1 - c), device_id_type=MESH))
                copies[-1].start()
        for cp in copies:
            cp.wait()

    hbm = pl.BlockSpec(memory_space=pl.ANY)
    return pl.pallas_call(
        body, name=name, out_shape=[jax.ShapeDtypeStruct((4,) + a.shape[1:], a.dtype) for a in xs],
        in_specs=[hbm] * na, out_specs=[hbm] * na,
        scratch_shapes=[pltpu.SemaphoreType.DMA((4 * na,)), pltpu.SemaphoreType.DMA((4 * na,))])(*xs)


def _row_tile(r, ncol):
    if r * ncol * 4 <= ROW_TILE_BYTES:
        return r
    return max(t for t in range(16, r, 16) if r % t == 0 and t * ncol * 4 <= ROW_TILE_BYTES)


def _pair_add(core, xs, sib, name):
    _, r, ncol = xs.shape
    tr = _row_tile(r, ncol)

    def body(c_ref, x_ref, s_ref, o_ref):
        o_ref[...] = (x_ref[...] + s_ref[...]).astype(MXU)

    return pl.pallas_call(
        body, name=name,
        grid_spec=pltpu.PrefetchScalarGridSpec(
            num_scalar_prefetch=1, grid=(4, r // tr),
            in_specs=[pl.BlockSpec((None, tr, ncol), lambda q, i, c_ref: (2 * q + c_ref[0], i, 0)),
                      pl.BlockSpec((None, tr, ncol), lambda q, i, c_ref: (q, i, 0))],
            out_specs=pl.BlockSpec((None, tr, ncol), lambda q, i, c_ref: (q, i, 0))),
        out_shape=jax.ShapeDtypeStruct((4, r, ncol), MXU),
        compiler_params=_cp("parallel", "parallel"))(core, xs, sib)


def _chip_exchange(xs, name):
    na = len(xs)
    flips = [(1, 0), (0, 1), (1, 1)]

    def body(*refs):
        x_refs, out_refs = refs[:na], refs[na:2 * na]
        send_sems, recv_sems, local_sems = refs[2 * na:]
        x, y, c = lax.axis_index("x"), lax.axis_index("y"), lax.axis_index("c")
        me = 2 * x + y
        mines, copies = [], []
        for a in range(na):
            mines.append(pltpu.make_async_copy(x_refs[a].at[me], out_refs[a].at[me], local_sems.at[a]))
            mines[-1].start()
        for k, (dx, dy) in enumerate(flips):
            px = 1 - x if dx else x
            py = 1 - y if dy else y
            for a in range(na):
                copies.append(pltpu.make_async_remote_copy(
                    src_ref=x_refs[a].at[2 * px + py], dst_ref=out_refs[a].at[me],
                    send_sem=send_sems.at[3 * a + k], recv_sem=recv_sems.at[3 * a + k],
                    device_id=(px, py, c), device_id_type=MESH))
                copies[-1].start()
        for cp in copies:
            cp.wait()
        for mine in mines:
            mine.wait()

    hbm = pl.BlockSpec(memory_space=pl.ANY)
    return pl.pallas_call(
        body, name=name, out_shape=[jax.ShapeDtypeStruct(a.shape, a.dtype) for a in xs],
        in_specs=[hbm] * na, out_specs=[hbm] * na,
        scratch_shapes=[pltpu.SemaphoreType.DMA((3 * na,)), pltpu.SemaphoreType.DMA((3 * na,)),
                        pltpu.SemaphoreType.DMA((na,))])(*xs)


def _ada_fwd(c_all, w_ada, b_my, name):
    nl, d, n = w_ada.shape
    nb = c_all.shape[0]

    def body(c_ref, w_ref, b_ref, o_ref):
        cv = c_ref[...]
        sc = (cv * _sigmoid(cv)).astype(MXU)
        o_ref[...] = _dot(sc, w_ref[...].astype(MXU)) + b_ref[...]

    return pl.pallas_call(
        body, name=name, grid=(nl,),
        in_specs=[pl.BlockSpec((nb, d), lambda l: (0, 0)),
                  pl.BlockSpec((None, d, n), lambda l: (l, 0, 0)),
                  pl.BlockSpec((None, 1, n), lambda l: (l, 0, 0))],
        out_specs=pl.BlockSpec((None, nb, n), lambda l: (l, 0, 0)),
        out_shape=jax.ShapeDtypeStruct((nl, nb, n), F32),
        compiler_params=_cp("parallel"))(c_all, w_ada, b_my)


def _ada_bwd(c_all, dmods_my, dmods_all, name):
    nl, nb, n = dmods_my.shape
    d = c_all.shape[1]
    nfull = dmods_all.shape[2]

    def body(c_ref, dm_ref, da_ref, dw_ref, db_ref):
        cv = c_ref[...]
        sc = (cv * _sigmoid(cv)).astype(MXU)
        dw_ref[...] = _dot_tn(sc, dm_ref[...].astype(MXU))
        db_ref[...] = _csum(da_ref[...])

    return pl.pallas_call(
        body, name=name, grid=(nl,),
        in_specs=[pl.BlockSpec((nb, d), lambda l: (0, 0)),
                  pl.BlockSpec((None, nb, n), lambda l: (l, 0, 0)),
                  pl.BlockSpec((None, nb, nfull), lambda l: (l, 0, 0))],
        out_specs=[pl.BlockSpec((None, d, n), lambda l: (l, 0, 0)),
                   pl.BlockSpec((None, 1, nfull), lambda l: (l, 0, 0))],
        out_shape=[jax.ShapeDtypeStruct((nl, d, n), F32), jax.ShapeDtypeStruct((nl, 1, nfull), F32)],
        compiler_params=_cp("parallel"))(c_all, dmods_my, dmods_all)


def _ln_mod_matmul(x, g, scale, shift, w, name):
    nb, s, d = x.shape
    n = w.shape[0]
    tm, tn = min(1024, s), 512

    def body(x_ref, g_ref, sc_ref, sh_ref, w_ref, y_ref, h_ref, h_s):
        @pl.when(pl.program_id(2) == 0)
        def _():
            xf = x_ref[...]
            rstd = lax.rsqrt(jnp.mean(xf * xf, axis=-1, keepdims=True) + EPS)
            hv = (xf * rstd * g_ref[...]) * (1.0 + sc_ref[...]) + sh_ref[...]
            h_s[...] = hv.astype(MXU)
            h_ref[...] = h_s[...]

        y_ref[...] = _dot_nt(h_s[...], w_ref[...])

    return pl.pallas_call(
        body, name=name, grid=(nb, s // tm, n // tn),
        in_specs=[pl.BlockSpec((None, tm, d), lambda b, i, j: (b, i, 0)),
                  pl.BlockSpec((1, d), lambda b, i, j: (0, 0)),
                  pl.BlockSpec((None, 1, d), lambda b, i, j: (b, 0, 0)),
                  pl.BlockSpec((None, 1, d), lambda b, i, j: (b, 0, 0)),
                  pl.BlockSpec((tn, d), lambda b, i, j: (j, 0))],
        out_specs=[pl.BlockSpec((None, tm, tn), lambda b, i, j: (b, i, j)),
                   pl.BlockSpec((None, tm, d), lambda b, i, j: (b, i, 0))],
        out_shape=[jax.ShapeDtypeStruct((nb, s, n), F32), jax.ShapeDtypeStruct((nb, s, d), MXU)],
        scratch_shapes=[pltpu.VMEM((tm, d), MXU)],
        compiler_params=_cp("parallel", "parallel", "arbitrary"))(x, g, scale, shift, w)


def _ln_mod_matmul_bwd(dy, w, x, g, scale, dres, conv_w, name):
    nb, s, n = dy.shape
    d = x.shape[-1]
    tm, tn = min(512, s), 512
    ni, nj = s // tm, n // tn
    hb = tm // 8
    conv = conv_w is not None

    def body(*refs):
        if conv:
            dy_ref, nx_ref, cw_ref = refs[:3]
            refs = refs[3:]
        else:
            dy_ref = refs[0]
            refs = refs[1:]
        w_ref, x_ref, g_ref, sc_ref, dr_ref, dx_ref, dyp_ref, dsh_ref, dsc_ref, dg_ref, acc = refs
        b, i, j = pl.program_id(0), pl.program_id(1), pl.program_id(2)

        @pl.when(j == 0)
        def _():
            acc[...] = jnp.zeros_like(acc)

        @pl.when((j == 0) & (i == 0))
        def _():
            dsh_ref[...] = jnp.zeros_like(dsh_ref)
            dsc_ref[...] = jnp.zeros_like(dsc_ref)

        @pl.when((j == 0) & (i == 0) & (b == 0))
        def _():
            dg_ref[...] = jnp.zeros_like(dg_ref)

        dv = dy_ref[...]
        if conv:
            rows = _iota((tm, 1), 0)
            nx = jnp.where(i == ni - 1, 0.0, nx_ref[...])
            n1 = jnp.where(rows == tm - 1, nx[0:1, :], pltpu.roll(dv, tm - 1, 0))
            n2 = jnp.where(rows == tm - 2, nx[0:1, :], jnp.where(rows == tm - 1, nx[1:2, :], pltpu.roll(dv, tm - 2, 0)))
            cw = cw_ref[...]
            dv = cw[2:3, :] * dv + cw[1:2, :] * n1 + cw[0:1, :] * n2
        dp = dv.astype(MXU)
        dyp_ref[...] = dp
        acc[...] += _dot(dp, w_ref[...])

        @pl.when(j == nj - 1)
        def _():
            dh = acc[...]
            xf = x_ref[...]
            rstd = lax.rsqrt(jnp.mean(xf * xf, axis=-1, keepdims=True) + EPS)
            xn = xf * rstd
            gg = g_ref[...]
            sc1 = 1.0 + sc_ref[...]
            dsh_ref[...] += _csum(dh)
            dsc_ref[...] += _csum(dh * xn * gg)
            dg_ref[...] += _csum(dh * xn * sc1)
            dn = dh * gg * sc1
            dx_ref[...] = dr_ref[...] + rstd * (dn - xn * jnp.mean(dn * xn, axis=-1, keepdims=True))

    in_specs = [pl.BlockSpec((None, tm, tn), lambda b, i, j: (b, i, j))]
    args = [dy]
    if conv:
        in_specs += [pl.BlockSpec((None, 8, tn), lambda b, i, j: (b, jnp.minimum((i + 1) * hb, s // 8 - 1), j)),
                     pl.BlockSpec((3, tn), lambda b, i, j: (0, j))]
        args += [dy, conv_w]
    in_specs += [pl.BlockSpec((tn, d), lambda b, i, j: (j, 0)),
                 pl.BlockSpec((None, tm, d), lambda b, i, j: (b, i, 0)),
                 pl.BlockSpec((1, d), lambda b, i, j: (0, 0)),
                 pl.BlockSpec((None, 1, d), lambda b, i, j: (b, 0, 0)),
                 pl.BlockSpec((None, tm, d), lambda b, i, j: (b, i, 0))]
    args += [w, x, g, scale, dres]
    return pl.pallas_call(
        body, name=name, grid=(nb, ni, nj), in_specs=in_specs,
        out_specs=[pl.BlockSpec((None, tm, d), lambda b, i, j: (b, i, 0)),
                   pl.BlockSpec((None, tm, tn), lambda b, i, j: (b, i, j)),
                   pl.BlockSpec((None, 1, d), lambda b, i, j: (b, 0, 0)),
                   pl.BlockSpec((None, 1, d), lambda b, i, j: (b, 0, 0)),
                   pl.BlockSpec((1, d), lambda b, i, j: (0, 0))],
        out_shape=[jax.ShapeDtypeStruct((nb, s, d), F32), jax.ShapeDtypeStruct((nb, s, n), MXU),
                   jax.ShapeDtypeStruct((nb, 1, d), F32), jax.ShapeDtypeStruct((nb, 1, d), F32),
                   jax.ShapeDtypeStruct((1, d), F32)],
        scratch_shapes=[pltpu.VMEM((tm, d), F32)],
        compiler_params=_cp("arbitrary", "arbitrary", "arbitrary"))(*args)


def _wgrad(xm, dym, name):
    t, k = xm.shape
    n = dym.shape[1]
    tk = 1408 if k % 1408 == 0 else 1024
    tt = min(512, t)

    def body(x_ref, dy_ref, o_ref):
        @pl.when(pl.program_id(1) == 0)
        def _():
            o_ref[...] = jnp.zeros_like(o_ref)

        o_ref[...] += _dot_tn(x_ref[...], dy_ref[...])

    return pl.pallas_call(
        body, name=name, grid=(k // tk, t // tt),
        in_specs=[pl.BlockSpec((tt, tk), lambda a, c: (c, a)),
                  pl.BlockSpec((tt, n), lambda a, c: (c, 0))],
        out_specs=pl.BlockSpec((tk, n), lambda a, c: (a, 0)),
        out_shape=jax.ShapeDtypeStruct((k, n), F32),
        compiler_params=_cp("parallel", "arbitrary"))(xm, dym)


def _out_proj(parts, ws, gate, res, name):
    nb, s, d = res.shape
    tm = min(512, s)
    npart = len(parts)

    def body(*refs):
        p_refs, w_refs = refs[:npart], refs[npart:2 * npart]
        gt_ref, res_ref, xo_ref, y_ref = refs[2 * npart:]
        y = _dot(p_refs[0][...].astype(MXU), w_refs[0][...])
        for p_ref, w_ref in zip(p_refs[1:], w_refs[1:]):
            y = y + _dot(p_ref[...].astype(MXU), w_ref[...])
        y_ref[...] = y
        xo_ref[...] = res_ref[...] + gt_ref[...] * y

    in_specs = [pl.BlockSpec((None, tm, p.shape[-1]), lambda b, i: (b, i, 0)) for p in parts]
    in_specs += [pl.BlockSpec(w.shape, lambda b, i: (0, 0)) for w in ws]
    in_specs += [pl.BlockSpec((None, 1, d), lambda b, i: (b, 0, 0)),
                 pl.BlockSpec((None, tm, d), lambda b, i: (b, i, 0))]
    return pl.pallas_call(
        body, name=name, grid=(nb, s // tm), in_specs=in_specs,
        out_specs=[pl.BlockSpec((None, tm, d), lambda b, i: (b, i, 0))] * 2,
        out_shape=[jax.ShapeDtypeStruct((nb, s, d), F32)] * 2,
        compiler_params=_cp("parallel", "parallel"))(*parts, *ws, gate, res)


def _gate_bwd_nt(dx, y, gate, ws, name):
    nb, s, d = dx.shape
    tm = min(256, s)
    npart = len(ws)

    def body(*refs):
        dx_ref, y_ref, gt_ref = refs[:3]
        w_refs = refs[3:3 + npart]
        da_refs = refs[3 + npart:3 + 2 * npart]
        dy_ref, dgt_ref = refs[3 + 2 * npart:]

        @pl.when(pl.program_id(1) == 0)
        def _():
            dgt_ref[...] = jnp.zeros_like(dgt_ref)

        dxv = dx_ref[...]
        dyv = (dxv * gt_ref[...]).astype(MXU)
        dy_ref[...] = dyv
        dgt_ref[...] += _csum(dxv * y_ref[...])
        for w_ref, da_ref in zip(w_refs, da_refs):
            da_ref[...] = _dot_nt(dyv, w_ref[...])

    tile = pl.BlockSpec((None, tm, d), lambda b, i: (b, i, 0))
    row = pl.BlockSpec((None, 1, d), lambda b, i: (b, 0, 0))
    outs = pl.pallas_call(
        body, name=name, grid=(nb, s // tm),
        in_specs=[tile, tile, row] + [pl.BlockSpec(w.shape, lambda b, i: (0, 0)) for w in ws],
        out_specs=[pl.BlockSpec((None, tm, w.shape[0]), lambda b, i: (b, i, 0)) for w in ws] + [tile, row],
        out_shape=[jax.ShapeDtypeStruct((nb, s, w.shape[0]), F32) for w in ws]
        + [jax.ShapeDtypeStruct((nb, s, d), MXU), jax.ShapeDtypeStruct((nb, 1, d), F32)],
        compiler_params=_cp("arbitrary", "arbitrary"))(dx, y, gate, *ws)
    return outs[:npart], outs[npart], outs[npart + 1]


def _conv_shifts(xv, halo, rows):
    p1 = jnp.where(rows == 0, halo[7:8, :], pltpu.roll(xv, 1, 0))
    p2 = jnp.where(rows == 0, halo[6:7, :], jnp.where(rows == 1, halo[7:8, :], pltpu.roll(xv, 2, 0)))
    return p1, p2


def _conv_gate_matmul(u, cw, cb, wd, gate, res, name):
    nb, s, f2 = u.shape
    f = f2 // 2
    d = wd.shape[1]
    tm = min(512, s)
    tk = f // 2
    nk = f // tk
    hb = tm // 8

    def body(ug_ref, uv_ref, hg_ref, hv_ref, cwg_ref, cwv_ref, cbg_ref, cbv_ref, wd_ref, gt_ref, res_ref,
             xo_ref, y_ref, acc):
        i, k = pl.program_id(1), pl.program_id(2)

        @pl.when(k == 0)
        def _():
            acc[...] = jnp.zeros_like(acc)

        rows = _iota((tm, 1), 0)

        def conv(x_ref, h_ref, w_ref, b_ref):
            xv = x_ref[...]
            halo = jnp.where(i == 0, 0.0, h_ref[...])
            p1, p2 = _conv_shifts(xv, halo, rows)
            wv = w_ref[...]
            return wv[2:3, :] * xv + wv[1:2, :] * p1 + wv[0:1, :] * p2 + b_ref[...]

        gv = conv(ug_ref, hg_ref, cwg_ref, cbg_ref)
        vv = conv(uv_ref, hv_ref, cwv_ref, cbv_ref)
        av = gv * _sigmoid(gv) * vv
        acc[...] += _dot(av.astype(MXU), wd_ref[...])

        @pl.when(k == nk - 1)
        def _():
            y = acc[...]
            y_ref[...] = y
            xo_ref[...] = res_ref[...] + gt_ref[...] * y

    def halo_idx(off):
        return lambda b, i, k: (b, jnp.maximum(i * hb - 1, 0), k + off)

    tile = pl.BlockSpec((None, tm, d), lambda b, i, k: (b, i, 0))
    return pl.pallas_call(
        body, name=name, grid=(nb, s // tm, nk),
        in_specs=[pl.BlockSpec((None, tm, tk), lambda b, i, k: (b, i, k)),
                  pl.BlockSpec((None, tm, tk), lambda b, i, k: (b, i, k + nk)),
                  pl.BlockSpec((None, 8, tk), halo_idx(0)),
                  pl.BlockSpec((None, 8, tk), halo_idx(nk)),
                  pl.BlockSpec((3, tk), lambda b, i, k: (0, k)),
                  pl.BlockSpec((3, tk), lambda b, i, k: (0, k + nk)),
                  pl.BlockSpec((1, tk), lambda b, i, k: (0, k)),
                  pl.BlockSpec((1, tk), lambda b, i, k: (0, k + nk)),
                  pl.BlockSpec((tk, d), lambda b, i, k: (k, 0)),
                  pl.BlockSpec((None, 1, d), lambda b, i, k: (b, 0, 0)),
                  tile],
        out_specs=[tile, tile],
        out_shape=[jax.ShapeDtypeStruct((nb, s, d), F32)] * 2,
        scratch_shapes=[pltpu.VMEM((tm, d), F32)],
        compiler_params=_cp("parallel", "parallel", "arbitrary"))(u, u, u, u, cw, cw, cb, cb, wd, gate, res)


def _conv_gate_bwd(da, u, cw, cb, name):
    nb, s, f2 = u.shape
    f = f2 // 2
    tm = min(128, s)
    hb = tm // 8

    def body(da_ref, u_ref, h_ref, cw_ref, cb_ref, du_ref, a_ref, st_ref):
        b, i = pl.program_id(0), pl.program_id(1)

        @pl.when((b == 0) & (i == 0))
        def _():
            st_ref[...] = jnp.zeros_like(st_ref)

        rows = _iota((tm, 1), 0)
        first = i == 0

        def conv(cs):
            xv = u_ref[:, cs]
            halo = jnp.where(first, 0.0, h_ref[:, cs])
            p1, p2 = _conv_shifts(xv, halo, rows)
            wv = cw_ref[:, cs]
            return xv, p1, p2, wv[2:3, :] * xv + wv[1:2, :] * p1 + wv[0:1, :] * p2 + cb_ref[:, cs]

        def stats(cs, du, xv, p1, p2):
            du_ref[:, cs] = du
            st_ref[0:1, cs] += _csum(du)
            st_ref[1:2, cs] += _csum(du * p2)
            st_ref[2:3, cs] += _csum(du * p1)
            st_ref[3:4, cs] += _csum(du * xv)

        for k in range(f // LANES):
            cg = slice(k * LANES, (k + 1) * LANES)
            cv = slice(f + k * LANES, f + (k + 1) * LANES)
            xg, g1, g2, gv = conv(cg)
            xv, v1, v2, vv = conv(cv)
            sg = _sigmoid(gv)
            sl = gv * sg
            a_ref[:, cg] = (sl * vv).astype(MXU)
            dav = da_ref[:, cg]
            stats(cg, dav * vv * (sg * (1.0 + gv * (1.0 - sg))), xg, g1, g2)
            stats(cv, dav * sl, xv, v1, v2)

    return pl.pallas_call(
        body, name=name, grid=(nb, s // tm),
        in_specs=[pl.BlockSpec((None, tm, f), lambda b, i: (b, i, 0)),
                  pl.BlockSpec((None, tm, f2), lambda b, i: (b, i, 0)),
                  pl.BlockSpec((None, 8, f2), lambda b, i: (b, jnp.maximum(i * hb - 1, 0), 0)),
                  pl.BlockSpec((3, f2), lambda b, i: (0, 0)),
                  pl.BlockSpec((1, f2), lambda b, i: (0, 0))],
        out_specs=[pl.BlockSpec((None, tm, f2), lambda b, i: (b, i, 0)),
                   pl.BlockSpec((None, tm, f), lambda b, i: (b, i, 0)),
                   pl.BlockSpec((8, f2), lambda b, i: (0, 0))],
        out_shape=[jax.ShapeDtypeStruct((nb, s, f2), F32), jax.ShapeDtypeStruct((nb, s, f), MXU),
                   jax.ShapeDtypeStruct((8, f2), F32)],
        compiler_params=_cp("arbitrary", "arbitrary"))(da, u, u, cw, cb)


def _rot(xv, lane):
    return jnp.where((lane >= 64) & (lane < 80), -pltpu.roll(xv, 112, 1),
                     jnp.where((lane >= 80) & (lane < 96), pltpu.roll(xv, 16, 1), 0.0))


def _rot_t(dv, lane):
    return jnp.where((lane >= 80) & (lane < 96), -pltpu.roll(dv, 16, 1),
                     jnp.where((lane >= 64) & (lane < 80), pltpu.roll(dv, 112, 1), 0.0))


def _mla_prep_specs(s, tm):
    def blk(width, col):
        return pl.BlockSpec((None, tm, width), lambda b, i: (b, i, col // width))

    full = lambda shape: pl.BlockSpec(shape, lambda b, i: (0, 0))
    return [blk(256, COL_CQ), blk(128, COL_CKV), blk(128, COL_KR),
            pl.BlockSpec((None, tm, LANES), lambda b, i: (b, i, 0)),
            pl.BlockSpec((None, tm, LANES), lambda b, i: (b, i, 0)),
            full((1, 256)), full((1, 128)), full((1, 128)), full((1, 128)),
            full((768, 256)), full((768, 128))]


def _mla_prep(proj, cs, sn, gcq, gckv, gqn, gkn, wuq, wukv, name):
    nb, s, _ = proj.shape
    tm = min(256, s)

    def body(cq_ref, ckv_ref, kr_ref, c_ref, s_ref, gcq_ref, gckv_ref, gqn_ref, gkn_ref, wuq_ref, wukv_ref,
             q_ref, k_ref, v_ref):
        lane = _iota((tm, LANES), 1)
        cv, sv = c_ref[...], s_ref[...]
        cq = cq_ref[...]
        cqn = cq * lax.rsqrt(jnp.mean(cq * cq, axis=-1, keepdims=True) + EPS) * gcq_ref[...]
        qb = _dot_nt(cqn.astype(MXU), wuq_ref[...])
        ckv = ckv_ref[...]
        ckvn = ckv * lax.rsqrt(jnp.mean(ckv * ckv, axis=-1, keepdims=True) + EPS) * gckv_ref[...]
        kvb = _dot_nt(ckvn.astype(MXU), wukv_ref[...])
        kr = kr_ref[...]
        for h in range(MLA_HEADS):
            hs = slice(h * LANES, (h + 1) * LANES)
            qh = qb[:, hs]
            qn = qh * lax.rsqrt(_rsum(qh * qh) / MLA_QK + EPS) * gqn_ref[...]
            q_ref[:, hs] = (qn * cv + _rot(qn, lane) * sv).astype(MXU)
            kc = jnp.where(lane < HEAD, kvb[:, hs], kr)
            kn = kc * lax.rsqrt(_rsum(kc * kc) / MLA_QK + EPS) * gkn_ref[...]
            k_ref[:, hs] = (kn * cv + _rot(kn, lane) * sv).astype(MXU)
        for j in range(MLA_HEADS // 2):
            va = kvb[:, (2 * j) * LANES:(2 * j + 1) * LANES]
            vb = kvb[:, (2 * j + 1) * LANES:(2 * j + 2) * LANES]
            v_ref[:, j * LANES:(j + 1) * LANES] = jnp.where(lane < HEAD, pltpu.roll(va, HEAD, 1), vb).astype(MXU)

    return pl.pallas_call(
        body, name=name, grid=(nb, s // tm), in_specs=_mla_prep_specs(s, tm),
        out_specs=[pl.BlockSpec((None, tm, 768), lambda b, i: (b, i, 0)),
                   pl.BlockSpec((None, tm, 768), lambda b, i: (b, i, 0)),
                   pl.BlockSpec((None, tm, 384), lambda b, i: (b, i, 0))],
        out_shape=[jax.ShapeDtypeStruct((nb, s, 768), MXU), jax.ShapeDtypeStruct((nb, s, 768), MXU),
                   jax.ShapeDtypeStruct((nb, s, 384), MXU)],
        compiler_params=_cp("parallel", "parallel"))(proj, proj, proj, cs, sn, gcq, gckv, gqn, gkn, wuq, wukv)


def _mla_prep_bwd(proj, cs, sn, gcq, gckv, gqn, gkn, wuq, wukv, dq, dk, dv, name):
    nb, s, _ = proj.shape
    tm = min(256, s)

    def body(cq_ref, ckv_ref, kr_ref, c_ref, s_ref, gcq_ref, gckv_ref, gqn_ref, gkn_ref, wuq_ref, wukv_ref,
             dq_ref, dk_ref, dv_ref,
             dcq_ref, dckv_ref, dkr_ref, dwuq_ref, dwukv_ref, dgcq_ref, dgckv_ref, dgqn_ref, dgkn_ref,
             dqb_s, dkvb_s):
        @pl.when((pl.program_id(0) == 0) & (pl.program_id(1) == 0))
        def _():
            for r in (dwuq_ref, dwukv_ref, dgcq_ref, dgckv_ref, dgqn_ref, dgkn_ref):
                r[...] = jnp.zeros_like(r)

        lane = _iota((tm, LANES), 1)
        cv, sv = c_ref[...], s_ref[...]
        gqn, gkn = gqn_ref[...], gkn_ref[...]
        cq = cq_ref[...]
        rc = lax.rsqrt(jnp.mean(cq * cq, axis=-1, keepdims=True) + EPS)
        chat = cq * rc
        cqn = (chat * gcq_ref[...]).astype(MXU)
        qb = _dot_nt(cqn, wuq_ref[...])
        ckv = ckv_ref[...]
        rkv = lax.rsqrt(jnp.mean(ckv * ckv, axis=-1, keepdims=True) + EPS)
        kvhat = ckv * rkv
        ckvn = (kvhat * gckv_ref[...]).astype(MXU)
        kvb = _dot_nt(ckvn, wukv_ref[...])
        kr = kr_ref[...]
        dgq = jnp.zeros((1, LANES), F32)
        dgk = jnp.zeros((1, LANES), F32)
        dkr = jnp.zeros((tm, LANES), F32)
        for h in range(MLA_HEADS):
            hs = slice(h * LANES, (h + 1) * LANES)
            qh = qb[:, hs]
            rq = lax.rsqrt(_rsum(qh * qh) / MLA_QK + EPS)
            qhat = qh * rq
            dqr = dq_ref[:, hs]
            dqn = dqr * cv + _rot_t(dqr * sv, lane)
            dgq = dgq + _csum(dqn * qhat)
            dyq = dqn * gqn
            dqb_s[:, hs] = (rq * (dyq - qhat * (_rsum(dyq * qhat) / MLA_QK))).astype(MXU)

            kc = jnp.where(lane < HEAD, kvb[:, hs], kr)
            rk = lax.rsqrt(_rsum(kc * kc) / MLA_QK + EPS)
            khat = kc * rk
            dkr_h = dk_ref[:, hs]
            dkn = dkr_h * cv + _rot_t(dkr_h * sv, lane)
            dgk = dgk + _csum(dkn * khat)
            dyk = dkn * gkn
            dkc = rk * (dyk - khat * (_rsum(dyk * khat) / MLA_QK))
            dkr = dkr + jnp.where(lane >= HEAD, dkc, 0.0)
            dvb = dv_ref[:, (h // 2) * LANES:(h // 2 + 1) * LANES]
            dvp = dvb if h % 2 == 1 else pltpu.roll(dvb, HEAD, 1)
            dkvb_s[:, hs] = jnp.where(lane < HEAD, dkc, dvp).astype(MXU)
        dgqn_ref[...] += dgq
        dgkn_ref[...] += dgk
        dkr_ref[...] = dkr

        dqb = dqb_s[...]
        dwuq_ref[...] += _dot_tn(dqb, cqn)
        dcqn = _dot(dqb, wuq_ref[...])
        dgcq_ref[...] += _csum(dcqn * chat)
        dyc = dcqn * gcq_ref[...]
        dcq_ref[...] = rc * (dyc - chat * jnp.mean(dyc * chat, axis=-1, keepdims=True))

        dkvb = dkvb_s[...]
        dwukv_ref[...] += _dot_tn(dkvb, ckvn)
        dckvn = _dot(dkvb, wukv_ref[...])
        dgckv_ref[...] += _csum(dckvn * kvhat)
        dykv = dckvn * gckv_ref[...]
        dckv_ref[...] = rkv * (dykv - kvhat * jnp.mean(dykv * kvhat, axis=-1, keepdims=True))

    full = lambda shape: pl.BlockSpec(shape, lambda b, i: (0, 0))
    tile = lambda width: pl.BlockSpec((None, tm, width), lambda b, i: (b, i, 0))
    return pl.pallas_call(
        body, name=name, grid=(nb, s // tm),
        in_specs=_mla_prep_specs(s, tm) + [tile(768), tile(768), tile(384)],
        out_specs=[tile(256), tile(128), tile(128), full((768, 256)), full((768, 128)),
                   full((1, 256)), full((1, 128)), full((1, 128)), full((1, 128))],
        out_shape=[jax.ShapeDtypeStruct((nb, s, 256), F32), jax.ShapeDtypeStruct((nb, s, 128), F32),
                   jax.ShapeDtypeStruct((nb, s, 128), F32),
                   jax.ShapeDtypeStruct((768, 256), F32), jax.ShapeDtypeStruct((768, 128), F32),
                   jax.ShapeDtypeStruct((1, 256), F32), jax.ShapeDtypeStruct((1, 128), F32),
                   jax.ShapeDtypeStruct((1, 128), F32), jax.ShapeDtypeStruct((1, 128), F32)],
        scratch_shapes=[pltpu.VMEM((tm, 768), MXU), pltpu.VMEM((tm, 768), MXU)],
        compiler_params=_cp("arbitrary", "arbitrary"))(
            proj, proj, proj, cs, sn, gcq, gckv, gqn, gkn, wuq, wukv, dq, dk, dv)


def _softplus(z):
    return jnp.maximum(z, 0.0) + jnp.log(1.0 + jnp.exp(-jnp.abs(z)))


def _sb_fwd(proj, name):
    nb, s, _ = proj.shape
    tq, tk = min(256, s), 128
    ratio = tq // tk

    def body(q_ref, k_ref, v_ref, o_ref, ct_ref, cnt_ref):
        i = pl.program_id(2)
        lo = _iota((tq, LANES), 1) < HEAD
        qv = q_ref[...]
        q0 = jnp.where(lo, qv, 0.0).astype(MXU)
        q1 = jnp.where(lo, 0.0, qv).astype(MXU)
        usuf = (_iota((tk, tk), 0) > _iota((tk, tk), 1)).astype(MXU)
        tpos = i * tq + _iota((tq, tk), 0)
        scol = _iota((tq, tk), 1)
        nch = (i + 1) * ratio

        def alive(st):
            return (st[0] < nch) & (st[5] > SB_DEAD)

        def step(st):
            t, c0, a0, c1, a1, _ = st
            j = nch - 1 - t
            off = pl.multiple_of(j * tk, tk)
            kc = k_ref[pl.ds(off, tk), :].astype(MXU)
            vc = v_ref[pl.ds(off, tk), :].astype(MXU)
            msk = (scol + j * tk) < tpos

            def head(qm, c, a):
                z = _dot_nt(qm, kc) * SB_SCALE
                sp = _softplus(z)
                lk = jnp.where(msk, -sp, 0.0)
                w = jnp.where(msk, jnp.exp(z - sp + _cumdot(lk, usuf) + c), 0.0)
                return c + _rsum(lk), a + _dot(w.astype(MXU), vc)

            c0, a0 = head(q0, c0, a0)
            c1, a1 = head(q1, c1, a1)
            return t + 1, c0, a0, c1, a1, jnp.maximum(jnp.max(c0), jnp.max(c1))

        z1 = jnp.zeros((tq, 1), F32)
        za = jnp.zeros((tq, LANES), F32)
        t, c0, a0, c1, a1, _ = lax.while_loop(alive, step, (jnp.int32(0), z1, za, z1, za, jnp.float32(0.0)))
        o_ref[...] = jnp.where(lo, a0, a1)
        ct_ref[...] = jnp.where(lo, c0, c1)
        cnt_ref[...] = jnp.zeros((8, LANES), F32) + t.astype(F32)

    kv = lambda col: pl.BlockSpec((None, s, LANES), lambda b, p, i: (b, 0, col // LANES + p))
    tile = pl.BlockSpec((None, tq, LANES), lambda b, p, i: (b, i, p))
    return pl.pallas_call(
        body, name=name, grid=(nb, 2, s // tq),
        in_specs=[pl.BlockSpec((None, tq, LANES), lambda b, p, i: (b, i, COL_SBQ // LANES + p)),
                  kv(COL_SBK), kv(COL_SBV)],
        out_specs=[tile, tile, pl.BlockSpec((None, None, None, 8, LANES), lambda b, p, i: (b, p, i, 0, 0))],
        out_shape=[jax.ShapeDtypeStruct((nb, s, 256), F32)] * 2
        + [jax.ShapeDtypeStruct((nb, 2, s // tq, 8, LANES), F32)],
        compiler_params=_cp("parallel", "parallel", "arbitrary"))(proj, proj, proj)


def _sb_bwd(proj, ct, cnt, do, name):
    nb, s, _ = proj.shape
    tq, tk = min(256, s), 128
    ratio = tq // tk

    def body(q_ref, k_ref, v_ref, ct_ref, cnt_ref, do_ref, dq_ref, dk_ref, dv_ref):
        i = pl.program_id(2)

        @pl.when(i == 0)
        def _():
            dk_ref[...] = jnp.zeros_like(dk_ref)
            dv_ref[...] = jnp.zeros_like(dv_ref)

        lane = _iota((tq, LANES), 1)
        lo = lane < HEAD
        lok = _iota((tk, LANES), 1) < HEAD
        qv, dov = q_ref[...], do_ref[...]
        qb, dob = qv.astype(MXU), dov.astype(MXU)
        q0 = jnp.where(lo, qv, 0.0).astype(MXU)
        q1 = jnp.where(lo, 0.0, qv).astype(MXU)
        do0 = jnp.where(lo, dov, 0.0).astype(MXU)
        do1 = jnp.where(lo, 0.0, dov).astype(MXU)
        ctv = ct_ref[...]
        ct0 = _rsum(jnp.where(lane == 0, ctv, 0.0))
        ct1 = _rsum(jnp.where(lane == LANES - 1, ctv, 0.0))
        uincl = (_iota((tk, tk), 0) <= _iota((tk, tk), 1)).astype(MXU)
        ustrict = (_iota((tk, tk), 0) < _iota((tk, tk), 1)).astype(MXU)
        tpos = i * tq + _iota((tq, tk), 0)
        scol = _iota((tq, tk), 1)
        nch = (i + 1) * ratio

        def step(j, carry):
            p0, g0, dq0, p1, g1, dq1 = carry
            off = pl.multiple_of(j * tk, tk)
            kc = k_ref[pl.ds(off, tk), :].astype(MXU)
            vc = v_ref[pl.ds(off, tk), :].astype(MXU)
            msk = (scol + j * tk) < tpos

            def head(qm, dom, ctot, pc, gc, dqa):
                z = _dot_nt(qm, kc) * SB_SCALE
                sp = _softplus(z)
                lk = jnp.where(msk, -sp, 0.0)
                lsig = z - sp
                w = jnp.where(msk, jnp.exp(lsig + (ctot - pc - _cumdot(lk, uincl))), 0.0)
                g = w * _dot_nt(dom, vc)
                gpre = gc + _cumdot(g, ustrict)
                sig = jnp.exp(lsig)
                dz = (jnp.where(msk, g * (1.0 - sig) - sig * gpre, 0.0) * SB_SCALE).astype(MXU)
                return (pc + _rsum(lk), gc + _rsum(g), dqa + _dot(dz, kc),
                        _dot_tn(dz, qb), _dot_tn(w.astype(MXU), dob))

            p0, g0, dq0, dk0, dv0 = head(q0, do0, ct0, p0, g0, dq0)
            p1, g1, dq1, dk1, dv1 = head(q1, do1, ct1, p1, g1, dq1)
            dk_ref[pl.ds(off, tk), :] += jnp.where(lok, dk0, dk1)
            dv_ref[pl.ds(off, tk), :] += jnp.where(lok, dv0, dv1)
            return p0, g0, dq0, p1, g1, dq1

        z1 = jnp.zeros((tq, 1), F32)
        za = jnp.zeros((tq, LANES), F32)
        first = nch - jnp.max(cnt_ref[...]).astype(jnp.int32)
        _, _, dq0, _, _, dq1 = lax.fori_loop(first, nch, step, (z1, z1, za, z1, z1, za))
        dq_ref[...] = jnp.where(lo, dq0, dq1)

    kv = lambda col: pl.BlockSpec((None, s, LANES), lambda b, p, i: (b, 0, col // LANES + p))
    tile = pl.BlockSpec((None, tq, LANES), lambda b, p, i: (b, i, p))
    acc = pl.BlockSpec((None, s, LANES), lambda b, p, i: (b, 0, p))
    return pl.pallas_call(
        body, name=name, grid=(nb, 2, s // tq),
        in_specs=[pl.BlockSpec((None, tq, LANES), lambda b, p, i: (b, i, COL_SBQ // LANES + p)),
                  kv(COL_SBK), kv(COL_SBV), tile,
                  pl.BlockSpec((None, None, None, 8, LANES), lambda b, p, i: (b, p, i, 0, 0)), tile],
        out_specs=[tile, acc, acc],
        out_shape=[jax.ShapeDtypeStruct((nb, s, 256), F32)] * 3,
        compiler_params=_cp("parallel", "parallel", "arbitrary"))(proj, proj, proj, ct, cnt, do)


def _mla_fwd(q, k, v, name):
    nb, s, _ = q.shape
    tq = tk = min(256, s)

    def body(q_ref, k_ref, v_ref, o_ref, lse_ref):
        i = pl.program_id(2)
        lo = _iota((tq, LANES), 1) < HEAD
        q0, q1 = q_ref[:, :LANES], q_ref[:, LANES:]
        tpos = i * tq + _iota((tq, tk), 0)
        scol = _iota((tq, tk), 1)

        def step(j, carry):
            m0, l0, a0, m1, l1, a1 = carry
            off = pl.multiple_of(j * tk, tk)
            vc = v_ref[pl.ds(off, tk), :]
            msk = (scol + j * tk) <= tpos

            def head(qh, kh, m, l, a):
                sc = jnp.where(msk, _dot_nt(qh, kh) * MLA_SCALE, NEG)
                mn = jnp.maximum(m, jnp.max(sc, axis=-1, keepdims=True))
                al = jnp.exp(m - mn)
                p = jnp.exp(sc - mn)
                return mn, al * l + _rsum(p), al * a + _dot(p.astype(MXU), vc)

            m0, l0, a0 = head(q0, k_ref[pl.ds(off, tk), :LANES], m0, l0, a0)
            m1, l1, a1 = head(q1, k_ref[pl.ds(off, tk), LANES:], m1, l1, a1)
            return m0, l0, a0, m1, l1, a1

        mi = jnp.full((tq, 1), NEG, F32)
        z1 = jnp.zeros((tq, 1), F32)
        za = jnp.zeros((tq, LANES), F32)
        m0, l0, a0, m1, l1, a1 = lax.fori_loop(0, i + 1, step, (mi, z1, za, mi, z1, za))
        o_ref[...] = jnp.where(lo, a0 / l0, a1 / l1)
        lse_ref[...] = jnp.where(lo, m0 + jnp.log(l0), m1 + jnp.log(l1))

    tile = pl.BlockSpec((None, tq, LANES), lambda b, p, i: (b, i, p))
    return pl.pallas_call(
        body, name=name, grid=(nb, MLA_HEADS // 2, s // tq),
        in_specs=[pl.BlockSpec((None, tq, 2 * LANES), lambda b, p, i: (b, i, p)),
                  pl.BlockSpec((None, s, 2 * LANES), lambda b, p, i: (b, 0, p)),
                  pl.BlockSpec((None, s, LANES), lambda b, p, i: (b, 0, p))],
        out_specs=[tile, tile],
        out_shape=[jax.ShapeDtypeStruct((nb, s, 384), F32)] * 2,
        compiler_params=_cp("parallel", "parallel", "arbitrary"))(q, k, v)


def _mla_bwd(q, k, v, o, lse, do, name):
    nb, s, _ = q.shape
    tq = tk = min(256, s)

    def body(q_ref, k_ref, v_ref, o_ref, lse_ref, do_ref, dq_ref, dk_ref, dv_ref):
        i = pl.program_id(2)

        @pl.when(i == 0)
        def _():
            dk_ref[...] = jnp.zeros_like(dk_ref)
            dv_ref[...] = jnp.zeros_like(dv_ref)

        lane = _iota((tq, LANES), 1)
        lo = lane < HEAD
        lok = _iota((tk, LANES), 1) < HEAD
        q0, q1 = q_ref[:, :LANES], q_ref[:, LANES:]
        dov = do_ref[...]
        dob = dov.astype(MXU)
        do0 = jnp.where(lo, dov, 0.0).astype(MXU)
        do1 = jnp.where(lo, 0.0, dov).astype(MXU)
        dd = dov * o_ref[...]
        dl0 = _rsum(jnp.where(lo, dd, 0.0))
        dl1 = _rsum(jnp.where(lo, 0.0, dd))
        lsev = lse_ref[...]
        ls0 = _rsum(jnp.where(lane == 0, lsev, 0.0))
        ls1 = _rsum(jnp.where(lane == LANES - 1, lsev, 0.0))
        tpos = i * tq + _iota((tq, tk), 0)
        scol = _iota((tq, tk), 1)

        def step(j, carry):
            dq0, dq1 = carry
            off = pl.multiple_of(j * tk, tk)
            vc = v_ref[pl.ds(off, tk), :]
            msk = (scol + j * tk) <= tpos

            def head(qh, kh, dom, ls, dl, dqa):
                sc = jnp.where(msk, _dot_nt(qh, kh) * MLA_SCALE, NEG)
                p = jnp.exp(sc - ls)
                ds = (p * (_dot_nt(dom, vc) - dl) * MLA_SCALE).astype(MXU)
                return dqa + _dot(ds, kh), _dot_tn(ds, qh), _dot_tn(p.astype(MXU), dob)

            dq0, dk0, dv0 = head(q0, k_ref[pl.ds(off, tk), :LANES], do0, ls0, dl0, dq0)
            dq1, dk1, dv1 = head(q1, k_ref[pl.ds(off, tk), LANES:], do1, ls1, dl1, dq1)
            dk_ref[pl.ds(off, tk), :LANES] += dk0
            dk_ref[pl.ds(off, tk), LANES:] += dk1
            dv_ref[pl.ds(off, tk), :] += jnp.where(lok, dv0, dv1)
            return dq0, dq1

        za = jnp.zeros((tq, LANES), F32)
        dq0, dq1 = lax.fori_loop(0, i + 1, step, (za, za))
        dq_ref[:, :LANES] = dq0
        dq_ref[:, LANES:] = dq1

    tile = pl.BlockSpec((None, tq, LANES), lambda b, p, i: (b, i, p))
    tile2 = pl.BlockSpec((None, tq, 2 * LANES), lambda b, p, i: (b, i, p))
    return pl.pallas_call(
        body, name=name, grid=(nb, MLA_HEADS // 2, s // tq),
        in_specs=[tile2,
                  pl.BlockSpec((None, s, 2 * LANES), lambda b, p, i: (b, 0, p)),
                  pl.BlockSpec((None, s, LANES), lambda b, p, i: (b, 0, p)),
                  tile, tile, tile],
        out_specs=[tile2,
                   pl.BlockSpec((None, s, 2 * LANES), lambda b, p, i: (b, 0, p)),
                   pl.BlockSpec((None, s, LANES), lambda b, p, i: (b, 0, p))],
        out_shape=[jax.ShapeDtypeStruct((nb, s, 768), F32), jax.ShapeDtypeStruct((nb, s, 768), F32),
                   jax.ShapeDtypeStruct((nb, s, 384), F32)],
        compiler_params=_cp("parallel", "parallel", "arbitrary"))(q, k, v, o, lse, do)


def _half_stats(xv, lo):
    x2 = xv * xv
    s0 = _rsum(jnp.where(lo, x2, 0.0))
    s1 = _rsum(jnp.where(lo, 0.0, x2))
    return jnp.where(lo, lax.rsqrt(s0 / HEAD + EPS), lax.rsqrt(s1 / HEAD + EPS))


def _half_mean(xv, lo):
    s0 = _rsum(jnp.where(lo, xv, 0.0))
    s1 = _rsum(jnp.where(lo, 0.0, xv))
    return jnp.where(lo, s0, s1) / HEAD


def _swa_in_specs():
    def band(col, prev):
        if prev:
            return pl.BlockSpec((None, BLOCK, LANES), lambda b, n: (b, jnp.maximum(n - 1, 0), col // LANES))
        return pl.BlockSpec((None, BLOCK, LANES), lambda b, n: (b, n, col // LANES))

    full = lambda shape: pl.BlockSpec(shape, lambda b, n: tuple(0 for _ in shape))
    return [pl.BlockSpec((None, BLOCK, 384), lambda b, n: (b, n, COL_SWQ // 384)),
            band(COL_SWK, False), band(COL_SWK, True), band(COL_SWV, False), band(COL_SWV, True),
            full((1, LANES)), full((1, LANES)), full((8, LANES)), full((SW_HEADS, BLOCK, 2 * BLOCK))]


def _swa_valid(n):
    a = _iota((BLOCK, 2 * BLOCK), 0)
    bcol = _iota((BLOCK, 2 * BLOCK), 1)
    dist = BLOCK + a - bcol
    return (dist >= 0) & (dist < BLOCK) & ((n > 0) | (bcol >= BLOCK))


def _swa_fwd(proj, gq, gk, sinks, bias, name):
    nb, s, _ = proj.shape

    def body(q_ref, kc_ref, kp_ref, vc_ref, vp_ref, gq_ref, gk_ref, sk_ref, bias_ref, o_ref):
        n = pl.program_id(1)
        lo = _iota((BLOCK, LANES), 1) < HEAD
        lo2 = _iota((2 * BLOCK, LANES), 1) < HEAD
        kband = jnp.concatenate([kp_ref[...], kc_ref[...]], axis=0)
        vband = jnp.concatenate([vp_ref[...], vc_ref[...]], axis=0)
        kn = kband * _half_stats(kband, lo2) * gk_ref[...]
        ks = (kn.astype(MXU), pltpu.roll(kn, HEAD, 1).astype(MXU))
        vs = (vband.astype(MXU), pltpu.roll(vband, HEAD, 1).astype(MXU))
        valid = _swa_valid(n)
        for blk in range(SW_HEADS // 2):
            qv = q_ref[:, blk * LANES:(blk + 1) * LANES]
            qn = qv * _half_stats(qv, lo) * gq_ref[...]
            outs = []
            for half in range(2):
                h = 2 * blk + half
                swap = 0 if half == h // 3 else 1
                qm = jnp.where(lo if half == 0 else ~lo, qn, 0.0).astype(MXU)
                sc = jnp.where(valid, _dot_nt(qm, ks[swap]) * SW_SCALE + bias_ref[h], NEG)
                sk = jnp.max(sk_ref[h:h + 1, :], axis=-1, keepdims=True)
                m = jnp.maximum(jnp.max(sc, axis=-1, keepdims=True), sk)
                p = jnp.exp(sc - m)
                l = _rsum(p) + jnp.exp(sk - m)
                outs.append(_dot((p / l).astype(MXU), vs[swap]))
            o_ref[:, blk * LANES:(blk + 1) * LANES] = jnp.where(lo, outs[0], outs[1])

    return pl.pallas_call(
        body, name=name, grid=(nb, s // BLOCK), in_specs=_swa_in_specs(),
        out_specs=pl.BlockSpec((None, BLOCK, 384), lambda b, n: (b, n, 0)),
        out_shape=jax.ShapeDtypeStruct((nb, s, 384), F32),
        compiler_params=_cp("parallel", "parallel"))(proj, proj, proj, proj, proj, gq, gk, sinks, bias)


def _swa_bwd(proj, gq, gk, sinks, bias, do, name):
    nb, s, _ = proj.shape

    def body(q_ref, kc_ref, kp_ref, vc_ref, vp_ref, gq_ref, gk_ref, sk_ref, bias_ref, do_ref,
             dq_ref, dkc_ref, dkp_ref, dvc_ref, dvp_ref, dbias_ref, dsk_ref, dgq_ref, dgk_ref):
        n = pl.program_id(1)

        @pl.when((pl.program_id(0) == 0) & (n == 0))
        def _():
            for r in (dbias_ref, dsk_ref, dgq_ref, dgk_ref):
                r[...] = jnp.zeros_like(r)

        lo = _iota((BLOCK, LANES), 1) < HEAD
        lo2 = _iota((2 * BLOCK, LANES), 1) < HEAD
        kband = jnp.concatenate([kp_ref[...], kc_ref[...]], axis=0)
        vband = jnp.concatenate([vp_ref[...], vc_ref[...]], axis=0)
        rk = _half_stats(kband, lo2)
        khat = kband * rk
        gkv = gk_ref[...]
        kn = khat * gkv
        ks = (kn.astype(MXU), pltpu.roll(kn, HEAD, 1).astype(MXU))
        vs = (vband.astype(MXU), pltpu.roll(vband, HEAD, 1).astype(MXU))
        valid = _swa_valid(n)
        dkn = jnp.zeros((2 * BLOCK, LANES), F32)
        dvb = jnp.zeros((2 * BLOCK, LANES), F32)
        gqv = gq_ref[...]
        dgq = jnp.zeros((1, LANES), F32)
        for blk in range(SW_HEADS // 2):
            bs = slice(blk * LANES, (blk + 1) * LANES)
            qv = q_ref[:, bs]
            rq = _half_stats(qv, lo)
            qhat = qv * rq
            qn = qhat * gqv
            dov = do_ref[:, bs]
            dqn = jnp.zeros((BLOCK, LANES), F32)
            for half in range(2):
                h = 2 * blk + half
                swap = 0 if half == h // 3 else 1
                hm = lo if half == 0 else ~lo
                qm = jnp.where(hm, qn, 0.0).astype(MXU)
                dom = jnp.where(hm, dov, 0.0).astype(MXU)
                sc = jnp.where(valid, _dot_nt(qm, ks[swap]) * SW_SCALE + bias_ref[h], NEG)
                sk = jnp.max(sk_ref[h:h + 1, :], axis=-1, keepdims=True)
                m = jnp.maximum(jnp.max(sc, axis=-1, keepdims=True), sk)
                e = jnp.exp(sc - m)
                es = jnp.exp(sk - m)
                l = _rsum(e) + es
                p = e / l
                dp = _dot_nt(dom, vs[swap])
                delta = _rsum(p * dp)
                ds = p * (dp - delta)
                dsk_ref[h:h + 1, :] += jnp.broadcast_to(_csum(-(es / l) * delta), (1, LANES))
                dbias_ref[h] += ds
                dsb = (ds * SW_SCALE).astype(MXU)
                dqn = dqn + jnp.where(hm, _dot(dsb, ks[swap]), 0.0)
                rk_ = _dot_tn(dsb, qm)
                rv_ = _dot_tn(p.astype(MXU), dom)
                if swap:
                    rk_ = pltpu.roll(rk_, HEAD, 1)
                    rv_ = pltpu.roll(rv_, HEAD, 1)
                dkn = dkn + rk_
                dvb = dvb + rv_
            dgq = dgq + _csum(dqn * qhat)
            dyq = dqn * gqv
            dq_ref[:, bs] = rq * (dyq - qhat * _half_mean(dyq * qhat, lo))
        dgq_ref[...] += dgq
        dgk_ref[...] += _csum(dkn * khat)
        dyk = dkn * gkv
        dkb = rk * (dyk - khat * _half_mean(dyk * khat, lo2))
        dkp_ref[...] = dkb[:BLOCK]
        dkc_ref[...] = dkb[BLOCK:]
        dvp_ref[...] = dvb[:BLOCK]
        dvc_ref[...] = dvb[BLOCK:]

    full = lambda shape: pl.BlockSpec(shape, lambda b, n: tuple(0 for _ in shape))
    tile = pl.BlockSpec((None, BLOCK, LANES), lambda b, n: (b, n, 0))
    tile3 = pl.BlockSpec((None, BLOCK, 384), lambda b, n: (b, n, 0))
    kvs = jax.ShapeDtypeStruct((nb, s, LANES), F32)
    return pl.pallas_call(
        body, name=name, grid=(nb, s // BLOCK), in_specs=_swa_in_specs() + [tile3],
        out_specs=[tile3, tile, tile, tile, tile, full((SW_HEADS, BLOCK, 2 * BLOCK)), full((8, LANES)),
                   full((1, LANES)), full((1, LANES))],
        out_shape=[jax.ShapeDtypeStruct((nb, s, 384), F32), kvs, kvs, kvs, kvs,
                   jax.ShapeDtypeStruct((SW_HEADS, BLOCK, 2 * BLOCK), F32), jax.ShapeDtypeStruct((8, LANES), F32),
                   jax.ShapeDtypeStruct((1, LANES), F32), jax.ShapeDtypeStruct((1, LANES), F32)],
        compiler_params=_cp("arbitrary", "arbitrary"))(proj, proj, proj, proj, proj, gq, gk, sinks, bias, do)


def _bias_build(table, bucket, name):
    def body(tb_ref, bk_ref, o_ref):
        bk = bk_ref[...]
        tb = tb_ref[...]
        row = _iota((8, LANES), 0)
        col = _iota((8, LANES), 1)
        for h in range(SW_HEADS):
            acc = jnp.zeros((BLOCK, 2 * BLOCK), F32)
            for t in range(REL_BUCKETS):
                val = jnp.sum(jnp.where((row == h) & (col == t), tb, 0.0), keepdims=True)
                acc = jnp.where(bk == t, val, acc)
            o_ref[h] = acc

    return pl.pallas_call(
        body, name=name, out_shape=jax.ShapeDtypeStruct((SW_HEADS, BLOCK, 2 * BLOCK), F32))(table, bucket)


def _bias_grad(dbias, bucket, name):
    def body(db_ref, bk_ref, o_ref):
        bk = bk_ref[...]
        row = _iota((8, LANES), 0)
        col = _iota((8, LANES), 1)
        res = jnp.zeros((8, LANES), F32)
        for h in range(SW_HEADS):
            dbh = db_ref[h]
            for t in range(REL_BUCKETS):
                val = jnp.sum(jnp.where(bk == t, dbh, 0.0), keepdims=True)
                res = jnp.where((row == h) & (col == t), val, res)
        o_ref[...] = res

    return pl.pallas_call(body, name=name, out_shape=jax.ShapeDtypeStruct((8, LANES), F32))(dbias, bucket)


def _loss_grad(y, target, name):
    nb, s, d = y.shape
    tm = min(512, s)

    def body(y_ref, t_ref, loss_ref, dy_ref):
        @pl.when((pl.program_id(0) == 0) & (pl.program_id(1) == 0))
        def _():
            loss_ref[...] = jnp.zeros_like(loss_ref)

        e = y_ref[...] - t_ref[...]
        dy_ref[...] = e / d
        loss_ref[...] += 0.5 * jnp.sum(_rsum(e * e) / d, keepdims=True)

    tile = pl.BlockSpec((None, tm, d), lambda b, i: (b, i, 0))
    return pl.pallas_call(
        body, name=name, grid=(nb, s // tm), in_specs=[tile, tile],
        out_specs=[pl.BlockSpec((8, LANES), lambda b, i: (0, 0)), tile],
        out_shape=[jax.ShapeDtypeStruct((8, LANES), F32), jax.ShapeDtypeStruct((nb, s, d), F32)],
        compiler_params=_cp("arbitrary", "arbitrary"))(y, target)


def _adamw(parts, w, m, v, name):
    npart, r, ncol = parts.shape
    tr = _row_tile(r, ncol)
    bc1 = 1.0 - ADAM_B1 ** ADAM_STEP
    bc2 = 1.0 - ADAM_B2 ** ADAM_STEP

    def body(p_ref, w_ref, m_ref, v_ref, g_ref, d_ref, nm_ref, nv_ref):
        g = p_ref[0].astype(F32)
        for k in range(1, npart):
            g = g + p_ref[k].astype(F32)
        mn = ADAM_B1 * m_ref[...] + (1.0 - ADAM_B1) * g
        vn = ADAM_B2 * v_ref[...] + (1.0 - ADAM_B2) * (g * g)
        g_ref[...] = g
        nm_ref[...] = mn
        nv_ref[...] = vn
        d_ref[...] = -ADAM_LR * ((mn / bc1) / (jnp.sqrt(vn / bc2) + ADAM_EPS) + ADAM_WD * w_ref[...])

    tile = pl.BlockSpec((tr, ncol), lambda i: (i, 0))
    return pl.pallas_call(
        body, name=name, grid=(r // tr,),
        in_specs=[pl.BlockSpec((npart, tr, ncol), lambda i: (0, i, 0)), tile, tile, tile],
        out_specs=[tile] * 4, out_shape=[jax.ShapeDtypeStruct((r, ncol), F32)] * 4,
        compiler_params=_cp("parallel"))(parts, w, m, v)


def _unpack(flat, shapes, lead=()):
    out, off = [], 0
    for shp in shapes:
        size = 1
        for dim in shp:
            size *= dim
        out.append(flat[..., off:off + size].reshape(lead + tuple(shp)))
        off += size
    return out


def _t5_bucket():
    a = jnp.arange(BLOCK)[:, None]
    b = jnp.arange(2 * BLOCK)[None, :]
    dist = BLOCK + a - b
    max_exact = REL_BUCKETS // 2
    nn = jnp.maximum(dist, 0)
    nf = jnp.maximum(nn, 1).astype(F32)
    large = max_exact + (jnp.log(nf / max_exact) / math.log(BLOCK / max_exact)
                         * (REL_BUCKETS - max_exact)).astype(jnp.int32)
    large = jnp.minimum(large, REL_BUCKETS - 1)
    return jnp.where(nn < max_exact, nn, large).astype(jnp.int32)


def _pad_lanes(g, n):
    return jnp.pad(g, (0, n - g.shape[0])).reshape(1, n)


def kernel(x, c, positions, rel_table, norm1_g, norm2_g, w_ada, b_ada, w_in, mla_cq_g, w_uq, mla_ckv_g, w_ukv, mla_qn_g, mla_kn_g, sw_qn_g, sw_kn_g, sw_sinks, w_out, w_up, conv_w, conv_b, w_down, loss_target, m_rel_table, m_norm1_g, m_norm2_g, m_w_ada, m_b_ada, m_w_in, m_mla_cq_g, m_w_uq, m_mla_ckv_g, m_w_ukv, m_mla_qn_g, m_mla_kn_g, m_sw_qn_g, m_sw_kn_g, m_sw_sinks, m_w_out, m_w_up, m_conv_w, m_conv_b, m_w_down, v_rel_table, v_norm1_g, v_norm2_g, v_w_ada, v_b_ada, v_w_in, v_mla_cq_g, v_w_uq, v_mla_ckv_g, v_w_ukv, v_mla_qn_g, v_mla_kn_g, v_sw_qn_g, v_sw_kn_g, v_sw_sinks, v_w_out, v_w_up, v_conv_w, v_conv_b, v_w_down):
    nb, s, d = x.shape
    nl = norm1_g.shape[0]
    me = 4 * lax.axis_index("x") + 2 * lax.axis_index("y") + lax.axis_index("c")
    n_ada = w_ada.shape[2]

    rows2d = lambda a: a.reshape(-1, a.shape[-1])
    tr = lambda a: jnp.swapaxes(a, -1, -2)
    local = [rows2d(tr(w).astype(MXU)) for w in (w_in, w_uq, w_ukv, w_up)]
    local += [rows2d(w.astype(MXU)) for w in (w_out, w_down)] + [rows2d(conv_w), c]
    got = _all_gather(local, "gather_inputs")
    stack_rows = lambda a: a.reshape(N_DEV, nl, -1, a.shape[-1]).transpose(1, 0, 2, 3).reshape(nl, -1, a.shape[-1])
    w_in_t, w_uq_t, w_ukv_t, w_up_t, w_out_f, w_down_f = [stack_rows(a) for a in got[:6]]
    conv_full = got[6].reshape(N_DEV, nl, 3, -1).transpose(1, 2, 0, 3).reshape(nl, 3, -1)
    c_all = got[7].reshape(N_DEV * nb, d)
    zrows = lambda n: jnp.zeros((nl, n, d), MXU)
    w_in_pt = jnp.concatenate([w_in_t[:, :1152], w_in_t[:, 1184:1824], zrows(64), w_in_t[:, 1152:1184], zrows(160)], axis=1)
    w_uq_pt = jnp.pad(w_uq_t.reshape(nl, MLA_HEADS, MLA_QK, 256), ((0, 0), (0, 0), (0, LANES - MLA_QK), (0, 0))).reshape(nl, 768, 256)

    b_my = lax.dynamic_slice_in_dim(b_ada, me * n_ada, n_ada, axis=1).reshape(nl, 1, n_ada)
    mods_my = _ada_fwd(c_all, w_ada, b_my, "ada_fwd")
    mods, = _all_gather([mods_my.reshape(nl * N_DEV * nb, n_ada)], "gather_mods")
    mods = mods.reshape(N_DEV, nl, N_DEV * nb, n_ada).transpose(1, 2, 0, 3).reshape(nl, N_DEV * nb, N_DEV * n_ada)
    mods = lax.dynamic_slice_in_dim(mods, me * nb, nb, axis=1)
    shift1, scale1, gate1, shift2, scale2, gate2 = [mods[:, :, k * d:(k + 1) * d].reshape(nl, nb, 1, d) for k in range(6)]

    half = 16
    inv_freq = jnp.power(ROPE_THETA, -jnp.arange(half, dtype=F32) / half)
    ang = positions.astype(F32)[..., None] * inv_freq
    ones = lambda n: jnp.ones((nb, s, n), F32)
    zeros = lambda n: jnp.zeros((nb, s, n), F32)
    rope_c = jnp.concatenate([ones(64), jnp.cos(ang), jnp.cos(ang), ones(32)], axis=-1)
    rope_s = jnp.concatenate([zeros(64), jnp.sin(ang), jnp.sin(ang), zeros(32)], axis=-1)
    bucket = _t5_bucket()
    bias = _bias_build(jnp.pad(rel_table.T, ((0, 8 - SW_HEADS), (0, LANES - REL_BUCKETS))), bucket, "rel_bias")

    row = lambda g: g.reshape(1, -1)
    twice = lambda g: jnp.concatenate([g, g]).reshape(1, LANES)

    saved = []
    xl = x
    for l in range(nl):
        proj, h1 = _ln_mod_matmul(xl, row(norm1_g[l]), scale1[l], shift1[l], w_in_pt[l], f"l{l}_in_proj")
        prep_args = (proj, rope_c, rope_s, row(mla_cq_g[l]), row(mla_ckv_g[l]), _pad_lanes(mla_qn_g[l], LANES),
                     _pad_lanes(mla_kn_g[l], LANES), w_uq_pt[l], w_ukv_t[l])
        qm, km, vm = _mla_prep(*prep_args, f"l{l}_mla_prep")
        o_a, ct_a, cnt_a = _sb_fwd(proj, f"l{l}_sb_fwd")
        o_b, lse_b = _mla_fwd(qm, km, vm, f"l{l}_mla_fwd")
        sinks = jnp.broadcast_to(jnp.pad(sw_sinks[l], (0, 2))[:, None], (8, LANES))
        swa_args = (proj, twice(sw_qn_g[l]), twice(sw_kn_g[l]), sinks, bias)
        o_c = _swa_fwd(*swa_args, f"l{l}_swa_fwd")
        wo = [w_out_f[l, :256], w_out_f[l, 256:640], w_out_f[l, 640:]]
        x_mid, y1 = _out_proj([o_a, o_b, o_c], wo, gate1[l], xl, f"l{l}_out_proj")
        u_pre, h2 = _ln_mod_matmul(x_mid, row(norm2_g[l]), scale2[l], shift2[l], w_up_t[l], f"l{l}_up_proj")
        x_out, y2 = _conv_gate_matmul(u_pre, conv_full[l], row(conv_b[l]), w_down_f[l], gate2[l], x_mid, f"l{l}_ffn_down")
        saved.append(dict(x=xl, proj=proj, h1=h1, prep=prep_args, qkv=(qm, km, vm), o_a=o_a, ct_a=ct_a, cnt_a=cnt_a, o_b=o_b, lse_b=lse_b,
                          swa=swa_args, o_c=o_c, wo=wo, y1=y1, x_mid=x_mid, u_pre=u_pre, h2=h2, y2=y2))
        xl = x_out

    loss_blk, dx = _loss_grad(xl, loss_target, "loss")
    loss = lax.psum(loss_blk[0, 0], ("x", "y", "c"))

    t = nb * s
    flat = lambda a: a.reshape(t, a.shape[-1])
    grads = [None] * nl
    dmods = [None] * nl
    sharded_out = [None] * nl
    sharded_names = ["w_in", "w_uq", "w_ukv", "w_up", "w_out", "w_down", "conv_w"]
    sharded_wmv = dict(w_in=(w_in, m_w_in, v_w_in), w_uq=(w_uq, m_w_uq, v_w_uq), w_ukv=(w_ukv, m_w_ukv, v_w_ukv),
                       w_up=(w_up, m_w_up, v_w_up), w_out=(w_out, m_w_out, v_w_out), w_down=(w_down, m_w_down, v_w_down),
                       conv_w=(conv_w, m_conv_w, v_conv_w))
    n_in, n_up, n_out, n_dn = w_in.shape[2], w_up.shape[2], w_out.shape[1], w_down.shape[1]
    small_sizes = [w_uq[0].size, w_ukv[0].size, conv_w[0].size]
    n_small_rows = -(-sum(small_sizes) // d)
    rows_used = n_in + n_out + n_small_rows
    rows_grad = -(-rows_used // 16) * 16

    def pack_rows(mats, vecs):
        lead = mats[0].shape[:-2]
        flat_part = jnp.concatenate(vecs, axis=-1)
        flat_part = jnp.pad(flat_part, [(0, 0)] * len(lead) + [(0, n_small_rows * d - flat_part.shape[-1])])
        tail = jnp.zeros(lead + (rows_grad - rows_used, d), F32)
        return jnp.concatenate(list(mats) + [flat_part.reshape(lead + (n_small_rows, d)), tail], axis=-2)

    def unpack_rows(a):
        o1, o2 = n_in, n_in + n_out
        flat_part = a[o2:o2 + n_small_rows].reshape(-1)
        s1, s2, s3 = small_sizes[0], small_sizes[0] + small_sizes[1], sum(small_sizes)
        return dict(w_in=a[:o1].T, w_out=a[o1:o2],
                    w_uq=flat_part[:s1].reshape(w_uq.shape[2], -1).T, w_ukv=flat_part[s1:s2].reshape(w_ukv.shape[2], -1).T,
                    conv_w=flat_part[s2:s3].reshape(conv_w.shape[1:]))

    dbias = jnp.zeros((SW_HEADS, BLOCK, 2 * BLOCK), F32)
    for l in reversed(range(nl)):
        sv = saved[l]
        (da,), dy2, dgate2 = _gate_bwd_nt(dx, sv["y2"], gate2[l], [w_down_f[l]], f"l{l}_ffn_down_bwd")
        du, a_act, cstats = _conv_gate_bwd(da, sv["u_pre"], conv_full[l], row(conv_b[l]), f"l{l}_conv_gate_bwd")
        dx_mid, du_pre, dshift2, dscale2, dg2 = _ln_mod_matmul_bwd(
            du, w_up_t[l], sv["x_mid"], row(norm2_g[l]), scale2[l], dx, conv_full[l], f"l{l}_up_proj_bwd")
        g_w_down = _wgrad(flat(a_act), flat(dy2), f"l{l}_w_down_grad")
        g_w_up_t = _wgrad(flat(du_pre), flat(sv["h2"]), f"l{l}_w_up_grad")

        (do_a, do_b, do_c), dy1, dgate1 = _gate_bwd_nt(dx_mid, sv["y1"], gate1[l], sv["wo"], f"l{l}_out_proj_bwd")
        mix = jnp.concatenate([sv["o_a"], sv["o_b"], sv["o_c"]], axis=-1).astype(MXU)
        g_w_out = _wgrad(flat(mix), flat(dy1), f"l{l}_w_out_grad")

        dsb_q, dsb_k, dsb_v = _sb_bwd(sv["proj"], sv["ct_a"], sv["cnt_a"], do_a, f"l{l}_sb_bwd")
        qm, km, vm = sv["qkv"]
        dqm, dkm, dvm = _mla_bwd(qm, km, vm, sv["o_b"], sv["lse_b"], do_b, f"l{l}_mla_bwd")
        dsw_q, dkc, dkp, dvc, dvp, dbias_l, dsinks, dg_swq, dg_swk = _swa_bwd(*sv["swa"], do_c, f"l{l}_swa_bwd")
        dbias = dbias + dbias_l
        shift_up = lambda a: jnp.concatenate([a[:, BLOCK:], jnp.zeros((nb, BLOCK, LANES), F32)], axis=1)
        dsw_k = dkc + shift_up(dkp)
        dsw_v = dvc + shift_up(dvp)
        dcq, dckv, dkr, g_w_uq_pt, g_w_ukv_t, dg_cq, dg_ckv, dg_qn, dg_kn = _mla_prep_bwd(
            *sv["prep"], dqm, dkm, dvm, f"l{l}_mla_prep_bwd")
        dproj = jnp.concatenate([dsb_q, dsb_k, dsb_v, dcq, dckv, dsw_q, dsw_k, dsw_v, dkr, zeros(128)], axis=-1)
        dx, dproj_m, dshift1, dscale1, dg1 = _ln_mod_matmul_bwd(
            dproj, w_in_pt[l], sv["x"], row(norm1_g[l]), scale1[l], dx_mid, None, f"l{l}_in_proj_bwd")
        g_w_in_pt = _wgrad(flat(dproj_m), flat(sv["h1"]), f"l{l}_w_in_grad")

        g_w_in_t = jnp.concatenate([g_w_in_pt[:1152], g_w_in_pt[1856:1888], g_w_in_pt[1152:1792]], axis=0)
        g_w_uq_t = g_w_uq_pt.reshape(MLA_HEADS, LANES, 256)[:, :MLA_QK].reshape(MLA_HEADS * MLA_QK, 256)
        dmods[l] = jnp.concatenate([dshift1, dscale1, dgate1, dshift2, dscale2, dgate2], axis=-1).reshape(nb, 6 * d)

        per_dev = lambda g: g.reshape(N_DEV, -1, d)
        conv_dev = cstats[1:4].reshape(3, N_DEV, -1).transpose(1, 0, 2)
        rest = pack_rows([per_dev(g_w_in_t), per_dev(g_w_out)],
                         [g_w_uq_t.reshape(N_DEV, -1), g_w_ukv_t.reshape(N_DEV, -1), conv_dev.reshape(N_DEV, -1)])
        send = [per_dev(g_w_up_t), per_dev(g_w_down), rest]
        sib = _pair_exchange(send, f"l{l}_pair_exchange")
        core = lax.axis_index("c").reshape(1).astype(jnp.int32)
        pair = [_pair_add(core, a, b, f"l{l}_pair_add_{k}") for a, b, k in zip(send, sib, ("up", "down", "rest"))]
        recv = _chip_exchange(pair, f"l{l}_chip_exchange")
        wmv = [{k: v[o][l] for k, v in sharded_wmv.items()} for o in range(3)]
        res_up = _adamw(recv[0], *[a["w_up"].T for a in wmv], f"l{l}_adamw_up")
        res_dn = _adamw(recv[1], *[a["w_down"] for a in wmv], f"l{l}_adamw_down")
        res_rest = _adamw(recv[2], *[pack_rows([a["w_in"].T, a["w_out"]], [a["w_uq"].T.reshape(-1), a["w_ukv"].T.reshape(-1),
                                                                            a["conv_w"].reshape(-1)]) for a in wmv],
                          f"l{l}_adamw_rest")
        sharded_out[l] = [dict(unpack_rows(rr), w_up=ru.T, w_down=rd) for ru, rd, rr in zip(res_up, res_dn, res_rest)]
        grads[l] = dict(
            norm1_g=dg1[0], norm2_g=dg2[0], mla_cq_g=dg_cq[0], mla_ckv_g=dg_ckv[0], mla_qn_g=dg_qn[0, :MLA_QK],
            mla_kn_g=dg_kn[0, :MLA_QK], sw_qn_g=dg_swq[0, :HEAD] + dg_swq[0, HEAD:], sw_kn_g=dg_swk[0, :HEAD] + dg_swk[0, HEAD:],
            sw_sinks=dsinks[:SW_HEADS, 0], conv_b=cstats[0])
    grad_x = dx
    g_rel = _bias_grad(dbias, bucket, "rel_table_grad")[:SW_HEADS, :REL_BUCKETS].T
    stack = lambda k: jnp.stack([grads[l][k] for l in range(nl)])

    dm_all, = _all_gather([jnp.stack(dmods).reshape(nl * nb, 6 * d)], "gather_dmods")
    dm_all = dm_all.reshape(N_DEV, nl, nb, 6 * d).transpose(1, 0, 2, 3).reshape(nl, N_DEV * nb, 6 * d)
    dm_my = lax.dynamic_slice_in_dim(dm_all, me * n_ada, n_ada, axis=2)
    g_w_ada, g_b_ada = _ada_bwd(c_all, dm_my, dm_all, "ada_bwd")
    g_b_ada = g_b_ada.reshape(nl, 6 * d)

    big_out = [{k: jnp.stack([sharded_out[l][o][k] for l in range(nl)]) for k in sharded_names} for o in range(4)]
    packf = lambda dct, names, rows: jnp.pad(jnp.concatenate([dct[k].reshape(-1) for k in names]),
                                             (0, rows * LANES - sum(dct[k].size for k in names))).reshape(rows, LANES)

    small_names = ["rel_table", "norm1_g", "norm2_g", "mla_cq_g", "mla_ckv_g", "mla_qn_g", "mla_kn_g",
                   "sw_qn_g", "sw_kn_g", "sw_sinks", "conv_b"]
    small_w = dict(rel_table=rel_table, norm1_g=norm1_g, norm2_g=norm2_g, mla_cq_g=mla_cq_g, mla_ckv_g=mla_ckv_g,
                   mla_qn_g=mla_qn_g, mla_kn_g=mla_kn_g, sw_qn_g=sw_qn_g, sw_kn_g=sw_kn_g, sw_sinks=sw_sinks, conv_b=conv_b)
    small_m = dict(rel_table=m_rel_table, norm1_g=m_norm1_g, norm2_g=m_norm2_g, mla_cq_g=m_mla_cq_g, mla_ckv_g=m_mla_ckv_g,
                   mla_qn_g=m_mla_qn_g, mla_kn_g=m_mla_kn_g, sw_qn_g=m_sw_qn_g, sw_kn_g=m_sw_kn_g, sw_sinks=m_sw_sinks, conv_b=m_conv_b)
    small_v = dict(rel_table=v_rel_table, norm1_g=v_norm1_g, norm2_g=v_norm2_g, mla_cq_g=v_mla_cq_g, mla_ckv_g=v_mla_ckv_g,
                   mla_qn_g=v_mla_qn_g, mla_kn_g=v_mla_kn_g, sw_qn_g=v_sw_qn_g, sw_kn_g=v_sw_kn_g, sw_sinks=v_sw_sinks, conv_b=v_conv_b)
    small_g = {k: (g_rel if k == "rel_table" else stack(k)) for k in small_names}
    n_small = sum(small_w[k].size for k in small_names)
    rows_small = -(-n_small // (8 * LANES)) * 8
    small_parts, = _all_gather([packf(small_g, small_names, rows_small)], "gather_small_grads")
    small_out = _adamw(small_parts, packf(small_w, small_names, rows_small), packf(small_m, small_names, rows_small),
                       packf(small_v, small_names, rows_small), "adamw_replicated")
    small_out = [dict(zip(small_names, _unpack(o.reshape(-1), [small_w[k].shape for k in small_names]))) for o in small_out]

    two_d = lambda a: a.reshape(-1, a.shape[-1])
    res_w = _adamw(two_d(g_w_ada)[None], two_d(w_ada), two_d(m_w_ada), two_d(v_w_ada), "adamw_w_ada")
    res_b = _adamw(g_b_ada[None], b_ada, m_b_ada, v_b_ada, "adamw_b_ada")
    ada_out = [dict(w_ada=rw.reshape(w_ada.shape), b_ada=rb) for rw, rb in zip(res_w, res_b)]

    order = ["rel_table", "norm1_g", "norm2_g", "w_ada", "b_ada", "w_in", "mla_cq_g", "w_uq", "mla_ckv_g", "w_ukv",
             "mla_qn_g", "mla_kn_g", "sw_qn_g", "sw_kn_g", "sw_sinks", "w_out", "w_up", "conv_w", "conv_b", "w_down"]
    outs = [{**big_out[k], **small_out[k], **ada_out[k]} for k in range(4)]
    return (loss, grad_x, *[outs[0][n] for n in order], *[outs[1][n] for n in order],
            *[outs[2][n] for n in order], *[outs[3][n] for n in order])
```

```python
import math

import jax
import jax.numpy as jnp
from jax import lax
from jax.experimental import pallas as pl
from jax.experimental.pallas import tpu as pltpu

F32 = jnp.float32
MXU = jnp.bfloat16
EPS = 1e-6
NEG = -1e30
VMEM_LIMIT_BYTES = 56 * 1024 * 1024
N_DEV = 8
MESH = pl.DeviceIdType.MESH

D_MODEL = 1024
D_FF = 2816
HEAD = 64
LANES = 128
MLA_HEADS = 6
MLA_QK = 96
SW_HEADS = 6
REL_BUCKETS = 32
BLOCK = 128
SB_SCALE = HEAD ** -0.5
SB_DEAD = -105.0
SW_SCALE = HEAD ** -0.5
MLA_SCALE = MLA_QK ** -0.5
ROPE_THETA = 10000.0
D_IN_PAD = 2048
COL_SBQ, COL_SBK, COL_SBV, COL_CQ, COL_CKV, COL_SWQ, COL_SWK, COL_SWV, COL_KR = 0, 256, 512, 768, 1024, 1152, 1536, 1664, 1792

HALO = 16
ROW_TILE_BYTES = 1 << 20
ADAM_LR, ADAM_B1, ADAM_B2, ADAM_EPS, ADAM_WD, ADAM_STEP = 0.001, 0.9, 0.999, 1e-08, 0.01, 10


def _cp(*sem):
    return pltpu.CompilerParams(dimension_semantics=sem, vmem_limit_bytes=VMEM_LIMIT_BYTES)


def _iota(shape, dim):
    return lax.broadcasted_iota(jnp.int32, shape, dim)


def _dot(a, b):
    return jnp.dot(a, b, preferred_element_type=F32)


def _dot_nt(a, b):
    return lax.dot_general(a, b, (((1,), (1,)), ((), ())), preferred_element_type=F32)


def _dot_tn(a, b):
    return lax.dot_general(a, b, (((0,), (0,)), ((), ())), preferred_element_type=F32)


def _cumdot(x, u):
    hi = x.astype(MXU)
    r = x - hi.astype(F32)
    mid = r.astype(MXU)
    lo = (r - mid.astype(F32)).astype(MXU)
    return _dot(hi, u) + _dot(mid, u) + _dot(lo, u)


def _sigmoid(x):
    return 1.0 / (1.0 + jnp.exp(-x))


def _rsum(x):
    return jnp.sum(x, axis=-1, keepdims=True)


def _csum(x):
    return jnp.sum(x, axis=0, keepdims=True)


def _all_gather(xs, name):
    na = len(xs)

    def body(*refs):
        x_refs, out_refs = refs[:na], refs[na:2 * na]
        send_sems, recv_sems, local_sems = refs[2 * na:]
        x, y, c = lax.axis_index("x"), lax.axis_index("y"), lax.axis_index("c")
        me, sibling = (x, y, c), (x, y, 1 - c)
        chips = [(1 - x, y), (x, 1 - y), (1 - x, 1 - y)]

        def slot(a, px, py, pc):
            return out_refs[a].at[4 * px + 2 * py + pc]

        def copy(a, k, block, to, src=None):
            return pltpu.make_async_remote_copy(
                src_ref=slot(a, *block) if src is None else src, dst_ref=slot(a, *block),
                send_sem=send_sems.at[7 * a + k], recv_sem=recv_sems.at[7 * a + k], device_id=to, device_id_type=MESH)

        mines, sends = [], []
        for a in range(na):
            mines.append(pltpu.make_async_copy(x_refs[a], slot(a, *me), local_sems.at[a]))
            mines[-1].start()
            first = [copy(a, 0, me, sibling, src=x_refs[a])]
            first += [copy(a, 1 + j, me, (*chip, c), src=x_refs[a]) for j, chip in enumerate(chips)]
            for cp in first:
                cp.start()
            sends += first
        for j, chip in enumerate(chips):
            for a in range(na):
                copy(a, 1 + j, (*chip, c), me).wait_recv()
                sends.append(copy(a, 4 + j, (*chip, c), sibling))
                sends[-1].start()
        for a in range(na):
            copy(a, 0, sibling, me).wait_recv()
            for j, chip in enumerate(chips):
                copy(a, 4 + j, (*chip, 1 - c), me).wait_recv()
        for cp in sends:
            cp.wait_send()
        for mine in mines:
            mine.wait()

    hbm = pl.BlockSpec(memory_space=pl.ANY)
    return pl.pallas_call(
        body, name=name, out_shape=[jax.ShapeDtypeStruct((N_DEV,) + a.shape, a.dtype) for a in xs],
        in_specs=[hbm] * na, out_specs=[hbm] * na,
        scratch_shapes=[pltpu.SemaphoreType.DMA((7 * na,)), pltpu.SemaphoreType.DMA((7 * na,)),
                        pltpu.SemaphoreType.DMA((na,))],
    )(*xs)


def _pair_exchange(xs, name):
    na = len(xs)

    def body(*refs):
        x_refs, out_refs = refs[:na], refs[na:2 * na]
        send_sems, recv_sems = refs[2 * na:]
        x, y, c = lax.axis_index("x"), lax.axis_index("y"), lax.axis_index("c")
        copies = []
        for a in range(na):
            for q in range(4):
                copies.append(pltpu.make_async_remote_copy(
                    src_ref=x_refs[a].at[2 * q + 1 - c], dst_ref=out_refs[a].at[q],
                    send_sem=send_sems.at[4 * a + q], recv_sem=recv_sems.at[4 * a + q],
                    device_id=(x, y, 1 - c), device_id_type=MESH))
                copies[-1].start()
        for cp in copies:
            cp.wait()

    hbm = pl.BlockSpec(memory_space=pl.ANY)
    return pl.pallas_call(
        body, name=name, out_shape=[jax.ShapeDtypeStruct((4,) + a.shape[1:], a.dtype) for a in xs],
        in_specs=[hbm] * na, out_specs=[hbm] * na,
        scratch_shapes=[pltpu.SemaphoreType.DMA((4 * na,)), pltpu.SemaphoreType.DMA((4 * na,))])(*xs)


def _row_tile(r, ncol):
    if r * ncol * 4 <= ROW_TILE_BYTES:
        return r
    return max(t for t in range(16, r, 16) if r % t == 0 and t * ncol * 4 <= ROW_TILE_BYTES)


def _pair_add(core, xs, sib, name):
    _, r, ncol = xs.shape
    tr = _row_tile(r, ncol)

    def body(c_ref, x_ref, s_ref, o_ref):
        o_ref[...] = (x_ref[...] + s_ref[...]).astype(MXU)

    return pl.pallas_call(
        body, name=name,
        grid_spec=pltpu.PrefetchScalarGridSpec(
            num_scalar_prefetch=1, grid=(4, r // tr),
            in_specs=[pl.BlockSpec((None, tr, ncol), lambda q, i, c_ref: (2 * q + c_ref[0], i, 0)),
                      pl.BlockSpec((None, tr, ncol), lambda q, i, c_ref: (q, i, 0))],
            out_specs=pl.BlockSpec((None, tr, ncol), lambda q, i, c_ref: (q, i, 0))),
        out_shape=jax.ShapeDtypeStruct((4, r, ncol), MXU),
        compiler_params=_cp("parallel", "parallel"))(core, xs, sib)


def _chip_exchange(xs, name):
    na = len(xs)
    flips = [(1, 0), (0, 1), (1, 1)]

    def body(*refs):
        x_refs, out_refs = refs[:na], refs[na:2 * na]
        send_sems, recv_sems, local_sems = refs[2 * na:]
        x, y, c = lax.axis_index("x"), lax.axis_index("y"), lax.axis_index("c")
        me = 2 * x + y
        mines, copies = [], []
        for a in range(na):
            mines.append(pltpu.make_async_copy(x_refs[a].at[me], out_refs[a].at[me], local_sems.at[a]))
            mines[-1].start()
        for k, (dx, dy) in enumerate(flips):
            px = 1 - x if dx else x
            py = 1 - y if dy else y
            for a in range(na):
                copies.append(pltpu.make_async_remote_copy(
                    src_ref=x_refs[a].at[2 * px + py], dst_ref=out_refs[a].at[me],
                    send_sem=send_sems.at[3 * a + k], recv_sem=recv_sems.at[3 * a + k],
                    device_id=(px, py, c), device_id_type=MESH))
                copies[-1].start()
        for cp in copies:
            cp.wait()
        for mine in mines:
            mine.wait()

    hbm = pl.BlockSpec(memory_space=pl.ANY)
    return pl.pallas_call(
        body, name=name, out_shape=[jax.ShapeDtypeStruct(a.shape, a.dtype) for a in xs],
        in_specs=[hbm] * na, out_specs=[hbm] * na,
        scratch_shapes=[pltpu.SemaphoreType.DMA((3 * na,)), pltpu.SemaphoreType.DMA((3 * na,)),
                        pltpu.SemaphoreType.DMA((na,))])(*xs)


def _ada_fwd(c_all, w_ada, b_my, name):
    nl, d, n = w_ada.shape
    nb = c_all.shape[0]

    def body(c_ref, w_ref, b_ref, o_ref):
        cv = c_ref[...]
        sc = (cv * _sigmoid(cv)).astype(MXU)
        o_ref[...] = _dot(sc, w_ref[...].astype(MXU)) + b_ref[...]

    return pl.pallas_call(
        body, name=name, grid=(nl,),
        in_specs=[pl.BlockSpec((nb, d), lambda l: (0, 0)),
                  pl.BlockSpec((None, d, n), lambda l: (l, 0, 0)),
                  pl.BlockSpec((None, 1, n), lambda l: (l, 0, 0))],
        out_specs=pl.BlockSpec((None, nb, n), lambda l: (l, 0, 0)),
        out_shape=jax.ShapeDtypeStruct((nl, nb, n), F32),
        compiler_params=_cp("parallel"))(c_all, w_ada, b_my)


def _ada_bwd(c_all, dmods_my, dmods_all, name):
    nl, nb, n = dmods_my.shape
    d = c_all.shape[1]
    nfull = dmods_all.shape[2]

    def body(c_ref, dm_ref, da_ref, dw_ref, db_ref):
        cv = c_ref[...]
        sc = (cv * _sigmoid(cv)).astype(MXU)
        dw_ref[...] = _dot_tn(sc, dm_ref[...].astype(MXU))
        db_ref[...] = _csum(da_ref[...])

    return pl.pallas_call(
        body, name=name, grid=(nl,),
        in_specs=[pl.BlockSpec((nb, d), lambda l: (0, 0)),
                  pl.BlockSpec((None, nb, n), lambda l: (l, 0, 0)),
                  pl.BlockSpec((None, nb, nfull), lambda l: (l, 0, 0))],
        out_specs=[pl.BlockSpec((None, d, n), lambda l: (l, 0, 0)),
                   pl.BlockSpec((None, 1, nfull), lambda l: (l, 0, 0))],
        out_shape=[jax.ShapeDtypeStruct((nl, d, n), F32), jax.ShapeDtypeStruct((nl, 1, nfull), F32)],
        compiler_params=_cp("parallel"))(c_all, dmods_my, dmods_all)


def _ln_mod_matmul(x, g, scale, shift, w, name):
    nb, s, d = x.shape
    n = w.shape[0]
    tm, tn = min(1024, s), 512

    def body(x_ref, g_ref, sc_ref, sh_ref, w_ref, y_ref, h_ref, h_s):
        @pl.when(pl.program_id(2) == 0)
        def _():
            xf = x_ref[...]
            rstd = lax.rsqrt(jnp.mean(xf * xf, axis=-1, keepdims=True) + EPS)
            hv = (xf * rstd * g_ref[...]) * (1.0 + sc_ref[...]) + sh_ref[...]
            h_s[...] = hv.astype(MXU)
            h_ref[...] = h_s[...]

        y_ref[...] = _dot_nt(h_s[...], w_ref[...]).astype(MXU)

    return pl.pallas_call(
        body, name=name, grid=(nb, s // tm, n // tn),
        in_specs=[pl.BlockSpec((None, tm, d), lambda b, i, j: (b, i, 0)),
                  pl.BlockSpec((1, d), lambda b, i, j: (0, 0)),
                  pl.BlockSpec((None, 1, d), lambda b, i, j: (b, 0, 0)),
                  pl.BlockSpec((None, 1, d), lambda b, i, j: (b, 0, 0)),
                  pl.BlockSpec((tn, d), lambda b, i, j: (j, 0))],
        out_specs=[pl.BlockSpec((None, tm, tn), lambda b, i, j: (b, i, j)),
                   pl.BlockSpec((None, tm, d), lambda b, i, j: (b, i, 0))],
        out_shape=[jax.ShapeDtypeStruct((nb, s, n), MXU), jax.ShapeDtypeStruct((nb, s, d), MXU)],
        scratch_shapes=[pltpu.VMEM((tm, d), MXU)],
        compiler_params=_cp("parallel", "parallel", "arbitrary"))(x, g, scale, shift, w)


def _ln_mod_matmul_bwd(dy, w, x, g, scale, dres, conv_w, name):
    nb, s, n = dy.shape
    d = x.shape[-1]
    tm, tn = min(512, s), 512
    ni, nj = s // tm, n // tn
    hb = tm // 8
    conv = conv_w is not None

    def body(*refs):
        if conv:
            dy_ref, nx_ref, cw_ref = refs[:3]
            refs = refs[3:]
        else:
            dy_ref = refs[0]
            refs = refs[1:]
        w_ref, x_ref, g_ref, sc_ref, dr_ref, dx_ref, dyp_ref, dsh_ref, dsc_ref, dg_ref, acc = refs
        b, i, j = pl.program_id(0), pl.program_id(1), pl.program_id(2)

        @pl.when(j == 0)
        def _():
            acc[...] = jnp.zeros_like(acc)

        @pl.when((j == 0) & (i == 0))
        def _():
            dsh_ref[...] = jnp.zeros_like(dsh_ref)
            dsc_ref[...] = jnp.zeros_like(dsc_ref)

        @pl.when((j == 0) & (i == 0) & (b == 0))
        def _():
            dg_ref[...] = jnp.zeros_like(dg_ref)

        dv = dy_ref[...]
        if conv:
            rows = _iota((tm, 1), 0)
            nx = jnp.where(i == ni - 1, 0.0, nx_ref[...])
            n1 = jnp.where(rows == tm - 1, nx[0:1, :], pltpu.roll(dv, tm - 1, 0))
            n2 = jnp.where(rows == tm - 2, nx[0:1, :], jnp.where(rows == tm - 1, nx[1:2, :], pltpu.roll(dv, tm - 2, 0)))
            cw = cw_ref[...]
            dv = cw[2:3, :] * dv + cw[1:2, :] * n1 + cw[0:1, :] * n2
        dp = dv.astype(MXU)
        dyp_ref[...] = dp
        acc[...] += _dot(dp, w_ref[...])

        @pl.when(j == nj - 1)
        def _():
            dh = acc[...]
            xf = x_ref[...]
            rstd = lax.rsqrt(jnp.mean(xf * xf, axis=-1, keepdims=True) + EPS)
            xn = xf * rstd
            gg = g_ref[...]
            sc1 = 1.0 + sc_ref[...]
            dsh_ref[...] += _csum(dh)
            dsc_ref[...] += _csum(dh * xn * gg)
            dg_ref[...] += _csum(dh * xn * sc1)
            dn = dh * gg * sc1
            dx_ref[...] = dr_ref[...] + rstd * (dn - xn * jnp.mean(dn * xn, axis=-1, keepdims=True))

    in_specs = [pl.BlockSpec((None, tm, tn), lambda b, i, j: (b, i, j))]
    args = [dy]
    if conv:
        in_specs += [pl.BlockSpec((None, 8, tn), lambda b, i, j: (b, jnp.minimum((i + 1) * hb, s // 8 - 1), j)),
                     pl.BlockSpec((3, tn), lambda b, i, j: (0, j))]
        args += [dy, conv_w]
    in_specs += [pl.BlockSpec((tn, d), lambda b, i, j: (j, 0)),
                 pl.BlockSpec((None, tm, d), lambda b, i, j: (b, i, 0)),
                 pl.BlockSpec((1, d), lambda b, i, j: (0, 0)),
                 pl.BlockSpec((None, 1, d), lambda b, i, j: (b, 0, 0)),
                 pl.BlockSpec((None, tm, d), lambda b, i, j: (b, i, 0))]
    args += [w, x, g, scale, dres]
    return pl.pallas_call(
        body, name=name, grid=(nb, ni, nj), in_specs=in_specs,
        out_specs=[pl.BlockSpec((None, tm, d), lambda b, i, j: (b, i, 0)),
                   pl.BlockSpec((None, tm, tn), lambda b, i, j: (b, i, j)),
                   pl.BlockSpec((None, 1, d), lambda b, i, j: (b, 0, 0)),
                   pl.BlockSpec((None, 1, d), lambda b, i, j: (b, 0, 0)),
                   pl.BlockSpec((1, d), lambda b, i, j: (0, 0))],
        out_shape=[jax.ShapeDtypeStruct((nb, s, d), F32), jax.ShapeDtypeStruct((nb, s, n), MXU),
                   jax.ShapeDtypeStruct((nb, 1, d), F32), jax.ShapeDtypeStruct((nb, 1, d), F32),
                   jax.ShapeDtypeStruct((1, d), F32)],
        scratch_shapes=[pltpu.VMEM((tm, d), F32)],
        compiler_params=_cp("arbitrary", "arbitrary", "arbitrary"))(*args)


def _wgrad(xm, dym, name):
    t, k = xm.shape
    n = dym.shape[1]
    tk = 1408 if k % 1408 == 0 else 1024
    tt = min(512, t)

    def body(x_ref, dy_ref, o_ref):
        @pl.when(pl.program_id(1) == 0)
        def _():
            o_ref[...] = jnp.zeros_like(o_ref)

        o_ref[...] += _dot_tn(x_ref[...], dy_ref[...])

    return pl.pallas_call(
        body, name=name, grid=(k // tk, t // tt),
        in_specs=[pl.BlockSpec((tt, tk), lambda a, c: (c, a)),
                  pl.BlockSpec((tt, n), lambda a, c: (c, 0))],
        out_specs=pl.BlockSpec((tk, n), lambda a, c: (a, 0)),
        out_shape=jax.ShapeDtypeStruct((k, n), F32),
        compiler_params=_cp("parallel", "arbitrary"))(xm, dym)


def _out_proj(parts, ws, gate, res, name):
    nb, s, d = res.shape
    tm = min(512, s)
    npart = len(parts)

    def body(*refs):
        p_refs, w_refs = refs[:npart], refs[npart:2 * npart]
        gt_ref, res_ref, xo_ref, y_ref = refs[2 * npart:]
        y = _dot(p_refs[0][...].astype(MXU), w_refs[0][...])
        for p_ref, w_ref in zip(p_refs[1:], w_refs[1:]):
            y = y + _dot(p_ref[...].astype(MXU), w_ref[...])
        y_ref[...] = y
        xo_ref[...] = res_ref[...] + gt_ref[...] * y

    in_specs = [pl.BlockSpec((None, tm, p.shape[-1]), lambda b, i: (b, i, 0)) for p in parts]
    in_specs += [pl.BlockSpec(w.shape, lambda b, i: (0, 0)) for w in ws]
    in_specs += [pl.BlockSpec((None, 1, d), lambda b, i: (b, 0, 0)),
                 pl.BlockSpec((None, tm, d), lambda b, i: (b, i, 0))]
    return pl.pallas_call(
        body, name=name, grid=(nb, s // tm), in_specs=in_specs,
        out_specs=[pl.BlockSpec((None, tm, d), lambda b, i: (b, i, 0))] * 2,
        out_shape=[jax.ShapeDtypeStruct((nb, s, d), F32)] * 2,
        compiler_params=_cp("parallel", "parallel"))(*parts, *ws, gate, res)


def _gate_bwd_nt(dx, y, gate, ws, name):
    nb, s, d = dx.shape
    tm = min(256, s)
    npart = len(ws)

    def body(*refs):
        dx_ref, y_ref, gt_ref = refs[:3]
        w_refs = refs[3:3 + npart]
        da_refs = refs[3 + npart:3 + 2 * npart]
        dy_ref, dgt_ref = refs[3 + 2 * npart:]

        @pl.when(pl.program_id(1) == 0)
        def _():
            dgt_ref[...] = jnp.zeros_like(dgt_ref)

        dxv = dx_ref[...]
        dyv = (dxv * gt_ref[...]).astype(MXU)
        dy_ref[...] = dyv
        dgt_ref[...] += _csum(dxv * y_ref[...])
        for w_ref, da_ref in zip(w_refs, da_refs):
            da_ref[...] = _dot_nt(dyv, w_ref[...])

    tile = pl.BlockSpec((None, tm, d), lambda b, i: (b, i, 0))
    row = pl.BlockSpec((None, 1, d), lambda b, i: (b, 0, 0))
    outs = pl.pallas_call(
        body, name=name, grid=(nb, s // tm),
        in_specs=[tile, tile, row] + [pl.BlockSpec(w.shape, lambda b, i: (0, 0)) for w in ws],
        out_specs=[pl.BlockSpec((None, tm, w.shape[0]), lambda b, i: (b, i, 0)) for w in ws] + [tile, row],
        out_shape=[jax.ShapeDtypeStruct((nb, s, w.shape[0]), F32) for w in ws]
        + [jax.ShapeDtypeStruct((nb, s, d), MXU), jax.ShapeDtypeStruct((nb, 1, d), F32)],
        compiler_params=_cp("arbitrary", "arbitrary"))(dx, y, gate, *ws)
    return outs[:npart], outs[npart], outs[npart + 1]


def _conv_shifts(xv, halo, rows):
    last, before = halo[HALO - 1:HALO, :], halo[HALO - 2:HALO - 1, :]
    p1 = jnp.where(rows == 0, last, pltpu.roll(xv, 1, 0))
    p2 = jnp.where(rows == 0, before, jnp.where(rows == 1, last, pltpu.roll(xv, 2, 0)))
    return p1, p2


def _conv_gate_matmul(u, cw, cb, wd, gate, res, name):
    nb, s, f2 = u.shape
    f = f2 // 2
    d = wd.shape[1]
    tm = min(512, s)
    tk = f // 2
    nk = f // tk
    hb = tm // HALO

    def body(ug_ref, uv_ref, hg_ref, hv_ref, cwg_ref, cwv_ref, cbg_ref, cbv_ref, wd_ref, gt_ref, res_ref,
             xo_ref, y_ref, acc):
        i, k = pl.program_id(1), pl.program_id(2)

        @pl.when(k == 0)
        def _():
            acc[...] = jnp.zeros_like(acc)

        rows = _iota((tm, 1), 0)

        def conv(x_ref, h_ref, w_ref, b_ref):
            xv = x_ref[...].astype(F32)
            halo = jnp.where(i == 0, 0.0, h_ref[...].astype(F32))
            p1, p2 = _conv_shifts(xv, halo, rows)
            wv = w_ref[...]
            return wv[2:3, :] * xv + wv[1:2, :] * p1 + wv[0:1, :] * p2 + b_ref[...]

        gv = conv(ug_ref, hg_ref, cwg_ref, cbg_ref)
        vv = conv(uv_ref, hv_ref, cwv_ref, cbv_ref)
        av = gv * _sigmoid(gv) * vv
        acc[...] += _dot(av.astype(MXU), wd_ref[...])

        @pl.when(k == nk - 1)
        def _():
            y = acc[...]
            y_ref[...] = y
            xo_ref[...] = res_ref[...] + gt_ref[...] * y

    def halo_idx(off):
        return lambda b, i, k: (b, jnp.maximum(i * hb - 1, 0), k + off)

    tile = pl.BlockSpec((None, tm, d), lambda b, i, k: (b, i, 0))
    return pl.pallas_call(
        body, name=name, grid=(nb, s // tm, nk),
        in_specs=[pl.BlockSpec((None, tm, tk), lambda b, i, k: (b, i, k)),
                  pl.BlockSpec((None, tm, tk), lambda b, i, k: (b, i, k + nk)),
                  pl.BlockSpec((None, HALO, tk), halo_idx(0)),
                  pl.BlockSpec((None, HALO, tk), halo_idx(nk)),
                  pl.BlockSpec((3, tk), lambda b, i, k: (0, k)),
                  pl.BlockSpec((3, tk), lambda b, i, k: (0, k + nk)),
                  pl.BlockSpec((1, tk), lambda b, i, k: (0, k)),
                  pl.BlockSpec((1, tk), lambda b, i, k: (0, k + nk)),
                  pl.BlockSpec((tk, d), lambda b, i, k: (k, 0)),
                  pl.BlockSpec((None, 1, d), lambda b, i, k: (b, 0, 0)),
                  tile],
        out_specs=[tile, tile],
        out_shape=[jax.ShapeDtypeStruct((nb, s, d), F32)] * 2,
        scratch_shapes=[pltpu.VMEM((tm, d), F32)],
        compiler_params=_cp("parallel", "parallel", "arbitrary"))(u, u, u, u, cw, cw, cb, cb, wd, gate, res)


def _conv_gate_bwd(da, u, cw, cb, name):
    nb, s, f2 = u.shape
    f = f2 // 2
    tm = min(128, s)
    hb = tm // HALO

    def body(da_ref, u_ref, h_ref, cw_ref, cb_ref, du_ref, a_ref, st_ref):
        b, i = pl.program_id(0), pl.program_id(1)

        @pl.when((b == 0) & (i == 0))
        def _():
            st_ref[...] = jnp.zeros_like(st_ref)

        rows = _iota((tm, 1), 0)
        first = i == 0

        def conv(cs):
            xv = u_ref[:, cs].astype(F32)
            halo = jnp.where(first, 0.0, h_ref[:, cs].astype(F32))
            p1, p2 = _conv_shifts(xv, halo, rows)
            wv = cw_ref[:, cs]
            return xv, p1, p2, wv[2:3, :] * xv + wv[1:2, :] * p1 + wv[0:1, :] * p2 + cb_ref[:, cs]

        def stats(cs, du, xv, p1, p2):
            du_ref[:, cs] = du
            st_ref[0:1, cs] += _csum(du)
            st_ref[1:2, cs] += _csum(du * p2)
            st_ref[2:3, cs] += _csum(du * p1)
            st_ref[3:4, cs] += _csum(du * xv)

        for k in range(f // LANES):
            cg = slice(k * LANES, (k + 1) * LANES)
            cv = slice(f + k * LANES, f + (k + 1) * LANES)
            xg, g1, g2, gv = conv(cg)
            xv, v1, v2, vv = conv(cv)
            sg = _sigmoid(gv)
            sl = gv * sg
            a_ref[:, cg] = (sl * vv).astype(MXU)
            dav = da_ref[:, cg]
            stats(cg, dav * vv * (sg * (1.0 + gv * (1.0 - sg))), xg, g1, g2)
            stats(cv, dav * sl, xv, v1, v2)

    return pl.pallas_call(
        body, name=name, grid=(nb, s // tm),
        in_specs=[pl.BlockSpec((None, tm, f), lambda b, i: (b, i, 0)),
                  pl.BlockSpec((None, tm, f2), lambda b, i: (b, i, 0)),
                  pl.BlockSpec((None, HALO, f2), lambda b, i: (b, jnp.maximum(i * hb - 1, 0), 0)),
                  pl.BlockSpec((3, f2), lambda b, i: (0, 0)),
                  pl.BlockSpec((1, f2), lambda b, i: (0, 0))],
        out_specs=[pl.BlockSpec((None, tm, f2), lambda b, i: (b, i, 0)),
                   pl.BlockSpec((None, tm, f), lambda b, i: (b, i, 0)),
                   pl.BlockSpec((8, f2), lambda b, i: (0, 0))],
        out_shape=[jax.ShapeDtypeStruct((nb, s, f2), F32), jax.ShapeDtypeStruct((nb, s, f), MXU),
                   jax.ShapeDtypeStruct((8, f2), F32)],
        compiler_params=_cp("arbitrary", "arbitrary"))(da, u, u, cw, cb)


def _rot(xv, lane):
    return jnp.where((lane >= 64) & (lane < 80), -pltpu.roll(xv, 112, 1),
                     jnp.where((lane >= 80) & (lane < 96), pltpu.roll(xv, 16, 1), 0.0))


def _rot_t(dv, lane):
    return jnp.where((lane >= 80) & (lane < 96), -pltpu.roll(dv, 16, 1),
                     jnp.where((lane >= 64) & (lane < 80), pltpu.roll(dv, 112, 1), 0.0))


def _mla_prep_specs(s, tm):
    def blk(width, col):
        return pl.BlockSpec((None, tm, width), lambda b, i: (b, i, col // width))

    full = lambda shape: pl.BlockSpec(shape, lambda b, i: (0, 0))
    return [blk(256, COL_CQ), blk(128, COL_CKV), blk(128, COL_KR),
            pl.BlockSpec((None, tm, LANES), lambda b, i: (b, i, 0)),
            pl.BlockSpec((None, tm, LANES), lambda b, i: (b, i, 0)),
            full((1, 256)), full((1, 128)), full((1, 128)), full((1, 128)),
            full((768, 256)), full((768, 128))]


def _mla_prep(proj, cs, sn, gcq, gckv, gqn, gkn, wuq, wukv, name):
    nb, s, _ = proj.shape
    tm = min(256, s)

    def body(cq_ref, ckv_ref, kr_ref, c_ref, s_ref, gcq_ref, gckv_ref, gqn_ref, gkn_ref, wuq_ref, wukv_ref,
             q_ref, k_ref, v_ref):
        lane = _iota((tm, LANES), 1)
        cv, sv = c_ref[...], s_ref[...]
        cq = cq_ref[...].astype(F32)
        cqn = cq * lax.rsqrt(jnp.mean(cq * cq, axis=-1, keepdims=True) + EPS) * gcq_ref[...]
        qb = _dot_nt(cqn.astype(MXU), wuq_ref[...])
        ckv = ckv_ref[...].astype(F32)
        ckvn = ckv * lax.rsqrt(jnp.mean(ckv * ckv, axis=-1, keepdims=True) + EPS) * gckv_ref[...]
        kvb = _dot_nt(ckvn.astype(MXU), wukv_ref[...])
        kr = kr_ref[...].astype(F32)
        for h in range(MLA_HEADS):
            hs = slice(h * LANES, (h + 1) * LANES)
            qh = qb[:, hs]
            qn = qh * lax.rsqrt(_rsum(qh * qh) / MLA_QK + EPS) * gqn_ref[...]
            q_ref[:, hs] = (qn * cv + _rot(qn, lane) * sv).astype(MXU)
            kc = jnp.where(lane < HEAD, kvb[:, hs], kr)
            kn = kc * lax.rsqrt(_rsum(kc * kc) / MLA_QK + EPS) * gkn_ref[...]
            k_ref[:, hs] = (kn * cv + _rot(kn, lane) * sv).astype(MXU)
        for j in range(MLA_HEADS // 2):
            va = kvb[:, (2 * j) * LANES:(2 * j + 1) * LANES]
            vb = kvb[:, (2 * j + 1) * LANES:(2 * j + 2) * LANES]
            v_ref[:, j * LANES:(j + 1) * LANES] = jnp.where(lane < HEAD, pltpu.roll(va, HEAD, 1), vb).astype(MXU)

    return pl.pallas_call(
        body, name=name, grid=(nb, s // tm), in_specs=_mla_prep_specs(s, tm),
        out_specs=[pl.BlockSpec((None, tm, 768), lambda b, i: (b, i, 0)),
                   pl.BlockSpec((None, tm, 768), lambda b, i: (b, i, 0)),
                   pl.BlockSpec((None, tm, 384), lambda b, i: (b, i, 0))],
        out_shape=[jax.ShapeDtypeStruct((nb, s, 768), MXU), jax.ShapeDtypeStruct((nb, s, 768), MXU),
                   jax.ShapeDtypeStruct((nb, s, 384), MXU)],
        compiler_params=_cp("parallel", "parallel"))(proj, proj, proj, cs, sn, gcq, gckv, gqn, gkn, wuq, wukv)


def _mla_prep_bwd(proj, cs, sn, gcq, gckv, gqn, gkn, wuq, wukv, dq, dk, dv, name):
    nb, s, _ = proj.shape
    tm = min(256, s)

    def body(cq_ref, ckv_ref, kr_ref, c_ref, s_ref, gcq_ref, gckv_ref, gqn_ref, gkn_ref, wuq_ref, wukv_ref,
             dq_ref, dk_ref, dv_ref,
             dcq_ref, dckv_ref, dkr_ref, dwuq_ref, dwukv_ref, dgcq_ref, dgckv_ref, dgqn_ref, dgkn_ref,
             dqb_s, dkvb_s):
        @pl.when((pl.program_id(0) == 0) & (pl.program_id(1) == 0))
        def _():
            for r in (dwuq_ref, dwukv_ref, dgcq_ref, dgckv_ref, dgqn_ref, dgkn_ref):
                r[...] = jnp.zeros_like(r)

        lane = _iota((tm, LANES), 1)
        cv, sv = c_ref[...], s_ref[...]
        gqn, gkn = gqn_ref[...], gkn_ref[...]
        cq = cq_ref[...].astype(F32)
        rc = lax.rsqrt(jnp.mean(cq * cq, axis=-1, keepdims=True) + EPS)
        chat = cq * rc
        cqn = (chat * gcq_ref[...]).astype(MXU)
        qb = _dot_nt(cqn, wuq_ref[...])
        ckv = ckv_ref[...].astype(F32)
        rkv = lax.rsqrt(jnp.mean(ckv * ckv, axis=-1, keepdims=True) + EPS)
        kvhat = ckv * rkv
        ckvn = (kvhat * gckv_ref[...]).astype(MXU)
        kvb = _dot_nt(ckvn, wukv_ref[...])
        kr = kr_ref[...].astype(F32)
        dgq = jnp.zeros((1, LANES), F32)
        dgk = jnp.zeros((1, LANES), F32)
        dkr = jnp.zeros((tm, LANES), F32)
        for h in range(MLA_HEADS):
            hs = slice(h * LANES, (h + 1) * LANES)
            qh = qb[:, hs]
            rq = lax.rsqrt(_rsum(qh * qh) / MLA_QK + EPS)
            qhat = qh * rq
            dqr = dq_ref[:, hs]
            dqn = dqr * cv + _rot_t(dqr * sv, lane)
            dgq = dgq + _csum(dqn * qhat)
            dyq = dqn * gqn
            dqb_s[:, hs] = (rq * (dyq - qhat * (_rsum(dyq * qhat) / MLA_QK))).astype(MXU)

            kc = jnp.where(lane < HEAD, kvb[:, hs], kr)
            rk = lax.rsqrt(_rsum(kc * kc) / MLA_QK + EPS)
            khat = kc * rk
            dkr_h = dk_ref[:, hs]
            dkn = dkr_h * cv + _rot_t(dkr_h * sv, lane)
            dgk = dgk + _csum(dkn * khat)
            dyk = dkn * gkn
            dkc = rk * (dyk - khat * (_rsum(dyk * khat) / MLA_QK))
            dkr = dkr + jnp.where(lane >= HEAD, dkc, 0.0)
            dvb = dv_ref[:, (h // 2) * LANES:(h // 2 + 1) * LANES]
            dvp = dvb if h % 2 == 1 else pltpu.roll(dvb, HEAD, 1)
            dkvb_s[:, hs] = jnp.where(lane < HEAD, dkc, dvp).astype(MXU)
        dgqn_ref[...] += dgq
        dgkn_ref[...] += dgk
        dkr_ref[...] = dkr

        dqb = dqb_s[...]
        dwuq_ref[...] += _dot_tn(dqb, cqn)
        dcqn = _dot(dqb, wuq_ref[...])
        dgcq_ref[...] += _csum(dcqn * chat)
        dyc = dcqn * gcq_ref[...]
        dcq_ref[...] = rc * (dyc - chat * jnp.mean(dyc * chat, axis=-1, keepdims=True))

        dkvb = dkvb_s[...]
        dwukv_ref[...] += _dot_tn(dkvb, ckvn)
        dckvn = _dot(dkvb, wukv_ref[...])
        dgckv_ref[...] += _csum(dckvn * kvhat)
        dykv = dckvn * gckv_ref[...]
        dckv_ref[...] = rkv * (dykv - kvhat * jnp.mean(dykv * kvhat, axis=-1, keepdims=True))

    full = lambda shape: pl.BlockSpec(shape, lambda b, i: (0, 0))
    tile = lambda width: pl.BlockSpec((None, tm, width), lambda b, i: (b, i, 0))
    return pl.pallas_call(
        body, name=name, grid=(nb, s // tm),
        in_specs=_mla_prep_specs(s, tm) + [tile(768), tile(768), tile(384)],
        out_specs=[tile(256), tile(128), tile(128), full((768, 256)), full((768, 128)),
                   full((1, 256)), full((1, 128)), full((1, 128)), full((1, 128))],
        out_shape=[jax.ShapeDtypeStruct((nb, s, 256), F32), jax.ShapeDtypeStruct((nb, s, 128), F32),
                   jax.ShapeDtypeStruct((nb, s, 128), F32),
                   jax.ShapeDtypeStruct((768, 256), F32), jax.ShapeDtypeStruct((768, 128), F32),
                   jax.ShapeDtypeStruct((1, 256), F32), jax.ShapeDtypeStruct((1, 128), F32),
                   jax.ShapeDtypeStruct((1, 128), F32), jax.ShapeDtypeStruct((1, 128), F32)],
        scratch_shapes=[pltpu.VMEM((tm, 768), MXU), pltpu.VMEM((tm, 768), MXU)],
        compiler_params=_cp("arbitrary", "arbitrary"))(
            proj, proj, proj, cs, sn, gcq, gckv, gqn, gkn, wuq, wukv, dq, dk, dv)


def _softplus(z):
    return jnp.maximum(z, 0.0) + jnp.log(1.0 + jnp.exp(-jnp.abs(z)))


def _sb_fwd(proj, name):
    nb, s, _ = proj.shape
    tq, tk = min(256, s), 128
    ratio = tq // tk

    def body(q_ref, k_ref, v_ref, o_ref, ct_ref, cnt_ref):
        i = pl.program_id(2)
        lo = _iota((tq, LANES), 1) < HEAD
        qv = q_ref[...]
        q0 = jnp.where(lo, qv, 0.0).astype(MXU)
        q1 = jnp.where(lo, 0.0, qv).astype(MXU)
        usuf = (_iota((tk, tk), 0) > _iota((tk, tk), 1)).astype(MXU)
        tpos = i * tq + _iota((tq, tk), 0)
        scol = _iota((tq, tk), 1)
        nch = (i + 1) * ratio

        def alive(st):
            return (st[0] < nch) & (st[5] > SB_DEAD)

        def step(st):
            t, c0, a0, c1, a1, _ = st
            j = nch - 1 - t
            off = pl.multiple_of(j * tk, tk)
            kc = k_ref[pl.ds(off, tk), :].astype(MXU)
            vc = v_ref[pl.ds(off, tk), :].astype(MXU)
            msk = (scol + j * tk) < tpos

            def head(qm, c, a):
                z = _dot_nt(qm, kc) * SB_SCALE
                sp = _softplus(z)
                lk = jnp.where(msk, -sp, 0.0)
                w = jnp.where(msk, jnp.exp(z - sp + _cumdot(lk, usuf) + c), 0.0)
                return c + _rsum(lk), a + _dot(w.astype(MXU), vc)

            c0, a0 = head(q0, c0, a0)
            c1, a1 = head(q1, c1, a1)
            return t + 1, c0, a0, c1, a1, jnp.maximum(jnp.max(c0), jnp.max(c1))

        z1 = jnp.zeros((tq, 1), F32)
        za = jnp.zeros((tq, LANES), F32)
        t, c0, a0, c1, a1, _ = lax.while_loop(alive, step, (jnp.int32(0), z1, za, z1, za, jnp.float32(0.0)))
        o_ref[...] = jnp.where(lo, a0, a1)
        ct_ref[...] = jnp.where(lo, c0, c1)
        cnt_ref[...] = jnp.zeros((8, LANES), F32) + t.astype(F32)

    kv = lambda col: pl.BlockSpec((None, s, LANES), lambda b, p, i: (b, 0, col // LANES + p))
    tile = pl.BlockSpec((None, tq, LANES), lambda b, p, i: (b, i, p))
    return pl.pallas_call(
        body, name=name, grid=(nb, 2, s // tq),
        in_specs=[pl.BlockSpec((None, tq, LANES), lambda b, p, i: (b, i, COL_SBQ // LANES + p)),
                  kv(COL_SBK), kv(COL_SBV)],
        out_specs=[tile, tile, pl.BlockSpec((None, None, None, 8, LANES), lambda b, p, i: (b, p, i, 0, 0))],
        out_shape=[jax.ShapeDtypeStruct((nb, s, 256), F32)] * 2
        + [jax.ShapeDtypeStruct((nb, 2, s // tq, 8, LANES), F32)],
        compiler_params=_cp("parallel", "parallel", "arbitrary"))(proj, proj, proj)


def _sb_bwd(proj, ct, cnt, do, name):
    nb, s, _ = proj.shape
    tq, tk = min(256, s), 128
    ratio = tq // tk

    def body(q_ref, k_ref, v_ref, ct_ref, cnt_ref, do_ref, dq_ref, dk_ref, dv_ref):
        i = pl.program_id(2)

        @pl.when(i == 0)
        def _():
            dk_ref[...] = jnp.zeros_like(dk_ref)
            dv_ref[...] = jnp.zeros_like(dv_ref)

        lane = _iota((tq, LANES), 1)
        lo = lane < HEAD
        lok = _iota((tk, LANES), 1) < HEAD
        qv, dov = q_ref[...], do_ref[...]
        qb, dob = qv.astype(MXU), dov.astype(MXU)
        q0 = jnp.where(lo, qv, 0.0).astype(MXU)
        q1 = jnp.where(lo, 0.0, qv).astype(MXU)
        do0 = jnp.where(lo, dov, 0.0).astype(MXU)
        do1 = jnp.where(lo, 0.0, dov).astype(MXU)
        ctv = ct_ref[...]
        ct0 = _rsum(jnp.where(lane == 0, ctv, 0.0))
        ct1 = _rsum(jnp.where(lane == LANES - 1, ctv, 0.0))
        uincl = (_iota((tk, tk), 0) <= _iota((tk, tk), 1)).astype(MXU)
        ustrict = (_iota((tk, tk), 0) < _iota((tk, tk), 1)).astype(MXU)
        tpos = i * tq + _iota((tq, tk), 0)
        scol = _iota((tq, tk), 1)
        nch = (i + 1) * ratio

        def step(j, carry):
            p0, g0, dq0, p1, g1, dq1 = carry
            off = pl.multiple_of(j * tk, tk)
            kc = k_ref[pl.ds(off, tk), :].astype(MXU)
            vc = v_ref[pl.ds(off, tk), :].astype(MXU)
            msk = (scol + j * tk) < tpos

            def head(qm, dom, ctot, pc, gc, dqa):
                z = _dot_nt(qm, kc) * SB_SCALE
                sp = _softplus(z)
                lk = jnp.where(msk, -sp, 0.0)
                lsig = z - sp
                w = jnp.where(msk, jnp.exp(lsig + (ctot - pc - _cumdot(lk, uincl))), 0.0)
                g = w * _dot_nt(dom, vc)
                gpre = gc + _cumdot(g, ustrict)
                sig = jnp.exp(lsig)
                dz = (jnp.where(msk, g * (1.0 - sig) - sig * gpre, 0.0) * SB_SCALE).astype(MXU)
                return (pc + _rsum(lk), gc + _rsum(g), dqa + _dot(dz, kc),
                        _dot_tn(dz, qb), _dot_tn(w.astype(MXU), dob))

            p0, g0, dq0, dk0, dv0 = head(q0, do0, ct0, p0, g0, dq0)
            p1, g1, dq1, dk1, dv1 = head(q1, do1, ct1, p1, g1, dq1)
            dk_ref[pl.ds(off, tk), :] += jnp.where(lok, dk0, dk1)
            dv_ref[pl.ds(off, tk), :] += jnp.where(lok, dv0, dv1)
            return p0, g0, dq0, p1, g1, dq1

        z1 = jnp.zeros((tq, 1), F32)
        za = jnp.zeros((tq, LANES), F32)
        first = nch - jnp.max(cnt_ref[...]).astype(jnp.int32)
        _, _, dq0, _, _, dq1 = lax.fori_loop(first, nch, step, (z1, z1, za, z1, z1, za))
        dq_ref[...] = jnp.where(lo, dq0, dq1)

    kv = lambda col: pl.BlockSpec((None, s, LANES), lambda b, p, i: (b, 0, col // LANES + p))
    tile = pl.BlockSpec((None, tq, LANES), lambda b, p, i: (b, i, p))
    acc = pl.BlockSpec((None, s, LANES), lambda b, p, i: (b, 0, p))
    return pl.pallas_call(
        body, name=name, grid=(nb, 2, s // tq),
        in_specs=[pl.BlockSpec((None, tq, LANES), lambda b, p, i: (b, i, COL_SBQ // LANES + p)),
                  kv(COL_SBK), kv(COL_SBV), tile,
                  pl.BlockSpec((None, None, None, 8, LANES), lambda b, p, i: (b, p, i, 0, 0)), tile],
        out_specs=[tile, acc, acc],
        out_shape=[jax.ShapeDtypeStruct((nb, s, 256), F32)] * 3,
        compiler_params=_cp("parallel", "parallel", "arbitrary"))(proj, proj, proj, ct, cnt, do)


def _mla_fwd(q, k, v, name):
    nb, s, _ = q.shape
    tq = tk = min(256, s)

    def body(q_ref, k_ref, v_ref, o_ref, lse_ref):
        i = pl.program_id(2)
        q0, q1 = q_ref[:, :LANES], q_ref[:, LANES:]
        krow = _iota((tk, tq), 0)
        qcol = _iota((tk, tq), 1)

        def step(j, carry, diagonal):
            m0, l0, a0, m1, l1, a1 = carry
            off = pl.multiple_of(j * tk, tk)
            vc = v_ref[pl.ds(off, tk), :]

            def head(qh, kh, m, l, a):
                st = _dot_nt(kh, qh) * MLA_SCALE
                if diagonal:
                    st = jnp.where(krow <= qcol, st, NEG)
                mn = jnp.maximum(m, jnp.max(st, axis=0, keepdims=True))
                al = jnp.exp(m - mn)
                pt = jnp.exp(st - mn)
                return mn, al * l + _csum(pt), al * a + _dot_tn(vc, pt.astype(MXU))

            m0, l0, a0 = head(q0, k_ref[pl.ds(off, tk), :LANES], m0, l0, a0)
            m1, l1, a1 = head(q1, k_ref[pl.ds(off, tk), LANES:], m1, l1, a1)
            return m0, l0, a0, m1, l1, a1

        mi = jnp.full((1, tq), NEG, F32)
        z1 = jnp.zeros((1, tq), F32)
        za = jnp.zeros((LANES, tq), F32)
        carry = lax.fori_loop(0, i, lambda j, cr: step(j, cr, False), (mi, z1, za, mi, z1, za))
        m0, l0, a0, m1, l1, a1 = step(i, carry, True)
        lo_rows = _iota((LANES, tq), 0) < HEAD
        o_ref[...] = jnp.where(lo_rows, a0 / l0, a1 / l1).T
        lse_ref[...] = jnp.zeros_like(lse_ref)
        lse_ref[0:1, :] = m0 + jnp.log(l0)
        lse_ref[1:2, :] = m1 + jnp.log(l1)

    tile = pl.BlockSpec((None, tq, LANES), lambda b, p, i: (b, i, p))
    return pl.pallas_call(
        body, name=name, grid=(nb, MLA_HEADS // 2, s // tq),
        in_specs=[pl.BlockSpec((None, tq, 2 * LANES), lambda b, p, i: (b, i, p)),
                  pl.BlockSpec((None, s, 2 * LANES), lambda b, p, i: (b, 0, p)),
                  pl.BlockSpec((None, s, LANES), lambda b, p, i: (b, 0, p))],
        out_specs=[tile, pl.BlockSpec((None, None, 8, tq), lambda b, p, i: (b, p, 0, i))],
        out_shape=[jax.ShapeDtypeStruct((nb, s, 384), F32), jax.ShapeDtypeStruct((nb, MLA_HEADS // 2, 8, s), F32)],
        compiler_params=_cp("parallel", "parallel", "arbitrary"))(q, k, v)


def _mla_bwd(q, k, v, o, lse, do, name):
    nb, s, _ = q.shape
    tq = tk = min(256, s)

    def body(q_ref, k_ref, v_ref, o_ref, lse_ref, do_ref, dq_ref, dk_ref, dv_ref):
        i = pl.program_id(2)

        @pl.when(i == 0)
        def _():
            dk_ref[...] = jnp.zeros_like(dk_ref)
            dv_ref[...] = jnp.zeros_like(dv_ref)

        lo = _iota((tq, LANES), 1) < HEAD
        lok = _iota((tk, LANES), 1) < HEAD
        q0, q1 = q_ref[:, :LANES], q_ref[:, LANES:]
        dov = do_ref[...]
        dob = dov.astype(MXU)
        do0 = jnp.where(lo, dov, 0.0).astype(MXU)
        do1 = jnp.where(lo, 0.0, dov).astype(MXU)
        dd = dov * o_ref[...]
        hi = dd.astype(MXU)
        r1 = dd - hi.astype(F32)
        mid = r1.astype(MXU)
        low = (r1 - mid.astype(F32)).astype(MXU)
        sel_lane = _iota((8, LANES), 1) < HEAD
        sel0 = sel_lane.astype(MXU)
        sel1 = (~sel_lane).astype(MXU)
        dl0 = (_dot_nt(sel0, hi) + _dot_nt(sel0, mid) + _dot_nt(sel0, low))[0:1, :]
        dl1 = (_dot_nt(sel1, hi) + _dot_nt(sel1, mid) + _dot_nt(sel1, low))[0:1, :]
        ls0, ls1 = lse_ref[0:1, :], lse_ref[1:2, :]
        krow = _iota((tk, tq), 0)
        qcol = _iota((tk, tq), 1)

        def step(j, carry, diagonal):
            dq0, dq1 = carry
            off = pl.multiple_of(j * tk, tk)
            vc = v_ref[pl.ds(off, tk), :]

            def head(qh, kh, dom, ls, dl, dqa):
                st = _dot_nt(kh, qh) * MLA_SCALE
                if diagonal:
                    st = jnp.where(krow <= qcol, st, NEG)
                pt = jnp.exp(st - ls)
                dst = (pt * (_dot_nt(vc, dom) - dl) * MLA_SCALE).astype(MXU)
                return dqa + _dot_tn(kh, dst), _dot(dst, qh), _dot(pt.astype(MXU), dob)

            dq0, dk0, dv0 = head(q0, k_ref[pl.ds(off, tk), :LANES], do0, ls0, dl0, dq0)
            dq1, dk1, dv1 = head(q1, k_ref[pl.ds(off, tk), LANES:], do1, ls1, dl1, dq1)
            dk_ref[pl.ds(off, tk), :LANES] += dk0
            dk_ref[pl.ds(off, tk), LANES:] += dk1
            dv_ref[pl.ds(off, tk), :] += jnp.where(lok, dv0, dv1)
            return dq0, dq1

        za = jnp.zeros((LANES, tq), F32)
        carry = lax.fori_loop(0, i, lambda j, cr: step(j, cr, False), (za, za))
        dq0, dq1 = step(i, carry, True)
        dq_ref[:, :LANES] = dq0.T
        dq_ref[:, LANES:] = dq1.T

    tile = pl.BlockSpec((None, tq, LANES), lambda b, p, i: (b, i, p))
    tile2 = pl.BlockSpec((None, tq, 2 * LANES), lambda b, p, i: (b, i, p))
    return pl.pallas_call(
        body, name=name, grid=(nb, MLA_HEADS // 2, s // tq),
        in_specs=[tile2,
                  pl.BlockSpec((None, s, 2 * LANES), lambda b, p, i: (b, 0, p)),
                  pl.BlockSpec((None, s, LANES), lambda b, p, i: (b, 0, p)),
                  tile, pl.BlockSpec((None, None, 8, tq), lambda b, p, i: (b, p, 0, i)), tile],
        out_specs=[tile2,
                   pl.BlockSpec((None, s, 2 * LANES), lambda b, p, i: (b, 0, p)),
                   pl.BlockSpec((None, s, LANES), lambda b, p, i: (b, 0, p))],
        out_shape=[jax.ShapeDtypeStruct((nb, s, 768), F32), jax.ShapeDtypeStruct((nb, s, 768), F32),
                   jax.ShapeDtypeStruct((nb, s, 384), F32)],
        compiler_params=_cp("parallel", "parallel", "arbitrary"))(q, k, v, o, lse, do)


def _half_stats(xv, lo):
    x2 = xv * xv
    s0 = _rsum(jnp.where(lo, x2, 0.0))
    s1 = _rsum(jnp.where(lo, 0.0, x2))
    return jnp.where(lo, lax.rsqrt(s0 / HEAD + EPS), lax.rsqrt(s1 / HEAD + EPS))


def _half_mean(xv, lo):
    s0 = _rsum(jnp.where(lo, xv, 0.0))
    s1 = _rsum(jnp.where(lo, 0.0, xv))
    return jnp.where(lo, s0, s1) / HEAD


def _swa_in_specs():
    def band(col, prev):
        if prev:
            return pl.BlockSpec((None, BLOCK, LANES), lambda b, n: (b, jnp.maximum(n - 1, 0), col // LANES))
        return pl.BlockSpec((None, BLOCK, LANES), lambda b, n: (b, n, col // LANES))

    full = lambda shape: pl.BlockSpec(shape, lambda b, n: tuple(0 for _ in shape))
    return [pl.BlockSpec((None, BLOCK, 384), lambda b, n: (b, n, COL_SWQ // 384)),
            band(COL_SWK, False), band(COL_SWK, True), band(COL_SWV, False), band(COL_SWV, True),
            full((1, LANES)), full((1, LANES)), full((8, LANES)), full((SW_HEADS, BLOCK, 2 * BLOCK))]


def _swa_valid(n):
    a = _iota((BLOCK, 2 * BLOCK), 0)
    bcol = _iota((BLOCK, 2 * BLOCK), 1)
    dist = BLOCK + a - bcol
    return (dist >= 0) & (dist < BLOCK) & ((n > 0) | (bcol >= BLOCK))


def _swa_fwd(proj, gq, gk, sinks, bias, name):
    nb, s, _ = proj.shape

    def body(q_ref, kc_ref, kp_ref, vc_ref, vp_ref, gq_ref, gk_ref, sk_ref, bias_ref, o_ref):
        n = pl.program_id(1)
        lo = _iota((BLOCK, LANES), 1) < HEAD
        lo2 = _iota((2 * BLOCK, LANES), 1) < HEAD
        kband = jnp.concatenate([kp_ref[...], kc_ref[...]], axis=0).astype(F32)
        vband = jnp.concatenate([vp_ref[...], vc_ref[...]], axis=0).astype(F32)
        kn = kband * _half_stats(kband, lo2) * gk_ref[...]
        ks = (kn.astype(MXU), pltpu.roll(kn, HEAD, 1).astype(MXU))
        vs = (vband.astype(MXU), pltpu.roll(vband, HEAD, 1).astype(MXU))
        valid = _swa_valid(n)
        for blk in range(SW_HEADS // 2):
            qv = q_ref[:, blk * LANES:(blk + 1) * LANES].astype(F32)
            qn = qv * _half_stats(qv, lo) * gq_ref[...]
            outs = []
            for half in range(2):
                h = 2 * blk + half
                swap = 0 if half == h // 3 else 1
                qm = jnp.where(lo if half == 0 else ~lo, qn, 0.0).astype(MXU)
                sc = jnp.where(valid, _dot_nt(qm, ks[swap]) * SW_SCALE + bias_ref[h], NEG)
                sk = jnp.max(sk_ref[h:h + 1, :], axis=-1, keepdims=True)
                m = jnp.maximum(jnp.max(sc, axis=-1, keepdims=True), sk)
                p = jnp.exp(sc - m)
                l = _rsum(p) + jnp.exp(sk - m)
                outs.append(_dot((p / l).astype(MXU), vs[swap]))
            o_ref[:, blk * LANES:(blk + 1) * LANES] = jnp.where(lo, outs[0], outs[1])

    return pl.pallas_call(
        body, name=name, grid=(nb, s // BLOCK), in_specs=_swa_in_specs(),
        out_specs=pl.BlockSpec((None, BLOCK, 384), lambda b, n: (b, n, 0)),
        out_shape=jax.ShapeDtypeStruct((nb, s, 384), F32),
        compiler_params=_cp("parallel", "parallel"))(proj, proj, proj, proj, proj, gq, gk, sinks, bias)


def _swa_bwd(proj, gq, gk, sinks, bias, do, name):
    nb, s, _ = proj.shape

    def body(q_ref, kc_ref, kp_ref, vc_ref, vp_ref, gq_ref, gk_ref, sk_ref, bias_ref, do_ref,
             dq_ref, dkc_ref, dkp_ref, dvc_ref, dvp_ref, dbias_ref, dsk_ref, dgq_ref, dgk_ref):
        n = pl.program_id(1)

        @pl.when((pl.program_id(0) == 0) & (n == 0))
        def _():
            for r in (dbias_ref, dsk_ref, dgq_ref, dgk_ref):
                r[...] = jnp.zeros_like(r)

        lo = _iota((BLOCK, LANES), 1) < HEAD
        lo2 = _iota((2 * BLOCK, LANES), 1) < HEAD
        kband = jnp.concatenate([kp_ref[...], kc_ref[...]], axis=0).astype(F32)
        vband = jnp.concatenate([vp_ref[...], vc_ref[...]], axis=0).astype(F32)
        rk = _half_stats(kband, lo2)
        khat = kband * rk
        gkv = gk_ref[...]
        kn = khat * gkv
        ks = (kn.astype(MXU), pltpu.roll(kn, HEAD, 1).astype(MXU))
        vs = (vband.astype(MXU), pltpu.roll(vband, HEAD, 1).astype(MXU))
        valid = _swa_valid(n)
        dkn = jnp.zeros((2 * BLOCK, LANES), F32)
        dvb = jnp.zeros((2 * BLOCK, LANES), F32)
        gqv = gq_ref[...]
        dgq = jnp.zeros((1, LANES), F32)
        for blk in range(SW_HEADS // 2):
            bs = slice(blk * LANES, (blk + 1) * LANES)
            qv = q_ref[:, bs].astype(F32)
            rq = _half_stats(qv, lo)
            qhat = qv * rq
            qn = qhat * gqv
            dov = do_ref[:, bs]
            dqn = jnp.zeros((BLOCK, LANES), F32)
            for half in range(2):
                h = 2 * blk + half
                swap = 0 if half == h // 3 else 1
                hm = lo if half == 0 else ~lo
                qm = jnp.where(hm, qn, 0.0).astype(MXU)
                dom = jnp.where(hm, dov, 0.0).astype(MXU)
                sc = jnp.where(valid, _dot_nt(qm, ks[swap]) * SW_SCALE + bias_ref[h], NEG)
                sk = jnp.max(sk_ref[h:h + 1, :], axis=-1, keepdims=True)
                m = jnp.maximum(jnp.max(sc, axis=-1, keepdims=True), sk)
                e = jnp.exp(sc - m)
                es = jnp.exp(sk - m)
                l = _rsum(e) + es
                p = e / l
                dp = _dot_nt(dom, vs[swap])
                delta = _rsum(p * dp)
                ds = p * (dp - delta)
                dsk_ref[h:h + 1, :] += jnp.broadcast_to(_csum(-(es / l) * delta), (1, LANES))
                dbias_ref[h] += ds
                dsb = (ds * SW_SCALE).astype(MXU)
                dqn = dqn + jnp.where(hm, _dot(dsb, ks[swap]), 0.0)
                rk_ = _dot_tn(dsb, qm)
                rv_ = _dot_tn(p.astype(MXU), dom)
                if swap:
                    rk_ = pltpu.roll(rk_, HEAD, 1)
                    rv_ = pltpu.roll(rv_, HEAD, 1)
                dkn = dkn + rk_
                dvb = dvb + rv_
            dgq = dgq + _csum(dqn * qhat)
            dyq = dqn * gqv
            dq_ref[:, bs] = rq * (dyq - qhat * _half_mean(dyq * qhat, lo))
        dgq_ref[...] += dgq
        dgk_ref[...] += _csum(dkn * khat)
        dyk = dkn * gkv
        dkb = rk * (dyk - khat * _half_mean(dyk * khat, lo2))
        dkp_ref[...] = dkb[:BLOCK]
        dkc_ref[...] = dkb[BLOCK:]
        dvp_ref[...] = dvb[:BLOCK]
        dvc_ref[...] = dvb[BLOCK:]

    full = lambda shape: pl.BlockSpec(shape, lambda b, n: tuple(0 for _ in shape))
    tile = pl.BlockSpec((None, BLOCK, LANES), lambda b, n: (b, n, 0))
    tile3 = pl.BlockSpec((None, BLOCK, 384), lambda b, n: (b, n, 0))
    kvs = jax.ShapeDtypeStruct((nb, s, LANES), F32)
    return pl.pallas_call(
        body, name=name, grid=(nb, s // BLOCK), in_specs=_swa_in_specs() + [tile3],
        out_specs=[tile3, tile, tile, tile, tile, full((SW_HEADS, BLOCK, 2 * BLOCK)), full((8, LANES)),
                   full((1, LANES)), full((1, LANES))],
        out_shape=[jax.ShapeDtypeStruct((nb, s, 384), F32), kvs, kvs, kvs, kvs,
                   jax.ShapeDtypeStruct((SW_HEADS, BLOCK, 2 * BLOCK), F32), jax.ShapeDtypeStruct((8, LANES), F32),
                   jax.ShapeDtypeStruct((1, LANES), F32), jax.ShapeDtypeStruct((1, LANES), F32)],
        compiler_params=_cp("arbitrary", "arbitrary"))(proj, proj, proj, proj, proj, gq, gk, sinks, bias, do)


def _bias_build(table, bucket, name):
    def body(tb_ref, bk_ref, o_ref):
        bk = bk_ref[...]
        tb = tb_ref[...]
        row = _iota((8, LANES), 0)
        col = _iota((8, LANES), 1)
        for h in range(SW_HEADS):
            acc = jnp.zeros((BLOCK, 2 * BLOCK), F32)
            for t in range(REL_BUCKETS):
                val = jnp.sum(jnp.where((row == h) & (col == t), tb, 0.0), keepdims=True)
                acc = jnp.where(bk == t, val, acc)
            o_ref[h] = acc

    return pl.pallas_call(
        body, name=name, out_shape=jax.ShapeDtypeStruct((SW_HEADS, BLOCK, 2 * BLOCK), F32))(table, bucket)


def _bias_grad(dbias, bucket, name):
    def body(db_ref, bk_ref, o_ref):
        bk = bk_ref[...]
        row = _iota((8, LANES), 0)
        col = _iota((8, LANES), 1)
        res = jnp.zeros((8, LANES), F32)
        for h in range(SW_HEADS):
            dbh = db_ref[h]
            for t in range(REL_BUCKETS):
                val = jnp.sum(jnp.where(bk == t, dbh, 0.0), keepdims=True)
                res = jnp.where((row == h) & (col == t), val, res)
        o_ref[...] = res

    return pl.pallas_call(body, name=name, out_shape=jax.ShapeDtypeStruct((8, LANES), F32))(dbias, bucket)


def _loss_grad(y, target, name):
    nb, s, d = y.shape
    tm = min(512, s)

    def body(y_ref, t_ref, loss_ref, dy_ref):
        @pl.when((pl.program_id(0) == 0) & (pl.program_id(1) == 0))
        def _():
            loss_ref[...] = jnp.zeros_like(loss_ref)

        e = y_ref[...] - t_ref[...]
        dy_ref[...] = e / d
        loss_ref[...] += 0.5 * jnp.sum(_rsum(e * e) / d, keepdims=True)

    tile = pl.BlockSpec((None, tm, d), lambda b, i: (b, i, 0))
    return pl.pallas_call(
        body, name=name, grid=(nb, s // tm), in_specs=[tile, tile],
        out_specs=[pl.BlockSpec((8, LANES), lambda b, i: (0, 0)), tile],
        out_shape=[jax.ShapeDtypeStruct((8, LANES), F32), jax.ShapeDtypeStruct((nb, s, d), F32)],
        compiler_params=_cp("arbitrary", "arbitrary"))(y, target)


def _adamw(parts, w, m, v, name):
    npart, r, ncol = parts.shape
    tr = _row_tile(r, ncol)
    bc1 = 1.0 - ADAM_B1 ** ADAM_STEP
    bc2 = 1.0 - ADAM_B2 ** ADAM_STEP

    def body(p_ref, w_ref, m_ref, v_ref, g_ref, d_ref, nm_ref, nv_ref):
        g = p_ref[0].astype(F32)
        for k in range(1, npart):
            g = g + p_ref[k].astype(F32)
        mn = ADAM_B1 * m_ref[...] + (1.0 - ADAM_B1) * g
        vn = ADAM_B2 * v_ref[...] + (1.0 - ADAM_B2) * (g * g)
        g_ref[...] = g
        nm_ref[...] = mn
        nv_ref[...] = vn
        d_ref[...] = -ADAM_LR * ((mn / bc1) / (jnp.sqrt(vn / bc2) + ADAM_EPS) + ADAM_WD * w_ref[...])

    tile = pl.BlockSpec((tr, ncol), lambda i: (i, 0))
    return pl.pallas_call(
        body, name=name, grid=(r // tr,),
        in_specs=[pl.BlockSpec((npart, tr, ncol), lambda i: (0, i, 0)), tile, tile, tile],
        out_specs=[tile] * 4, out_shape=[jax.ShapeDtypeStruct((r, ncol), F32)] * 4,
        compiler_params=_cp("parallel"))(parts, w, m, v)


def _unpack(flat, shapes, lead=()):
    out, off = [], 0
    for shp in shapes:
        size = 1
        for dim in shp:
            size *= dim
        out.append(flat[..., off:off + size].reshape(lead + tuple(shp)))
        off += size
    return out


def _t5_bucket():
    a = jnp.arange(BLOCK)[:, None]
    b = jnp.arange(2 * BLOCK)[None, :]
    dist = BLOCK + a - b
    max_exact = REL_BUCKETS // 2
    nn = jnp.maximum(dist, 0)
    nf = jnp.maximum(nn, 1).astype(F32)
    large = max_exact + (jnp.log(nf / max_exact) / math.log(BLOCK / max_exact)
                         * (REL_BUCKETS - max_exact)).astype(jnp.int32)
    large = jnp.minimum(large, REL_BUCKETS - 1)
    return jnp.where(nn < max_exact, nn, large).astype(jnp.int32)


def _pad_lanes(g, n):
    return jnp.pad(g, (0, n - g.shape[0])).reshape(1, n)


def kernel(x, c, positions, rel_table, norm1_g, norm2_g, w_ada, b_ada, w_in, mla_cq_g, w_uq, mla_ckv_g, w_ukv, mla_qn_g, mla_kn_g, sw_qn_g, sw_kn_g, sw_sinks, w_out, w_up, conv_w, conv_b, w_down, loss_target, m_rel_table, m_norm1_g, m_norm2_g, m_w_ada, m_b_ada, m_w_in, m_mla_cq_g, m_w_uq, m_mla_ckv_g, m_w_ukv, m_mla_qn_g, m_mla_kn_g, m_sw_qn_g, m_sw_kn_g, m_sw_sinks, m_w_out, m_w_up, m_conv_w, m_conv_b, m_w_down, v_rel_table, v_norm1_g, v_norm2_g, v_w_ada, v_b_ada, v_w_in, v_mla_cq_g, v_w_uq, v_mla_ckv_g, v_w_ukv, v_mla_qn_g, v_mla_kn_g, v_sw_qn_g, v_sw_kn_g, v_sw_sinks, v_w_out, v_w_up, v_conv_w, v_conv_b, v_w_down):
    nb, s, d = x.shape
    nl = norm1_g.shape[0]
    me = 4 * lax.axis_index("x") + 2 * lax.axis_index("y") + lax.axis_index("c")
    n_ada = w_ada.shape[2]

    rows2d = lambda a: a.reshape(-1, a.shape[-1])
    tr = lambda a: jnp.swapaxes(a, -1, -2)
    local = [rows2d(tr(w).astype(MXU)) for w in (w_in, w_uq, w_ukv, w_up)]
    local += [rows2d(w.astype(MXU)) for w in (w_out, w_down)] + [rows2d(conv_w), c]
    got = _all_gather(local, "gather_inputs")
    stack_rows = lambda a: a.reshape(N_DEV, nl, -1, a.shape[-1]).transpose(1, 0, 2, 3).reshape(nl, -1, a.shape[-1])
    w_in_t, w_uq_t, w_ukv_t, w_up_t, w_out_f, w_down_f = [stack_rows(a) for a in got[:6]]
    conv_full = got[6].reshape(N_DEV, nl, 3, -1).transpose(1, 2, 0, 3).reshape(nl, 3, -1)
    c_all = got[7].reshape(N_DEV * nb, d)
    zrows = lambda n: jnp.zeros((nl, n, d), MXU)
    w_in_pt = jnp.concatenate([w_in_t[:, :1152], w_in_t[:, 1184:1824], zrows(64), w_in_t[:, 1152:1184], zrows(160)], axis=1)
    w_uq_pt = jnp.pad(w_uq_t.reshape(nl, MLA_HEADS, MLA_QK, 256), ((0, 0), (0, 0), (0, LANES - MLA_QK), (0, 0))).reshape(nl, 768, 256)

    b_my = lax.dynamic_slice_in_dim(b_ada, me * n_ada, n_ada, axis=1).reshape(nl, 1, n_ada)
    mods_my = _ada_fwd(c_all, w_ada, b_my, "ada_fwd")
    mods, = _all_gather([mods_my.reshape(nl * N_DEV * nb, n_ada)], "gather_mods")
    mods = mods.reshape(N_DEV, nl, N_DEV * nb, n_ada).transpose(1, 2, 0, 3).reshape(nl, N_DEV * nb, N_DEV * n_ada)
    mods = lax.dynamic_slice_in_dim(mods, me * nb, nb, axis=1)
    shift1, scale1, gate1, shift2, scale2, gate2 = [mods[:, :, k * d:(k + 1) * d].reshape(nl, nb, 1, d) for k in range(6)]

    half = 16
    inv_freq = jnp.power(ROPE_THETA, -jnp.arange(half, dtype=F32) / half)
    ang = positions.astype(F32)[..., None] * inv_freq
    ones = lambda n: jnp.ones((nb, s, n), F32)
    zeros = lambda n: jnp.zeros((nb, s, n), F32)
    rope_c = jnp.concatenate([ones(64), jnp.cos(ang), jnp.cos(ang), ones(32)], axis=-1)
    rope_s = jnp.concatenate([zeros(64), jnp.sin(ang), jnp.sin(ang), zeros(32)], axis=-1)
    bucket = _t5_bucket()
    bias = _bias_build(jnp.pad(rel_table.T, ((0, 8 - SW_HEADS), (0, LANES - REL_BUCKETS))), bucket, "rel_bias")

    row = lambda g: g.reshape(1, -1)
    twice = lambda g: jnp.concatenate([g, g]).reshape(1, LANES)

    saved = []
    xl = x
    for l in range(nl):
        proj, h1 = _ln_mod_matmul(xl, row(norm1_g[l]), scale1[l], shift1[l], w_in_pt[l], f"l{l}_in_proj")
        prep_args = (proj, rope_c, rope_s, row(mla_cq_g[l]), row(mla_ckv_g[l]), _pad_lanes(mla_qn_g[l], LANES),
                     _pad_lanes(mla_kn_g[l], LANES), w_uq_pt[l], w_ukv_t[l])
        qm, km, vm = _mla_prep(*prep_args, f"l{l}_mla_prep")
        o_a, ct_a, cnt_a = _sb_fwd(proj, f"l{l}_sb_fwd")
        o_b, lse_b = _mla_fwd(qm, km, vm, f"l{l}_mla_fwd")
        sinks = jnp.broadcast_to(jnp.pad(sw_sinks[l], (0, 2))[:, None], (8, LANES))
        swa_args = (proj, twice(sw_qn_g[l]), twice(sw_kn_g[l]), sinks, bias)
        o_c = _swa_fwd(*swa_args, f"l{l}_swa_fwd")
        wo = [w_out_f[l, :256], w_out_f[l, 256:640], w_out_f[l, 640:]]
        x_mid, y1 = _out_proj([o_a, o_b, o_c], wo, gate1[l], xl, f"l{l}_out_proj")
        u_pre, h2 = _ln_mod_matmul(x_mid, row(norm2_g[l]), scale2[l], shift2[l], w_up_t[l], f"l{l}_up_proj")
        x_out, y2 = _conv_gate_matmul(u_pre, conv_full[l], row(conv_b[l]), w_down_f[l], gate2[l], x_mid, f"l{l}_ffn_down")
        saved.append(dict(x=xl, proj=proj, h1=h1, prep=prep_args, qkv=(qm, km, vm), o_a=o_a, ct_a=ct_a, cnt_a=cnt_a, o_b=o_b, lse_b=lse_b,
                          swa=swa_args, o_c=o_c, wo=wo, y1=y1, x_mid=x_mid, u_pre=u_pre, h2=h2, y2=y2))
        xl = x_out

    loss_blk, dx = _loss_grad(xl, loss_target, "loss")
    loss = lax.psum(loss_blk[0, 0], ("x", "y", "c"))

    t = nb * s
    flat = lambda a: a.reshape(t, a.shape[-1])
    grads = [None] * nl
    dmods = [None] * nl
    sharded_out = [None] * nl
    sharded_names = ["w_in", "w_uq", "w_ukv", "w_up", "w_out", "w_down", "conv_w"]
    sharded_wmv = dict(w_in=(w_in, m_w_in, v_w_in), w_uq=(w_uq, m_w_uq, v_w_uq), w_ukv=(w_ukv, m_w_ukv, v_w_ukv),
                       w_up=(w_up, m_w_up, v_w_up), w_out=(w_out, m_w_out, v_w_out), w_down=(w_down, m_w_down, v_w_down),
                       conv_w=(conv_w, m_conv_w, v_conv_w))
    n_in, n_up, n_out, n_dn = w_in.shape[2], w_up.shape[2], w_out.shape[1], w_down.shape[1]
    small_sizes = [w_uq[0].size, w_ukv[0].size, conv_w[0].size]
    n_small_rows = -(-sum(small_sizes) // d)
    rows_used = n_in + n_out + n_small_rows
    rows_grad = -(-rows_used // 16) * 16

    def pack_rows(mats, vecs):
        lead = mats[0].shape[:-2]
        flat_part = jnp.concatenate(vecs, axis=-1)
        flat_part = jnp.pad(flat_part, [(0, 0)] * len(lead) + [(0, n_small_rows * d - flat_part.shape[-1])])
        tail = jnp.zeros(lead + (rows_grad - rows_used, d), F32)
        return jnp.concatenate(list(mats) + [flat_part.reshape(lead + (n_small_rows, d)), tail], axis=-2)

    def unpack_rows(a):
        o1, o2 = n_in, n_in + n_out
        flat_part = a[o2:o2 + n_small_rows].reshape(-1)
        s1, s2, s3 = small_sizes[0], small_sizes[0] + small_sizes[1], sum(small_sizes)
        return dict(w_in=a[:o1].T, w_out=a[o1:o2],
                    w_uq=flat_part[:s1].reshape(w_uq.shape[2], -1).T, w_ukv=flat_part[s1:s2].reshape(w_ukv.shape[2], -1).T,
                    conv_w=flat_part[s2:s3].reshape(conv_w.shape[1:]))

    dbias = jnp.zeros((SW_HEADS, BLOCK, 2 * BLOCK), F32)
    for l in reversed(range(nl)):
        sv = saved[l]
        (da,), dy2, dgate2 = _gate_bwd_nt(dx, sv["y2"], gate2[l], [w_down_f[l]], f"l{l}_ffn_down_bwd")
        du, a_act, cstats = _conv_gate_bwd(da, sv["u_pre"], conv_full[l], row(conv_b[l]), f"l{l}_conv_gate_bwd")
        dx_mid, du_pre, dshift2, dscale2, dg2 = _ln_mod_matmul_bwd(
            du, w_up_t[l], sv["x_mid"], row(norm2_g[l]), scale2[l], dx, conv_full[l], f"l{l}_up_proj_bwd")
        g_w_down = _wgrad(flat(a_act), flat(dy2), f"l{l}_w_down_grad")
        g_w_up_t = _wgrad(flat(du_pre), flat(sv["h2"]), f"l{l}_w_up_grad")

        (do_a, do_b, do_c), dy1, dgate1 = _gate_bwd_nt(dx_mid, sv["y1"], gate1[l], sv["wo"], f"l{l}_out_proj_bwd")
        mix = jnp.concatenate([sv["o_a"], sv["o_b"], sv["o_c"]], axis=-1).astype(MXU)
        g_w_out = _wgrad(flat(mix), flat(dy1), f"l{l}_w_out_grad")

        dsb_q, dsb_k, dsb_v = _sb_bwd(sv["proj"], sv["ct_a"], sv["cnt_a"], do_a, f"l{l}_sb_bwd")
        qm, km, vm = sv["qkv"]
        dqm, dkm, dvm = _mla_bwd(qm, km, vm, sv["o_b"], sv["lse_b"], do_b, f"l{l}_mla_bwd")
        dsw_q, dkc, dkp, dvc, dvp, dbias_l, dsinks, dg_swq, dg_swk = _swa_bwd(*sv["swa"], do_c, f"l{l}_swa_bwd")
        dbias = dbias + dbias_l
        shift_up = lambda a: jnp.concatenate([a[:, BLOCK:], jnp.zeros((nb, BLOCK, LANES), F32)], axis=1)
        dsw_k = dkc + shift_up(dkp)
        dsw_v = dvc + shift_up(dvp)
        dcq, dckv, dkr, g_w_uq_pt, g_w_ukv_t, dg_cq, dg_ckv, dg_qn, dg_kn = _mla_prep_bwd(
            *sv["prep"], dqm, dkm, dvm, f"l{l}_mla_prep_bwd")
        dproj = jnp.concatenate([dsb_q, dsb_k, dsb_v, dcq, dckv, dsw_q, dsw_k, dsw_v, dkr, zeros(128)], axis=-1)
        dx, dproj_m, dshift1, dscale1, dg1 = _ln_mod_matmul_bwd(
            dproj, w_in_pt[l], sv["x"], row(norm1_g[l]), scale1[l], dx_mid, None, f"l{l}_in_proj_bwd")
        g_w_in_pt = _wgrad(flat(dproj_m), flat(sv["h1"]), f"l{l}_w_in_grad")

        g_w_in_t = jnp.concatenate([g_w_in_pt[:1152], g_w_in_pt[1856:1888], g_w_in_pt[1152:1792]], axis=0)
        g_w_uq_t = g_w_uq_pt.reshape(MLA_HEADS, LANES, 256)[:, :MLA_QK].reshape(MLA_HEADS * MLA_QK, 256)
        dmods[l] = jnp.concatenate([dshift1, dscale1, dgate1, dshift2, dscale2, dgate2], axis=-1).reshape(nb, 6 * d)

        per_dev = lambda g: g.reshape(N_DEV, -1, d)
        conv_dev = cstats[1:4].reshape(3, N_DEV, -1).transpose(1, 0, 2)
        rest = pack_rows([per_dev(g_w_in_t), per_dev(g_w_out)],
                         [g_w_uq_t.reshape(N_DEV, -1), g_w_ukv_t.reshape(N_DEV, -1), conv_dev.reshape(N_DEV, -1)])
        send = [per_dev(g_w_up_t), per_dev(g_w_down), rest]
        sib = _pair_exchange(send, f"l{l}_pair_exchange")
        core = lax.axis_index("c").reshape(1).astype(jnp.int32)
        pair = [_pair_add(core, a, b, f"l{l}_pair_add_{k}") for a, b, k in zip(send, sib, ("up", "down", "rest"))]
        recv = _chip_exchange(pair, f"l{l}_chip_exchange")
        wmv = [{k: v[o][l] for k, v in sharded_wmv.items()} for o in range(3)]
        res_up = _adamw(recv[0], *[a["w_up"].T for a in wmv], f"l{l}_adamw_up")
        res_dn = _adamw(recv[1], *[a["w_down"] for a in wmv], f"l{l}_adamw_down")
        res_rest = _adamw(recv[2], *[pack_rows([a["w_in"].T, a["w_out"]], [a["w_uq"].T.reshape(-1), a["w_ukv"].T.reshape(-1),
                                                                            a["conv_w"].reshape(-1)]) for a in wmv],
                          f"l{l}_adamw_rest")
        sharded_out[l] = [dict(unpack_rows(rr), w_up=ru.T, w_down=rd) for ru, rd, rr in zip(res_up, res_dn, res_rest)]
        grads[l] = dict(
            norm1_g=dg1[0], norm2_g=dg2[0], mla_cq_g=dg_cq[0], mla_ckv_g=dg_ckv[0], mla_qn_g=dg_qn[0, :MLA_QK],
            mla_kn_g=dg_kn[0, :MLA_QK], sw_qn_g=dg_swq[0, :HEAD] + dg_swq[0, HEAD:], sw_kn_g=dg_swk[0, :HEAD] + dg_swk[0, HEAD:],
            sw_sinks=dsinks[:SW_HEADS, 0], conv_b=cstats[0])
    grad_x = dx
    g_rel = _bias_grad(dbias, bucket, "rel_table_grad")[:SW_HEADS, :REL_BUCKETS].T
    stack = lambda k: jnp.stack([grads[l][k] for l in range(nl)])

    dm_all, = _all_gather([jnp.stack(dmods).reshape(nl * nb, 6 * d)], "gather_dmods")
    dm_all = dm_all.reshape(N_DEV, nl, nb, 6 * d).transpose(1, 0, 2, 3).reshape(nl, N_DEV * nb, 6 * d)
    dm_my = lax.dynamic_slice_in_dim(dm_all, me * n_ada, n_ada, axis=2)
    g_w_ada, g_b_ada = _ada_bwd(c_all, dm_my, dm_all, "ada_bwd")
    g_b_ada = g_b_ada.reshape(nl, 6 * d)

    big_out = [{k: jnp.stack([sharded_out[l][o][k] for l in range(nl)]) for k in sharded_names} for o in range(4)]
    packf = lambda dct, names, rows: jnp.pad(jnp.concatenate([dct[k].reshape(-1) for k in names]),
                                             (0, rows * LANES - sum(dct[k].size for k in names))).reshape(rows, LANES)

    small_names = ["rel_table", "norm1_g", "norm2_g", "mla_cq_g", "mla_ckv_g", "mla_qn_g", "mla_kn_g",
                   "sw_qn_g", "sw_kn_g", "sw_sinks", "conv_b"]
    small_w = dict(rel_table=rel_table, norm1_g=norm1_g, norm2_g=norm2_g, mla_cq_g=mla_cq_g, mla_ckv_g=mla_ckv_g,
                   mla_qn_g=mla_qn_g, mla_kn_g=mla_kn_g, sw_qn_g=sw_qn_g, sw_kn_g=sw_kn_g, sw_sinks=sw_sinks, conv_b=conv_b)
    small_m = dict(rel_table=m_rel_table, norm1_g=m_norm1_g, norm2_g=m_norm2_g, mla_cq_g=m_mla_cq_g, mla_ckv_g=m_mla_ckv_g,
                   mla_qn_g=m_mla_qn_g, mla_kn_g=m_mla_kn_g, sw_qn_g=m_sw_qn_g, sw_kn_g=m_sw_kn_g, sw_sinks=m_sw_sinks, conv_b=m_conv_b)
    small_v = dict(rel_table=v_rel_table, norm1_g=v_norm1_g, norm2_g=v_norm2_g, mla_cq_g=v_mla_cq_g, mla_ckv_g=v_mla_ckv_g,
                   mla_qn_g=v_mla_qn_g, mla_kn_g=v_mla_kn_g, sw_qn_g=v_sw_qn_g, sw_kn_g=v_sw_kn_g, sw_sinks=v_sw_sinks, conv_b=v_conv_b)
    small_g = {k: (g_rel if k == "rel_table" else stack(k)) for k in small_names}
    n_small = sum(small_w[k].size for k in small_names)
    rows_small = -(-n_small // (8 * LANES)) * 8
    small_parts, = _all_gather([packf(small_g, small_names, rows_small)], "gather_small_grads")
    small_out = _adamw(small_parts, packf(small_w, small_names, rows_small), packf(small_m, small_names, rows_small),
                       packf(small_v, small_names, rows_small), "adamw_replicated")
    small_out = [dict(zip(small_names, _unpack(o.reshape(-1), [small_w[k].shape for k in small_names]))) for o in small_out]

    two_d = lambda a: a.reshape(-1, a.shape[-1])
    res_w = _adamw(two_d(g_w_ada)[None], two_d(w_ada), two_d(m_w_ada), two_d(v_w_ada), "adamw_w_ada")
    res_b = _adamw(g_b_ada[None], b_ada, m_b_ada, v_b_ada, "adamw_b_ada")
    ada_out = [dict(w_ada=rw.reshape(w_ada.shape), b_ada=rb) for rw, rb in zip(res_w, res_b)]

    order = ["rel_table", "norm1_g", "norm2_g", "w_ada", "b_ada", "w_in", "mla_cq_g", "w_uq", "mla_ckv_g", "w_ukv",
             "mla_qn_g", "mla_kn_g", "sw_qn_g", "sw_kn_g", "sw_sinks", "w_out", "w_up", "conv_w", "conv_b", "w_down"]
    outs = [{**big_out[k], **small_out[k], **ada_out[k]} for k in range(4)]
    return (loss, grad_x, *[outs[0][n] for n in order], *[outs[1][n] for n in order],
            *[outs[2][n] for n in order], *[outs[3][n] for n in order])
```

```python
import math

import jax
import jax.numpy as jnp
from jax import lax
from jax.experimental import pallas as pl
from jax.experimental.pallas import tpu as pltpu

F32 = jnp.float32
MXU = jnp.bfloat16
EPS = 1e-6
NEG = -1e30
VMEM_LIMIT_BYTES = 56 * 1024 * 1024
N_DEV = 8
MESH = pl.DeviceIdType.MESH

D_MODEL = 1024
D_FF = 2816
HEAD = 64
LANES = 128
MLA_HEADS = 6
MLA_QK = 96
SW_HEADS = 6
REL_BUCKETS = 32
BLOCK = 128
SB_SCALE = HEAD ** -0.5
SB_DEAD = -105.0
SW_SCALE = HEAD ** -0.5
MLA_SCALE = MLA_QK ** -0.5
ROPE_THETA = 10000.0
D_IN_PAD = 2048
COL_SBQ, COL_SBK, COL_SBV, COL_CQ, COL_CKV, COL_SWQ, COL_SWK, COL_SWV, COL_KR = 0, 256, 512, 768, 1024, 1152, 1536, 1664, 1792

HALO = 16
ROW_TILE_BYTES = 1 << 20
ADAM_LR, ADAM_B1, ADAM_B2, ADAM_EPS, ADAM_WD, ADAM_STEP = 0.001, 0.9, 0.999, 1e-08, 0.01, 10


def _cp(*sem):
    return pltpu.CompilerParams(dimension_semantics=sem, vmem_limit_bytes=VMEM_LIMIT_BYTES)


def _iota(shape, dim):
    return lax.broadcasted_iota(jnp.int32, shape, dim)


def _dot(a, b):
    return jnp.dot(a, b, preferred_element_type=F32)


def _dot_nt(a, b):
    return lax.dot_general(a, b, (((1,), (1,)), ((), ())), preferred_element_type=F32)


def _dot_tn(a, b):
    return lax.dot_general(a, b, (((0,), (0,)), ((), ())), preferred_element_type=F32)


def _cumdot(x, u):
    hi = x.astype(MXU)
    mid = (x - hi.astype(F32)).astype(MXU)
    return _dot(hi, u) + _dot(mid, u)


def _sigmoid(x):
    return 1.0 / (1.0 + jnp.exp(-x))


def _rsum(x):
    return jnp.sum(x, axis=-1, keepdims=True)


def _csum(x):
    return jnp.sum(x, axis=0, keepdims=True)


def _all_gather(xs, name):
    na = len(xs)

    def body(*refs):
        x_refs, out_refs = refs[:na], refs[na:2 * na]
        send_sems, recv_sems, local_sems = refs[2 * na:]
        x, y, c = lax.axis_index("x"), lax.axis_index("y"), lax.axis_index("c")
        me, sibling = (x, y, c), (x, y, 1 - c)
        chips = [(1 - x, y), (x, 1 - y), (1 - x, 1 - y)]

        def slot(a, px, py, pc):
            return out_refs[a].at[4 * px + 2 * py + pc]

        def copy(a, k, block, to, src=None):
            return pltpu.make_async_remote_copy(
                src_ref=slot(a, *block) if src is None else src, dst_ref=slot(a, *block),
                send_sem=send_sems.at[7 * a + k], recv_sem=recv_sems.at[7 * a + k], device_id=to, device_id_type=MESH)

        mines, sends = [], []
        for a in range(na):
            mines.append(pltpu.make_async_copy(x_refs[a], slot(a, *me), local_sems.at[a]))
            mines[-1].start()
            first = [copy(a, 0, me, sibling, src=x_refs[a])]
            first += [copy(a, 1 + j, me, (*chip, c), src=x_refs[a]) for j, chip in enumerate(chips)]
            for cp in first:
                cp.start()
            sends += first
        for j, chip in enumerate(chips):
            for a in range(na):
                copy(a, 1 + j, (*chip, c), me).wait_recv()
                sends.append(copy(a, 4 + j, (*chip, c), sibling))
                sends[-1].start()
        for a in range(na):
            copy(a, 0, sibling, me).wait_recv()
            for j, chip in enumerate(chips):
                copy(a, 4 + j, (*chip, 1 - c), me).wait_recv()
        for cp in sends:
            cp.wait_send()
        for mine in mines:
            mine.wait()

    hbm = pl.BlockSpec(memory_space=pl.ANY)
    return pl.pallas_call(
        body, name=name, out_shape=[jax.ShapeDtypeStruct((N_DEV,) + a.shape, a.dtype) for a in xs],
        in_specs=[hbm] * na, out_specs=[hbm] * na,
        scratch_shapes=[pltpu.SemaphoreType.DMA((7 * na,)), pltpu.SemaphoreType.DMA((7 * na,)),
                        pltpu.SemaphoreType.DMA((na,))],
    )(*xs)


def _pair_exchange(xs, name):
    na = len(xs)

    def body(*refs):
        x_refs, out_refs = refs[:na], refs[na:2 * na]
        send_sems, recv_sems = refs[2 * na:]
        x, y, c = lax.axis_index("x"), lax.axis_index("y"), lax.axis_index("c")
        copies = []
        for a in range(na):
            for q in range(4):
                copies.append(pltpu.make_async_remote_copy(
                    src_ref=x_refs[a].at[2 * q + 1 - c], dst_ref=out_refs[a].at[q],
                    send_sem=send_sems.at[4 * a + q], recv_sem=recv_sems.at[4 * a + q],
                    device_id=(x, y, 1 - c), device_id_type=MESH))
                copies[-1].start()
        for cp in copies:
            cp.wait()

    hbm = pl.BlockSpec(memory_space=pl.ANY)
    return pl.pallas_call(
        body, name=name, out_shape=[jax.ShapeDtypeStruct((4,) + a.shape[1:], a.dtype) for a in xs],
        in_specs=[hbm] * na, out_specs=[hbm] * na,
        scratch_shapes=[pltpu.SemaphoreType.DMA((4 * na,)), pltpu.SemaphoreType.DMA((4 * na,))])(*xs)


def _row_tile(r, ncol):
    if r * ncol * 4 <= ROW_TILE_BYTES:
        return r
    return max(t for t in range(16, r, 16) if r % t == 0 and t * ncol * 4 <= ROW_TILE_BYTES)


def _pair_add(core, xs, sib, name):
    _, r, ncol = xs.shape
    tr = _row_tile(r, ncol)

    def body(c_ref, x_ref, s_ref, o_ref):
        o_ref[...] = (x_ref[...] + s_ref[...]).astype(MXU)

    return pl.pallas_call(
        body, name=name,
        grid_spec=pltpu.PrefetchScalarGridSpec(
            num_scalar_prefetch=1, grid=(4, r // tr),
            in_specs=[pl.BlockSpec((None, tr, ncol), lambda q, i, c_ref: (2 * q + c_ref[0], i, 0)),
                      pl.BlockSpec((None, tr, ncol), lambda q, i, c_ref: (q, i, 0))],
            out_specs=pl.BlockSpec((None, tr, ncol), lambda q, i, c_ref: (q, i, 0))),
        out_shape=jax.ShapeDtypeStruct((4, r, ncol), MXU),
        compiler_params=_cp("parallel", "parallel"))(core, xs, sib)


def _chip_exchange(xs, name):
    na = len(xs)

    def body(*refs):
        copies = _chip_exchange_copies(refs[:na], refs[na:2 * na], *refs[2 * na:])
        for cp in copies:
            cp.start()
        for cp in copies:
            cp.wait()

    hbm = pl.BlockSpec(memory_space=pl.ANY)
    return pl.pallas_call(
        body, name=name, out_shape=[jax.ShapeDtypeStruct(a.shape, a.dtype) for a in xs],
        in_specs=[hbm] * na, out_specs=[hbm] * na, scratch_shapes=_chip_exchange_sems(na))(*xs)


def _chip_exchange_sems(na):
    return [pltpu.SemaphoreType.DMA((3 * na,)), pltpu.SemaphoreType.DMA((3 * na,)), pltpu.SemaphoreType.DMA((na,))]


def _chip_exchange_copies(x_refs, out_refs, send_sems, recv_sems, local_sems):
    x, y, c = lax.axis_index("x"), lax.axis_index("y"), lax.axis_index("c")
    me = 2 * x + y
    copies = [pltpu.make_async_copy(x_refs[a].at[me], out_refs[a].at[me], local_sems.at[a]) for a in range(len(x_refs))]
    for k, (dx, dy) in enumerate([(1, 0), (0, 1), (1, 1)]):
        px = 1 - x if dx else x
        py = 1 - y if dy else y
        for a in range(len(x_refs)):
            copies.append(pltpu.make_async_remote_copy(
                src_ref=x_refs[a].at[2 * px + py], dst_ref=out_refs[a].at[me],
                send_sem=send_sems.at[3 * a + k], recv_sem=recv_sems.at[3 * a + k],
                device_id=(px, py, c), device_id_type=MESH))
    return copies


def _ada_fwd(c_all, w_ada, b_my, name):
    nl, d, n = w_ada.shape
    nb = c_all.shape[0]

    def body(c_ref, w_ref, b_ref, o_ref):
        cv = c_ref[...]
        sc = (cv * _sigmoid(cv)).astype(MXU)
        o_ref[...] = _dot(sc, w_ref[...].astype(MXU)) + b_ref[...]

    return pl.pallas_call(
        body, name=name, grid=(nl,),
        in_specs=[pl.BlockSpec((nb, d), lambda l: (0, 0)),
                  pl.BlockSpec((None, d, n), lambda l: (l, 0, 0)),
                  pl.BlockSpec((None, 1, n), lambda l: (l, 0, 0))],
        out_specs=pl.BlockSpec((None, nb, n), lambda l: (l, 0, 0)),
        out_shape=jax.ShapeDtypeStruct((nl, nb, n), F32),
        compiler_params=_cp("parallel"))(c_all, w_ada, b_my)


def _ada_bwd(c_all, dmods_my, dmods_all, name):
    nl, nb, n = dmods_my.shape
    d = c_all.shape[1]
    nfull = dmods_all.shape[2]

    def body(c_ref, dm_ref, da_ref, dw_ref, db_ref):
        cv = c_ref[...]
        sc = (cv * _sigmoid(cv)).astype(MXU)
        dw_ref[...] = _dot_tn(sc, dm_ref[...].astype(MXU))
        db_ref[...] = _csum(da_ref[...])

    return pl.pallas_call(
        body, name=name, grid=(nl,),
        in_specs=[pl.BlockSpec((nb, d), lambda l: (0, 0)),
                  pl.BlockSpec((None, nb, n), lambda l: (l, 0, 0)),
                  pl.BlockSpec((None, nb, nfull), lambda l: (l, 0, 0))],
        out_specs=[pl.BlockSpec((None, d, n), lambda l: (l, 0, 0)),
                   pl.BlockSpec((None, 1, nfull), lambda l: (l, 0, 0))],
        out_shape=[jax.ShapeDtypeStruct((nl, d, n), F32), jax.ShapeDtypeStruct((nl, 1, nfull), F32)],
        compiler_params=_cp("parallel"))(c_all, dmods_my, dmods_all)


def _ln_mod_matmul(x, g, scale, shift, w, name):
    nb, s, d = x.shape
    n = w.shape[0]
    tm, tn = min(1024, s), 512

    def body(x_ref, g_ref, sc_ref, sh_ref, w_ref, y_ref, h_ref, h_s):
        @pl.when(pl.program_id(2) == 0)
        def _():
            xf = x_ref[...]
            rstd = lax.rsqrt(jnp.mean(xf * xf, axis=-1, keepdims=True) + EPS)
            hv = (xf * rstd * g_ref[...]) * (1.0 + sc_ref[...]) + sh_ref[...]
            h_s[...] = hv.astype(MXU)
            h_ref[...] = h_s[...]

        y_ref[...] = _dot_nt(h_s[...], w_ref[...]).astype(MXU)

    return pl.pallas_call(
        body, name=name, grid=(nb, s // tm, n // tn),
        in_specs=[pl.BlockSpec((None, tm, d), lambda b, i, j: (b, i, 0)),
                  pl.BlockSpec((1, d), lambda b, i, j: (0, 0)),
                  pl.BlockSpec((None, 1, d), lambda b, i, j: (b, 0, 0)),
                  pl.BlockSpec((None, 1, d), lambda b, i, j: (b, 0, 0)),
                  pl.BlockSpec((tn, d), lambda b, i, j: (j, 0))],
        out_specs=[pl.BlockSpec((None, tm, tn), lambda b, i, j: (b, i, j)),
                   pl.BlockSpec((None, tm, d), lambda b, i, j: (b, i, 0))],
        out_shape=[jax.ShapeDtypeStruct((nb, s, n), MXU), jax.ShapeDtypeStruct((nb, s, d), MXU)],
        scratch_shapes=[pltpu.VMEM((tm, d), MXU)],
        compiler_params=_cp("parallel", "parallel", "arbitrary"))(x, g, scale, shift, w)


def _ln_mod_matmul_bwd(dy, w, x, g, scale, dres, conv_w, name, riding=()):
    nb, s, n = dy.shape
    d = x.shape[-1]
    tm, tn = min(512, s), 512
    ni, nj = s // tm, n // tn
    hb = tm // HALO
    conv = conv_w is not None
    na = len(riding)

    def body(*refs):
        if conv:
            dy_ref, nx_ref, cw_ref = refs[:3]
            refs = refs[3:]
        else:
            dy_ref = refs[0]
            refs = refs[1:]
        w_ref, x_ref, g_ref, sc_ref, dr_ref = refs[:5]
        ride_in, refs = refs[5:5 + na], refs[5 + na:]
        dx_ref, dyp_ref, dsh_ref, dsc_ref, dg_ref = refs[:5]
        ride_out, refs = refs[5:5 + na], refs[5 + na:]
        acc = refs[0]
        b, i, j = pl.program_id(0), pl.program_id(1), pl.program_id(2)
        if na:
            copies = _chip_exchange_copies(ride_in, ride_out, *refs[1:])

            @pl.when((b == 0) & (i == 0) & (j == 0))
            def _():
                for cp in copies:
                    cp.start()

        @pl.when(j == 0)
        def _():
            acc[...] = jnp.zeros_like(acc)

        @pl.when((j == 0) & (i == 0))
        def _():
            dsh_ref[...] = jnp.zeros_like(dsh_ref)
            dsc_ref[...] = jnp.zeros_like(dsc_ref)

        @pl.when((j == 0) & (i == 0) & (b == 0))
        def _():
            dg_ref[...] = jnp.zeros_like(dg_ref)

        dv = dy_ref[...].astype(F32)
        if conv:
            rows = _iota((tm, 1), 0)
            nx = jnp.where(i == ni - 1, 0.0, nx_ref[...].astype(F32))
            n1 = jnp.where(rows == tm - 1, nx[0:1, :], pltpu.roll(dv, tm - 1, 0))
            n2 = jnp.where(rows == tm - 2, nx[0:1, :], jnp.where(rows == tm - 1, nx[1:2, :], pltpu.roll(dv, tm - 2, 0)))
            cw = cw_ref[...]
            dv = cw[2:3, :] * dv + cw[1:2, :] * n1 + cw[0:1, :] * n2
        dp = dv.astype(MXU)
        dyp_ref[...] = dp
        acc[...] += _dot(dp, w_ref[...])

        @pl.when(j == nj - 1)
        def _():
            dh = acc[...]
            xf = x_ref[...]
            rstd = lax.rsqrt(jnp.mean(xf * xf, axis=-1, keepdims=True) + EPS)
            xn = xf * rstd
            gg = g_ref[...]
            sc1 = 1.0 + sc_ref[...]
            dsh_ref[...] += _csum(dh)
            dsc_ref[...] += _csum(dh * xn * gg)
            dg_ref[...] += _csum(dh * xn * sc1)
            dn = dh * gg * sc1
            dx_ref[...] = dr_ref[...] + rstd * (dn - xn * jnp.mean(dn * xn, axis=-1, keepdims=True))

        if na:
            @pl.when((b == nb - 1) & (i == ni - 1) & (j == nj - 1))
            def _():
                for cp in copies:
                    cp.wait()

    hbm = pl.BlockSpec(memory_space=pl.ANY)
    in_specs = [pl.BlockSpec((None, tm, tn), lambda b, i, j: (b, i, j))]
    args = [dy]
    if conv:
        in_specs += [pl.BlockSpec((None, HALO, tn), lambda b, i, j: (b, jnp.minimum((i + 1) * hb, s // HALO - 1), j)),
                     pl.BlockSpec((3, tn), lambda b, i, j: (0, j))]
        args += [dy, conv_w]
    in_specs += [pl.BlockSpec((tn, d), lambda b, i, j: (j, 0)),
                 pl.BlockSpec((None, tm, d), lambda b, i, j: (b, i, 0)),
                 pl.BlockSpec((1, d), lambda b, i, j: (0, 0)),
                 pl.BlockSpec((None, 1, d), lambda b, i, j: (b, 0, 0)),
                 pl.BlockSpec((None, tm, d), lambda b, i, j: (b, i, 0))]
    in_specs += [hbm] * na
    args += [w, x, g, scale, dres, *riding]
    return pl.pallas_call(
        body, name=name, grid=(nb, ni, nj), in_specs=in_specs,
        out_specs=[pl.BlockSpec((None, tm, d), lambda b, i, j: (b, i, 0)),
                   pl.BlockSpec((None, tm, tn), lambda b, i, j: (b, i, j)),
                   pl.BlockSpec((None, 1, d), lambda b, i, j: (b, 0, 0)),
                   pl.BlockSpec((None, 1, d), lambda b, i, j: (b, 0, 0)),
                   pl.BlockSpec((1, d), lambda b, i, j: (0, 0))] + [hbm] * na,
        out_shape=[jax.ShapeDtypeStruct((nb, s, d), F32), jax.ShapeDtypeStruct((nb, s, n), MXU),
                   jax.ShapeDtypeStruct((nb, 1, d), F32), jax.ShapeDtypeStruct((nb, 1, d), F32),
                   jax.ShapeDtypeStruct((1, d), F32)] + [jax.ShapeDtypeStruct(a.shape, a.dtype) for a in riding],
        scratch_shapes=[pltpu.VMEM((tm, d), F32)] + (_chip_exchange_sems(na) if na else []),
        compiler_params=_cp("arbitrary", "arbitrary", "arbitrary"))(*args)


def _wgrad(xm, dym, name):
    t, k = xm.shape
    n = dym.shape[1]
    tk = 1408 if k % 1408 == 0 else 1024
    tt = min(512, t)

    def body(x_ref, dy_ref, o_ref):
        @pl.when(pl.program_id(1) == 0)
        def _():
            o_ref[...] = jnp.zeros_like(o_ref)

        o_ref[...] += _dot_tn(x_ref[...], dy_ref[...])

    return pl.pallas_call(
        body, name=name, grid=(k // tk, t // tt),
        in_specs=[pl.BlockSpec((tt, tk), lambda a, c: (c, a)),
                  pl.BlockSpec((tt, n), lambda a, c: (c, 0))],
        out_specs=pl.BlockSpec((tk, n), lambda a, c: (a, 0)),
        out_shape=jax.ShapeDtypeStruct((k, n), F32),
        compiler_params=_cp("parallel", "arbitrary"))(xm, dym)


def _out_proj(parts, ws, gate, res, name):
    nb, s, d = res.shape
    tm = min(512, s)
    npart = len(parts)

    def body(*refs):
        p_refs, w_refs = refs[:npart], refs[npart:2 * npart]
        gt_ref, res_ref, xo_ref, y_ref = refs[2 * npart:]
        y = _dot(p_refs[0][...].astype(MXU), w_refs[0][...])
        for p_ref, w_ref in zip(p_refs[1:], w_refs[1:]):
            y = y + _dot(p_ref[...].astype(MXU), w_ref[...])
        y_ref[...] = y
        xo_ref[...] = res_ref[...] + gt_ref[...] * y

    in_specs = [pl.BlockSpec((None, tm, p.shape[-1]), lambda b, i: (b, i, 0)) for p in parts]
    in_specs += [pl.BlockSpec(w.shape, lambda b, i: (0, 0)) for w in ws]
    in_specs += [pl.BlockSpec((None, 1, d), lambda b, i: (b, 0, 0)),
                 pl.BlockSpec((None, tm, d), lambda b, i: (b, i, 0))]
    return pl.pallas_call(
        body, name=name, grid=(nb, s // tm), in_specs=in_specs,
        out_specs=[pl.BlockSpec((None, tm, d), lambda b, i: (b, i, 0))] * 2,
        out_shape=[jax.ShapeDtypeStruct((nb, s, d), F32)] * 2,
        compiler_params=_cp("parallel", "parallel"))(*parts, *ws, gate, res)


def _gate_bwd_nt(dx, y, gate, ws, name):
    nb, s, d = dx.shape
    tm = min(256, s)
    npart = len(ws)

    def body(*refs):
        dx_ref, y_ref, gt_ref = refs[:3]
        w_refs = refs[3:3 + npart]
        da_refs = refs[3 + npart:3 + 2 * npart]
        dy_ref, dgt_ref = refs[3 + 2 * npart:]

        @pl.when(pl.program_id(1) == 0)
        def _():
            dgt_ref[...] = jnp.zeros_like(dgt_ref)

        dxv = dx_ref[...]
        dyv = (dxv * gt_ref[...]).astype(MXU)
        dy_ref[...] = dyv
        dgt_ref[...] += _csum(dxv * y_ref[...])
        for w_ref, da_ref in zip(w_refs, da_refs):
            da_ref[...] = _dot_nt(dyv, w_ref[...])

    tile = pl.BlockSpec((None, tm, d), lambda b, i: (b, i, 0))
    row = pl.BlockSpec((None, 1, d), lambda b, i: (b, 0, 0))
    outs = pl.pallas_call(
        body, name=name, grid=(nb, s // tm),
        in_specs=[tile, tile, row] + [pl.BlockSpec(w.shape, lambda b, i: (0, 0)) for w in ws],
        out_specs=[pl.BlockSpec((None, tm, w.shape[0]), lambda b, i: (b, i, 0)) for w in ws] + [tile, row],
        out_shape=[jax.ShapeDtypeStruct((nb, s, w.shape[0]), F32) for w in ws]
        + [jax.ShapeDtypeStruct((nb, s, d), MXU), jax.ShapeDtypeStruct((nb, 1, d), F32)],
        compiler_params=_cp("arbitrary", "arbitrary"))(dx, y, gate, *ws)
    return outs[:npart], outs[npart], outs[npart + 1]


def _conv_shifts(xv, halo, rows):
    last, before = halo[HALO - 1:HALO, :], halo[HALO - 2:HALO - 1, :]
    p1 = jnp.where(rows == 0, last, pltpu.roll(xv, 1, 0))
    p2 = jnp.where(rows == 0, before, jnp.where(rows == 1, last, pltpu.roll(xv, 2, 0)))
    return p1, p2


def _conv_gate_matmul(u, cw, cb, wd, gate, res, name):
    nb, s, f2 = u.shape
    f = f2 // 2
    d = wd.shape[1]
    tm = min(512, s)
    tk = f // 2
    nk = f // tk
    hb = tm // HALO

    def body(ug_ref, uv_ref, hg_ref, hv_ref, cwg_ref, cwv_ref, cbg_ref, cbv_ref, wd_ref, gt_ref, res_ref,
             xo_ref, y_ref, acc):
        i, k = pl.program_id(1), pl.program_id(2)

        @pl.when(k == 0)
        def _():
            acc[...] = jnp.zeros_like(acc)

        rows = _iota((tm, 1), 0)

        def conv(x_ref, h_ref, w_ref, b_ref):
            xv = x_ref[...].astype(F32)
            halo = jnp.where(i == 0, 0.0, h_ref[...].astype(F32))
            p1, p2 = _conv_shifts(xv, halo, rows)
            wv = w_ref[...]
            return wv[2:3, :] * xv + wv[1:2, :] * p1 + wv[0:1, :] * p2 + b_ref[...]

        gv = conv(ug_ref, hg_ref, cwg_ref, cbg_ref)
        vv = conv(uv_ref, hv_ref, cwv_ref, cbv_ref)
        av = gv * _sigmoid(gv) * vv
        acc[...] += _dot(av.astype(MXU), wd_ref[...])

        @pl.when(k == nk - 1)
        def _():
            y = acc[...]
            y_ref[...] = y
            xo_ref[...] = res_ref[...] + gt_ref[...] * y

    def halo_idx(off):
        return lambda b, i, k: (b, jnp.maximum(i * hb - 1, 0), k + off)

    tile = pl.BlockSpec((None, tm, d), lambda b, i, k: (b, i, 0))
    return pl.pallas_call(
        body, name=name, grid=(nb, s // tm, nk),
        in_specs=[pl.BlockSpec((None, tm, tk), lambda b, i, k: (b, i, k)),
                  pl.BlockSpec((None, tm, tk), lambda b, i, k: (b, i, k + nk)),
                  pl.BlockSpec((None, HALO, tk), halo_idx(0)),
                  pl.BlockSpec((None, HALO, tk), halo_idx(nk)),
                  pl.BlockSpec((3, tk), lambda b, i, k: (0, k)),
                  pl.BlockSpec((3, tk), lambda b, i, k: (0, k + nk)),
                  pl.BlockSpec((1, tk), lambda b, i, k: (0, k)),
                  pl.BlockSpec((1, tk), lambda b, i, k: (0, k + nk)),
                  pl.BlockSpec((tk, d), lambda b, i, k: (k, 0)),
                  pl.BlockSpec((None, 1, d), lambda b, i, k: (b, 0, 0)),
                  tile],
        out_specs=[tile, tile],
        out_shape=[jax.ShapeDtypeStruct((nb, s, d), F32)] * 2,
        scratch_shapes=[pltpu.VMEM((tm, d), F32)],
        compiler_params=_cp("parallel", "parallel", "arbitrary"))(u, u, u, u, cw, cw, cb, cb, wd, gate, res)


def _conv_gate_bwd(da, u, cw, cb, name):
    nb, s, f2 = u.shape
    f = f2 // 2
    tm = min(128, s)
    hb = tm // HALO

    def body(da_ref, u_ref, h_ref, cw_ref, cb_ref, du_ref, a_ref, st_ref):
        b, i = pl.program_id(0), pl.program_id(1)

        @pl.when((b == 0) & (i == 0))
        def _():
            st_ref[...] = jnp.zeros_like(st_ref)

        rows = _iota((tm, 1), 0)
        first = i == 0

        def conv(cs):
            xv = u_ref[:, cs].astype(F32)
            halo = jnp.where(first, 0.0, h_ref[:, cs].astype(F32))
            p1, p2 = _conv_shifts(xv, halo, rows)
            wv = cw_ref[:, cs]
            return xv, p1, p2, wv[2:3, :] * xv + wv[1:2, :] * p1 + wv[0:1, :] * p2 + cb_ref[:, cs]

        def stats(cs, du, xv, p1, p2):
            du_ref[:, cs] = du.astype(MXU)
            st_ref[0:1, cs] += _csum(du)
            st_ref[1:2, cs] += _csum(du * p2)
            st_ref[2:3, cs] += _csum(du * p1)
            st_ref[3:4, cs] += _csum(du * xv)

        for k in range(f // LANES):
            cg = slice(k * LANES, (k + 1) * LANES)
            cv = slice(f + k * LANES, f + (k + 1) * LANES)
            xg, g1, g2, gv = conv(cg)
            xv, v1, v2, vv = conv(cv)
            sg = _sigmoid(gv)
            sl = gv * sg
            a_ref[:, cg] = (sl * vv).astype(MXU)
            dav = da_ref[:, cg]
            stats(cg, dav * vv * (sg * (1.0 + gv * (1.0 - sg))), xg, g1, g2)
            stats(cv, dav * sl, xv, v1, v2)

    return pl.pallas_call(
        body, name=name, grid=(nb, s // tm),
        in_specs=[pl.BlockSpec((None, tm, f), lambda b, i: (b, i, 0)),
                  pl.BlockSpec((None, tm, f2), lambda b, i: (b, i, 0)),
                  pl.BlockSpec((None, HALO, f2), lambda b, i: (b, jnp.maximum(i * hb - 1, 0), 0)),
                  pl.BlockSpec((3, f2), lambda b, i: (0, 0)),
                  pl.BlockSpec((1, f2), lambda b, i: (0, 0))],
        out_specs=[pl.BlockSpec((None, tm, f2), lambda b, i: (b, i, 0)),
                   pl.BlockSpec((None, tm, f), lambda b, i: (b, i, 0)),
                   pl.BlockSpec((8, f2), lambda b, i: (0, 0))],
        out_shape=[jax.ShapeDtypeStruct((nb, s, f2), MXU), jax.ShapeDtypeStruct((nb, s, f), MXU),
                   jax.ShapeDtypeStruct((8, f2), F32)],
        compiler_params=_cp("arbitrary", "arbitrary"))(da, u, u, cw, cb)


def _rot(xv, lane):
    return jnp.where((lane >= 64) & (lane < 80), -pltpu.roll(xv, 112, 1),
                     jnp.where((lane >= 80) & (lane < 96), pltpu.roll(xv, 16, 1), 0.0))


def _rot_t(dv, lane):
    return jnp.where((lane >= 80) & (lane < 96), -pltpu.roll(dv, 16, 1),
                     jnp.where((lane >= 64) & (lane < 80), pltpu.roll(dv, 112, 1), 0.0))


def _mla_prep_specs(s, tm):
    def blk(width, col):
        return pl.BlockSpec((None, tm, width), lambda b, i: (b, i, col // width))

    full = lambda shape: pl.BlockSpec(shape, lambda b, i: (0, 0))
    return [blk(256, COL_CQ), blk(128, COL_CKV), blk(128, COL_KR),
            pl.BlockSpec((None, tm, LANES), lambda b, i: (b, i, 0)),
            pl.BlockSpec((None, tm, LANES), lambda b, i: (b, i, 0)),
            full((1, 256)), full((1, 128)), full((1, 128)), full((1, 128)),
            full((768, 256)), full((768, 128))]


def _mla_prep(proj, cs, sn, gcq, gckv, gqn, gkn, wuq, wukv, name):
    nb, s, _ = proj.shape
    tm = min(256, s)

    def body(cq_ref, ckv_ref, kr_ref, c_ref, s_ref, gcq_ref, gckv_ref, gqn_ref, gkn_ref, wuq_ref, wukv_ref,
             q_ref, k_ref, v_ref):
        lane = _iota((tm, LANES), 1)
        cv, sv = c_ref[...], s_ref[...]
        cq = cq_ref[...].astype(F32)
        cqn = cq * lax.rsqrt(jnp.mean(cq * cq, axis=-1, keepdims=True) + EPS) * gcq_ref[...]
        qb = _dot_nt(cqn.astype(MXU), wuq_ref[...])
        ckv = ckv_ref[...].astype(F32)
        ckvn = ckv * lax.rsqrt(jnp.mean(ckv * ckv, axis=-1, keepdims=True) + EPS) * gckv_ref[...]
        kvb = _dot_nt(ckvn.astype(MXU), wukv_ref[...])
        kr = kr_ref[...].astype(F32)
        for h in range(MLA_HEADS):
            hs = slice(h * LANES, (h + 1) * LANES)
            qh = qb[:, hs]
            qn = qh * lax.rsqrt(_rsum(qh * qh) / MLA_QK + EPS) * gqn_ref[...]
            q_ref[:, hs] = (qn * cv + _rot(qn, lane) * sv).astype(MXU)
            kc = jnp.where(lane < HEAD, kvb[:, hs], kr)
            kn = kc * lax.rsqrt(_rsum(kc * kc) / MLA_QK + EPS) * gkn_ref[...]
            k_ref[:, hs] = (kn * cv + _rot(kn, lane) * sv).astype(MXU)
        for j in range(MLA_HEADS // 2):
            va = kvb[:, (2 * j) * LANES:(2 * j + 1) * LANES]
            vb = kvb[:, (2 * j + 1) * LANES:(2 * j + 2) * LANES]
            v_ref[:, j * LANES:(j + 1) * LANES] = jnp.where(lane < HEAD, pltpu.roll(va, HEAD, 1), vb).astype(MXU)

    return pl.pallas_call(
        body, name=name, grid=(nb, s // tm), in_specs=_mla_prep_specs(s, tm),
        out_specs=[pl.BlockSpec((None, tm, 768), lambda b, i: (b, i, 0)),
                   pl.BlockSpec((None, tm, 768), lambda b, i: (b, i, 0)),
                   pl.BlockSpec((None, tm, 384), lambda b, i: (b, i, 0))],
        out_shape=[jax.ShapeDtypeStruct((nb, s, 768), MXU), jax.ShapeDtypeStruct((nb, s, 768), MXU),
                   jax.ShapeDtypeStruct((nb, s, 384), MXU)],
        compiler_params=_cp("parallel", "parallel"))(proj, proj, proj, cs, sn, gcq, gckv, gqn, gkn, wuq, wukv)


def _mla_prep_bwd(proj, cs, sn, gcq, gckv, gqn, gkn, wuq, wukv, dq, dk, dv, name):
    nb, s, _ = proj.shape
    tm = min(256, s)

    def body(cq_ref, ckv_ref, kr_ref, c_ref, s_ref, gcq_ref, gckv_ref, gqn_ref, gkn_ref, wuq_ref, wukv_ref,
             dq_ref, dk_ref, dv_ref,
             dcq_ref, dckv_ref, dkr_ref, dwuq_ref, dwukv_ref, dgcq_ref, dgckv_ref, dgqn_ref, dgkn_ref,
             dqb_s, dkvb_s):
        @pl.when((pl.program_id(0) == 0) & (pl.program_id(1) == 0))
        def _():
            for r in (dwuq_ref, dwukv_ref, dgcq_ref, dgckv_ref, dgqn_ref, dgkn_ref):
                r[...] = jnp.zeros_like(r)

        lane = _iota((tm, LANES), 1)
        cv, sv = c_ref[...], s_ref[...]
        gqn, gkn = gqn_ref[...], gkn_ref[...]
        cq = cq_ref[...].astype(F32)
        rc = lax.rsqrt(jnp.mean(cq * cq, axis=-1, keepdims=True) + EPS)
        chat = cq * rc
        cqn = (chat * gcq_ref[...]).astype(MXU)
        qb = _dot_nt(cqn, wuq_ref[...])
        ckv = ckv_ref[...].astype(F32)
        rkv = lax.rsqrt(jnp.mean(ckv * ckv, axis=-1, keepdims=True) + EPS)
        kvhat = ckv * rkv
        ckvn = (kvhat * gckv_ref[...]).astype(MXU)
        kvb = _dot_nt(ckvn, wukv_ref[...])
        kr = kr_ref[...].astype(F32)
        dgq = jnp.zeros((1, LANES), F32)
        dgk = jnp.zeros((1, LANES), F32)
        dkr = jnp.zeros((tm, LANES), F32)
        for h in range(MLA_HEADS):
            hs = slice(h * LANES, (h + 1) * LANES)
            qh = qb[:, hs]
            rq = lax.rsqrt(_rsum(qh * qh) / MLA_QK + EPS)
            qhat = qh * rq
            dqr = dq_ref[:, hs]
            dqn = dqr * cv + _rot_t(dqr * sv, lane)
            dgq = dgq + _csum(dqn * qhat)
            dyq = dqn * gqn
            dqb_s[:, hs] = (rq * (dyq - qhat * (_rsum(dyq * qhat) / MLA_QK))).astype(MXU)

            kc = jnp.where(lane < HEAD, kvb[:, hs], kr)
            rk = lax.rsqrt(_rsum(kc * kc) / MLA_QK + EPS)
            khat = kc * rk
            dkr_h = dk_ref[:, hs]
            dkn = dkr_h * cv + _rot_t(dkr_h * sv, lane)
            dgk = dgk + _csum(dkn * khat)
            dyk = dkn * gkn
            dkc = rk * (dyk - khat * (_rsum(dyk * khat) / MLA_QK))
            dkr = dkr + jnp.where(lane >= HEAD, dkc, 0.0)
            dvb = dv_ref[:, (h // 2) * LANES:(h // 2 + 1) * LANES]
            dvp = dvb if h % 2 == 1 else pltpu.roll(dvb, HEAD, 1)
            dkvb_s[:, hs] = jnp.where(lane < HEAD, dkc, dvp).astype(MXU)
        dgqn_ref[...] += dgq
        dgkn_ref[...] += dgk
        dkr_ref[...] = dkr

        dqb = dqb_s[...]
        dwuq_ref[...] += _dot_tn(dqb, cqn)
        dcqn = _dot(dqb, wuq_ref[...])
        dgcq_ref[...] += _csum(dcqn * chat)
        dyc = dcqn * gcq_ref[...]
        dcq_ref[...] = rc * (dyc - chat * jnp.mean(dyc * chat, axis=-1, keepdims=True))

        dkvb = dkvb_s[...]
        dwukv_ref[...] += _dot_tn(dkvb, ckvn)
        dckvn = _dot(dkvb, wukv_ref[...])
        dgckv_ref[...] += _csum(dckvn * kvhat)
        dykv = dckvn * gckv_ref[...]
        dckv_ref[...] = rkv * (dykv - kvhat * jnp.mean(dykv * kvhat, axis=-1, keepdims=True))

    full = lambda shape: pl.BlockSpec(shape, lambda b, i: (0, 0))
    tile = lambda width: pl.BlockSpec((None, tm, width), lambda b, i: (b, i, 0))
    return pl.pallas_call(
        body, name=name, grid=(nb, s // tm),
        in_specs=_mla_prep_specs(s, tm) + [tile(768), tile(768), tile(384)],
        out_specs=[tile(256), tile(128), tile(128), full((768, 256)), full((768, 128)),
                   full((1, 256)), full((1, 128)), full((1, 128)), full((1, 128))],
        out_shape=[jax.ShapeDtypeStruct((nb, s, 256), F32), jax.ShapeDtypeStruct((nb, s, 128), F32),
                   jax.ShapeDtypeStruct((nb, s, 128), F32),
                   jax.ShapeDtypeStruct((768, 256), F32), jax.ShapeDtypeStruct((768, 128), F32),
                   jax.ShapeDtypeStruct((1, 256), F32), jax.ShapeDtypeStruct((1, 128), F32),
                   jax.ShapeDtypeStruct((1, 128), F32), jax.ShapeDtypeStruct((1, 128), F32)],
        scratch_shapes=[pltpu.VMEM((tm, 768), MXU), pltpu.VMEM((tm, 768), MXU)],
        compiler_params=_cp("arbitrary", "arbitrary"))(
            proj, proj, proj, cs, sn, gcq, gckv, gqn, gkn, wuq, wukv, dq, dk, dv)


def _softplus(z):
    return jnp.maximum(z, 0.0) + jnp.log(1.0 + jnp.exp(-jnp.abs(z)))


def _sb_fwd(proj, name):
    nb, s, _ = proj.shape
    tq, tk = min(256, s), 128
    ratio = tq // tk

    def body(q_ref, k_ref, v_ref, o_ref, ct_ref, cnt_ref):
        i = pl.program_id(2)
        lo = _iota((tq, LANES), 1) < HEAD
        qv = q_ref[...]
        q0 = jnp.where(lo, qv, 0.0).astype(MXU)
        q1 = jnp.where(lo, 0.0, qv).astype(MXU)
        usuf = (_iota((tk, tk), 0) > _iota((tk, tk), 1)).astype(MXU)
        tpos = i * tq + _iota((tq, tk), 0)
        scol = _iota((tq, tk), 1)
        nch = (i + 1) * ratio

        def alive(st):
            return (st[0] < nch) & (st[5] > SB_DEAD)

        def step(st):
            t, c0, a0, c1, a1, _ = st
            j = nch - 1 - t
            off = pl.multiple_of(j * tk, tk)
            kc = k_ref[pl.ds(off, tk), :].astype(MXU)
            vc = v_ref[pl.ds(off, tk), :].astype(MXU)
            msk = (scol + j * tk) < tpos

            def head(qm, c, a):
                z = _dot_nt(qm, kc) * SB_SCALE
                sp = _softplus(z)
                lk = jnp.where(msk, -sp, 0.0)
                w = jnp.where(msk, jnp.exp(z - sp + _cumdot(lk, usuf) + c), 0.0)
                return c + _rsum(lk), a + _dot(w.astype(MXU), vc)

            c0, a0 = head(q0, c0, a0)
            c1, a1 = head(q1, c1, a1)
            return t + 1, c0, a0, c1, a1, jnp.maximum(jnp.max(c0), jnp.max(c1))

        z1 = jnp.zeros((tq, 1), F32)
        za = jnp.zeros((tq, LANES), F32)
        t, c0, a0, c1, a1, _ = lax.while_loop(alive, step, (jnp.int32(0), z1, za, z1, za, jnp.float32(0.0)))
        o_ref[...] = jnp.where(lo, a0, a1)
        ct_ref[...] = jnp.where(lo, c0, c1)
        cnt_ref[...] = jnp.zeros((8, LANES), F32) + t.astype(F32)

    kv = lambda col: pl.BlockSpec((None, s, LANES), lambda b, p, i: (b, 0, col // LANES + p))
    tile = pl.BlockSpec((None, tq, LANES), lambda b, p, i: (b, i, p))
    return pl.pallas_call(
        body, name=name, grid=(nb, 2, s // tq),
        in_specs=[pl.BlockSpec((None, tq, LANES), lambda b, p, i: (b, i, COL_SBQ // LANES + p)),
                  kv(COL_SBK), kv(COL_SBV)],
        out_specs=[tile, tile, pl.BlockSpec((None, None, None, 8, LANES), lambda b, p, i: (b, p, i, 0, 0))],
        out_shape=[jax.ShapeDtypeStruct((nb, s, 256), F32)] * 2
        + [jax.ShapeDtypeStruct((nb, 2, s // tq, 8, LANES), F32)],
        compiler_params=_cp("parallel", "parallel", "arbitrary"))(proj, proj, proj)


def _sb_bwd(proj, ct, cnt, do, name):
    nb, s, _ = proj.shape
    tq, tk = min(256, s), 128
    ratio = tq // tk

    def body(q_ref, k_ref, v_ref, ct_ref, cnt_ref, do_ref, dq_ref, dk_ref, dv_ref):
        i = pl.program_id(2)

        @pl.when(i == 0)
        def _():
            dk_ref[...] = jnp.zeros_like(dk_ref)
            dv_ref[...] = jnp.zeros_like(dv_ref)

        lane = _iota((tq, LANES), 1)
        lo = lane < HEAD
        lok = _iota((tk, LANES), 1) < HEAD
        qv, dov = q_ref[...], do_ref[...]
        qb, dob = qv.astype(MXU), dov.astype(MXU)
        q0 = jnp.where(lo, qv, 0.0).astype(MXU)
        q1 = jnp.where(lo, 0.0, qv).astype(MXU)
        do0 = jnp.where(lo, dov, 0.0).astype(MXU)
        do1 = jnp.where(lo, 0.0, dov).astype(MXU)
        ctv = ct_ref[...]
        ct0 = _rsum(jnp.where(lane == 0, ctv, 0.0))
        ct1 = _rsum(jnp.where(lane == LANES - 1, ctv, 0.0))
        uincl = (_iota((tk, tk), 0) <= _iota((tk, tk), 1)).astype(MXU)
        ustrict = (_iota((tk, tk), 0) < _iota((tk, tk), 1)).astype(MXU)
        tpos = i * tq + _iota((tq, tk), 0)
        scol = _iota((tq, tk), 1)
        nch = (i + 1) * ratio

        def step(j, carry):
            p0, g0, dq0, p1, g1, dq1 = carry
            off = pl.multiple_of(j * tk, tk)
            kc = k_ref[pl.ds(off, tk), :].astype(MXU)
            vc = v_ref[pl.ds(off, tk), :].astype(MXU)
            msk = (scol + j * tk) < tpos

            def head(qm, dom, ctot, pc, gc, dqa):
                z = _dot_nt(qm, kc) * SB_SCALE
                sp = _softplus(z)
                lk = jnp.where(msk, -sp, 0.0)
                lsig = z - sp
                w = jnp.where(msk, jnp.exp(lsig + (ctot - pc - _cumdot(lk, uincl))), 0.0)
                g = w * _dot_nt(dom, vc)
                gpre = gc + _cumdot(g, ustrict)
                sig = jnp.exp(lsig)
                dz = (jnp.where(msk, g * (1.0 - sig) - sig * gpre, 0.0) * SB_SCALE).astype(MXU)
                return (pc + _rsum(lk), gc + _rsum(g), dqa + _dot(dz, kc),
                        _dot_tn(dz, qb), _dot_tn(w.astype(MXU), dob))

            p0, g0, dq0, dk0, dv0 = head(q0, do0, ct0, p0, g0, dq0)
            p1, g1, dq1, dk1, dv1 = head(q1, do1, ct1, p1, g1, dq1)
            dk_ref[pl.ds(off, tk), :] += jnp.where(lok, dk0, dk1)
            dv_ref[pl.ds(off, tk), :] += jnp.where(lok, dv0, dv1)
            return p0, g0, dq0, p1, g1, dq1

        z1 = jnp.zeros((tq, 1), F32)
        za = jnp.zeros((tq, LANES), F32)
        first = nch - jnp.max(cnt_ref[...]).astype(jnp.int32)
        _, _, dq0, _, _, dq1 = lax.fori_loop(first, nch, step, (z1, z1, za, z1, z1, za))
        dq_ref[...] = jnp.where(lo, dq0, dq1)

    kv = lambda col: pl.BlockSpec((None, s, LANES), lambda b, p, i: (b, 0, col // LANES + p))
    tile = pl.BlockSpec((None, tq, LANES), lambda b, p, i: (b, i, p))
    acc = pl.BlockSpec((None, s, LANES), lambda b, p, i: (b, 0, p))
    return pl.pallas_call(
        body, name=name, grid=(nb, 2, s // tq),
        in_specs=[pl.BlockSpec((None, tq, LANES), lambda b, p, i: (b, i, COL_SBQ // LANES + p)),
                  kv(COL_SBK), kv(COL_SBV), tile,
                  pl.BlockSpec((None, None, None, 8, LANES), lambda b, p, i: (b, p, i, 0, 0)), tile],
        out_specs=[tile, acc, acc],
        out_shape=[jax.ShapeDtypeStruct((nb, s, 256), F32)] * 3,
        compiler_params=_cp("parallel", "parallel", "arbitrary"))(proj, proj, proj, ct, cnt, do)


def _mla_fwd(q, k, v, name):
    nb, s, _ = q.shape
    tq = tk = min(256, s)

    def body(q_ref, k_ref, v_ref, o_ref, lse_ref):
        i = pl.program_id(2)
        q0, q1 = q_ref[:, :LANES], q_ref[:, LANES:]
        krow = _iota((tk, tq), 0)
        qcol = _iota((tk, tq), 1)

        def step(j, carry, diagonal):
            m0, l0, a0, m1, l1, a1 = carry
            off = pl.multiple_of(j * tk, tk)
            vc = v_ref[pl.ds(off, tk), :]

            def head(qh, kh, m, l, a):
                st = _dot_nt(kh, qh) * MLA_SCALE
                if diagonal:
                    st = jnp.where(krow <= qcol, st, NEG)
                mn = jnp.maximum(m, jnp.max(st, axis=0, keepdims=True))
                al = jnp.exp(m - mn)
                pt = jnp.exp(st - mn)
                return mn, al * l + _csum(pt), al * a + _dot_tn(vc, pt.astype(MXU))

            m0, l0, a0 = head(q0, k_ref[pl.ds(off, tk), :LANES], m0, l0, a0)
            m1, l1, a1 = head(q1, k_ref[pl.ds(off, tk), LANES:], m1, l1, a1)
            return m0, l0, a0, m1, l1, a1

        mi = jnp.full((1, tq), NEG, F32)
        z1 = jnp.zeros((1, tq), F32)
        za = jnp.zeros((LANES, tq), F32)
        carry = lax.fori_loop(0, i, lambda j, cr: step(j, cr, False), (mi, z1, za, mi, z1, za))
        m0, l0, a0, m1, l1, a1 = step(i, carry, True)
        lo_rows = _iota((LANES, tq), 0) < HEAD
        o_ref[...] = jnp.where(lo_rows, a0 / l0, a1 / l1).T
        lse_ref[...] = jnp.zeros_like(lse_ref)
        lse_ref[0:1, :] = m0 + jnp.log(l0)
        lse_ref[1:2, :] = m1 + jnp.log(l1)

    tile = pl.BlockSpec((None, tq, LANES), lambda b, p, i: (b, i, p))
    return pl.pallas_call(
        body, name=name, grid=(nb, MLA_HEADS // 2, s // tq),
        in_specs=[pl.BlockSpec((None, tq, 2 * LANES), lambda b, p, i: (b, i, p)),
                  pl.BlockSpec((None, s, 2 * LANES), lambda b, p, i: (b, 0, p)),
                  pl.BlockSpec((None, s, LANES), lambda b, p, i: (b, 0, p))],
        out_specs=[tile, pl.BlockSpec((None, None, 8, tq), lambda b, p, i: (b, p, 0, i))],
        out_shape=[jax.ShapeDtypeStruct((nb, s, 384), F32), jax.ShapeDtypeStruct((nb, MLA_HEADS // 2, 8, s), F32)],
        compiler_params=_cp("parallel", "parallel", "arbitrary"))(q, k, v)


def _mla_bwd(q, k, v, o, lse, do, name):
    nb, s, _ = q.shape
    tq = tk = min(256, s)

    def body(q_ref, k_ref, v_ref, o_ref, lse_ref, do_ref, dq_ref, dk_ref, dv_ref):
        i = pl.program_id(2)

        @pl.when(i == 0)
        def _():
            dk_ref[...] = jnp.zeros_like(dk_ref)
            dv_ref[...] = jnp.zeros_like(dv_ref)

        lo = _iota((tq, LANES), 1) < HEAD
        lok = _iota((tk, LANES), 1) < HEAD
        q0, q1 = q_ref[:, :LANES], q_ref[:, LANES:]
        dov = do_ref[...]
        dob = dov.astype(MXU)
        do0 = jnp.where(lo, dov, 0.0).astype(MXU)
        do1 = jnp.where(lo, 0.0, dov).astype(MXU)
        dd = dov * o_ref[...]
        hi = dd.astype(MXU)
        r1 = dd - hi.astype(F32)
        mid = r1.astype(MXU)
        low = (r1 - mid.astype(F32)).astype(MXU)
        sel_lane = _iota((8, LANES), 1) < HEAD
        sel0 = sel_lane.astype(MXU)
        sel1 = (~sel_lane).astype(MXU)
        dl0 = (_dot_nt(sel0, hi) + _dot_nt(sel0, mid) + _dot_nt(sel0, low))[0:1, :]
        dl1 = (_dot_nt(sel1, hi) + _dot_nt(sel1, mid) + _dot_nt(sel1, low))[0:1, :]
        ls0, ls1 = lse_ref[0:1, :], lse_ref[1:2, :]
        krow = _iota((tk, tq), 0)
        qcol = _iota((tk, tq), 1)

        def step(j, carry, diagonal):
            dq0, dq1 = carry
            off = pl.multiple_of(j * tk, tk)
            vc = v_ref[pl.ds(off, tk), :]

            def head(qh, kh, dom, ls, dl, dqa):
                st = _dot_nt(kh, qh) * MLA_SCALE
                if diagonal:
                    st = jnp.where(krow <= qcol, st, NEG)
                pt = jnp.exp(st - ls)
                dst = (pt * (_dot_nt(vc, dom) - dl) * MLA_SCALE).astype(MXU)
                return dqa + _dot_tn(kh, dst), _dot(dst, qh), _dot(pt.astype(MXU), dob)

            dq0, dk0, dv0 = head(q0, k_ref[pl.ds(off, tk), :LANES], do0, ls0, dl0, dq0)
            dq1, dk1, dv1 = head(q1, k_ref[pl.ds(off, tk), LANES:], do1, ls1, dl1, dq1)
            dk_ref[pl.ds(off, tk), :LANES] += dk0
            dk_ref[pl.ds(off, tk), LANES:] += dk1
            dv_ref[pl.ds(off, tk), :] += jnp.where(lok, dv0, dv1)
            return dq0, dq1

        za = jnp.zeros((LANES, tq), F32)
        carry = lax.fori_loop(0, i, lambda j, cr: step(j, cr, False), (za, za))
        dq0, dq1 = step(i, carry, True)
        dq_ref[:, :LANES] = dq0.T
        dq_ref[:, LANES:] = dq1.T

    tile = pl.BlockSpec((None, tq, LANES), lambda b, p, i: (b, i, p))
    tile2 = pl.BlockSpec((None, tq, 2 * LANES), lambda b, p, i: (b, i, p))
    return pl.pallas_call(
        body, name=name, grid=(nb, MLA_HEADS // 2, s // tq),
        in_specs=[tile2,
                  pl.BlockSpec((None, s, 2 * LANES), lambda b, p, i: (b, 0, p)),
                  pl.BlockSpec((None, s, LANES), lambda b, p, i: (b, 0, p)),
                  tile, pl.BlockSpec((None, None, 8, tq), lambda b, p, i: (b, p, 0, i)), tile],
        out_specs=[tile2,
                   pl.BlockSpec((None, s, 2 * LANES), lambda b, p, i: (b, 0, p)),
                   pl.BlockSpec((None, s, LANES), lambda b, p, i: (b, 0, p))],
        out_shape=[jax.ShapeDtypeStruct((nb, s, 768), F32), jax.ShapeDtypeStruct((nb, s, 768), F32),
                   jax.ShapeDtypeStruct((nb, s, 384), F32)],
        compiler_params=_cp("parallel", "parallel", "arbitrary"))(q, k, v, o, lse, do)


def _half_stats(xv, lo):
    x2 = xv * xv
    s0 = _rsum(jnp.where(lo, x2, 0.0))
    s1 = _rsum(jnp.where(lo, 0.0, x2))
    return jnp.where(lo, lax.rsqrt(s0 / HEAD + EPS), lax.rsqrt(s1 / HEAD + EPS))


def _half_mean(xv, lo):
    s0 = _rsum(jnp.where(lo, xv, 0.0))
    s1 = _rsum(jnp.where(lo, 0.0, xv))
    return jnp.where(lo, s0, s1) / HEAD


def _swa_in_specs():
    def band(col, prev):
        if prev:
            return pl.BlockSpec((None, BLOCK, LANES), lambda b, n: (b, jnp.maximum(n - 1, 0), col // LANES))
        return pl.BlockSpec((None, BLOCK, LANES), lambda b, n: (b, n, col // LANES))

    full = lambda shape: pl.BlockSpec(shape, lambda b, n: tuple(0 for _ in shape))
    return [pl.BlockSpec((None, BLOCK, 384), lambda b, n: (b, n, COL_SWQ // 384)),
            band(COL_SWK, False), band(COL_SWK, True), band(COL_SWV, False), band(COL_SWV, True),
            full((1, LANES)), full((1, LANES)), full((8, LANES)), full((SW_HEADS, BLOCK, 2 * BLOCK))]


def _swa_valid(n):
    a = _iota((BLOCK, 2 * BLOCK), 0)
    bcol = _iota((BLOCK, 2 * BLOCK), 1)
    dist = BLOCK + a - bcol
    return (dist >= 0) & (dist < BLOCK) & ((n > 0) | (bcol >= BLOCK))


def _swa_fwd(proj, gq, gk, sinks, bias, name):
    nb, s, _ = proj.shape

    def body(q_ref, kc_ref, kp_ref, vc_ref, vp_ref, gq_ref, gk_ref, sk_ref, bias_ref, o_ref):
        n = pl.program_id(1)
        lo = _iota((BLOCK, LANES), 1) < HEAD
        lo2 = _iota((2 * BLOCK, LANES), 1) < HEAD
        kband = jnp.concatenate([kp_ref[...], kc_ref[...]], axis=0).astype(F32)
        vband = jnp.concatenate([vp_ref[...], vc_ref[...]], axis=0).astype(F32)
        kn = kband * _half_stats(kband, lo2) * gk_ref[...]
        ks = (kn.astype(MXU), pltpu.roll(kn, HEAD, 1).astype(MXU))
        vs = (vband.astype(MXU), pltpu.roll(vband, HEAD, 1).astype(MXU))
        valid = _swa_valid(n)
        for blk in range(SW_HEADS // 2):
            qv = q_ref[:, blk * LANES:(blk + 1) * LANES].astype(F32)
            qn = qv * _half_stats(qv, lo) * gq_ref[...]
            outs = []
            for half in range(2):
                h = 2 * blk + half
                swap = 0 if half == h // 3 else 1
                qm = jnp.where(lo if half == 0 else ~lo, qn, 0.0).astype(MXU)
                sc = jnp.where(valid, _dot_nt(qm, ks[swap]) * SW_SCALE + bias_ref[h], NEG)
                sk = jnp.max(sk_ref[h:h + 1, :], axis=-1, keepdims=True)
                m = jnp.maximum(jnp.max(sc, axis=-1, keepdims=True), sk)
                p = jnp.exp(sc - m)
                l = _rsum(p) + jnp.exp(sk - m)
                outs.append(_dot((p / l).astype(MXU), vs[swap]))
            o_ref[:, blk * LANES:(blk + 1) * LANES] = jnp.where(lo, outs[0], outs[1])

    return pl.pallas_call(
        body, name=name, grid=(nb, s // BLOCK), in_specs=_swa_in_specs(),
        out_specs=pl.BlockSpec((None, BLOCK, 384), lambda b, n: (b, n, 0)),
        out_shape=jax.ShapeDtypeStruct((nb, s, 384), F32),
        compiler_params=_cp("parallel", "parallel"))(proj, proj, proj, proj, proj, gq, gk, sinks, bias)


def _swa_bwd(proj, gq, gk, sinks, bias, do, name):
    nb, s, _ = proj.shape

    def body(q_ref, kc_ref, kp_ref, vc_ref, vp_ref, gq_ref, gk_ref, sk_ref, bias_ref, do_ref,
             dq_ref, dkc_ref, dkp_ref, dvc_ref, dvp_ref, dbias_ref, dsk_ref, dgq_ref, dgk_ref):
        n = pl.program_id(1)

        @pl.when((pl.program_id(0) == 0) & (n == 0))
        def _():
            for r in (dbias_ref, dsk_ref, dgq_ref, dgk_ref):
                r[...] = jnp.zeros_like(r)

        lo = _iota((BLOCK, LANES), 1) < HEAD
        lo2 = _iota((2 * BLOCK, LANES), 1) < HEAD
        kband = jnp.concatenate([kp_ref[...], kc_ref[...]], axis=0).astype(F32)
        vband = jnp.concatenate([vp_ref[...], vc_ref[...]], axis=0).astype(F32)
        rk = _half_stats(kband, lo2)
        khat = kband * rk
        gkv = gk_ref[...]
        kn = khat * gkv
        ks = (kn.astype(MXU), pltpu.roll(kn, HEAD, 1).astype(MXU))
        vs = (vband.astype(MXU), pltpu.roll(vband, HEAD, 1).astype(MXU))
        valid = _swa_valid(n)
        dkn = jnp.zeros((2 * BLOCK, LANES), F32)
        dvb = jnp.zeros((2 * BLOCK, LANES), F32)
        gqv = gq_ref[...]
        dgq = jnp.zeros((1, LANES), F32)
        for blk in range(SW_HEADS // 2):
            bs = slice(blk * LANES, (blk + 1) * LANES)
            qv = q_ref[:, bs].astype(F32)
            rq = _half_stats(qv, lo)
            qhat = qv * rq
            qn = qhat * gqv
            dov = do_ref[:, bs]
            dqn = jnp.zeros((BLOCK, LANES), F32)
            for half in range(2):
                h = 2 * blk + half
                swap = 0 if half == h // 3 else 1
                hm = lo if half == 0 else ~lo
                qm = jnp.where(hm, qn, 0.0).astype(MXU)
                dom = jnp.where(hm, dov, 0.0).astype(MXU)
                sc = jnp.where(valid, _dot_nt(qm, ks[swap]) * SW_SCALE + bias_ref[h], NEG)
                sk = jnp.max(sk_ref[h:h + 1, :], axis=-1, keepdims=True)
                m = jnp.maximum(jnp.max(sc, axis=-1, keepdims=True), sk)
                e = jnp.exp(sc - m)
                es = jnp.exp(sk - m)
                l = _rsum(e) + es
                p = e / l
                dp = _dot_nt(dom, vs[swap])
                delta = _rsum(p * dp)
                ds = p * (dp - delta)
                dsk_ref[h:h + 1, :] += jnp.broadcast_to(_csum(-(es / l) * delta), (1, LANES))
                dbias_ref[h] += ds
                dsb = (ds * SW_SCALE).astype(MXU)
                dqn = dqn + jnp.where(hm, _dot(dsb, ks[swap]), 0.0)
                rk_ = _dot_tn(dsb, qm)
                rv_ = _dot_tn(p.astype(MXU), dom)
                if swap:
                    rk_ = pltpu.roll(rk_, HEAD, 1)
                    rv_ = pltpu.roll(rv_, HEAD, 1)
                dkn = dkn + rk_
                dvb = dvb + rv_
            dgq = dgq + _csum(dqn * qhat)
            dyq = dqn * gqv
            dq_ref[:, bs] = rq * (dyq - qhat * _half_mean(dyq * qhat, lo))
        dgq_ref[...] += dgq
        dgk_ref[...] += _csum(dkn * khat)
        dyk = dkn * gkv
        dkb = rk * (dyk - khat * _half_mean(dyk * khat, lo2))
        dkp_ref[...] = dkb[:BLOCK]
        dkc_ref[...] = dkb[BLOCK:]
        dvp_ref[...] = dvb[:BLOCK]
        dvc_ref[...] = dvb[BLOCK:]

    full = lambda shape: pl.BlockSpec(shape, lambda b, n: tuple(0 for _ in shape))
    tile = pl.BlockSpec((None, BLOCK, LANES), lambda b, n: (b, n, 0))
    tile3 = pl.BlockSpec((None, BLOCK, 384), lambda b, n: (b, n, 0))
    kvs = jax.ShapeDtypeStruct((nb, s, LANES), F32)
    return pl.pallas_call(
        body, name=name, grid=(nb, s // BLOCK), in_specs=_swa_in_specs() + [tile3],
        out_specs=[tile3, tile, tile, tile, tile, full((SW_HEADS, BLOCK, 2 * BLOCK)), full((8, LANES)),
                   full((1, LANES)), full((1, LANES))],
        out_shape=[jax.ShapeDtypeStruct((nb, s, 384), F32), kvs, kvs, kvs, kvs,
                   jax.ShapeDtypeStruct((SW_HEADS, BLOCK, 2 * BLOCK), F32), jax.ShapeDtypeStruct((8, LANES), F32),
                   jax.ShapeDtypeStruct((1, LANES), F32), jax.ShapeDtypeStruct((1, LANES), F32)],
        compiler_params=_cp("arbitrary", "arbitrary"))(proj, proj, proj, proj, proj, gq, gk, sinks, bias, do)


def _bias_build(table, bucket, name):
    def body(tb_ref, bk_ref, o_ref):
        bk = bk_ref[...]
        tb = tb_ref[...]
        row = _iota((8, LANES), 0)
        col = _iota((8, LANES), 1)
        for h in range(SW_HEADS):
            acc = jnp.zeros((BLOCK, 2 * BLOCK), F32)
            for t in range(REL_BUCKETS):
                val = jnp.sum(jnp.where((row == h) & (col == t), tb, 0.0), keepdims=True)
                acc = jnp.where(bk == t, val, acc)
            o_ref[h] = acc

    return pl.pallas_call(
        body, name=name, out_shape=jax.ShapeDtypeStruct((SW_HEADS, BLOCK, 2 * BLOCK), F32))(table, bucket)


def _bias_grad(dbias, bucket, name):
    def body(db_ref, bk_ref, o_ref):
        bk = bk_ref[...]
        row = _iota((8, LANES), 0)
        col = _iota((8, LANES), 1)
        res = jnp.zeros((8, LANES), F32)
        for h in range(SW_HEADS):
            dbh = db_ref[h]
            for t in range(REL_BUCKETS):
                val = jnp.sum(jnp.where(bk == t, dbh, 0.0), keepdims=True)
                res = jnp.where((row == h) & (col == t), val, res)
        o_ref[...] = res

    return pl.pallas_call(body, name=name, out_shape=jax.ShapeDtypeStruct((8, LANES), F32))(dbias, bucket)


def _loss_grad(y, target, name):
    nb, s, d = y.shape
    tm = min(512, s)

    def body(y_ref, t_ref, loss_ref, dy_ref):
        @pl.when((pl.program_id(0) == 0) & (pl.program_id(1) == 0))
        def _():
            loss_ref[...] = jnp.zeros_like(loss_ref)

        e = y_ref[...] - t_ref[...]
        dy_ref[...] = e / d
        loss_ref[...] += 0.5 * jnp.sum(_rsum(e * e) / d, keepdims=True)

    tile = pl.BlockSpec((None, tm, d), lambda b, i: (b, i, 0))
    return pl.pallas_call(
        body, name=name, grid=(nb, s // tm), in_specs=[tile, tile],
        out_specs=[pl.BlockSpec((8, LANES), lambda b, i: (0, 0)), tile],
        out_shape=[jax.ShapeDtypeStruct((8, LANES), F32), jax.ShapeDtypeStruct((nb, s, d), F32)],
        compiler_params=_cp("arbitrary", "arbitrary"))(y, target)


def _adamw(parts, w, m, v, name):
    npart, r, ncol = parts.shape
    tr = _row_tile(r, ncol)
    bc1 = 1.0 - ADAM_B1 ** ADAM_STEP
    bc2 = 1.0 - ADAM_B2 ** ADAM_STEP

    def body(p_ref, w_ref, m_ref, v_ref, g_ref, d_ref, nm_ref, nv_ref):
        g = p_ref[0].astype(F32)
        for k in range(1, npart):
            g = g + p_ref[k].astype(F32)
        mn = ADAM_B1 * m_ref[...] + (1.0 - ADAM_B1) * g
        vn = ADAM_B2 * v_ref[...] + (1.0 - ADAM_B2) * (g * g)
        g_ref[...] = g
        nm_ref[...] = mn
        nv_ref[...] = vn
        d_ref[...] = -ADAM_LR * ((mn / bc1) / (jnp.sqrt(vn / bc2) + ADAM_EPS) + ADAM_WD * w_ref[...])

    tile = pl.BlockSpec((tr, ncol), lambda i: (i, 0))
    return pl.pallas_call(
        body, name=name, grid=(r // tr,),
        in_specs=[pl.BlockSpec((npart, tr, ncol), lambda i: (0, i, 0)), tile, tile, tile],
        out_specs=[tile] * 4, out_shape=[jax.ShapeDtypeStruct((r, ncol), F32)] * 4,
        compiler_params=_cp("parallel"))(parts, w, m, v)


def _unpack(flat, shapes, lead=()):
    out, off = [], 0
    for shp in shapes:
        size = 1
        for dim in shp:
            size *= dim
        out.append(flat[..., off:off + size].reshape(lead + tuple(shp)))
        off += size
    return out


def _t5_bucket():
    a = jnp.arange(BLOCK)[:, None]
    b = jnp.arange(2 * BLOCK)[None, :]
    dist = BLOCK + a - b
    max_exact = REL_BUCKETS // 2
    nn = jnp.maximum(dist, 0)
    nf = jnp.maximum(nn, 1).astype(F32)
    large = max_exact + (jnp.log(nf / max_exact) / math.log(BLOCK / max_exact)
                         * (REL_BUCKETS - max_exact)).astype(jnp.int32)
    large = jnp.minimum(large, REL_BUCKETS - 1)
    return jnp.where(nn < max_exact, nn, large).astype(jnp.int32)


def _pad_lanes(g, n):
    return jnp.pad(g, (0, n - g.shape[0])).reshape(1, n)


def kernel(x, c, positions, rel_table, norm1_g, norm2_g, w_ada, b_ada, w_in, mla_cq_g, w_uq, mla_ckv_g, w_ukv, mla_qn_g, mla_kn_g, sw_qn_g, sw_kn_g, sw_sinks, w_out, w_up, conv_w, conv_b, w_down, loss_target, m_rel_table, m_norm1_g, m_norm2_g, m_w_ada, m_b_ada, m_w_in, m_mla_cq_g, m_w_uq, m_mla_ckv_g, m_w_ukv, m_mla_qn_g, m_mla_kn_g, m_sw_qn_g, m_sw_kn_g, m_sw_sinks, m_w_out, m_w_up, m_conv_w, m_conv_b, m_w_down, v_rel_table, v_norm1_g, v_norm2_g, v_w_ada, v_b_ada, v_w_in, v_mla_cq_g, v_w_uq, v_mla_ckv_g, v_w_ukv, v_mla_qn_g, v_mla_kn_g, v_sw_qn_g, v_sw_kn_g, v_sw_sinks, v_w_out, v_w_up, v_conv_w, v_conv_b, v_w_down):
    nb, s, d = x.shape
    nl = norm1_g.shape[0]
    me = 4 * lax.axis_index("x") + 2 * lax.axis_index("y") + lax.axis_index("c")
    n_ada = w_ada.shape[2]

    rows2d = lambda a: a.reshape(-1, a.shape[-1])
    tr = lambda a: jnp.swapaxes(a, -1, -2)
    local = [rows2d(tr(w).astype(MXU)) for w in (w_in, w_uq, w_ukv, w_up)]
    local += [rows2d(w.astype(MXU)) for w in (w_out, w_down)] + [rows2d(conv_w), c]
    got = _all_gather(local, "gather_inputs")
    stack_rows = lambda a: a.reshape(N_DEV, nl, -1, a.shape[-1]).transpose(1, 0, 2, 3).reshape(nl, -1, a.shape[-1])
    w_in_t, w_uq_t, w_ukv_t, w_up_t, w_out_f, w_down_f = [stack_rows(a) for a in got[:6]]
    conv_full = got[6].reshape(N_DEV, nl, 3, -1).transpose(1, 2, 0, 3).reshape(nl, 3, -1)
    c_all = got[7].reshape(N_DEV * nb, d)
    zrows = lambda n: jnp.zeros((nl, n, d), MXU)
    w_in_pt = jnp.concatenate([w_in_t[:, :1152], w_in_t[:, 1184:1824], zrows(64), w_in_t[:, 1152:1184], zrows(160)], axis=1)
    w_uq_pt = jnp.pad(w_uq_t.reshape(nl, MLA_HEADS, MLA_QK, 256), ((0, 0), (0, 0), (0, LANES - MLA_QK), (0, 0))).reshape(nl, 768, 256)

    b_my = lax.dynamic_slice_in_dim(b_ada, me * n_ada, n_ada, axis=1).reshape(nl, 1, n_ada)
    mods_my = _ada_fwd(c_all, w_ada, b_my, "ada_fwd")
    mods, = _all_gather([mods_my.reshape(nl * N_DEV * nb, n_ada)], "gather_mods")
    mods = mods.reshape(N_DEV, nl, N_DEV * nb, n_ada).transpose(1, 2, 0, 3).reshape(nl, N_DEV * nb, N_DEV * n_ada)
    mods = lax.dynamic_slice_in_dim(mods, me * nb, nb, axis=1)
    shift1, scale1, gate1, shift2, scale2, gate2 = [mods[:, :, k * d:(k + 1) * d].reshape(nl, nb, 1, d) for k in range(6)]

    half = 16
    inv_freq = jnp.power(ROPE_THETA, -jnp.arange(half, dtype=F32) / half)
    ang = positions.astype(F32)[..., None] * inv_freq
    ones = lambda n: jnp.ones((nb, s, n), F32)
    zeros = lambda n: jnp.zeros((nb, s, n), F32)
    rope_c = jnp.concatenate([ones(64), jnp.cos(ang), jnp.cos(ang), ones(32)], axis=-1)
    rope_s = jnp.concatenate([zeros(64), jnp.sin(ang), jnp.sin(ang), zeros(32)], axis=-1)
    bucket = _t5_bucket()
    bias = _bias_build(jnp.pad(rel_table.T, ((0, 8 - SW_HEADS), (0, LANES - REL_BUCKETS))), bucket, "rel_bias")

    row = lambda g: g.reshape(1, -1)
    twice = lambda g: jnp.concatenate([g, g]).reshape(1, LANES)

    saved = []
    xl = x
    for l in range(nl):
        proj, h1 = _ln_mod_matmul(xl, row(norm1_g[l]), scale1[l], shift1[l], w_in_pt[l], f"l{l}_in_proj")
        prep_args = (proj, rope_c, rope_s, row(mla_cq_g[l]), row(mla_ckv_g[l]), _pad_lanes(mla_qn_g[l], LANES),
                     _pad_lanes(mla_kn_g[l], LANES), w_uq_pt[l], w_ukv_t[l])
        qm, km, vm = _mla_prep(*prep_args, f"l{l}_mla_prep")
        o_a, ct_a, cnt_a = _sb_fwd(proj, f"l{l}_sb_fwd")
        o_b, lse_b = _mla_fwd(qm, km, vm, f"l{l}_mla_fwd")
        sinks = jnp.broadcast_to(jnp.pad(sw_sinks[l], (0, 2))[:, None], (8, LANES))
        swa_args = (proj, twice(sw_qn_g[l]), twice(sw_kn_g[l]), sinks, bias)
        o_c = _swa_fwd(*swa_args, f"l{l}_swa_fwd")
        wo = [w_out_f[l, :256], w_out_f[l, 256:640], w_out_f[l, 640:]]
        x_mid, y1 = _out_proj([o_a, o_b, o_c], wo, gate1[l], xl, f"l{l}_out_proj")
        u_pre, h2 = _ln_mod_matmul(x_mid, row(norm2_g[l]), scale2[l], shift2[l], w_up_t[l], f"l{l}_up_proj")
        x_out, y2 = _conv_gate_matmul(u_pre, conv_full[l], row(conv_b[l]), w_down_f[l], gate2[l], x_mid, f"l{l}_ffn_down")
        saved.append(dict(x=xl, proj=proj, h1=h1, prep=prep_args, qkv=(qm, km, vm), o_a=o_a, ct_a=ct_a, cnt_a=cnt_a, o_b=o_b, lse_b=lse_b,
                          swa=swa_args, o_c=o_c, wo=wo, y1=y1, x_mid=x_mid, u_pre=u_pre, h2=h2, y2=y2))
        xl = x_out

    loss_blk, dx = _loss_grad(xl, loss_target, "loss")
    loss = lax.psum(loss_blk[0, 0], ("x", "y", "c"))

    t = nb * s
    flat = lambda a: a.reshape(t, a.shape[-1])
    grads = [None] * nl
    dmods = [None] * nl
    sharded_out = [None] * nl
    sharded_names = ["w_in", "w_uq", "w_ukv", "w_up", "w_out", "w_down", "conv_w"]
    sharded_wmv = dict(w_in=(w_in, m_w_in, v_w_in), w_uq=(w_uq, m_w_uq, v_w_uq), w_ukv=(w_ukv, m_w_ukv, v_w_ukv),
                       w_up=(w_up, m_w_up, v_w_up), w_out=(w_out, m_w_out, v_w_out), w_down=(w_down, m_w_down, v_w_down),
                       conv_w=(conv_w, m_conv_w, v_conv_w))
    n_in, n_up, n_out, n_dn = w_in.shape[2], w_up.shape[2], w_out.shape[1], w_down.shape[1]
    small_sizes = [w_uq[0].size, w_ukv[0].size, conv_w[0].size]
    n_small_rows = -(-sum(small_sizes) // d)
    rows_used = n_in + n_out + n_small_rows
    rows_grad = -(-rows_used // 16) * 16

    def pack_rows(mats, vecs):
        lead = mats[0].shape[:-2]
        flat_part = jnp.concatenate(vecs, axis=-1)
        flat_part = jnp.pad(flat_part, [(0, 0)] * len(lead) + [(0, n_small_rows * d - flat_part.shape[-1])])
        tail = jnp.zeros(lead + (rows_grad - rows_used, d), F32)
        return jnp.concatenate(list(mats) + [flat_part.reshape(lead + (n_small_rows, d)), tail], axis=-2)

    def unpack_rows(a):
        o1, o2 = n_in, n_in + n_out
        flat_part = a[o2:o2 + n_small_rows].reshape(-1)
        s1, s2, s3 = small_sizes[0], small_sizes[0] + small_sizes[1], sum(small_sizes)
        return dict(w_in=a[:o1].T, w_out=a[o1:o2],
                    w_uq=flat_part[:s1].reshape(w_uq.shape[2], -1).T, w_ukv=flat_part[s1:s2].reshape(w_ukv.shape[2], -1).T,
                    conv_w=flat_part[s2:s3].reshape(conv_w.shape[1:]))

    def update_sharded(l, recv):
        wmv = [{k: v[o][l] for k, v in sharded_wmv.items()} for o in range(3)]
        res_up = _adamw(recv[0], *[a["w_up"].T for a in wmv], f"l{l}_adamw_up")
        res_dn = _adamw(recv[1], *[a["w_down"] for a in wmv], f"l{l}_adamw_down")
        res_rest = _adamw(recv[2], *[pack_rows([a["w_in"].T, a["w_out"]], [a["w_uq"].T.reshape(-1), a["w_ukv"].T.reshape(-1),
                                                                            a["conv_w"].reshape(-1)]) for a in wmv],
                          f"l{l}_adamw_rest")
        sharded_out[l] = [dict(unpack_rows(rr), w_up=ru.T, w_down=rd) for ru, rd, rr in zip(res_up, res_dn, res_rest)]

    pending = None
    dbias = jnp.zeros((SW_HEADS, BLOCK, 2 * BLOCK), F32)
    for l in reversed(range(nl)):
        sv = saved[l]
        (da,), dy2, dgate2 = _gate_bwd_nt(dx, sv["y2"], gate2[l], [w_down_f[l]], f"l{l}_ffn_down_bwd")
        du, a_act, cstats = _conv_gate_bwd(da, sv["u_pre"], conv_full[l], row(conv_b[l]), f"l{l}_conv_gate_bwd")
        res = _ln_mod_matmul_bwd(du, w_up_t[l], sv["x_mid"], row(norm2_g[l]), scale2[l], dx, conv_full[l],
                                 f"l{l}_up_proj_bwd", riding=pending[1] if pending else ())
        dx_mid, du_pre, dshift2, dscale2, dg2 = res[:5]
        if pending:
            update_sharded(pending[0], res[5:])
            pending = None
        g_w_down = _wgrad(flat(a_act), flat(dy2), f"l{l}_w_down_grad")
        g_w_up_t = _wgrad(flat(du_pre), flat(sv["h2"]), f"l{l}_w_up_grad")

        (do_a, do_b, do_c), dy1, dgate1 = _gate_bwd_nt(dx_mid, sv["y1"], gate1[l], sv["wo"], f"l{l}_out_proj_bwd")
        mix = jnp.concatenate([sv["o_a"], sv["o_b"], sv["o_c"]], axis=-1).astype(MXU)
        g_w_out = _wgrad(flat(mix), flat(dy1), f"l{l}_w_out_grad")

        dsb_q, dsb_k, dsb_v = _sb_bwd(sv["proj"], sv["ct_a"], sv["cnt_a"], do_a, f"l{l}_sb_bwd")
        qm, km, vm = sv["qkv"]
        dqm, dkm, dvm = _mla_bwd(qm, km, vm, sv["o_b"], sv["lse_b"], do_b, f"l{l}_mla_bwd")
        dsw_q, dkc, dkp, dvc, dvp, dbias_l, dsinks, dg_swq, dg_swk = _swa_bwd(*sv["swa"], do_c, f"l{l}_swa_bwd")
        dbias = dbias + dbias_l
        shift_up = lambda a: jnp.concatenate([a[:, BLOCK:], jnp.zeros((nb, BLOCK, LANES), F32)], axis=1)
        dsw_k = dkc + shift_up(dkp)
        dsw_v = dvc + shift_up(dvp)
        dcq, dckv, dkr, g_w_uq_pt, g_w_ukv_t, dg_cq, dg_ckv, dg_qn, dg_kn = _mla_prep_bwd(
            *sv["prep"], dqm, dkm, dvm, f"l{l}_mla_prep_bwd")
        dproj = jnp.concatenate([dsb_q, dsb_k, dsb_v, dcq, dckv, dsw_q, dsw_k, dsw_v, dkr, zeros(128)], axis=-1)
        dx, dproj_m, dshift1, dscale1, dg1 = _ln_mod_matmul_bwd(
            dproj, w_in_pt[l], sv["x"], row(norm1_g[l]), scale1[l], dx_mid, None, f"l{l}_in_proj_bwd")
        g_w_in_pt = _wgrad(flat(dproj_m), flat(sv["h1"]), f"l{l}_w_in_grad")

        g_w_in_t = jnp.concatenate([g_w_in_pt[:1152], g_w_in_pt[1856:1888], g_w_in_pt[1152:1792]], axis=0)
        g_w_uq_t = g_w_uq_pt.reshape(MLA_HEADS, LANES, 256)[:, :MLA_QK].reshape(MLA_HEADS * MLA_QK, 256)
        dmods[l] = jnp.concatenate([dshift1, dscale1, dgate1, dshift2, dscale2, dgate2], axis=-1).reshape(nb, 6 * d)

        per_dev = lambda g: g.reshape(N_DEV, -1, d)
        conv_dev = cstats[1:4].reshape(3, N_DEV, -1).transpose(1, 0, 2)
        rest = pack_rows([per_dev(g_w_in_t), per_dev(g_w_out)],
                         [g_w_uq_t.reshape(N_DEV, -1), g_w_ukv_t.reshape(N_DEV, -1), conv_dev.reshape(N_DEV, -1)])
        send = [per_dev(g_w_up_t), per_dev(g_w_down), rest]
        sib = _pair_exchange(send, f"l{l}_pair_exchange")
        core = lax.axis_index("c").reshape(1).astype(jnp.int32)
        pair = [_pair_add(core, a, b, f"l{l}_pair_add_{k}") for a, b, k in zip(send, sib, ("up", "down", "rest"))]
        if l > 0:
            pending = (l, pair)
        else:
            update_sharded(l, _chip_exchange(pair, f"l{l}_chip_exchange"))
        grads[l] = dict(
            norm1_g=dg1[0], norm2_g=dg2[0], mla_cq_g=dg_cq[0], mla_ckv_g=dg_ckv[0], mla_qn_g=dg_qn[0, :MLA_QK],
            mla_kn_g=dg_kn[0, :MLA_QK], sw_qn_g=dg_swq[0, :HEAD] + dg_swq[0, HEAD:], sw_kn_g=dg_swk[0, :HEAD] + dg_swk[0, HEAD:],
            sw_sinks=dsinks[:SW_HEADS, 0], conv_b=cstats[0])
    grad_x = dx
    g_rel = _bias_grad(dbias, bucket, "rel_table_grad")[:SW_HEADS, :REL_BUCKETS].T
    stack = lambda k: jnp.stack([grads[l][k] for l in range(nl)])

    dm_all, = _all_gather([jnp.stack(dmods).reshape(nl * nb, 6 * d)], "gather_dmods")
    dm_all = dm_all.reshape(N_DEV, nl, nb, 6 * d).transpose(1, 0, 2, 3).reshape(nl, N_DEV * nb, 6 * d)
    dm_my = lax.dynamic_slice_in_dim(dm_all, me * n_ada, n_ada, axis=2)
    g_w_ada, g_b_ada = _ada_bwd(c_all, dm_my, dm_all, "ada_bwd")
    g_b_ada = g_b_ada.reshape(nl, 6 * d)

    big_out = [{k: jnp.stack([sharded_out[l][o][k] for l in range(nl)]) for k in sharded_names} for o in range(4)]
    packf = lambda dct, names, rows: jnp.pad(jnp.concatenate([dct[k].reshape(-1) for k in names]),
                                             (0, rows * LANES - sum(dct[k].size for k in names))).reshape(rows, LANES)

    small_names = ["rel_table", "norm1_g", "norm2_g", "mla_cq_g", "mla_ckv_g", "mla_qn_g", "mla_kn_g",
                   "sw_qn_g", "sw_kn_g", "sw_sinks", "conv_b"]
    small_w = dict(rel_table=rel_table, norm1_g=norm1_g, norm2_g=norm2_g, mla_cq_g=mla_cq_g, mla_ckv_g=mla_ckv_g,
                   mla_qn_g=mla_qn_g, mla_kn_g=mla_kn_g, sw_qn_g=sw_qn_g, sw_kn_g=sw_kn_g, sw_sinks=sw_sinks, conv_b=conv_b)
    small_m = dict(rel_table=m_rel_table, norm1_g=m_norm1_g, norm2_g=m_norm2_g, mla_cq_g=m_mla_cq_g, mla_ckv_g=m_mla_ckv_g,
                   mla_qn_g=m_mla_qn_g, mla_kn_g=m_mla_kn_g, sw_qn_g=m_sw_qn_g, sw_kn_g=m_sw_kn_g, sw_sinks=m_sw_sinks, conv_b=m_conv_b)
    small_v = dict(rel_table=v_rel_table, norm1_g=v_norm1_g, norm2_g=v_norm2_g, mla_cq_g=v_mla_cq_g, mla_ckv_g=v_mla_ckv_g,
                   mla_qn_g=v_mla_qn_g, mla_kn_g=v_mla_kn_g, sw_qn_g=v_sw_qn_g, sw_kn_g=v_sw_kn_g, sw_sinks=v_sw_sinks, conv_b=v_conv_b)
    small_g = {k: (g_rel if k == "rel_table" else stack(k)) for k in small_names}
    n_small = sum(small_w[k].size for k in small_names)
    rows_small = -(-n_small // (8 * LANES)) * 8
    small_parts, = _all_gather([packf(small_g, small_names, rows_small)], "gather_small_grads")
    small_out = _adamw(small_parts, packf(small_w, small_names, rows_small), packf(small_m, small_names, rows_small),
                       packf(small_v, small_names, rows_small), "adamw_replicated")
    small_out = [dict(zip(small_names, _unpack(o.reshape(-1), [small_w[k].shape for k in small_names]))) for o in small_out]

    two_d = lambda a: a.reshape(-1, a.shape[-1])
    res_w = _adamw(two_d(g_w_ada)[None], two_d(w_ada), two_d(m_w_ada), two_d(v_w_ada), "adamw_w_ada")
    res_b = _adamw(g_b_ada[None], b_ada, m_b_ada, v_b_ada, "adamw_b_ada")
    ada_out = [dict(w_ada=rw.reshape(w_ada.shape), b_ada=rb) for rw, rb in zip(res_w, res_b)]

    order = ["rel_table", "norm1_g", "norm2_g", "w_ada", "b_ada", "w_in", "mla_cq_g", "w_uq", "mla_ckv_g", "w_ukv",
             "mla_qn_g", "mla_kn_g", "sw_qn_g", "sw_kn_g", "sw_sinks", "w_out", "w_up", "conv_w", "conv_b", "w_down"]
    outs = [{**big_out[k], **small_out[k], **ada_out[k]} for k in range(4)]
    return (loss, grad_x, *[outs[0][n] for n in order], *[outs[1][n] for n in order],
            *[outs[2][n] for n in order], *[outs[3][n] for n in order])
```

```python
import math

import jax
import jax.numpy as jnp
from jax import lax
from jax.experimental import pallas as pl
from jax.experimental.pallas import tpu as pltpu

F32 = jnp.float32
MXU = jnp.bfloat16
EPS = 1e-6
NEG = -1e30
VMEM_LIMIT_BYTES = 56 * 1024 * 1024
N_DEV = 8
MESH = pl.DeviceIdType.MESH

D_MODEL = 1024
D_FF = 2816
HEAD = 64
LANES = 128
MLA_HEADS = 6
MLA_QK = 96
SW_HEADS = 6
REL_BUCKETS = 32
BLOCK = 128
SB_SCALE = HEAD ** -0.5
SB_DEAD = -105.0
SW_SCALE = HEAD ** -0.5
MLA_SCALE = MLA_QK ** -0.5
ROPE_THETA = 10000.0
D_IN_PAD = 2048
COL_SBQ, COL_SBK, COL_SBV, COL_CQ, COL_CKV, COL_SWQ, COL_SWK, COL_SWV, COL_KR = 0, 256, 512, 768, 1024, 1152, 1536, 1664, 1792

HALO = 16
ROW_TILE_BYTES = 1 << 20
ADAM_LR, ADAM_B1, ADAM_B2, ADAM_EPS, ADAM_WD, ADAM_STEP = 0.001, 0.9, 0.999, 1e-08, 0.01, 10


def _cp(*sem):
    return pltpu.CompilerParams(dimension_semantics=sem, vmem_limit_bytes=VMEM_LIMIT_BYTES)


def _iota(shape, dim):
    return lax.broadcasted_iota(jnp.int32, shape, dim)


def _dot(a, b):
    return jnp.dot(a, b, preferred_element_type=F32)


def _dot_nt(a, b):
    return lax.dot_general(a, b, (((1,), (1,)), ((), ())), preferred_element_type=F32)


def _dot_tn(a, b):
    return lax.dot_general(a, b, (((0,), (0,)), ((), ())), preferred_element_type=F32)


def _cumdot(x, u):
    hi = x.astype(MXU)
    mid = (x - hi.astype(F32)).astype(MXU)
    return _dot(hi, u) + _dot(mid, u)


def _sigmoid(x):
    return 1.0 / (1.0 + jnp.exp(-x))


def _rsum(x):
    return jnp.sum(x, axis=-1, keepdims=True)


def _csum(x):
    return jnp.sum(x, axis=0, keepdims=True)


def _all_gather(xs, name):
    na = len(xs)

    def body(*refs):
        start, finish = _gather_steps(refs[:na], refs[na:2 * na], *refs[2 * na:])
        start()
        finish()

    hbm = pl.BlockSpec(memory_space=pl.ANY)
    return pl.pallas_call(
        body, name=name, out_shape=_gather_out_shapes(xs), in_specs=[hbm] * na, out_specs=[hbm] * na,
        scratch_shapes=_gather_sems(na))(*xs)


def _gather_out_shapes(xs):
    return [jax.ShapeDtypeStruct((N_DEV,) + a.shape, a.dtype) for a in xs]


def _gather_sems(na):
    return [pltpu.SemaphoreType.DMA((7 * na,)), pltpu.SemaphoreType.DMA((7 * na,)), pltpu.SemaphoreType.DMA((na,))]


def _gather_steps(x_refs, out_refs, send_sems, recv_sems, local_sems):
    na = len(x_refs)
    x, y, c = lax.axis_index("x"), lax.axis_index("y"), lax.axis_index("c")
    me, sibling = (x, y, c), (x, y, 1 - c)
    chips = [(1 - x, y), (x, 1 - y), (1 - x, 1 - y)]

    def slot(a, px, py, pc):
        return out_refs[a].at[4 * px + 2 * py + pc]

    def copy(a, k, block, to, src=None):
        return pltpu.make_async_remote_copy(
            src_ref=slot(a, *block) if src is None else src, dst_ref=slot(a, *block),
            send_sem=send_sems.at[7 * a + k], recv_sem=recv_sems.at[7 * a + k], device_id=to, device_id_type=MESH)

    def own_copies(a):
        return ([copy(a, 0, me, sibling, src=x_refs[a])]
                + [copy(a, 1 + j, me, (*chip, c), src=x_refs[a]) for j, chip in enumerate(chips)])

    def local_copy(a):
        return pltpu.make_async_copy(x_refs[a], slot(a, *me), local_sems.at[a])

    def start():
        for a in range(na):
            local_copy(a).start()
            for cp in own_copies(a):
                cp.start()

    def finish():
        passed = []
        for j, chip in enumerate(chips):
            for a in range(na):
                copy(a, 1 + j, (*chip, c), me).wait_recv()
                passed.append(copy(a, 4 + j, (*chip, c), sibling))
                passed[-1].start()
        for a in range(na):
            copy(a, 0, sibling, me).wait_recv()
            for j, chip in enumerate(chips):
                copy(a, 4 + j, (*chip, 1 - c), me).wait_recv()
        for a in range(na):
            for cp in own_copies(a):
                cp.wait_send()
        for cp in passed:
            cp.wait_send()
        for a in range(na):
            local_copy(a).wait()

    return start, finish


def _with_gather(body, n_in, n_out, na, grid):
    if not na:
        return body

    def wrapped(*refs):
        ins, ride_in = refs[:n_in], refs[n_in:n_in + na]
        outs = refs[n_in + na:n_in + na + n_out]
        ride_out = refs[n_in + na + n_out:n_in + 2 * na + n_out]
        ids = [pl.program_id(k) for k in range(len(grid))]
        first, last = ids[0] == 0, ids[0] == grid[0] - 1
        for k in range(1, len(grid)):
            first, last = first & (ids[k] == 0), last & (ids[k] == grid[k] - 1)
        start, finish = _gather_steps(ride_in, ride_out, *refs[n_in + 2 * na + n_out:])
        pl.when(first)(start)
        body(*ins, *outs)
        pl.when(last)(finish)

    return wrapped


def _pair_exchange(xs, name):
    na = len(xs)

    def body(*refs):
        x_refs, out_refs = refs[:na], refs[na:2 * na]
        send_sems, recv_sems = refs[2 * na:]
        x, y, c = lax.axis_index("x"), lax.axis_index("y"), lax.axis_index("c")
        copies = []
        for a in range(na):
            for q in range(4):
                copies.append(pltpu.make_async_remote_copy(
                    src_ref=x_refs[a].at[2 * q + 1 - c], dst_ref=out_refs[a].at[q],
                    send_sem=send_sems.at[4 * a + q], recv_sem=recv_sems.at[4 * a + q],
                    device_id=(x, y, 1 - c), device_id_type=MESH))
                copies[-1].start()
        for cp in copies:
            cp.wait()

    hbm = pl.BlockSpec(memory_space=pl.ANY)
    return pl.pallas_call(
        body, name=name, out_shape=[jax.ShapeDtypeStruct((4,) + a.shape[1:], a.dtype) for a in xs],
        in_specs=[hbm] * na, out_specs=[hbm] * na,
        scratch_shapes=[pltpu.SemaphoreType.DMA((4 * na,)), pltpu.SemaphoreType.DMA((4 * na,))])(*xs)


def _row_tile(r, ncol):
    if r * ncol * 4 <= ROW_TILE_BYTES:
        return r
    return max(t for t in range(16, r, 16) if r % t == 0 and t * ncol * 4 <= ROW_TILE_BYTES)


def _pair_add(core, xs, sib, name):
    _, r, ncol = xs.shape
    tr = _row_tile(r, ncol)

    def body(c_ref, x_ref, s_ref, o_ref):
        o_ref[...] = (x_ref[...] + s_ref[...]).astype(MXU)

    return pl.pallas_call(
        body, name=name,
        grid_spec=pltpu.PrefetchScalarGridSpec(
            num_scalar_prefetch=1, grid=(4, r // tr),
            in_specs=[pl.BlockSpec((None, tr, ncol), lambda q, i, c_ref: (2 * q + c_ref[0], i, 0)),
                      pl.BlockSpec((None, tr, ncol), lambda q, i, c_ref: (q, i, 0))],
            out_specs=pl.BlockSpec((None, tr, ncol), lambda q, i, c_ref: (q, i, 0))),
        out_shape=jax.ShapeDtypeStruct((4, r, ncol), MXU),
        compiler_params=_cp("parallel", "parallel"))(core, xs, sib)


def _chip_exchange(xs, name):
    na = len(xs)

    def body(*refs):
        copies = _chip_exchange_copies(refs[:na], refs[na:2 * na], *refs[2 * na:])
        for cp in copies:
            cp.start()
        for cp in copies:
            cp.wait()

    hbm = pl.BlockSpec(memory_space=pl.ANY)
    return pl.pallas_call(
        body, name=name, out_shape=[jax.ShapeDtypeStruct(a.shape, a.dtype) for a in xs],
        in_specs=[hbm] * na, out_specs=[hbm] * na, scratch_shapes=_chip_exchange_sems(na))(*xs)


def _chip_exchange_sems(na):
    return [pltpu.SemaphoreType.DMA((3 * na,)), pltpu.SemaphoreType.DMA((3 * na,)), pltpu.SemaphoreType.DMA((na,))]


def _chip_exchange_copies(x_refs, out_refs, send_sems, recv_sems, local_sems):
    x, y, c = lax.axis_index("x"), lax.axis_index("y"), lax.axis_index("c")
    me = 2 * x + y
    copies = [pltpu.make_async_copy(x_refs[a].at[me], out_refs[a].at[me], local_sems.at[a]) for a in range(len(x_refs))]
    for k, (dx, dy) in enumerate([(1, 0), (0, 1), (1, 1)]):
        px = 1 - x if dx else x
        py = 1 - y if dy else y
        for a in range(len(x_refs)):
            copies.append(pltpu.make_async_remote_copy(
                src_ref=x_refs[a].at[2 * px + py], dst_ref=out_refs[a].at[me],
                send_sem=send_sems.at[3 * a + k], recv_sem=recv_sems.at[3 * a + k],
                device_id=(px, py, c), device_id_type=MESH))
    return copies


def _ada_fwd(c_all, w_ada, b_my, name):
    nl, d, n = w_ada.shape
    nb = c_all.shape[0]

    def body(c_ref, w_ref, b_ref, o_ref):
        cv = c_ref[...]
        sc = (cv * _sigmoid(cv)).astype(MXU)
        o_ref[...] = _dot(sc, w_ref[...].astype(MXU)) + b_ref[...]

    return pl.pallas_call(
        body, name=name, grid=(nl,),
        in_specs=[pl.BlockSpec((nb, d), lambda l: (0, 0)),
                  pl.BlockSpec((None, d, n), lambda l: (l, 0, 0)),
                  pl.BlockSpec((None, 1, n), lambda l: (l, 0, 0))],
        out_specs=pl.BlockSpec((None, nb, n), lambda l: (l, 0, 0)),
        out_shape=jax.ShapeDtypeStruct((nl, nb, n), F32),
        compiler_params=_cp("parallel"))(c_all, w_ada, b_my)


def _ada_bwd(c_all, dmods_my, dmods_all, name):
    nl, nb, n = dmods_my.shape
    d = c_all.shape[1]
    nfull = dmods_all.shape[2]

    def body(c_ref, dm_ref, da_ref, dw_ref, db_ref):
        cv = c_ref[...]
        sc = (cv * _sigmoid(cv)).astype(MXU)
        dw_ref[...] = _dot_tn(sc, dm_ref[...].astype(MXU))
        db_ref[...] = _csum(da_ref[...])

    return pl.pallas_call(
        body, name=name, grid=(nl,),
        in_specs=[pl.BlockSpec((nb, d), lambda l: (0, 0)),
                  pl.BlockSpec((None, nb, n), lambda l: (l, 0, 0)),
                  pl.BlockSpec((None, nb, nfull), lambda l: (l, 0, 0))],
        out_specs=[pl.BlockSpec((None, d, n), lambda l: (l, 0, 0)),
                   pl.BlockSpec((None, 1, nfull), lambda l: (l, 0, 0))],
        out_shape=[jax.ShapeDtypeStruct((nl, d, n), F32), jax.ShapeDtypeStruct((nl, 1, nfull), F32)],
        compiler_params=_cp("parallel"))(c_all, dmods_my, dmods_all)


def _ln_mod_matmul(x, g, scale, shift, w, name):
    nb, s, d = x.shape
    n = w.shape[0]
    tm, tn = min(1024, s), 512

    def body(x_ref, g_ref, sc_ref, sh_ref, w_ref, y_ref, h_ref, h_s):
        @pl.when(pl.program_id(2) == 0)
        def _():
            xf = x_ref[...]
            rstd = lax.rsqrt(jnp.mean(xf * xf, axis=-1, keepdims=True) + EPS)
            hv = (xf * rstd * g_ref[...]) * (1.0 + sc_ref[...]) + sh_ref[...]
            h_s[...] = hv.astype(MXU)
            h_ref[...] = h_s[...]

        y_ref[...] = _dot_nt(h_s[...], w_ref[...]).astype(MXU)

    return pl.pallas_call(
        body, name=name, grid=(nb, s // tm, n // tn),
        in_specs=[pl.BlockSpec((None, tm, d), lambda b, i, j: (b, i, 0)),
                  pl.BlockSpec((1, d), lambda b, i, j: (0, 0)),
                  pl.BlockSpec((None, 1, d), lambda b, i, j: (b, 0, 0)),
                  pl.BlockSpec((None, 1, d), lambda b, i, j: (b, 0, 0)),
                  pl.BlockSpec((tn, d), lambda b, i, j: (j, 0))],
        out_specs=[pl.BlockSpec((None, tm, tn), lambda b, i, j: (b, i, j)),
                   pl.BlockSpec((None, tm, d), lambda b, i, j: (b, i, 0))],
        out_shape=[jax.ShapeDtypeStruct((nb, s, n), MXU), jax.ShapeDtypeStruct((nb, s, d), MXU)],
        scratch_shapes=[pltpu.VMEM((tm, d), MXU)],
        compiler_params=_cp("parallel", "parallel", "arbitrary"))(x, g, scale, shift, w)


def _ln_mod_matmul_bwd(dy, w, x, g, scale, dres, conv_w, name, riding=()):
    nb, s, n = dy.shape
    d = x.shape[-1]
    tm, tn = min(512, s), 512
    ni, nj = s // tm, n // tn
    hb = tm // HALO
    conv = conv_w is not None
    na = len(riding)

    def body(*refs):
        if conv:
            dy_ref, nx_ref, cw_ref = refs[:3]
            refs = refs[3:]
        else:
            dy_ref = refs[0]
            refs = refs[1:]
        w_ref, x_ref, g_ref, sc_ref, dr_ref = refs[:5]
        ride_in, refs = refs[5:5 + na], refs[5 + na:]
        dx_ref, dyp_ref, dsh_ref, dsc_ref, dg_ref = refs[:5]
        ride_out, refs = refs[5:5 + na], refs[5 + na:]
        acc = refs[0]
        b, i, j = pl.program_id(0), pl.program_id(1), pl.program_id(2)
        if na:
            copies = _chip_exchange_copies(ride_in, ride_out, *refs[1:])

            @pl.when((b == 0) & (i == 0) & (j == 0))
            def _():
                for cp in copies:
                    cp.start()

        @pl.when(j == 0)
        def _():
            acc[...] = jnp.zeros_like(acc)

        @pl.when((j == 0) & (i == 0))
        def _():
            dsh_ref[...] = jnp.zeros_like(dsh_ref)
            dsc_ref[...] = jnp.zeros_like(dsc_ref)

        @pl.when((j == 0) & (i == 0) & (b == 0))
        def _():
            dg_ref[...] = jnp.zeros_like(dg_ref)

        dv = dy_ref[...].astype(F32)
        if conv:
            rows = _iota((tm, 1), 0)
            nx = jnp.where(i == ni - 1, 0.0, nx_ref[...].astype(F32))
            n1 = jnp.where(rows == tm - 1, nx[0:1, :], pltpu.roll(dv, tm - 1, 0))
            n2 = jnp.where(rows == tm - 2, nx[0:1, :], jnp.where(rows == tm - 1, nx[1:2, :], pltpu.roll(dv, tm - 2, 0)))
            cw = cw_ref[...]
            dv = cw[2:3, :] * dv + cw[1:2, :] * n1 + cw[0:1, :] * n2
        dp = dv.astype(MXU)
        dyp_ref[...] = dp
        acc[...] += _dot(dp, w_ref[...])

        @pl.when(j == nj - 1)
        def _():
            dh = acc[...]
            xf = x_ref[...]
            rstd = lax.rsqrt(jnp.mean(xf * xf, axis=-1, keepdims=True) + EPS)
            xn = xf * rstd
            gg = g_ref[...]
            sc1 = 1.0 + sc_ref[...]
            dsh_ref[...] += _csum(dh)
            dsc_ref[...] += _csum(dh * xn * gg)
            dg_ref[...] += _csum(dh * xn * sc1)
            dn = dh * gg * sc1
            dx_ref[...] = dr_ref[...] + rstd * (dn - xn * jnp.mean(dn * xn, axis=-1, keepdims=True))

        if na:
            @pl.when((b == nb - 1) & (i == ni - 1) & (j == nj - 1))
            def _():
                for cp in copies:
                    cp.wait()

    hbm = pl.BlockSpec(memory_space=pl.ANY)
    in_specs = [pl.BlockSpec((None, tm, tn), lambda b, i, j: (b, i, j))]
    args = [dy]
    if conv:
        in_specs += [pl.BlockSpec((None, HALO, tn), lambda b, i, j: (b, jnp.minimum((i + 1) * hb, s // HALO - 1), j)),
                     pl.BlockSpec((3, tn), lambda b, i, j: (0, j))]
        args += [dy, conv_w]
    in_specs += [pl.BlockSpec((tn, d), lambda b, i, j: (j, 0)),
                 pl.BlockSpec((None, tm, d), lambda b, i, j: (b, i, 0)),
                 pl.BlockSpec((1, d), lambda b, i, j: (0, 0)),
                 pl.BlockSpec((None, 1, d), lambda b, i, j: (b, 0, 0)),
                 pl.BlockSpec((None, tm, d), lambda b, i, j: (b, i, 0))]
    in_specs += [hbm] * na
    args += [w, x, g, scale, dres, *riding]
    return pl.pallas_call(
        body, name=name, grid=(nb, ni, nj), in_specs=in_specs,
        out_specs=[pl.BlockSpec((None, tm, d), lambda b, i, j: (b, i, 0)),
                   pl.BlockSpec((None, tm, tn), lambda b, i, j: (b, i, j)),
                   pl.BlockSpec((None, 1, d), lambda b, i, j: (b, 0, 0)),
                   pl.BlockSpec((None, 1, d), lambda b, i, j: (b, 0, 0)),
                   pl.BlockSpec((1, d), lambda b, i, j: (0, 0))] + [hbm] * na,
        out_shape=[jax.ShapeDtypeStruct((nb, s, d), F32), jax.ShapeDtypeStruct((nb, s, n), MXU),
                   jax.ShapeDtypeStruct((nb, 1, d), F32), jax.ShapeDtypeStruct((nb, 1, d), F32),
                   jax.ShapeDtypeStruct((1, d), F32)] + [jax.ShapeDtypeStruct(a.shape, a.dtype) for a in riding],
        scratch_shapes=[pltpu.VMEM((tm, d), F32)] + (_chip_exchange_sems(na) if na else []),
        compiler_params=_cp("arbitrary", "arbitrary", "arbitrary"))(*args)


def _wgrad(xm, dym, name):
    t, k = xm.shape
    n = dym.shape[1]
    tk = 1408 if k % 1408 == 0 else 1024
    tt = min(512, t)

    def body(x_ref, dy_ref, o_ref):
        @pl.when(pl.program_id(1) == 0)
        def _():
            o_ref[...] = jnp.zeros_like(o_ref)

        o_ref[...] += _dot_tn(x_ref[...], dy_ref[...])

    return pl.pallas_call(
        body, name=name, grid=(k // tk, t // tt),
        in_specs=[pl.BlockSpec((tt, tk), lambda a, c: (c, a)),
                  pl.BlockSpec((tt, n), lambda a, c: (c, 0))],
        out_specs=pl.BlockSpec((tk, n), lambda a, c: (a, 0)),
        out_shape=jax.ShapeDtypeStruct((k, n), F32),
        compiler_params=_cp("parallel", "arbitrary"))(xm, dym)


def _out_proj(parts, ws, gate, res, name):
    nb, s, d = res.shape
    tm = min(512, s)
    npart = len(parts)

    def body(*refs):
        p_refs, w_refs = refs[:npart], refs[npart:2 * npart]
        gt_ref, res_ref, xo_ref, y_ref = refs[2 * npart:]
        y = _dot(p_refs[0][...].astype(MXU), w_refs[0][...])
        for p_ref, w_ref in zip(p_refs[1:], w_refs[1:]):
            y = y + _dot(p_ref[...].astype(MXU), w_ref[...])
        y_ref[...] = y
        xo_ref[...] = res_ref[...] + gt_ref[...] * y

    in_specs = [pl.BlockSpec((None, tm, p.shape[-1]), lambda b, i: (b, i, 0)) for p in parts]
    in_specs += [pl.BlockSpec(w.shape, lambda b, i: (0, 0)) for w in ws]
    in_specs += [pl.BlockSpec((None, 1, d), lambda b, i: (b, 0, 0)),
                 pl.BlockSpec((None, tm, d), lambda b, i: (b, i, 0))]
    return pl.pallas_call(
        body, name=name, grid=(nb, s // tm), in_specs=in_specs,
        out_specs=[pl.BlockSpec((None, tm, d), lambda b, i: (b, i, 0))] * 2,
        out_shape=[jax.ShapeDtypeStruct((nb, s, d), F32)] * 2,
        compiler_params=_cp("parallel", "parallel"))(*parts, *ws, gate, res)


def _gate_bwd_nt(dx, y, gate, ws, name):
    nb, s, d = dx.shape
    tm = min(256, s)
    npart = len(ws)

    def body(*refs):
        dx_ref, y_ref, gt_ref = refs[:3]
        w_refs = refs[3:3 + npart]
        da_refs = refs[3 + npart:3 + 2 * npart]
        dy_ref, dgt_ref = refs[3 + 2 * npart:]

        @pl.when(pl.program_id(1) == 0)
        def _():
            dgt_ref[...] = jnp.zeros_like(dgt_ref)

        dxv = dx_ref[...]
        dyv = (dxv * gt_ref[...]).astype(MXU)
        dy_ref[...] = dyv
        dgt_ref[...] += _csum(dxv * y_ref[...])
        for w_ref, da_ref in zip(w_refs, da_refs):
            da_ref[...] = _dot_nt(dyv, w_ref[...])

    tile = pl.BlockSpec((None, tm, d), lambda b, i: (b, i, 0))
    row = pl.BlockSpec((None, 1, d), lambda b, i: (b, 0, 0))
    outs = pl.pallas_call(
        body, name=name, grid=(nb, s // tm),
        in_specs=[tile, tile, row] + [pl.BlockSpec(w.shape, lambda b, i: (0, 0)) for w in ws],
        out_specs=[pl.BlockSpec((None, tm, w.shape[0]), lambda b, i: (b, i, 0)) for w in ws] + [tile, row],
        out_shape=[jax.ShapeDtypeStruct((nb, s, w.shape[0]), F32) for w in ws]
        + [jax.ShapeDtypeStruct((nb, s, d), MXU), jax.ShapeDtypeStruct((nb, 1, d), F32)],
        compiler_params=_cp("arbitrary", "arbitrary"))(dx, y, gate, *ws)
    return outs[:npart], outs[npart], outs[npart + 1]


def _conv_shifts(xv, halo, rows):
    last, before = halo[HALO - 1:HALO, :], halo[HALO - 2:HALO - 1, :]
    p1 = jnp.where(rows == 0, last, pltpu.roll(xv, 1, 0))
    p2 = jnp.where(rows == 0, before, jnp.where(rows == 1, last, pltpu.roll(xv, 2, 0)))
    return p1, p2


def _conv_gate_matmul(u, cw, cb, wd, gate, res, name):
    nb, s, f2 = u.shape
    f = f2 // 2
    d = wd.shape[1]
    tm = min(512, s)
    tk = f // 2
    nk = f // tk
    hb = tm // HALO

    def body(ug_ref, uv_ref, hg_ref, hv_ref, cwg_ref, cwv_ref, cbg_ref, cbv_ref, wd_ref, gt_ref, res_ref,
             xo_ref, y_ref, acc):
        i, k = pl.program_id(1), pl.program_id(2)

        @pl.when(k == 0)
        def _():
            acc[...] = jnp.zeros_like(acc)

        rows = _iota((tm, 1), 0)

        def conv(x_ref, h_ref, w_ref, b_ref):
            xv = x_ref[...].astype(F32)
            halo = jnp.where(i == 0, 0.0, h_ref[...].astype(F32))
            p1, p2 = _conv_shifts(xv, halo, rows)
            wv = w_ref[...]
            return wv[2:3, :] * xv + wv[1:2, :] * p1 + wv[0:1, :] * p2 + b_ref[...]

        gv = conv(ug_ref, hg_ref, cwg_ref, cbg_ref)
        vv = conv(uv_ref, hv_ref, cwv_ref, cbv_ref)
        av = gv * _sigmoid(gv) * vv
        acc[...] += _dot(av.astype(MXU), wd_ref[...])

        @pl.when(k == nk - 1)
        def _():
            y = acc[...]
            y_ref[...] = y
            xo_ref[...] = res_ref[...] + gt_ref[...] * y

    def halo_idx(off):
        return lambda b, i, k: (b, jnp.maximum(i * hb - 1, 0), k + off)

    tile = pl.BlockSpec((None, tm, d), lambda b, i, k: (b, i, 0))
    return pl.pallas_call(
        body, name=name, grid=(nb, s // tm, nk),
        in_specs=[pl.BlockSpec((None, tm, tk), lambda b, i, k: (b, i, k)),
                  pl.BlockSpec((None, tm, tk), lambda b, i, k: (b, i, k + nk)),
                  pl.BlockSpec((None, HALO, tk), halo_idx(0)),
                  pl.BlockSpec((None, HALO, tk), halo_idx(nk)),
                  pl.BlockSpec((3, tk), lambda b, i, k: (0, k)),
                  pl.BlockSpec((3, tk), lambda b, i, k: (0, k + nk)),
                  pl.BlockSpec((1, tk), lambda b, i, k: (0, k)),
                  pl.BlockSpec((1, tk), lambda b, i, k: (0, k + nk)),
                  pl.BlockSpec((tk, d), lambda b, i, k: (k, 0)),
                  pl.BlockSpec((None, 1, d), lambda b, i, k: (b, 0, 0)),
                  tile],
        out_specs=[tile, tile],
        out_shape=[jax.ShapeDtypeStruct((nb, s, d), F32)] * 2,
        scratch_shapes=[pltpu.VMEM((tm, d), F32)],
        compiler_params=_cp("parallel", "parallel", "arbitrary"))(u, u, u, u, cw, cw, cb, cb, wd, gate, res)


def _conv_gate_bwd(da, u, cw, cb, name):
    nb, s, f2 = u.shape
    f = f2 // 2
    tm = min(128, s)
    hb = tm // HALO

    def body(da_ref, u_ref, h_ref, cw_ref, cb_ref, du_ref, a_ref, st_ref):
        b, i = pl.program_id(0), pl.program_id(1)

        @pl.when((b == 0) & (i == 0))
        def _():
            st_ref[...] = jnp.zeros_like(st_ref)

        rows = _iota((tm, 1), 0)
        first = i == 0

        def conv(cs):
            xv = u_ref[:, cs].astype(F32)
            halo = jnp.where(first, 0.0, h_ref[:, cs].astype(F32))
            p1, p2 = _conv_shifts(xv, halo, rows)
            wv = cw_ref[:, cs]
            return xv, p1, p2, wv[2:3, :] * xv + wv[1:2, :] * p1 + wv[0:1, :] * p2 + cb_ref[:, cs]

        def stats(cs, du, xv, p1, p2):
            du_ref[:, cs] = du.astype(MXU)
            st_ref[0:1, cs] += _csum(du)
            st_ref[1:2, cs] += _csum(du * p2)
            st_ref[2:3, cs] += _csum(du * p1)
            st_ref[3:4, cs] += _csum(du * xv)

        for k in range(f // LANES):
            cg = slice(k * LANES, (k + 1) * LANES)
            cv = slice(f + k * LANES, f + (k + 1) * LANES)
            xg, g1, g2, gv = conv(cg)
            xv, v1, v2, vv = conv(cv)
            sg = _sigmoid(gv)
            sl = gv * sg
            a_ref[:, cg] = (sl * vv).astype(MXU)
            dav = da_ref[:, cg]
            stats(cg, dav * vv * (sg * (1.0 + gv * (1.0 - sg))), xg, g1, g2)
            stats(cv, dav * sl, xv, v1, v2)

    return pl.pallas_call(
        body, name=name, grid=(nb, s // tm),
        in_specs=[pl.BlockSpec((None, tm, f), lambda b, i: (b, i, 0)),
                  pl.BlockSpec((None, tm, f2), lambda b, i: (b, i, 0)),
                  pl.BlockSpec((None, HALO, f2), lambda b, i: (b, jnp.maximum(i * hb - 1, 0), 0)),
                  pl.BlockSpec((3, f2), lambda b, i: (0, 0)),
                  pl.BlockSpec((1, f2), lambda b, i: (0, 0))],
        out_specs=[pl.BlockSpec((None, tm, f2), lambda b, i: (b, i, 0)),
                   pl.BlockSpec((None, tm, f), lambda b, i: (b, i, 0)),
                   pl.BlockSpec((8, f2), lambda b, i: (0, 0))],
        out_shape=[jax.ShapeDtypeStruct((nb, s, f2), MXU), jax.ShapeDtypeStruct((nb, s, f), MXU),
                   jax.ShapeDtypeStruct((8, f2), F32)],
        compiler_params=_cp("arbitrary", "arbitrary"))(da, u, u, cw, cb)


def _rot(xv, lane):
    return jnp.where((lane >= 64) & (lane < 80), -pltpu.roll(xv, 112, 1),
                     jnp.where((lane >= 80) & (lane < 96), pltpu.roll(xv, 16, 1), 0.0))


def _rot_t(dv, lane):
    return jnp.where((lane >= 80) & (lane < 96), -pltpu.roll(dv, 16, 1),
                     jnp.where((lane >= 64) & (lane < 80), pltpu.roll(dv, 112, 1), 0.0))


def _mla_prep_specs(s, tm):
    def blk(width, col):
        return pl.BlockSpec((None, tm, width), lambda b, i: (b, i, col // width))

    full = lambda shape: pl.BlockSpec(shape, lambda b, i: (0, 0))
    return [blk(256, COL_CQ), blk(128, COL_CKV), blk(128, COL_KR),
            pl.BlockSpec((None, tm, LANES), lambda b, i: (b, i, 0)),
            pl.BlockSpec((None, tm, LANES), lambda b, i: (b, i, 0)),
            full((1, 256)), full((1, 128)), full((1, 128)), full((1, 128)),
            full((768, 256)), full((768, 128))]


def _mla_prep(proj, cs, sn, gcq, gckv, gqn, gkn, wuq, wukv, name):
    nb, s, _ = proj.shape
    tm = min(256, s)

    def body(cq_ref, ckv_ref, kr_ref, c_ref, s_ref, gcq_ref, gckv_ref, gqn_ref, gkn_ref, wuq_ref, wukv_ref,
             q_ref, k_ref, v_ref):
        lane = _iota((tm, LANES), 1)
        cv, sv = c_ref[...], s_ref[...]
        cq = cq_ref[...].astype(F32)
        cqn = cq * lax.rsqrt(jnp.mean(cq * cq, axis=-1, keepdims=True) + EPS) * gcq_ref[...]
        qb = _dot_nt(cqn.astype(MXU), wuq_ref[...])
        ckv = ckv_ref[...].astype(F32)
        ckvn = ckv * lax.rsqrt(jnp.mean(ckv * ckv, axis=-1, keepdims=True) + EPS) * gckv_ref[...]
        kvb = _dot_nt(ckvn.astype(MXU), wukv_ref[...])
        kr = kr_ref[...].astype(F32)
        for h in range(MLA_HEADS):
            hs = slice(h * LANES, (h + 1) * LANES)
            qh = qb[:, hs]
            qn = qh * lax.rsqrt(_rsum(qh * qh) / MLA_QK + EPS) * gqn_ref[...]
            q_ref[:, hs] = (qn * cv + _rot(qn, lane) * sv).astype(MXU)
            kc = jnp.where(lane < HEAD, kvb[:, hs], kr)
            kn = kc * lax.rsqrt(_rsum(kc * kc) / MLA_QK + EPS) * gkn_ref[...]
            k_ref[:, hs] = (kn * cv + _rot(kn, lane) * sv).astype(MXU)
        for j in range(MLA_HEADS // 2):
            va = kvb[:, (2 * j) * LANES:(2 * j + 1) * LANES]
            vb = kvb[:, (2 * j + 1) * LANES:(2 * j + 2) * LANES]
            v_ref[:, j * LANES:(j + 1) * LANES] = jnp.where(lane < HEAD, pltpu.roll(va, HEAD, 1), vb).astype(MXU)

    return pl.pallas_call(
        body, name=name, grid=(nb, s // tm), in_specs=_mla_prep_specs(s, tm),
        out_specs=[pl.BlockSpec((None, tm, 768), lambda b, i: (b, i, 0)),
                   pl.BlockSpec((None, tm, 768), lambda b, i: (b, i, 0)),
                   pl.BlockSpec((None, tm, 384), lambda b, i: (b, i, 0))],
        out_shape=[jax.ShapeDtypeStruct((nb, s, 768), MXU), jax.ShapeDtypeStruct((nb, s, 768), MXU),
                   jax.ShapeDtypeStruct((nb, s, 384), MXU)],
        compiler_params=_cp("parallel", "parallel"))(proj, proj, proj, cs, sn, gcq, gckv, gqn, gkn, wuq, wukv)


def _mla_prep_bwd(proj, cs, sn, gcq, gckv, gqn, gkn, wuq, wukv, dq, dk, dv, name):
    nb, s, _ = proj.shape
    tm = min(256, s)

    def body(cq_ref, ckv_ref, kr_ref, c_ref, s_ref, gcq_ref, gckv_ref, gqn_ref, gkn_ref, wuq_ref, wukv_ref,
             dq_ref, dk_ref, dv_ref,
             dcq_ref, dckv_ref, dkr_ref, dwuq_ref, dwukv_ref, dgcq_ref, dgckv_ref, dgqn_ref, dgkn_ref,
             dqb_s, dkvb_s):
        @pl.when((pl.program_id(0) == 0) & (pl.program_id(1) == 0))
        def _():
            for r in (dwuq_ref, dwukv_ref, dgcq_ref, dgckv_ref, dgqn_ref, dgkn_ref):
                r[...] = jnp.zeros_like(r)

        lane = _iota((tm, LANES), 1)
        cv, sv = c_ref[...], s_ref[...]
        gqn, gkn = gqn_ref[...], gkn_ref[...]
        cq = cq_ref[...].astype(F32)
        rc = lax.rsqrt(jnp.mean(cq * cq, axis=-1, keepdims=True) + EPS)
        chat = cq * rc
        cqn = (chat * gcq_ref[...]).astype(MXU)
        qb = _dot_nt(cqn, wuq_ref[...])
        ckv = ckv_ref[...].astype(F32)
        rkv = lax.rsqrt(jnp.mean(ckv * ckv, axis=-1, keepdims=True) + EPS)
        kvhat = ckv * rkv
        ckvn = (kvhat * gckv_ref[...]).astype(MXU)
        kvb = _dot_nt(ckvn, wukv_ref[...])
        kr = kr_ref[...].astype(F32)
        dgq = jnp.zeros((1, LANES), F32)
        dgk = jnp.zeros((1, LANES), F32)
        dkr = jnp.zeros((tm, LANES), F32)
        for h in range(MLA_HEADS):
            hs = slice(h * LANES, (h + 1) * LANES)
            qh = qb[:, hs]
            rq = lax.rsqrt(_rsum(qh * qh) / MLA_QK + EPS)
            qhat = qh * rq
            dqr = dq_ref[:, hs]
            dqn = dqr * cv + _rot_t(dqr * sv, lane)
            dgq = dgq + _csum(dqn * qhat)
            dyq = dqn * gqn
            dqb_s[:, hs] = (rq * (dyq - qhat * (_rsum(dyq * qhat) / MLA_QK))).astype(MXU)

            kc = jnp.where(lane < HEAD, kvb[:, hs], kr)
            rk = lax.rsqrt(_rsum(kc * kc) / MLA_QK + EPS)
            khat = kc * rk
            dkr_h = dk_ref[:, hs]
            dkn = dkr_h * cv + _rot_t(dkr_h * sv, lane)
            dgk = dgk + _csum(dkn * khat)
            dyk = dkn * gkn
            dkc = rk * (dyk - khat * (_rsum(dyk * khat) / MLA_QK))
            dkr = dkr + jnp.where(lane >= HEAD, dkc, 0.0)
            dvb = dv_ref[:, (h // 2) * LANES:(h // 2 + 1) * LANES]
            dvp = dvb if h % 2 == 1 else pltpu.roll(dvb, HEAD, 1)
            dkvb_s[:, hs] = jnp.where(lane < HEAD, dkc, dvp).astype(MXU)
        dgqn_ref[...] += dgq
        dgkn_ref[...] += dgk
        dkr_ref[...] = dkr

        dqb = dqb_s[...]
        dwuq_ref[...] += _dot_tn(dqb, cqn)
        dcqn = _dot(dqb, wuq_ref[...])
        dgcq_ref[...] += _csum(dcqn * chat)
        dyc = dcqn * gcq_ref[...]
        dcq_ref[...] = rc * (dyc - chat * jnp.mean(dyc * chat, axis=-1, keepdims=True))

        dkvb = dkvb_s[...]
        dwukv_ref[...] += _dot_tn(dkvb, ckvn)
        dckvn = _dot(dkvb, wukv_ref[...])
        dgckv_ref[...] += _csum(dckvn * kvhat)
        dykv = dckvn * gckv_ref[...]
        dckv_ref[...] = rkv * (dykv - kvhat * jnp.mean(dykv * kvhat, axis=-1, keepdims=True))

    full = lambda shape: pl.BlockSpec(shape, lambda b, i: (0, 0))
    tile = lambda width: pl.BlockSpec((None, tm, width), lambda b, i: (b, i, 0))
    return pl.pallas_call(
        body, name=name, grid=(nb, s // tm),
        in_specs=_mla_prep_specs(s, tm) + [tile(768), tile(768), tile(384)],
        out_specs=[tile(256), tile(128), tile(128), full((768, 256)), full((768, 128)),
                   full((1, 256)), full((1, 128)), full((1, 128)), full((1, 128))],
        out_shape=[jax.ShapeDtypeStruct((nb, s, 256), F32), jax.ShapeDtypeStruct((nb, s, 128), F32),
                   jax.ShapeDtypeStruct((nb, s, 128), F32),
                   jax.ShapeDtypeStruct((768, 256), F32), jax.ShapeDtypeStruct((768, 128), F32),
                   jax.ShapeDtypeStruct((1, 256), F32), jax.ShapeDtypeStruct((1, 128), F32),
                   jax.ShapeDtypeStruct((1, 128), F32), jax.ShapeDtypeStruct((1, 128), F32)],
        scratch_shapes=[pltpu.VMEM((tm, 768), MXU), pltpu.VMEM((tm, 768), MXU)],
        compiler_params=_cp("arbitrary", "arbitrary"))(
            proj, proj, proj, cs, sn, gcq, gckv, gqn, gkn, wuq, wukv, dq, dk, dv)


def _softplus(z):
    return jnp.maximum(z, 0.0) + jnp.log(1.0 + jnp.exp(-jnp.abs(z)))


def _sb_fwd(proj, name, riding=()):
    nb, s, _ = proj.shape
    tq, tk = min(256, s), 128
    ratio = tq // tk
    na = len(riding)
    grid = (nb, 2, s // tq)

    def body(q_ref, k_ref, v_ref, o_ref, ct_ref, cnt_ref):
        i = pl.program_id(2)
        lo = _iota((tq, LANES), 1) < HEAD
        qv = q_ref[...]
        q0 = jnp.where(lo, qv, 0.0).astype(MXU)
        q1 = jnp.where(lo, 0.0, qv).astype(MXU)
        usuf = (_iota((tk, tk), 0) > _iota((tk, tk), 1)).astype(MXU)
        tpos = i * tq + _iota((tq, tk), 0)
        scol = _iota((tq, tk), 1)
        nch = (i + 1) * ratio

        def alive(st):
            return (st[0] < nch) & (st[5] > SB_DEAD)

        def step(st):
            t, c0, a0, c1, a1, _ = st
            j = nch - 1 - t
            off = pl.multiple_of(j * tk, tk)
            kc = k_ref[pl.ds(off, tk), :].astype(MXU)
            vc = v_ref[pl.ds(off, tk), :].astype(MXU)
            msk = (scol + j * tk) < tpos

            def head(qm, c, a):
                z = _dot_nt(qm, kc) * SB_SCALE
                sp = _softplus(z)
                lk = jnp.where(msk, -sp, 0.0)
                w = jnp.where(msk, jnp.exp(z - sp + _cumdot(lk, usuf) + c), 0.0)
                return c + _rsum(lk), a + _dot(w.astype(MXU), vc)

            c0, a0 = head(q0, c0, a0)
            c1, a1 = head(q1, c1, a1)
            return t + 1, c0, a0, c1, a1, jnp.maximum(jnp.max(c0), jnp.max(c1))

        z1 = jnp.zeros((tq, 1), F32)
        za = jnp.zeros((tq, LANES), F32)
        t, c0, a0, c1, a1, _ = lax.while_loop(alive, step, (jnp.int32(0), z1, za, z1, za, jnp.float32(0.0)))
        o_ref[...] = jnp.where(lo, a0, a1)
        ct_ref[...] = jnp.where(lo, c0, c1)
        cnt_ref[...] = jnp.zeros((8, LANES), F32) + t.astype(F32)

    kv = lambda col: pl.BlockSpec((None, s, LANES), lambda b, p, i: (b, 0, col // LANES + p))
    tile = pl.BlockSpec((None, tq, LANES), lambda b, p, i: (b, i, p))
    hbm = pl.BlockSpec(memory_space=pl.ANY)
    return pl.pallas_call(
        _with_gather(body, 3, 3, na, grid), name=name, grid=grid,
        in_specs=[pl.BlockSpec((None, tq, LANES), lambda b, p, i: (b, i, COL_SBQ // LANES + p)),
                  kv(COL_SBK), kv(COL_SBV)] + [hbm] * na,
        out_specs=[tile, tile, pl.BlockSpec((None, None, None, 8, LANES), lambda b, p, i: (b, p, i, 0, 0))] + [hbm] * na,
        out_shape=[jax.ShapeDtypeStruct((nb, s, 256), F32)] * 2
        + [jax.ShapeDtypeStruct((nb, 2, s // tq, 8, LANES), F32)] + _gather_out_shapes(riding),
        scratch_shapes=_gather_sems(na) if na else [],
        compiler_params=_cp("arbitrary", "arbitrary", "arbitrary"))(proj, proj, proj, *riding)


def _sb_bwd(proj, ct, cnt, do, name):
    nb, s, _ = proj.shape
    tq, tk = min(256, s), 128
    ratio = tq // tk

    def body(q_ref, k_ref, v_ref, ct_ref, cnt_ref, do_ref, dq_ref, dk_ref, dv_ref):
        i = pl.program_id(2)

        @pl.when(i == 0)
        def _():
            dk_ref[...] = jnp.zeros_like(dk_ref)
            dv_ref[...] = jnp.zeros_like(dv_ref)

        lane = _iota((tq, LANES), 1)
        lo = lane < HEAD
        lok = _iota((tk, LANES), 1) < HEAD
        qv, dov = q_ref[...], do_ref[...]
        qb, dob = qv.astype(MXU), dov.astype(MXU)
        q0 = jnp.where(lo, qv, 0.0).astype(MXU)
        q1 = jnp.where(lo, 0.0, qv).astype(MXU)
        do0 = jnp.where(lo, dov, 0.0).astype(MXU)
        do1 = jnp.where(lo, 0.0, dov).astype(MXU)
        ctv = ct_ref[...]
        ct0 = _rsum(jnp.where(lane == 0, ctv, 0.0))
        ct1 = _rsum(jnp.where(lane == LANES - 1, ctv, 0.0))
        uincl = (_iota((tk, tk), 0) <= _iota((tk, tk), 1)).astype(MXU)
        ustrict = (_iota((tk, tk), 0) < _iota((tk, tk), 1)).astype(MXU)
        tpos = i * tq + _iota((tq, tk), 0)
        scol = _iota((tq, tk), 1)
        nch = (i + 1) * ratio

        def step(j, carry):
            p0, g0, dq0, p1, g1, dq1 = carry
            off = pl.multiple_of(j * tk, tk)
            kc = k_ref[pl.ds(off, tk), :].astype(MXU)
            vc = v_ref[pl.ds(off, tk), :].astype(MXU)
            msk = (scol + j * tk) < tpos

            def head(qm, dom, ctot, pc, gc, dqa):
                z = _dot_nt(qm, kc) * SB_SCALE
                sp = _softplus(z)
                lk = jnp.where(msk, -sp, 0.0)
                lsig = z - sp
                w = jnp.where(msk, jnp.exp(lsig + (ctot - pc - _cumdot(lk, uincl))), 0.0)
                g = w * _dot_nt(dom, vc)
                gpre = gc + _cumdot(g, ustrict)
                sig = jnp.exp(lsig)
                dz = (jnp.where(msk, g * (1.0 - sig) - sig * gpre, 0.0) * SB_SCALE).astype(MXU)
                return (pc + _rsum(lk), gc + _rsum(g), dqa + _dot(dz, kc),
                        _dot_tn(dz, qb), _dot_tn(w.astype(MXU), dob))

            p0, g0, dq0, dk0, dv0 = head(q0, do0, ct0, p0, g0, dq0)
            p1, g1, dq1, dk1, dv1 = head(q1, do1, ct1, p1, g1, dq1)
            dk_ref[pl.ds(off, tk), :] += jnp.where(lok, dk0, dk1)
            dv_ref[pl.ds(off, tk), :] += jnp.where(lok, dv0, dv1)
            return p0, g0, dq0, p1, g1, dq1

        z1 = jnp.zeros((tq, 1), F32)
        za = jnp.zeros((tq, LANES), F32)
        first = nch - jnp.max(cnt_ref[...]).astype(jnp.int32)
        _, _, dq0, _, _, dq1 = lax.fori_loop(first, nch, step, (z1, z1, za, z1, z1, za))
        dq_ref[...] = jnp.where(lo, dq0, dq1)

    kv = lambda col: pl.BlockSpec((None, s, LANES), lambda b, p, i: (b, 0, col // LANES + p))
    tile = pl.BlockSpec((None, tq, LANES), lambda b, p, i: (b, i, p))
    acc = pl.BlockSpec((None, s, LANES), lambda b, p, i: (b, 0, p))
    return pl.pallas_call(
        body, name=name, grid=(nb, 2, s // tq),
        in_specs=[pl.BlockSpec((None, tq, LANES), lambda b, p, i: (b, i, COL_SBQ // LANES + p)),
                  kv(COL_SBK), kv(COL_SBV), tile,
                  pl.BlockSpec((None, None, None, 8, LANES), lambda b, p, i: (b, p, i, 0, 0)), tile],
        out_specs=[tile, acc, acc],
        out_shape=[jax.ShapeDtypeStruct((nb, s, 256), F32)] * 3,
        compiler_params=_cp("parallel", "parallel", "arbitrary"))(proj, proj, proj, ct, cnt, do)


def _mla_fwd(q, k, v, name, riding=()):
    nb, s, _ = q.shape
    tq = tk = min(256, s)
    na = len(riding)
    grid = (nb, MLA_HEADS // 2, s // tq)

    def body(q_ref, k_ref, v_ref, o_ref, lse_ref):
        i = pl.program_id(2)
        q0, q1 = q_ref[:, :LANES], q_ref[:, LANES:]
        krow = _iota((tk, tq), 0)
        qcol = _iota((tk, tq), 1)

        def step(j, carry, diagonal):
            m0, l0, a0, m1, l1, a1 = carry
            off = pl.multiple_of(j * tk, tk)
            vc = v_ref[pl.ds(off, tk), :]

            def head(qh, kh, m, l, a):
                st = _dot_nt(kh, qh) * MLA_SCALE
                if diagonal:
                    st = jnp.where(krow <= qcol, st, NEG)
                mn = jnp.maximum(m, jnp.max(st, axis=0, keepdims=True))
                al = jnp.exp(m - mn)
                pt = jnp.exp(st - mn)
                return mn, al * l + _csum(pt), al * a + _dot_tn(vc, pt.astype(MXU))

            m0, l0, a0 = head(q0, k_ref[pl.ds(off, tk), :LANES], m0, l0, a0)
            m1, l1, a1 = head(q1, k_ref[pl.ds(off, tk), LANES:], m1, l1, a1)
            return m0, l0, a0, m1, l1, a1

        mi = jnp.full((1, tq), NEG, F32)
        z1 = jnp.zeros((1, tq), F32)
        za = jnp.zeros((LANES, tq), F32)
        carry = lax.fori_loop(0, i, lambda j, cr: step(j, cr, False), (mi, z1, za, mi, z1, za))
        m0, l0, a0, m1, l1, a1 = step(i, carry, True)
        lo_rows = _iota((LANES, tq), 0) < HEAD
        o_ref[...] = jnp.where(lo_rows, a0 / l0, a1 / l1).T
        lse_ref[...] = jnp.zeros_like(lse_ref)
        lse_ref[0:1, :] = m0 + jnp.log(l0)
        lse_ref[1:2, :] = m1 + jnp.log(l1)

    tile = pl.BlockSpec((None, tq, LANES), lambda b, p, i: (b, i, p))
    hbm = pl.BlockSpec(memory_space=pl.ANY)
    return pl.pallas_call(
        _with_gather(body, 3, 2, na, grid), name=name, grid=grid,
        in_specs=[pl.BlockSpec((None, tq, 2 * LANES), lambda b, p, i: (b, i, p)),
                  pl.BlockSpec((None, s, 2 * LANES), lambda b, p, i: (b, 0, p)),
                  pl.BlockSpec((None, s, LANES), lambda b, p, i: (b, 0, p))] + [hbm] * na,
        out_specs=[tile, pl.BlockSpec((None, None, 8, tq), lambda b, p, i: (b, p, 0, i))] + [hbm] * na,
        out_shape=[jax.ShapeDtypeStruct((nb, s, 384), F32), jax.ShapeDtypeStruct((nb, MLA_HEADS // 2, 8, s), F32)]
        + _gather_out_shapes(riding),
        scratch_shapes=_gather_sems(na) if na else [],
        compiler_params=_cp("arbitrary", "arbitrary", "arbitrary"))(q, k, v, *riding)


def _mla_bwd(q, k, v, o, lse, do, name):
    nb, s, _ = q.shape
    tq = tk = min(256, s)

    def body(q_ref, k_ref, v_ref, o_ref, lse_ref, do_ref, dq_ref, dk_ref, dv_ref):
        i = pl.program_id(2)

        @pl.when(i == 0)
        def _():
            dk_ref[...] = jnp.zeros_like(dk_ref)
            dv_ref[...] = jnp.zeros_like(dv_ref)

        lo = _iota((tq, LANES), 1) < HEAD
        lok = _iota((tk, LANES), 1) < HEAD
        q0, q1 = q_ref[:, :LANES], q_ref[:, LANES:]
        dov = do_ref[...]
        dob = dov.astype(MXU)
        do0 = jnp.where(lo, dov, 0.0).astype(MXU)
        do1 = jnp.where(lo, 0.0, dov).astype(MXU)
        dd = dov * o_ref[...]
        hi = dd.astype(MXU)
        r1 = dd - hi.astype(F32)
        mid = r1.astype(MXU)
        low = (r1 - mid.astype(F32)).astype(MXU)
        sel_lane = _iota((8, LANES), 1) < HEAD
        sel0 = sel_lane.astype(MXU)
        sel1 = (~sel_lane).astype(MXU)
        dl0 = (_dot_nt(sel0, hi) + _dot_nt(sel0, mid) + _dot_nt(sel0, low))[0:1, :]
        dl1 = (_dot_nt(sel1, hi) + _dot_nt(sel1, mid) + _dot_nt(sel1, low))[0:1, :]
        ls0, ls1 = lse_ref[0:1, :], lse_ref[1:2, :]
        krow = _iota((tk, tq), 0)
        qcol = _iota((tk, tq), 1)

        def step(j, carry, diagonal):
            dq0, dq1 = carry
            off = pl.multiple_of(j * tk, tk)
            vc = v_ref[pl.ds(off, tk), :]

            def head(qh, kh, dom, ls, dl, dqa):
                st = _dot_nt(kh, qh) * MLA_SCALE
                if diagonal:
                    st = jnp.where(krow <= qcol, st, NEG)
                pt = jnp.exp(st - ls)
                dst = (pt * (_dot_nt(vc, dom) - dl) * MLA_SCALE).astype(MXU)
                return dqa + _dot_tn(kh, dst), _dot(dst, qh), _dot(pt.astype(MXU), dob)

            dq0, dk0, dv0 = head(q0, k_ref[pl.ds(off, tk), :LANES], do0, ls0, dl0, dq0)
            dq1, dk1, dv1 = head(q1, k_ref[pl.ds(off, tk), LANES:], do1, ls1, dl1, dq1)
            dk_ref[pl.ds(off, tk), :LANES] += dk0
            dk_ref[pl.ds(off, tk), LANES:] += dk1
            dv_ref[pl.ds(off, tk), :] += jnp.where(lok, dv0, dv1)
            return dq0, dq1

        za = jnp.zeros((LANES, tq), F32)
        carry = lax.fori_loop(0, i, lambda j, cr: step(j, cr, False), (za, za))
        dq0, dq1 = step(i, carry, True)
        dq_ref[:, :LANES] = dq0.T
        dq_ref[:, LANES:] = dq1.T

    tile = pl.BlockSpec((None, tq, LANES), lambda b, p, i: (b, i, p))
    tile2 = pl.BlockSpec((None, tq, 2 * LANES), lambda b, p, i: (b, i, p))
    return pl.pallas_call(
        body, name=name, grid=(nb, MLA_HEADS // 2, s // tq),
        in_specs=[tile2,
                  pl.BlockSpec((None, s, 2 * LANES), lambda b, p, i: (b, 0, p)),
                  pl.BlockSpec((None, s, LANES), lambda b, p, i: (b, 0, p)),
                  tile, pl.BlockSpec((None, None, 8, tq), lambda b, p, i: (b, p, 0, i)), tile],
        out_specs=[tile2,
                   pl.BlockSpec((None, s, 2 * LANES), lambda b, p, i: (b, 0, p)),
                   pl.BlockSpec((None, s, LANES), lambda b, p, i: (b, 0, p))],
        out_shape=[jax.ShapeDtypeStruct((nb, s, 768), F32), jax.ShapeDtypeStruct((nb, s, 768), F32),
                   jax.ShapeDtypeStruct((nb, s, 384), F32)],
        compiler_params=_cp("parallel", "parallel", "arbitrary"))(q, k, v, o, lse, do)


def _half_stats(xv, lo):
    x2 = xv * xv
    s0 = _rsum(jnp.where(lo, x2, 0.0))
    s1 = _rsum(jnp.where(lo, 0.0, x2))
    return jnp.where(lo, lax.rsqrt(s0 / HEAD + EPS), lax.rsqrt(s1 / HEAD + EPS))


def _half_mean(xv, lo):
    s0 = _rsum(jnp.where(lo, xv, 0.0))
    s1 = _rsum(jnp.where(lo, 0.0, xv))
    return jnp.where(lo, s0, s1) / HEAD


def _swa_in_specs():
    def band(col, prev):
        if prev:
            return pl.BlockSpec((None, BLOCK, LANES), lambda b, n: (b, jnp.maximum(n - 1, 0), col // LANES))
        return pl.BlockSpec((None, BLOCK, LANES), lambda b, n: (b, n, col // LANES))

    full = lambda shape: pl.BlockSpec(shape, lambda b, n: tuple(0 for _ in shape))
    return [pl.BlockSpec((None, BLOCK, 384), lambda b, n: (b, n, COL_SWQ // 384)),
            band(COL_SWK, False), band(COL_SWK, True), band(COL_SWV, False), band(COL_SWV, True),
            full((1, LANES)), full((1, LANES)), full((8, LANES)), full((SW_HEADS, BLOCK, 2 * BLOCK))]


def _swa_valid(n):
    a = _iota((BLOCK, 2 * BLOCK), 0)
    bcol = _iota((BLOCK, 2 * BLOCK), 1)
    dist = BLOCK + a - bcol
    return (dist >= 0) & (dist < BLOCK) & ((n > 0) | (bcol >= BLOCK))


def _swa_fwd(proj, gq, gk, sinks, bias, name):
    nb, s, _ = proj.shape

    def body(q_ref, kc_ref, kp_ref, vc_ref, vp_ref, gq_ref, gk_ref, sk_ref, bias_ref, o_ref):
        n = pl.program_id(1)
        lo = _iota((BLOCK, LANES), 1) < HEAD
        lo2 = _iota((2 * BLOCK, LANES), 1) < HEAD
        kband = jnp.concatenate([kp_ref[...], kc_ref[...]], axis=0).astype(F32)
        vband = jnp.concatenate([vp_ref[...], vc_ref[...]], axis=0).astype(F32)
        kn = kband * _half_stats(kband, lo2) * gk_ref[...]
        ks = (kn.astype(MXU), pltpu.roll(kn, HEAD, 1).astype(MXU))
        vs = (vband.astype(MXU), pltpu.roll(vband, HEAD, 1).astype(MXU))
        valid = _swa_valid(n)
        for blk in range(SW_HEADS // 2):
            qv = q_ref[:, blk * LANES:(blk + 1) * LANES].astype(F32)
            qn = qv * _half_stats(qv, lo) * gq_ref[...]
            outs = []
            for half in range(2):
                h = 2 * blk + half
                swap = 0 if half == h // 3 else 1
                qm = jnp.where(lo if half == 0 else ~lo, qn, 0.0).astype(MXU)
                sc = jnp.where(valid, _dot_nt(qm, ks[swap]) * SW_SCALE + bias_ref[h], NEG)
                sk = jnp.max(sk_ref[h:h + 1, :], axis=-1, keepdims=True)
                m = jnp.maximum(jnp.max(sc, axis=-1, keepdims=True), sk)
                p = jnp.exp(sc - m)
                l = _rsum(p) + jnp.exp(sk - m)
                outs.append(_dot((p / l).astype(MXU), vs[swap]))
            o_ref[:, blk * LANES:(blk + 1) * LANES] = jnp.where(lo, outs[0], outs[1])

    return pl.pallas_call(
        body, name=name, grid=(nb, s // BLOCK), in_specs=_swa_in_specs(),
        out_specs=pl.BlockSpec((None, BLOCK, 384), lambda b, n: (b, n, 0)),
        out_shape=jax.ShapeDtypeStruct((nb, s, 384), F32),
        compiler_params=_cp("parallel", "parallel"))(proj, proj, proj, proj, proj, gq, gk, sinks, bias)


def _swa_bwd(proj, gq, gk, sinks, bias, do, name):
    nb, s, _ = proj.shape

    def body(q_ref, kc_ref, kp_ref, vc_ref, vp_ref, gq_ref, gk_ref, sk_ref, bias_ref, do_ref,
             dq_ref, dkc_ref, dkp_ref, dvc_ref, dvp_ref, dbias_ref, dsk_ref, dgq_ref, dgk_ref):
        n = pl.program_id(1)

        @pl.when((pl.program_id(0) == 0) & (n == 0))
        def _():
            for r in (dbias_ref, dsk_ref, dgq_ref, dgk_ref):
                r[...] = jnp.zeros_like(r)

        lo = _iota((BLOCK, LANES), 1) < HEAD
        lo2 = _iota((2 * BLOCK, LANES), 1) < HEAD
        kband = jnp.concatenate([kp_ref[...], kc_ref[...]], axis=0).astype(F32)
        vband = jnp.concatenate([vp_ref[...], vc_ref[...]], axis=0).astype(F32)
        rk = _half_stats(kband, lo2)
        khat = kband * rk
        gkv = gk_ref[...]
        kn = khat * gkv
        ks = (kn.astype(MXU), pltpu.roll(kn, HEAD, 1).astype(MXU))
        vs = (vband.astype(MXU), pltpu.roll(vband, HEAD, 1).astype(MXU))
        valid = _swa_valid(n)
        dkn = jnp.zeros((2 * BLOCK, LANES), F32)
        dvb = jnp.zeros((2 * BLOCK, LANES), F32)
        gqv = gq_ref[...]
        dgq = jnp.zeros((1, LANES), F32)
        for blk in range(SW_HEADS // 2):
            bs = slice(blk * LANES, (blk + 1) * LANES)
            qv = q_ref[:, bs].astype(F32)
            rq = _half_stats(qv, lo)
            qhat = qv * rq
            qn = qhat * gqv
            dov = do_ref[:, bs]
            dqn = jnp.zeros((BLOCK, LANES), F32)
            for half in range(2):
                h = 2 * blk + half
                swap = 0 if half == h // 3 else 1
                hm = lo if half == 0 else ~lo
                qm = jnp.where(hm, qn, 0.0).astype(MXU)
                dom = jnp.where(hm, dov, 0.0).astype(MXU)
                sc = jnp.where(valid, _dot_nt(qm, ks[swap]) * SW_SCALE + bias_ref[h], NEG)
                sk = jnp.max(sk_ref[h:h + 1, :], axis=-1, keepdims=True)
                m = jnp.maximum(jnp.max(sc, axis=-1, keepdims=True), sk)
                e = jnp.exp(sc - m)
                es = jnp.exp(sk - m)
                l = _rsum(e) + es
                p = e / l
                dp = _dot_nt(dom, vs[swap])
                delta = _rsum(p * dp)
                ds = p * (dp - delta)
                dsk_ref[h:h + 1, :] += jnp.broadcast_to(_csum(-(es / l) * delta), (1, LANES))
                dbias_ref[h] += ds
                dsb = (ds * SW_SCALE).astype(MXU)
                dqn = dqn + jnp.where(hm, _dot(dsb, ks[swap]), 0.0)
                rk_ = _dot_tn(dsb, qm)
                rv_ = _dot_tn(p.astype(MXU), dom)
                if swap:
                    rk_ = pltpu.roll(rk_, HEAD, 1)
                    rv_ = pltpu.roll(rv_, HEAD, 1)
                dkn = dkn + rk_
                dvb = dvb + rv_
            dgq = dgq + _csum(dqn * qhat)
            dyq = dqn * gqv
            dq_ref[:, bs] = rq * (dyq - qhat * _half_mean(dyq * qhat, lo))
        dgq_ref[...] += dgq
        dgk_ref[...] += _csum(dkn * khat)
        dyk = dkn * gkv
        dkb = rk * (dyk - khat * _half_mean(dyk * khat, lo2))
        dkp_ref[...] = dkb[:BLOCK]
        dkc_ref[...] = dkb[BLOCK:]
        dvp_ref[...] = dvb[:BLOCK]
        dvc_ref[...] = dvb[BLOCK:]

    full = lambda shape: pl.BlockSpec(shape, lambda b, n: tuple(0 for _ in shape))
    tile = pl.BlockSpec((None, BLOCK, LANES), lambda b, n: (b, n, 0))
    tile3 = pl.BlockSpec((None, BLOCK, 384), lambda b, n: (b, n, 0))
    kvs = jax.ShapeDtypeStruct((nb, s, LANES), F32)
    return pl.pallas_call(
        body, name=name, grid=(nb, s // BLOCK), in_specs=_swa_in_specs() + [tile3],
        out_specs=[tile3, tile, tile, tile, tile, full((SW_HEADS, BLOCK, 2 * BLOCK)), full((8, LANES)),
                   full((1, LANES)), full((1, LANES))],
        out_shape=[jax.ShapeDtypeStruct((nb, s, 384), F32), kvs, kvs, kvs, kvs,
                   jax.ShapeDtypeStruct((SW_HEADS, BLOCK, 2 * BLOCK), F32), jax.ShapeDtypeStruct((8, LANES), F32),
                   jax.ShapeDtypeStruct((1, LANES), F32), jax.ShapeDtypeStruct((1, LANES), F32)],
        compiler_params=_cp("arbitrary", "arbitrary"))(proj, proj, proj, proj, proj, gq, gk, sinks, bias, do)


def _bias_build(table, bucket, name):
    def body(tb_ref, bk_ref, o_ref):
        bk = bk_ref[...]
        tb = tb_ref[...]
        row = _iota((8, LANES), 0)
        col = _iota((8, LANES), 1)
        for h in range(SW_HEADS):
            acc = jnp.zeros((BLOCK, 2 * BLOCK), F32)
            for t in range(REL_BUCKETS):
                val = jnp.sum(jnp.where((row == h) & (col == t), tb, 0.0), keepdims=True)
                acc = jnp.where(bk == t, val, acc)
            o_ref[h] = acc

    return pl.pallas_call(
        body, name=name, out_shape=jax.ShapeDtypeStruct((SW_HEADS, BLOCK, 2 * BLOCK), F32))(table, bucket)


def _bias_grad(dbias, bucket, name):
    def body(db_ref, bk_ref, o_ref):
        bk = bk_ref[...]
        row = _iota((8, LANES), 0)
        col = _iota((8, LANES), 1)
        res = jnp.zeros((8, LANES), F32)
        for h in range(SW_HEADS):
            dbh = db_ref[h]
            for t in range(REL_BUCKETS):
                val = jnp.sum(jnp.where(bk == t, dbh, 0.0), keepdims=True)
                res = jnp.where((row == h) & (col == t), val, res)
        o_ref[...] = res

    return pl.pallas_call(body, name=name, out_shape=jax.ShapeDtypeStruct((8, LANES), F32))(dbias, bucket)


def _loss_grad(y, target, name):
    nb, s, d = y.shape
    tm = min(512, s)

    def body(y_ref, t_ref, loss_ref, dy_ref):
        @pl.when((pl.program_id(0) == 0) & (pl.program_id(1) == 0))
        def _():
            loss_ref[...] = jnp.zeros_like(loss_ref)

        e = y_ref[...] - t_ref[...]
        dy_ref[...] = e / d
        loss_ref[...] += 0.5 * jnp.sum(_rsum(e * e) / d, keepdims=True)

    tile = pl.BlockSpec((None, tm, d), lambda b, i: (b, i, 0))
    return pl.pallas_call(
        body, name=name, grid=(nb, s // tm), in_specs=[tile, tile],
        out_specs=[pl.BlockSpec((8, LANES), lambda b, i: (0, 0)), tile],
        out_shape=[jax.ShapeDtypeStruct((8, LANES), F32), jax.ShapeDtypeStruct((nb, s, d), F32)],
        compiler_params=_cp("arbitrary", "arbitrary"))(y, target)


def _adamw(parts, w, m, v, name):
    npart, r, ncol = parts.shape
    tr = _row_tile(r, ncol)
    bc1 = 1.0 - ADAM_B1 ** ADAM_STEP
    bc2 = 1.0 - ADAM_B2 ** ADAM_STEP

    def body(p_ref, w_ref, m_ref, v_ref, g_ref, d_ref, nm_ref, nv_ref):
        g = p_ref[0].astype(F32)
        for k in range(1, npart):
            g = g + p_ref[k].astype(F32)
        mn = ADAM_B1 * m_ref[...] + (1.0 - ADAM_B1) * g
        vn = ADAM_B2 * v_ref[...] + (1.0 - ADAM_B2) * (g * g)
        g_ref[...] = g
        nm_ref[...] = mn
        nv_ref[...] = vn
        d_ref[...] = -ADAM_LR * ((mn / bc1) / (jnp.sqrt(vn / bc2) + ADAM_EPS) + ADAM_WD * w_ref[...])

    tile = pl.BlockSpec((tr, ncol), lambda i: (i, 0))
    return pl.pallas_call(
        body, name=name, grid=(r // tr,),
        in_specs=[pl.BlockSpec((npart, tr, ncol), lambda i: (0, i, 0)), tile, tile, tile],
        out_specs=[tile] * 4, out_shape=[jax.ShapeDtypeStruct((r, ncol), F32)] * 4,
        compiler_params=_cp("parallel"))(parts, w, m, v)


def _unpack(flat, shapes, lead=()):
    out, off = [], 0
    for shp in shapes:
        size = 1
        for dim in shp:
            size *= dim
        out.append(flat[..., off:off + size].reshape(lead + tuple(shp)))
        off += size
    return out


def _t5_bucket():
    a = jnp.arange(BLOCK)[:, None]
    b = jnp.arange(2 * BLOCK)[None, :]
    dist = BLOCK + a - b
    max_exact = REL_BUCKETS // 2
    nn = jnp.maximum(dist, 0)
    nf = jnp.maximum(nn, 1).astype(F32)
    large = max_exact + (jnp.log(nf / max_exact) / math.log(BLOCK / max_exact)
                         * (REL_BUCKETS - max_exact)).astype(jnp.int32)
    large = jnp.minimum(large, REL_BUCKETS - 1)
    return jnp.where(nn < max_exact, nn, large).astype(jnp.int32)


def _pad_lanes(g, n):
    return jnp.pad(g, (0, n - g.shape[0])).reshape(1, n)


def kernel(x, c, positions, rel_table, norm1_g, norm2_g, w_ada, b_ada, w_in, mla_cq_g, w_uq, mla_ckv_g, w_ukv, mla_qn_g, mla_kn_g, sw_qn_g, sw_kn_g, sw_sinks, w_out, w_up, conv_w, conv_b, w_down, loss_target, m_rel_table, m_norm1_g, m_norm2_g, m_w_ada, m_b_ada, m_w_in, m_mla_cq_g, m_w_uq, m_mla_ckv_g, m_w_ukv, m_mla_qn_g, m_mla_kn_g, m_sw_qn_g, m_sw_kn_g, m_sw_sinks, m_w_out, m_w_up, m_conv_w, m_conv_b, m_w_down, v_rel_table, v_norm1_g, v_norm2_g, v_w_ada, v_b_ada, v_w_in, v_mla_cq_g, v_w_uq, v_mla_ckv_g, v_w_ukv, v_mla_qn_g, v_mla_kn_g, v_sw_qn_g, v_sw_kn_g, v_sw_sinks, v_w_out, v_w_up, v_conv_w, v_conv_b, v_w_down):
    nb, s, d = x.shape
    nl = norm1_g.shape[0]
    me = 4 * lax.axis_index("x") + 2 * lax.axis_index("y") + lax.axis_index("c")
    n_ada = w_ada.shape[2]

    shard = lambda w, l, transposed: (jnp.swapaxes(w[l], 0, 1) if transposed else w[l]).astype(MXU)
    attn_local = lambda l: [shard(w_in, l, True), shard(w_uq, l, True), shard(w_ukv, l, True), shard(w_out, l, False)]
    ffn_local = lambda l: [shard(w_up, l, True), shard(w_down, l, False)]
    full = lambda a: a.reshape(-1, a.shape[-1])
    zrows = lambda n: jnp.zeros((n, d), MXU)
    pad_in = lambda wt: jnp.concatenate([wt[:1152], wt[1184:1824], zrows(64), wt[1152:1184], zrows(160)], axis=0)
    pad_uq = lambda wt: jnp.pad(wt.reshape(MLA_HEADS, MLA_QK, 256), ((0, 0), (0, LANES - MLA_QK), (0, 0))).reshape(768, 256)
    got = _all_gather(attn_local(0) + [conv_w.reshape(-1, conv_w.shape[-1]), c], "gather_inputs")
    w_in_pt, w_uq_pt, w_ukv_t, w_out_f = [pad_in(full(got[0]))], [pad_uq(full(got[1]))], [full(got[2])], [full(got[3])]
    w_up_t, w_down_f = [], []
    conv_full = got[4].reshape(N_DEV, nl, 3, -1).transpose(1, 2, 0, 3).reshape(nl, 3, -1)
    c_all = got[5].reshape(N_DEV * nb, d)

    b_my = lax.dynamic_slice_in_dim(b_ada, me * n_ada, n_ada, axis=1).reshape(nl, 1, n_ada)
    mods_my = _ada_fwd(c_all, w_ada, b_my, "ada_fwd")
    mods, = _all_gather([mods_my.reshape(nl * N_DEV * nb, n_ada)], "gather_mods")
    mods = mods.reshape(N_DEV, nl, N_DEV * nb, n_ada).transpose(1, 2, 0, 3).reshape(nl, N_DEV * nb, N_DEV * n_ada)
    mods = lax.dynamic_slice_in_dim(mods, me * nb, nb, axis=1)
    shift1, scale1, gate1, shift2, scale2, gate2 = [mods[:, :, k * d:(k + 1) * d].reshape(nl, nb, 1, d) for k in range(6)]

    half = 16
    inv_freq = jnp.power(ROPE_THETA, -jnp.arange(half, dtype=F32) / half)
    ang = positions.astype(F32)[..., None] * inv_freq
    ones = lambda n: jnp.ones((nb, s, n), F32)
    zeros = lambda n: jnp.zeros((nb, s, n), F32)
    rope_c = jnp.concatenate([ones(64), jnp.cos(ang), jnp.cos(ang), ones(32)], axis=-1)
    rope_s = jnp.concatenate([zeros(64), jnp.sin(ang), jnp.sin(ang), zeros(32)], axis=-1)
    bucket = _t5_bucket()
    bias = _bias_build(jnp.pad(rel_table.T, ((0, 8 - SW_HEADS), (0, LANES - REL_BUCKETS))), bucket, "rel_bias")

    row = lambda g: g.reshape(1, -1)
    twice = lambda g: jnp.concatenate([g, g]).reshape(1, LANES)

    saved = []
    xl = x
    for l in range(nl):
        proj, h1 = _ln_mod_matmul(xl, row(norm1_g[l]), scale1[l], shift1[l], w_in_pt[l], f"l{l}_in_proj")
        prep_args = (proj, rope_c, rope_s, row(mla_cq_g[l]), row(mla_ckv_g[l]), _pad_lanes(mla_qn_g[l], LANES),
                     _pad_lanes(mla_kn_g[l], LANES), w_uq_pt[l], w_ukv_t[l])
        qm, km, vm = _mla_prep(*prep_args, f"l{l}_mla_prep")
        o_a, ct_a, cnt_a, up_g, down_g = _sb_fwd(proj, f"l{l}_sb_fwd", riding=ffn_local(l))
        w_up_t.append(full(up_g))
        w_down_f.append(full(down_g))
        o_b, lse_b, *nxt = _mla_fwd(qm, km, vm, f"l{l}_mla_fwd", riding=attn_local(l + 1) if l + 1 < nl else ())
        if nxt:
            w_in_pt.append(pad_in(full(nxt[0])))
            w_uq_pt.append(pad_uq(full(nxt[1])))
            w_ukv_t.append(full(nxt[2]))
            w_out_f.append(full(nxt[3]))
        sinks = jnp.broadcast_to(jnp.pad(sw_sinks[l], (0, 2))[:, None], (8, LANES))
        swa_args = (proj, twice(sw_qn_g[l]), twice(sw_kn_g[l]), sinks, bias)
        o_c = _swa_fwd(*swa_args, f"l{l}_swa_fwd")
        wo = [w_out_f[l][:256], w_out_f[l][256:640], w_out_f[l][640:]]
        x_mid, y1 = _out_proj([o_a, o_b, o_c], wo, gate1[l], xl, f"l{l}_out_proj")
        u_pre, h2 = _ln_mod_matmul(x_mid, row(norm2_g[l]), scale2[l], shift2[l], w_up_t[l], f"l{l}_up_proj")
        x_out, y2 = _conv_gate_matmul(u_pre, conv_full[l], row(conv_b[l]), w_down_f[l], gate2[l], x_mid, f"l{l}_ffn_down")
        saved.append(dict(x=xl, proj=proj, h1=h1, prep=prep_args, qkv=(qm, km, vm), o_a=o_a, ct_a=ct_a, cnt_a=cnt_a, o_b=o_b, lse_b=lse_b,
                          swa=swa_args, o_c=o_c, wo=wo, y1=y1, x_mid=x_mid, u_pre=u_pre, h2=h2, y2=y2))
        xl = x_out

    loss_blk, dx = _loss_grad(xl, loss_target, "loss")
    loss = lax.psum(loss_blk[0, 0], ("x", "y", "c"))

    t = nb * s
    flat = lambda a: a.reshape(t, a.shape[-1])
    grads = [None] * nl
    dmods = [None] * nl
    sharded_out = [None] * nl
    sharded_names = ["w_in", "w_uq", "w_ukv", "w_up", "w_out", "w_down", "conv_w"]
    sharded_wmv = dict(w_in=(w_in, m_w_in, v_w_in), w_uq=(w_uq, m_w_uq, v_w_uq), w_ukv=(w_ukv, m_w_ukv, v_w_ukv),
                       w_up=(w_up, m_w_up, v_w_up), w_out=(w_out, m_w_out, v_w_out), w_down=(w_down, m_w_down, v_w_down),
                       conv_w=(conv_w, m_conv_w, v_conv_w))
    n_in, n_up, n_out, n_dn = w_in.shape[2], w_up.shape[2], w_out.shape[1], w_down.shape[1]
    small_sizes = [w_uq[0].size, w_ukv[0].size, conv_w[0].size]
    n_small_rows = -(-sum(small_sizes) // d)
    rows_used = n_in + n_out + n_small_rows
    rows_grad = -(-rows_used // 16) * 16

    def pack_rows(mats, vecs):
        lead = mats[0].shape[:-2]
        flat_part = jnp.concatenate(vecs, axis=-1)
        flat_part = jnp.pad(flat_part, [(0, 0)] * len(lead) + [(0, n_small_rows * d - flat_part.shape[-1])])
        tail = jnp.zeros(lead + (rows_grad - rows_used, d), F32)
        return jnp.concatenate(list(mats) + [flat_part.reshape(lead + (n_small_rows, d)), tail], axis=-2)

    def unpack_rows(a):
        o1, o2 = n_in, n_in + n_out
        flat_part = a[o2:o2 + n_small_rows].reshape(-1)
        s1, s2, s3 = small_sizes[0], small_sizes[0] + small_sizes[1], sum(small_sizes)
        return dict(w_in=a[:o1].T, w_out=a[o1:o2],
                    w_uq=flat_part[:s1].reshape(w_uq.shape[2], -1).T, w_ukv=flat_part[s1:s2].reshape(w_ukv.shape[2], -1).T,
                    conv_w=flat_part[s2:s3].reshape(conv_w.shape[1:]))

    def update_sharded(l, recv):
        wmv = [{k: v[o][l] for k, v in sharded_wmv.items()} for o in range(3)]
        res_up = _adamw(recv[0], *[a["w_up"].T for a in wmv], f"l{l}_adamw_up")
        res_dn = _adamw(recv[1], *[a["w_down"] for a in wmv], f"l{l}_adamw_down")
        res_rest = _adamw(recv[2], *[pack_rows([a["w_in"].T, a["w_out"]], [a["w_uq"].T.reshape(-1), a["w_ukv"].T.reshape(-1),
                                                                            a["conv_w"].reshape(-1)]) for a in wmv],
                          f"l{l}_adamw_rest")
        sharded_out[l] = [dict(unpack_rows(rr), w_up=ru.T, w_down=rd) for ru, rd, rr in zip(res_up, res_dn, res_rest)]

    pending = None
    dbias = jnp.zeros((SW_HEADS, BLOCK, 2 * BLOCK), F32)
    for l in reversed(range(nl)):
        sv = saved[l]
        (da,), dy2, dgate2 = _gate_bwd_nt(dx, sv["y2"], gate2[l], [w_down_f[l]], f"l{l}_ffn_down_bwd")
        du, a_act, cstats = _conv_gate_bwd(da, sv["u_pre"], conv_full[l], row(conv_b[l]), f"l{l}_conv_gate_bwd")
        res = _ln_mod_matmul_bwd(du, w_up_t[l], sv["x_mid"], row(norm2_g[l]), scale2[l], dx, conv_full[l],
                                 f"l{l}_up_proj_bwd", riding=pending[1] if pending else ())
        dx_mid, du_pre, dshift2, dscale2, dg2 = res[:5]
        if pending:
            update_sharded(pending[0], res[5:])
            pending = None
        g_w_down = _wgrad(flat(a_act), flat(dy2), f"l{l}_w_down_grad")
        g_w_up_t = _wgrad(flat(du_pre), flat(sv["h2"]), f"l{l}_w_up_grad")

        (do_a, do_b, do_c), dy1, dgate1 = _gate_bwd_nt(dx_mid, sv["y1"], gate1[l], sv["wo"], f"l{l}_out_proj_bwd")
        mix = jnp.concatenate([sv["o_a"], sv["o_b"], sv["o_c"]], axis=-1).astype(MXU)
        g_w_out = _wgrad(flat(mix), flat(dy1), f"l{l}_w_out_grad")

        dsb_q, dsb_k, dsb_v = _sb_bwd(sv["proj"], sv["ct_a"], sv["cnt_a"], do_a, f"l{l}_sb_bwd")
        qm, km, vm = sv["qkv"]
        dqm, dkm, dvm = _mla_bwd(qm, km, vm, sv["o_b"], sv["lse_b"], do_b, f"l{l}_mla_bwd")
        dsw_q, dkc, dkp, dvc, dvp, dbias_l, dsinks, dg_swq, dg_swk = _swa_bwd(*sv["swa"], do_c, f"l{l}_swa_bwd")
        dbias = dbias + dbias_l
        shift_up = lambda a: jnp.concatenate([a[:, BLOCK:], jnp.zeros((nb, BLOCK, LANES), F32)], axis=1)
        dsw_k = dkc + shift_up(dkp)
        dsw_v = dvc + shift_up(dvp)
        dcq, dckv, dkr, g_w_uq_pt, g_w_ukv_t, dg_cq, dg_ckv, dg_qn, dg_kn = _mla_prep_bwd(
            *sv["prep"], dqm, dkm, dvm, f"l{l}_mla_prep_bwd")
        dproj = jnp.concatenate([dsb_q, dsb_k, dsb_v, dcq, dckv, dsw_q, dsw_k, dsw_v, dkr, zeros(128)], axis=-1)
        dx, dproj_m, dshift1, dscale1, dg1 = _ln_mod_matmul_bwd(
            dproj, w_in_pt[l], sv["x"], row(norm1_g[l]), scale1[l], dx_mid, None, f"l{l}_in_proj_bwd")
        g_w_in_pt = _wgrad(flat(dproj_m), flat(sv["h1"]), f"l{l}_w_in_grad")

        g_w_in_t = jnp.concatenate([g_w_in_pt[:1152], g_w_in_pt[1856:1888], g_w_in_pt[1152:1792]], axis=0)
        g_w_uq_t = g_w_uq_pt.reshape(MLA_HEADS, LANES, 256)[:, :MLA_QK].reshape(MLA_HEADS * MLA_QK, 256)
        dmods[l] = jnp.concatenate([dshift1, dscale1, dgate1, dshift2, dscale2, dgate2], axis=-1).reshape(nb, 6 * d)

        per_dev = lambda g: g.reshape(N_DEV, -1, d)
        conv_dev = cstats[1:4].reshape(3, N_DEV, -1).transpose(1, 0, 2)
        rest = pack_rows([per_dev(g_w_in_t), per_dev(g_w_out)],
                         [g_w_uq_t.reshape(N_DEV, -1), g_w_ukv_t.reshape(N_DEV, -1), conv_dev.reshape(N_DEV, -1)])
        send = [per_dev(g_w_up_t), per_dev(g_w_down), rest]
        sib = _pair_exchange(send, f"l{l}_pair_exchange")
        core = lax.axis_index("c").reshape(1).astype(jnp.int32)
        pair = [_pair_add(core, a, b, f"l{l}_pair_add_{k}") for a, b, k in zip(send, sib, ("up", "down", "rest"))]
        if l > 0:
            pending = (l, pair)
        else:
            update_sharded(l, _chip_exchange(pair, f"l{l}_chip_exchange"))
        grads[l] = dict(
            norm1_g=dg1[0], norm2_g=dg2[0], mla_cq_g=dg_cq[0], mla_ckv_g=dg_ckv[0], mla_qn_g=dg_qn[0, :MLA_QK],
            mla_kn_g=dg_kn[0, :MLA_QK], sw_qn_g=dg_swq[0, :HEAD] + dg_swq[0, HEAD:], sw_kn_g=dg_swk[0, :HEAD] + dg_swk[0, HEAD:],
            sw_sinks=dsinks[:SW_HEADS, 0], conv_b=cstats[0])
    grad_x = dx
    g_rel = _bias_grad(dbias, bucket, "rel_table_grad")[:SW_HEADS, :REL_BUCKETS].T
    stack = lambda k: jnp.stack([grads[l][k] for l in range(nl)])

    dm_all, = _all_gather([jnp.stack(dmods).reshape(nl * nb, 6 * d)], "gather_dmods")
    dm_all = dm_all.reshape(N_DEV, nl, nb, 6 * d).transpose(1, 0, 2, 3).reshape(nl, N_DEV * nb, 6 * d)
    dm_my = lax.dynamic_slice_in_dim(dm_all, me * n_ada, n_ada, axis=2)
    g_w_ada, g_b_ada = _ada_bwd(c_all, dm_my, dm_all, "ada_bwd")
    g_b_ada = g_b_ada.reshape(nl, 6 * d)

    big_out = [{k: jnp.stack([sharded_out[l][o][k] for l in range(nl)]) for k in sharded_names} for o in range(4)]
    packf = lambda dct, names, rows: jnp.pad(jnp.concatenate([dct[k].reshape(-1) for k in names]),
                                             (0, rows * LANES - sum(dct[k].size for k in names))).reshape(rows, LANES)

    small_names = ["rel_table", "norm1_g", "norm2_g", "mla_cq_g", "mla_ckv_g", "mla_qn_g", "mla_kn_g",
                   "sw_qn_g", "sw_kn_g", "sw_sinks", "conv_b"]
    small_w = dict(rel_table=rel_table, norm1_g=norm1_g, norm2_g=norm2_g, mla_cq_g=mla_cq_g, mla_ckv_g=mla_ckv_g,
                   mla_qn_g=mla_qn_g, mla_kn_g=mla_kn_g, sw_qn_g=sw_qn_g, sw_kn_g=sw_kn_g, sw_sinks=sw_sinks, conv_b=conv_b)
    small_m = dict(rel_table=m_rel_table, norm1_g=m_norm1_g, norm2_g=m_norm2_g, mla_cq_g=m_mla_cq_g, mla_ckv_g=m_mla_ckv_g,
                   mla_qn_g=m_mla_qn_g, mla_kn_g=m_mla_kn_g, sw_qn_g=m_sw_qn_g, sw_kn_g=m_sw_kn_g, sw_sinks=m_sw_sinks, conv_b=m_conv_b)
    small_v = dict(rel_table=v_rel_table, norm1_g=v_norm1_g, norm2_g=v_norm2_g, mla_cq_g=v_mla_cq_g, mla_ckv_g=v_mla_ckv_g,
                   mla_qn_g=v_mla_qn_g, mla_kn_g=v_mla_kn_g, sw_qn_g=v_sw_qn_g, sw_kn_g=v_sw_kn_g, sw_sinks=v_sw_sinks, conv_b=v_conv_b)
    small_g = {k: (g_rel if k == "rel_table" else stack(k)) for k in small_names}
    n_small = sum(small_w[k].size for k in small_names)
    rows_small = -(-n_small // (8 * LANES)) * 8
    small_parts, = _all_gather([packf(small_g, small_names, rows_small)], "gather_small_grads")
    small_out = _adamw(small_parts, packf(small_w, small_names, rows_small), packf(small_m, small_names, rows_small),
                       packf(small_v, small_names, rows_small), "adamw_replicated")
    small_out = [dict(zip(small_names, _unpack(o.reshape(-1), [small_w[k].shape for k in small_names]))) for o in small_out]

    two_d = lambda a: a.reshape(-1, a.shape[-1])
    res_w = _adamw(two_d(g_w_ada)[None], two_d(w_ada), two_d(m_w_ada), two_d(v_w_ada), "adamw_w_ada")
    res_b = _adamw(g_b_ada[None], b_ada, m_b_ada, v_b_ada, "adamw_b_ada")
    ada_out = [dict(w_ada=rw.reshape(w_ada.shape), b_ada=rb) for rw, rb in zip(res_w, res_b)]

    order = ["rel_table", "norm1_g", "norm2_g", "w_ada", "b_ada", "w_in", "mla_cq_g", "w_uq", "mla_ckv_g", "w_ukv",
             "mla_qn_g", "mla_kn_g", "sw_qn_g", "sw_kn_g", "sw_sinks", "w_out", "w_up", "conv_w", "conv_b", "w_down"]
    outs = [{**big_out[k], **small_out[k], **ada_out[k]} for k in range(4)]
    return (loss, grad_x, *[outs[0][n] for n in order], *[outs[1][n] for n in order],
            *[outs[2][n] for n in order], *[outs[3][n] for n in order])
```

```python
import math

import jax
import jax.numpy as jnp
from jax import lax
from jax.experimental import pallas as pl
from jax.experimental.pallas import tpu as pltpu

F32 = jnp.float32
MXU = jnp.bfloat16
EPS = 1e-6
NEG = -1e30
VMEM_LIMIT_BYTES = 56 * 1024 * 1024
N_DEV = 8
MESH = pl.DeviceIdType.MESH

D_MODEL = 1024
D_FF = 2816
HEAD = 64
LANES = 128
MLA_HEADS = 6
MLA_QK = 96
SW_HEADS = 6
REL_BUCKETS = 32
BLOCK = 128
SB_SCALE = HEAD ** -0.5
SB_DEAD = -105.0
SW_SCALE = HEAD ** -0.5
MLA_SCALE = MLA_QK ** -0.5
ROPE_THETA = 10000.0
D_IN_PAD = 2048
COL_SBQ, COL_SBK, COL_SBV, COL_CQ, COL_CKV, COL_SWQ, COL_SWK, COL_SWV, COL_KR = 0, 256, 512, 768, 1024, 1152, 1536, 1664, 1792

HALO = 16
ROW_TILE_BYTES = 1 << 20
ADAM_LR, ADAM_B1, ADAM_B2, ADAM_EPS, ADAM_WD, ADAM_STEP = 0.001, 0.9, 0.999, 1e-08, 0.01, 10


def _cp(*sem):
    return pltpu.CompilerParams(dimension_semantics=sem, vmem_limit_bytes=VMEM_LIMIT_BYTES)


def _iota(shape, dim):
    return lax.broadcasted_iota(jnp.int32, shape, dim)


def _dot(a, b):
    return jnp.dot(a, b, preferred_element_type=F32)


def _dot_nt(a, b):
    return lax.dot_general(a, b, (((1,), (1,)), ((), ())), preferred_element_type=F32)


def _dot_tn(a, b):
    return lax.dot_general(a, b, (((0,), (0,)), ((), ())), preferred_element_type=F32)


def _cumdot(x, u):
    hi = x.astype(MXU)
    mid = (x - hi.astype(F32)).astype(MXU)
    return _dot(hi, u) + _dot(mid, u)


def _sigmoid(x):
    return 1.0 / (1.0 + jnp.exp(-x))


def _rsum(x):
    return jnp.sum(x, axis=-1, keepdims=True)


def _csum(x):
    return jnp.sum(x, axis=0, keepdims=True)


def _all_gather(xs, name):
    na = len(xs)

    def body(*refs):
        start, finish = _gather_steps(refs[:na], refs[na:2 * na], *refs[2 * na:])
        start()
        finish()

    hbm = pl.BlockSpec(memory_space=pl.ANY)
    return pl.pallas_call(
        body, name=name, out_shape=_gather_out_shapes(xs), in_specs=[hbm] * na, out_specs=[hbm] * na,
        scratch_shapes=_gather_sems(na))(*xs)


def _gather_out_shapes(xs):
    return [jax.ShapeDtypeStruct((N_DEV,) + a.shape, a.dtype) for a in xs]


def _gather_sems(na):
    return [pltpu.SemaphoreType.DMA((7 * na,)), pltpu.SemaphoreType.DMA((7 * na,)), pltpu.SemaphoreType.DMA((na,))]


def _gather_steps(x_refs, out_refs, send_sems, recv_sems, local_sems):
    na = len(x_refs)
    x, y, c = lax.axis_index("x"), lax.axis_index("y"), lax.axis_index("c")
    me, sibling = (x, y, c), (x, y, 1 - c)
    chips = [(1 - x, y), (x, 1 - y), (1 - x, 1 - y)]

    def slot(a, px, py, pc):
        return out_refs[a].at[4 * px + 2 * py + pc]

    def copy(a, k, block, to, src=None):
        return pltpu.make_async_remote_copy(
            src_ref=slot(a, *block) if src is None else src, dst_ref=slot(a, *block),
            send_sem=send_sems.at[7 * a + k], recv_sem=recv_sems.at[7 * a + k], device_id=to, device_id_type=MESH)

    def own_copies(a):
        return ([copy(a, 0, me, sibling, src=x_refs[a])]
                + [copy(a, 1 + j, me, (*chip, c), src=x_refs[a]) for j, chip in enumerate(chips)])

    def local_copy(a):
        return pltpu.make_async_copy(x_refs[a], slot(a, *me), local_sems.at[a])

    def start():
        for a in range(na):
            local_copy(a).start()
            for cp in own_copies(a):
                cp.start()

    def finish():
        passed = []
        for j, chip in enumerate(chips):
            for a in range(na):
                copy(a, 1 + j, (*chip, c), me).wait_recv()
                passed.append(copy(a, 4 + j, (*chip, c), sibling))
                passed[-1].start()
        for a in range(na):
            copy(a, 0, sibling, me).wait_recv()
            for j, chip in enumerate(chips):
                copy(a, 4 + j, (*chip, 1 - c), me).wait_recv()
        for a in range(na):
            for cp in own_copies(a):
                cp.wait_send()
        for cp in passed:
            cp.wait_send()
        for a in range(na):
            local_copy(a).wait()

    return start, finish


def _with_gather(body, n_in, n_out, na, grid):
    if not na:
        return body

    def wrapped(*refs):
        ins, ride_in = refs[:n_in], refs[n_in:n_in + na]
        outs = refs[n_in + na:n_in + na + n_out]
        ride_out = refs[n_in + na + n_out:n_in + 2 * na + n_out]
        ids = [pl.program_id(k) for k in range(len(grid))]
        first, last = ids[0] == 0, ids[0] == grid[0] - 1
        for k in range(1, len(grid)):
            first, last = first & (ids[k] == 0), last & (ids[k] == grid[k] - 1)
        start, finish = _gather_steps(ride_in, ride_out, *refs[n_in + 2 * na + n_out:])
        pl.when(first)(start)
        body(*ins, *outs)
        pl.when(last)(finish)

    return wrapped


def _pair_exchange(xs, name):
    na = len(xs)

    def body(*refs):
        x_refs, out_refs = refs[:na], refs[na:2 * na]
        send_sems, recv_sems = refs[2 * na:]
        x, y, c = lax.axis_index("x"), lax.axis_index("y"), lax.axis_index("c")
        copies = []
        for a in range(na):
            for q in range(4):
                copies.append(pltpu.make_async_remote_copy(
                    src_ref=x_refs[a].at[2 * q + 1 - c], dst_ref=out_refs[a].at[q],
                    send_sem=send_sems.at[4 * a + q], recv_sem=recv_sems.at[4 * a + q],
                    device_id=(x, y, 1 - c), device_id_type=MESH))
                copies[-1].start()
        for cp in copies:
            cp.wait()

    hbm = pl.BlockSpec(memory_space=pl.ANY)
    return pl.pallas_call(
        body, name=name, out_shape=[jax.ShapeDtypeStruct((4,) + a.shape[1:], a.dtype) for a in xs],
        in_specs=[hbm] * na, out_specs=[hbm] * na,
        scratch_shapes=[pltpu.SemaphoreType.DMA((4 * na,)), pltpu.SemaphoreType.DMA((4 * na,))])(*xs)


def _row_tile(r, ncol):
    if r * ncol * 4 <= ROW_TILE_BYTES:
        return r
    return max(t for t in range(16, r, 16) if r % t == 0 and t * ncol * 4 <= ROW_TILE_BYTES)


def _pair_add(core, xs, sib, name):
    _, r, ncol = xs.shape
    tr = _row_tile(r, ncol)

    def body(c_ref, x_ref, s_ref, o_ref):
        o_ref[...] = (x_ref[...] + s_ref[...]).astype(MXU)

    return pl.pallas_call(
        body, name=name,
        grid_spec=pltpu.PrefetchScalarGridSpec(
            num_scalar_prefetch=1, grid=(4, r // tr),
            in_specs=[pl.BlockSpec((None, tr, ncol), lambda q, i, c_ref: (2 * q + c_ref[0], i, 0)),
                      pl.BlockSpec((None, tr, ncol), lambda q, i, c_ref: (q, i, 0))],
            out_specs=pl.BlockSpec((None, tr, ncol), lambda q, i, c_ref: (q, i, 0))),
        out_shape=jax.ShapeDtypeStruct((4, r, ncol), MXU),
        compiler_params=_cp("parallel", "parallel"))(core, xs, sib)


def _chip_exchange(xs, name):
    na = len(xs)

    def body(*refs):
        copies = _chip_exchange_copies(refs[:na], refs[na:2 * na], *refs[2 * na:])
        for cp in copies:
            cp.start()
        for cp in copies:
            cp.wait()

    hbm = pl.BlockSpec(memory_space=pl.ANY)
    return pl.pallas_call(
        body, name=name, out_shape=[jax.ShapeDtypeStruct(a.shape, a.dtype) for a in xs],
        in_specs=[hbm] * na, out_specs=[hbm] * na, scratch_shapes=_chip_exchange_sems(na))(*xs)


def _with_chip_exchange(body, n_in, n_out, na, grid):
    if not na:
        return body

    def wrapped(*refs):
        ins, ride_in = refs[:n_in], refs[n_in:n_in + na]
        outs = refs[n_in + na:n_in + na + n_out]
        ride_out = refs[n_in + na + n_out:n_in + 2 * na + n_out]
        ids = [pl.program_id(k) for k in range(len(grid))]
        first, last = ids[0] == 0, ids[0] == grid[0] - 1
        for k in range(1, len(grid)):
            first, last = first & (ids[k] == 0), last & (ids[k] == grid[k] - 1)
        copies = _chip_exchange_copies(ride_in, ride_out, *refs[n_in + 2 * na + n_out:])

        @pl.when(first)
        def _():
            for cp in copies:
                cp.start()

        body(*ins, *outs)

        @pl.when(last)
        def _():
            for cp in copies:
                cp.wait()

    return wrapped


def _chip_exchange_sems(na):
    return [pltpu.SemaphoreType.DMA((3 * na,)), pltpu.SemaphoreType.DMA((3 * na,)), pltpu.SemaphoreType.DMA((na,))]


def _chip_exchange_copies(x_refs, out_refs, send_sems, recv_sems, local_sems):
    x, y, c = lax.axis_index("x"), lax.axis_index("y"), lax.axis_index("c")
    me = 2 * x + y
    copies = [pltpu.make_async_copy(x_refs[a].at[me], out_refs[a].at[me], local_sems.at[a]) for a in range(len(x_refs))]
    for k, (dx, dy) in enumerate([(1, 0), (0, 1), (1, 1)]):
        px = 1 - x if dx else x
        py = 1 - y if dy else y
        for a in range(len(x_refs)):
            copies.append(pltpu.make_async_remote_copy(
                src_ref=x_refs[a].at[2 * px + py], dst_ref=out_refs[a].at[me],
                send_sem=send_sems.at[3 * a + k], recv_sem=recv_sems.at[3 * a + k],
                device_id=(px, py, c), device_id_type=MESH))
    return copies


def _ada_fwd(c_all, w_ada, b_my, name):
    nl, d, n = w_ada.shape
    nb = c_all.shape[0]

    def body(c_ref, w_ref, b_ref, o_ref):
        cv = c_ref[...]
        sc = (cv * _sigmoid(cv)).astype(MXU)
        o_ref[...] = _dot(sc, w_ref[...].astype(MXU)) + b_ref[...]

    return pl.pallas_call(
        body, name=name, grid=(nl,),
        in_specs=[pl.BlockSpec((nb, d), lambda l: (0, 0)),
                  pl.BlockSpec((None, d, n), lambda l: (l, 0, 0)),
                  pl.BlockSpec((None, 1, n), lambda l: (l, 0, 0))],
        out_specs=pl.BlockSpec((None, nb, n), lambda l: (l, 0, 0)),
        out_shape=jax.ShapeDtypeStruct((nl, nb, n), F32),
        compiler_params=_cp("parallel"))(c_all, w_ada, b_my)


def _ada_bwd(c_all, dmods_my, dmods_all, name):
    nl, nb, n = dmods_my.shape
    d = c_all.shape[1]
    nfull = dmods_all.shape[2]

    def body(c_ref, dm_ref, da_ref, dw_ref, db_ref):
        cv = c_ref[...]
        sc = (cv * _sigmoid(cv)).astype(MXU)
        dw_ref[...] = _dot_tn(sc, dm_ref[...].astype(MXU))
        db_ref[...] = _csum(da_ref[...])

    return pl.pallas_call(
        body, name=name, grid=(nl,),
        in_specs=[pl.BlockSpec((nb, d), lambda l: (0, 0)),
                  pl.BlockSpec((None, nb, n), lambda l: (l, 0, 0)),
                  pl.BlockSpec((None, nb, nfull), lambda l: (l, 0, 0))],
        out_specs=[pl.BlockSpec((None, d, n), lambda l: (l, 0, 0)),
                   pl.BlockSpec((None, 1, nfull), lambda l: (l, 0, 0))],
        out_shape=[jax.ShapeDtypeStruct((nl, d, n), F32), jax.ShapeDtypeStruct((nl, 1, nfull), F32)],
        compiler_params=_cp("parallel"))(c_all, dmods_my, dmods_all)


def _ln_mod_matmul(x, g, scale, shift, w, name):
    nb, s, d = x.shape
    n = w.shape[0]
    tm, tn = min(1024, s), 512

    def body(x_ref, g_ref, sc_ref, sh_ref, w_ref, y_ref, h_ref, h_s):
        @pl.when(pl.program_id(2) == 0)
        def _():
            xf = x_ref[...]
            rstd = lax.rsqrt(jnp.mean(xf * xf, axis=-1, keepdims=True) + EPS)
            hv = (xf * rstd * g_ref[...]) * (1.0 + sc_ref[...]) + sh_ref[...]
            h_s[...] = hv.astype(MXU)
            h_ref[...] = h_s[...]

        y_ref[...] = _dot_nt(h_s[...], w_ref[...]).astype(MXU)

    return pl.pallas_call(
        body, name=name, grid=(nb, s // tm, n // tn),
        in_specs=[pl.BlockSpec((None, tm, d), lambda b, i, j: (b, i, 0)),
                  pl.BlockSpec((1, d), lambda b, i, j: (0, 0)),
                  pl.BlockSpec((None, 1, d), lambda b, i, j: (b, 0, 0)),
                  pl.BlockSpec((None, 1, d), lambda b, i, j: (b, 0, 0)),
                  pl.BlockSpec((tn, d), lambda b, i, j: (j, 0))],
        out_specs=[pl.BlockSpec((None, tm, tn), lambda b, i, j: (b, i, j)),
                   pl.BlockSpec((None, tm, d), lambda b, i, j: (b, i, 0))],
        out_shape=[jax.ShapeDtypeStruct((nb, s, n), MXU), jax.ShapeDtypeStruct((nb, s, d), MXU)],
        scratch_shapes=[pltpu.VMEM((tm, d), MXU)],
        compiler_params=_cp("parallel", "parallel", "arbitrary"))(x, g, scale, shift, w)


def _ln_mod_matmul_bwd(dy, w, x, g, scale, dres, conv_w, name, riding=()):
    nb, s, n = dy.shape
    d = x.shape[-1]
    tm, tn = min(512, s), 512
    ni, nj = s // tm, n // tn
    hb = tm // HALO
    conv = conv_w is not None
    na = len(riding)

    def body(*refs):
        if conv:
            dy_ref, nx_ref, cw_ref = refs[:3]
            refs = refs[3:]
        else:
            dy_ref = refs[0]
            refs = refs[1:]
        w_ref, x_ref, g_ref, sc_ref, dr_ref = refs[:5]
        ride_in, refs = refs[5:5 + na], refs[5 + na:]
        dx_ref, dyp_ref, dsh_ref, dsc_ref, dg_ref = refs[:5]
        ride_out, refs = refs[5:5 + na], refs[5 + na:]
        acc = refs[0]
        b, i, j = pl.program_id(0), pl.program_id(1), pl.program_id(2)
        if na:
            copies = _chip_exchange_copies(ride_in, ride_out, *refs[1:])

            @pl.when((b == 0) & (i == 0) & (j == 0))
            def _():
                for cp in copies:
                    cp.start()

        @pl.when(j == 0)
        def _():
            acc[...] = jnp.zeros_like(acc)

        @pl.when((j == 0) & (i == 0))
        def _():
            dsh_ref[...] = jnp.zeros_like(dsh_ref)
            dsc_ref[...] = jnp.zeros_like(dsc_ref)

        @pl.when((j == 0) & (i == 0) & (b == 0))
        def _():
            dg_ref[...] = jnp.zeros_like(dg_ref)

        dv = dy_ref[...].astype(F32)
        if conv:
            rows = _iota((tm, 1), 0)
            nx = jnp.where(i == ni - 1, 0.0, nx_ref[...].astype(F32))
            n1 = jnp.where(rows == tm - 1, nx[0:1, :], pltpu.roll(dv, tm - 1, 0))
            n2 = jnp.where(rows == tm - 2, nx[0:1, :], jnp.where(rows == tm - 1, nx[1:2, :], pltpu.roll(dv, tm - 2, 0)))
            cw = cw_ref[...]
            dv = cw[2:3, :] * dv + cw[1:2, :] * n1 + cw[0:1, :] * n2
        dp = dv.astype(MXU)
        dyp_ref[...] = dp
        acc[...] += _dot(dp, w_ref[...])

        @pl.when(j == nj - 1)
        def _():
            dh = acc[...]
            xf = x_ref[...]
            rstd = lax.rsqrt(jnp.mean(xf * xf, axis=-1, keepdims=True) + EPS)
            xn = xf * rstd
            gg = g_ref[...]
            sc1 = 1.0 + sc_ref[...]
            dsh_ref[...] += _csum(dh)
            dsc_ref[...] += _csum(dh * xn * gg)
            dg_ref[...] += _csum(dh * xn * sc1)
            dn = dh * gg * sc1
            dx_ref[...] = dr_ref[...] + rstd * (dn - xn * jnp.mean(dn * xn, axis=-1, keepdims=True))

        if na:
            @pl.when((b == nb - 1) & (i == ni - 1) & (j == nj - 1))
            def _():
                for cp in copies:
                    cp.wait()

    hbm = pl.BlockSpec(memory_space=pl.ANY)
    in_specs = [pl.BlockSpec((None, tm, tn), lambda b, i, j: (b, i, j))]
    args = [dy]
    if conv:
        in_specs += [pl.BlockSpec((None, HALO, tn), lambda b, i, j: (b, jnp.minimum((i + 1) * hb, s // HALO - 1), j)),
                     pl.BlockSpec((3, tn), lambda b, i, j: (0, j))]
        args += [dy, conv_w]
    in_specs += [pl.BlockSpec((tn, d), lambda b, i, j: (j, 0)),
                 pl.BlockSpec((None, tm, d), lambda b, i, j: (b, i, 0)),
                 pl.BlockSpec((1, d), lambda b, i, j: (0, 0)),
                 pl.BlockSpec((None, 1, d), lambda b, i, j: (b, 0, 0)),
                 pl.BlockSpec((None, tm, d), lambda b, i, j: (b, i, 0))]
    in_specs += [hbm] * na
    args += [w, x, g, scale, dres, *riding]
    return pl.pallas_call(
        body, name=name, grid=(nb, ni, nj), in_specs=in_specs,
        out_specs=[pl.BlockSpec((None, tm, d), lambda b, i, j: (b, i, 0)),
                   pl.BlockSpec((None, tm, tn), lambda b, i, j: (b, i, j)),
                   pl.BlockSpec((None, 1, d), lambda b, i, j: (b, 0, 0)),
                   pl.BlockSpec((None, 1, d), lambda b, i, j: (b, 0, 0)),
                   pl.BlockSpec((1, d), lambda b, i, j: (0, 0))] + [hbm] * na,
        out_shape=[jax.ShapeDtypeStruct((nb, s, d), F32), jax.ShapeDtypeStruct((nb, s, n), MXU),
                   jax.ShapeDtypeStruct((nb, 1, d), F32), jax.ShapeDtypeStruct((nb, 1, d), F32),
                   jax.ShapeDtypeStruct((1, d), F32)] + [jax.ShapeDtypeStruct(a.shape, a.dtype) for a in riding],
        scratch_shapes=[pltpu.VMEM((tm, d), F32)] + (_chip_exchange_sems(na) if na else []),
        compiler_params=_cp("arbitrary", "arbitrary", "arbitrary"))(*args)


def _wgrad(xm, dym, name):
    t, k = xm.shape
    n = dym.shape[1]
    tk = 1408 if k % 1408 == 0 else 1024
    tt = min(512, t)

    def body(x_ref, dy_ref, o_ref):
        @pl.when(pl.program_id(1) == 0)
        def _():
            o_ref[...] = jnp.zeros_like(o_ref)

        o_ref[...] += _dot_tn(x_ref[...], dy_ref[...])

    return pl.pallas_call(
        body, name=name, grid=(k // tk, t // tt),
        in_specs=[pl.BlockSpec((tt, tk), lambda a, c: (c, a)),
                  pl.BlockSpec((tt, n), lambda a, c: (c, 0))],
        out_specs=pl.BlockSpec((tk, n), lambda a, c: (a, 0)),
        out_shape=jax.ShapeDtypeStruct((k, n), F32),
        compiler_params=_cp("parallel", "arbitrary"))(xm, dym)


def _out_proj(parts, ws, gate, res, name):
    nb, s, d = res.shape
    tm = min(512, s)
    npart = len(parts)

    def body(*refs):
        p_refs, w_refs = refs[:npart], refs[npart:2 * npart]
        gt_ref, res_ref, xo_ref, y_ref = refs[2 * npart:]
        y = _dot(p_refs[0][...].astype(MXU), w_refs[0][...])
        for p_ref, w_ref in zip(p_refs[1:], w_refs[1:]):
            y = y + _dot(p_ref[...].astype(MXU), w_ref[...])
        y_ref[...] = y
        xo_ref[...] = res_ref[...] + gt_ref[...] * y

    in_specs = [pl.BlockSpec((None, tm, p.shape[-1]), lambda b, i: (b, i, 0)) for p in parts]
    in_specs += [pl.BlockSpec(w.shape, lambda b, i: (0, 0)) for w in ws]
    in_specs += [pl.BlockSpec((None, 1, d), lambda b, i: (b, 0, 0)),
                 pl.BlockSpec((None, tm, d), lambda b, i: (b, i, 0))]
    return pl.pallas_call(
        body, name=name, grid=(nb, s // tm), in_specs=in_specs,
        out_specs=[pl.BlockSpec((None, tm, d), lambda b, i: (b, i, 0))] * 2,
        out_shape=[jax.ShapeDtypeStruct((nb, s, d), F32)] * 2,
        compiler_params=_cp("parallel", "parallel"))(*parts, *ws, gate, res)


def _gate_bwd_nt(dx, y, gate, ws, name):
    nb, s, d = dx.shape
    tm = min(256, s)
    npart = len(ws)

    def body(*refs):
        dx_ref, y_ref, gt_ref = refs[:3]
        w_refs = refs[3:3 + npart]
        da_refs = refs[3 + npart:3 + 2 * npart]
        dy_ref, dgt_ref = refs[3 + 2 * npart:]

        @pl.when(pl.program_id(1) == 0)
        def _():
            dgt_ref[...] = jnp.zeros_like(dgt_ref)

        dxv = dx_ref[...]
        dyv = (dxv * gt_ref[...]).astype(MXU)
        dy_ref[...] = dyv
        dgt_ref[...] += _csum(dxv * y_ref[...])
        for w_ref, da_ref in zip(w_refs, da_refs):
            da_ref[...] = _dot_nt(dyv, w_ref[...])

    tile = pl.BlockSpec((None, tm, d), lambda b, i: (b, i, 0))
    row = pl.BlockSpec((None, 1, d), lambda b, i: (b, 0, 0))
    outs = pl.pallas_call(
        body, name=name, grid=(nb, s // tm),
        in_specs=[tile, tile, row] + [pl.BlockSpec(w.shape, lambda b, i: (0, 0)) for w in ws],
        out_specs=[pl.BlockSpec((None, tm, w.shape[0]), lambda b, i: (b, i, 0)) for w in ws] + [tile, row],
        out_shape=[jax.ShapeDtypeStruct((nb, s, w.shape[0]), F32) for w in ws]
        + [jax.ShapeDtypeStruct((nb, s, d), MXU), jax.ShapeDtypeStruct((nb, 1, d), F32)],
        compiler_params=_cp("arbitrary", "arbitrary"))(dx, y, gate, *ws)
    return outs[:npart], outs[npart], outs[npart + 1]


def _conv_shifts(xv, halo, rows):
    last, before = halo[HALO - 1:HALO, :], halo[HALO - 2:HALO - 1, :]
    p1 = jnp.where(rows == 0, last, pltpu.roll(xv, 1, 0))
    p2 = jnp.where(rows == 0, before, jnp.where(rows == 1, last, pltpu.roll(xv, 2, 0)))
    return p1, p2


def _conv_gate_matmul(u, cw, cb, wd, gate, res, name):
    nb, s, f2 = u.shape
    f = f2 // 2
    d = wd.shape[1]
    tm = min(512, s)
    tk = f // 2
    nk = f // tk
    hb = tm // HALO

    def body(ug_ref, uv_ref, hg_ref, hv_ref, cwg_ref, cwv_ref, cbg_ref, cbv_ref, wd_ref, gt_ref, res_ref,
             xo_ref, y_ref, acc):
        i, k = pl.program_id(1), pl.program_id(2)

        @pl.when(k == 0)
        def _():
            acc[...] = jnp.zeros_like(acc)

        rows = _iota((tm, 1), 0)

        def conv(x_ref, h_ref, w_ref, b_ref):
            xv = x_ref[...].astype(F32)
            halo = jnp.where(i == 0, 0.0, h_ref[...].astype(F32))
            p1, p2 = _conv_shifts(xv, halo, rows)
            wv = w_ref[...]
            return wv[2:3, :] * xv + wv[1:2, :] * p1 + wv[0:1, :] * p2 + b_ref[...]

        gv = conv(ug_ref, hg_ref, cwg_ref, cbg_ref)
        vv = conv(uv_ref, hv_ref, cwv_ref, cbv_ref)
        av = gv * _sigmoid(gv) * vv
        acc[...] += _dot(av.astype(MXU), wd_ref[...])

        @pl.when(k == nk - 1)
        def _():
            y = acc[...]
            y_ref[...] = y
            xo_ref[...] = res_ref[...] + gt_ref[...] * y

    def halo_idx(off):
        return lambda b, i, k: (b, jnp.maximum(i * hb - 1, 0), k + off)

    tile = pl.BlockSpec((None, tm, d), lambda b, i, k: (b, i, 0))
    return pl.pallas_call(
        body, name=name, grid=(nb, s // tm, nk),
        in_specs=[pl.BlockSpec((None, tm, tk), lambda b, i, k: (b, i, k)),
                  pl.BlockSpec((None, tm, tk), lambda b, i, k: (b, i, k + nk)),
                  pl.BlockSpec((None, HALO, tk), halo_idx(0)),
                  pl.BlockSpec((None, HALO, tk), halo_idx(nk)),
                  pl.BlockSpec((3, tk), lambda b, i, k: (0, k)),
                  pl.BlockSpec((3, tk), lambda b, i, k: (0, k + nk)),
                  pl.BlockSpec((1, tk), lambda b, i, k: (0, k)),
                  pl.BlockSpec((1, tk), lambda b, i, k: (0, k + nk)),
                  pl.BlockSpec((tk, d), lambda b, i, k: (k, 0)),
                  pl.BlockSpec((None, 1, d), lambda b, i, k: (b, 0, 0)),
                  tile],
        out_specs=[tile, tile],
        out_shape=[jax.ShapeDtypeStruct((nb, s, d), F32)] * 2,
        scratch_shapes=[pltpu.VMEM((tm, d), F32)],
        compiler_params=_cp("parallel", "parallel", "arbitrary"))(u, u, u, u, cw, cw, cb, cb, wd, gate, res)


def _conv_gate_bwd(da, u, cw, cb, name):
    nb, s, f2 = u.shape
    f = f2 // 2
    tm = min(128, s)
    hb = tm // HALO

    def body(da_ref, u_ref, h_ref, cw_ref, cb_ref, du_ref, a_ref, st_ref):
        b, i = pl.program_id(0), pl.program_id(1)

        @pl.when((b == 0) & (i == 0))
        def _():
            st_ref[...] = jnp.zeros_like(st_ref)

        rows = _iota((tm, 1), 0)
        first = i == 0

        def conv(cs):
            xv = u_ref[:, cs].astype(F32)
            halo = jnp.where(first, 0.0, h_ref[:, cs].astype(F32))
            p1, p2 = _conv_shifts(xv, halo, rows)
            wv = cw_ref[:, cs]
            return xv, p1, p2, wv[2:3, :] * xv + wv[1:2, :] * p1 + wv[0:1, :] * p2 + cb_ref[:, cs]

        def stats(cs, du, xv, p1, p2):
            du_ref[:, cs] = du.astype(MXU)
            st_ref[0:1, cs] += _csum(du)
            st_ref[1:2, cs] += _csum(du * p2)
            st_ref[2:3, cs] += _csum(du * p1)
            st_ref[3:4, cs] += _csum(du * xv)

        for k in range(f // LANES):
            cg = slice(k * LANES, (k + 1) * LANES)
            cv = slice(f + k * LANES, f + (k + 1) * LANES)
            xg, g1, g2, gv = conv(cg)
            xv, v1, v2, vv = conv(cv)
            sg = _sigmoid(gv)
            sl = gv * sg
            a_ref[:, cg] = (sl * vv).astype(MXU)
            dav = da_ref[:, cg]
            stats(cg, dav * vv * (sg * (1.0 + gv * (1.0 - sg))), xg, g1, g2)
            stats(cv, dav * sl, xv, v1, v2)

    return pl.pallas_call(
        body, name=name, grid=(nb, s // tm),
        in_specs=[pl.BlockSpec((None, tm, f), lambda b, i: (b, i, 0)),
                  pl.BlockSpec((None, tm, f2), lambda b, i: (b, i, 0)),
                  pl.BlockSpec((None, HALO, f2), lambda b, i: (b, jnp.maximum(i * hb - 1, 0), 0)),
                  pl.BlockSpec((3, f2), lambda b, i: (0, 0)),
                  pl.BlockSpec((1, f2), lambda b, i: (0, 0))],
        out_specs=[pl.BlockSpec((None, tm, f2), lambda b, i: (b, i, 0)),
                   pl.BlockSpec((None, tm, f), lambda b, i: (b, i, 0)),
                   pl.BlockSpec((8, f2), lambda b, i: (0, 0))],
        out_shape=[jax.ShapeDtypeStruct((nb, s, f2), MXU), jax.ShapeDtypeStruct((nb, s, f), MXU),
                   jax.ShapeDtypeStruct((8, f2), F32)],
        compiler_params=_cp("arbitrary", "arbitrary"))(da, u, u, cw, cb)


def _rot(xv, lane):
    return jnp.where((lane >= 64) & (lane < 80), -pltpu.roll(xv, 112, 1),
                     jnp.where((lane >= 80) & (lane < 96), pltpu.roll(xv, 16, 1), 0.0))


def _rot_t(dv, lane):
    return jnp.where((lane >= 80) & (lane < 96), -pltpu.roll(dv, 16, 1),
                     jnp.where((lane >= 64) & (lane < 80), pltpu.roll(dv, 112, 1), 0.0))


def _mla_prep_specs(s, tm):
    def blk(width, col):
        return pl.BlockSpec((None, tm, width), lambda b, i: (b, i, col // width))

    full = lambda shape: pl.BlockSpec(shape, lambda b, i: (0, 0))
    return [blk(256, COL_CQ), blk(128, COL_CKV), blk(128, COL_KR),
            pl.BlockSpec((None, tm, LANES), lambda b, i: (b, i, 0)),
            pl.BlockSpec((None, tm, LANES), lambda b, i: (b, i, 0)),
            full((1, 256)), full((1, 128)), full((1, 128)), full((1, 128)),
            full((768, 256)), full((768, 128))]


def _mla_prep(proj, cs, sn, gcq, gckv, gqn, gkn, wuq, wukv, name):
    nb, s, _ = proj.shape
    tm = min(256, s)

    def body(cq_ref, ckv_ref, kr_ref, c_ref, s_ref, gcq_ref, gckv_ref, gqn_ref, gkn_ref, wuq_ref, wukv_ref,
             q_ref, k_ref, v_ref):
        lane = _iota((tm, LANES), 1)
        cv, sv = c_ref[...], s_ref[...]
        cq = cq_ref[...].astype(F32)
        cqn = cq * lax.rsqrt(jnp.mean(cq * cq, axis=-1, keepdims=True) + EPS) * gcq_ref[...]
        qb = _dot_nt(cqn.astype(MXU), wuq_ref[...])
        ckv = ckv_ref[...].astype(F32)
        ckvn = ckv * lax.rsqrt(jnp.mean(ckv * ckv, axis=-1, keepdims=True) + EPS) * gckv_ref[...]
        kvb = _dot_nt(ckvn.astype(MXU), wukv_ref[...])
        kr = kr_ref[...].astype(F32)
        for h in range(MLA_HEADS):
            hs = slice(h * LANES, (h + 1) * LANES)
            qh = qb[:, hs]
            qn = qh * lax.rsqrt(_rsum(qh * qh) / MLA_QK + EPS) * gqn_ref[...]
            q_ref[:, hs] = (qn * cv + _rot(qn, lane) * sv).astype(MXU)
            kc = jnp.where(lane < HEAD, kvb[:, hs], kr)
            kn = kc * lax.rsqrt(_rsum(kc * kc) / MLA_QK + EPS) * gkn_ref[...]
            k_ref[:, hs] = (kn * cv + _rot(kn, lane) * sv).astype(MXU)
        for j in range(MLA_HEADS // 2):
            va = kvb[:, (2 * j) * LANES:(2 * j + 1) * LANES]
            vb = kvb[:, (2 * j + 1) * LANES:(2 * j + 2) * LANES]
            v_ref[:, j * LANES:(j + 1) * LANES] = jnp.where(lane < HEAD, pltpu.roll(va, HEAD, 1), vb).astype(MXU)

    return pl.pallas_call(
        body, name=name, grid=(nb, s // tm), in_specs=_mla_prep_specs(s, tm),
        out_specs=[pl.BlockSpec((None, tm, 768), lambda b, i: (b, i, 0)),
                   pl.BlockSpec((None, tm, 768), lambda b, i: (b, i, 0)),
                   pl.BlockSpec((None, tm, 384), lambda b, i: (b, i, 0))],
        out_shape=[jax.ShapeDtypeStruct((nb, s, 768), MXU), jax.ShapeDtypeStruct((nb, s, 768), MXU),
                   jax.ShapeDtypeStruct((nb, s, 384), MXU)],
        compiler_params=_cp("parallel", "parallel"))(proj, proj, proj, cs, sn, gcq, gckv, gqn, gkn, wuq, wukv)


def _mla_prep_bwd(proj, cs, sn, gcq, gckv, gqn, gkn, wuq, wukv, dq, dk, dv, name):
    nb, s, _ = proj.shape
    tm = min(256, s)

    def body(cq_ref, ckv_ref, kr_ref, c_ref, s_ref, gcq_ref, gckv_ref, gqn_ref, gkn_ref, wuq_ref, wukv_ref,
             dq_ref, dk_ref, dv_ref,
             dcq_ref, dckv_ref, dkr_ref, dwuq_ref, dwukv_ref, dgcq_ref, dgckv_ref, dgqn_ref, dgkn_ref,
             dqb_s, dkvb_s):
        @pl.when((pl.program_id(0) == 0) & (pl.program_id(1) == 0))
        def _():
            for r in (dwuq_ref, dwukv_ref, dgcq_ref, dgckv_ref, dgqn_ref, dgkn_ref):
                r[...] = jnp.zeros_like(r)

        lane = _iota((tm, LANES), 1)
        cv, sv = c_ref[...], s_ref[...]
        gqn, gkn = gqn_ref[...], gkn_ref[...]
        cq = cq_ref[...].astype(F32)
        rc = lax.rsqrt(jnp.mean(cq * cq, axis=-1, keepdims=True) + EPS)
        chat = cq * rc
        cqn = (chat * gcq_ref[...]).astype(MXU)
        qb = _dot_nt(cqn, wuq_ref[...])
        ckv = ckv_ref[...].astype(F32)
        rkv = lax.rsqrt(jnp.mean(ckv * ckv, axis=-1, keepdims=True) + EPS)
        kvhat = ckv * rkv
        ckvn = (kvhat * gckv_ref[...]).astype(MXU)
        kvb = _dot_nt(ckvn, wukv_ref[...])
        kr = kr_ref[...].astype(F32)
        dgq = jnp.zeros((1, LANES), F32)
        dgk = jnp.zeros((1, LANES), F32)
        dkr = jnp.zeros((tm, LANES), F32)
        for h in range(MLA_HEADS):
            hs = slice(h * LANES, (h + 1) * LANES)
            qh = qb[:, hs]
            rq = lax.rsqrt(_rsum(qh * qh) / MLA_QK + EPS)
            qhat = qh * rq
            dqr = dq_ref[:, hs]
            dqn = dqr * cv + _rot_t(dqr * sv, lane)
            dgq = dgq + _csum(dqn * qhat)
            dyq = dqn * gqn
            dqb_s[:, hs] = (rq * (dyq - qhat * (_rsum(dyq * qhat) / MLA_QK))).astype(MXU)

            kc = jnp.where(lane < HEAD, kvb[:, hs], kr)
            rk = lax.rsqrt(_rsum(kc * kc) / MLA_QK + EPS)
            khat = kc * rk
            dkr_h = dk_ref[:, hs]
            dkn = dkr_h * cv + _rot_t(dkr_h * sv, lane)
            dgk = dgk + _csum(dkn * khat)
            dyk = dkn * gkn
            dkc = rk * (dyk - khat * (_rsum(dyk * khat) / MLA_QK))
            dkr = dkr + jnp.where(lane >= HEAD, dkc, 0.0)
            dvb = dv_ref[:, (h // 2) * LANES:(h // 2 + 1) * LANES]
            dvp = dvb if h % 2 == 1 else pltpu.roll(dvb, HEAD, 1)
            dkvb_s[:, hs] = jnp.where(lane < HEAD, dkc, dvp).astype(MXU)
        dgqn_ref[...] += dgq
        dgkn_ref[...] += dgk
        dkr_ref[...] = dkr

        dqb = dqb_s[...]
        dwuq_ref[...] += _dot_tn(dqb, cqn)
        dcqn = _dot(dqb, wuq_ref[...])
        dgcq_ref[...] += _csum(dcqn * chat)
        dyc = dcqn * gcq_ref[...]
        dcq_ref[...] = rc * (dyc - chat * jnp.mean(dyc * chat, axis=-1, keepdims=True))

        dkvb = dkvb_s[...]
        dwukv_ref[...] += _dot_tn(dkvb, ckvn)
        dckvn = _dot(dkvb, wukv_ref[...])
        dgckv_ref[...] += _csum(dckvn * kvhat)
        dykv = dckvn * gckv_ref[...]
        dckv_ref[...] = rkv * (dykv - kvhat * jnp.mean(dykv * kvhat, axis=-1, keepdims=True))

    full = lambda shape: pl.BlockSpec(shape, lambda b, i: (0, 0))
    tile = lambda width: pl.BlockSpec((None, tm, width), lambda b, i: (b, i, 0))
    return pl.pallas_call(
        body, name=name, grid=(nb, s // tm),
        in_specs=_mla_prep_specs(s, tm) + [tile(768), tile(768), tile(384)],
        out_specs=[tile(256), tile(128), tile(128), full((768, 256)), full((768, 128)),
                   full((1, 256)), full((1, 128)), full((1, 128)), full((1, 128))],
        out_shape=[jax.ShapeDtypeStruct((nb, s, 256), F32), jax.ShapeDtypeStruct((nb, s, 128), F32),
                   jax.ShapeDtypeStruct((nb, s, 128), F32),
                   jax.ShapeDtypeStruct((768, 256), F32), jax.ShapeDtypeStruct((768, 128), F32),
                   jax.ShapeDtypeStruct((1, 256), F32), jax.ShapeDtypeStruct((1, 128), F32),
                   jax.ShapeDtypeStruct((1, 128), F32), jax.ShapeDtypeStruct((1, 128), F32)],
        scratch_shapes=[pltpu.VMEM((tm, 768), MXU), pltpu.VMEM((tm, 768), MXU)],
        compiler_params=_cp("arbitrary", "arbitrary"))(
            proj, proj, proj, cs, sn, gcq, gckv, gqn, gkn, wuq, wukv, dq, dk, dv)


def _softplus(z):
    return jnp.maximum(z, 0.0) + jnp.log(1.0 + jnp.exp(-jnp.abs(z)))


def _sb_fwd(proj, name, riding=()):
    nb, s, _ = proj.shape
    tq, tk = min(256, s), 128
    ratio = tq // tk
    na = len(riding)
    grid = (nb, 2, s // tq)

    def body(q_ref, k_ref, v_ref, o_ref, ct_ref, cnt_ref):
        i = pl.program_id(2)
        lo = _iota((tq, LANES), 1) < HEAD
        qv = q_ref[...]
        q0 = jnp.where(lo, qv, 0.0).astype(MXU)
        q1 = jnp.where(lo, 0.0, qv).astype(MXU)
        usuf = (_iota((tk, tk), 0) > _iota((tk, tk), 1)).astype(MXU)
        tpos = i * tq + _iota((tq, tk), 0)
        scol = _iota((tq, tk), 1)
        nch = (i + 1) * ratio

        def alive(st):
            return (st[0] < nch) & (st[5] > SB_DEAD)

        def step(st):
            t, c0, a0, c1, a1, _ = st
            j = nch - 1 - t
            off = pl.multiple_of(j * tk, tk)
            kc = k_ref[pl.ds(off, tk), :].astype(MXU)
            vc = v_ref[pl.ds(off, tk), :].astype(MXU)
            msk = (scol + j * tk) < tpos

            def head(qm, c, a):
                z = _dot_nt(qm, kc) * SB_SCALE
                sp = _softplus(z)
                lk = jnp.where(msk, -sp, 0.0)
                w = jnp.where(msk, jnp.exp(z - sp + _cumdot(lk, usuf) + c), 0.0)
                return c + _rsum(lk), a + _dot(w.astype(MXU), vc)

            c0, a0 = head(q0, c0, a0)
            c1, a1 = head(q1, c1, a1)
            return t + 1, c0, a0, c1, a1, jnp.maximum(jnp.max(c0), jnp.max(c1))

        z1 = jnp.zeros((tq, 1), F32)
        za = jnp.zeros((tq, LANES), F32)
        t, c0, a0, c1, a1, _ = lax.while_loop(alive, step, (jnp.int32(0), z1, za, z1, za, jnp.float32(0.0)))
        o_ref[...] = jnp.where(lo, a0, a1)
        ct_ref[...] = jnp.where(lo, c0, c1)
        cnt_ref[...] = jnp.zeros((8, LANES), F32) + t.astype(F32)

    kv = lambda col: pl.BlockSpec((None, s, LANES), lambda b, p, i: (b, 0, col // LANES + p))
    tile = pl.BlockSpec((None, tq, LANES), lambda b, p, i: (b, i, p))
    hbm = pl.BlockSpec(memory_space=pl.ANY)
    return pl.pallas_call(
        _with_gather(body, 3, 3, na, grid), name=name, grid=grid,
        in_specs=[pl.BlockSpec((None, tq, LANES), lambda b, p, i: (b, i, COL_SBQ // LANES + p)),
                  kv(COL_SBK), kv(COL_SBV)] + [hbm] * na,
        out_specs=[tile, tile, pl.BlockSpec((None, None, None, 8, LANES), lambda b, p, i: (b, p, i, 0, 0))] + [hbm] * na,
        out_shape=[jax.ShapeDtypeStruct((nb, s, 256), F32)] * 2
        + [jax.ShapeDtypeStruct((nb, 2, s // tq, 8, LANES), F32)] + _gather_out_shapes(riding),
        scratch_shapes=_gather_sems(na) if na else [],
        compiler_params=_cp("arbitrary", "arbitrary", "arbitrary"))(proj, proj, proj, *riding)


def _sb_bwd(proj, ct, cnt, do, name):
    nb, s, _ = proj.shape
    tq, tk = min(256, s), 128
    ratio = tq // tk

    def body(q_ref, k_ref, v_ref, ct_ref, cnt_ref, do_ref, dq_ref, dk_ref, dv_ref):
        i = pl.program_id(2)

        @pl.when(i == 0)
        def _():
            dk_ref[...] = jnp.zeros_like(dk_ref)
            dv_ref[...] = jnp.zeros_like(dv_ref)

        lane = _iota((tq, LANES), 1)
        lo = lane < HEAD
        lok = _iota((tk, LANES), 1) < HEAD
        qv, dov = q_ref[...], do_ref[...]
        qb, dob = qv.astype(MXU), dov.astype(MXU)
        q0 = jnp.where(lo, qv, 0.0).astype(MXU)
        q1 = jnp.where(lo, 0.0, qv).astype(MXU)
        do0 = jnp.where(lo, dov, 0.0).astype(MXU)
        do1 = jnp.where(lo, 0.0, dov).astype(MXU)
        ctv = ct_ref[...]
        ct0 = _rsum(jnp.where(lane == 0, ctv, 0.0))
        ct1 = _rsum(jnp.where(lane == LANES - 1, ctv, 0.0))
        uincl = (_iota((tk, tk), 0) <= _iota((tk, tk), 1)).astype(MXU)
        ustrict = (_iota((tk, tk), 0) < _iota((tk, tk), 1)).astype(MXU)
        tpos = i * tq + _iota((tq, tk), 0)
        scol = _iota((tq, tk), 1)
        nch = (i + 1) * ratio

        def step(j, carry):
            p0, g0, dq0, p1, g1, dq1 = carry
            off = pl.multiple_of(j * tk, tk)
            kc = k_ref[pl.ds(off, tk), :].astype(MXU)
            vc = v_ref[pl.ds(off, tk), :].astype(MXU)
            msk = (scol + j * tk) < tpos

            def head(qm, dom, ctot, pc, gc, dqa):
                z = _dot_nt(qm, kc) * SB_SCALE
                sp = _softplus(z)
                lk = jnp.where(msk, -sp, 0.0)
                lsig = z - sp
                w = jnp.where(msk, jnp.exp(lsig + (ctot - pc - _cumdot(lk, uincl))), 0.0)
                g = w * _dot_nt(dom, vc)
                gpre = gc + _cumdot(g, ustrict)
                sig = jnp.exp(lsig)
                dz = (jnp.where(msk, g * (1.0 - sig) - sig * gpre, 0.0) * SB_SCALE).astype(MXU)
                return (pc + _rsum(lk), gc + _rsum(g), dqa + _dot(dz, kc),
                        _dot_tn(dz, qb), _dot_tn(w.astype(MXU), dob))

            p0, g0, dq0, dk0, dv0 = head(q0, do0, ct0, p0, g0, dq0)
            p1, g1, dq1, dk1, dv1 = head(q1, do1, ct1, p1, g1, dq1)
            dk_ref[pl.ds(off, tk), :] += jnp.where(lok, dk0, dk1)
            dv_ref[pl.ds(off, tk), :] += jnp.where(lok, dv0, dv1)
            return p0, g0, dq0, p1, g1, dq1

        z1 = jnp.zeros((tq, 1), F32)
        za = jnp.zeros((tq, LANES), F32)
        first = nch - jnp.max(cnt_ref[...]).astype(jnp.int32)
        _, _, dq0, _, _, dq1 = lax.fori_loop(first, nch, step, (z1, z1, za, z1, z1, za))
        dq_ref[...] = jnp.where(lo, dq0, dq1)

    kv = lambda col: pl.BlockSpec((None, s, LANES), lambda b, p, i: (b, 0, col // LANES + p))
    tile = pl.BlockSpec((None, tq, LANES), lambda b, p, i: (b, i, p))
    acc = pl.BlockSpec((None, s, LANES), lambda b, p, i: (b, 0, p))
    return pl.pallas_call(
        body, name=name, grid=(nb, 2, s // tq),
        in_specs=[pl.BlockSpec((None, tq, LANES), lambda b, p, i: (b, i, COL_SBQ // LANES + p)),
                  kv(COL_SBK), kv(COL_SBV), tile,
                  pl.BlockSpec((None, None, None, 8, LANES), lambda b, p, i: (b, p, i, 0, 0)), tile],
        out_specs=[tile, acc, acc],
        out_shape=[jax.ShapeDtypeStruct((nb, s, 256), F32)] * 3,
        compiler_params=_cp("parallel", "parallel", "arbitrary"))(proj, proj, proj, ct, cnt, do)


def _mla_fwd(q, k, v, name, riding=()):
    nb, s, _ = q.shape
    tq = tk = min(256, s)
    na = len(riding)
    grid = (nb, MLA_HEADS // 2, s // tq)

    def body(q_ref, k_ref, v_ref, o_ref, lse_ref):
        i = pl.program_id(2)
        q0, q1 = q_ref[:, :LANES], q_ref[:, LANES:]
        krow = _iota((tk, tq), 0)
        qcol = _iota((tk, tq), 1)

        def step(j, carry, diagonal):
            m0, l0, a0, m1, l1, a1 = carry
            off = pl.multiple_of(j * tk, tk)
            vc = v_ref[pl.ds(off, tk), :]

            def head(qh, kh, m, l, a):
                st = _dot_nt(kh, qh) * MLA_SCALE
                if diagonal:
                    st = jnp.where(krow <= qcol, st, NEG)
                mn = jnp.maximum(m, jnp.max(st, axis=0, keepdims=True))
                al = jnp.exp(m - mn)
                pt = jnp.exp(st - mn)
                return mn, al * l + _csum(pt), al * a + _dot_tn(vc, pt.astype(MXU))

            m0, l0, a0 = head(q0, k_ref[pl.ds(off, tk), :LANES], m0, l0, a0)
            m1, l1, a1 = head(q1, k_ref[pl.ds(off, tk), LANES:], m1, l1, a1)
            return m0, l0, a0, m1, l1, a1

        mi = jnp.full((1, tq), NEG, F32)
        z1 = jnp.zeros((1, tq), F32)
        za = jnp.zeros((LANES, tq), F32)
        carry = lax.fori_loop(0, i, lambda j, cr: step(j, cr, False), (mi, z1, za, mi, z1, za))
        m0, l0, a0, m1, l1, a1 = step(i, carry, True)
        lo_rows = _iota((LANES, tq), 0) < HEAD
        o_ref[...] = jnp.where(lo_rows, a0 / l0, a1 / l1).T
        lse_ref[...] = jnp.zeros_like(lse_ref)
        lse_ref[0:1, :] = m0 + jnp.log(l0)
        lse_ref[1:2, :] = m1 + jnp.log(l1)

    tile = pl.BlockSpec((None, tq, LANES), lambda b, p, i: (b, i, p))
    hbm = pl.BlockSpec(memory_space=pl.ANY)
    return pl.pallas_call(
        _with_gather(body, 3, 2, na, grid), name=name, grid=grid,
        in_specs=[pl.BlockSpec((None, tq, 2 * LANES), lambda b, p, i: (b, i, p)),
                  pl.BlockSpec((None, s, 2 * LANES), lambda b, p, i: (b, 0, p)),
                  pl.BlockSpec((None, s, LANES), lambda b, p, i: (b, 0, p))] + [hbm] * na,
        out_specs=[tile, pl.BlockSpec((None, None, 8, tq), lambda b, p, i: (b, p, 0, i))] + [hbm] * na,
        out_shape=[jax.ShapeDtypeStruct((nb, s, 384), F32), jax.ShapeDtypeStruct((nb, MLA_HEADS // 2, 8, s), F32)]
        + _gather_out_shapes(riding),
        scratch_shapes=_gather_sems(na) if na else [],
        compiler_params=_cp("arbitrary", "arbitrary", "arbitrary"))(q, k, v, *riding)


def _mla_bwd(q, k, v, o, lse, do, name, riding=()):
    nb, s, _ = q.shape
    tq = tk = min(256, s)
    na = len(riding)
    grid = (nb, MLA_HEADS // 2, s // tq)

    def body(q_ref, k_ref, v_ref, o_ref, lse_ref, do_ref, dq_ref, dk_ref, dv_ref):
        i = pl.program_id(2)

        @pl.when(i == 0)
        def _():
            dk_ref[...] = jnp.zeros_like(dk_ref)
            dv_ref[...] = jnp.zeros_like(dv_ref)

        lo = _iota((tq, LANES), 1) < HEAD
        lok = _iota((tk, LANES), 1) < HEAD
        q0, q1 = q_ref[:, :LANES], q_ref[:, LANES:]
        dov = do_ref[...]
        dob = dov.astype(MXU)
        do0 = jnp.where(lo, dov, 0.0).astype(MXU)
        do1 = jnp.where(lo, 0.0, dov).astype(MXU)
        dd = dov * o_ref[...]
        hi = dd.astype(MXU)
        r1 = dd - hi.astype(F32)
        mid = r1.astype(MXU)
        low = (r1 - mid.astype(F32)).astype(MXU)
        sel_lane = _iota((8, LANES), 1) < HEAD
        sel0 = sel_lane.astype(MXU)
        sel1 = (~sel_lane).astype(MXU)
        dl0 = (_dot_nt(sel0, hi) + _dot_nt(sel0, mid) + _dot_nt(sel0, low))[0:1, :]
        dl1 = (_dot_nt(sel1, hi) + _dot_nt(sel1, mid) + _dot_nt(sel1, low))[0:1, :]
        ls0, ls1 = lse_ref[0:1, :], lse_ref[1:2, :]
        krow = _iota((tk, tq), 0)
        qcol = _iota((tk, tq), 1)

        def step(j, carry, diagonal):
            dq0, dq1 = carry
            off = pl.multiple_of(j * tk, tk)
            vc = v_ref[pl.ds(off, tk), :]

            def head(qh, kh, dom, ls, dl, dqa):
                st = _dot_nt(kh, qh) * MLA_SCALE
                if diagonal:
                    st = jnp.where(krow <= qcol, st, NEG)
                pt = jnp.exp(st - ls)
                dst = (pt * (_dot_nt(vc, dom) - dl) * MLA_SCALE).astype(MXU)
                return dqa + _dot_tn(kh, dst), _dot(dst, qh), _dot(pt.astype(MXU), dob)

            dq0, dk0, dv0 = head(q0, k_ref[pl.ds(off, tk), :LANES], do0, ls0, dl0, dq0)
            dq1, dk1, dv1 = head(q1, k_ref[pl.ds(off, tk), LANES:], do1, ls1, dl1, dq1)
            dk_ref[pl.ds(off, tk), :LANES] += dk0
            dk_ref[pl.ds(off, tk), LANES:] += dk1
            dv_ref[pl.ds(off, tk), :] += jnp.where(lok, dv0, dv1)
            return dq0, dq1

        za = jnp.zeros((LANES, tq), F32)
        carry = lax.fori_loop(0, i, lambda j, cr: step(j, cr, False), (za, za))
        dq0, dq1 = step(i, carry, True)
        dq_ref[:, :LANES] = dq0.T
        dq_ref[:, LANES:] = dq1.T

    tile = pl.BlockSpec((None, tq, LANES), lambda b, p, i: (b, i, p))
    tile2 = pl.BlockSpec((None, tq, 2 * LANES), lambda b, p, i: (b, i, p))
    hbm = pl.BlockSpec(memory_space=pl.ANY)
    return pl.pallas_call(
        _with_chip_exchange(body, 6, 3, na, grid), name=name, grid=grid,
        in_specs=[tile2,
                  pl.BlockSpec((None, s, 2 * LANES), lambda b, p, i: (b, 0, p)),
                  pl.BlockSpec((None, s, LANES), lambda b, p, i: (b, 0, p)),
                  tile, pl.BlockSpec((None, None, 8, tq), lambda b, p, i: (b, p, 0, i)), tile] + [hbm] * na,
        out_specs=[tile2,
                   pl.BlockSpec((None, s, 2 * LANES), lambda b, p, i: (b, 0, p)),
                   pl.BlockSpec((None, s, LANES), lambda b, p, i: (b, 0, p))] + [hbm] * na,
        out_shape=[jax.ShapeDtypeStruct((nb, s, 768), F32), jax.ShapeDtypeStruct((nb, s, 768), F32),
                   jax.ShapeDtypeStruct((nb, s, 384), F32)] + [jax.ShapeDtypeStruct(a.shape, a.dtype) for a in riding],
        scratch_shapes=_chip_exchange_sems(na) if na else [],
        compiler_params=_cp("arbitrary", "arbitrary", "arbitrary"))(q, k, v, o, lse, do, *riding)


def _half_stats(xv, lo):
    x2 = xv * xv
    s0 = _rsum(jnp.where(lo, x2, 0.0))
    s1 = _rsum(jnp.where(lo, 0.0, x2))
    return jnp.where(lo, lax.rsqrt(s0 / HEAD + EPS), lax.rsqrt(s1 / HEAD + EPS))


def _half_mean(xv, lo):
    s0 = _rsum(jnp.where(lo, xv, 0.0))
    s1 = _rsum(jnp.where(lo, 0.0, xv))
    return jnp.where(lo, s0, s1) / HEAD


def _swa_in_specs():
    def band(col, prev):
        if prev:
            return pl.BlockSpec((None, BLOCK, LANES), lambda b, n: (b, jnp.maximum(n - 1, 0), col // LANES))
        return pl.BlockSpec((None, BLOCK, LANES), lambda b, n: (b, n, col // LANES))

    full = lambda shape: pl.BlockSpec(shape, lambda b, n: tuple(0 for _ in shape))
    return [pl.BlockSpec((None, BLOCK, 384), lambda b, n: (b, n, COL_SWQ // 384)),
            band(COL_SWK, False), band(COL_SWK, True), band(COL_SWV, False), band(COL_SWV, True),
            full((1, LANES)), full((1, LANES)), full((8, LANES)), full((SW_HEADS, BLOCK, 2 * BLOCK))]


def _swa_valid(n):
    a = _iota((BLOCK, 2 * BLOCK), 0)
    bcol = _iota((BLOCK, 2 * BLOCK), 1)
    dist = BLOCK + a - bcol
    return (dist >= 0) & (dist < BLOCK) & ((n > 0) | (bcol >= BLOCK))


def _swa_fwd(proj, gq, gk, sinks, bias, name):
    nb, s, _ = proj.shape

    def body(q_ref, kc_ref, kp_ref, vc_ref, vp_ref, gq_ref, gk_ref, sk_ref, bias_ref, o_ref):
        n = pl.program_id(1)
        lo = _iota((BLOCK, LANES), 1) < HEAD
        lo2 = _iota((2 * BLOCK, LANES), 1) < HEAD
        kband = jnp.concatenate([kp_ref[...], kc_ref[...]], axis=0).astype(F32)
        vband = jnp.concatenate([vp_ref[...], vc_ref[...]], axis=0).astype(F32)
        kn = kband * _half_stats(kband, lo2) * gk_ref[...]
        ks = (kn.astype(MXU), pltpu.roll(kn, HEAD, 1).astype(MXU))
        vs = (vband.astype(MXU), pltpu.roll(vband, HEAD, 1).astype(MXU))
        valid = _swa_valid(n)
        for blk in range(SW_HEADS // 2):
            qv = q_ref[:, blk * LANES:(blk + 1) * LANES].astype(F32)
            qn = qv * _half_stats(qv, lo) * gq_ref[...]
            outs = []
            for half in range(2):
                h = 2 * blk + half
                swap = 0 if half == h // 3 else 1
                qm = jnp.where(lo if half == 0 else ~lo, qn, 0.0).astype(MXU)
                sc = jnp.where(valid, _dot_nt(qm, ks[swap]) * SW_SCALE + bias_ref[h], NEG)
                sk = jnp.max(sk_ref[h:h + 1, :], axis=-1, keepdims=True)
                m = jnp.maximum(jnp.max(sc, axis=-1, keepdims=True), sk)
                p = jnp.exp(sc - m)
                l = _rsum(p) + jnp.exp(sk - m)
                outs.append(_dot((p / l).astype(MXU), vs[swap]))
            o_ref[:, blk * LANES:(blk + 1) * LANES] = jnp.where(lo, outs[0], outs[1])

    return pl.pallas_call(
        body, name=name, grid=(nb, s // BLOCK), in_specs=_swa_in_specs(),
        out_specs=pl.BlockSpec((None, BLOCK, 384), lambda b, n: (b, n, 0)),
        out_shape=jax.ShapeDtypeStruct((nb, s, 384), F32),
        compiler_params=_cp("parallel", "parallel"))(proj, proj, proj, proj, proj, gq, gk, sinks, bias)


def _swa_bwd(proj, gq, gk, sinks, bias, do, name):
    nb, s, _ = proj.shape

    def body(q_ref, kc_ref, kp_ref, vc_ref, vp_ref, gq_ref, gk_ref, sk_ref, bias_ref, do_ref,
             dq_ref, dkc_ref, dkp_ref, dvc_ref, dvp_ref, dbias_ref, dsk_ref, dgq_ref, dgk_ref):
        n = pl.program_id(1)

        @pl.when((pl.program_id(0) == 0) & (n == 0))
        def _():
            for r in (dbias_ref, dsk_ref, dgq_ref, dgk_ref):
                r[...] = jnp.zeros_like(r)

        lo = _iota((BLOCK, LANES), 1) < HEAD
        lo2 = _iota((2 * BLOCK, LANES), 1) < HEAD
        kband = jnp.concatenate([kp_ref[...], kc_ref[...]], axis=0).astype(F32)
        vband = jnp.concatenate([vp_ref[...], vc_ref[...]], axis=0).astype(F32)
        rk = _half_stats(kband, lo2)
        khat = kband * rk
        gkv = gk_ref[...]
        kn = khat * gkv
        ks = (kn.astype(MXU), pltpu.roll(kn, HEAD, 1).astype(MXU))
        vs = (vband.astype(MXU), pltpu.roll(vband, HEAD, 1).astype(MXU))
        valid = _swa_valid(n)
        dkn = jnp.zeros((2 * BLOCK, LANES), F32)
        dvb = jnp.zeros((2 * BLOCK, LANES), F32)
        gqv = gq_ref[...]
        dgq = jnp.zeros((1, LANES), F32)
        for blk in range(SW_HEADS // 2):
            bs = slice(blk * LANES, (blk + 1) * LANES)
            qv = q_ref[:, bs].astype(F32)
            rq = _half_stats(qv, lo)
            qhat = qv * rq
            qn = qhat * gqv
            dov = do_ref[:, bs]
            dqn = jnp.zeros((BLOCK, LANES), F32)
            for half in range(2):
                h = 2 * blk + half
                swap = 0 if half == h // 3 else 1
                hm = lo if half == 0 else ~lo
                qm = jnp.where(hm, qn, 0.0).astype(MXU)
                dom = jnp.where(hm, dov, 0.0).astype(MXU)
                sc = jnp.where(valid, _dot_nt(qm, ks[swap]) * SW_SCALE + bias_ref[h], NEG)
                sk = jnp.max(sk_ref[h:h + 1, :], axis=-1, keepdims=True)
                m = jnp.maximum(jnp.max(sc, axis=-1, keepdims=True), sk)
                e = jnp.exp(sc - m)
                es = jnp.exp(sk - m)
                l = _rsum(e) + es
                p = e / l
                dp = _dot_nt(dom, vs[swap])
                delta = _rsum(p * dp)
                ds = p * (dp - delta)
                dsk_ref[h:h + 1, :] += jnp.broadcast_to(_csum(-(es / l) * delta), (1, LANES))
                dbias_ref[h] += ds
                dsb = (ds * SW_SCALE).astype(MXU)
                dqn = dqn + jnp.where(hm, _dot(dsb, ks[swap]), 0.0)
                rk_ = _dot_tn(dsb, qm)
                rv_ = _dot_tn(p.astype(MXU), dom)
                if swap:
                    rk_ = pltpu.roll(rk_, HEAD, 1)
                    rv_ = pltpu.roll(rv_, HEAD, 1)
                dkn = dkn + rk_
                dvb = dvb + rv_
            dgq = dgq + _csum(dqn * qhat)
            dyq = dqn * gqv
            dq_ref[:, bs] = rq * (dyq - qhat * _half_mean(dyq * qhat, lo))
        dgq_ref[...] += dgq
        dgk_ref[...] += _csum(dkn * khat)
        dyk = dkn * gkv
        dkb = rk * (dyk - khat * _half_mean(dyk * khat, lo2))
        dkp_ref[...] = dkb[:BLOCK]
        dkc_ref[...] = dkb[BLOCK:]
        dvp_ref[...] = dvb[:BLOCK]
        dvc_ref[...] = dvb[BLOCK:]

    full = lambda shape: pl.BlockSpec(shape, lambda b, n: tuple(0 for _ in shape))
    tile = pl.BlockSpec((None, BLOCK, LANES), lambda b, n: (b, n, 0))
    tile3 = pl.BlockSpec((None, BLOCK, 384), lambda b, n: (b, n, 0))
    kvs = jax.ShapeDtypeStruct((nb, s, LANES), F32)
    return pl.pallas_call(
        body, name=name, grid=(nb, s // BLOCK), in_specs=_swa_in_specs() + [tile3],
        out_specs=[tile3, tile, tile, tile, tile, full((SW_HEADS, BLOCK, 2 * BLOCK)), full((8, LANES)),
                   full((1, LANES)), full((1, LANES))],
        out_shape=[jax.ShapeDtypeStruct((nb, s, 384), F32), kvs, kvs, kvs, kvs,
                   jax.ShapeDtypeStruct((SW_HEADS, BLOCK, 2 * BLOCK), F32), jax.ShapeDtypeStruct((8, LANES), F32),
                   jax.ShapeDtypeStruct((1, LANES), F32), jax.ShapeDtypeStruct((1, LANES), F32)],
        compiler_params=_cp("arbitrary", "arbitrary"))(proj, proj, proj, proj, proj, gq, gk, sinks, bias, do)


def _bias_build(table, bucket, name):
    def body(tb_ref, bk_ref, o_ref):
        bk = bk_ref[...]
        tb = tb_ref[...]
        row = _iota((8, LANES), 0)
        col = _iota((8, LANES), 1)
        for h in range(SW_HEADS):
            acc = jnp.zeros((BLOCK, 2 * BLOCK), F32)
            for t in range(REL_BUCKETS):
                val = jnp.sum(jnp.where((row == h) & (col == t), tb, 0.0), keepdims=True)
                acc = jnp.where(bk == t, val, acc)
            o_ref[h] = acc

    return pl.pallas_call(
        body, name=name, out_shape=jax.ShapeDtypeStruct((SW_HEADS, BLOCK, 2 * BLOCK), F32))(table, bucket)


def _bias_grad(dbias, bucket, name):
    def body(db_ref, bk_ref, o_ref):
        bk = bk_ref[...]
        row = _iota((8, LANES), 0)
        col = _iota((8, LANES), 1)
        res = jnp.zeros((8, LANES), F32)
        for h in range(SW_HEADS):
            dbh = db_ref[h]
            for t in range(REL_BUCKETS):
                val = jnp.sum(jnp.where(bk == t, dbh, 0.0), keepdims=True)
                res = jnp.where((row == h) & (col == t), val, res)
        o_ref[...] = res

    return pl.pallas_call(body, name=name, out_shape=jax.ShapeDtypeStruct((8, LANES), F32))(dbias, bucket)


def _loss_grad(y, target, name):
    nb, s, d = y.shape
    tm = min(512, s)

    def body(y_ref, t_ref, loss_ref, dy_ref):
        @pl.when((pl.program_id(0) == 0) & (pl.program_id(1) == 0))
        def _():
            loss_ref[...] = jnp.zeros_like(loss_ref)

        e = y_ref[...] - t_ref[...]
        dy_ref[...] = e / d
        loss_ref[...] += 0.5 * jnp.sum(_rsum(e * e) / d, keepdims=True)

    tile = pl.BlockSpec((None, tm, d), lambda b, i: (b, i, 0))
    return pl.pallas_call(
        body, name=name, grid=(nb, s // tm), in_specs=[tile, tile],
        out_specs=[pl.BlockSpec((8, LANES), lambda b, i: (0, 0)), tile],
        out_shape=[jax.ShapeDtypeStruct((8, LANES), F32), jax.ShapeDtypeStruct((nb, s, d), F32)],
        compiler_params=_cp("arbitrary", "arbitrary"))(y, target)


def _adamw(parts, w, m, v, name):
    npart, r, ncol = parts.shape
    tr = _row_tile(r, ncol)
    bc1 = 1.0 - ADAM_B1 ** ADAM_STEP
    bc2 = 1.0 - ADAM_B2 ** ADAM_STEP

    def body(p_ref, w_ref, m_ref, v_ref, g_ref, d_ref, nm_ref, nv_ref):
        g = p_ref[0].astype(F32)
        for k in range(1, npart):
            g = g + p_ref[k].astype(F32)
        mn = ADAM_B1 * m_ref[...] + (1.0 - ADAM_B1) * g
        vn = ADAM_B2 * v_ref[...] + (1.0 - ADAM_B2) * (g * g)
        g_ref[...] = g
        nm_ref[...] = mn
        nv_ref[...] = vn
        d_ref[...] = -ADAM_LR * ((mn / bc1) / (jnp.sqrt(vn / bc2) + ADAM_EPS) + ADAM_WD * w_ref[...])

    tile = pl.BlockSpec((tr, ncol), lambda i: (i, 0))
    return pl.pallas_call(
        body, name=name, grid=(r // tr,),
        in_specs=[pl.BlockSpec((npart, tr, ncol), lambda i: (0, i, 0)), tile, tile, tile],
        out_specs=[tile] * 4, out_shape=[jax.ShapeDtypeStruct((r, ncol), F32)] * 4,
        compiler_params=_cp("parallel"))(parts, w, m, v)


def _unpack(flat, shapes, lead=()):
    out, off = [], 0
    for shp in shapes:
        size = 1
        for dim in shp:
            size *= dim
        out.append(flat[..., off:off + size].reshape(lead + tuple(shp)))
        off += size
    return out


def _t5_bucket():
    a = jnp.arange(BLOCK)[:, None]
    b = jnp.arange(2 * BLOCK)[None, :]
    dist = BLOCK + a - b
    max_exact = REL_BUCKETS // 2
    nn = jnp.maximum(dist, 0)
    nf = jnp.maximum(nn, 1).astype(F32)
    large = max_exact + (jnp.log(nf / max_exact) / math.log(BLOCK / max_exact)
                         * (REL_BUCKETS - max_exact)).astype(jnp.int32)
    large = jnp.minimum(large, REL_BUCKETS - 1)
    return jnp.where(nn < max_exact, nn, large).astype(jnp.int32)


def _pad_lanes(g, n):
    return jnp.pad(g, (0, n - g.shape[0])).reshape(1, n)


def kernel(x, c, positions, rel_table, norm1_g, norm2_g, w_ada, b_ada, w_in, mla_cq_g, w_uq, mla_ckv_g, w_ukv, mla_qn_g, mla_kn_g, sw_qn_g, sw_kn_g, sw_sinks, w_out, w_up, conv_w, conv_b, w_down, loss_target, m_rel_table, m_norm1_g, m_norm2_g, m_w_ada, m_b_ada, m_w_in, m_mla_cq_g, m_w_uq, m_mla_ckv_g, m_w_ukv, m_mla_qn_g, m_mla_kn_g, m_sw_qn_g, m_sw_kn_g, m_sw_sinks, m_w_out, m_w_up, m_conv_w, m_conv_b, m_w_down, v_rel_table, v_norm1_g, v_norm2_g, v_w_ada, v_b_ada, v_w_in, v_mla_cq_g, v_w_uq, v_mla_ckv_g, v_w_ukv, v_mla_qn_g, v_mla_kn_g, v_sw_qn_g, v_sw_kn_g, v_sw_sinks, v_w_out, v_w_up, v_conv_w, v_conv_b, v_w_down):
    nb, s, d = x.shape
    nl = norm1_g.shape[0]
    me = 4 * lax.axis_index("x") + 2 * lax.axis_index("y") + lax.axis_index("c")
    n_ada = w_ada.shape[2]

    shard = lambda w, l, transposed: (jnp.swapaxes(w[l], 0, 1) if transposed else w[l]).astype(MXU)
    attn_local = lambda l: [shard(w_in, l, True), shard(w_uq, l, True), shard(w_ukv, l, True), shard(w_out, l, False)]
    ffn_local = lambda l: [shard(w_up, l, True), shard(w_down, l, False)]
    full = lambda a: a.reshape(-1, a.shape[-1])
    zrows = lambda n: jnp.zeros((n, d), MXU)
    pad_in = lambda wt: jnp.concatenate([wt[:1152], wt[1184:1824], zrows(64), wt[1152:1184], zrows(160)], axis=0)
    pad_uq = lambda wt: jnp.pad(wt.reshape(MLA_HEADS, MLA_QK, 256), ((0, 0), (0, LANES - MLA_QK), (0, 0))).reshape(768, 256)
    got = _all_gather(attn_local(0) + [conv_w.reshape(-1, conv_w.shape[-1]), c], "gather_inputs")
    w_in_pt, w_uq_pt, w_ukv_t, w_out_f = [pad_in(full(got[0]))], [pad_uq(full(got[1]))], [full(got[2])], [full(got[3])]
    w_up_t, w_down_f = [], []
    conv_full = got[4].reshape(N_DEV, nl, 3, -1).transpose(1, 2, 0, 3).reshape(nl, 3, -1)
    c_all = got[5].reshape(N_DEV * nb, d)

    b_my = lax.dynamic_slice_in_dim(b_ada, me * n_ada, n_ada, axis=1).reshape(nl, 1, n_ada)
    mods_my = _ada_fwd(c_all, w_ada, b_my, "ada_fwd")
    mods, = _all_gather([mods_my.reshape(nl * N_DEV * nb, n_ada)], "gather_mods")
    mods = mods.reshape(N_DEV, nl, N_DEV * nb, n_ada).transpose(1, 2, 0, 3).reshape(nl, N_DEV * nb, N_DEV * n_ada)
    mods = lax.dynamic_slice_in_dim(mods, me * nb, nb, axis=1)
    shift1, scale1, gate1, shift2, scale2, gate2 = [mods[:, :, k * d:(k + 1) * d].reshape(nl, nb, 1, d) for k in range(6)]

    half = 16
    inv_freq = jnp.power(ROPE_THETA, -jnp.arange(half, dtype=F32) / half)
    ang = positions.astype(F32)[..., None] * inv_freq
    ones = lambda n: jnp.ones((nb, s, n), F32)
    zeros = lambda n: jnp.zeros((nb, s, n), F32)
    rope_c = jnp.concatenate([ones(64), jnp.cos(ang), jnp.cos(ang), ones(32)], axis=-1)
    rope_s = jnp.concatenate([zeros(64), jnp.sin(ang), jnp.sin(ang), zeros(32)], axis=-1)
    bucket = _t5_bucket()
    bias = _bias_build(jnp.pad(rel_table.T, ((0, 8 - SW_HEADS), (0, LANES - REL_BUCKETS))), bucket, "rel_bias")

    row = lambda g: g.reshape(1, -1)
    twice = lambda g: jnp.concatenate([g, g]).reshape(1, LANES)

    saved = []
    xl = x
    for l in range(nl):
        proj, h1 = _ln_mod_matmul(xl, row(norm1_g[l]), scale1[l], shift1[l], w_in_pt[l], f"l{l}_in_proj")
        prep_args = (proj, rope_c, rope_s, row(mla_cq_g[l]), row(mla_ckv_g[l]), _pad_lanes(mla_qn_g[l], LANES),
                     _pad_lanes(mla_kn_g[l], LANES), w_uq_pt[l], w_ukv_t[l])
        qm, km, vm = _mla_prep(*prep_args, f"l{l}_mla_prep")
        o_a, ct_a, cnt_a, up_g, down_g = _sb_fwd(proj, f"l{l}_sb_fwd", riding=ffn_local(l))
        w_up_t.append(full(up_g))
        w_down_f.append(full(down_g))
        o_b, lse_b, *nxt = _mla_fwd(qm, km, vm, f"l{l}_mla_fwd", riding=attn_local(l + 1) if l + 1 < nl else ())
        if nxt:
            w_in_pt.append(pad_in(full(nxt[0])))
            w_uq_pt.append(pad_uq(full(nxt[1])))
            w_ukv_t.append(full(nxt[2]))
            w_out_f.append(full(nxt[3]))
        sinks = jnp.broadcast_to(jnp.pad(sw_sinks[l], (0, 2))[:, None], (8, LANES))
        swa_args = (proj, twice(sw_qn_g[l]), twice(sw_kn_g[l]), sinks, bias)
        o_c = _swa_fwd(*swa_args, f"l{l}_swa_fwd")
        wo = [w_out_f[l][:256], w_out_f[l][256:640], w_out_f[l][640:]]
        x_mid, y1 = _out_proj([o_a, o_b, o_c], wo, gate1[l], xl, f"l{l}_out_proj")
        u_pre, h2 = _ln_mod_matmul(x_mid, row(norm2_g[l]), scale2[l], shift2[l], w_up_t[l], f"l{l}_up_proj")
        x_out, y2 = _conv_gate_matmul(u_pre, conv_full[l], row(conv_b[l]), w_down_f[l], gate2[l], x_mid, f"l{l}_ffn_down")
        saved.append(dict(x=xl, proj=proj, h1=h1, prep=prep_args, qkv=(qm, km, vm), o_a=o_a, ct_a=ct_a, cnt_a=cnt_a, o_b=o_b, lse_b=lse_b,
                          swa=swa_args, o_c=o_c, wo=wo, y1=y1, x_mid=x_mid, u_pre=u_pre, h2=h2, y2=y2))
        xl = x_out

    loss_blk, dx = _loss_grad(xl, loss_target, "loss")
    loss = lax.psum(loss_blk[0, 0], ("x", "y", "c"))

    t = nb * s
    flat = lambda a: a.reshape(t, a.shape[-1])
    grads = [None] * nl
    dmods = [None] * nl
    sharded_out = [None] * nl
    sharded_names = ["w_in", "w_uq", "w_ukv", "w_up", "w_out", "w_down", "conv_w"]
    sharded_wmv = dict(w_in=(w_in, m_w_in, v_w_in), w_uq=(w_uq, m_w_uq, v_w_uq), w_ukv=(w_ukv, m_w_ukv, v_w_ukv),
                       w_up=(w_up, m_w_up, v_w_up), w_out=(w_out, m_w_out, v_w_out), w_down=(w_down, m_w_down, v_w_down),
                       conv_w=(conv_w, m_conv_w, v_conv_w))
    n_in, n_up, n_out, n_dn = w_in.shape[2], w_up.shape[2], w_out.shape[1], w_down.shape[1]
    small_sizes = [w_uq[0].size, w_ukv[0].size, conv_w[0].size]
    n_small_rows = -(-sum(small_sizes) // d)
    rows_used = n_in + n_out + n_small_rows
    rows_grad = -(-rows_used // 16) * 16

    def pack_rows(mats, vecs):
        lead = mats[0].shape[:-2]
        flat_part = jnp.concatenate(vecs, axis=-1)
        flat_part = jnp.pad(flat_part, [(0, 0)] * len(lead) + [(0, n_small_rows * d - flat_part.shape[-1])])
        tail = jnp.zeros(lead + (rows_grad - rows_used, d), F32)
        return jnp.concatenate(list(mats) + [flat_part.reshape(lead + (n_small_rows, d)), tail], axis=-2)

    def unpack_rows(a):
        o1, o2 = n_in, n_in + n_out
        flat_part = a[o2:o2 + n_small_rows].reshape(-1)
        s1, s2, s3 = small_sizes[0], small_sizes[0] + small_sizes[1], sum(small_sizes)
        return dict(w_in=a[:o1].T, w_out=a[o1:o2],
                    w_uq=flat_part[:s1].reshape(w_uq.shape[2], -1).T, w_ukv=flat_part[s1:s2].reshape(w_ukv.shape[2], -1).T,
                    conv_w=flat_part[s2:s3].reshape(conv_w.shape[1:]))

    ffn_out = [None] * nl
    core = lax.axis_index("c").reshape(1).astype(jnp.int32)

    def update_ffn(l, recv):
        wmv = [{k: v[o][l] for k, v in sharded_wmv.items()} for o in range(3)]
        res_up = _adamw(recv[0], *[a["w_up"].T for a in wmv], f"l{l}_adamw_up")
        res_dn = _adamw(recv[1], *[a["w_down"] for a in wmv], f"l{l}_adamw_down")
        ffn_out[l] = [dict(w_up=ru.T, w_down=rd) for ru, rd in zip(res_up, res_dn)]

    def update_rest(l, recv):
        wmv = [{k: v[o][l] for k, v in sharded_wmv.items()} for o in range(3)]
        res_rest = _adamw(recv, *[pack_rows([a["w_in"].T, a["w_out"]], [a["w_uq"].T.reshape(-1), a["w_ukv"].T.reshape(-1),
                                                                         a["conv_w"].reshape(-1)]) for a in wmv],
                          f"l{l}_adamw_rest")
        sharded_out[l] = [dict(unpack_rows(rr), **ff) for rr, ff in zip(res_rest, ffn_out[l])]

    pending = None
    dbias = jnp.zeros((SW_HEADS, BLOCK, 2 * BLOCK), F32)
    for l in reversed(range(nl)):
        sv = saved[l]
        (da,), dy2, dgate2 = _gate_bwd_nt(dx, sv["y2"], gate2[l], [w_down_f[l]], f"l{l}_ffn_down_bwd")
        du, a_act, cstats = _conv_gate_bwd(da, sv["u_pre"], conv_full[l], row(conv_b[l]), f"l{l}_conv_gate_bwd")
        res = _ln_mod_matmul_bwd(du, w_up_t[l], sv["x_mid"], row(norm2_g[l]), scale2[l], dx, conv_full[l],
                                 f"l{l}_up_proj_bwd", riding=[pending[1]] if pending else ())
        dx_mid, du_pre, dshift2, dscale2, dg2 = res[:5]
        if pending:
            update_rest(pending[0], res[5])
            pending = None
        g_w_down = _wgrad(flat(a_act), flat(dy2), f"l{l}_w_down_grad")
        g_w_up_t = _wgrad(flat(du_pre), flat(sv["h2"]), f"l{l}_w_up_grad")
        per_dev = lambda g: g.reshape(N_DEV, -1, d)
        ffn_send = [per_dev(g_w_up_t), per_dev(g_w_down)]
        ffn_sib = _pair_exchange(ffn_send, f"l{l}_pair_exchange_ffn")
        ffn_pair = [_pair_add(core, a, b, f"l{l}_pair_add_{k}") for a, b, k in zip(ffn_send, ffn_sib, ("up", "down"))]

        (do_a, do_b, do_c), dy1, dgate1 = _gate_bwd_nt(dx_mid, sv["y1"], gate1[l], sv["wo"], f"l{l}_out_proj_bwd")
        mix = jnp.concatenate([sv["o_a"], sv["o_b"], sv["o_c"]], axis=-1).astype(MXU)
        g_w_out = _wgrad(flat(mix), flat(dy1), f"l{l}_w_out_grad")

        dsb_q, dsb_k, dsb_v = _sb_bwd(sv["proj"], sv["ct_a"], sv["cnt_a"], do_a, f"l{l}_sb_bwd")
        qm, km, vm = sv["qkv"]
        dqm, dkm, dvm, *ffn_recv = _mla_bwd(qm, km, vm, sv["o_b"], sv["lse_b"], do_b, f"l{l}_mla_bwd", riding=ffn_pair)
        update_ffn(l, ffn_recv)
        dsw_q, dkc, dkp, dvc, dvp, dbias_l, dsinks, dg_swq, dg_swk = _swa_bwd(*sv["swa"], do_c, f"l{l}_swa_bwd")
        dbias = dbias + dbias_l
        shift_up = lambda a: jnp.concatenate([a[:, BLOCK:], jnp.zeros((nb, BLOCK, LANES), F32)], axis=1)
        dsw_k = dkc + shift_up(dkp)
        dsw_v = dvc + shift_up(dvp)
        dcq, dckv, dkr, g_w_uq_pt, g_w_ukv_t, dg_cq, dg_ckv, dg_qn, dg_kn = _mla_prep_bwd(
            *sv["prep"], dqm, dkm, dvm, f"l{l}_mla_prep_bwd")
        dproj = jnp.concatenate([dsb_q, dsb_k, dsb_v, dcq, dckv, dsw_q, dsw_k, dsw_v, dkr, zeros(128)], axis=-1)
        dx, dproj_m, dshift1, dscale1, dg1 = _ln_mod_matmul_bwd(
            dproj, w_in_pt[l], sv["x"], row(norm1_g[l]), scale1[l], dx_mid, None, f"l{l}_in_proj_bwd")
        g_w_in_pt = _wgrad(flat(dproj_m), flat(sv["h1"]), f"l{l}_w_in_grad")

        g_w_in_t = jnp.concatenate([g_w_in_pt[:1152], g_w_in_pt[1856:1888], g_w_in_pt[1152:1792]], axis=0)
        g_w_uq_t = g_w_uq_pt.reshape(MLA_HEADS, LANES, 256)[:, :MLA_QK].reshape(MLA_HEADS * MLA_QK, 256)
        dmods[l] = jnp.concatenate([dshift1, dscale1, dgate1, dshift2, dscale2, dgate2], axis=-1).reshape(nb, 6 * d)

        conv_dev = cstats[1:4].reshape(3, N_DEV, -1).transpose(1, 0, 2)
        rest = pack_rows([per_dev(g_w_in_t), per_dev(g_w_out)],
                         [g_w_uq_t.reshape(N_DEV, -1), g_w_ukv_t.reshape(N_DEV, -1), conv_dev.reshape(N_DEV, -1)])
        rest_sib, = _pair_exchange([rest], f"l{l}_pair_exchange_rest")
        rest_pair = _pair_add(core, rest, rest_sib, f"l{l}_pair_add_rest")
        if l > 0:
            pending = (l, rest_pair)
        else:
            update_rest(l, _chip_exchange([rest_pair], f"l{l}_chip_exchange")[0])
        grads[l] = dict(
            norm1_g=dg1[0], norm2_g=dg2[0], mla_cq_g=dg_cq[0], mla_ckv_g=dg_ckv[0], mla_qn_g=dg_qn[0, :MLA_QK],
            mla_kn_g=dg_kn[0, :MLA_QK], sw_qn_g=dg_swq[0, :HEAD] + dg_swq[0, HEAD:], sw_kn_g=dg_swk[0, :HEAD] + dg_swk[0, HEAD:],
            sw_sinks=dsinks[:SW_HEADS, 0], conv_b=cstats[0])
    grad_x = dx
    g_rel = _bias_grad(dbias, bucket, "rel_table_grad")[:SW_HEADS, :REL_BUCKETS].T
    stack = lambda k: jnp.stack([grads[l][k] for l in range(nl)])

    dm_all, = _all_gather([jnp.stack(dmods).reshape(nl * nb, 6 * d)], "gather_dmods")
    dm_all = dm_all.reshape(N_DEV, nl, nb, 6 * d).transpose(1, 0, 2, 3).reshape(nl, N_DEV * nb, 6 * d)
    dm_my = lax.dynamic_slice_in_dim(dm_all, me * n_ada, n_ada, axis=2)
    g_w_ada, g_b_ada = _ada_bwd(c_all, dm_my, dm_all, "ada_bwd")
    g_b_ada = g_b_ada.reshape(nl, 6 * d)

    big_out = [{k: jnp.stack([sharded_out[l][o][k] for l in range(nl)]) for k in sharded_names} for o in range(4)]
    packf = lambda dct, names, rows: jnp.pad(jnp.concatenate([dct[k].reshape(-1) for k in names]),
                                             (0, rows * LANES - sum(dct[k].size for k in names))).reshape(rows, LANES)

    small_names = ["rel_table", "norm1_g", "norm2_g", "mla_cq_g", "mla_ckv_g", "mla_qn_g", "mla_kn_g",
                   "sw_qn_g", "sw_kn_g", "sw_sinks", "conv_b"]
    small_w = dict(rel_table=rel_table, norm1_g=norm1_g, norm2_g=norm2_g, mla_cq_g=mla_cq_g, mla_ckv_g=mla_ckv_g,
                   mla_qn_g=mla_qn_g, mla_kn_g=mla_kn_g, sw_qn_g=sw_qn_g, sw_kn_g=sw_kn_g, sw_sinks=sw_sinks, conv_b=conv_b)
    small_m = dict(rel_table=m_rel_table, norm1_g=m_norm1_g, norm2_g=m_norm2_g, mla_cq_g=m_mla_cq_g, mla_ckv_g=m_mla_ckv_g,
                   mla_qn_g=m_mla_qn_g, mla_kn_g=m_mla_kn_g, sw_qn_g=m_sw_qn_g, sw_kn_g=m_sw_kn_g, sw_sinks=m_sw_sinks, conv_b=m_conv_b)
    small_v = dict(rel_table=v_rel_table, norm1_g=v_norm1_g, norm2_g=v_norm2_g, mla_cq_g=v_mla_cq_g, mla_ckv_g=v_mla_ckv_g,
                   mla_qn_g=v_mla_qn_g, mla_kn_g=v_mla_kn_g, sw_qn_g=v_sw_qn_g, sw_kn_g=v_sw_kn_g, sw_sinks=v_sw_sinks, conv_b=v_conv_b)
    small_g = {k: (g_rel if k == "rel_table" else stack(k)) for k in small_names}
    n_small = sum(small_w[k].size for k in small_names)
    rows_small = -(-n_small // (8 * LANES)) * 8
    small_parts, = _all_gather([packf(small_g, small_names, rows_small)], "gather_small_grads")
    small_out = _adamw(small_parts, packf(small_w, small_names, rows_small), packf(small_m, small_names, rows_small),
                       packf(small_v, small_names, rows_small), "adamw_replicated")
    small_out = [dict(zip(small_names, _unpack(o.reshape(-1), [small_w[k].shape for k in small_names]))) for o in small_out]

    two_d = lambda a: a.reshape(-1, a.shape[-1])
    res_w = _adamw(two_d(g_w_ada)[None], two_d(w_ada), two_d(m_w_ada), two_d(v_w_ada), "adamw_w_ada")
    res_b = _adamw(g_b_ada[None], b_ada, m_b_ada, v_b_ada, "adamw_b_ada")
    ada_out = [dict(w_ada=rw.reshape(w_ada.shape), b_ada=rb) for rw, rb in zip(res_w, res_b)]

    order = ["rel_table", "norm1_g", "norm2_g", "w_ada", "b_ada", "w_in", "mla_cq_g", "w_uq", "mla_ckv_g", "w_ukv",
             "mla_qn_g", "mla_kn_g", "sw_qn_g", "sw_kn_g", "sw_sinks", "w_out", "w_up", "conv_w", "conv_b", "w_down"]
    outs = [{**big_out[k], **small_out[k], **ada_out[k]} for k in range(4)]
    return (loss, grad_x, *[outs[0][n] for n in order], *[outs[1][n] for n in order],
            *[outs[2][n] for n in order], *[outs[3][n] for n in order])
```

```python
import math

import jax
import jax.numpy as jnp
from jax import lax
from jax.experimental import pallas as pl
from jax.experimental.pallas import tpu as pltpu

F32 = jnp.float32
MXU = jnp.bfloat16
EPS = 1e-6
NEG = -1e30
VMEM_LIMIT_BYTES = 56 * 1024 * 1024
N_DEV = 8
MESH = pl.DeviceIdType.MESH

D_MODEL = 1024
D_FF = 2816
HEAD = 64
LANES = 128
MLA_HEADS = 6
MLA_QK = 96
SW_HEADS = 6
REL_BUCKETS = 32
BLOCK = 128
SB_SCALE = HEAD ** -0.5
SB_DEAD = -105.0
SW_SCALE = HEAD ** -0.5
MLA_SCALE = MLA_QK ** -0.5
ROPE_THETA = 10000.0
D_IN_PAD = 2048
COL_SBQ, COL_SBK, COL_SBV, COL_CQ, COL_CKV, COL_SWQ, COL_SWK, COL_SWV, COL_KR = 0, 256, 512, 768, 1024, 1152, 1536, 1664, 1792

HALO = 16
ROW_TILE_BYTES = 1 << 20
ADAM_LR, ADAM_B1, ADAM_B2, ADAM_EPS, ADAM_WD, ADAM_STEP = 0.001, 0.9, 0.999, 1e-08, 0.01, 10


def _cp(*sem):
    return pltpu.CompilerParams(dimension_semantics=sem, vmem_limit_bytes=VMEM_LIMIT_BYTES)


def _iota(shape, dim):
    return lax.broadcasted_iota(jnp.int32, shape, dim)


def _dot(a, b):
    return jnp.dot(a, b, preferred_element_type=F32)


def _dot_nt(a, b):
    return lax.dot_general(a, b, (((1,), (1,)), ((), ())), preferred_element_type=F32)


def _dot_tn(a, b):
    return lax.dot_general(a, b, (((0,), (0,)), ((), ())), preferred_element_type=F32)


def _cumdot(x, u):
    hi = x.astype(MXU)
    mid = (x - hi.astype(F32)).astype(MXU)
    return _dot(hi, u) + _dot(mid, u)


def _sigmoid(x):
    return 1.0 / (1.0 + jnp.exp(-x))


def _rsum(x):
    return jnp.sum(x, axis=-1, keepdims=True)


def _csum(x):
    return jnp.sum(x, axis=0, keepdims=True)


def _all_gather(xs, name):
    na = len(xs)

    def body(*refs):
        start, finish = _gather_steps(refs[:na], refs[na:2 * na], *refs[2 * na:])
        start()
        finish()

    hbm = pl.BlockSpec(memory_space=pl.ANY)
    return pl.pallas_call(
        body, name=name, out_shape=_gather_out_shapes(xs), in_specs=[hbm] * na, out_specs=[hbm] * na,
        scratch_shapes=_gather_sems(na))(*xs)


def _gather_out_shapes(xs):
    return [jax.ShapeDtypeStruct((N_DEV,) + a.shape, a.dtype) for a in xs]


def _gather_sems(na):
    return [pltpu.SemaphoreType.DMA((7 * na,)), pltpu.SemaphoreType.DMA((7 * na,)), pltpu.SemaphoreType.DMA((na,))]


def _gather_steps(x_refs, out_refs, send_sems, recv_sems, local_sems):
    na = len(x_refs)
    x, y, c = lax.axis_index("x"), lax.axis_index("y"), lax.axis_index("c")
    me, sibling = (x, y, c), (x, y, 1 - c)
    chips = [(1 - x, y), (x, 1 - y), (1 - x, 1 - y)]

    def slot(a, px, py, pc):
        return out_refs[a].at[4 * px + 2 * py + pc]

    def copy(a, k, block, to, src=None):
        return pltpu.make_async_remote_copy(
            src_ref=slot(a, *block) if src is None else src, dst_ref=slot(a, *block),
            send_sem=send_sems.at[7 * a + k], recv_sem=recv_sems.at[7 * a + k], device_id=to, device_id_type=MESH)

    def own_copies(a):
        return ([copy(a, 0, me, sibling, src=x_refs[a])]
                + [copy(a, 1 + j, me, (*chip, c), src=x_refs[a]) for j, chip in enumerate(chips)])

    def local_copy(a):
        return pltpu.make_async_copy(x_refs[a], slot(a, *me), local_sems.at[a])

    def start():
        for a in range(na):
            local_copy(a).start()
            for cp in own_copies(a):
                cp.start()

    def finish():
        passed = []
        for j, chip in enumerate(chips):
            for a in range(na):
                copy(a, 1 + j, (*chip, c), me).wait_recv()
                passed.append(copy(a, 4 + j, (*chip, c), sibling))
                passed[-1].start()
        for a in range(na):
            copy(a, 0, sibling, me).wait_recv()
            for j, chip in enumerate(chips):
                copy(a, 4 + j, (*chip, 1 - c), me).wait_recv()
        for a in range(na):
            for cp in own_copies(a):
                cp.wait_send()
        for cp in passed:
            cp.wait_send()
        for a in range(na):
            local_copy(a).wait()

    return start, finish


def _with_gather(body, n_in, n_out, na, grid):
    if not na:
        return body

    def wrapped(*refs):
        ins, ride_in = refs[:n_in], refs[n_in:n_in + na]
        outs = refs[n_in + na:n_in + na + n_out]
        ride_out = refs[n_in + na + n_out:n_in + 2 * na + n_out]
        ids = [pl.program_id(k) for k in range(len(grid))]
        first, last = ids[0] == 0, ids[0] == grid[0] - 1
        for k in range(1, len(grid)):
            first, last = first & (ids[k] == 0), last & (ids[k] == grid[k] - 1)
        start, finish = _gather_steps(ride_in, ride_out, *refs[n_in + 2 * na + n_out:])
        pl.when(first)(start)
        body(*ins, *outs)
        pl.when(last)(finish)

    return wrapped


def _pair_exchange(xs, name):
    na = len(xs)

    def body(*refs):
        x_refs, out_refs = refs[:na], refs[na:2 * na]
        send_sems, recv_sems = refs[2 * na:]
        x, y, c = lax.axis_index("x"), lax.axis_index("y"), lax.axis_index("c")
        copies = []
        for a in range(na):
            for q in range(4):
                copies.append(pltpu.make_async_remote_copy(
                    src_ref=x_refs[a].at[2 * q + 1 - c], dst_ref=out_refs[a].at[q],
                    send_sem=send_sems.at[4 * a + q], recv_sem=recv_sems.at[4 * a + q],
                    device_id=(x, y, 1 - c), device_id_type=MESH))
                copies[-1].start()
        for cp in copies:
            cp.wait()

    hbm = pl.BlockSpec(memory_space=pl.ANY)
    return pl.pallas_call(
        body, name=name, out_shape=[jax.ShapeDtypeStruct((4,) + a.shape[1:], a.dtype) for a in xs],
        in_specs=[hbm] * na, out_specs=[hbm] * na,
        scratch_shapes=[pltpu.SemaphoreType.DMA((4 * na,)), pltpu.SemaphoreType.DMA((4 * na,))])(*xs)


def _row_tile(r, ncol):
    if r * ncol * 4 <= ROW_TILE_BYTES:
        return r
    return max(t for t in range(16, r, 16) if r % t == 0 and t * ncol * 4 <= ROW_TILE_BYTES)


def _pair_add(core, xs, sib, name):
    _, r, ncol = xs.shape
    tr = _row_tile(r, ncol)

    def body(c_ref, x_ref, s_ref, o_ref):
        o_ref[...] = (x_ref[...] + s_ref[...]).astype(MXU)

    return pl.pallas_call(
        body, name=name,
        grid_spec=pltpu.PrefetchScalarGridSpec(
            num_scalar_prefetch=1, grid=(4, r // tr),
            in_specs=[pl.BlockSpec((None, tr, ncol), lambda q, i, c_ref: (2 * q + c_ref[0], i, 0)),
                      pl.BlockSpec((None, tr, ncol), lambda q, i, c_ref: (q, i, 0))],
            out_specs=pl.BlockSpec((None, tr, ncol), lambda q, i, c_ref: (q, i, 0))),
        out_shape=jax.ShapeDtypeStruct((4, r, ncol), MXU),
        compiler_params=_cp("parallel", "parallel"))(core, xs, sib)


def _chip_exchange(xs, name):
    na = len(xs)

    def body(*refs):
        copies = _chip_exchange_copies(refs[:na], refs[na:2 * na], *refs[2 * na:])
        for cp in copies:
            cp.start()
        for cp in copies:
            cp.wait()

    hbm = pl.BlockSpec(memory_space=pl.ANY)
    return pl.pallas_call(
        body, name=name, out_shape=[jax.ShapeDtypeStruct(a.shape, a.dtype) for a in xs],
        in_specs=[hbm] * na, out_specs=[hbm] * na, scratch_shapes=_chip_exchange_sems(na))(*xs)


def _with_chip_exchange(body, n_in, n_out, na, grid):
    if not na:
        return body

    def wrapped(*refs):
        ins, ride_in = refs[:n_in], refs[n_in:n_in + na]
        outs = refs[n_in + na:n_in + na + n_out]
        ride_out = refs[n_in + na + n_out:n_in + 2 * na + n_out]
        ids = [pl.program_id(k) for k in range(len(grid))]
        first, last = ids[0] == 0, ids[0] == grid[0] - 1
        for k in range(1, len(grid)):
            first, last = first & (ids[k] == 0), last & (ids[k] == grid[k] - 1)
        copies = _chip_exchange_copies(ride_in, ride_out, *refs[n_in + 2 * na + n_out:])

        @pl.when(first)
        def _():
            for cp in copies:
                cp.start()

        body(*ins, *outs)

        @pl.when(last)
        def _():
            for cp in copies:
                cp.wait()

    return wrapped


def _chip_exchange_sems(na):
    return [pltpu.SemaphoreType.DMA((3 * na,)), pltpu.SemaphoreType.DMA((3 * na,)), pltpu.SemaphoreType.DMA((na,))]


def _chip_exchange_copies(x_refs, out_refs, send_sems, recv_sems, local_sems):
    x, y, c = lax.axis_index("x"), lax.axis_index("y"), lax.axis_index("c")
    me = 2 * x + y
    copies = [pltpu.make_async_copy(x_refs[a].at[me], out_refs[a].at[me], local_sems.at[a]) for a in range(len(x_refs))]
    for k, (dx, dy) in enumerate([(1, 0), (0, 1), (1, 1)]):
        px = 1 - x if dx else x
        py = 1 - y if dy else y
        for a in range(len(x_refs)):
            copies.append(pltpu.make_async_remote_copy(
                src_ref=x_refs[a].at[2 * px + py], dst_ref=out_refs[a].at[me],
                send_sem=send_sems.at[3 * a + k], recv_sem=recv_sems.at[3 * a + k],
                device_id=(px, py, c), device_id_type=MESH))
    return copies


def _ada_fwd(c_all, w_ada, b_my, name):
    nl, d, n = w_ada.shape
    nb = c_all.shape[0]

    def body(c_ref, w_ref, b_ref, o_ref):
        cv = c_ref[...]
        sc = (cv * _sigmoid(cv)).astype(MXU)
        o_ref[...] = _dot(sc, w_ref[...].astype(MXU)) + b_ref[...]

    return pl.pallas_call(
        body, name=name, grid=(nl,),
        in_specs=[pl.BlockSpec((nb, d), lambda l: (0, 0)),
                  pl.BlockSpec((None, d, n), lambda l: (l, 0, 0)),
                  pl.BlockSpec((None, 1, n), lambda l: (l, 0, 0))],
        out_specs=pl.BlockSpec((None, nb, n), lambda l: (l, 0, 0)),
        out_shape=jax.ShapeDtypeStruct((nl, nb, n), F32),
        compiler_params=_cp("parallel"))(c_all, w_ada, b_my)


def _ada_bwd(c_all, dmods_my, dmods_all, name):
    nl, nb, n = dmods_my.shape
    d = c_all.shape[1]
    nfull = dmods_all.shape[2]

    def body(c_ref, dm_ref, da_ref, dw_ref, db_ref):
        cv = c_ref[...]
        sc = (cv * _sigmoid(cv)).astype(MXU)
        dw_ref[...] = _dot_tn(sc, dm_ref[...].astype(MXU))
        db_ref[...] = _csum(da_ref[...])

    return pl.pallas_call(
        body, name=name, grid=(nl,),
        in_specs=[pl.BlockSpec((nb, d), lambda l: (0, 0)),
                  pl.BlockSpec((None, nb, n), lambda l: (l, 0, 0)),
                  pl.BlockSpec((None, nb, nfull), lambda l: (l, 0, 0))],
        out_specs=[pl.BlockSpec((None, d, n), lambda l: (l, 0, 0)),
                   pl.BlockSpec((None, 1, nfull), lambda l: (l, 0, 0))],
        out_shape=[jax.ShapeDtypeStruct((nl, d, n), F32), jax.ShapeDtypeStruct((nl, 1, nfull), F32)],
        compiler_params=_cp("parallel"))(c_all, dmods_my, dmods_all)


def _ln_mod_matmul(x, g, scale, shift, w, name):
    nb, s, d = x.shape
    n = w.shape[0]
    tm, tn = min(1024, s), 512

    def body(x_ref, g_ref, sc_ref, sh_ref, w_ref, y_ref, h_ref, h_s):
        @pl.when(pl.program_id(2) == 0)
        def _():
            xf = x_ref[...]
            rstd = lax.rsqrt(jnp.mean(xf * xf, axis=-1, keepdims=True) + EPS)
            hv = (xf * rstd * g_ref[...]) * (1.0 + sc_ref[...]) + sh_ref[...]
            h_s[...] = hv.astype(MXU)
            h_ref[...] = h_s[...]

        y_ref[...] = _dot_nt(h_s[...], w_ref[...]).astype(MXU)

    return pl.pallas_call(
        body, name=name, grid=(nb, s // tm, n // tn),
        in_specs=[pl.BlockSpec((None, tm, d), lambda b, i, j: (b, i, 0)),
                  pl.BlockSpec((1, d), lambda b, i, j: (0, 0)),
                  pl.BlockSpec((None, 1, d), lambda b, i, j: (b, 0, 0)),
                  pl.BlockSpec((None, 1, d), lambda b, i, j: (b, 0, 0)),
                  pl.BlockSpec((tn, d), lambda b, i, j: (j, 0))],
        out_specs=[pl.BlockSpec((None, tm, tn), lambda b, i, j: (b, i, j)),
                   pl.BlockSpec((None, tm, d), lambda b, i, j: (b, i, 0))],
        out_shape=[jax.ShapeDtypeStruct((nb, s, n), MXU), jax.ShapeDtypeStruct((nb, s, d), MXU)],
        scratch_shapes=[pltpu.VMEM((tm, d), MXU)],
        compiler_params=_cp("parallel", "parallel", "arbitrary"))(x, g, scale, shift, w)


def _ln_mod_matmul_bwd(dy, w, x, g, scale, dres, conv_w, name, riding=()):
    nb, s, n = dy.shape
    d = x.shape[-1]
    tm, tn = min(512, s), 512
    ni, nj = s // tm, n // tn
    hb = tm // HALO
    conv = conv_w is not None
    na = len(riding)

    def body(*refs):
        if conv:
            dy_ref, nx_ref, cw_ref = refs[:3]
            refs = refs[3:]
        else:
            dy_ref = refs[0]
            refs = refs[1:]
        w_ref, x_ref, g_ref, sc_ref, dr_ref = refs[:5]
        ride_in, refs = refs[5:5 + na], refs[5 + na:]
        dx_ref, dyp_ref, dsh_ref, dsc_ref, dg_ref = refs[:5]
        ride_out, refs = refs[5:5 + na], refs[5 + na:]
        acc = refs[0]
        b, i, j = pl.program_id(0), pl.program_id(1), pl.program_id(2)
        if na:
            copies = _chip_exchange_copies(ride_in, ride_out, *refs[1:])

            @pl.when((b == 0) & (i == 0) & (j == 0))
            def _():
                for cp in copies:
                    cp.start()

        @pl.when(j == 0)
        def _():
            acc[...] = jnp.zeros_like(acc)

        @pl.when((j == 0) & (i == 0))
        def _():
            dsh_ref[...] = jnp.zeros_like(dsh_ref)
            dsc_ref[...] = jnp.zeros_like(dsc_ref)

        @pl.when((j == 0) & (i == 0) & (b == 0))
        def _():
            dg_ref[...] = jnp.zeros_like(dg_ref)

        dv = dy_ref[...].astype(F32)
        if conv:
            rows = _iota((tm, 1), 0)
            nx = jnp.where(i == ni - 1, 0.0, nx_ref[...].astype(F32))
            n1 = jnp.where(rows == tm - 1, nx[0:1, :], pltpu.roll(dv, tm - 1, 0))
            n2 = jnp.where(rows == tm - 2, nx[0:1, :], jnp.where(rows == tm - 1, nx[1:2, :], pltpu.roll(dv, tm - 2, 0)))
            cw = cw_ref[...]
            dv = cw[2:3, :] * dv + cw[1:2, :] * n1 + cw[0:1, :] * n2
        dp = dv.astype(MXU)
        dyp_ref[...] = dp
        acc[...] += _dot(dp, w_ref[...])

        @pl.when(j == nj - 1)
        def _():
            dh = acc[...]
            xf = x_ref[...]
            rstd = lax.rsqrt(jnp.mean(xf * xf, axis=-1, keepdims=True) + EPS)
            xn = xf * rstd
            gg = g_ref[...]
            sc1 = 1.0 + sc_ref[...]
            dsh_ref[...] += _csum(dh)
            dsc_ref[...] += _csum(dh * xn * gg)
            dg_ref[...] += _csum(dh * xn * sc1)
            dn = dh * gg * sc1
            dx_ref[...] = dr_ref[...] + rstd * (dn - xn * jnp.mean(dn * xn, axis=-1, keepdims=True))

        if na:
            @pl.when((b == nb - 1) & (i == ni - 1) & (j == nj - 1))
            def _():
                for cp in copies:
                    cp.wait()

    hbm = pl.BlockSpec(memory_space=pl.ANY)
    in_specs = [pl.BlockSpec((None, tm, tn), lambda b, i, j: (b, i, j))]
    args = [dy]
    if conv:
        in_specs += [pl.BlockSpec((None, HALO, tn), lambda b, i, j: (b, jnp.minimum((i + 1) * hb, s // HALO - 1), j)),
                     pl.BlockSpec((3, tn), lambda b, i, j: (0, j))]
        args += [dy, conv_w]
    in_specs += [pl.BlockSpec((tn, d), lambda b, i, j: (j, 0)),
                 pl.BlockSpec((None, tm, d), lambda b, i, j: (b, i, 0)),
                 pl.BlockSpec((1, d), lambda b, i, j: (0, 0)),
                 pl.BlockSpec((None, 1, d), lambda b, i, j: (b, 0, 0)),
                 pl.BlockSpec((None, tm, d), lambda b, i, j: (b, i, 0))]
    in_specs += [hbm] * na
    args += [w, x, g, scale, dres, *riding]
    return pl.pallas_call(
        body, name=name, grid=(nb, ni, nj), in_specs=in_specs,
        out_specs=[pl.BlockSpec((None, tm, d), lambda b, i, j: (b, i, 0)),
                   pl.BlockSpec((None, tm, tn), lambda b, i, j: (b, i, j)),
                   pl.BlockSpec((None, 1, d), lambda b, i, j: (b, 0, 0)),
                   pl.BlockSpec((None, 1, d), lambda b, i, j: (b, 0, 0)),
                   pl.BlockSpec((1, d), lambda b, i, j: (0, 0))] + [hbm] * na,
        out_shape=[jax.ShapeDtypeStruct((nb, s, d), F32), jax.ShapeDtypeStruct((nb, s, n), MXU),
                   jax.ShapeDtypeStruct((nb, 1, d), F32), jax.ShapeDtypeStruct((nb, 1, d), F32),
                   jax.ShapeDtypeStruct((1, d), F32)] + [jax.ShapeDtypeStruct(a.shape, a.dtype) for a in riding],
        scratch_shapes=[pltpu.VMEM((tm, d), F32)] + (_chip_exchange_sems(na) if na else []),
        compiler_params=_cp("arbitrary", "arbitrary", "arbitrary"))(*args)


def _wgrad(xm, dym, name):
    t, k = xm.shape
    n = dym.shape[1]
    tk = 1408 if k % 1408 == 0 else 1024
    tt = min(512, t)

    def body(x_ref, dy_ref, o_ref):
        @pl.when(pl.program_id(1) == 0)
        def _():
            o_ref[...] = jnp.zeros_like(o_ref)

        o_ref[...] += _dot_tn(x_ref[...], dy_ref[...])

    return pl.pallas_call(
        body, name=name, grid=(k // tk, t // tt),
        in_specs=[pl.BlockSpec((tt, tk), lambda a, c: (c, a)),
                  pl.BlockSpec((tt, n), lambda a, c: (c, 0))],
        out_specs=pl.BlockSpec((tk, n), lambda a, c: (a, 0)),
        out_shape=jax.ShapeDtypeStruct((k, n), F32),
        compiler_params=_cp("parallel", "arbitrary"))(xm, dym)


def _out_proj(parts, ws, gate, res, name):
    nb, s, d = res.shape
    tm = min(512, s)
    npart = len(parts)

    def body(*refs):
        p_refs, w_refs = refs[:npart], refs[npart:2 * npart]
        gt_ref, res_ref, xo_ref, y_ref = refs[2 * npart:]
        y = _dot(p_refs[0][...].astype(MXU), w_refs[0][...])
        for p_ref, w_ref in zip(p_refs[1:], w_refs[1:]):
            y = y + _dot(p_ref[...].astype(MXU), w_ref[...])
        y_ref[...] = y
        xo_ref[...] = res_ref[...] + gt_ref[...] * y

    in_specs = [pl.BlockSpec((None, tm, p.shape[-1]), lambda b, i: (b, i, 0)) for p in parts]
    in_specs += [pl.BlockSpec(w.shape, lambda b, i: (0, 0)) for w in ws]
    in_specs += [pl.BlockSpec((None, 1, d), lambda b, i: (b, 0, 0)),
                 pl.BlockSpec((None, tm, d), lambda b, i: (b, i, 0))]
    return pl.pallas_call(
        body, name=name, grid=(nb, s // tm), in_specs=in_specs,
        out_specs=[pl.BlockSpec((None, tm, d), lambda b, i: (b, i, 0))] * 2,
        out_shape=[jax.ShapeDtypeStruct((nb, s, d), F32)] * 2,
        compiler_params=_cp("parallel", "parallel"))(*parts, *ws, gate, res)


def _gate_bwd_nt(dx, y, gate, ws, name):
    nb, s, d = dx.shape
    tm = min(256, s)
    npart = len(ws)

    def body(*refs):
        dx_ref, y_ref, gt_ref = refs[:3]
        w_refs = refs[3:3 + npart]
        da_refs = refs[3 + npart:3 + 2 * npart]
        dy_ref, dgt_ref = refs[3 + 2 * npart:]

        @pl.when(pl.program_id(1) == 0)
        def _():
            dgt_ref[...] = jnp.zeros_like(dgt_ref)

        dxv = dx_ref[...]
        dyv = (dxv * gt_ref[...]).astype(MXU)
        dy_ref[...] = dyv
        dgt_ref[...] += _csum(dxv * y_ref[...])
        for w_ref, da_ref in zip(w_refs, da_refs):
            da_ref[...] = _dot_nt(dyv, w_ref[...])

    tile = pl.BlockSpec((None, tm, d), lambda b, i: (b, i, 0))
    row = pl.BlockSpec((None, 1, d), lambda b, i: (b, 0, 0))
    outs = pl.pallas_call(
        body, name=name, grid=(nb, s // tm),
        in_specs=[tile, tile, row] + [pl.BlockSpec(w.shape, lambda b, i: (0, 0)) for w in ws],
        out_specs=[pl.BlockSpec((None, tm, w.shape[0]), lambda b, i: (b, i, 0)) for w in ws] + [tile, row],
        out_shape=[jax.ShapeDtypeStruct((nb, s, w.shape[0]), F32) for w in ws]
        + [jax.ShapeDtypeStruct((nb, s, d), MXU), jax.ShapeDtypeStruct((nb, 1, d), F32)],
        compiler_params=_cp("arbitrary", "arbitrary"))(dx, y, gate, *ws)
    return outs[:npart], outs[npart], outs[npart + 1]


def _conv_shifts(xv, halo, rows):
    last, before = halo[HALO - 1:HALO, :], halo[HALO - 2:HALO - 1, :]
    p1 = jnp.where(rows == 0, last, pltpu.roll(xv, 1, 0))
    p2 = jnp.where(rows == 0, before, jnp.where(rows == 1, last, pltpu.roll(xv, 2, 0)))
    return p1, p2


def _conv_gate_matmul(u, cw, cb, wd, gate, res, name):
    nb, s, f2 = u.shape
    f = f2 // 2
    d = wd.shape[1]
    tm = min(512, s)
    tk = f // 2
    nk = f // tk
    hb = tm // HALO

    def body(ug_ref, uv_ref, hg_ref, hv_ref, cwg_ref, cwv_ref, cbg_ref, cbv_ref, wd_ref, gt_ref, res_ref,
             xo_ref, y_ref, acc):
        i, k = pl.program_id(1), pl.program_id(2)

        @pl.when(k == 0)
        def _():
            acc[...] = jnp.zeros_like(acc)

        rows = _iota((tm, 1), 0)

        def conv(x_ref, h_ref, w_ref, b_ref):
            xv = x_ref[...].astype(F32)
            halo = jnp.where(i == 0, 0.0, h_ref[...].astype(F32))
            p1, p2 = _conv_shifts(xv, halo, rows)
            wv = w_ref[...]
            return wv[2:3, :] * xv + wv[1:2, :] * p1 + wv[0:1, :] * p2 + b_ref[...]

        gv = conv(ug_ref, hg_ref, cwg_ref, cbg_ref)
        vv = conv(uv_ref, hv_ref, cwv_ref, cbv_ref)
        av = gv * _sigmoid(gv) * vv
        acc[...] += _dot(av.astype(MXU), wd_ref[...])

        @pl.when(k == nk - 1)
        def _():
            y = acc[...]
            y_ref[...] = y
            xo_ref[...] = res_ref[...] + gt_ref[...] * y

    def halo_idx(off):
        return lambda b, i, k: (b, jnp.maximum(i * hb - 1, 0), k + off)

    tile = pl.BlockSpec((None, tm, d), lambda b, i, k: (b, i, 0))
    return pl.pallas_call(
        body, name=name, grid=(nb, s // tm, nk),
        in_specs=[pl.BlockSpec((None, tm, tk), lambda b, i, k: (b, i, k)),
                  pl.BlockSpec((None, tm, tk), lambda b, i, k: (b, i, k + nk)),
                  pl.BlockSpec((None, HALO, tk), halo_idx(0)),
                  pl.BlockSpec((None, HALO, tk), halo_idx(nk)),
                  pl.BlockSpec((3, tk), lambda b, i, k: (0, k)),
                  pl.BlockSpec((3, tk), lambda b, i, k: (0, k + nk)),
                  pl.BlockSpec((1, tk), lambda b, i, k: (0, k)),
                  pl.BlockSpec((1, tk), lambda b, i, k: (0, k + nk)),
                  pl.BlockSpec((tk, d), lambda b, i, k: (k, 0)),
                  pl.BlockSpec((None, 1, d), lambda b, i, k: (b, 0, 0)),
                  tile],
        out_specs=[tile, tile],
        out_shape=[jax.ShapeDtypeStruct((nb, s, d), F32)] * 2,
        scratch_shapes=[pltpu.VMEM((tm, d), F32)],
        compiler_params=_cp("parallel", "parallel", "arbitrary"))(u, u, u, u, cw, cw, cb, cb, wd, gate, res)


def _conv_gate_bwd(da, u, cw, cb, name):
    nb, s, f2 = u.shape
    f = f2 // 2
    tm = min(128, s)
    hb = tm // HALO

    def body(da_ref, u_ref, h_ref, cw_ref, cb_ref, du_ref, a_ref, st_ref):
        b, i = pl.program_id(0), pl.program_id(1)

        @pl.when((b == 0) & (i == 0))
        def _():
            st_ref[...] = jnp.zeros_like(st_ref)

        rows = _iota((tm, 1), 0)
        first = i == 0

        def conv(cs):
            xv = u_ref[:, cs].astype(F32)
            halo = jnp.where(first, 0.0, h_ref[:, cs].astype(F32))
            p1, p2 = _conv_shifts(xv, halo, rows)
            wv = cw_ref[:, cs]
            return xv, p1, p2, wv[2:3, :] * xv + wv[1:2, :] * p1 + wv[0:1, :] * p2 + cb_ref[:, cs]

        def stats(cs, du, xv, p1, p2):
            du_ref[:, cs] = du.astype(MXU)
            st_ref[0:1, cs] += _csum(du)
            st_ref[1:2, cs] += _csum(du * p2)
            st_ref[2:3, cs] += _csum(du * p1)
            st_ref[3:4, cs] += _csum(du * xv)

        for k in range(f // LANES):
            cg = slice(k * LANES, (k + 1) * LANES)
            cv = slice(f + k * LANES, f + (k + 1) * LANES)
            xg, g1, g2, gv = conv(cg)
            xv, v1, v2, vv = conv(cv)
            sg = _sigmoid(gv)
            sl = gv * sg
            a_ref[:, cg] = (sl * vv).astype(MXU)
            dav = da_ref[:, cg]
            stats(cg, dav * vv * (sg * (1.0 + gv * (1.0 - sg))), xg, g1, g2)
            stats(cv, dav * sl, xv, v1, v2)

    return pl.pallas_call(
        body, name=name, grid=(nb, s // tm),
        in_specs=[pl.BlockSpec((None, tm, f), lambda b, i: (b, i, 0)),
                  pl.BlockSpec((None, tm, f2), lambda b, i: (b, i, 0)),
                  pl.BlockSpec((None, HALO, f2), lambda b, i: (b, jnp.maximum(i * hb - 1, 0), 0)),
                  pl.BlockSpec((3, f2), lambda b, i: (0, 0)),
                  pl.BlockSpec((1, f2), lambda b, i: (0, 0))],
        out_specs=[pl.BlockSpec((None, tm, f2), lambda b, i: (b, i, 0)),
                   pl.BlockSpec((None, tm, f), lambda b, i: (b, i, 0)),
                   pl.BlockSpec((8, f2), lambda b, i: (0, 0))],
        out_shape=[jax.ShapeDtypeStruct((nb, s, f2), MXU), jax.ShapeDtypeStruct((nb, s, f), MXU),
                   jax.ShapeDtypeStruct((8, f2), F32)],
        compiler_params=_cp("arbitrary", "arbitrary"))(da, u, u, cw, cb)


def _rot(xv, lane):
    return jnp.where((lane >= 64) & (lane < 80), -pltpu.roll(xv, 112, 1),
                     jnp.where((lane >= 80) & (lane < 96), pltpu.roll(xv, 16, 1), 0.0))


def _rot_t(dv, lane):
    return jnp.where((lane >= 80) & (lane < 96), -pltpu.roll(dv, 16, 1),
                     jnp.where((lane >= 64) & (lane < 80), pltpu.roll(dv, 112, 1), 0.0))


def _mla_prep_specs(s, tm):
    def blk(width, col):
        return pl.BlockSpec((None, tm, width), lambda b, i: (b, i, col // width))

    full = lambda shape: pl.BlockSpec(shape, lambda b, i: (0, 0))
    return [blk(256, COL_CQ), blk(128, COL_CKV), blk(128, COL_KR),
            pl.BlockSpec((None, tm, LANES), lambda b, i: (b, i, 0)),
            pl.BlockSpec((None, tm, LANES), lambda b, i: (b, i, 0)),
            full((1, 256)), full((1, 128)), full((1, 128)), full((1, 128)),
            full((768, 256)), full((768, 128))]


def _mla_prep(proj, cs, sn, gcq, gckv, gqn, gkn, wuq, wukv, name):
    nb, s, _ = proj.shape
    tm = min(256, s)

    def body(cq_ref, ckv_ref, kr_ref, c_ref, s_ref, gcq_ref, gckv_ref, gqn_ref, gkn_ref, wuq_ref, wukv_ref,
             q_ref, k_ref, v_ref):
        lane = _iota((tm, LANES), 1)
        cv, sv = c_ref[...], s_ref[...]
        cq = cq_ref[...].astype(F32)
        cqn = cq * lax.rsqrt(jnp.mean(cq * cq, axis=-1, keepdims=True) + EPS) * gcq_ref[...]
        qb = _dot_nt(cqn.astype(MXU), wuq_ref[...])
        ckv = ckv_ref[...].astype(F32)
        ckvn = ckv * lax.rsqrt(jnp.mean(ckv * ckv, axis=-1, keepdims=True) + EPS) * gckv_ref[...]
        kvb = _dot_nt(ckvn.astype(MXU), wukv_ref[...])
        kr = kr_ref[...].astype(F32)
        for h in range(MLA_HEADS):
            hs = slice(h * LANES, (h + 1) * LANES)
            qh = qb[:, hs]
            qn = qh * lax.rsqrt(_rsum(qh * qh) / MLA_QK + EPS) * gqn_ref[...]
            q_ref[:, hs] = (qn * cv + _rot(qn, lane) * sv).astype(MXU)
            kc = jnp.where(lane < HEAD, kvb[:, hs], kr)
            kn = kc * lax.rsqrt(_rsum(kc * kc) / MLA_QK + EPS) * gkn_ref[...]
            k_ref[:, hs] = (kn * cv + _rot(kn, lane) * sv).astype(MXU)
        for j in range(MLA_HEADS // 2):
            va = kvb[:, (2 * j) * LANES:(2 * j + 1) * LANES]
            vb = kvb[:, (2 * j + 1) * LANES:(2 * j + 2) * LANES]
            v_ref[:, j * LANES:(j + 1) * LANES] = jnp.where(lane < HEAD, pltpu.roll(va, HEAD, 1), vb).astype(MXU)

    return pl.pallas_call(
        body, name=name, grid=(nb, s // tm), in_specs=_mla_prep_specs(s, tm),
        out_specs=[pl.BlockSpec((None, tm, 768), lambda b, i: (b, i, 0)),
                   pl.BlockSpec((None, tm, 768), lambda b, i: (b, i, 0)),
                   pl.BlockSpec((None, tm, 384), lambda b, i: (b, i, 0))],
        out_shape=[jax.ShapeDtypeStruct((nb, s, 768), MXU), jax.ShapeDtypeStruct((nb, s, 768), MXU),
                   jax.ShapeDtypeStruct((nb, s, 384), MXU)],
        compiler_params=_cp("parallel", "parallel"))(proj, proj, proj, cs, sn, gcq, gckv, gqn, gkn, wuq, wukv)


def _mla_prep_bwd(proj, cs, sn, gcq, gckv, gqn, gkn, wuq, wukv, dq, dk, dv, name):
    nb, s, _ = proj.shape
    tm = min(256, s)

    def body(cq_ref, ckv_ref, kr_ref, c_ref, s_ref, gcq_ref, gckv_ref, gqn_ref, gkn_ref, wuq_ref, wukv_ref,
             dq_ref, dk_ref, dv_ref,
             dcq_ref, dckv_ref, dkr_ref, dwuq_ref, dwukv_ref, dgcq_ref, dgckv_ref, dgqn_ref, dgkn_ref,
             dqb_s, dkvb_s):
        @pl.when((pl.program_id(0) == 0) & (pl.program_id(1) == 0))
        def _():
            for r in (dwuq_ref, dwukv_ref, dgcq_ref, dgckv_ref, dgqn_ref, dgkn_ref):
                r[...] = jnp.zeros_like(r)

        lane = _iota((tm, LANES), 1)
        cv, sv = c_ref[...], s_ref[...]
        gqn, gkn = gqn_ref[...], gkn_ref[...]
        cq = cq_ref[...].astype(F32)
        rc = lax.rsqrt(jnp.mean(cq * cq, axis=-1, keepdims=True) + EPS)
        chat = cq * rc
        cqn = (chat * gcq_ref[...]).astype(MXU)
        qb = _dot_nt(cqn, wuq_ref[...])
        ckv = ckv_ref[...].astype(F32)
        rkv = lax.rsqrt(jnp.mean(ckv * ckv, axis=-1, keepdims=True) + EPS)
        kvhat = ckv * rkv
        ckvn = (kvhat * gckv_ref[...]).astype(MXU)
        kvb = _dot_nt(ckvn, wukv_ref[...])
        kr = kr_ref[...].astype(F32)
        dgq = jnp.zeros((1, LANES), F32)
        dgk = jnp.zeros((1, LANES), F32)
        dkr = jnp.zeros((tm, LANES), F32)
        for h in range(MLA_HEADS):
            hs = slice(h * LANES, (h + 1) * LANES)
            qh = qb[:, hs]
            rq = lax.rsqrt(_rsum(qh * qh) / MLA_QK + EPS)
            qhat = qh * rq
            dqr = dq_ref[:, hs]
            dqn = dqr * cv + _rot_t(dqr * sv, lane)
            dgq = dgq + _csum(dqn * qhat)
            dyq = dqn * gqn
            dqb_s[:, hs] = (rq * (dyq - qhat * (_rsum(dyq * qhat) / MLA_QK))).astype(MXU)

            kc = jnp.where(lane < HEAD, kvb[:, hs], kr)
            rk = lax.rsqrt(_rsum(kc * kc) / MLA_QK + EPS)
            khat = kc * rk
            dkr_h = dk_ref[:, hs]
            dkn = dkr_h * cv + _rot_t(dkr_h * sv, lane)
            dgk = dgk + _csum(dkn * khat)
            dyk = dkn * gkn
            dkc = rk * (dyk - khat * (_rsum(dyk * khat) / MLA_QK))
            dkr = dkr + jnp.where(lane >= HEAD, dkc, 0.0)
            dvb = dv_ref[:, (h // 2) * LANES:(h // 2 + 1) * LANES]
            dvp = dvb if h % 2 == 1 else pltpu.roll(dvb, HEAD, 1)
            dkvb_s[:, hs] = jnp.where(lane < HEAD, dkc, dvp).astype(MXU)
        dgqn_ref[...] += dgq
        dgkn_ref[...] += dgk
        dkr_ref[...] = dkr

        dqb = dqb_s[...]
        dwuq_ref[...] += _dot_tn(dqb, cqn)
        dcqn = _dot(dqb, wuq_ref[...])
        dgcq_ref[...] += _csum(dcqn * chat)
        dyc = dcqn * gcq_ref[...]
        dcq_ref[...] = rc * (dyc - chat * jnp.mean(dyc * chat, axis=-1, keepdims=True))

        dkvb = dkvb_s[...]
        dwukv_ref[...] += _dot_tn(dkvb, ckvn)
        dckvn = _dot(dkvb, wukv_ref[...])
        dgckv_ref[...] += _csum(dckvn * kvhat)
        dykv = dckvn * gckv_ref[...]
        dckv_ref[...] = rkv * (dykv - kvhat * jnp.mean(dykv * kvhat, axis=-1, keepdims=True))

    full = lambda shape: pl.BlockSpec(shape, lambda b, i: (0, 0))
    tile = lambda width: pl.BlockSpec((None, tm, width), lambda b, i: (b, i, 0))
    return pl.pallas_call(
        body, name=name, grid=(nb, s // tm),
        in_specs=_mla_prep_specs(s, tm) + [tile(768), tile(768), tile(384)],
        out_specs=[tile(256), tile(128), tile(128), full((768, 256)), full((768, 128)),
                   full((1, 256)), full((1, 128)), full((1, 128)), full((1, 128))],
        out_shape=[jax.ShapeDtypeStruct((nb, s, 256), F32), jax.ShapeDtypeStruct((nb, s, 128), F32),
                   jax.ShapeDtypeStruct((nb, s, 128), F32),
                   jax.ShapeDtypeStruct((768, 256), F32), jax.ShapeDtypeStruct((768, 128), F32),
                   jax.ShapeDtypeStruct((1, 256), F32), jax.ShapeDtypeStruct((1, 128), F32),
                   jax.ShapeDtypeStruct((1, 128), F32), jax.ShapeDtypeStruct((1, 128), F32)],
        scratch_shapes=[pltpu.VMEM((tm, 768), MXU), pltpu.VMEM((tm, 768), MXU)],
        compiler_params=_cp("arbitrary", "arbitrary"))(
            proj, proj, proj, cs, sn, gcq, gckv, gqn, gkn, wuq, wukv, dq, dk, dv)


def _softplus(z):
    return jnp.maximum(z, 0.0) + jnp.log(1.0 + jnp.exp(-jnp.abs(z)))


def _sb_fwd(proj, name, riding=()):
    nb, s, _ = proj.shape
    tq, tk = min(256, s), min(256, s)
    ratio = tq // tk
    na = len(riding)
    grid = (nb, 2, s // tq)

    def body(q_ref, k_ref, v_ref, o_ref, ct_ref, cnt_ref):
        i = pl.program_id(2)
        lo = _iota((tq, LANES), 1) < HEAD
        qv = q_ref[...]
        q0 = jnp.where(lo, qv, 0.0).astype(MXU)
        q1 = jnp.where(lo, 0.0, qv).astype(MXU)
        usuf = (_iota((tk, tk), 0) > _iota((tk, tk), 1)).astype(MXU)
        tpos = i * tq + _iota((tq, tk), 0)
        scol = _iota((tq, tk), 1)
        nch = (i + 1) * ratio

        def alive(st):
            return (st[0] < nch) & (st[5] > SB_DEAD)

        def step(st):
            t, c0, a0, c1, a1, _ = st
            j = nch - 1 - t
            off = pl.multiple_of(j * tk, tk)
            kc = k_ref[pl.ds(off, tk), :].astype(MXU)
            vc = v_ref[pl.ds(off, tk), :].astype(MXU)
            msk = (scol + j * tk) < tpos

            def head(qm, c, a):
                z = _dot_nt(qm, kc) * SB_SCALE
                sp = _softplus(z)
                lk = jnp.where(msk, -sp, 0.0)
                w = jnp.where(msk, jnp.exp(z - sp + _cumdot(lk, usuf) + c), 0.0)
                return c + _rsum(lk), a + _dot(w.astype(MXU), vc)

            c0, a0 = head(q0, c0, a0)
            c1, a1 = head(q1, c1, a1)
            return t + 1, c0, a0, c1, a1, jnp.maximum(jnp.max(c0), jnp.max(c1))

        z1 = jnp.zeros((tq, 1), F32)
        za = jnp.zeros((tq, LANES), F32)
        t, c0, a0, c1, a1, _ = lax.while_loop(alive, step, (jnp.int32(0), z1, za, z1, za, jnp.float32(0.0)))
        o_ref[...] = jnp.where(lo, a0, a1)
        ct_ref[...] = jnp.where(lo, c0, c1)
        cnt_ref[...] = jnp.zeros((8, LANES), F32) + t.astype(F32)

    kv = lambda col: pl.BlockSpec((None, s, LANES), lambda b, p, i: (b, 0, col // LANES + p))
    tile = pl.BlockSpec((None, tq, LANES), lambda b, p, i: (b, i, p))
    hbm = pl.BlockSpec(memory_space=pl.ANY)
    return pl.pallas_call(
        _with_gather(body, 3, 3, na, grid), name=name, grid=grid,
        in_specs=[pl.BlockSpec((None, tq, LANES), lambda b, p, i: (b, i, COL_SBQ // LANES + p)),
                  kv(COL_SBK), kv(COL_SBV)] + [hbm] * na,
        out_specs=[tile, tile, pl.BlockSpec((None, None, None, 8, LANES), lambda b, p, i: (b, p, i, 0, 0))] + [hbm] * na,
        out_shape=[jax.ShapeDtypeStruct((nb, s, 256), F32)] * 2
        + [jax.ShapeDtypeStruct((nb, 2, s // tq, 8, LANES), F32)] + _gather_out_shapes(riding),
        scratch_shapes=_gather_sems(na) if na else [],
        compiler_params=_cp("arbitrary", "arbitrary", "arbitrary"))(proj, proj, proj, *riding)


def _sb_bwd(proj, ct, cnt, do, name):
    nb, s, _ = proj.shape
    tq, tk = min(256, s), min(256, s)
    ratio = tq // tk

    def body(q_ref, k_ref, v_ref, ct_ref, cnt_ref, do_ref, dq_ref, dk_ref, dv_ref):
        i = pl.program_id(2)

        @pl.when(i == 0)
        def _():
            dk_ref[...] = jnp.zeros_like(dk_ref)
            dv_ref[...] = jnp.zeros_like(dv_ref)

        lane = _iota((tq, LANES), 1)
        lo = lane < HEAD
        lok = _iota((tk, LANES), 1) < HEAD
        qv, dov = q_ref[...], do_ref[...]
        qb, dob = qv.astype(MXU), dov.astype(MXU)
        q0 = jnp.where(lo, qv, 0.0).astype(MXU)
        q1 = jnp.where(lo, 0.0, qv).astype(MXU)
        do0 = jnp.where(lo, dov, 0.0).astype(MXU)
        do1 = jnp.where(lo, 0.0, dov).astype(MXU)
        ctv = ct_ref[...]
        ct0 = _rsum(jnp.where(lane == 0, ctv, 0.0))
        ct1 = _rsum(jnp.where(lane == LANES - 1, ctv, 0.0))
        uincl = (_iota((tk, tk), 0) <= _iota((tk, tk), 1)).astype(MXU)
        ustrict = (_iota((tk, tk), 0) < _iota((tk, tk), 1)).astype(MXU)
        tpos = i * tq + _iota((tq, tk), 0)
        scol = _iota((tq, tk), 1)
        nch = (i + 1) * ratio

        def step(j, carry):
            p0, g0, dq0, p1, g1, dq1 = carry
            off = pl.multiple_of(j * tk, tk)
            kc = k_ref[pl.ds(off, tk), :].astype(MXU)
            vc = v_ref[pl.ds(off, tk), :].astype(MXU)
            msk = (scol + j * tk) < tpos

            def head(qm, dom, ctot, pc, gc, dqa):
                z = _dot_nt(qm, kc) * SB_SCALE
                sp = _softplus(z)
                lk = jnp.where(msk, -sp, 0.0)
                lsig = z - sp
                w = jnp.where(msk, jnp.exp(lsig + (ctot - pc - _cumdot(lk, uincl))), 0.0)
                g = w * _dot_nt(dom, vc)
                gpre = gc + _cumdot(g, ustrict)
                sig = jnp.exp(lsig)
                dz = (jnp.where(msk, g * (1.0 - sig) - sig * gpre, 0.0) * SB_SCALE).astype(MXU)
                return (pc + _rsum(lk), gc + _rsum(g), dqa + _dot(dz, kc),
                        _dot_tn(dz, qb), _dot_tn(w.astype(MXU), dob))

            p0, g0, dq0, dk0, dv0 = head(q0, do0, ct0, p0, g0, dq0)
            p1, g1, dq1, dk1, dv1 = head(q1, do1, ct1, p1, g1, dq1)
            dk_ref[pl.ds(off, tk), :] += jnp.where(lok, dk0, dk1)
            dv_ref[pl.ds(off, tk), :] += jnp.where(lok, dv0, dv1)
            return p0, g0, dq0, p1, g1, dq1

        z1 = jnp.zeros((tq, 1), F32)
        za = jnp.zeros((tq, LANES), F32)
        first = nch - jnp.max(cnt_ref[...]).astype(jnp.int32)
        _, _, dq0, _, _, dq1 = lax.fori_loop(first, nch, step, (z1, z1, za, z1, z1, za))
        dq_ref[...] = jnp.where(lo, dq0, dq1)

    kv = lambda col: pl.BlockSpec((None, s, LANES), lambda b, p, i: (b, 0, col // LANES + p))
    tile = pl.BlockSpec((None, tq, LANES), lambda b, p, i: (b, i, p))
    acc = pl.BlockSpec((None, s, LANES), lambda b, p, i: (b, 0, p))
    return pl.pallas_call(
        body, name=name, grid=(nb, 2, s // tq),
        in_specs=[pl.BlockSpec((None, tq, LANES), lambda b, p, i: (b, i, COL_SBQ // LANES + p)),
                  kv(COL_SBK), kv(COL_SBV), tile,
                  pl.BlockSpec((None, None, None, 8, LANES), lambda b, p, i: (b, p, i, 0, 0)), tile],
        out_specs=[tile, acc, acc],
        out_shape=[jax.ShapeDtypeStruct((nb, s, 256), F32)] * 3,
        compiler_params=_cp("parallel", "parallel", "arbitrary"))(proj, proj, proj, ct, cnt, do)


def _mla_fwd(q, k, v, name, riding=()):
    nb, s, _ = q.shape
    tq = tk = min(256, s)
    na = len(riding)
    grid = (nb, MLA_HEADS // 2, s // tq)

    def body(q_ref, k_ref, v_ref, o_ref, lse_ref):
        i = pl.program_id(2)
        q0, q1 = q_ref[:, :LANES], q_ref[:, LANES:]
        krow = _iota((tk, tq), 0)
        qcol = _iota((tk, tq), 1)

        def step(j, carry, diagonal):
            m0, l0, a0, m1, l1, a1 = carry
            off = pl.multiple_of(j * tk, tk)
            vc = v_ref[pl.ds(off, tk), :]

            def head(qh, kh, m, l, a):
                st = _dot_nt(kh, qh) * MLA_SCALE
                if diagonal:
                    st = jnp.where(krow <= qcol, st, NEG)
                mn = jnp.maximum(m, jnp.max(st, axis=0, keepdims=True))
                al = jnp.exp(m - mn)
                pt = jnp.exp(st - mn)
                return mn, al * l + _csum(pt), al * a + _dot_tn(vc, pt.astype(MXU))

            m0, l0, a0 = head(q0, k_ref[pl.ds(off, tk), :LANES], m0, l0, a0)
            m1, l1, a1 = head(q1, k_ref[pl.ds(off, tk), LANES:], m1, l1, a1)
            return m0, l0, a0, m1, l1, a1

        mi = jnp.full((1, tq), NEG, F32)
        z1 = jnp.zeros((1, tq), F32)
        za = jnp.zeros((LANES, tq), F32)
        carry = lax.fori_loop(0, i, lambda j, cr: step(j, cr, False), (mi, z1, za, mi, z1, za))
        m0, l0, a0, m1, l1, a1 = step(i, carry, True)
        lo_rows = _iota((LANES, tq), 0) < HEAD
        o_ref[...] = jnp.where(lo_rows, a0 / l0, a1 / l1).T
        lse_ref[...] = jnp.zeros_like(lse_ref)
        lse_ref[0:1, :] = m0 + jnp.log(l0)
        lse_ref[1:2, :] = m1 + jnp.log(l1)

    tile = pl.BlockSpec((None, tq, LANES), lambda b, p, i: (b, i, p))
    hbm = pl.BlockSpec(memory_space=pl.ANY)
    return pl.pallas_call(
        _with_gather(body, 3, 2, na, grid), name=name, grid=grid,
        in_specs=[pl.BlockSpec((None, tq, 2 * LANES), lambda b, p, i: (b, i, p)),
                  pl.BlockSpec((None, s, 2 * LANES), lambda b, p, i: (b, 0, p)),
                  pl.BlockSpec((None, s, LANES), lambda b, p, i: (b, 0, p))] + [hbm] * na,
        out_specs=[tile, pl.BlockSpec((None, None, 8, tq), lambda b, p, i: (b, p, 0, i))] + [hbm] * na,
        out_shape=[jax.ShapeDtypeStruct((nb, s, 384), F32), jax.ShapeDtypeStruct((nb, MLA_HEADS // 2, 8, s), F32)]
        + _gather_out_shapes(riding),
        scratch_shapes=_gather_sems(na) if na else [],
        compiler_params=_cp("arbitrary", "arbitrary", "arbitrary"))(q, k, v, *riding)


def _mla_bwd(q, k, v, o, lse, do, name, riding=()):
    nb, s, _ = q.shape
    tq = tk = min(256, s)
    na = len(riding)
    grid = (nb, MLA_HEADS // 2, s // tq)

    def body(q_ref, k_ref, v_ref, o_ref, lse_ref, do_ref, dq_ref, dk_ref, dv_ref):
        i = pl.program_id(2)

        @pl.when(i == 0)
        def _():
            dk_ref[...] = jnp.zeros_like(dk_ref)
            dv_ref[...] = jnp.zeros_like(dv_ref)

        lo = _iota((tq, LANES), 1) < HEAD
        lok = _iota((tk, LANES), 1) < HEAD
        q0, q1 = q_ref[:, :LANES], q_ref[:, LANES:]
        dov = do_ref[...]
        dob = dov.astype(MXU)
        do0 = jnp.where(lo, dov, 0.0).astype(MXU)
        do1 = jnp.where(lo, 0.0, dov).astype(MXU)
        dd = dov * o_ref[...]
        hi = dd.astype(MXU)
        r1 = dd - hi.astype(F32)
        mid = r1.astype(MXU)
        low = (r1 - mid.astype(F32)).astype(MXU)
        sel_lane = _iota((8, LANES), 1) < HEAD
        sel0 = sel_lane.astype(MXU)
        sel1 = (~sel_lane).astype(MXU)
        dl0 = (_dot_nt(sel0, hi) + _dot_nt(sel0, mid) + _dot_nt(sel0, low))[0:1, :]
        dl1 = (_dot_nt(sel1, hi) + _dot_nt(sel1, mid) + _dot_nt(sel1, low))[0:1, :]
        ls0, ls1 = lse_ref[0:1, :], lse_ref[1:2, :]
        krow = _iota((tk, tq), 0)
        qcol = _iota((tk, tq), 1)

        def step(j, carry, diagonal):
            dq0, dq1 = carry
            off = pl.multiple_of(j * tk, tk)
            vc = v_ref[pl.ds(off, tk), :]

            def head(qh, kh, dom, ls, dl, dqa):
                st = _dot_nt(kh, qh) * MLA_SCALE
                if diagonal:
                    st = jnp.where(krow <= qcol, st, NEG)
                pt = jnp.exp(st - ls)
                dst = (pt * (_dot_nt(vc, dom) - dl) * MLA_SCALE).astype(MXU)
                return dqa + _dot_tn(kh, dst), _dot(dst, qh), _dot(pt.astype(MXU), dob)

            dq0, dk0, dv0 = head(q0, k_ref[pl.ds(off, tk), :LANES], do0, ls0, dl0, dq0)
            dq1, dk1, dv1 = head(q1, k_ref[pl.ds(off, tk), LANES:], do1, ls1, dl1, dq1)
            dk_ref[pl.ds(off, tk), :LANES] += dk0
            dk_ref[pl.ds(off, tk), LANES:] += dk1
            dv_ref[pl.ds(off, tk), :] += jnp.where(lok, dv0, dv1)
            return dq0, dq1

        za = jnp.zeros((LANES, tq), F32)
        carry = lax.fori_loop(0, i, lambda j, cr: step(j, cr, False), (za, za))
        dq0, dq1 = step(i, carry, True)
        dq_ref[:, :LANES] = dq0.T
        dq_ref[:, LANES:] = dq1.T

    tile = pl.BlockSpec((None, tq, LANES), lambda b, p, i: (b, i, p))
    tile2 = pl.BlockSpec((None, tq, 2 * LANES), lambda b, p, i: (b, i, p))
    hbm = pl.BlockSpec(memory_space=pl.ANY)
    return pl.pallas_call(
        _with_chip_exchange(body, 6, 3, na, grid), name=name, grid=grid,
        in_specs=[tile2,
                  pl.BlockSpec((None, s, 2 * LANES), lambda b, p, i: (b, 0, p)),
                  pl.BlockSpec((None, s, LANES), lambda b, p, i: (b, 0, p)),
                  tile, pl.BlockSpec((None, None, 8, tq), lambda b, p, i: (b, p, 0, i)), tile] + [hbm] * na,
        out_specs=[tile2,
                   pl.BlockSpec((None, s, 2 * LANES), lambda b, p, i: (b, 0, p)),
                   pl.BlockSpec((None, s, LANES), lambda b, p, i: (b, 0, p))] + [hbm] * na,
        out_shape=[jax.ShapeDtypeStruct((nb, s, 768), F32), jax.ShapeDtypeStruct((nb, s, 768), F32),
                   jax.ShapeDtypeStruct((nb, s, 384), F32)] + [jax.ShapeDtypeStruct(a.shape, a.dtype) for a in riding],
        scratch_shapes=_chip_exchange_sems(na) if na else [],
        compiler_params=_cp("arbitrary", "arbitrary", "arbitrary"))(q, k, v, o, lse, do, *riding)


def _half_stats(xv, lo):
    x2 = xv * xv
    s0 = _rsum(jnp.where(lo, x2, 0.0))
    s1 = _rsum(jnp.where(lo, 0.0, x2))
    return jnp.where(lo, lax.rsqrt(s0 / HEAD + EPS), lax.rsqrt(s1 / HEAD + EPS))


def _half_mean(xv, lo):
    s0 = _rsum(jnp.where(lo, xv, 0.0))
    s1 = _rsum(jnp.where(lo, 0.0, xv))
    return jnp.where(lo, s0, s1) / HEAD


def _swa_in_specs():
    def band(col, prev):
        if prev:
            return pl.BlockSpec((None, BLOCK, LANES), lambda b, n: (b, jnp.maximum(n - 1, 0), col // LANES))
        return pl.BlockSpec((None, BLOCK, LANES), lambda b, n: (b, n, col // LANES))

    full = lambda shape: pl.BlockSpec(shape, lambda b, n: tuple(0 for _ in shape))
    return [pl.BlockSpec((None, BLOCK, 384), lambda b, n: (b, n, COL_SWQ // 384)),
            band(COL_SWK, False), band(COL_SWK, True), band(COL_SWV, False), band(COL_SWV, True),
            full((1, LANES)), full((1, LANES)), full((8, LANES)), full((SW_HEADS, BLOCK, 2 * BLOCK))]


def _swa_valid(n):
    a = _iota((BLOCK, 2 * BLOCK), 0)
    bcol = _iota((BLOCK, 2 * BLOCK), 1)
    dist = BLOCK + a - bcol
    return (dist >= 0) & (dist < BLOCK) & ((n > 0) | (bcol >= BLOCK))


def _swa_fwd(proj, gq, gk, sinks, bias, name):
    nb, s, _ = proj.shape

    def body(q_ref, kc_ref, kp_ref, vc_ref, vp_ref, gq_ref, gk_ref, sk_ref, bias_ref, o_ref):
        n = pl.program_id(1)
        lo = _iota((BLOCK, LANES), 1) < HEAD
        lo2 = _iota((2 * BLOCK, LANES), 1) < HEAD
        kband = jnp.concatenate([kp_ref[...], kc_ref[...]], axis=0).astype(F32)
        vband = jnp.concatenate([vp_ref[...], vc_ref[...]], axis=0).astype(F32)
        kn = kband * _half_stats(kband, lo2) * gk_ref[...]
        ks = (kn.astype(MXU), pltpu.roll(kn, HEAD, 1).astype(MXU))
        vs = (vband.astype(MXU), pltpu.roll(vband, HEAD, 1).astype(MXU))
        valid = _swa_valid(n)
        for blk in range(SW_HEADS // 2):
            qv = q_ref[:, blk * LANES:(blk + 1) * LANES].astype(F32)
            qn = qv * _half_stats(qv, lo) * gq_ref[...]
            outs = []
            for half in range(2):
                h = 2 * blk + half
                swap = 0 if half == h // 3 else 1
                qm = jnp.where(lo if half == 0 else ~lo, qn, 0.0).astype(MXU)
                sc = jnp.where(valid, _dot_nt(qm, ks[swap]) * SW_SCALE + bias_ref[h], NEG)
                sk = jnp.max(sk_ref[h:h + 1, :], axis=-1, keepdims=True)
                m = jnp.maximum(jnp.max(sc, axis=-1, keepdims=True), sk)
                p = jnp.exp(sc - m)
                l = _rsum(p) + jnp.exp(sk - m)
                outs.append(_dot((p / l).astype(MXU), vs[swap]))
            o_ref[:, blk * LANES:(blk + 1) * LANES] = jnp.where(lo, outs[0], outs[1])

    return pl.pallas_call(
        body, name=name, grid=(nb, s // BLOCK), in_specs=_swa_in_specs(),
        out_specs=pl.BlockSpec((None, BLOCK, 384), lambda b, n: (b, n, 0)),
        out_shape=jax.ShapeDtypeStruct((nb, s, 384), F32),
        compiler_params=_cp("parallel", "parallel"))(proj, proj, proj, proj, proj, gq, gk, sinks, bias)


def _swa_bwd(proj, gq, gk, sinks, bias, do, name):
    nb, s, _ = proj.shape

    def body(q_ref, kc_ref, kp_ref, vc_ref, vp_ref, gq_ref, gk_ref, sk_ref, bias_ref, do_ref,
             dq_ref, dkc_ref, dkp_ref, dvc_ref, dvp_ref, dbias_ref, dsk_ref, dgq_ref, dgk_ref):
        n = pl.program_id(1)

        @pl.when((pl.program_id(0) == 0) & (n == 0))
        def _():
            for r in (dbias_ref, dsk_ref, dgq_ref, dgk_ref):
                r[...] = jnp.zeros_like(r)

        lo = _iota((BLOCK, LANES), 1) < HEAD
        lo2 = _iota((2 * BLOCK, LANES), 1) < HEAD
        kband = jnp.concatenate([kp_ref[...], kc_ref[...]], axis=0).astype(F32)
        vband = jnp.concatenate([vp_ref[...], vc_ref[...]], axis=0).astype(F32)
        rk = _half_stats(kband, lo2)
        khat = kband * rk
        gkv = gk_ref[...]
        kn = khat * gkv
        ks = (kn.astype(MXU), pltpu.roll(kn, HEAD, 1).astype(MXU))
        vs = (vband.astype(MXU), pltpu.roll(vband, HEAD, 1).astype(MXU))
        valid = _swa_valid(n)
        dkn = jnp.zeros((2 * BLOCK, LANES), F32)
        dvb = jnp.zeros((2 * BLOCK, LANES), F32)
        gqv = gq_ref[...]
        dgq = jnp.zeros((1, LANES), F32)
        for blk in range(SW_HEADS // 2):
            bs = slice(blk * LANES, (blk + 1) * LANES)
            qv = q_ref[:, bs].astype(F32)
            rq = _half_stats(qv, lo)
            qhat = qv * rq
            qn = qhat * gqv
            dov = do_ref[:, bs]
            dqn = jnp.zeros((BLOCK, LANES), F32)
            for half in range(2):
                h = 2 * blk + half
                swap = 0 if half == h // 3 else 1
                hm = lo if half == 0 else ~lo
                qm = jnp.where(hm, qn, 0.0).astype(MXU)
                dom = jnp.where(hm, dov, 0.0).astype(MXU)
                sc = jnp.where(valid, _dot_nt(qm, ks[swap]) * SW_SCALE + bias_ref[h], NEG)
                sk = jnp.max(sk_ref[h:h + 1, :], axis=-1, keepdims=True)
                m = jnp.maximum(jnp.max(sc, axis=-1, keepdims=True), sk)
                e = jnp.exp(sc - m)
                es = jnp.exp(sk - m)
                l = _rsum(e) + es
                p = e / l
                dp = _dot_nt(dom, vs[swap])
                delta = _rsum(p * dp)
                ds = p * (dp - delta)
                dsk_ref[h:h + 1, :] += jnp.broadcast_to(_csum(-(es / l) * delta), (1, LANES))
                dbias_ref[h] += ds
                dsb = (ds * SW_SCALE).astype(MXU)
                dqn = dqn + jnp.where(hm, _dot(dsb, ks[swap]), 0.0)
                rk_ = _dot_tn(dsb, qm)
                rv_ = _dot_tn(p.astype(MXU), dom)
                if swap:
                    rk_ = pltpu.roll(rk_, HEAD, 1)
                    rv_ = pltpu.roll(rv_, HEAD, 1)
                dkn = dkn + rk_
                dvb = dvb + rv_
            dgq = dgq + _csum(dqn * qhat)
            dyq = dqn * gqv
            dq_ref[:, bs] = rq * (dyq - qhat * _half_mean(dyq * qhat, lo))
        dgq_ref[...] += dgq
        dgk_ref[...] += _csum(dkn * khat)
        dyk = dkn * gkv
        dkb = rk * (dyk - khat * _half_mean(dyk * khat, lo2))
        dkp_ref[...] = dkb[:BLOCK]
        dkc_ref[...] = dkb[BLOCK:]
        dvp_ref[...] = dvb[:BLOCK]
        dvc_ref[...] = dvb[BLOCK:]

    full = lambda shape: pl.BlockSpec(shape, lambda b, n: tuple(0 for _ in shape))
    tile = pl.BlockSpec((None, BLOCK, LANES), lambda b, n: (b, n, 0))
    tile3 = pl.BlockSpec((None, BLOCK, 384), lambda b, n: (b, n, 0))
    kvs = jax.ShapeDtypeStruct((nb, s, LANES), F32)
    return pl.pallas_call(
        body, name=name, grid=(nb, s // BLOCK), in_specs=_swa_in_specs() + [tile3],
        out_specs=[tile3, tile, tile, tile, tile, full((SW_HEADS, BLOCK, 2 * BLOCK)), full((8, LANES)),
                   full((1, LANES)), full((1, LANES))],
        out_shape=[jax.ShapeDtypeStruct((nb, s, 384), F32), kvs, kvs, kvs, kvs,
                   jax.ShapeDtypeStruct((SW_HEADS, BLOCK, 2 * BLOCK), F32), jax.ShapeDtypeStruct((8, LANES), F32),
                   jax.ShapeDtypeStruct((1, LANES), F32), jax.ShapeDtypeStruct((1, LANES), F32)],
        compiler_params=_cp("arbitrary", "arbitrary"))(proj, proj, proj, proj, proj, gq, gk, sinks, bias, do)


def _bias_build(table, bucket, name):
    def body(tb_ref, bk_ref, o_ref):
        bk = bk_ref[...]
        tb = tb_ref[...]
        row = _iota((8, LANES), 0)
        col = _iota((8, LANES), 1)
        for h in range(SW_HEADS):
            acc = jnp.zeros((BLOCK, 2 * BLOCK), F32)
            for t in range(REL_BUCKETS):
                val = jnp.sum(jnp.where((row == h) & (col == t), tb, 0.0), keepdims=True)
                acc = jnp.where(bk == t, val, acc)
            o_ref[h] = acc

    return pl.pallas_call(
        body, name=name, out_shape=jax.ShapeDtypeStruct((SW_HEADS, BLOCK, 2 * BLOCK), F32))(table, bucket)


def _bias_grad(dbias, bucket, name):
    def body(db_ref, bk_ref, o_ref):
        bk = bk_ref[...]
        row = _iota((8, LANES), 0)
        col = _iota((8, LANES), 1)
        res = jnp.zeros((8, LANES), F32)
        for h in range(SW_HEADS):
            dbh = db_ref[h]
            for t in range(REL_BUCKETS):
                val = jnp.sum(jnp.where(bk == t, dbh, 0.0), keepdims=True)
                res = jnp.where((row == h) & (col == t), val, res)
        o_ref[...] = res

    return pl.pallas_call(body, name=name, out_shape=jax.ShapeDtypeStruct((8, LANES), F32))(dbias, bucket)


def _loss_grad(y, target, name):
    nb, s, d = y.shape
    tm = min(512, s)

    def body(y_ref, t_ref, loss_ref, dy_ref):
        @pl.when((pl.program_id(0) == 0) & (pl.program_id(1) == 0))
        def _():
            loss_ref[...] = jnp.zeros_like(loss_ref)

        e = y_ref[...] - t_ref[...]
        dy_ref[...] = e / d
        loss_ref[...] += 0.5 * jnp.sum(_rsum(e * e) / d, keepdims=True)

    tile = pl.BlockSpec((None, tm, d), lambda b, i: (b, i, 0))
    return pl.pallas_call(
        body, name=name, grid=(nb, s // tm), in_specs=[tile, tile],
        out_specs=[pl.BlockSpec((8, LANES), lambda b, i: (0, 0)), tile],
        out_shape=[jax.ShapeDtypeStruct((8, LANES), F32), jax.ShapeDtypeStruct((nb, s, d), F32)],
        compiler_params=_cp("arbitrary", "arbitrary"))(y, target)


def _adamw(parts, w, m, v, name):
    npart, r, ncol = parts.shape
    tr = _row_tile(r, ncol)
    bc1 = 1.0 - ADAM_B1 ** ADAM_STEP
    bc2 = 1.0 - ADAM_B2 ** ADAM_STEP

    def body(p_ref, w_ref, m_ref, v_ref, g_ref, d_ref, nm_ref, nv_ref):
        g = p_ref[0].astype(F32)
        for k in range(1, npart):
            g = g + p_ref[k].astype(F32)
        mn = ADAM_B1 * m_ref[...] + (1.0 - ADAM_B1) * g
        vn = ADAM_B2 * v_ref[...] + (1.0 - ADAM_B2) * (g * g)
        g_ref[...] = g
        nm_ref[...] = mn
        nv_ref[...] = vn
        d_ref[...] = -ADAM_LR * ((mn / bc1) / (jnp.sqrt(vn / bc2) + ADAM_EPS) + ADAM_WD * w_ref[...])

    tile = pl.BlockSpec((tr, ncol), lambda i: (i, 0))
    return pl.pallas_call(
        body, name=name, grid=(r // tr,),
        in_specs=[pl.BlockSpec((npart, tr, ncol), lambda i: (0, i, 0)), tile, tile, tile],
        out_specs=[tile] * 4, out_shape=[jax.ShapeDtypeStruct((r, ncol), F32)] * 4,
        compiler_params=_cp("parallel"))(parts, w, m, v)


def _unpack(flat, shapes, lead=()):
    out, off = [], 0
    for shp in shapes:
        size = 1
        for dim in shp:
            size *= dim
        out.append(flat[..., off:off + size].reshape(lead + tuple(shp)))
        off += size
    return out


def _t5_bucket():
    a = jnp.arange(BLOCK)[:, None]
    b = jnp.arange(2 * BLOCK)[None, :]
    dist = BLOCK + a - b
    max_exact = REL_BUCKETS // 2
    nn = jnp.maximum(dist, 0)
    nf = jnp.maximum(nn, 1).astype(F32)
    large = max_exact + (jnp.log(nf / max_exact) / math.log(BLOCK / max_exact)
                         * (REL_BUCKETS - max_exact)).astype(jnp.int32)
    large = jnp.minimum(large, REL_BUCKETS - 1)
    return jnp.where(nn < max_exact, nn, large).astype(jnp.int32)


def _pad_lanes(g, n):
    return jnp.pad(g, (0, n - g.shape[0])).reshape(1, n)


def kernel(x, c, positions, rel_table, norm1_g, norm2_g, w_ada, b_ada, w_in, mla_cq_g, w_uq, mla_ckv_g, w_ukv, mla_qn_g, mla_kn_g, sw_qn_g, sw_kn_g, sw_sinks, w_out, w_up, conv_w, conv_b, w_down, loss_target, m_rel_table, m_norm1_g, m_norm2_g, m_w_ada, m_b_ada, m_w_in, m_mla_cq_g, m_w_uq, m_mla_ckv_g, m_w_ukv, m_mla_qn_g, m_mla_kn_g, m_sw_qn_g, m_sw_kn_g, m_sw_sinks, m_w_out, m_w_up, m_conv_w, m_conv_b, m_w_down, v_rel_table, v_norm1_g, v_norm2_g, v_w_ada, v_b_ada, v_w_in, v_mla_cq_g, v_w_uq, v_mla_ckv_g, v_w_ukv, v_mla_qn_g, v_mla_kn_g, v_sw_qn_g, v_sw_kn_g, v_sw_sinks, v_w_out, v_w_up, v_conv_w, v_conv_b, v_w_down):
    nb, s, d = x.shape
    nl = norm1_g.shape[0]
    me = 4 * lax.axis_index("x") + 2 * lax.axis_index("y") + lax.axis_index("c")
    n_ada = w_ada.shape[2]

    shard = lambda w, l, transposed: (jnp.swapaxes(w[l], 0, 1) if transposed else w[l]).astype(MXU)
    attn_local = lambda l: [shard(w_in, l, True), shard(w_uq, l, True), shard(w_ukv, l, True), shard(w_out, l, False)]
    ffn_local = lambda l: [shard(w_up, l, True), shard(w_down, l, False)]
    full = lambda a: a.reshape(-1, a.shape[-1])
    zrows = lambda n: jnp.zeros((n, d), MXU)
    pad_in = lambda wt: jnp.concatenate([wt[:1152], wt[1184:1824], zrows(64), wt[1152:1184], zrows(160)], axis=0)
    pad_uq = lambda wt: jnp.pad(wt.reshape(MLA_HEADS, MLA_QK, 256), ((0, 0), (0, LANES - MLA_QK), (0, 0))).reshape(768, 256)
    got = _all_gather(attn_local(0) + [conv_w.reshape(-1, conv_w.shape[-1]), c], "gather_inputs")
    w_in_pt, w_uq_pt, w_ukv_t, w_out_f = [pad_in(full(got[0]))], [pad_uq(full(got[1]))], [full(got[2])], [full(got[3])]
    w_up_t, w_down_f = [], []
    conv_full = got[4].reshape(N_DEV, nl, 3, -1).transpose(1, 2, 0, 3).reshape(nl, 3, -1)
    c_all = got[5].reshape(N_DEV * nb, d)

    b_my = lax.dynamic_slice_in_dim(b_ada, me * n_ada, n_ada, axis=1).reshape(nl, 1, n_ada)
    mods_my = _ada_fwd(c_all, w_ada, b_my, "ada_fwd")
    mods, = _all_gather([mods_my.reshape(nl * N_DEV * nb, n_ada)], "gather_mods")
    mods = mods.reshape(N_DEV, nl, N_DEV * nb, n_ada).transpose(1, 2, 0, 3).reshape(nl, N_DEV * nb, N_DEV * n_ada)
    mods = lax.dynamic_slice_in_dim(mods, me * nb, nb, axis=1)
    shift1, scale1, gate1, shift2, scale2, gate2 = [mods[:, :, k * d:(k + 1) * d].reshape(nl, nb, 1, d) for k in range(6)]

    half = 16
    inv_freq = jnp.power(ROPE_THETA, -jnp.arange(half, dtype=F32) / half)
    ang = positions.astype(F32)[..., None] * inv_freq
    ones = lambda n: jnp.ones((nb, s, n), F32)
    zeros = lambda n: jnp.zeros((nb, s, n), F32)
    rope_c = jnp.concatenate([ones(64), jnp.cos(ang), jnp.cos(ang), ones(32)], axis=-1)
    rope_s = jnp.concatenate([zeros(64), jnp.sin(ang), jnp.sin(ang), zeros(32)], axis=-1)
    bucket = _t5_bucket()
    bias = _bias_build(jnp.pad(rel_table.T, ((0, 8 - SW_HEADS), (0, LANES - REL_BUCKETS))), bucket, "rel_bias")

    row = lambda g: g.reshape(1, -1)
    twice = lambda g: jnp.concatenate([g, g]).reshape(1, LANES)

    saved = []
    xl = x
    for l in range(nl):
        proj, h1 = _ln_mod_matmul(xl, row(norm1_g[l]), scale1[l], shift1[l], w_in_pt[l], f"l{l}_in_proj")
        prep_args = (proj, rope_c, rope_s, row(mla_cq_g[l]), row(mla_ckv_g[l]), _pad_lanes(mla_qn_g[l], LANES),
                     _pad_lanes(mla_kn_g[l], LANES), w_uq_pt[l], w_ukv_t[l])
        qm, km, vm = _mla_prep(*prep_args, f"l{l}_mla_prep")
        o_a, ct_a, cnt_a, up_g, down_g = _sb_fwd(proj, f"l{l}_sb_fwd", riding=ffn_local(l))
        w_up_t.append(full(up_g))
        w_down_f.append(full(down_g))
        o_b, lse_b, *nxt = _mla_fwd(qm, km, vm, f"l{l}_mla_fwd", riding=attn_local(l + 1) if l + 1 < nl else ())
        if nxt:
            w_in_pt.append(pad_in(full(nxt[0])))
            w_uq_pt.append(pad_uq(full(nxt[1])))
            w_ukv_t.append(full(nxt[2]))
            w_out_f.append(full(nxt[3]))
        sinks = jnp.broadcast_to(jnp.pad(sw_sinks[l], (0, 2))[:, None], (8, LANES))
        swa_args = (proj, twice(sw_qn_g[l]), twice(sw_kn_g[l]), sinks, bias)
        o_c = _swa_fwd(*swa_args, f"l{l}_swa_fwd")
        wo = [w_out_f[l][:256], w_out_f[l][256:640], w_out_f[l][640:]]
        x_mid, y1 = _out_proj([o_a, o_b, o_c], wo, gate1[l], xl, f"l{l}_out_proj")
        u_pre, h2 = _ln_mod_matmul(x_mid, row(norm2_g[l]), scale2[l], shift2[l], w_up_t[l], f"l{l}_up_proj")
        x_out, y2 = _conv_gate_matmul(u_pre, conv_full[l], row(conv_b[l]), w_down_f[l], gate2[l], x_mid, f"l{l}_ffn_down")
        saved.append(dict(x=xl, proj=proj, h1=h1, prep=prep_args, qkv=(qm, km, vm), o_a=o_a, ct_a=ct_a, cnt_a=cnt_a, o_b=o_b, lse_b=lse_b,
                          swa=swa_args, o_c=o_c, wo=wo, y1=y1, x_mid=x_mid, u_pre=u_pre, h2=h2, y2=y2))
        xl = x_out

    loss_blk, dx = _loss_grad(xl, loss_target, "loss")
    loss = lax.psum(loss_blk[0, 0], ("x", "y", "c"))

    t = nb * s
    flat = lambda a: a.reshape(t, a.shape[-1])
    grads = [None] * nl
    dmods = [None] * nl
    sharded_out = [None] * nl
    sharded_names = ["w_in", "w_uq", "w_ukv", "w_up", "w_out", "w_down", "conv_w"]
    sharded_wmv = dict(w_in=(w_in, m_w_in, v_w_in), w_uq=(w_uq, m_w_uq, v_w_uq), w_ukv=(w_ukv, m_w_ukv, v_w_ukv),
                       w_up=(w_up, m_w_up, v_w_up), w_out=(w_out, m_w_out, v_w_out), w_down=(w_down, m_w_down, v_w_down),
                       conv_w=(conv_w, m_conv_w, v_conv_w))
    n_in, n_up, n_out, n_dn = w_in.shape[2], w_up.shape[2], w_out.shape[1], w_down.shape[1]
    small_sizes = [w_uq[0].size, w_ukv[0].size, conv_w[0].size]
    n_small_rows = -(-sum(small_sizes) // d)
    rows_used = n_in + n_out + n_small_rows
    rows_grad = -(-rows_used // 16) * 16

    def pack_rows(mats, vecs):
        lead = mats[0].shape[:-2]
        flat_part = jnp.concatenate(vecs, axis=-1)
        flat_part = jnp.pad(flat_part, [(0, 0)] * len(lead) + [(0, n_small_rows * d - flat_part.shape[-1])])
        tail = jnp.zeros(lead + (rows_grad - rows_used, d), F32)
        return jnp.concatenate(list(mats) + [flat_part.reshape(lead + (n_small_rows, d)), tail], axis=-2)

    def unpack_rows(a):
        o1, o2 = n_in, n_in + n_out
        flat_part = a[o2:o2 + n_small_rows].reshape(-1)
        s1, s2, s3 = small_sizes[0], small_sizes[0] + small_sizes[1], sum(small_sizes)
        return dict(w_in=a[:o1].T, w_out=a[o1:o2],
                    w_uq=flat_part[:s1].reshape(w_uq.shape[2], -1).T, w_ukv=flat_part[s1:s2].reshape(w_ukv.shape[2], -1).T,
                    conv_w=flat_part[s2:s3].reshape(conv_w.shape[1:]))

    ffn_out = [None] * nl
    core = lax.axis_index("c").reshape(1).astype(jnp.int32)

    def update_ffn(l, recv):
        wmv = [{k: v[o][l] for k, v in sharded_wmv.items()} for o in range(3)]
        res_up = _adamw(recv[0], *[a["w_up"].T for a in wmv], f"l{l}_adamw_up")
        res_dn = _adamw(recv[1], *[a["w_down"] for a in wmv], f"l{l}_adamw_down")
        ffn_out[l] = [dict(w_up=ru.T, w_down=rd) for ru, rd in zip(res_up, res_dn)]

    def update_rest(l, recv):
        wmv = [{k: v[o][l] for k, v in sharded_wmv.items()} for o in range(3)]
        res_rest = _adamw(recv, *[pack_rows([a["w_in"].T, a["w_out"]], [a["w_uq"].T.reshape(-1), a["w_ukv"].T.reshape(-1),
                                                                         a["conv_w"].reshape(-1)]) for a in wmv],
                          f"l{l}_adamw_rest")
        sharded_out[l] = [dict(unpack_rows(rr), **ff) for rr, ff in zip(res_rest, ffn_out[l])]

    pending = None
    dbias = jnp.zeros((SW_HEADS, BLOCK, 2 * BLOCK), F32)
    for l in reversed(range(nl)):
        sv = saved[l]
        (da,), dy2, dgate2 = _gate_bwd_nt(dx, sv["y2"], gate2[l], [w_down_f[l]], f"l{l}_ffn_down_bwd")
        du, a_act, cstats = _conv_gate_bwd(da, sv["u_pre"], conv_full[l], row(conv_b[l]), f"l{l}_conv_gate_bwd")
        res = _ln_mod_matmul_bwd(du, w_up_t[l], sv["x_mid"], row(norm2_g[l]), scale2[l], dx, conv_full[l],
                                 f"l{l}_up_proj_bwd", riding=[pending[1]] if pending else ())
        dx_mid, du_pre, dshift2, dscale2, dg2 = res[:5]
        if pending:
            update_rest(pending[0], res[5])
            pending = None
        g_w_down = _wgrad(flat(a_act), flat(dy2), f"l{l}_w_down_grad")
        g_w_up_t = _wgrad(flat(du_pre), flat(sv["h2"]), f"l{l}_w_up_grad")
        per_dev = lambda g: g.reshape(N_DEV, -1, d)
        ffn_send = [per_dev(g_w_up_t), per_dev(g_w_down)]
        ffn_sib = _pair_exchange(ffn_send, f"l{l}_pair_exchange_ffn")
        ffn_pair = [_pair_add(core, a, b, f"l{l}_pair_add_{k}") for a, b, k in zip(ffn_send, ffn_sib, ("up", "down"))]

        (do_a, do_b, do_c), dy1, dgate1 = _gate_bwd_nt(dx_mid, sv["y1"], gate1[l], sv["wo"], f"l{l}_out_proj_bwd")
        mix = jnp.concatenate([sv["o_a"], sv["o_b"], sv["o_c"]], axis=-1).astype(MXU)
        g_w_out = _wgrad(flat(mix), flat(dy1), f"l{l}_w_out_grad")

        dsb_q, dsb_k, dsb_v = _sb_bwd(sv["proj"], sv["ct_a"], sv["cnt_a"], do_a, f"l{l}_sb_bwd")
        qm, km, vm = sv["qkv"]
        dqm, dkm, dvm, *ffn_recv = _mla_bwd(qm, km, vm, sv["o_b"], sv["lse_b"], do_b, f"l{l}_mla_bwd", riding=ffn_pair)
        update_ffn(l, ffn_recv)
        dsw_q, dkc, dkp, dvc, dvp, dbias_l, dsinks, dg_swq, dg_swk = _swa_bwd(*sv["swa"], do_c, f"l{l}_swa_bwd")
        dbias = dbias + dbias_l
        shift_up = lambda a: jnp.concatenate([a[:, BLOCK:], jnp.zeros((nb, BLOCK, LANES), F32)], axis=1)
        dsw_k = dkc + shift_up(dkp)
        dsw_v = dvc + shift_up(dvp)
        dcq, dckv, dkr, g_w_uq_pt, g_w_ukv_t, dg_cq, dg_ckv, dg_qn, dg_kn = _mla_prep_bwd(
            *sv["prep"], dqm, dkm, dvm, f"l{l}_mla_prep_bwd")
        dproj = jnp.concatenate([dsb_q, dsb_k, dsb_v, dcq, dckv, dsw_q, dsw_k, dsw_v, dkr, zeros(128)], axis=-1)
        dx, dproj_m, dshift1, dscale1, dg1 = _ln_mod_matmul_bwd(
            dproj, w_in_pt[l], sv["x"], row(norm1_g[l]), scale1[l], dx_mid, None, f"l{l}_in_proj_bwd")
        g_w_in_pt = _wgrad(flat(dproj_m), flat(sv["h1"]), f"l{l}_w_in_grad")

        g_w_in_t = jnp.concatenate([g_w_in_pt[:1152], g_w_in_pt[1856:1888], g_w_in_pt[1152:1792]], axis=0)
        g_w_uq_t = g_w_uq_pt.reshape(MLA_HEADS, LANES, 256)[:, :MLA_QK].reshape(MLA_HEADS * MLA_QK, 256)
        dmods[l] = jnp.concatenate([dshift1, dscale1, dgate1, dshift2, dscale2, dgate2], axis=-1).reshape(nb, 6 * d)

        conv_dev = cstats[1:4].reshape(3, N_DEV, -1).transpose(1, 0, 2)
        rest = pack_rows([per_dev(g_w_in_t), per_dev(g_w_out)],
                         [g_w_uq_t.reshape(N_DEV, -1), g_w_ukv_t.reshape(N_DEV, -1), conv_dev.reshape(N_DEV, -1)])
        rest_sib, = _pair_exchange([rest], f"l{l}_pair_exchange_rest")
        rest_pair = _pair_add(core, rest, rest_sib, f"l{l}_pair_add_rest")
        if l > 0:
            pending = (l, rest_pair)
        else:
            update_rest(l, _chip_exchange([rest_pair], f"l{l}_chip_exchange")[0])
        grads[l] = dict(
            norm1_g=dg1[0], norm2_g=dg2[0], mla_cq_g=dg_cq[0], mla_ckv_g=dg_ckv[0], mla_qn_g=dg_qn[0, :MLA_QK],
            mla_kn_g=dg_kn[0, :MLA_QK], sw_qn_g=dg_swq[0, :HEAD] + dg_swq[0, HEAD:], sw_kn_g=dg_swk[0, :HEAD] + dg_swk[0, HEAD:],
            sw_sinks=dsinks[:SW_HEADS, 0], conv_b=cstats[0])
    grad_x = dx
    g_rel = _bias_grad(dbias, bucket, "rel_table_grad")[:SW_HEADS, :REL_BUCKETS].T
    stack = lambda k: jnp.stack([grads[l][k] for l in range(nl)])

    dm_all, = _all_gather([jnp.stack(dmods).reshape(nl * nb, 6 * d)], "gather_dmods")
    dm_all = dm_all.reshape(N_DEV, nl, nb, 6 * d).transpose(1, 0, 2, 3).reshape(nl, N_DEV * nb, 6 * d)
    dm_my = lax.dynamic_slice_in_dim(dm_all, me * n_ada, n_ada, axis=2)
    g_w_ada, g_b_ada = _ada_bwd(c_all, dm_my, dm_all, "ada_bwd")
    g_b_ada = g_b_ada.reshape(nl, 6 * d)

    big_out = [{k: jnp.stack([sharded_out[l][o][k] for l in range(nl)]) for k in sharded_names} for o in range(4)]
    packf = lambda dct, names, rows: jnp.pad(jnp.concatenate([dct[k].reshape(-1) for k in names]),
                                             (0, rows * LANES - sum(dct[k].size for k in names))).reshape(rows, LANES)

    small_names = ["rel_table", "norm1_g", "norm2_g", "mla_cq_g", "mla_ckv_g", "mla_qn_g", "mla_kn_g",
                   "sw_qn_g", "sw_kn_g", "sw_sinks", "conv_b"]
    small_w = dict(rel_table=rel_table, norm1_g=norm1_g, norm2_g=norm2_g, mla_cq_g=mla_cq_g, mla_ckv_g=mla_ckv_g,
                   mla_qn_g=mla_qn_g, mla_kn_g=mla_kn_g, sw_qn_g=sw_qn_g, sw_kn_g=sw_kn_g, sw_sinks=sw_sinks, conv_b=conv_b)
    small_m = dict(rel_table=m_rel_table, norm1_g=m_norm1_g, norm2_g=m_norm2_g, mla_cq_g=m_mla_cq_g, mla_ckv_g=m_mla_ckv_g,
                   mla_qn_g=m_mla_qn_g, mla_kn_g=m_mla_kn_g, sw_qn_g=m_sw_qn_g, sw_kn_g=m_sw_kn_g, sw_sinks=m_sw_sinks, conv_b=m_conv_b)
    small_v = dict(rel_table=v_rel_table, norm1_g=v_norm1_g, norm2_g=v_norm2_g, mla_cq_g=v_mla_cq_g, mla_ckv_g=v_mla_ckv_g,
                   mla_qn_g=v_mla_qn_g, mla_kn_g=v_mla_kn_g, sw_qn_g=v_sw_qn_g, sw_kn_g=v_sw_kn_g, sw_sinks=v_sw_sinks, conv_b=v_conv_b)
    small_g = {k: (g_rel if k == "rel_table" else stack(k)) for k in small_names}
    n_small = sum(small_w[k].size for k in small_names)
    rows_small = -(-n_small // (8 * LANES)) * 8
    small_parts, = _all_gather([packf(small_g, small_names, rows_small)], "gather_small_grads")
    small_out = _adamw(small_parts, packf(small_w, small_names, rows_small), packf(small_m, small_names, rows_small),
                       packf(small_v, small_names, rows_small), "adamw_replicated")
    small_out = [dict(zip(small_names, _unpack(o.reshape(-1), [small_w[k].shape for k in small_names]))) for o in small_out]

    two_d = lambda a: a.reshape(-1, a.shape[-1])
    res_w = _adamw(two_d(g_w_ada)[None], two_d(w_ada), two_d(m_w_ada), two_d(v_w_ada), "adamw_w_ada")
    res_b = _adamw(g_b_ada[None], b_ada, m_b_ada, v_b_ada, "adamw_b_ada")
    ada_out = [dict(w_ada=rw.reshape(w_ada.shape), b_ada=rb) for rw, rb in zip(res_w, res_b)]

    order = ["rel_table", "norm1_g", "norm2_g", "w_ada", "b_ada", "w_in", "mla_cq_g", "w_uq", "mla_ckv_g", "w_ukv",
             "mla_qn_g", "mla_kn_g", "sw_qn_g", "sw_kn_g", "sw_sinks", "w_out", "w_up", "conv_w", "conv_b", "w_down"]
    outs = [{**big_out[k], **small_out[k], **ada_out[k]} for k in range(4)]
    return (loss, grad_x, *[outs[0][n] for n in order], *[outs[1][n] for n in order],
            *[outs[2][n] for n in order], *[outs[3][n] for n in order])
```

```python
import math

import jax
import jax.numpy as jnp
from jax import lax
from jax.experimental import pallas as pl
from jax.experimental.pallas import tpu as pltpu

F32 = jnp.float32
MXU = jnp.bfloat16
EPS = 1e-6
NEG = -1e30
VMEM_LIMIT_BYTES = 56 * 1024 * 1024
N_DEV = 8
MESH = pl.DeviceIdType.MESH

D_MODEL = 1024
D_FF = 2816
HEAD = 64
LANES = 128
MLA_HEADS = 6
MLA_QK = 96
SW_HEADS = 6
REL_BUCKETS = 32
BLOCK = 128
SB_SCALE = HEAD ** -0.5
SB_DEAD = -105.0
SW_SCALE = HEAD ** -0.5
MLA_SCALE = MLA_QK ** -0.5
ROPE_THETA = 10000.0
D_IN_PAD = 2048
COL_SBQ, COL_SBK, COL_SBV, COL_CQ, COL_CKV, COL_SWQ, COL_SWK, COL_SWV, COL_KR = 0, 256, 512, 768, 1024, 1152, 1536, 1664, 1792

HALO = 16
ROW_TILE_BYTES = 1 << 20
ADAM_LR, ADAM_B1, ADAM_B2, ADAM_EPS, ADAM_WD, ADAM_STEP = 0.001, 0.9, 0.999, 1e-08, 0.01, 10


def _cp(*sem):
    return pltpu.CompilerParams(dimension_semantics=sem, vmem_limit_bytes=VMEM_LIMIT_BYTES)


def _iota(shape, dim):
    return lax.broadcasted_iota(jnp.int32, shape, dim)


def _dot(a, b):
    return jnp.dot(a, b, preferred_element_type=F32)


def _dot_nt(a, b):
    return lax.dot_general(a, b, (((1,), (1,)), ((), ())), preferred_element_type=F32)


def _dot_tn(a, b):
    return lax.dot_general(a, b, (((0,), (0,)), ((), ())), preferred_element_type=F32)


def _cumdot(x, u):
    hi = x.astype(MXU)
    mid = (x - hi.astype(F32)).astype(MXU)
    return _dot(hi, u) + _dot(mid, u)


def _sigmoid(x):
    return 1.0 / (1.0 + jnp.exp(-x))


def _rsum(x):
    return jnp.sum(x, axis=-1, keepdims=True)


def _csum(x):
    return jnp.sum(x, axis=0, keepdims=True)


def _all_gather(xs, name):
    na = len(xs)

    def body(*refs):
        start, finish = _gather_steps(refs[:na], refs[na:2 * na], *refs[2 * na:])
        start()
        finish()

    hbm = pl.BlockSpec(memory_space=pl.ANY)
    return pl.pallas_call(
        body, name=name, out_shape=_gather_out_shapes(xs), in_specs=[hbm] * na, out_specs=[hbm] * na,
        scratch_shapes=_gather_sems(na))(*xs)


def _gather_out_shapes(xs):
    return [jax.ShapeDtypeStruct((N_DEV,) + a.shape, a.dtype) for a in xs]


def _gather_sems(na):
    return [pltpu.SemaphoreType.DMA((7 * na,)), pltpu.SemaphoreType.DMA((7 * na,)), pltpu.SemaphoreType.DMA((na,))]


def _gather_steps(x_refs, out_refs, send_sems, recv_sems, local_sems):
    na = len(x_refs)
    x, y, c = lax.axis_index("x"), lax.axis_index("y"), lax.axis_index("c")
    me, sibling = (x, y, c), (x, y, 1 - c)
    chips = [(1 - x, y), (x, 1 - y), (1 - x, 1 - y)]

    def slot(a, px, py, pc):
        return out_refs[a].at[4 * px + 2 * py + pc]

    def copy(a, k, block, to, src=None):
        return pltpu.make_async_remote_copy(
            src_ref=slot(a, *block) if src is None else src, dst_ref=slot(a, *block),
            send_sem=send_sems.at[7 * a + k], recv_sem=recv_sems.at[7 * a + k], device_id=to, device_id_type=MESH)

    def own_copies(a):
        return ([copy(a, 0, me, sibling, src=x_refs[a])]
                + [copy(a, 1 + j, me, (*chip, c), src=x_refs[a]) for j, chip in enumerate(chips)])

    def local_copy(a):
        return pltpu.make_async_copy(x_refs[a], slot(a, *me), local_sems.at[a])

    def start():
        for a in range(na):
            local_copy(a).start()
            for cp in own_copies(a):
                cp.start()

    def finish():
        passed = []
        for j, chip in enumerate(chips):
            for a in range(na):
                copy(a, 1 + j, (*chip, c), me).wait_recv()
                passed.append(copy(a, 4 + j, (*chip, c), sibling))
                passed[-1].start()
        for a in range(na):
            copy(a, 0, sibling, me).wait_recv()
            for j, chip in enumerate(chips):
                copy(a, 4 + j, (*chip, 1 - c), me).wait_recv()
        for a in range(na):
            for cp in own_copies(a):
                cp.wait_send()
        for cp in passed:
            cp.wait_send()
        for a in range(na):
            local_copy(a).wait()

    return start, finish


def _with_gather(body, n_in, n_out, na, grid):
    if not na:
        return body

    def wrapped(*refs):
        ins, ride_in = refs[:n_in], refs[n_in:n_in + na]
        outs = refs[n_in + na:n_in + na + n_out]
        ride_out = refs[n_in + na + n_out:n_in + 2 * na + n_out]
        ids = [pl.program_id(k) for k in range(len(grid))]
        first, last = ids[0] == 0, ids[0] == grid[0] - 1
        for k in range(1, len(grid)):
            first, last = first & (ids[k] == 0), last & (ids[k] == grid[k] - 1)
        start, finish = _gather_steps(ride_in, ride_out, *refs[n_in + 2 * na + n_out:])
        pl.when(first)(start)
        body(*ins, *outs)
        pl.when(last)(finish)

    return wrapped


def _pair_exchange(xs, name):
    na = len(xs)

    def body(*refs):
        x_refs, out_refs = refs[:na], refs[na:2 * na]
        send_sems, recv_sems = refs[2 * na:]
        x, y, c = lax.axis_index("x"), lax.axis_index("y"), lax.axis_index("c")
        copies = []
        for a in range(na):
            for q in range(4):
                copies.append(pltpu.make_async_remote_copy(
                    src_ref=x_refs[a].at[2 * q + 1 - c], dst_ref=out_refs[a].at[q],
                    send_sem=send_sems.at[4 * a + q], recv_sem=recv_sems.at[4 * a + q],
                    device_id=(x, y, 1 - c), device_id_type=MESH))
                copies[-1].start()
        for cp in copies:
            cp.wait()

    hbm = pl.BlockSpec(memory_space=pl.ANY)
    return pl.pallas_call(
        body, name=name, out_shape=[jax.ShapeDtypeStruct((4,) + a.shape[1:], a.dtype) for a in xs],
        in_specs=[hbm] * na, out_specs=[hbm] * na,
        scratch_shapes=[pltpu.SemaphoreType.DMA((4 * na,)), pltpu.SemaphoreType.DMA((4 * na,))])(*xs)


def _row_tile(r, ncol):
    if r * ncol * 4 <= ROW_TILE_BYTES:
        return r
    return max(t for t in range(16, r, 16) if r % t == 0 and t * ncol * 4 <= ROW_TILE_BYTES)


def _pair_add(core, xs, sib, name):
    _, r, ncol = xs.shape
    tr = _row_tile(r, ncol)

    def body(c_ref, x_ref, s_ref, o_ref):
        o_ref[...] = (x_ref[...] + s_ref[...]).astype(MXU)

    return pl.pallas_call(
        body, name=name,
        grid_spec=pltpu.PrefetchScalarGridSpec(
            num_scalar_prefetch=1, grid=(4, r // tr),
            in_specs=[pl.BlockSpec((None, tr, ncol), lambda q, i, c_ref: (2 * q + c_ref[0], i, 0)),
                      pl.BlockSpec((None, tr, ncol), lambda q, i, c_ref: (q, i, 0))],
            out_specs=pl.BlockSpec((None, tr, ncol), lambda q, i, c_ref: (q, i, 0))),
        out_shape=jax.ShapeDtypeStruct((4, r, ncol), MXU),
        compiler_params=_cp("parallel", "parallel"))(core, xs, sib)


def _chip_exchange(xs, name):
    na = len(xs)

    def body(*refs):
        copies = _chip_exchange_copies(refs[:na], refs[na:2 * na], *refs[2 * na:])
        for cp in copies:
            cp.start()
        for cp in copies:
            cp.wait()

    hbm = pl.BlockSpec(memory_space=pl.ANY)
    return pl.pallas_call(
        body, name=name, out_shape=[jax.ShapeDtypeStruct(a.shape, a.dtype) for a in xs],
        in_specs=[hbm] * na, out_specs=[hbm] * na, scratch_shapes=_chip_exchange_sems(na))(*xs)


def _with_chip_exchange(body, n_in, n_out, na, grid):
    if not na:
        return body

    def wrapped(*refs):
        ins, ride_in = refs[:n_in], refs[n_in:n_in + na]
        outs = refs[n_in + na:n_in + na + n_out]
        ride_out = refs[n_in + na + n_out:n_in + 2 * na + n_out]
        ids = [pl.program_id(k) for k in range(len(grid))]
        first, last = ids[0] == 0, ids[0] == grid[0] - 1
        for k in range(1, len(grid)):
            first, last = first & (ids[k] == 0), last & (ids[k] == grid[k] - 1)
        copies = _chip_exchange_copies(ride_in, ride_out, *refs[n_in + 2 * na + n_out:])

        @pl.when(first)
        def _():
            for cp in copies:
                cp.start()

        body(*ins, *outs)

        @pl.when(last)
        def _():
            for cp in copies:
                cp.wait()

    return wrapped


def _chip_exchange_sems(na):
    return [pltpu.SemaphoreType.DMA((3 * na,)), pltpu.SemaphoreType.DMA((3 * na,)), pltpu.SemaphoreType.DMA((na,))]


def _chip_exchange_copies(x_refs, out_refs, send_sems, recv_sems, local_sems):
    x, y, c = lax.axis_index("x"), lax.axis_index("y"), lax.axis_index("c")
    me = 2 * x + y
    copies = [pltpu.make_async_copy(x_refs[a].at[me], out_refs[a].at[me], local_sems.at[a]) for a in range(len(x_refs))]
    for k, (dx, dy) in enumerate([(1, 0), (0, 1), (1, 1)]):
        px = 1 - x if dx else x
        py = 1 - y if dy else y
        for a in range(len(x_refs)):
            copies.append(pltpu.make_async_remote_copy(
                src_ref=x_refs[a].at[2 * px + py], dst_ref=out_refs[a].at[me],
                send_sem=send_sems.at[3 * a + k], recv_sem=recv_sems.at[3 * a + k],
                device_id=(px, py, c), device_id_type=MESH))
    return copies


def _ada_fwd(c_all, w_ada, b_my, name):
    nl, d, n = w_ada.shape
    nb = c_all.shape[0]

    def body(c_ref, w_ref, b_ref, o_ref):
        cv = c_ref[...]
        sc = (cv * _sigmoid(cv)).astype(MXU)
        o_ref[...] = _dot(sc, w_ref[...].astype(MXU)) + b_ref[...]

    return pl.pallas_call(
        body, name=name, grid=(nl,),
        in_specs=[pl.BlockSpec((nb, d), lambda l: (0, 0)),
                  pl.BlockSpec((None, d, n), lambda l: (l, 0, 0)),
                  pl.BlockSpec((None, 1, n), lambda l: (l, 0, 0))],
        out_specs=pl.BlockSpec((None, nb, n), lambda l: (l, 0, 0)),
        out_shape=jax.ShapeDtypeStruct((nl, nb, n), F32),
        compiler_params=_cp("parallel"))(c_all, w_ada, b_my)


def _ada_bwd(c_all, dmods_my, dmods_all, name):
    nl, nb, n = dmods_my.shape
    d = c_all.shape[1]
    nfull = dmods_all.shape[2]

    def body(c_ref, dm_ref, da_ref, dw_ref, db_ref):
        cv = c_ref[...]
        sc = (cv * _sigmoid(cv)).astype(MXU)
        dw_ref[...] = _dot_tn(sc, dm_ref[...].astype(MXU))
        db_ref[...] = _csum(da_ref[...])

    return pl.pallas_call(
        body, name=name, grid=(nl,),
        in_specs=[pl.BlockSpec((nb, d), lambda l: (0, 0)),
                  pl.BlockSpec((None, nb, n), lambda l: (l, 0, 0)),
                  pl.BlockSpec((None, nb, nfull), lambda l: (l, 0, 0))],
        out_specs=[pl.BlockSpec((None, d, n), lambda l: (l, 0, 0)),
                   pl.BlockSpec((None, 1, nfull), lambda l: (l, 0, 0))],
        out_shape=[jax.ShapeDtypeStruct((nl, d, n), F32), jax.ShapeDtypeStruct((nl, 1, nfull), F32)],
        compiler_params=_cp("parallel"))(c_all, dmods_my, dmods_all)


def _ln_mod_matmul(x, g, scale, shift, w, name):
    nb, s, d = x.shape
    n = w.shape[0]
    tm, tn = min(1024, s), (1408 if n % 1408 == 0 else 1024)

    def body(x_ref, g_ref, sc_ref, sh_ref, w_ref, y_ref, h_ref, h_s):
        @pl.when(pl.program_id(2) == 0)
        def _():
            xf = x_ref[...]
            rstd = lax.rsqrt(jnp.mean(xf * xf, axis=-1, keepdims=True) + EPS)
            hv = (xf * rstd * g_ref[...]) * (1.0 + sc_ref[...]) + sh_ref[...]
            h_s[...] = hv.astype(MXU)
            h_ref[...] = h_s[...]

        y_ref[...] = _dot_nt(h_s[...], w_ref[...]).astype(MXU)

    return pl.pallas_call(
        body, name=name, grid=(nb, s // tm, n // tn),
        in_specs=[pl.BlockSpec((None, tm, d), lambda b, i, j: (b, i, 0)),
                  pl.BlockSpec((1, d), lambda b, i, j: (0, 0)),
                  pl.BlockSpec((None, 1, d), lambda b, i, j: (b, 0, 0)),
                  pl.BlockSpec((None, 1, d), lambda b, i, j: (b, 0, 0)),
                  pl.BlockSpec((tn, d), lambda b, i, j: (j, 0))],
        out_specs=[pl.BlockSpec((None, tm, tn), lambda b, i, j: (b, i, j)),
                   pl.BlockSpec((None, tm, d), lambda b, i, j: (b, i, 0))],
        out_shape=[jax.ShapeDtypeStruct((nb, s, n), MXU), jax.ShapeDtypeStruct((nb, s, d), MXU)],
        scratch_shapes=[pltpu.VMEM((tm, d), MXU)],
        compiler_params=_cp("parallel", "parallel", "arbitrary"))(x, g, scale, shift, w)


def _ln_mod_matmul_bwd(dy, w, x, g, scale, dres, conv_w, name, riding=()):
    nb, s, n = dy.shape
    d = x.shape[-1]
    tm, tn = min(512, s), (1408 if n % 1408 == 0 else 1024)
    ni, nj = s // tm, n // tn
    hb = tm // HALO
    conv = conv_w is not None
    na = len(riding)

    def body(*refs):
        if conv:
            dy_ref, nx_ref, cw_ref = refs[:3]
            refs = refs[3:]
        else:
            dy_ref = refs[0]
            refs = refs[1:]
        w_ref, x_ref, g_ref, sc_ref, dr_ref = refs[:5]
        ride_in, refs = refs[5:5 + na], refs[5 + na:]
        dx_ref, dyp_ref, dsh_ref, dsc_ref, dg_ref = refs[:5]
        ride_out, refs = refs[5:5 + na], refs[5 + na:]
        acc = refs[0]
        b, i, j = pl.program_id(0), pl.program_id(1), pl.program_id(2)
        if na:
            copies = _chip_exchange_copies(ride_in, ride_out, *refs[1:])

            @pl.when((b == 0) & (i == 0) & (j == 0))
            def _():
                for cp in copies:
                    cp.start()

        @pl.when(j == 0)
        def _():
            acc[...] = jnp.zeros_like(acc)

        @pl.when((j == 0) & (i == 0))
        def _():
            dsh_ref[...] = jnp.zeros_like(dsh_ref)
            dsc_ref[...] = jnp.zeros_like(dsc_ref)

        @pl.when((j == 0) & (i == 0) & (b == 0))
        def _():
            dg_ref[...] = jnp.zeros_like(dg_ref)

        dv = dy_ref[...].astype(F32)
        if conv:
            rows = _iota((tm, 1), 0)
            nx = jnp.where(i == ni - 1, 0.0, nx_ref[...].astype(F32))
            n1 = jnp.where(rows == tm - 1, nx[0:1, :], pltpu.roll(dv, tm - 1, 0))
            n2 = jnp.where(rows == tm - 2, nx[0:1, :], jnp.where(rows == tm - 1, nx[1:2, :], pltpu.roll(dv, tm - 2, 0)))
            cw = cw_ref[...]
            dv = cw[2:3, :] * dv + cw[1:2, :] * n1 + cw[0:1, :] * n2
        dp = dv.astype(MXU)
        dyp_ref[...] = dp
        acc[...] += _dot(dp, w_ref[...])

        @pl.when(j == nj - 1)
        def _():
            dh = acc[...]
            xf = x_ref[...]
            rstd = lax.rsqrt(jnp.mean(xf * xf, axis=-1, keepdims=True) + EPS)
            xn = xf * rstd
            gg = g_ref[...]
            sc1 = 1.0 + sc_ref[...]
            dsh_ref[...] += _csum(dh)
            dsc_ref[...] += _csum(dh * xn * gg)
            dg_ref[...] += _csum(dh * xn * sc1)
            dn = dh * gg * sc1
            dx_ref[...] = dr_ref[...] + rstd * (dn - xn * jnp.mean(dn * xn, axis=-1, keepdims=True))

        if na:
            @pl.when((b == nb - 1) & (i == ni - 1) & (j == nj - 1))
            def _():
                for cp in copies:
                    cp.wait()

    hbm = pl.BlockSpec(memory_space=pl.ANY)
    in_specs = [pl.BlockSpec((None, tm, tn), lambda b, i, j: (b, i, j))]
    args = [dy]
    if conv:
        in_specs += [pl.BlockSpec((None, HALO, tn), lambda b, i, j: (b, jnp.minimum((i + 1) * hb, s // HALO - 1), j)),
                     pl.BlockSpec((3, tn), lambda b, i, j: (0, j))]
        args += [dy, conv_w]
    in_specs += [pl.BlockSpec((tn, d), lambda b, i, j: (j, 0)),
                 pl.BlockSpec((None, tm, d), lambda b, i, j: (b, i, 0)),
                 pl.BlockSpec((1, d), lambda b, i, j: (0, 0)),
                 pl.BlockSpec((None, 1, d), lambda b, i, j: (b, 0, 0)),
                 pl.BlockSpec((None, tm, d), lambda b, i, j: (b, i, 0))]
    in_specs += [hbm] * na
    args += [w, x, g, scale, dres, *riding]
    return pl.pallas_call(
        body, name=name, grid=(nb, ni, nj), in_specs=in_specs,
        out_specs=[pl.BlockSpec((None, tm, d), lambda b, i, j: (b, i, 0)),
                   pl.BlockSpec((None, tm, tn), lambda b, i, j: (b, i, j)),
                   pl.BlockSpec((None, 1, d), lambda b, i, j: (b, 0, 0)),
                   pl.BlockSpec((None, 1, d), lambda b, i, j: (b, 0, 0)),
                   pl.BlockSpec((1, d), lambda b, i, j: (0, 0))] + [hbm] * na,
        out_shape=[jax.ShapeDtypeStruct((nb, s, d), F32), jax.ShapeDtypeStruct((nb, s, n), MXU),
                   jax.ShapeDtypeStruct((nb, 1, d), F32), jax.ShapeDtypeStruct((nb, 1, d), F32),
                   jax.ShapeDtypeStruct((1, d), F32)] + [jax.ShapeDtypeStruct(a.shape, a.dtype) for a in riding],
        scratch_shapes=[pltpu.VMEM((tm, d), F32)] + (_chip_exchange_sems(na) if na else []),
        compiler_params=_cp("arbitrary", "arbitrary", "arbitrary"))(*args)


def _wgrad(xm, dym, name):
    t, k = xm.shape
    n = dym.shape[1]
    tk = 1408 if k % 1408 == 0 else 1024
    tt = min(1024, t)

    def body(x_ref, dy_ref, o_ref):
        @pl.when(pl.program_id(1) == 0)
        def _():
            o_ref[...] = jnp.zeros_like(o_ref)

        o_ref[...] += _dot_tn(x_ref[...], dy_ref[...])

    return pl.pallas_call(
        body, name=name, grid=(k // tk, t // tt),
        in_specs=[pl.BlockSpec((tt, tk), lambda a, c: (c, a)),
                  pl.BlockSpec((tt, n), lambda a, c: (c, 0))],
        out_specs=pl.BlockSpec((tk, n), lambda a, c: (a, 0)),
        out_shape=jax.ShapeDtypeStruct((k, n), F32),
        compiler_params=_cp("parallel", "arbitrary"))(xm, dym)


def _out_proj(parts, ws, gate, res, name):
    nb, s, d = res.shape
    tm = min(512, s)
    npart = len(parts)

    def body(*refs):
        p_refs, w_refs = refs[:npart], refs[npart:2 * npart]
        gt_ref, res_ref, xo_ref, y_ref = refs[2 * npart:]
        y = _dot(p_refs[0][...].astype(MXU), w_refs[0][...])
        for p_ref, w_ref in zip(p_refs[1:], w_refs[1:]):
            y = y + _dot(p_ref[...].astype(MXU), w_ref[...])
        y_ref[...] = y
        xo_ref[...] = res_ref[...] + gt_ref[...] * y

    in_specs = [pl.BlockSpec((None, tm, p.shape[-1]), lambda b, i: (b, i, 0)) for p in parts]
    in_specs += [pl.BlockSpec(w.shape, lambda b, i: (0, 0)) for w in ws]
    in_specs += [pl.BlockSpec((None, 1, d), lambda b, i: (b, 0, 0)),
                 pl.BlockSpec((None, tm, d), lambda b, i: (b, i, 0))]
    return pl.pallas_call(
        body, name=name, grid=(nb, s // tm), in_specs=in_specs,
        out_specs=[pl.BlockSpec((None, tm, d), lambda b, i: (b, i, 0))] * 2,
        out_shape=[jax.ShapeDtypeStruct((nb, s, d), F32)] * 2,
        compiler_params=_cp("parallel", "parallel"))(*parts, *ws, gate, res)


def _gate_bwd_nt(dx, y, gate, ws, name):
    nb, s, d = dx.shape
    tm = min(256, s)
    npart = len(ws)

    def body(*refs):
        dx_ref, y_ref, gt_ref = refs[:3]
        w_refs = refs[3:3 + npart]
        da_refs = refs[3 + npart:3 + 2 * npart]
        dy_ref, dgt_ref = refs[3 + 2 * npart:]

        @pl.when(pl.program_id(1) == 0)
        def _():
            dgt_ref[...] = jnp.zeros_like(dgt_ref)

        dxv = dx_ref[...]
        dyv = (dxv * gt_ref[...]).astype(MXU)
        dy_ref[...] = dyv
        dgt_ref[...] += _csum(dxv * y_ref[...])
        for w_ref, da_ref in zip(w_refs, da_refs):
            da_ref[...] = _dot_nt(dyv, w_ref[...])

    tile = pl.BlockSpec((None, tm, d), lambda b, i: (b, i, 0))
    row = pl.BlockSpec((None, 1, d), lambda b, i: (b, 0, 0))
    outs = pl.pallas_call(
        body, name=name, grid=(nb, s // tm),
        in_specs=[tile, tile, row] + [pl.BlockSpec(w.shape, lambda b, i: (0, 0)) for w in ws],
        out_specs=[pl.BlockSpec((None, tm, w.shape[0]), lambda b, i: (b, i, 0)) for w in ws] + [tile, row],
        out_shape=[jax.ShapeDtypeStruct((nb, s, w.shape[0]), F32) for w in ws]
        + [jax.ShapeDtypeStruct((nb, s, d), MXU), jax.ShapeDtypeStruct((nb, 1, d), F32)],
        compiler_params=_cp("arbitrary", "arbitrary"))(dx, y, gate, *ws)
    return outs[:npart], outs[npart], outs[npart + 1]


def _conv_shifts(xv, halo, rows):
    last, before = halo[HALO - 1:HALO, :], halo[HALO - 2:HALO - 1, :]
    p1 = jnp.where(rows == 0, last, pltpu.roll(xv, 1, 0))
    p2 = jnp.where(rows == 0, before, jnp.where(rows == 1, last, pltpu.roll(xv, 2, 0)))
    return p1, p2


def _conv_gate_matmul(u, cw, cb, wd, gate, res, name):
    nb, s, f2 = u.shape
    f = f2 // 2
    d = wd.shape[1]
    tm = min(512, s)
    tk = f // 2
    nk = f // tk
    hb = tm // HALO

    def body(ug_ref, uv_ref, hg_ref, hv_ref, cwg_ref, cwv_ref, cbg_ref, cbv_ref, wd_ref, gt_ref, res_ref,
             xo_ref, y_ref, acc):
        i, k = pl.program_id(1), pl.program_id(2)

        @pl.when(k == 0)
        def _():
            acc[...] = jnp.zeros_like(acc)

        rows = _iota((tm, 1), 0)

        def conv(x_ref, h_ref, w_ref, b_ref):
            xv = x_ref[...].astype(F32)
            halo = jnp.where(i == 0, 0.0, h_ref[...].astype(F32))
            p1, p2 = _conv_shifts(xv, halo, rows)
            wv = w_ref[...]
            return wv[2:3, :] * xv + wv[1:2, :] * p1 + wv[0:1, :] * p2 + b_ref[...]

        gv = conv(ug_ref, hg_ref, cwg_ref, cbg_ref)
        vv = conv(uv_ref, hv_ref, cwv_ref, cbv_ref)
        av = gv * _sigmoid(gv) * vv
        acc[...] += _dot(av.astype(MXU), wd_ref[...])

        @pl.when(k == nk - 1)
        def _():
            y = acc[...]
            y_ref[...] = y
            xo_ref[...] = res_ref[...] + gt_ref[...] * y

    def halo_idx(off):
        return lambda b, i, k: (b, jnp.maximum(i * hb - 1, 0), k + off)

    tile = pl.BlockSpec((None, tm, d), lambda b, i, k: (b, i, 0))
    return pl.pallas_call(
        body, name=name, grid=(nb, s // tm, nk),
        in_specs=[pl.BlockSpec((None, tm, tk), lambda b, i, k: (b, i, k)),
                  pl.BlockSpec((None, tm, tk), lambda b, i, k: (b, i, k + nk)),
                  pl.BlockSpec((None, HALO, tk), halo_idx(0)),
                  pl.BlockSpec((None, HALO, tk), halo_idx(nk)),
                  pl.BlockSpec((3, tk), lambda b, i, k: (0, k)),
                  pl.BlockSpec((3, tk), lambda b, i, k: (0, k + nk)),
                  pl.BlockSpec((1, tk), lambda b, i, k: (0, k)),
                  pl.BlockSpec((1, tk), lambda b, i, k: (0, k + nk)),
                  pl.BlockSpec((tk, d), lambda b, i, k: (k, 0)),
                  pl.BlockSpec((None, 1, d), lambda b, i, k: (b, 0, 0)),
                  tile],
        out_specs=[tile, tile],
        out_shape=[jax.ShapeDtypeStruct((nb, s, d), F32)] * 2,
        scratch_shapes=[pltpu.VMEM((tm, d), F32)],
        compiler_params=_cp("parallel", "parallel", "arbitrary"))(u, u, u, u, cw, cw, cb, cb, wd, gate, res)


def _conv_gate_bwd(da, u, cw, cb, name):
    nb, s, f2 = u.shape
    f = f2 // 2
    tm = min(128, s)
    hb = tm // HALO

    def body(da_ref, u_ref, h_ref, cw_ref, cb_ref, du_ref, a_ref, st_ref):
        b, i = pl.program_id(0), pl.program_id(1)

        @pl.when((b == 0) & (i == 0))
        def _():
            st_ref[...] = jnp.zeros_like(st_ref)

        rows = _iota((tm, 1), 0)
        first = i == 0

        def conv(cs):
            xv = u_ref[:, cs].astype(F32)
            halo = jnp.where(first, 0.0, h_ref[:, cs].astype(F32))
            p1, p2 = _conv_shifts(xv, halo, rows)
            wv = cw_ref[:, cs]
            return xv, p1, p2, wv[2:3, :] * xv + wv[1:2, :] * p1 + wv[0:1, :] * p2 + cb_ref[:, cs]

        def stats(cs, du, xv, p1, p2):
            du_ref[:, cs] = du.astype(MXU)
            st_ref[0:1, cs] += _csum(du)
            st_ref[1:2, cs] += _csum(du * p2)
            st_ref[2:3, cs] += _csum(du * p1)
            st_ref[3:4, cs] += _csum(du * xv)

        for k in range(f // LANES):
            cg = slice(k * LANES, (k + 1) * LANES)
            cv = slice(f + k * LANES, f + (k + 1) * LANES)
            xg, g1, g2, gv = conv(cg)
            xv, v1, v2, vv = conv(cv)
            sg = _sigmoid(gv)
            sl = gv * sg
            a_ref[:, cg] = (sl * vv).astype(MXU)
            dav = da_ref[:, cg]
            stats(cg, dav * vv * (sg * (1.0 + gv * (1.0 - sg))), xg, g1, g2)
            stats(cv, dav * sl, xv, v1, v2)

    return pl.pallas_call(
        body, name=name, grid=(nb, s // tm),
        in_specs=[pl.BlockSpec((None, tm, f), lambda b, i: (b, i, 0)),
                  pl.BlockSpec((None, tm, f2), lambda b, i: (b, i, 0)),
                  pl.BlockSpec((None, HALO, f2), lambda b, i: (b, jnp.maximum(i * hb - 1, 0), 0)),
                  pl.BlockSpec((3, f2), lambda b, i: (0, 0)),
                  pl.BlockSpec((1, f2), lambda b, i: (0, 0))],
        out_specs=[pl.BlockSpec((None, tm, f2), lambda b, i: (b, i, 0)),
                   pl.BlockSpec((None, tm, f), lambda b, i: (b, i, 0)),
                   pl.BlockSpec((8, f2), lambda b, i: (0, 0))],
        out_shape=[jax.ShapeDtypeStruct((nb, s, f2), MXU), jax.ShapeDtypeStruct((nb, s, f), MXU),
                   jax.ShapeDtypeStruct((8, f2), F32)],
        compiler_params=_cp("arbitrary", "arbitrary"))(da, u, u, cw, cb)


def _rot(xv, lane):
    return jnp.where((lane >= 64) & (lane < 80), -pltpu.roll(xv, 112, 1),
                     jnp.where((lane >= 80) & (lane < 96), pltpu.roll(xv, 16, 1), 0.0))


def _rot_t(dv, lane):
    return jnp.where((lane >= 80) & (lane < 96), -pltpu.roll(dv, 16, 1),
                     jnp.where((lane >= 64) & (lane < 80), pltpu.roll(dv, 112, 1), 0.0))


def _mla_prep_specs(s, tm):
    def blk(width, col):
        return pl.BlockSpec((None, tm, width), lambda b, i: (b, i, col // width))

    full = lambda shape: pl.BlockSpec(shape, lambda b, i: (0, 0))
    return [blk(256, COL_CQ), blk(128, COL_CKV), blk(128, COL_KR),
            pl.BlockSpec((None, tm, LANES), lambda b, i: (b, i, 0)),
            pl.BlockSpec((None, tm, LANES), lambda b, i: (b, i, 0)),
            full((1, 256)), full((1, 128)), full((1, 128)), full((1, 128)),
            full((768, 256)), full((768, 128))]


def _mla_prep(proj, cs, sn, gcq, gckv, gqn, gkn, wuq, wukv, name):
    nb, s, _ = proj.shape
    tm = min(256, s)

    def body(cq_ref, ckv_ref, kr_ref, c_ref, s_ref, gcq_ref, gckv_ref, gqn_ref, gkn_ref, wuq_ref, wukv_ref,
             q_ref, k_ref, v_ref):
        lane = _iota((tm, LANES), 1)
        cv, sv = c_ref[...], s_ref[...]
        cq = cq_ref[...].astype(F32)
        cqn = cq * lax.rsqrt(jnp.mean(cq * cq, axis=-1, keepdims=True) + EPS) * gcq_ref[...]
        qb = _dot_nt(cqn.astype(MXU), wuq_ref[...])
        ckv = ckv_ref[...].astype(F32)
        ckvn = ckv * lax.rsqrt(jnp.mean(ckv * ckv, axis=-1, keepdims=True) + EPS) * gckv_ref[...]
        kvb = _dot_nt(ckvn.astype(MXU), wukv_ref[...])
        kr = kr_ref[...].astype(F32)
        for h in range(MLA_HEADS):
            hs = slice(h * LANES, (h + 1) * LANES)
            qh = qb[:, hs]
            qn = qh * lax.rsqrt(_rsum(qh * qh) / MLA_QK + EPS) * gqn_ref[...]
            q_ref[:, hs] = (qn * cv + _rot(qn, lane) * sv).astype(MXU)
            kc = jnp.where(lane < HEAD, kvb[:, hs], kr)
            kn = kc * lax.rsqrt(_rsum(kc * kc) / MLA_QK + EPS) * gkn_ref[...]
            k_ref[:, hs] = (kn * cv + _rot(kn, lane) * sv).astype(MXU)
        for j in range(MLA_HEADS // 2):
            va = kvb[:, (2 * j) * LANES:(2 * j + 1) * LANES]
            vb = kvb[:, (2 * j + 1) * LANES:(2 * j + 2) * LANES]
            v_ref[:, j * LANES:(j + 1) * LANES] = jnp.where(lane < HEAD, pltpu.roll(va, HEAD, 1), vb).astype(MXU)

    return pl.pallas_call(
        body, name=name, grid=(nb, s // tm), in_specs=_mla_prep_specs(s, tm),
        out_specs=[pl.BlockSpec((None, tm, 768), lambda b, i: (b, i, 0)),
                   pl.BlockSpec((None, tm, 768), lambda b, i: (b, i, 0)),
                   pl.BlockSpec((None, tm, 384), lambda b, i: (b, i, 0))],
        out_shape=[jax.ShapeDtypeStruct((nb, s, 768), MXU), jax.ShapeDtypeStruct((nb, s, 768), MXU),
                   jax.ShapeDtypeStruct((nb, s, 384), MXU)],
        compiler_params=_cp("parallel", "parallel"))(proj, proj, proj, cs, sn, gcq, gckv, gqn, gkn, wuq, wukv)


def _mla_prep_bwd(proj, cs, sn, gcq, gckv, gqn, gkn, wuq, wukv, dq, dk, dv, name):
    nb, s, _ = proj.shape
    tm = min(256, s)

    def body(cq_ref, ckv_ref, kr_ref, c_ref, s_ref, gcq_ref, gckv_ref, gqn_ref, gkn_ref, wuq_ref, wukv_ref,
             dq_ref, dk_ref, dv_ref,
             dcq_ref, dckv_ref, dkr_ref, dwuq_ref, dwukv_ref, dgcq_ref, dgckv_ref, dgqn_ref, dgkn_ref,
             dqb_s, dkvb_s):
        @pl.when((pl.program_id(0) == 0) & (pl.program_id(1) == 0))
        def _():
            for r in (dwuq_ref, dwukv_ref, dgcq_ref, dgckv_ref, dgqn_ref, dgkn_ref):
                r[...] = jnp.zeros_like(r)

        lane = _iota((tm, LANES), 1)
        cv, sv = c_ref[...], s_ref[...]
        gqn, gkn = gqn_ref[...], gkn_ref[...]
        cq = cq_ref[...].astype(F32)
        rc = lax.rsqrt(jnp.mean(cq * cq, axis=-1, keepdims=True) + EPS)
        chat = cq * rc
        cqn = (chat * gcq_ref[...]).astype(MXU)
        qb = _dot_nt(cqn, wuq_ref[...])
        ckv = ckv_ref[...].astype(F32)
        rkv = lax.rsqrt(jnp.mean(ckv * ckv, axis=-1, keepdims=True) + EPS)
        kvhat = ckv * rkv
        ckvn = (kvhat * gckv_ref[...]).astype(MXU)
        kvb = _dot_nt(ckvn, wukv_ref[...])
        kr = kr_ref[...].astype(F32)
        dgq = jnp.zeros((1, LANES), F32)
        dgk = jnp.zeros((1, LANES), F32)
        dkr = jnp.zeros((tm, LANES), F32)
        for h in range(MLA_HEADS):
            hs = slice(h * LANES, (h + 1) * LANES)
            qh = qb[:, hs]
            rq = lax.rsqrt(_rsum(qh * qh) / MLA_QK + EPS)
            qhat = qh * rq
            dqr = dq_ref[:, hs]
            dqn = dqr * cv + _rot_t(dqr * sv, lane)
            dgq = dgq + _csum(dqn * qhat)
            dyq = dqn * gqn
            dqb_s[:, hs] = (rq * (dyq - qhat * (_rsum(dyq * qhat) / MLA_QK))).astype(MXU)

            kc = jnp.where(lane < HEAD, kvb[:, hs], kr)
            rk = lax.rsqrt(_rsum(kc * kc) / MLA_QK + EPS)
            khat = kc * rk
            dkr_h = dk_ref[:, hs]
            dkn = dkr_h * cv + _rot_t(dkr_h * sv, lane)
            dgk = dgk + _csum(dkn * khat)
            dyk = dkn * gkn
            dkc = rk * (dyk - khat * (_rsum(dyk * khat) / MLA_QK))
            dkr = dkr + jnp.where(lane >= HEAD, dkc, 0.0)
            dvb = dv_ref[:, (h // 2) * LANES:(h // 2 + 1) * LANES]
            dvp = dvb if h % 2 == 1 else pltpu.roll(dvb, HEAD, 1)
            dkvb_s[:, hs] = jnp.where(lane < HEAD, dkc, dvp).astype(MXU)
        dgqn_ref[...] += dgq
        dgkn_ref[...] += dgk
        dkr_ref[...] = dkr

        dqb = dqb_s[...]
        dwuq_ref[...] += _dot_tn(dqb, cqn)
        dcqn = _dot(dqb, wuq_ref[...])
        dgcq_ref[...] += _csum(dcqn * chat)
        dyc = dcqn * gcq_ref[...]
        dcq_ref[...] = rc * (dyc - chat * jnp.mean(dyc * chat, axis=-1, keepdims=True))

        dkvb = dkvb_s[...]
        dwukv_ref[...] += _dot_tn(dkvb, ckvn)
        dckvn = _dot(dkvb, wukv_ref[...])
        dgckv_ref[...] += _csum(dckvn * kvhat)
        dykv = dckvn * gckv_ref[...]
        dckv_ref[...] = rkv * (dykv - kvhat * jnp.mean(dykv * kvhat, axis=-1, keepdims=True))

    full = lambda shape: pl.BlockSpec(shape, lambda b, i: (0, 0))
    tile = lambda width: pl.BlockSpec((None, tm, width), lambda b, i: (b, i, 0))
    return pl.pallas_call(
        body, name=name, grid=(nb, s // tm),
        in_specs=_mla_prep_specs(s, tm) + [tile(768), tile(768), tile(384)],
        out_specs=[tile(256), tile(128), tile(128), full((768, 256)), full((768, 128)),
                   full((1, 256)), full((1, 128)), full((1, 128)), full((1, 128))],
        out_shape=[jax.ShapeDtypeStruct((nb, s, 256), F32), jax.ShapeDtypeStruct((nb, s, 128), F32),
                   jax.ShapeDtypeStruct((nb, s, 128), F32),
                   jax.ShapeDtypeStruct((768, 256), F32), jax.ShapeDtypeStruct((768, 128), F32),
                   jax.ShapeDtypeStruct((1, 256), F32), jax.ShapeDtypeStruct((1, 128), F32),
                   jax.ShapeDtypeStruct((1, 128), F32), jax.ShapeDtypeStruct((1, 128), F32)],
        scratch_shapes=[pltpu.VMEM((tm, 768), MXU), pltpu.VMEM((tm, 768), MXU)],
        compiler_params=_cp("arbitrary", "arbitrary"))(
            proj, proj, proj, cs, sn, gcq, gckv, gqn, gkn, wuq, wukv, dq, dk, dv)


def _softplus(z):
    return jnp.maximum(z, 0.0) + jnp.log(1.0 + jnp.exp(-jnp.abs(z)))


def _sb_fwd(proj, name, riding=()):
    nb, s, _ = proj.shape
    tq, tk = min(256, s), min(256, s)
    ratio = tq // tk
    na = len(riding)
    grid = (nb, 2, s // tq)

    def body(q_ref, k_ref, v_ref, o_ref, ct_ref, cnt_ref):
        i = pl.program_id(2)
        lo = _iota((tq, LANES), 1) < HEAD
        qv = q_ref[...]
        q0 = jnp.where(lo, qv, 0.0).astype(MXU)
        q1 = jnp.where(lo, 0.0, qv).astype(MXU)
        usuf = (_iota((tk, tk), 0) > _iota((tk, tk), 1)).astype(MXU)
        tpos = i * tq + _iota((tq, tk), 0)
        scol = _iota((tq, tk), 1)
        nch = (i + 1) * ratio

        def alive(st):
            return (st[0] < nch) & (st[5] > SB_DEAD)

        def step(st):
            t, c0, a0, c1, a1, _ = st
            j = nch - 1 - t
            off = pl.multiple_of(j * tk, tk)
            kc = k_ref[pl.ds(off, tk), :].astype(MXU)
            vc = v_ref[pl.ds(off, tk), :].astype(MXU)
            msk = (scol + j * tk) < tpos

            def head(qm, c, a):
                z = _dot_nt(qm, kc) * SB_SCALE
                sp = _softplus(z)
                lk = jnp.where(msk, -sp, 0.0)
                w = jnp.where(msk, jnp.exp(z - sp + _cumdot(lk, usuf) + c), 0.0)
                return c + _rsum(lk), a + _dot(w.astype(MXU), vc)

            c0, a0 = head(q0, c0, a0)
            c1, a1 = head(q1, c1, a1)
            return t + 1, c0, a0, c1, a1, jnp.maximum(jnp.max(c0), jnp.max(c1))

        z1 = jnp.zeros((tq, 1), F32)
        za = jnp.zeros((tq, LANES), F32)
        t, c0, a0, c1, a1, _ = lax.while_loop(alive, step, (jnp.int32(0), z1, za, z1, za, jnp.float32(0.0)))
        o_ref[...] = jnp.where(lo, a0, a1)
        ct_ref[...] = jnp.where(lo, c0, c1)
        cnt_ref[...] = jnp.zeros((8, LANES), F32) + t.astype(F32)

    kv = lambda col: pl.BlockSpec((None, s, LANES), lambda b, p, i: (b, 0, col // LANES + p))
    tile = pl.BlockSpec((None, tq, LANES), lambda b, p, i: (b, i, p))
    hbm = pl.BlockSpec(memory_space=pl.ANY)
    return pl.pallas_call(
        _with_gather(body, 3, 3, na, grid), name=name, grid=grid,
        in_specs=[pl.BlockSpec((None, tq, LANES), lambda b, p, i: (b, i, COL_SBQ // LANES + p)),
                  kv(COL_SBK), kv(COL_SBV)] + [hbm] * na,
        out_specs=[tile, tile, pl.BlockSpec((None, None, None, 8, LANES), lambda b, p, i: (b, p, i, 0, 0))] + [hbm] * na,
        out_shape=[jax.ShapeDtypeStruct((nb, s, 256), F32)] * 2
        + [jax.ShapeDtypeStruct((nb, 2, s // tq, 8, LANES), F32)] + _gather_out_shapes(riding),
        scratch_shapes=_gather_sems(na) if na else [],
        compiler_params=_cp("arbitrary", "arbitrary", "arbitrary"))(proj, proj, proj, *riding)


def _sb_bwd(proj, ct, cnt, do, name):
    nb, s, _ = proj.shape
    tq, tk = min(256, s), min(256, s)
    ratio = tq // tk

    def body(q_ref, k_ref, v_ref, ct_ref, cnt_ref, do_ref, dq_ref, dk_ref, dv_ref):
        i = pl.program_id(2)

        @pl.when(i == 0)
        def _():
            dk_ref[...] = jnp.zeros_like(dk_ref)
            dv_ref[...] = jnp.zeros_like(dv_ref)

        lane = _iota((tq, LANES), 1)
        lo = lane < HEAD
        lok = _iota((tk, LANES), 1) < HEAD
        qv, dov = q_ref[...], do_ref[...]
        qb, dob = qv.astype(MXU), dov.astype(MXU)
        q0 = jnp.where(lo, qv, 0.0).astype(MXU)
        q1 = jnp.where(lo, 0.0, qv).astype(MXU)
        do0 = jnp.where(lo, dov, 0.0).astype(MXU)
        do1 = jnp.where(lo, 0.0, dov).astype(MXU)
        ctv = ct_ref[...]
        ct0 = _rsum(jnp.where(lane == 0, ctv, 0.0))
        ct1 = _rsum(jnp.where(lane == LANES - 1, ctv, 0.0))
        uincl = (_iota((tk, tk), 0) <= _iota((tk, tk), 1)).astype(MXU)
        ustrict = (_iota((tk, tk), 0) < _iota((tk, tk), 1)).astype(MXU)
        tpos = i * tq + _iota((tq, tk), 0)
        scol = _iota((tq, tk), 1)
        nch = (i + 1) * ratio

        def step(j, carry):
            p0, g0, dq0, p1, g1, dq1 = carry
            off = pl.multiple_of(j * tk, tk)
            kc = k_ref[pl.ds(off, tk), :].astype(MXU)
            vc = v_ref[pl.ds(off, tk), :].astype(MXU)
            msk = (scol + j * tk) < tpos

            def head(qm, dom, ctot, pc, gc, dqa):
                z = _dot_nt(qm, kc) * SB_SCALE
                sp = _softplus(z)
                lk = jnp.where(msk, -sp, 0.0)
                lsig = z - sp
                w = jnp.where(msk, jnp.exp(lsig + (ctot - pc - _cumdot(lk, uincl))), 0.0)
                g = w * _dot_nt(dom, vc)
                gpre = gc + _cumdot(g, ustrict)
                sig = jnp.exp(lsig)
                dz = (jnp.where(msk, g * (1.0 - sig) - sig * gpre, 0.0) * SB_SCALE).astype(MXU)
                return (pc + _rsum(lk), gc + _rsum(g), dqa + _dot(dz, kc),
                        _dot_tn(dz, qb), _dot_tn(w.astype(MXU), dob))

            p0, g0, dq0, dk0, dv0 = head(q0, do0, ct0, p0, g0, dq0)
            p1, g1, dq1, dk1, dv1 = head(q1, do1, ct1, p1, g1, dq1)
            dk_ref[pl.ds(off, tk), :] += jnp.where(lok, dk0, dk1)
            dv_ref[pl.ds(off, tk), :] += jnp.where(lok, dv0, dv1)
            return p0, g0, dq0, p1, g1, dq1

        z1 = jnp.zeros((tq, 1), F32)
        za = jnp.zeros((tq, LANES), F32)
        first = nch - jnp.max(cnt_ref[...]).astype(jnp.int32)
        _, _, dq0, _, _, dq1 = lax.fori_loop(first, nch, step, (z1, z1, za, z1, z1, za))
        dq_ref[...] = jnp.where(lo, dq0, dq1)

    kv = lambda col: pl.BlockSpec((None, s, LANES), lambda b, p, i: (b, 0, col // LANES + p))
    tile = pl.BlockSpec((None, tq, LANES), lambda b, p, i: (b, i, p))
    acc = pl.BlockSpec((None, s, LANES), lambda b, p, i: (b, 0, p))
    return pl.pallas_call(
        body, name=name, grid=(nb, 2, s // tq),
        in_specs=[pl.BlockSpec((None, tq, LANES), lambda b, p, i: (b, i, COL_SBQ // LANES + p)),
                  kv(COL_SBK), kv(COL_SBV), tile,
                  pl.BlockSpec((None, None, None, 8, LANES), lambda b, p, i: (b, p, i, 0, 0)), tile],
        out_specs=[tile, acc, acc],
        out_shape=[jax.ShapeDtypeStruct((nb, s, 256), F32)] * 3,
        compiler_params=_cp("parallel", "parallel", "arbitrary"))(proj, proj, proj, ct, cnt, do)


def _mla_fwd(q, k, v, name, riding=()):
    nb, s, _ = q.shape
    tq = tk = min(256, s)
    na = len(riding)
    grid = (nb, MLA_HEADS // 2, s // tq)

    def body(q_ref, k_ref, v_ref, o_ref, lse_ref):
        i = pl.program_id(2)
        q0, q1 = q_ref[:, :LANES], q_ref[:, LANES:]
        krow = _iota((tk, tq), 0)
        qcol = _iota((tk, tq), 1)

        def step(j, carry, diagonal):
            m0, l0, a0, m1, l1, a1 = carry
            off = pl.multiple_of(j * tk, tk)
            vc = v_ref[pl.ds(off, tk), :]

            def head(qh, kh, m, l, a):
                st = _dot_nt(kh, qh) * MLA_SCALE
                if diagonal:
                    st = jnp.where(krow <= qcol, st, NEG)
                mn = jnp.maximum(m, jnp.max(st, axis=0, keepdims=True))
                al = jnp.exp(m - mn)
                pt = jnp.exp(st - mn)
                return mn, al * l + _csum(pt), al * a + _dot_tn(vc, pt.astype(MXU))

            m0, l0, a0 = head(q0, k_ref[pl.ds(off, tk), :LANES], m0, l0, a0)
            m1, l1, a1 = head(q1, k_ref[pl.ds(off, tk), LANES:], m1, l1, a1)
            return m0, l0, a0, m1, l1, a1

        mi = jnp.full((1, tq), NEG, F32)
        z1 = jnp.zeros((1, tq), F32)
        za = jnp.zeros((LANES, tq), F32)
        carry = lax.fori_loop(0, i, lambda j, cr: step(j, cr, False), (mi, z1, za, mi, z1, za))
        m0, l0, a0, m1, l1, a1 = step(i, carry, True)
        lo_rows = _iota((LANES, tq), 0) < HEAD
        o_ref[...] = jnp.where(lo_rows, a0 / l0, a1 / l1).T
        lse_ref[...] = jnp.zeros_like(lse_ref)
        lse_ref[0:1, :] = m0 + jnp.log(l0)
        lse_ref[1:2, :] = m1 + jnp.log(l1)

    tile = pl.BlockSpec((None, tq, LANES), lambda b, p, i: (b, i, p))
    hbm = pl.BlockSpec(memory_space=pl.ANY)
    return pl.pallas_call(
        _with_gather(body, 3, 2, na, grid), name=name, grid=grid,
        in_specs=[pl.BlockSpec((None, tq, 2 * LANES), lambda b, p, i: (b, i, p)),
                  pl.BlockSpec((None, s, 2 * LANES), lambda b, p, i: (b, 0, p)),
                  pl.BlockSpec((None, s, LANES), lambda b, p, i: (b, 0, p))] + [hbm] * na,
        out_specs=[tile, pl.BlockSpec((None, None, 8, tq), lambda b, p, i: (b, p, 0, i))] + [hbm] * na,
        out_shape=[jax.ShapeDtypeStruct((nb, s, 384), F32), jax.ShapeDtypeStruct((nb, MLA_HEADS // 2, 8, s), F32)]
        + _gather_out_shapes(riding),
        scratch_shapes=_gather_sems(na) if na else [],
        compiler_params=_cp("arbitrary", "arbitrary", "arbitrary"))(q, k, v, *riding)


def _mla_bwd(q, k, v, o, lse, do, name, riding=()):
    nb, s, _ = q.shape
    tq = tk = min(256, s)
    na = len(riding)
    grid = (nb, MLA_HEADS // 2, s // tq)

    def body(q_ref, k_ref, v_ref, o_ref, lse_ref, do_ref, dq_ref, dk_ref, dv_ref):
        i = pl.program_id(2)

        @pl.when(i == 0)
        def _():
            dk_ref[...] = jnp.zeros_like(dk_ref)
            dv_ref[...] = jnp.zeros_like(dv_ref)

        lo = _iota((tq, LANES), 1) < HEAD
        lok = _iota((tk, LANES), 1) < HEAD
        q0, q1 = q_ref[:, :LANES], q_ref[:, LANES:]
        dov = do_ref[...]
        dob = dov.astype(MXU)
        do0 = jnp.where(lo, dov, 0.0).astype(MXU)
        do1 = jnp.where(lo, 0.0, dov).astype(MXU)
        dd = dov * o_ref[...]
        hi = dd.astype(MXU)
        r1 = dd - hi.astype(F32)
        mid = r1.astype(MXU)
        low = (r1 - mid.astype(F32)).astype(MXU)
        sel_lane = _iota((8, LANES), 1) < HEAD
        sel0 = sel_lane.astype(MXU)
        sel1 = (~sel_lane).astype(MXU)
        dl0 = (_dot_nt(sel0, hi) + _dot_nt(sel0, mid) + _dot_nt(sel0, low))[0:1, :]
        dl1 = (_dot_nt(sel1, hi) + _dot_nt(sel1, mid) + _dot_nt(sel1, low))[0:1, :]
        ls0, ls1 = lse_ref[0:1, :], lse_ref[1:2, :]
        krow = _iota((tk, tq), 0)
        qcol = _iota((tk, tq), 1)

        def step(j, carry, diagonal):
            dq0, dq1 = carry
            off = pl.multiple_of(j * tk, tk)
            vc = v_ref[pl.ds(off, tk), :]

            def head(qh, kh, dom, ls, dl, dqa):
                st = _dot_nt(kh, qh) * MLA_SCALE
                if diagonal:
                    st = jnp.where(krow <= qcol, st, NEG)
                pt = jnp.exp(st - ls)
                dst = (pt * (_dot_nt(vc, dom) - dl) * MLA_SCALE).astype(MXU)
                return dqa + _dot_tn(kh, dst), _dot(dst, qh), _dot(pt.astype(MXU), dob)

            dq0, dk0, dv0 = head(q0, k_ref[pl.ds(off, tk), :LANES], do0, ls0, dl0, dq0)
            dq1, dk1, dv1 = head(q1, k_ref[pl.ds(off, tk), LANES:], do1, ls1, dl1, dq1)
            dk_ref[pl.ds(off, tk), :LANES] += dk0
            dk_ref[pl.ds(off, tk), LANES:] += dk1
            dv_ref[pl.ds(off, tk), :] += jnp.where(lok, dv0, dv1)
            return dq0, dq1

        za = jnp.zeros((LANES, tq), F32)
        carry = lax.fori_loop(0, i, lambda j, cr: step(j, cr, False), (za, za))
        dq0, dq1 = step(i, carry, True)
        dq_ref[:, :LANES] = dq0.T
        dq_ref[:, LANES:] = dq1.T

    tile = pl.BlockSpec((None, tq, LANES), lambda b, p, i: (b, i, p))
    tile2 = pl.BlockSpec((None, tq, 2 * LANES), lambda b, p, i: (b, i, p))
    hbm = pl.BlockSpec(memory_space=pl.ANY)
    return pl.pallas_call(
        _with_chip_exchange(body, 6, 3, na, grid), name=name, grid=grid,
        in_specs=[tile2,
                  pl.BlockSpec((None, s, 2 * LANES), lambda b, p, i: (b, 0, p)),
                  pl.BlockSpec((None, s, LANES), lambda b, p, i: (b, 0, p)),
                  tile, pl.BlockSpec((None, None, 8, tq), lambda b, p, i: (b, p, 0, i)), tile] + [hbm] * na,
        out_specs=[tile2,
                   pl.BlockSpec((None, s, 2 * LANES), lambda b, p, i: (b, 0, p)),
                   pl.BlockSpec((None, s, LANES), lambda b, p, i: (b, 0, p))] + [hbm] * na,
        out_shape=[jax.ShapeDtypeStruct((nb, s, 768), F32), jax.ShapeDtypeStruct((nb, s, 768), F32),
                   jax.ShapeDtypeStruct((nb, s, 384), F32)] + [jax.ShapeDtypeStruct(a.shape, a.dtype) for a in riding],
        scratch_shapes=_chip_exchange_sems(na) if na else [],
        compiler_params=_cp("arbitrary", "arbitrary", "arbitrary"))(q, k, v, o, lse, do, *riding)


def _half_stats(xv, lo):
    x2 = xv * xv
    s0 = _rsum(jnp.where(lo, x2, 0.0))
    s1 = _rsum(jnp.where(lo, 0.0, x2))
    return jnp.where(lo, lax.rsqrt(s0 / HEAD + EPS), lax.rsqrt(s1 / HEAD + EPS))


def _half_mean(xv, lo):
    s0 = _rsum(jnp.where(lo, xv, 0.0))
    s1 = _rsum(jnp.where(lo, 0.0, xv))
    return jnp.where(lo, s0, s1) / HEAD


def _swa_in_specs():
    def band(col, prev):
        if prev:
            return pl.BlockSpec((None, BLOCK, LANES), lambda b, n: (b, jnp.maximum(n - 1, 0), col // LANES))
        return pl.BlockSpec((None, BLOCK, LANES), lambda b, n: (b, n, col // LANES))

    full = lambda shape: pl.BlockSpec(shape, lambda b, n: tuple(0 for _ in shape))
    return [pl.BlockSpec((None, BLOCK, 384), lambda b, n: (b, n, COL_SWQ // 384)),
            band(COL_SWK, False), band(COL_SWK, True), band(COL_SWV, False), band(COL_SWV, True),
            full((1, LANES)), full((1, LANES)), full((8, LANES)), full((SW_HEADS, BLOCK, 2 * BLOCK))]


def _swa_valid(n):
    a = _iota((BLOCK, 2 * BLOCK), 0)
    bcol = _iota((BLOCK, 2 * BLOCK), 1)
    dist = BLOCK + a - bcol
    return (dist >= 0) & (dist < BLOCK) & ((n > 0) | (bcol >= BLOCK))


def _swa_fwd(proj, gq, gk, sinks, bias, name):
    nb, s, _ = proj.shape

    def body(q_ref, kc_ref, kp_ref, vc_ref, vp_ref, gq_ref, gk_ref, sk_ref, bias_ref, o_ref):
        n = pl.program_id(1)
        lo = _iota((BLOCK, LANES), 1) < HEAD
        lo2 = _iota((2 * BLOCK, LANES), 1) < HEAD
        kband = jnp.concatenate([kp_ref[...], kc_ref[...]], axis=0).astype(F32)
        vband = jnp.concatenate([vp_ref[...], vc_ref[...]], axis=0).astype(F32)
        kn = kband * _half_stats(kband, lo2) * gk_ref[...]
        ks = (kn.astype(MXU), pltpu.roll(kn, HEAD, 1).astype(MXU))
        vs = (vband.astype(MXU), pltpu.roll(vband, HEAD, 1).astype(MXU))
        valid = _swa_valid(n)
        for blk in range(SW_HEADS // 2):
            qv = q_ref[:, blk * LANES:(blk + 1) * LANES].astype(F32)
            qn = qv * _half_stats(qv, lo) * gq_ref[...]
            outs = []
            for half in range(2):
                h = 2 * blk + half
                swap = 0 if half == h // 3 else 1
                qm = jnp.where(lo if half == 0 else ~lo, qn, 0.0).astype(MXU)
                sc = jnp.where(valid, _dot_nt(qm, ks[swap]) * SW_SCALE + bias_ref[h], NEG)
                sk = jnp.max(sk_ref[h:h + 1, :], axis=-1, keepdims=True)
                m = jnp.maximum(jnp.max(sc, axis=-1, keepdims=True), sk)
                p = jnp.exp(sc - m)
                l = _rsum(p) + jnp.exp(sk - m)
                outs.append(_dot((p / l).astype(MXU), vs[swap]))
            o_ref[:, blk * LANES:(blk + 1) * LANES] = jnp.where(lo, outs[0], outs[1])

    return pl.pallas_call(
        body, name=name, grid=(nb, s // BLOCK), in_specs=_swa_in_specs(),
        out_specs=pl.BlockSpec((None, BLOCK, 384), lambda b, n: (b, n, 0)),
        out_shape=jax.ShapeDtypeStruct((nb, s, 384), F32),
        compiler_params=_cp("parallel", "parallel"))(proj, proj, proj, proj, proj, gq, gk, sinks, bias)


def _swa_bwd(proj, gq, gk, sinks, bias, do, name):
    nb, s, _ = proj.shape

    def body(q_ref, kc_ref, kp_ref, vc_ref, vp_ref, gq_ref, gk_ref, sk_ref, bias_ref, do_ref,
             dq_ref, dkc_ref, dkp_ref, dvc_ref, dvp_ref, dbias_ref, dsk_ref, dgq_ref, dgk_ref):
        n = pl.program_id(1)

        @pl.when((pl.program_id(0) == 0) & (n == 0))
        def _():
            for r in (dbias_ref, dsk_ref, dgq_ref, dgk_ref):
                r[...] = jnp.zeros_like(r)

        lo = _iota((BLOCK, LANES), 1) < HEAD
        lo2 = _iota((2 * BLOCK, LANES), 1) < HEAD
        kband = jnp.concatenate([kp_ref[...], kc_ref[...]], axis=0).astype(F32)
        vband = jnp.concatenate([vp_ref[...], vc_ref[...]], axis=0).astype(F32)
        rk = _half_stats(kband, lo2)
        khat = kband * rk
        gkv = gk_ref[...]
        kn = khat * gkv
        ks = (kn.astype(MXU), pltpu.roll(kn, HEAD, 1).astype(MXU))
        vs = (vband.astype(MXU), pltpu.roll(vband, HEAD, 1).astype(MXU))
        valid = _swa_valid(n)
        dkn = jnp.zeros((2 * BLOCK, LANES), F32)
        dvb = jnp.zeros((2 * BLOCK, LANES), F32)
        gqv = gq_ref[...]
        dgq = jnp.zeros((1, LANES), F32)
        for blk in range(SW_HEADS // 2):
            bs = slice(blk * LANES, (blk + 1) * LANES)
            qv = q_ref[:, bs].astype(F32)
            rq = _half_stats(qv, lo)
            qhat = qv * rq
            qn = qhat * gqv
            dov = do_ref[:, bs]
            dqn = jnp.zeros((BLOCK, LANES), F32)
            for half in range(2):
                h = 2 * blk + half
                swap = 0 if half == h // 3 else 1
                hm = lo if half == 0 else ~lo
                qm = jnp.where(hm, qn, 0.0).astype(MXU)
                dom = jnp.where(hm, dov, 0.0).astype(MXU)
                sc = jnp.where(valid, _dot_nt(qm, ks[swap]) * SW_SCALE + bias_ref[h], NEG)
                sk = jnp.max(sk_ref[h:h + 1, :], axis=-1, keepdims=True)
                m = jnp.maximum(jnp.max(sc, axis=-1, keepdims=True), sk)
                e = jnp.exp(sc - m)
                es = jnp.exp(sk - m)
                l = _rsum(e) + es
                p = e / l
                dp = _dot_nt(dom, vs[swap])
                delta = _rsum(p * dp)
                ds = p * (dp - delta)
                dsk_ref[h:h + 1, :] += jnp.broadcast_to(_csum(-(es / l) * delta), (1, LANES))
                dbias_ref[h] += ds
                dsb = (ds * SW_SCALE).astype(MXU)
                dqn = dqn + jnp.where(hm, _dot(dsb, ks[swap]), 0.0)
                rk_ = _dot_tn(dsb, qm)
                rv_ = _dot_tn(p.astype(MXU), dom)
                if swap:
                    rk_ = pltpu.roll(rk_, HEAD, 1)
                    rv_ = pltpu.roll(rv_, HEAD, 1)
                dkn = dkn + rk_
                dvb = dvb + rv_
            dgq = dgq + _csum(dqn * qhat)
            dyq = dqn * gqv
            dq_ref[:, bs] = rq * (dyq - qhat * _half_mean(dyq * qhat, lo))
        dgq_ref[...] += dgq
        dgk_ref[...] += _csum(dkn * khat)
        dyk = dkn * gkv
        dkb = rk * (dyk - khat * _half_mean(dyk * khat, lo2))
        dkp_ref[...] = dkb[:BLOCK]
        dkc_ref[...] = dkb[BLOCK:]
        dvp_ref[...] = dvb[:BLOCK]
        dvc_ref[...] = dvb[BLOCK:]

    full = lambda shape: pl.BlockSpec(shape, lambda b, n: tuple(0 for _ in shape))
    tile = pl.BlockSpec((None, BLOCK, LANES), lambda b, n: (b, n, 0))
    tile3 = pl.BlockSpec((None, BLOCK, 384), lambda b, n: (b, n, 0))
    kvs = jax.ShapeDtypeStruct((nb, s, LANES), F32)
    return pl.pallas_call(
        body, name=name, grid=(nb, s // BLOCK), in_specs=_swa_in_specs() + [tile3],
        out_specs=[tile3, tile, tile, tile, tile, full((SW_HEADS, BLOCK, 2 * BLOCK)), full((8, LANES)),
                   full((1, LANES)), full((1, LANES))],
        out_shape=[jax.ShapeDtypeStruct((nb, s, 384), F32), kvs, kvs, kvs, kvs,
                   jax.ShapeDtypeStruct((SW_HEADS, BLOCK, 2 * BLOCK), F32), jax.ShapeDtypeStruct((8, LANES), F32),
                   jax.ShapeDtypeStruct((1, LANES), F32), jax.ShapeDtypeStruct((1, LANES), F32)],
        compiler_params=_cp("arbitrary", "arbitrary"))(proj, proj, proj, proj, proj, gq, gk, sinks, bias, do)


def _bias_build(table, bucket, name):
    def body(tb_ref, bk_ref, o_ref):
        bk = bk_ref[...]
        tb = tb_ref[...]
        row = _iota((8, LANES), 0)
        col = _iota((8, LANES), 1)
        for h in range(SW_HEADS):
            acc = jnp.zeros((BLOCK, 2 * BLOCK), F32)
            for t in range(REL_BUCKETS):
                val = jnp.sum(jnp.where((row == h) & (col == t), tb, 0.0), keepdims=True)
                acc = jnp.where(bk == t, val, acc)
            o_ref[h] = acc

    return pl.pallas_call(
        body, name=name, out_shape=jax.ShapeDtypeStruct((SW_HEADS, BLOCK, 2 * BLOCK), F32))(table, bucket)


def _bias_grad(dbias, bucket, name):
    def body(db_ref, bk_ref, o_ref):
        bk = bk_ref[...]
        row = _iota((8, LANES), 0)
        col = _iota((8, LANES), 1)
        res = jnp.zeros((8, LANES), F32)
        for h in range(SW_HEADS):
            dbh = db_ref[h]
            for t in range(REL_BUCKETS):
                val = jnp.sum(jnp.where(bk == t, dbh, 0.0), keepdims=True)
                res = jnp.where((row == h) & (col == t), val, res)
        o_ref[...] = res

    return pl.pallas_call(body, name=name, out_shape=jax.ShapeDtypeStruct((8, LANES), F32))(dbias, bucket)


def _loss_grad(y, target, name):
    nb, s, d = y.shape
    tm = min(512, s)

    def body(y_ref, t_ref, loss_ref, dy_ref):
        @pl.when((pl.program_id(0) == 0) & (pl.program_id(1) == 0))
        def _():
            loss_ref[...] = jnp.zeros_like(loss_ref)

        e = y_ref[...] - t_ref[...]
        dy_ref[...] = e / d
        loss_ref[...] += 0.5 * jnp.sum(_rsum(e * e) / d, keepdims=True)

    tile = pl.BlockSpec((None, tm, d), lambda b, i: (b, i, 0))
    return pl.pallas_call(
        body, name=name, grid=(nb, s // tm), in_specs=[tile, tile],
        out_specs=[pl.BlockSpec((8, LANES), lambda b, i: (0, 0)), tile],
        out_shape=[jax.ShapeDtypeStruct((8, LANES), F32), jax.ShapeDtypeStruct((nb, s, d), F32)],
        compiler_params=_cp("arbitrary", "arbitrary"))(y, target)


def _adamw(parts, w, m, v, name):
    npart, r, ncol = parts.shape
    tr = _row_tile(r, ncol)
    bc1 = 1.0 - ADAM_B1 ** ADAM_STEP
    bc2 = 1.0 - ADAM_B2 ** ADAM_STEP

    def body(p_ref, w_ref, m_ref, v_ref, g_ref, d_ref, nm_ref, nv_ref):
        g = p_ref[0].astype(F32)
        for k in range(1, npart):
            g = g + p_ref[k].astype(F32)
        mn = ADAM_B1 * m_ref[...] + (1.0 - ADAM_B1) * g
        vn = ADAM_B2 * v_ref[...] + (1.0 - ADAM_B2) * (g * g)
        g_ref[...] = g
        nm_ref[...] = mn
        nv_ref[...] = vn
        d_ref[...] = -ADAM_LR * ((mn / bc1) / (jnp.sqrt(vn / bc2) + ADAM_EPS) + ADAM_WD * w_ref[...])

    tile = pl.BlockSpec((tr, ncol), lambda i: (i, 0))
    return pl.pallas_call(
        body, name=name, grid=(r // tr,),
        in_specs=[pl.BlockSpec((npart, tr, ncol), lambda i: (0, i, 0)), tile, tile, tile],
        out_specs=[tile] * 4, out_shape=[jax.ShapeDtypeStruct((r, ncol), F32)] * 4,
        compiler_params=_cp("parallel"))(parts, w, m, v)


def _unpack(flat, shapes, lead=()):
    out, off = [], 0
    for shp in shapes:
        size = 1
        for dim in shp:
            size *= dim
        out.append(flat[..., off:off + size].reshape(lead + tuple(shp)))
        off += size
    return out


def _t5_bucket():
    a = jnp.arange(BLOCK)[:, None]
    b = jnp.arange(2 * BLOCK)[None, :]
    dist = BLOCK + a - b
    max_exact = REL_BUCKETS // 2
    nn = jnp.maximum(dist, 0)
    nf = jnp.maximum(nn, 1).astype(F32)
    large = max_exact + (jnp.log(nf / max_exact) / math.log(BLOCK / max_exact)
                         * (REL_BUCKETS - max_exact)).astype(jnp.int32)
    large = jnp.minimum(large, REL_BUCKETS - 1)
    return jnp.where(nn < max_exact, nn, large).astype(jnp.int32)


def _pad_lanes(g, n):
    return jnp.pad(g, (0, n - g.shape[0])).reshape(1, n)


def kernel(x, c, positions, rel_table, norm1_g, norm2_g, w_ada, b_ada, w_in, mla_cq_g, w_uq, mla_ckv_g, w_ukv, mla_qn_g, mla_kn_g, sw_qn_g, sw_kn_g, sw_sinks, w_out, w_up, conv_w, conv_b, w_down, loss_target, m_rel_table, m_norm1_g, m_norm2_g, m_w_ada, m_b_ada, m_w_in, m_mla_cq_g, m_w_uq, m_mla_ckv_g, m_w_ukv, m_mla_qn_g, m_mla_kn_g, m_sw_qn_g, m_sw_kn_g, m_sw_sinks, m_w_out, m_w_up, m_conv_w, m_conv_b, m_w_down, v_rel_table, v_norm1_g, v_norm2_g, v_w_ada, v_b_ada, v_w_in, v_mla_cq_g, v_w_uq, v_mla_ckv_g, v_w_ukv, v_mla_qn_g, v_mla_kn_g, v_sw_qn_g, v_sw_kn_g, v_sw_sinks, v_w_out, v_w_up, v_conv_w, v_conv_b, v_w_down):
    nb, s, d = x.shape
    nl = norm1_g.shape[0]
    me = 4 * lax.axis_index("x") + 2 * lax.axis_index("y") + lax.axis_index("c")
    n_ada = w_ada.shape[2]

    shard = lambda w, l, transposed: (jnp.swapaxes(w[l], 0, 1) if transposed else w[l]).astype(MXU)
    attn_local = lambda l: [shard(w_in, l, True), shard(w_uq, l, True), shard(w_ukv, l, True), shard(w_out, l, False)]
    ffn_local = lambda l: [shard(w_up, l, True), shard(w_down, l, False)]
    full = lambda a: a.reshape(-1, a.shape[-1])
    zrows = lambda n: jnp.zeros((n, d), MXU)
    pad_in = lambda wt: jnp.concatenate([wt[:1152], wt[1184:1824], zrows(64), wt[1152:1184], zrows(160)], axis=0)
    pad_uq = lambda wt: jnp.pad(wt.reshape(MLA_HEADS, MLA_QK, 256), ((0, 0), (0, LANES - MLA_QK), (0, 0))).reshape(768, 256)
    got = _all_gather(attn_local(0) + [conv_w.reshape(-1, conv_w.shape[-1]), c], "gather_inputs")
    w_in_pt, w_uq_pt, w_ukv_t, w_out_f = [pad_in(full(got[0]))], [pad_uq(full(got[1]))], [full(got[2])], [full(got[3])]
    w_up_t, w_down_f = [], []
    conv_full = got[4].reshape(N_DEV, nl, 3, -1).transpose(1, 2, 0, 3).reshape(nl, 3, -1)
    c_all = got[5].reshape(N_DEV * nb, d)

    b_my = lax.dynamic_slice_in_dim(b_ada, me * n_ada, n_ada, axis=1).reshape(nl, 1, n_ada)
    mods_my = _ada_fwd(c_all, w_ada, b_my, "ada_fwd")
    mods, = _all_gather([mods_my.reshape(nl * N_DEV * nb, n_ada)], "gather_mods")
    mods = mods.reshape(N_DEV, nl, N_DEV * nb, n_ada).transpose(1, 2, 0, 3).reshape(nl, N_DEV * nb, N_DEV * n_ada)
    mods = lax.dynamic_slice_in_dim(mods, me * nb, nb, axis=1)
    shift1, scale1, gate1, shift2, scale2, gate2 = [mods[:, :, k * d:(k + 1) * d].reshape(nl, nb, 1, d) for k in range(6)]

    half = 16
    inv_freq = jnp.power(ROPE_THETA, -jnp.arange(half, dtype=F32) / half)
    ang = positions.astype(F32)[..., None] * inv_freq
    ones = lambda n: jnp.ones((nb, s, n), F32)
    zeros = lambda n: jnp.zeros((nb, s, n), F32)
    rope_c = jnp.concatenate([ones(64), jnp.cos(ang), jnp.cos(ang), ones(32)], axis=-1)
    rope_s = jnp.concatenate([zeros(64), jnp.sin(ang), jnp.sin(ang), zeros(32)], axis=-1)
    bucket = _t5_bucket()
    bias = _bias_build(jnp.pad(rel_table.T, ((0, 8 - SW_HEADS), (0, LANES - REL_BUCKETS))), bucket, "rel_bias")

    row = lambda g: g.reshape(1, -1)
    twice = lambda g: jnp.concatenate([g, g]).reshape(1, LANES)

    saved = []
    xl = x
    for l in range(nl):
        proj, h1 = _ln_mod_matmul(xl, row(norm1_g[l]), scale1[l], shift1[l], w_in_pt[l], f"l{l}_in_proj")
        prep_args = (proj, rope_c, rope_s, row(mla_cq_g[l]), row(mla_ckv_g[l]), _pad_lanes(mla_qn_g[l], LANES),
                     _pad_lanes(mla_kn_g[l], LANES), w_uq_pt[l], w_ukv_t[l])
        qm, km, vm = _mla_prep(*prep_args, f"l{l}_mla_prep")
        o_a, ct_a, cnt_a, up_g, down_g = _sb_fwd(proj, f"l{l}_sb_fwd", riding=ffn_local(l))
        w_up_t.append(full(up_g))
        w_down_f.append(full(down_g))
        o_b, lse_b, *nxt = _mla_fwd(qm, km, vm, f"l{l}_mla_fwd", riding=attn_local(l + 1) if l + 1 < nl else ())
        if nxt:
            w_in_pt.append(pad_in(full(nxt[0])))
            w_uq_pt.append(pad_uq(full(nxt[1])))
            w_ukv_t.append(full(nxt[2]))
            w_out_f.append(full(nxt[3]))
        sinks = jnp.broadcast_to(jnp.pad(sw_sinks[l], (0, 2))[:, None], (8, LANES))
        swa_args = (proj, twice(sw_qn_g[l]), twice(sw_kn_g[l]), sinks, bias)
        o_c = _swa_fwd(*swa_args, f"l{l}_swa_fwd")
        wo = [w_out_f[l][:256], w_out_f[l][256:640], w_out_f[l][640:]]
        x_mid, y1 = _out_proj([o_a, o_b, o_c], wo, gate1[l], xl, f"l{l}_out_proj")
        u_pre, h2 = _ln_mod_matmul(x_mid, row(norm2_g[l]), scale2[l], shift2[l], w_up_t[l], f"l{l}_up_proj")
        x_out, y2 = _conv_gate_matmul(u_pre, conv_full[l], row(conv_b[l]), w_down_f[l], gate2[l], x_mid, f"l{l}_ffn_down")
        saved.append(dict(x=xl, proj=proj, h1=h1, prep=prep_args, qkv=(qm, km, vm), o_a=o_a, ct_a=ct_a, cnt_a=cnt_a, o_b=o_b, lse_b=lse_b,
                          swa=swa_args, o_c=o_c, wo=wo, y1=y1, x_mid=x_mid, u_pre=u_pre, h2=h2, y2=y2))
        xl = x_out

    loss_blk, dx = _loss_grad(xl, loss_target, "loss")
    loss = lax.psum(loss_blk[0, 0], ("x", "y", "c"))

    t = nb * s
    flat = lambda a: a.reshape(t, a.shape[-1])
    grads = [None] * nl
    dmods = [None] * nl
    sharded_out = [None] * nl
    sharded_names = ["w_in", "w_uq", "w_ukv", "w_up", "w_out", "w_down", "conv_w"]
    sharded_wmv = dict(w_in=(w_in, m_w_in, v_w_in), w_uq=(w_uq, m_w_uq, v_w_uq), w_ukv=(w_ukv, m_w_ukv, v_w_ukv),
                       w_up=(w_up, m_w_up, v_w_up), w_out=(w_out, m_w_out, v_w_out), w_down=(w_down, m_w_down, v_w_down),
                       conv_w=(conv_w, m_conv_w, v_conv_w))
    n_in, n_up, n_out, n_dn = w_in.shape[2], w_up.shape[2], w_out.shape[1], w_down.shape[1]
    small_sizes = [w_uq[0].size, w_ukv[0].size, conv_w[0].size]
    n_small_rows = -(-sum(small_sizes) // d)
    rows_used = n_in + n_out + n_small_rows
    rows_grad = -(-rows_used // 16) * 16

    def pack_rows(mats, vecs):
        lead = mats[0].shape[:-2]
        flat_part = jnp.concatenate(vecs, axis=-1)
        flat_part = jnp.pad(flat_part, [(0, 0)] * len(lead) + [(0, n_small_rows * d - flat_part.shape[-1])])
        tail = jnp.zeros(lead + (rows_grad - rows_used, d), F32)
        return jnp.concatenate(list(mats) + [flat_part.reshape(lead + (n_small_rows, d)), tail], axis=-2)

    def unpack_rows(a):
        o1, o2 = n_in, n_in + n_out
        flat_part = a[o2:o2 + n_small_rows].reshape(-1)
        s1, s2, s3 = small_sizes[0], small_sizes[0] + small_sizes[1], sum(small_sizes)
        return dict(w_in=a[:o1].T, w_out=a[o1:o2],
                    w_uq=flat_part[:s1].reshape(w_uq.shape[2], -1).T, w_ukv=flat_part[s1:s2].reshape(w_ukv.shape[2], -1).T,
                    conv_w=flat_part[s2:s3].reshape(conv_w.shape[1:]))

    ffn_out = [None] * nl
    core = lax.axis_index("c").reshape(1).astype(jnp.int32)

    def update_ffn(l, recv):
        wmv = [{k: v[o][l] for k, v in sharded_wmv.items()} for o in range(3)]
        res_up = _adamw(recv[0], *[a["w_up"].T for a in wmv], f"l{l}_adamw_up")
        res_dn = _adamw(recv[1], *[a["w_down"] for a in wmv], f"l{l}_adamw_down")
        ffn_out[l] = [dict(w_up=ru.T, w_down=rd) for ru, rd in zip(res_up, res_dn)]

    def update_rest(l, recv):
        wmv = [{k: v[o][l] for k, v in sharded_wmv.items()} for o in range(3)]
        res_rest = _adamw(recv, *[pack_rows([a["w_in"].T, a["w_out"]], [a["w_uq"].T.reshape(-1), a["w_ukv"].T.reshape(-1),
                                                                         a["conv_w"].reshape(-1)]) for a in wmv],
                          f"l{l}_adamw_rest")
        sharded_out[l] = [dict(unpack_rows(rr), **ff) for rr, ff in zip(res_rest, ffn_out[l])]

    pending = None
    dbias = jnp.zeros((SW_HEADS, BLOCK, 2 * BLOCK), F32)
    for l in reversed(range(nl)):
        sv = saved[l]
        (da,), dy2, dgate2 = _gate_bwd_nt(dx, sv["y2"], gate2[l], [w_down_f[l]], f"l{l}_ffn_down_bwd")
        du, a_act, cstats = _conv_gate_bwd(da, sv["u_pre"], conv_full[l], row(conv_b[l]), f"l{l}_conv_gate_bwd")
        res = _ln_mod_matmul_bwd(du, w_up_t[l], sv["x_mid"], row(norm2_g[l]), scale2[l], dx, conv_full[l],
                                 f"l{l}_up_proj_bwd", riding=[pending[1]] if pending else ())
        dx_mid, du_pre, dshift2, dscale2, dg2 = res[:5]
        if pending:
            update_rest(pending[0], res[5])
            pending = None
        g_w_down = _wgrad(flat(a_act), flat(dy2), f"l{l}_w_down_grad")
        g_w_up_t = _wgrad(flat(du_pre), flat(sv["h2"]), f"l{l}_w_up_grad")
        per_dev = lambda g: g.reshape(N_DEV, -1, d)
        ffn_send = [per_dev(g_w_up_t), per_dev(g_w_down)]
        ffn_sib = _pair_exchange(ffn_send, f"l{l}_pair_exchange_ffn")
        ffn_pair = [_pair_add(core, a, b, f"l{l}_pair_add_{k}") for a, b, k in zip(ffn_send, ffn_sib, ("up", "down"))]

        (do_a, do_b, do_c), dy1, dgate1 = _gate_bwd_nt(dx_mid, sv["y1"], gate1[l], sv["wo"], f"l{l}_out_proj_bwd")
        mix = jnp.concatenate([sv["o_a"], sv["o_b"], sv["o_c"]], axis=-1).astype(MXU)
        g_w_out = _wgrad(flat(mix), flat(dy1), f"l{l}_w_out_grad")

        dsb_q, dsb_k, dsb_v = _sb_bwd(sv["proj"], sv["ct_a"], sv["cnt_a"], do_a, f"l{l}_sb_bwd")
        qm, km, vm = sv["qkv"]
        dqm, dkm, dvm, *ffn_recv = _mla_bwd(qm, km, vm, sv["o_b"], sv["lse_b"], do_b, f"l{l}_mla_bwd", riding=ffn_pair)
        update_ffn(l, ffn_recv)
        dsw_q, dkc, dkp, dvc, dvp, dbias_l, dsinks, dg_swq, dg_swk = _swa_bwd(*sv["swa"], do_c, f"l{l}_swa_bwd")
        dbias = dbias + dbias_l
        shift_up = lambda a: jnp.concatenate([a[:, BLOCK:], jnp.zeros((nb, BLOCK, LANES), F32)], axis=1)
        dsw_k = dkc + shift_up(dkp)
        dsw_v = dvc + shift_up(dvp)
        dcq, dckv, dkr, g_w_uq_pt, g_w_ukv_t, dg_cq, dg_ckv, dg_qn, dg_kn = _mla_prep_bwd(
            *sv["prep"], dqm, dkm, dvm, f"l{l}_mla_prep_bwd")
        dproj = jnp.concatenate([dsb_q, dsb_k, dsb_v, dcq, dckv, dsw_q, dsw_k, dsw_v, dkr, zeros(128)], axis=-1)
        dx, dproj_m, dshift1, dscale1, dg1 = _ln_mod_matmul_bwd(
            dproj, w_in_pt[l], sv["x"], row(norm1_g[l]), scale1[l], dx_mid, None, f"l{l}_in_proj_bwd")
        g_w_in_pt = _wgrad(flat(dproj_m), flat(sv["h1"]), f"l{l}_w_in_grad")

        g_w_in_t = jnp.concatenate([g_w_in_pt[:1152], g_w_in_pt[1856:1888], g_w_in_pt[1152:1792]], axis=0)
        g_w_uq_t = g_w_uq_pt.reshape(MLA_HEADS, LANES, 256)[:, :MLA_QK].reshape(MLA_HEADS * MLA_QK, 256)
        dmods[l] = jnp.concatenate([dshift1, dscale1, dgate1, dshift2, dscale2, dgate2], axis=-1).reshape(nb, 6 * d)

        conv_dev = cstats[1:4].reshape(3, N_DEV, -1).transpose(1, 0, 2)
        rest = pack_rows([per_dev(g_w_in_t), per_dev(g_w_out)],
                         [g_w_uq_t.reshape(N_DEV, -1), g_w_ukv_t.reshape(N_DEV, -1), conv_dev.reshape(N_DEV, -1)])
        rest_sib, = _pair_exchange([rest], f"l{l}_pair_exchange_rest")
        rest_pair = _pair_add(core, rest, rest_sib, f"l{l}_pair_add_rest")
        if l > 0:
            pending = (l, rest_pair)
        else:
            update_rest(l, _chip_exchange([rest_pair], f"l{l}_chip_exchange")[0])
        grads[l] = dict(
            norm1_g=dg1[0], norm2_g=dg2[0], mla_cq_g=dg_cq[0], mla_ckv_g=dg_ckv[0], mla_qn_g=dg_qn[0, :MLA_QK],
            mla_kn_g=dg_kn[0, :MLA_QK], sw_qn_g=dg_swq[0, :HEAD] + dg_swq[0, HEAD:], sw_kn_g=dg_swk[0, :HEAD] + dg_swk[0, HEAD:],
            sw_sinks=dsinks[:SW_HEADS, 0], conv_b=cstats[0])
    grad_x = dx
    g_rel = _bias_grad(dbias, bucket, "rel_table_grad")[:SW_HEADS, :REL_BUCKETS].T
    stack = lambda k: jnp.stack([grads[l][k] for l in range(nl)])

    dm_all, = _all_gather([jnp.stack(dmods).reshape(nl * nb, 6 * d)], "gather_dmods")
    dm_all = dm_all.reshape(N_DEV, nl, nb, 6 * d).transpose(1, 0, 2, 3).reshape(nl, N_DEV * nb, 6 * d)
    dm_my = lax.dynamic_slice_in_dim(dm_all, me * n_ada, n_ada, axis=2)
    g_w_ada, g_b_ada = _ada_bwd(c_all, dm_my, dm_all, "ada_bwd")
    g_b_ada = g_b_ada.reshape(nl, 6 * d)

    big_out = [{k: jnp.stack([sharded_out[l][o][k] for l in range(nl)]) for k in sharded_names} for o in range(4)]
    packf = lambda dct, names, rows: jnp.pad(jnp.concatenate([dct[k].reshape(-1) for k in names]),
                                             (0, rows * LANES - sum(dct[k].size for k in names))).reshape(rows, LANES)

    small_names = ["rel_table", "norm1_g", "norm2_g", "mla_cq_g", "mla_ckv_g", "mla_qn_g", "mla_kn_g",
                   "sw_qn_g", "sw_kn_g", "sw_sinks", "conv_b"]
    small_w = dict(rel_table=rel_table, norm1_g=norm1_g, norm2_g=norm2_g, mla_cq_g=mla_cq_g, mla_ckv_g=mla_ckv_g,
                   mla_qn_g=mla_qn_g, mla_kn_g=mla_kn_g, sw_qn_g=sw_qn_g, sw_kn_g=sw_kn_g, sw_sinks=sw_sinks, conv_b=conv_b)
    small_m = dict(rel_table=m_rel_table, norm1_g=m_norm1_g, norm2_g=m_norm2_g, mla_cq_g=m_mla_cq_g, mla_ckv_g=m_mla_ckv_g,
                   mla_qn_g=m_mla_qn_g, mla_kn_g=m_mla_kn_g, sw_qn_g=m_sw_qn_g, sw_kn_g=m_sw_kn_g, sw_sinks=m_sw_sinks, conv_b=m_conv_b)
    small_v = dict(rel_table=v_rel_table, norm1_g=v_norm1_g, norm2_g=v_norm2_g, mla_cq_g=v_mla_cq_g, mla_ckv_g=v_mla_ckv_g,
                   mla_qn_g=v_mla_qn_g, mla_kn_g=v_mla_kn_g, sw_qn_g=v_sw_qn_g, sw_kn_g=v_sw_kn_g, sw_sinks=v_sw_sinks, conv_b=v_conv_b)
    small_g = {k: (g_rel if k == "rel_table" else stack(k)) for k in small_names}
    n_small = sum(small_w[k].size for k in small_names)
    rows_small = -(-n_small // (8 * LANES)) * 8
    small_parts, = _all_gather([packf(small_g, small_names, rows_small)], "gather_small_grads")
    small_out = _adamw(small_parts, packf(small_w, small_names, rows_small), packf(small_m, small_names, rows_small),
                       packf(small_v, small_names, rows_small), "adamw_replicated")
    small_out = [dict(zip(small_names, _unpack(o.reshape(-1), [small_w[k].shape for k in small_names]))) for o in small_out]

    two_d = lambda a: a.reshape(-1, a.shape[-1])
    res_w = _adamw(two_d(g_w_ada)[None], two_d(w_ada), two_d(m_w_ada), two_d(v_w_ada), "adamw_w_ada")
    res_b = _adamw(g_b_ada[None], b_ada, m_b_ada, v_b_ada, "adamw_b_ada")
    ada_out = [dict(w_ada=rw.reshape(w_ada.shape), b_ada=rb) for rw, rb in zip(res_w, res_b)]

    order = ["rel_table", "norm1_g", "norm2_g", "w_ada", "b_ada", "w_in", "mla_cq_g", "w_uq", "mla_ckv_g", "w_ukv",
             "mla_qn_g", "mla_kn_g", "sw_qn_g", "sw_kn_g", "sw_sinks", "w_out", "w_up", "conv_w", "conv_b", "w_down"]
    outs = [{**big_out[k], **small_out[k], **ada_out[k]} for k in range(4)]
    return (loss, grad_x, *[outs[0][n] for n in order], *[outs[1][n] for n in order],
            *[outs[2][n] for n in order], *[outs[3][n] for n in order])
```

```python
import math

import jax
import jax.numpy as jnp
from jax import lax
from jax.experimental import pallas as pl
from jax.experimental.pallas import tpu as pltpu

F32 = jnp.float32
MXU = jnp.bfloat16
EPS = 1e-6
NEG = -1e30
VMEM_LIMIT_BYTES = 56 * 1024 * 1024
N_DEV = 8
MESH = pl.DeviceIdType.MESH

D_MODEL = 1024
D_FF = 2816
HEAD = 64
LANES = 128
MLA_HEADS = 6
MLA_QK = 96
SW_HEADS = 6
REL_BUCKETS = 32
BLOCK = 128
SB_SCALE = HEAD ** -0.5
SB_DEAD = -105.0
SW_SCALE = HEAD ** -0.5
MLA_SCALE = MLA_QK ** -0.5
ROPE_THETA = 10000.0
D_IN_PAD = 2048
COL_SBQ, COL_SBK, COL_SBV, COL_CQ, COL_CKV, COL_SWQ, COL_SWK, COL_SWV, COL_KR = 0, 256, 512, 768, 1024, 1152, 1536, 1664, 1792

HALO = 16
ROW_TILE_BYTES = 1 << 20
ADAM_LR, ADAM_B1, ADAM_B2, ADAM_EPS, ADAM_WD, ADAM_STEP = 0.001, 0.9, 0.999, 1e-08, 0.01, 10


def _cp(*sem):
    return pltpu.CompilerParams(dimension_semantics=sem, vmem_limit_bytes=VMEM_LIMIT_BYTES)


def _iota(shape, dim):
    return lax.broadcasted_iota(jnp.int32, shape, dim)


def _dot(a, b):
    return jnp.dot(a, b, preferred_element_type=F32)


def _dot_nt(a, b):
    return lax.dot_general(a, b, (((1,), (1,)), ((), ())), preferred_element_type=F32)


def _dot_tn(a, b):
    return lax.dot_general(a, b, (((0,), (0,)), ((), ())), preferred_element_type=F32)


def _cumdot(x, u):
    hi = x.astype(MXU)
    mid = (x - hi.astype(F32)).astype(MXU)
    return _dot(hi, u) + _dot(mid, u)


def _sigmoid(x):
    return 1.0 / (1.0 + jnp.exp(-x))


def _rsum(x):
    return jnp.sum(x, axis=-1, keepdims=True)


def _csum(x):
    return jnp.sum(x, axis=0, keepdims=True)


def _all_gather(xs, name):
    na = len(xs)

    def body(*refs):
        start, finish = _gather_steps(refs[:na], refs[na:2 * na], *refs[2 * na:])
        start()
        finish()

    hbm = pl.BlockSpec(memory_space=pl.ANY)
    return pl.pallas_call(
        body, name=name, out_shape=_gather_out_shapes(xs), in_specs=[hbm] * na, out_specs=[hbm] * na,
        scratch_shapes=_gather_sems(na))(*xs)


def _gather_out_shapes(xs):
    return [jax.ShapeDtypeStruct((N_DEV,) + a.shape, a.dtype) for a in xs]


def _gather_sems(na):
    return [pltpu.SemaphoreType.DMA((7 * na,)), pltpu.SemaphoreType.DMA((7 * na,)), pltpu.SemaphoreType.DMA((na,))]


def _gather_steps(x_refs, out_refs, send_sems, recv_sems, local_sems):
    na = len(x_refs)
    x, y, c = lax.axis_index("x"), lax.axis_index("y"), lax.axis_index("c")
    me, sibling = (x, y, c), (x, y, 1 - c)
    chips = [(1 - x, y), (x, 1 - y), (1 - x, 1 - y)]

    def slot(a, px, py, pc):
        return out_refs[a].at[4 * px + 2 * py + pc]

    def copy(a, k, block, to, src=None):
        return pltpu.make_async_remote_copy(
            src_ref=slot(a, *block) if src is None else src, dst_ref=slot(a, *block),
            send_sem=send_sems.at[7 * a + k], recv_sem=recv_sems.at[7 * a + k], device_id=to, device_id_type=MESH)

    def own_copies(a):
        return ([copy(a, 0, me, sibling, src=x_refs[a])]
                + [copy(a, 1 + j, me, (*chip, c), src=x_refs[a]) for j, chip in enumerate(chips)])

    def local_copy(a):
        return pltpu.make_async_copy(x_refs[a], slot(a, *me), local_sems.at[a])

    def start():
        for a in range(na):
            local_copy(a).start()
            for cp in own_copies(a):
                cp.start()

    def finish():
        passed = []
        for j, chip in enumerate(chips):
            for a in range(na):
                copy(a, 1 + j, (*chip, c), me).wait_recv()
                passed.append(copy(a, 4 + j, (*chip, c), sibling))
                passed[-1].start()
        for a in range(na):
            copy(a, 0, sibling, me).wait_recv()
            for j, chip in enumerate(chips):
                copy(a, 4 + j, (*chip, 1 - c), me).wait_recv()
        for a in range(na):
            for cp in own_copies(a):
                cp.wait_send()
        for cp in passed:
            cp.wait_send()
        for a in range(na):
            local_copy(a).wait()

    return start, finish


def _with_gather(body, n_in, n_out, na, grid):
    if not na:
        return body

    def wrapped(*refs):
        ins, ride_in = refs[:n_in], refs[n_in:n_in + na]
        outs = refs[n_in + na:n_in + na + n_out]
        ride_out = refs[n_in + na + n_out:n_in + 2 * na + n_out]
        ids = [pl.program_id(k) for k in range(len(grid))]
        first, last = ids[0] == 0, ids[0] == grid[0] - 1
        for k in range(1, len(grid)):
            first, last = first & (ids[k] == 0), last & (ids[k] == grid[k] - 1)
        start, finish = _gather_steps(ride_in, ride_out, *refs[n_in + 2 * na + n_out:])
        pl.when(first)(start)
        body(*ins, *outs)
        pl.when(last)(finish)

    return wrapped


def _pair_exchange(xs, name):
    na = len(xs)

    def body(*refs):
        x_refs, out_refs = refs[:na], refs[na:2 * na]
        send_sems, recv_sems = refs[2 * na:]
        x, y, c = lax.axis_index("x"), lax.axis_index("y"), lax.axis_index("c")
        copies = []
        for a in range(na):
            for q in range(4):
                copies.append(pltpu.make_async_remote_copy(
                    src_ref=x_refs[a].at[2 * q + 1 - c], dst_ref=out_refs[a].at[q],
                    send_sem=send_sems.at[4 * a + q], recv_sem=recv_sems.at[4 * a + q],
                    device_id=(x, y, 1 - c), device_id_type=MESH))
                copies[-1].start()
        for cp in copies:
            cp.wait()

    hbm = pl.BlockSpec(memory_space=pl.ANY)
    return pl.pallas_call(
        body, name=name, out_shape=[jax.ShapeDtypeStruct((4,) + a.shape[1:], a.dtype) for a in xs],
        in_specs=[hbm] * na, out_specs=[hbm] * na,
        scratch_shapes=[pltpu.SemaphoreType.DMA((4 * na,)), pltpu.SemaphoreType.DMA((4 * na,))])(*xs)


def _row_tile(r, ncol):
    if r * ncol * 4 <= ROW_TILE_BYTES:
        return r
    return max(t for t in range(16, r, 16) if r % t == 0 and t * ncol * 4 <= ROW_TILE_BYTES)


def _pair_add(core, xs, sib, name):
    _, r, ncol = xs.shape
    tr = _row_tile(r, ncol)

    def body(c_ref, x_ref, s_ref, o_ref):
        o_ref[...] = (x_ref[...] + s_ref[...]).astype(MXU)

    return pl.pallas_call(
        body, name=name,
        grid_spec=pltpu.PrefetchScalarGridSpec(
            num_scalar_prefetch=1, grid=(4, r // tr),
            in_specs=[pl.BlockSpec((None, tr, ncol), lambda q, i, c_ref: (2 * q + c_ref[0], i, 0)),
                      pl.BlockSpec((None, tr, ncol), lambda q, i, c_ref: (q, i, 0))],
            out_specs=pl.BlockSpec((None, tr, ncol), lambda q, i, c_ref: (q, i, 0))),
        out_shape=jax.ShapeDtypeStruct((4, r, ncol), MXU),
        compiler_params=_cp("parallel", "parallel"))(core, xs, sib)


def _chip_exchange(xs, name):
    na = len(xs)

    def body(*refs):
        copies = _chip_exchange_copies(refs[:na], refs[na:2 * na], *refs[2 * na:])
        for cp in copies:
            cp.start()
        for cp in copies:
            cp.wait()

    hbm = pl.BlockSpec(memory_space=pl.ANY)
    return pl.pallas_call(
        body, name=name, out_shape=[jax.ShapeDtypeStruct(a.shape, a.dtype) for a in xs],
        in_specs=[hbm] * na, out_specs=[hbm] * na, scratch_shapes=_chip_exchange_sems(na))(*xs)


def _with_chip_exchange(body, n_in, n_out, na, grid):
    if not na:
        return body

    def wrapped(*refs):
        ins, ride_in = refs[:n_in], refs[n_in:n_in + na]
        outs = refs[n_in + na:n_in + na + n_out]
        ride_out = refs[n_in + na + n_out:n_in + 2 * na + n_out]
        ids = [pl.program_id(k) for k in range(len(grid))]
        first, last = ids[0] == 0, ids[0] == grid[0] - 1
        for k in range(1, len(grid)):
            first, last = first & (ids[k] == 0), last & (ids[k] == grid[k] - 1)
        copies = _chip_exchange_copies(ride_in, ride_out, *refs[n_in + 2 * na + n_out:])

        @pl.when(first)
        def _():
            for cp in copies:
                cp.start()

        body(*ins, *outs)

        @pl.when(last)
        def _():
            for cp in copies:
                cp.wait()

    return wrapped


def _chip_exchange_sems(na):
    return [pltpu.SemaphoreType.DMA((3 * na,)), pltpu.SemaphoreType.DMA((3 * na,)), pltpu.SemaphoreType.DMA((na,))]


def _chip_exchange_copies(x_refs, out_refs, send_sems, recv_sems, local_sems):
    x, y, c = lax.axis_index("x"), lax.axis_index("y"), lax.axis_index("c")
    me = 2 * x + y
    copies = [pltpu.make_async_copy(x_refs[a].at[me], out_refs[a].at[me], local_sems.at[a]) for a in range(len(x_refs))]
    for k, (dx, dy) in enumerate([(1, 0), (0, 1), (1, 1)]):
        px = 1 - x if dx else x
        py = 1 - y if dy else y
        for a in range(len(x_refs)):
            copies.append(pltpu.make_async_remote_copy(
                src_ref=x_refs[a].at[2 * px + py], dst_ref=out_refs[a].at[me],
                send_sem=send_sems.at[3 * a + k], recv_sem=recv_sems.at[3 * a + k],
                device_id=(px, py, c), device_id_type=MESH))
    return copies


def _ada_fwd(c_all, w_ada, b_my, name):
    nl, d, n = w_ada.shape
    nb = c_all.shape[0]

    def body(c_ref, w_ref, b_ref, o_ref):
        cv = c_ref[...]
        sc = (cv * _sigmoid(cv)).astype(MXU)
        o_ref[...] = _dot(sc, w_ref[...].astype(MXU)) + b_ref[...]

    return pl.pallas_call(
        body, name=name, grid=(nl,),
        in_specs=[pl.BlockSpec((nb, d), lambda l: (0, 0)),
                  pl.BlockSpec((None, d, n), lambda l: (l, 0, 0)),
                  pl.BlockSpec((None, 1, n), lambda l: (l, 0, 0))],
        out_specs=pl.BlockSpec((None, nb, n), lambda l: (l, 0, 0)),
        out_shape=jax.ShapeDtypeStruct((nl, nb, n), F32),
        compiler_params=_cp("parallel"))(c_all, w_ada, b_my)


def _ada_bwd(c_all, dmods_my, dmods_all, name):
    nl, nb, n = dmods_my.shape
    d = c_all.shape[1]
    nfull = dmods_all.shape[2]

    def body(c_ref, dm_ref, da_ref, dw_ref, db_ref):
        cv = c_ref[...]
        sc = (cv * _sigmoid(cv)).astype(MXU)
        dw_ref[...] = _dot_tn(sc, dm_ref[...].astype(MXU))
        db_ref[...] = _csum(da_ref[...])

    return pl.pallas_call(
        body, name=name, grid=(nl,),
        in_specs=[pl.BlockSpec((nb, d), lambda l: (0, 0)),
                  pl.BlockSpec((None, nb, n), lambda l: (l, 0, 0)),
                  pl.BlockSpec((None, nb, nfull), lambda l: (l, 0, 0))],
        out_specs=[pl.BlockSpec((None, d, n), lambda l: (l, 0, 0)),
                   pl.BlockSpec((None, 1, nfull), lambda l: (l, 0, 0))],
        out_shape=[jax.ShapeDtypeStruct((nl, d, n), F32), jax.ShapeDtypeStruct((nl, 1, nfull), F32)],
        compiler_params=_cp("parallel"))(c_all, dmods_my, dmods_all)


def _ln_mod_matmul(x, g, scale, shift, w, name):
    nb, s, d = x.shape
    n = w.shape[0]
    tm, tn = min(1024, s), (1408 if n % 1408 == 0 else 1024)

    def body(x_ref, g_ref, sc_ref, sh_ref, w_ref, y_ref, h_ref, h_s):
        @pl.when(pl.program_id(2) == 0)
        def _():
            xf = x_ref[...]
            rstd = lax.rsqrt(jnp.mean(xf * xf, axis=-1, keepdims=True) + EPS)
            hv = (xf * rstd * g_ref[...]) * (1.0 + sc_ref[...]) + sh_ref[...]
            h_s[...] = hv.astype(MXU)
            h_ref[...] = h_s[...]

        y_ref[...] = _dot_nt(h_s[...], w_ref[...]).astype(MXU)

    return pl.pallas_call(
        body, name=name, grid=(nb, s // tm, n // tn),
        in_specs=[pl.BlockSpec((None, tm, d), lambda b, i, j: (b, i, 0)),
                  pl.BlockSpec((1, d), lambda b, i, j: (0, 0)),
                  pl.BlockSpec((None, 1, d), lambda b, i, j: (b, 0, 0)),
                  pl.BlockSpec((None, 1, d), lambda b, i, j: (b, 0, 0)),
                  pl.BlockSpec((tn, d), lambda b, i, j: (j, 0))],
        out_specs=[pl.BlockSpec((None, tm, tn), lambda b, i, j: (b, i, j)),
                   pl.BlockSpec((None, tm, d), lambda b, i, j: (b, i, 0))],
        out_shape=[jax.ShapeDtypeStruct((nb, s, n), MXU), jax.ShapeDtypeStruct((nb, s, d), MXU)],
        scratch_shapes=[pltpu.VMEM((tm, d), MXU)],
        compiler_params=_cp("parallel", "parallel", "arbitrary"))(x, g, scale, shift, w)


def _ln_mod_matmul_bwd(dy, w, x, g, scale, dres, conv_w, name, riding=()):
    nb, s, n = dy.shape
    d = x.shape[-1]
    tm, tn = min(512, s), (1408 if n % 1408 == 0 else 1024)
    ni, nj = s // tm, n // tn
    hb = tm // HALO
    conv = conv_w is not None
    na = len(riding)

    def body(*refs):
        if conv:
            dy_ref, nx_ref, cw_ref = refs[:3]
            refs = refs[3:]
        else:
            dy_ref = refs[0]
            refs = refs[1:]
        w_ref, x_ref, g_ref, sc_ref, dr_ref = refs[:5]
        ride_in, refs = refs[5:5 + na], refs[5 + na:]
        dx_ref, dyp_ref, dsh_ref, dsc_ref, dg_ref = refs[:5]
        ride_out, refs = refs[5:5 + na], refs[5 + na:]
        acc = refs[0]
        b, i, j = pl.program_id(0), pl.program_id(1), pl.program_id(2)
        if na:
            copies = _chip_exchange_copies(ride_in, ride_out, *refs[1:])

            @pl.when((b == 0) & (i == 0) & (j == 0))
            def _():
                for cp in copies:
                    cp.start()

        @pl.when(j == 0)
        def _():
            acc[...] = jnp.zeros_like(acc)

        @pl.when((j == 0) & (i == 0))
        def _():
            dsh_ref[...] = jnp.zeros_like(dsh_ref)
            dsc_ref[...] = jnp.zeros_like(dsc_ref)

        @pl.when((j == 0) & (i == 0) & (b == 0))
        def _():
            dg_ref[...] = jnp.zeros_like(dg_ref)

        dv = dy_ref[...].astype(F32)
        if conv:
            rows = _iota((tm, 1), 0)
            nx = jnp.where(i == ni - 1, 0.0, nx_ref[...].astype(F32))
            n1 = jnp.where(rows == tm - 1, nx[0:1, :], pltpu.roll(dv, tm - 1, 0))
            n2 = jnp.where(rows == tm - 2, nx[0:1, :], jnp.where(rows == tm - 1, nx[1:2, :], pltpu.roll(dv, tm - 2, 0)))
            cw = cw_ref[...]
            dv = cw[2:3, :] * dv + cw[1:2, :] * n1 + cw[0:1, :] * n2
        dp = dv.astype(MXU)
        dyp_ref[...] = dp
        acc[...] += _dot(dp, w_ref[...])

        @pl.when(j == nj - 1)
        def _():
            dh = acc[...]
            xf = x_ref[...]
            rstd = lax.rsqrt(jnp.mean(xf * xf, axis=-1, keepdims=True) + EPS)
            xn = xf * rstd
            gg = g_ref[...]
            sc1 = 1.0 + sc_ref[...]
            dsh_ref[...] += _csum(dh)
            dsc_ref[...] += _csum(dh * xn * gg)
            dg_ref[...] += _csum(dh * xn * sc1)
            dn = dh * gg * sc1
            dx_ref[...] = dr_ref[...] + rstd * (dn - xn * jnp.mean(dn * xn, axis=-1, keepdims=True))

        if na:
            @pl.when((b == nb - 1) & (i == ni - 1) & (j == nj - 1))
            def _():
                for cp in copies:
                    cp.wait()

    hbm = pl.BlockSpec(memory_space=pl.ANY)
    in_specs = [pl.BlockSpec((None, tm, tn), lambda b, i, j: (b, i, j))]
    args = [dy]
    if conv:
        in_specs += [pl.BlockSpec((None, HALO, tn), lambda b, i, j: (b, jnp.minimum((i + 1) * hb, s // HALO - 1), j)),
                     pl.BlockSpec((3, tn), lambda b, i, j: (0, j))]
        args += [dy, conv_w]
    in_specs += [pl.BlockSpec((tn, d), lambda b, i, j: (j, 0)),
                 pl.BlockSpec((None, tm, d), lambda b, i, j: (b, i, 0)),
                 pl.BlockSpec((1, d), lambda b, i, j: (0, 0)),
                 pl.BlockSpec((None, 1, d), lambda b, i, j: (b, 0, 0)),
                 pl.BlockSpec((None, tm, d), lambda b, i, j: (b, i, 0))]
    in_specs += [hbm] * na
    args += [w, x, g, scale, dres, *riding]
    return pl.pallas_call(
        body, name=name, grid=(nb, ni, nj), in_specs=in_specs,
        out_specs=[pl.BlockSpec((None, tm, d), lambda b, i, j: (b, i, 0)),
                   pl.BlockSpec((None, tm, tn), lambda b, i, j: (b, i, j)),
                   pl.BlockSpec((None, 1, d), lambda b, i, j: (b, 0, 0)),
                   pl.BlockSpec((None, 1, d), lambda b, i, j: (b, 0, 0)),
                   pl.BlockSpec((1, d), lambda b, i, j: (0, 0))] + [hbm] * na,
        out_shape=[jax.ShapeDtypeStruct((nb, s, d), F32), jax.ShapeDtypeStruct((nb, s, n), MXU),
                   jax.ShapeDtypeStruct((nb, 1, d), F32), jax.ShapeDtypeStruct((nb, 1, d), F32),
                   jax.ShapeDtypeStruct((1, d), F32)] + [jax.ShapeDtypeStruct(a.shape, a.dtype) for a in riding],
        scratch_shapes=[pltpu.VMEM((tm, d), F32)] + (_chip_exchange_sems(na) if na else []),
        compiler_params=_cp("arbitrary", "arbitrary", "arbitrary"))(*args)


def _wgrad(xm, dym, name):
    t, k = xm.shape
    n = dym.shape[1]
    tk = 1408 if k % 1408 == 0 else 1024
    tt = min(1024, t)

    def body(x_ref, dy_ref, o_ref):
        @pl.when(pl.program_id(1) == 0)
        def _():
            o_ref[...] = jnp.zeros_like(o_ref)

        o_ref[...] += _dot_tn(x_ref[...], dy_ref[...])

    return pl.pallas_call(
        body, name=name, grid=(k // tk, t // tt),
        in_specs=[pl.BlockSpec((tt, tk), lambda a, c: (c, a)),
                  pl.BlockSpec((tt, n), lambda a, c: (c, 0))],
        out_specs=pl.BlockSpec((tk, n), lambda a, c: (a, 0)),
        out_shape=jax.ShapeDtypeStruct((k, n), F32),
        compiler_params=_cp("parallel", "arbitrary"))(xm, dym)


def _out_proj(parts, ws, gate, res, name):
    nb, s, d = res.shape
    tm = min(512, s)
    npart = len(parts)

    def body(*refs):
        p_refs, w_refs = refs[:npart], refs[npart:2 * npart]
        gt_ref, res_ref, xo_ref, y_ref = refs[2 * npart:]
        y = _dot(p_refs[0][...].astype(MXU), w_refs[0][...])
        for p_ref, w_ref in zip(p_refs[1:], w_refs[1:]):
            y = y + _dot(p_ref[...].astype(MXU), w_ref[...])
        y_ref[...] = y
        xo_ref[...] = res_ref[...] + gt_ref[...] * y

    in_specs = [pl.BlockSpec((None, tm, p.shape[-1]), lambda b, i: (b, i, 0)) for p in parts]
    in_specs += [pl.BlockSpec(w.shape, lambda b, i: (0, 0)) for w in ws]
    in_specs += [pl.BlockSpec((None, 1, d), lambda b, i: (b, 0, 0)),
                 pl.BlockSpec((None, tm, d), lambda b, i: (b, i, 0))]
    return pl.pallas_call(
        body, name=name, grid=(nb, s // tm), in_specs=in_specs,
        out_specs=[pl.BlockSpec((None, tm, d), lambda b, i: (b, i, 0))] * 2,
        out_shape=[jax.ShapeDtypeStruct((nb, s, d), F32)] * 2,
        compiler_params=_cp("parallel", "parallel"))(*parts, *ws, gate, res)


def _gate_bwd_nt(dx, y, gate, ws, name):
    nb, s, d = dx.shape
    tm = min(512, s)
    npart = len(ws)

    def body(*refs):
        dx_ref, y_ref, gt_ref = refs[:3]
        w_refs = refs[3:3 + npart]
        da_refs = refs[3 + npart:3 + 2 * npart]
        dy_ref, dgt_ref = refs[3 + 2 * npart:]

        @pl.when(pl.program_id(1) == 0)
        def _():
            dgt_ref[...] = jnp.zeros_like(dgt_ref)

        dxv = dx_ref[...]
        dyv = (dxv * gt_ref[...]).astype(MXU)
        dy_ref[...] = dyv
        dgt_ref[...] += _csum(dxv * y_ref[...])
        for w_ref, da_ref in zip(w_refs, da_refs):
            da_ref[...] = _dot_nt(dyv, w_ref[...])

    tile = pl.BlockSpec((None, tm, d), lambda b, i: (b, i, 0))
    row = pl.BlockSpec((None, 1, d), lambda b, i: (b, 0, 0))
    outs = pl.pallas_call(
        body, name=name, grid=(nb, s // tm),
        in_specs=[tile, tile, row] + [pl.BlockSpec(w.shape, lambda b, i: (0, 0)) for w in ws],
        out_specs=[pl.BlockSpec((None, tm, w.shape[0]), lambda b, i: (b, i, 0)) for w in ws] + [tile, row],
        out_shape=[jax.ShapeDtypeStruct((nb, s, w.shape[0]), F32) for w in ws]
        + [jax.ShapeDtypeStruct((nb, s, d), MXU), jax.ShapeDtypeStruct((nb, 1, d), F32)],
        compiler_params=_cp("arbitrary", "arbitrary"))(dx, y, gate, *ws)
    return outs[:npart], outs[npart], outs[npart + 1]


def _conv_shifts(xv, halo, rows):
    last, before = halo[HALO - 1:HALO, :], halo[HALO - 2:HALO - 1, :]
    p1 = jnp.where(rows == 0, last, pltpu.roll(xv, 1, 0))
    p2 = jnp.where(rows == 0, before, jnp.where(rows == 1, last, pltpu.roll(xv, 2, 0)))
    return p1, p2


def _conv_gate_matmul(u, cw, cb, wd, gate, res, name):
    nb, s, f2 = u.shape
    f = f2 // 2
    d = wd.shape[1]
    tm = min(512, s)
    tk = f // 2
    nk = f // tk
    hb = tm // HALO

    def body(ug_ref, uv_ref, hg_ref, hv_ref, cwg_ref, cwv_ref, cbg_ref, cbv_ref, wd_ref, gt_ref, res_ref,
             xo_ref, y_ref, acc):
        i, k = pl.program_id(1), pl.program_id(2)

        @pl.when(k == 0)
        def _():
            acc[...] = jnp.zeros_like(acc)

        rows = _iota((tm, 1), 0)

        def conv(x_ref, h_ref, w_ref, b_ref):
            xv = x_ref[...].astype(F32)
            halo = jnp.where(i == 0, 0.0, h_ref[...].astype(F32))
            p1, p2 = _conv_shifts(xv, halo, rows)
            wv = w_ref[...]
            return wv[2:3, :] * xv + wv[1:2, :] * p1 + wv[0:1, :] * p2 + b_ref[...]

        gv = conv(ug_ref, hg_ref, cwg_ref, cbg_ref)
        vv = conv(uv_ref, hv_ref, cwv_ref, cbv_ref)
        av = gv * _sigmoid(gv) * vv
        acc[...] += _dot(av.astype(MXU), wd_ref[...])

        @pl.when(k == nk - 1)
        def _():
            y = acc[...]
            y_ref[...] = y
            xo_ref[...] = res_ref[...] + gt_ref[...] * y

    def halo_idx(off):
        return lambda b, i, k: (b, jnp.maximum(i * hb - 1, 0), k + off)

    tile = pl.BlockSpec((None, tm, d), lambda b, i, k: (b, i, 0))
    return pl.pallas_call(
        body, name=name, grid=(nb, s // tm, nk),
        in_specs=[pl.BlockSpec((None, tm, tk), lambda b, i, k: (b, i, k)),
                  pl.BlockSpec((None, tm, tk), lambda b, i, k: (b, i, k + nk)),
                  pl.BlockSpec((None, HALO, tk), halo_idx(0)),
                  pl.BlockSpec((None, HALO, tk), halo_idx(nk)),
                  pl.BlockSpec((3, tk), lambda b, i, k: (0, k)),
                  pl.BlockSpec((3, tk), lambda b, i, k: (0, k + nk)),
                  pl.BlockSpec((1, tk), lambda b, i, k: (0, k)),
                  pl.BlockSpec((1, tk), lambda b, i, k: (0, k + nk)),
                  pl.BlockSpec((tk, d), lambda b, i, k: (k, 0)),
                  pl.BlockSpec((None, 1, d), lambda b, i, k: (b, 0, 0)),
                  tile],
        out_specs=[tile, tile],
        out_shape=[jax.ShapeDtypeStruct((nb, s, d), F32)] * 2,
        scratch_shapes=[pltpu.VMEM((tm, d), F32)],
        compiler_params=_cp("parallel", "parallel", "arbitrary"))(u, u, u, u, cw, cw, cb, cb, wd, gate, res)


def _conv_gate_bwd(da, u, cw, cb, name):
    nb, s, f2 = u.shape
    f = f2 // 2
    tm = min(128, s)
    hb = tm // HALO

    def body(da_ref, u_ref, h_ref, cw_ref, cb_ref, du_ref, a_ref, st_ref):
        b, i = pl.program_id(0), pl.program_id(1)

        @pl.when((b == 0) & (i == 0))
        def _():
            st_ref[...] = jnp.zeros_like(st_ref)

        rows = _iota((tm, 1), 0)
        first = i == 0

        def conv(cs):
            xv = u_ref[:, cs].astype(F32)
            halo = jnp.where(first, 0.0, h_ref[:, cs].astype(F32))
            p1, p2 = _conv_shifts(xv, halo, rows)
            wv = cw_ref[:, cs]
            return xv, p1, p2, wv[2:3, :] * xv + wv[1:2, :] * p1 + wv[0:1, :] * p2 + cb_ref[:, cs]

        def stats(cs, du, xv, p1, p2):
            du_ref[:, cs] = du.astype(MXU)
            st_ref[0:1, cs] += _csum(du)
            st_ref[1:2, cs] += _csum(du * p2)
            st_ref[2:3, cs] += _csum(du * p1)
            st_ref[3:4, cs] += _csum(du * xv)

        for k in range(f // LANES):
            cg = slice(k * LANES, (k + 1) * LANES)
            cv = slice(f + k * LANES, f + (k + 1) * LANES)
            xg, g1, g2, gv = conv(cg)
            xv, v1, v2, vv = conv(cv)
            sg = _sigmoid(gv)
            sl = gv * sg
            a_ref[:, cg] = (sl * vv).astype(MXU)
            dav = da_ref[:, cg]
            stats(cg, dav * vv * (sg * (1.0 + gv * (1.0 - sg))), xg, g1, g2)
            stats(cv, dav * sl, xv, v1, v2)

    return pl.pallas_call(
        body, name=name, grid=(nb, s // tm),
        in_specs=[pl.BlockSpec((None, tm, f), lambda b, i: (b, i, 0)),
                  pl.BlockSpec((None, tm, f2), lambda b, i: (b, i, 0)),
                  pl.BlockSpec((None, HALO, f2), lambda b, i: (b, jnp.maximum(i * hb - 1, 0), 0)),
                  pl.BlockSpec((3, f2), lambda b, i: (0, 0)),
                  pl.BlockSpec((1, f2), lambda b, i: (0, 0))],
        out_specs=[pl.BlockSpec((None, tm, f2), lambda b, i: (b, i, 0)),
                   pl.BlockSpec((None, tm, f), lambda b, i: (b, i, 0)),
                   pl.BlockSpec((8, f2), lambda b, i: (0, 0))],
        out_shape=[jax.ShapeDtypeStruct((nb, s, f2), MXU), jax.ShapeDtypeStruct((nb, s, f), MXU),
                   jax.ShapeDtypeStruct((8, f2), F32)],
        compiler_params=_cp("arbitrary", "arbitrary"))(da, u, u, cw, cb)


def _rot(xv, lane):
    return jnp.where((lane >= 64) & (lane < 80), -pltpu.roll(xv, 112, 1),
                     jnp.where((lane >= 80) & (lane < 96), pltpu.roll(xv, 16, 1), 0.0))


def _rot_t(dv, lane):
    return jnp.where((lane >= 80) & (lane < 96), -pltpu.roll(dv, 16, 1),
                     jnp.where((lane >= 64) & (lane < 80), pltpu.roll(dv, 112, 1), 0.0))


def _mla_prep_specs(s, tm):
    def blk(width, col):
        return pl.BlockSpec((None, tm, width), lambda b, i: (b, i, col // width))

    full = lambda shape: pl.BlockSpec(shape, lambda b, i: (0, 0))
    return [blk(256, COL_CQ), blk(128, COL_CKV), blk(128, COL_KR),
            pl.BlockSpec((None, tm, LANES), lambda b, i: (b, i, 0)),
            pl.BlockSpec((None, tm, LANES), lambda b, i: (b, i, 0)),
            full((1, 256)), full((1, 128)), full((1, 128)), full((1, 128)),
            full((768, 256)), full((768, 128))]


def _mla_prep(proj, cs, sn, gcq, gckv, gqn, gkn, wuq, wukv, name):
    nb, s, _ = proj.shape
    tm = min(256, s)

    def body(cq_ref, ckv_ref, kr_ref, c_ref, s_ref, gcq_ref, gckv_ref, gqn_ref, gkn_ref, wuq_ref, wukv_ref,
             q_ref, k_ref, v_ref):
        lane = _iota((tm, LANES), 1)
        cv, sv = c_ref[...], s_ref[...]
        cq = cq_ref[...].astype(F32)
        cqn = cq * lax.rsqrt(jnp.mean(cq * cq, axis=-1, keepdims=True) + EPS) * gcq_ref[...]
        qb = _dot_nt(cqn.astype(MXU), wuq_ref[...])
        ckv = ckv_ref[...].astype(F32)
        ckvn = ckv * lax.rsqrt(jnp.mean(ckv * ckv, axis=-1, keepdims=True) + EPS) * gckv_ref[...]
        kvb = _dot_nt(ckvn.astype(MXU), wukv_ref[...])
        kr = kr_ref[...].astype(F32)
        for h in range(MLA_HEADS):
            hs = slice(h * LANES, (h + 1) * LANES)
            qh = qb[:, hs]
            qn = qh * lax.rsqrt(_rsum(qh * qh) / MLA_QK + EPS) * gqn_ref[...]
            q_ref[:, hs] = (qn * cv + _rot(qn, lane) * sv).astype(MXU)
            kc = jnp.where(lane < HEAD, kvb[:, hs], kr)
            kn = kc * lax.rsqrt(_rsum(kc * kc) / MLA_QK + EPS) * gkn_ref[...]
            k_ref[:, hs] = (kn * cv + _rot(kn, lane) * sv).astype(MXU)
        for j in range(MLA_HEADS // 2):
            va = kvb[:, (2 * j) * LANES:(2 * j + 1) * LANES]
            vb = kvb[:, (2 * j + 1) * LANES:(2 * j + 2) * LANES]
            v_ref[:, j * LANES:(j + 1) * LANES] = jnp.where(lane < HEAD, pltpu.roll(va, HEAD, 1), vb).astype(MXU)

    return pl.pallas_call(
        body, name=name, grid=(nb, s // tm), in_specs=_mla_prep_specs(s, tm),
        out_specs=[pl.BlockSpec((None, tm, 768), lambda b, i: (b, i, 0)),
                   pl.BlockSpec((None, tm, 768), lambda b, i: (b, i, 0)),
                   pl.BlockSpec((None, tm, 384), lambda b, i: (b, i, 0))],
        out_shape=[jax.ShapeDtypeStruct((nb, s, 768), MXU), jax.ShapeDtypeStruct((nb, s, 768), MXU),
                   jax.ShapeDtypeStruct((nb, s, 384), MXU)],
        compiler_params=_cp("parallel", "parallel"))(proj, proj, proj, cs, sn, gcq, gckv, gqn, gkn, wuq, wukv)


def _mla_prep_bwd(proj, cs, sn, gcq, gckv, gqn, gkn, wuq, wukv, dq, dk, dv, name):
    nb, s, _ = proj.shape
    tm = min(256, s)

    def body(cq_ref, ckv_ref, kr_ref, c_ref, s_ref, gcq_ref, gckv_ref, gqn_ref, gkn_ref, wuq_ref, wukv_ref,
             dq_ref, dk_ref, dv_ref,
             dcq_ref, dckv_ref, dkr_ref, dwuq_ref, dwukv_ref, dgcq_ref, dgckv_ref, dgqn_ref, dgkn_ref,
             dqb_s, dkvb_s):
        @pl.when((pl.program_id(0) == 0) & (pl.program_id(1) == 0))
        def _():
            for r in (dwuq_ref, dwukv_ref, dgcq_ref, dgckv_ref, dgqn_ref, dgkn_ref):
                r[...] = jnp.zeros_like(r)

        lane = _iota((tm, LANES), 1)
        cv, sv = c_ref[...], s_ref[...]
        gqn, gkn = gqn_ref[...], gkn_ref[...]
        cq = cq_ref[...].astype(F32)
        rc = lax.rsqrt(jnp.mean(cq * cq, axis=-1, keepdims=True) + EPS)
        chat = cq * rc
        cqn = (chat * gcq_ref[...]).astype(MXU)
        qb = _dot_nt(cqn, wuq_ref[...])
        ckv = ckv_ref[...].astype(F32)
        rkv = lax.rsqrt(jnp.mean(ckv * ckv, axis=-1, keepdims=True) + EPS)
        kvhat = ckv * rkv
        ckvn = (kvhat * gckv_ref[...]).astype(MXU)
        kvb = _dot_nt(ckvn, wukv_ref[...])
        kr = kr_ref[...].astype(F32)
        dgq = jnp.zeros((1, LANES), F32)
        dgk = jnp.zeros((1, LANES), F32)
        dkr = jnp.zeros((tm, LANES), F32)
        for h in range(MLA_HEADS):
            hs = slice(h * LANES, (h + 1) * LANES)
            qh = qb[:, hs]
            rq = lax.rsqrt(_rsum(qh * qh) / MLA_QK + EPS)
            qhat = qh * rq
            dqr = dq_ref[:, hs]
            dqn = dqr * cv + _rot_t(dqr * sv, lane)
            dgq = dgq + _csum(dqn * qhat)
            dyq = dqn * gqn
            dqb_s[:, hs] = (rq * (dyq - qhat * (_rsum(dyq * qhat) / MLA_QK))).astype(MXU)

            kc = jnp.where(lane < HEAD, kvb[:, hs], kr)
            rk = lax.rsqrt(_rsum(kc * kc) / MLA_QK + EPS)
            khat = kc * rk
            dkr_h = dk_ref[:, hs]
            dkn = dkr_h * cv + _rot_t(dkr_h * sv, lane)
            dgk = dgk + _csum(dkn * khat)
            dyk = dkn * gkn
            dkc = rk * (dyk - khat * (_rsum(dyk * khat) / MLA_QK))
            dkr = dkr + jnp.where(lane >= HEAD, dkc, 0.0)
            dvb = dv_ref[:, (h // 2) * LANES:(h // 2 + 1) * LANES]
            dvp = dvb if h % 2 == 1 else pltpu.roll(dvb, HEAD, 1)
            dkvb_s[:, hs] = jnp.where(lane < HEAD, dkc, dvp).astype(MXU)
        dgqn_ref[...] += dgq
        dgkn_ref[...] += dgk
        dkr_ref[...] = dkr

        dqb = dqb_s[...]
        dwuq_ref[...] += _dot_tn(dqb, cqn)
        dcqn = _dot(dqb, wuq_ref[...])
        dgcq_ref[...] += _csum(dcqn * chat)
        dyc = dcqn * gcq_ref[...]
        dcq_ref[...] = rc * (dyc - chat * jnp.mean(dyc * chat, axis=-1, keepdims=True))

        dkvb = dkvb_s[...]
        dwukv_ref[...] += _dot_tn(dkvb, ckvn)
        dckvn = _dot(dkvb, wukv_ref[...])
        dgckv_ref[...] += _csum(dckvn * kvhat)
        dykv = dckvn * gckv_ref[...]
        dckv_ref[...] = rkv * (dykv - kvhat * jnp.mean(dykv * kvhat, axis=-1, keepdims=True))

    full = lambda shape: pl.BlockSpec(shape, lambda b, i: (0, 0))
    tile = lambda width: pl.BlockSpec((None, tm, width), lambda b, i: (b, i, 0))
    return pl.pallas_call(
        body, name=name, grid=(nb, s // tm),
        in_specs=_mla_prep_specs(s, tm) + [tile(768), tile(768), tile(384)],
        out_specs=[tile(256), tile(128), tile(128), full((768, 256)), full((768, 128)),
                   full((1, 256)), full((1, 128)), full((1, 128)), full((1, 128))],
        out_shape=[jax.ShapeDtypeStruct((nb, s, 256), F32), jax.ShapeDtypeStruct((nb, s, 128), F32),
                   jax.ShapeDtypeStruct((nb, s, 128), F32),
                   jax.ShapeDtypeStruct((768, 256), F32), jax.ShapeDtypeStruct((768, 128), F32),
                   jax.ShapeDtypeStruct((1, 256), F32), jax.ShapeDtypeStruct((1, 128), F32),
                   jax.ShapeDtypeStruct((1, 128), F32), jax.ShapeDtypeStruct((1, 128), F32)],
        scratch_shapes=[pltpu.VMEM((tm, 768), MXU), pltpu.VMEM((tm, 768), MXU)],
        compiler_params=_cp("arbitrary", "arbitrary"))(
            proj, proj, proj, cs, sn, gcq, gckv, gqn, gkn, wuq, wukv, dq, dk, dv)


def _softplus(z):
    return jnp.maximum(z, 0.0) + jnp.log(1.0 + jnp.exp(-jnp.abs(z)))


def _sb_fwd(proj, name, riding=()):
    nb, s, _ = proj.shape
    tq, tk = min(256, s), min(256, s)
    ratio = tq // tk
    na = len(riding)
    grid = (nb, 2, s // tq)

    def body(q_ref, k_ref, v_ref, o_ref, ct_ref, cnt_ref):
        i = pl.program_id(2)
        lo = _iota((tq, LANES), 1) < HEAD
        qv = q_ref[...]
        q0 = jnp.where(lo, qv, 0.0).astype(MXU)
        q1 = jnp.where(lo, 0.0, qv).astype(MXU)
        usuf = (_iota((tk, tk), 0) > _iota((tk, tk), 1)).astype(MXU)
        tpos = i * tq + _iota((tq, tk), 0)
        scol = _iota((tq, tk), 1)
        nch = (i + 1) * ratio

        def alive(st):
            return (st[0] < nch) & (st[5] > SB_DEAD)

        def step(st):
            t, c0, a0, c1, a1, _ = st
            j = nch - 1 - t
            off = pl.multiple_of(j * tk, tk)
            kc = k_ref[pl.ds(off, tk), :].astype(MXU)
            vc = v_ref[pl.ds(off, tk), :].astype(MXU)
            msk = (scol + j * tk) < tpos

            def head(qm, c, a):
                z = _dot_nt(qm, kc) * SB_SCALE
                sp = _softplus(z)
                lk = jnp.where(msk, -sp, 0.0)
                w = jnp.where(msk, jnp.exp(z - sp + _cumdot(lk, usuf) + c), 0.0)
                return c + _rsum(lk), a + _dot(w.astype(MXU), vc)

            c0, a0 = head(q0, c0, a0)
            c1, a1 = head(q1, c1, a1)
            return t + 1, c0, a0, c1, a1, jnp.maximum(jnp.max(c0), jnp.max(c1))

        z1 = jnp.zeros((tq, 1), F32)
        za = jnp.zeros((tq, LANES), F32)
        t, c0, a0, c1, a1, _ = lax.while_loop(alive, step, (jnp.int32(0), z1, za, z1, za, jnp.float32(0.0)))
        o_ref[...] = jnp.where(lo, a0, a1)
        ct_ref[...] = jnp.where(lo, c0, c1)
        cnt_ref[...] = jnp.zeros((8, LANES), F32) + t.astype(F32)

    kv = lambda col: pl.BlockSpec((None, s, LANES), lambda b, p, i: (b, 0, col // LANES + p))
    tile = pl.BlockSpec((None, tq, LANES), lambda b, p, i: (b, i, p))
    hbm = pl.BlockSpec(memory_space=pl.ANY)
    return pl.pallas_call(
        _with_gather(body, 3, 3, na, grid), name=name, grid=grid,
        in_specs=[pl.BlockSpec((None, tq, LANES), lambda b, p, i: (b, i, COL_SBQ // LANES + p)),
                  kv(COL_SBK), kv(COL_SBV)] + [hbm] * na,
        out_specs=[tile, tile, pl.BlockSpec((None, None, None, 8, LANES), lambda b, p, i: (b, p, i, 0, 0))] + [hbm] * na,
        out_shape=[jax.ShapeDtypeStruct((nb, s, 256), F32)] * 2
        + [jax.ShapeDtypeStruct((nb, 2, s // tq, 8, LANES), F32)] + _gather_out_shapes(riding),
        scratch_shapes=_gather_sems(na) if na else [],
        compiler_params=_cp("arbitrary", "arbitrary", "arbitrary"))(proj, proj, proj, *riding)


def _sb_bwd(proj, ct, cnt, do, name):
    nb, s, _ = proj.shape
    tq, tk = min(256, s), min(256, s)
    ratio = tq // tk

    def body(q_ref, k_ref, v_ref, ct_ref, cnt_ref, do_ref, dq_ref, dk_ref, dv_ref):
        i = pl.program_id(2)

        @pl.when(i == 0)
        def _():
            dk_ref[...] = jnp.zeros_like(dk_ref)
            dv_ref[...] = jnp.zeros_like(dv_ref)

        lane = _iota((tq, LANES), 1)
        lo = lane < HEAD
        lok = _iota((tk, LANES), 1) < HEAD
        qv, dov = q_ref[...], do_ref[...]
        qb, dob = qv.astype(MXU), dov.astype(MXU)
        q0 = jnp.where(lo, qv, 0.0).astype(MXU)
        q1 = jnp.where(lo, 0.0, qv).astype(MXU)
        do0 = jnp.where(lo, dov, 0.0).astype(MXU)
        do1 = jnp.where(lo, 0.0, dov).astype(MXU)
        ctv = ct_ref[...]
        ct0 = _rsum(jnp.where(lane == 0, ctv, 0.0))
        ct1 = _rsum(jnp.where(lane == LANES - 1, ctv, 0.0))
        uincl = (_iota((tk, tk), 0) <= _iota((tk, tk), 1)).astype(MXU)
        ustrict = (_iota((tk, tk), 0) < _iota((tk, tk), 1)).astype(MXU)
        tpos = i * tq + _iota((tq, tk), 0)
        scol = _iota((tq, tk), 1)
        nch = (i + 1) * ratio

        def step(j, carry):
            p0, g0, dq0, p1, g1, dq1 = carry
            off = pl.multiple_of(j * tk, tk)
            kc = k_ref[pl.ds(off, tk), :].astype(MXU)
            vc = v_ref[pl.ds(off, tk), :].astype(MXU)
            msk = (scol + j * tk) < tpos

            def head(qm, dom, ctot, pc, gc, dqa):
                z = _dot_nt(qm, kc) * SB_SCALE
                sp = _softplus(z)
                lk = jnp.where(msk, -sp, 0.0)
                lsig = z - sp
                w = jnp.where(msk, jnp.exp(lsig + (ctot - pc - _cumdot(lk, uincl))), 0.0)
                g = w * _dot_nt(dom, vc)
                gpre = gc + _cumdot(g, ustrict)
                sig = jnp.exp(lsig)
                dz = (jnp.where(msk, g * (1.0 - sig) - sig * gpre, 0.0) * SB_SCALE).astype(MXU)
                return (pc + _rsum(lk), gc + _rsum(g), dqa + _dot(dz, kc),
                        _dot_tn(dz, qb), _dot_tn(w.astype(MXU), dob))

            p0, g0, dq0, dk0, dv0 = head(q0, do0, ct0, p0, g0, dq0)
            p1, g1, dq1, dk1, dv1 = head(q1, do1, ct1, p1, g1, dq1)
            dk_ref[pl.ds(off, tk), :] += jnp.where(lok, dk0, dk1)
            dv_ref[pl.ds(off, tk), :] += jnp.where(lok, dv0, dv1)
            return p0, g0, dq0, p1, g1, dq1

        z1 = jnp.zeros((tq, 1), F32)
        za = jnp.zeros((tq, LANES), F32)
        first = nch - jnp.max(cnt_ref[...]).astype(jnp.int32)
        _, _, dq0, _, _, dq1 = lax.fori_loop(first, nch, step, (z1, z1, za, z1, z1, za))
        dq_ref[...] = jnp.where(lo, dq0, dq1)

    kv = lambda col: pl.BlockSpec((None, s, LANES), lambda b, p, i: (b, 0, col // LANES + p))
    tile = pl.BlockSpec((None, tq, LANES), lambda b, p, i: (b, i, p))
    acc = pl.BlockSpec((None, s, LANES), lambda b, p, i: (b, 0, p))
    return pl.pallas_call(
        body, name=name, grid=(nb, 2, s // tq),
        in_specs=[pl.BlockSpec((None, tq, LANES), lambda b, p, i: (b, i, COL_SBQ // LANES + p)),
                  kv(COL_SBK), kv(COL_SBV), tile,
                  pl.BlockSpec((None, None, None, 8, LANES), lambda b, p, i: (b, p, i, 0, 0)), tile],
        out_specs=[tile, acc, acc],
        out_shape=[jax.ShapeDtypeStruct((nb, s, 256), F32)] * 3,
        compiler_params=_cp("parallel", "parallel", "arbitrary"))(proj, proj, proj, ct, cnt, do)


def _mla_fwd(q, k, v, name, riding=()):
    nb, s, _ = q.shape
    tq = tk = min(512, s)
    na = len(riding)
    grid = (nb, MLA_HEADS // 2, s // tq)

    def body(q_ref, k_ref, v_ref, o_ref, lse_ref):
        i = pl.program_id(2)
        q0, q1 = q_ref[:, :LANES], q_ref[:, LANES:]
        krow = _iota((tk, tq), 0)
        qcol = _iota((tk, tq), 1)

        def step(j, carry, diagonal):
            m0, l0, a0, m1, l1, a1 = carry
            off = pl.multiple_of(j * tk, tk)
            vc = v_ref[pl.ds(off, tk), :]

            def head(qh, kh, m, l, a):
                st = _dot_nt(kh, qh) * MLA_SCALE
                if diagonal:
                    st = jnp.where(krow <= qcol, st, NEG)
                mn = jnp.maximum(m, jnp.max(st, axis=0, keepdims=True))
                al = jnp.exp(m - mn)
                pt = jnp.exp(st - mn)
                return mn, al * l + _csum(pt), al * a + _dot_tn(vc, pt.astype(MXU))

            m0, l0, a0 = head(q0, k_ref[pl.ds(off, tk), :LANES], m0, l0, a0)
            m1, l1, a1 = head(q1, k_ref[pl.ds(off, tk), LANES:], m1, l1, a1)
            return m0, l0, a0, m1, l1, a1

        mi = jnp.full((1, tq), NEG, F32)
        z1 = jnp.zeros((1, tq), F32)
        za = jnp.zeros((LANES, tq), F32)
        carry = lax.fori_loop(0, i, lambda j, cr: step(j, cr, False), (mi, z1, za, mi, z1, za))
        m0, l0, a0, m1, l1, a1 = step(i, carry, True)
        lo_rows = _iota((LANES, tq), 0) < HEAD
        o_ref[...] = jnp.where(lo_rows, a0 / l0, a1 / l1).T
        lse_ref[...] = jnp.zeros_like(lse_ref)
        lse_ref[0:1, :] = m0 + jnp.log(l0)
        lse_ref[1:2, :] = m1 + jnp.log(l1)

    tile = pl.BlockSpec((None, tq, LANES), lambda b, p, i: (b, i, p))
    hbm = pl.BlockSpec(memory_space=pl.ANY)
    return pl.pallas_call(
        _with_gather(body, 3, 2, na, grid), name=name, grid=grid,
        in_specs=[pl.BlockSpec((None, tq, 2 * LANES), lambda b, p, i: (b, i, p)),
                  pl.BlockSpec((None, s, 2 * LANES), lambda b, p, i: (b, 0, p)),
                  pl.BlockSpec((None, s, LANES), lambda b, p, i: (b, 0, p))] + [hbm] * na,
        out_specs=[tile, pl.BlockSpec((None, None, 8, tq), lambda b, p, i: (b, p, 0, i))] + [hbm] * na,
        out_shape=[jax.ShapeDtypeStruct((nb, s, 384), F32), jax.ShapeDtypeStruct((nb, MLA_HEADS // 2, 8, s), F32)]
        + _gather_out_shapes(riding),
        scratch_shapes=_gather_sems(na) if na else [],
        compiler_params=_cp("arbitrary", "arbitrary", "arbitrary"))(q, k, v, *riding)


def _mla_bwd(q, k, v, o, lse, do, name, riding=()):
    nb, s, _ = q.shape
    tq = tk = min(512, s)
    na = len(riding)
    grid = (nb, MLA_HEADS // 2, s // tq)

    def body(q_ref, k_ref, v_ref, o_ref, lse_ref, do_ref, dq_ref, dk_ref, dv_ref):
        i = pl.program_id(2)

        @pl.when(i == 0)
        def _():
            dk_ref[...] = jnp.zeros_like(dk_ref)
            dv_ref[...] = jnp.zeros_like(dv_ref)

        lo = _iota((tq, LANES), 1) < HEAD
        lok = _iota((tk, LANES), 1) < HEAD
        q0, q1 = q_ref[:, :LANES], q_ref[:, LANES:]
        dov = do_ref[...]
        dob = dov.astype(MXU)
        do0 = jnp.where(lo, dov, 0.0).astype(MXU)
        do1 = jnp.where(lo, 0.0, dov).astype(MXU)
        dd = dov * o_ref[...]
        hi = dd.astype(MXU)
        r1 = dd - hi.astype(F32)
        mid = r1.astype(MXU)
        low = (r1 - mid.astype(F32)).astype(MXU)
        sel_lane = _iota((8, LANES), 1) < HEAD
        sel0 = sel_lane.astype(MXU)
        sel1 = (~sel_lane).astype(MXU)
        dl0 = (_dot_nt(sel0, hi) + _dot_nt(sel0, mid) + _dot_nt(sel0, low))[0:1, :]
        dl1 = (_dot_nt(sel1, hi) + _dot_nt(sel1, mid) + _dot_nt(sel1, low))[0:1, :]
        ls0, ls1 = lse_ref[0:1, :], lse_ref[1:2, :]
        krow = _iota((tk, tq), 0)
        qcol = _iota((tk, tq), 1)

        def step(j, carry, diagonal):
            dq0, dq1 = carry
            off = pl.multiple_of(j * tk, tk)
            vc = v_ref[pl.ds(off, tk), :]

            def head(qh, kh, dom, ls, dl, dqa):
                st = _dot_nt(kh, qh) * MLA_SCALE
                if diagonal:
                    st = jnp.where(krow <= qcol, st, NEG)
                pt = jnp.exp(st - ls)
                dst = (pt * (_dot_nt(vc, dom) - dl) * MLA_SCALE).astype(MXU)
                return dqa + _dot_tn(kh, dst), _dot(dst, qh), _dot(pt.astype(MXU), dob)

            dq0, dk0, dv0 = head(q0, k_ref[pl.ds(off, tk), :LANES], do0, ls0, dl0, dq0)
            dq1, dk1, dv1 = head(q1, k_ref[pl.ds(off, tk), LANES:], do1, ls1, dl1, dq1)
            dk_ref[pl.ds(off, tk), :LANES] += dk0
            dk_ref[pl.ds(off, tk), LANES:] += dk1
            dv_ref[pl.ds(off, tk), :] += jnp.where(lok, dv0, dv1)
            return dq0, dq1

        za = jnp.zeros((LANES, tq), F32)
        carry = lax.fori_loop(0, i, lambda j, cr: step(j, cr, False), (za, za))
        dq0, dq1 = step(i, carry, True)
        dq_ref[:, :LANES] = dq0.T
        dq_ref[:, LANES:] = dq1.T

    tile = pl.BlockSpec((None, tq, LANES), lambda b, p, i: (b, i, p))
    tile2 = pl.BlockSpec((None, tq, 2 * LANES), lambda b, p, i: (b, i, p))
    hbm = pl.BlockSpec(memory_space=pl.ANY)
    return pl.pallas_call(
        _with_chip_exchange(body, 6, 3, na, grid), name=name, grid=grid,
        in_specs=[tile2,
                  pl.BlockSpec((None, s, 2 * LANES), lambda b, p, i: (b, 0, p)),
                  pl.BlockSpec((None, s, LANES), lambda b, p, i: (b, 0, p)),
                  tile, pl.BlockSpec((None, None, 8, tq), lambda b, p, i: (b, p, 0, i)), tile] + [hbm] * na,
        out_specs=[tile2,
                   pl.BlockSpec((None, s, 2 * LANES), lambda b, p, i: (b, 0, p)),
                   pl.BlockSpec((None, s, LANES), lambda b, p, i: (b, 0, p))] + [hbm] * na,
        out_shape=[jax.ShapeDtypeStruct((nb, s, 768), F32), jax.ShapeDtypeStruct((nb, s, 768), F32),
                   jax.ShapeDtypeStruct((nb, s, 384), F32)] + [jax.ShapeDtypeStruct(a.shape, a.dtype) for a in riding],
        scratch_shapes=_chip_exchange_sems(na) if na else [],
        compiler_params=_cp("arbitrary", "arbitrary", "arbitrary"))(q, k, v, o, lse, do, *riding)


def _half_stats(xv, lo):
    x2 = xv * xv
    s0 = _rsum(jnp.where(lo, x2, 0.0))
    s1 = _rsum(jnp.where(lo, 0.0, x2))
    return jnp.where(lo, lax.rsqrt(s0 / HEAD + EPS), lax.rsqrt(s1 / HEAD + EPS))


def _half_mean(xv, lo):
    s0 = _rsum(jnp.where(lo, xv, 0.0))
    s1 = _rsum(jnp.where(lo, 0.0, xv))
    return jnp.where(lo, s0, s1) / HEAD


def _swa_in_specs():
    def band(col, prev):
        if prev:
            return pl.BlockSpec((None, BLOCK, LANES), lambda b, n: (b, jnp.maximum(n - 1, 0), col // LANES))
        return pl.BlockSpec((None, BLOCK, LANES), lambda b, n: (b, n, col // LANES))

    full = lambda shape: pl.BlockSpec(shape, lambda b, n: tuple(0 for _ in shape))
    return [pl.BlockSpec((None, BLOCK, 384), lambda b, n: (b, n, COL_SWQ // 384)),
            band(COL_SWK, False), band(COL_SWK, True), band(COL_SWV, False), band(COL_SWV, True),
            full((1, LANES)), full((1, LANES)), full((8, LANES)), full((SW_HEADS, BLOCK, 2 * BLOCK))]


def _swa_valid(n):
    a = _iota((BLOCK, 2 * BLOCK), 0)
    bcol = _iota((BLOCK, 2 * BLOCK), 1)
    dist = BLOCK + a - bcol
    return (dist >= 0) & (dist < BLOCK) & ((n > 0) | (bcol >= BLOCK))


def _swa_fwd(proj, gq, gk, sinks, bias, name):
    nb, s, _ = proj.shape

    def body(q_ref, kc_ref, kp_ref, vc_ref, vp_ref, gq_ref, gk_ref, sk_ref, bias_ref, o_ref):
        n = pl.program_id(1)
        lo = _iota((BLOCK, LANES), 1) < HEAD
        lo2 = _iota((2 * BLOCK, LANES), 1) < HEAD
        kband = jnp.concatenate([kp_ref[...], kc_ref[...]], axis=0).astype(F32)
        vband = jnp.concatenate([vp_ref[...], vc_ref[...]], axis=0).astype(F32)
        kn = kband * _half_stats(kband, lo2) * gk_ref[...]
        ks = (kn.astype(MXU), pltpu.roll(kn, HEAD, 1).astype(MXU))
        vs = (vband.astype(MXU), pltpu.roll(vband, HEAD, 1).astype(MXU))
        valid = _swa_valid(n)
        for blk in range(SW_HEADS // 2):
            qv = q_ref[:, blk * LANES:(blk + 1) * LANES].astype(F32)
            qn = qv * _half_stats(qv, lo) * gq_ref[...]
            outs = []
            for half in range(2):
                h = 2 * blk + half
                swap = 0 if half == h // 3 else 1
                qm = jnp.where(lo if half == 0 else ~lo, qn, 0.0).astype(MXU)
                sc = jnp.where(valid, _dot_nt(qm, ks[swap]) * SW_SCALE + bias_ref[h], NEG)
                sk = jnp.max(sk_ref[h:h + 1, :], axis=-1, keepdims=True)
                m = jnp.maximum(jnp.max(sc, axis=-1, keepdims=True), sk)
                p = jnp.exp(sc - m)
                l = _rsum(p) + jnp.exp(sk - m)
                outs.append(_dot((p / l).astype(MXU), vs[swap]))
            o_ref[:, blk * LANES:(blk + 1) * LANES] = jnp.where(lo, outs[0], outs[1])

    return pl.pallas_call(
        body, name=name, grid=(nb, s // BLOCK), in_specs=_swa_in_specs(),
        out_specs=pl.BlockSpec((None, BLOCK, 384), lambda b, n: (b, n, 0)),
        out_shape=jax.ShapeDtypeStruct((nb, s, 384), F32),
        compiler_params=_cp("parallel", "parallel"))(proj, proj, proj, proj, proj, gq, gk, sinks, bias)


def _swa_bwd(proj, gq, gk, sinks, bias, do, name):
    nb, s, _ = proj.shape

    def body(q_ref, kc_ref, kp_ref, vc_ref, vp_ref, gq_ref, gk_ref, sk_ref, bias_ref, do_ref,
             dq_ref, dkc_ref, dkp_ref, dvc_ref, dvp_ref, dbias_ref, dsk_ref, dgq_ref, dgk_ref):
        n = pl.program_id(1)

        @pl.when((pl.program_id(0) == 0) & (n == 0))
        def _():
            for r in (dbias_ref, dsk_ref, dgq_ref, dgk_ref):
                r[...] = jnp.zeros_like(r)

        lo = _iota((BLOCK, LANES), 1) < HEAD
        lo2 = _iota((2 * BLOCK, LANES), 1) < HEAD
        kband = jnp.concatenate([kp_ref[...], kc_ref[...]], axis=0).astype(F32)
        vband = jnp.concatenate([vp_ref[...], vc_ref[...]], axis=0).astype(F32)
        rk = _half_stats(kband, lo2)
        khat = kband * rk
        gkv = gk_ref[...]
        kn = khat * gkv
        ks = (kn.astype(MXU), pltpu.roll(kn, HEAD, 1).astype(MXU))
        vs = (vband.astype(MXU), pltpu.roll(vband, HEAD, 1).astype(MXU))
        valid = _swa_valid(n)
        dkn = jnp.zeros((2 * BLOCK, LANES), F32)
        dvb = jnp.zeros((2 * BLOCK, LANES), F32)
        gqv = gq_ref[...]
        dgq = jnp.zeros((1, LANES), F32)
        for blk in range(SW_HEADS // 2):
            bs = slice(blk * LANES, (blk + 1) * LANES)
            qv = q_ref[:, bs].astype(F32)
            rq = _half_stats(qv, lo)
            qhat = qv * rq
            qn = qhat * gqv
            dov = do_ref[:, bs]
            dqn = jnp.zeros((BLOCK, LANES), F32)
            for half in range(2):
                h = 2 * blk + half
                swap = 0 if half == h // 3 else 1
                hm = lo if half == 0 else ~lo
                qm = jnp.where(hm, qn, 0.0).astype(MXU)
                dom = jnp.where(hm, dov, 0.0).astype(MXU)
                sc = jnp.where(valid, _dot_nt(qm, ks[swap]) * SW_SCALE + bias_ref[h], NEG)
                sk = jnp.max(sk_ref[h:h + 1, :], axis=-1, keepdims=True)
                m = jnp.maximum(jnp.max(sc, axis=-1, keepdims=True), sk)
                e = jnp.exp(sc - m)
                es = jnp.exp(sk - m)
                l = _rsum(e) + es
                p = e / l
                dp = _dot_nt(dom, vs[swap])
                delta = _rsum(p * dp)
                ds = p * (dp - delta)
                dsk_ref[h:h + 1, :] += jnp.broadcast_to(_csum(-(es / l) * delta), (1, LANES))
                dbias_ref[h] += ds
                dsb = (ds * SW_SCALE).astype(MXU)
                dqn = dqn + jnp.where(hm, _dot(dsb, ks[swap]), 0.0)
                rk_ = _dot_tn(dsb, qm)
                rv_ = _dot_tn(p.astype(MXU), dom)
                if swap:
                    rk_ = pltpu.roll(rk_, HEAD, 1)
                    rv_ = pltpu.roll(rv_, HEAD, 1)
                dkn = dkn + rk_
                dvb = dvb + rv_
            dgq = dgq + _csum(dqn * qhat)
            dyq = dqn * gqv
            dq_ref[:, bs] = rq * (dyq - qhat * _half_mean(dyq * qhat, lo))
        dgq_ref[...] += dgq
        dgk_ref[...] += _csum(dkn * khat)
        dyk = dkn * gkv
        dkb = rk * (dyk - khat * _half_mean(dyk * khat, lo2))
        dkp_ref[...] = dkb[:BLOCK]
        dkc_ref[...] = dkb[BLOCK:]
        dvp_ref[...] = dvb[:BLOCK]
        dvc_ref[...] = dvb[BLOCK:]

    full = lambda shape: pl.BlockSpec(shape, lambda b, n: tuple(0 for _ in shape))
    tile = pl.BlockSpec((None, BLOCK, LANES), lambda b, n: (b, n, 0))
    tile3 = pl.BlockSpec((None, BLOCK, 384), lambda b, n: (b, n, 0))
    kvs = jax.ShapeDtypeStruct((nb, s, LANES), F32)
    return pl.pallas_call(
        body, name=name, grid=(nb, s // BLOCK), in_specs=_swa_in_specs() + [tile3],
        out_specs=[tile3, tile, tile, tile, tile, full((SW_HEADS, BLOCK, 2 * BLOCK)), full((8, LANES)),
                   full((1, LANES)), full((1, LANES))],
        out_shape=[jax.ShapeDtypeStruct((nb, s, 384), F32), kvs, kvs, kvs, kvs,
                   jax.ShapeDtypeStruct((SW_HEADS, BLOCK, 2 * BLOCK), F32), jax.ShapeDtypeStruct((8, LANES), F32),
                   jax.ShapeDtypeStruct((1, LANES), F32), jax.ShapeDtypeStruct((1, LANES), F32)],
        compiler_params=_cp("arbitrary", "arbitrary"))(proj, proj, proj, proj, proj, gq, gk, sinks, bias, do)


def _bias_build(table, bucket, name):
    def body(tb_ref, bk_ref, o_ref):
        bk = bk_ref[...]
        tb = tb_ref[...]
        row = _iota((8, LANES), 0)
        col = _iota((8, LANES), 1)
        for h in range(SW_HEADS):
            acc = jnp.zeros((BLOCK, 2 * BLOCK), F32)
            for t in range(REL_BUCKETS):
                val = jnp.sum(jnp.where((row == h) & (col == t), tb, 0.0), keepdims=True)
                acc = jnp.where(bk == t, val, acc)
            o_ref[h] = acc

    return pl.pallas_call(
        body, name=name, out_shape=jax.ShapeDtypeStruct((SW_HEADS, BLOCK, 2 * BLOCK), F32))(table, bucket)


def _bias_grad(dbias, bucket, name):
    def body(db_ref, bk_ref, o_ref):
        bk = bk_ref[...]
        row = _iota((8, LANES), 0)
        col = _iota((8, LANES), 1)
        res = jnp.zeros((8, LANES), F32)
        for h in range(SW_HEADS):
            dbh = db_ref[h]
            for t in range(REL_BUCKETS):
                val = jnp.sum(jnp.where(bk == t, dbh, 0.0), keepdims=True)
                res = jnp.where((row == h) & (col == t), val, res)
        o_ref[...] = res

    return pl.pallas_call(body, name=name, out_shape=jax.ShapeDtypeStruct((8, LANES), F32))(dbias, bucket)


def _loss_grad(y, target, name):
    nb, s, d = y.shape
    tm = min(512, s)

    def body(y_ref, t_ref, loss_ref, dy_ref):
        @pl.when((pl.program_id(0) == 0) & (pl.program_id(1) == 0))
        def _():
            loss_ref[...] = jnp.zeros_like(loss_ref)

        e = y_ref[...] - t_ref[...]
        dy_ref[...] = e / d
        loss_ref[...] += 0.5 * jnp.sum(_rsum(e * e) / d, keepdims=True)

    tile = pl.BlockSpec((None, tm, d), lambda b, i: (b, i, 0))
    return pl.pallas_call(
        body, name=name, grid=(nb, s // tm), in_specs=[tile, tile],
        out_specs=[pl.BlockSpec((8, LANES), lambda b, i: (0, 0)), tile],
        out_shape=[jax.ShapeDtypeStruct((8, LANES), F32), jax.ShapeDtypeStruct((nb, s, d), F32)],
        compiler_params=_cp("arbitrary", "arbitrary"))(y, target)


def _adamw(parts, w, m, v, name):
    npart, r, ncol = parts.shape
    tr = _row_tile(r, ncol)
    bc1 = 1.0 - ADAM_B1 ** ADAM_STEP
    bc2 = 1.0 - ADAM_B2 ** ADAM_STEP

    def body(p_ref, w_ref, m_ref, v_ref, g_ref, d_ref, nm_ref, nv_ref):
        g = p_ref[0].astype(F32)
        for k in range(1, npart):
            g = g + p_ref[k].astype(F32)
        mn = ADAM_B1 * m_ref[...] + (1.0 - ADAM_B1) * g
        vn = ADAM_B2 * v_ref[...] + (1.0 - ADAM_B2) * (g * g)
        g_ref[...] = g
        nm_ref[...] = mn
        nv_ref[...] = vn
        d_ref[...] = -ADAM_LR * ((mn / bc1) / (jnp.sqrt(vn / bc2) + ADAM_EPS) + ADAM_WD * w_ref[...])

    tile = pl.BlockSpec((tr, ncol), lambda i: (i, 0))
    return pl.pallas_call(
        body, name=name, grid=(r // tr,),
        in_specs=[pl.BlockSpec((npart, tr, ncol), lambda i: (0, i, 0)), tile, tile, tile],
        out_specs=[tile] * 4, out_shape=[jax.ShapeDtypeStruct((r, ncol), F32)] * 4,
        compiler_params=_cp("parallel"))(parts, w, m, v)


def _unpack(flat, shapes, lead=()):
    out, off = [], 0
    for shp in shapes:
        size = 1
        for dim in shp:
            size *= dim
        out.append(flat[..., off:off + size].reshape(lead + tuple(shp)))
        off += size
    return out


def _t5_bucket():
    a = jnp.arange(BLOCK)[:, None]
    b = jnp.arange(2 * BLOCK)[None, :]
    dist = BLOCK + a - b
    max_exact = REL_BUCKETS // 2
    nn = jnp.maximum(dist, 0)
    nf = jnp.maximum(nn, 1).astype(F32)
    large = max_exact + (jnp.log(nf / max_exact) / math.log(BLOCK / max_exact)
                         * (REL_BUCKETS - max_exact)).astype(jnp.int32)
    large = jnp.minimum(large, REL_BUCKETS - 1)
    return jnp.where(nn < max_exact, nn, large).astype(jnp.int32)


def _pad_lanes(g, n):
    return jnp.pad(g, (0, n - g.shape[0])).reshape(1, n)


def kernel(x, c, positions, rel_table, norm1_g, norm2_g, w_ada, b_ada, w_in, mla_cq_g, w_uq, mla_ckv_g, w_ukv, mla_qn_g, mla_kn_g, sw_qn_g, sw_kn_g, sw_sinks, w_out, w_up, conv_w, conv_b, w_down, loss_target, m_rel_table, m_norm1_g, m_norm2_g, m_w_ada, m_b_ada, m_w_in, m_mla_cq_g, m_w_uq, m_mla_ckv_g, m_w_ukv, m_mla_qn_g, m_mla_kn_g, m_sw_qn_g, m_sw_kn_g, m_sw_sinks, m_w_out, m_w_up, m_conv_w, m_conv_b, m_w_down, v_rel_table, v_norm1_g, v_norm2_g, v_w_ada, v_b_ada, v_w_in, v_mla_cq_g, v_w_uq, v_mla_ckv_g, v_w_ukv, v_mla_qn_g, v_mla_kn_g, v_sw_qn_g, v_sw_kn_g, v_sw_sinks, v_w_out, v_w_up, v_conv_w, v_conv_b, v_w_down):
    nb, s, d = x.shape
    nl = norm1_g.shape[0]
    me = 4 * lax.axis_index("x") + 2 * lax.axis_index("y") + lax.axis_index("c")
    n_ada = w_ada.shape[2]

    shard = lambda w, l, transposed: (jnp.swapaxes(w[l], 0, 1) if transposed else w[l]).astype(MXU)
    attn_local = lambda l: [shard(w_in, l, True), shard(w_uq, l, True), shard(w_ukv, l, True), shard(w_out, l, False)]
    ffn_local = lambda l: [shard(w_up, l, True), shard(w_down, l, False)]
    full = lambda a: a.reshape(-1, a.shape[-1])
    zrows = lambda n: jnp.zeros((n, d), MXU)
    pad_in = lambda wt: jnp.concatenate([wt[:1152], wt[1184:1824], zrows(64), wt[1152:1184], zrows(160)], axis=0)
    pad_uq = lambda wt: jnp.pad(wt.reshape(MLA_HEADS, MLA_QK, 256), ((0, 0), (0, LANES - MLA_QK), (0, 0))).reshape(768, 256)
    got = _all_gather(attn_local(0) + [conv_w.reshape(-1, conv_w.shape[-1]), c], "gather_inputs")
    w_in_pt, w_uq_pt, w_ukv_t, w_out_f = [pad_in(full(got[0]))], [pad_uq(full(got[1]))], [full(got[2])], [full(got[3])]
    w_up_t, w_down_f = [], []
    conv_full = got[4].reshape(N_DEV, nl, 3, -1).transpose(1, 2, 0, 3).reshape(nl, 3, -1)
    c_all = got[5].reshape(N_DEV * nb, d)

    b_my = lax.dynamic_slice_in_dim(b_ada, me * n_ada, n_ada, axis=1).reshape(nl, 1, n_ada)
    mods_my = _ada_fwd(c_all, w_ada, b_my, "ada_fwd")
    mods, = _all_gather([mods_my.reshape(nl * N_DEV * nb, n_ada)], "gather_mods")
    mods = mods.reshape(N_DEV, nl, N_DEV * nb, n_ada).transpose(1, 2, 0, 3).reshape(nl, N_DEV * nb, N_DEV * n_ada)
    mods = lax.dynamic_slice_in_dim(mods, me * nb, nb, axis=1)
    shift1, scale1, gate1, shift2, scale2, gate2 = [mods[:, :, k * d:(k + 1) * d].reshape(nl, nb, 1, d) for k in range(6)]

    half = 16
    inv_freq = jnp.power(ROPE_THETA, -jnp.arange(half, dtype=F32) / half)
    ang = positions.astype(F32)[..., None] * inv_freq
    ones = lambda n: jnp.ones((nb, s, n), F32)
    zeros = lambda n: jnp.zeros((nb, s, n), F32)
    rope_c = jnp.concatenate([ones(64), jnp.cos(ang), jnp.cos(ang), ones(32)], axis=-1)
    rope_s = jnp.concatenate([zeros(64), jnp.sin(ang), jnp.sin(ang), zeros(32)], axis=-1)
    bucket = _t5_bucket()
    bias = _bias_build(jnp.pad(rel_table.T, ((0, 8 - SW_HEADS), (0, LANES - REL_BUCKETS))), bucket, "rel_bias")

    row = lambda g: g.reshape(1, -1)
    twice = lambda g: jnp.concatenate([g, g]).reshape(1, LANES)

    saved = []
    xl = x
    for l in range(nl):
        proj, h1 = _ln_mod_matmul(xl, row(norm1_g[l]), scale1[l], shift1[l], w_in_pt[l], f"l{l}_in_proj")
        prep_args = (proj, rope_c, rope_s, row(mla_cq_g[l]), row(mla_ckv_g[l]), _pad_lanes(mla_qn_g[l], LANES),
                     _pad_lanes(mla_kn_g[l], LANES), w_uq_pt[l], w_ukv_t[l])
        qm, km, vm = _mla_prep(*prep_args, f"l{l}_mla_prep")
        o_a, ct_a, cnt_a, up_g, down_g = _sb_fwd(proj, f"l{l}_sb_fwd", riding=ffn_local(l))
        w_up_t.append(full(up_g))
        w_down_f.append(full(down_g))
        o_b, lse_b, *nxt = _mla_fwd(qm, km, vm, f"l{l}_mla_fwd", riding=attn_local(l + 1) if l + 1 < nl else ())
        if nxt:
            w_in_pt.append(pad_in(full(nxt[0])))
            w_uq_pt.append(pad_uq(full(nxt[1])))
            w_ukv_t.append(full(nxt[2]))
            w_out_f.append(full(nxt[3]))
        sinks = jnp.broadcast_to(jnp.pad(sw_sinks[l], (0, 2))[:, None], (8, LANES))
        swa_args = (proj, twice(sw_qn_g[l]), twice(sw_kn_g[l]), sinks, bias)
        o_c = _swa_fwd(*swa_args, f"l{l}_swa_fwd")
        wo = [w_out_f[l][:256], w_out_f[l][256:640], w_out_f[l][640:]]
        x_mid, y1 = _out_proj([o_a, o_b, o_c], wo, gate1[l], xl, f"l{l}_out_proj")
        u_pre, h2 = _ln_mod_matmul(x_mid, row(norm2_g[l]), scale2[l], shift2[l], w_up_t[l], f"l{l}_up_proj")
        x_out, y2 = _conv_gate_matmul(u_pre, conv_full[l], row(conv_b[l]), w_down_f[l], gate2[l], x_mid, f"l{l}_ffn_down")
        saved.append(dict(x=xl, proj=proj, h1=h1, prep=prep_args, qkv=(qm, km, vm), o_a=o_a, ct_a=ct_a, cnt_a=cnt_a, o_b=o_b, lse_b=lse_b,
                          swa=swa_args, o_c=o_c, wo=wo, y1=y1, x_mid=x_mid, u_pre=u_pre, h2=h2, y2=y2))
        xl = x_out

    loss_blk, dx = _loss_grad(xl, loss_target, "loss")
    loss = lax.psum(loss_blk[0, 0], ("x", "y", "c"))

    t = nb * s
    flat = lambda a: a.reshape(t, a.shape[-1])
    grads = [None] * nl
    dmods = [None] * nl
    sharded_out = [None] * nl
    sharded_names = ["w_in", "w_uq", "w_ukv", "w_up", "w_out", "w_down", "conv_w"]
    sharded_wmv = dict(w_in=(w_in, m_w_in, v_w_in), w_uq=(w_uq, m_w_uq, v_w_uq), w_ukv=(w_ukv, m_w_ukv, v_w_ukv),
                       w_up=(w_up, m_w_up, v_w_up), w_out=(w_out, m_w_out, v_w_out), w_down=(w_down, m_w_down, v_w_down),
                       conv_w=(conv_w, m_conv_w, v_conv_w))
    n_in, n_up, n_out, n_dn = w_in.shape[2], w_up.shape[2], w_out.shape[1], w_down.shape[1]
    small_sizes = [w_uq[0].size, w_ukv[0].size, conv_w[0].size]
    n_small_rows = -(-sum(small_sizes) // d)
    rows_used = n_in + n_out + n_small_rows
    rows_grad = -(-rows_used // 16) * 16

    def pack_rows(mats, vecs):
        lead = mats[0].shape[:-2]
        flat_part = jnp.concatenate(vecs, axis=-1)
        flat_part = jnp.pad(flat_part, [(0, 0)] * len(lead) + [(0, n_small_rows * d - flat_part.shape[-1])])
        tail = jnp.zeros(lead + (rows_grad - rows_used, d), F32)
        return jnp.concatenate(list(mats) + [flat_part.reshape(lead + (n_small_rows, d)), tail], axis=-2)

    def unpack_rows(a):
        o1, o2 = n_in, n_in + n_out
        flat_part = a[o2:o2 + n_small_rows].reshape(-1)
        s1, s2, s3 = small_sizes[0], small_sizes[0] + small_sizes[1], sum(small_sizes)
        return dict(w_in=a[:o1].T, w_out=a[o1:o2],
                    w_uq=flat_part[:s1].reshape(w_uq.shape[2], -1).T, w_ukv=flat_part[s1:s2].reshape(w_ukv.shape[2], -1).T,
                    conv_w=flat_part[s2:s3].reshape(conv_w.shape[1:]))

    ffn_out = [None] * nl
    core = lax.axis_index("c").reshape(1).astype(jnp.int32)

    def update_ffn(l, recv):
        wmv = [{k: v[o][l] for k, v in sharded_wmv.items()} for o in range(3)]
        res_up = _adamw(recv[0], *[a["w_up"].T for a in wmv], f"l{l}_adamw_up")
        res_dn = _adamw(recv[1], *[a["w_down"] for a in wmv], f"l{l}_adamw_down")
        ffn_out[l] = [dict(w_up=ru.T, w_down=rd) for ru, rd in zip(res_up, res_dn)]

    def update_rest(l, recv):
        wmv = [{k: v[o][l] for k, v in sharded_wmv.items()} for o in range(3)]
        res_rest = _adamw(recv, *[pack_rows([a["w_in"].T, a["w_out"]], [a["w_uq"].T.reshape(-1), a["w_ukv"].T.reshape(-1),
                                                                         a["conv_w"].reshape(-1)]) for a in wmv],
                          f"l{l}_adamw_rest")
        sharded_out[l] = [dict(unpack_rows(rr), **ff) for rr, ff in zip(res_rest, ffn_out[l])]

    pending = None
    dbias = jnp.zeros((SW_HEADS, BLOCK, 2 * BLOCK), F32)
    for l in reversed(range(nl)):
        sv = saved[l]
        (da,), dy2, dgate2 = _gate_bwd_nt(dx, sv["y2"], gate2[l], [w_down_f[l]], f"l{l}_ffn_down_bwd")
        du, a_act, cstats = _conv_gate_bwd(da, sv["u_pre"], conv_full[l], row(conv_b[l]), f"l{l}_conv_gate_bwd")
        res = _ln_mod_matmul_bwd(du, w_up_t[l], sv["x_mid"], row(norm2_g[l]), scale2[l], dx, conv_full[l],
                                 f"l{l}_up_proj_bwd", riding=[pending[1]] if pending else ())
        dx_mid, du_pre, dshift2, dscale2, dg2 = res[:5]
        if pending:
            update_rest(pending[0], res[5])
            pending = None
        g_w_down = _wgrad(flat(a_act), flat(dy2), f"l{l}_w_down_grad")
        g_w_up_t = _wgrad(flat(du_pre), flat(sv["h2"]), f"l{l}_w_up_grad")
        per_dev = lambda g: g.reshape(N_DEV, -1, d)
        ffn_send = [per_dev(g_w_up_t), per_dev(g_w_down)]
        ffn_sib = _pair_exchange(ffn_send, f"l{l}_pair_exchange_ffn")
        ffn_pair = [_pair_add(core, a, b, f"l{l}_pair_add_{k}") for a, b, k in zip(ffn_send, ffn_sib, ("up", "down"))]

        (do_a, do_b, do_c), dy1, dgate1 = _gate_bwd_nt(dx_mid, sv["y1"], gate1[l], sv["wo"], f"l{l}_out_proj_bwd")
        mix = jnp.concatenate([sv["o_a"], sv["o_b"], sv["o_c"]], axis=-1).astype(MXU)
        g_w_out = _wgrad(flat(mix), flat(dy1), f"l{l}_w_out_grad")

        dsb_q, dsb_k, dsb_v = _sb_bwd(sv["proj"], sv["ct_a"], sv["cnt_a"], do_a, f"l{l}_sb_bwd")
        qm, km, vm = sv["qkv"]
        dqm, dkm, dvm, *ffn_recv = _mla_bwd(qm, km, vm, sv["o_b"], sv["lse_b"], do_b, f"l{l}_mla_bwd", riding=ffn_pair)
        update_ffn(l, ffn_recv)
        dsw_q, dkc, dkp, dvc, dvp, dbias_l, dsinks, dg_swq, dg_swk = _swa_bwd(*sv["swa"], do_c, f"l{l}_swa_bwd")
        dbias = dbias + dbias_l
        shift_up = lambda a: jnp.concatenate([a[:, BLOCK:], jnp.zeros((nb, BLOCK, LANES), F32)], axis=1)
        dsw_k = dkc + shift_up(dkp)
        dsw_v = dvc + shift_up(dvp)
        dcq, dckv, dkr, g_w_uq_pt, g_w_ukv_t, dg_cq, dg_ckv, dg_qn, dg_kn = _mla_prep_bwd(
            *sv["prep"], dqm, dkm, dvm, f"l{l}_mla_prep_bwd")
        dproj = jnp.concatenate([dsb_q, dsb_k, dsb_v, dcq, dckv, dsw_q, dsw_k, dsw_v, dkr, zeros(128)], axis=-1).astype(MXU)
        dx, dproj_m, dshift1, dscale1, dg1 = _ln_mod_matmul_bwd(
            dproj, w_in_pt[l], sv["x"], row(norm1_g[l]), scale1[l], dx_mid, None, f"l{l}_in_proj_bwd")
        g_w_in_pt = _wgrad(flat(dproj_m), flat(sv["h1"]), f"l{l}_w_in_grad")

        g_w_in_t = jnp.concatenate([g_w_in_pt[:1152], g_w_in_pt[1856:1888], g_w_in_pt[1152:1792]], axis=0)
        g_w_uq_t = g_w_uq_pt.reshape(MLA_HEADS, LANES, 256)[:, :MLA_QK].reshape(MLA_HEADS * MLA_QK, 256)
        dmods[l] = jnp.concatenate([dshift1, dscale1, dgate1, dshift2, dscale2, dgate2], axis=-1).reshape(nb, 6 * d)

        conv_dev = cstats[1:4].reshape(3, N_DEV, -1).transpose(1, 0, 2)
        rest = pack_rows([per_dev(g_w_in_t), per_dev(g_w_out)],
                         [g_w_uq_t.reshape(N_DEV, -1), g_w_ukv_t.reshape(N_DEV, -1), conv_dev.reshape(N_DEV, -1)])
        rest_sib, = _pair_exchange([rest], f"l{l}_pair_exchange_rest")
        rest_pair = _pair_add(core, rest, rest_sib, f"l{l}_pair_add_rest")
        if l > 0:
            pending = (l, rest_pair)
        else:
            update_rest(l, _chip_exchange([rest_pair], f"l{l}_chip_exchange")[0])
        grads[l] = dict(
            norm1_g=dg1[0], norm2_g=dg2[0], mla_cq_g=dg_cq[0], mla_ckv_g=dg_ckv[0], mla_qn_g=dg_qn[0, :MLA_QK],
            mla_kn_g=dg_kn[0, :MLA_QK], sw_qn_g=dg_swq[0, :HEAD] + dg_swq[0, HEAD:], sw_kn_g=dg_swk[0, :HEAD] + dg_swk[0, HEAD:],
            sw_sinks=dsinks[:SW_HEADS, 0], conv_b=cstats[0])
    grad_x = dx
    g_rel = _bias_grad(dbias, bucket, "rel_table_grad")[:SW_HEADS, :REL_BUCKETS].T
    stack = lambda k: jnp.stack([grads[l][k] for l in range(nl)])

    dm_all, = _all_gather([jnp.stack(dmods).reshape(nl * nb, 6 * d)], "gather_dmods")
    dm_all = dm_all.reshape(N_DEV, nl, nb, 6 * d).transpose(1, 0, 2, 3).reshape(nl, N_DEV * nb, 6 * d)
    dm_my = lax.dynamic_slice_in_dim(dm_all, me * n_ada, n_ada, axis=2)
    g_w_ada, g_b_ada = _ada_bwd(c_all, dm_my, dm_all, "ada_bwd")
    g_b_ada = g_b_ada.reshape(nl, 6 * d)

    big_out = [{k: jnp.stack([sharded_out[l][o][k] for l in range(nl)]) for k in sharded_names} for o in range(4)]
    packf = lambda dct, names, rows: jnp.pad(jnp.concatenate([dct[k].reshape(-1) for k in names]),
                                             (0, rows * LANES - sum(dct[k].size for k in names))).reshape(rows, LANES)

    small_names = ["rel_table", "norm1_g", "norm2_g", "mla_cq_g", "mla_ckv_g", "mla_qn_g", "mla_kn_g",
                   "sw_qn_g", "sw_kn_g", "sw_sinks", "conv_b"]
    small_w = dict(rel_table=rel_table, norm1_g=norm1_g, norm2_g=norm2_g, mla_cq_g=mla_cq_g, mla_ckv_g=mla_ckv_g,
                   mla_qn_g=mla_qn_g, mla_kn_g=mla_kn_g, sw_qn_g=sw_qn_g, sw_kn_g=sw_kn_g, sw_sinks=sw_sinks, conv_b=conv_b)
    small_m = dict(rel_table=m_rel_table, norm1_g=m_norm1_g, norm2_g=m_norm2_g, mla_cq_g=m_mla_cq_g, mla_ckv_g=m_mla_ckv_g,
                   mla_qn_g=m_mla_qn_g, mla_kn_g=m_mla_kn_g, sw_qn_g=m_sw_qn_g, sw_kn_g=m_sw_kn_g, sw_sinks=m_sw_sinks, conv_b=m_conv_b)
    small_v = dict(rel_table=v_rel_table, norm1_g=v_norm1_g, norm2_g=v_norm2_g, mla_cq_g=v_mla_cq_g, mla_ckv_g=v_mla_ckv_g,
                   mla_qn_g=v_mla_qn_g, mla_kn_g=v_mla_kn_g, sw_qn_g=v_sw_qn_g, sw_kn_g=v_sw_kn_g, sw_sinks=v_sw_sinks, conv_b=v_conv_b)
    small_g = {k: (g_rel if k == "rel_table" else stack(k)) for k in small_names}
    n_small = sum(small_w[k].size for k in small_names)
    rows_small = -(-n_small // (8 * LANES)) * 8
    small_parts, = _all_gather([packf(small_g, small_names, rows_small)], "gather_small_grads")
    small_out = _adamw(small_parts, packf(small_w, small_names, rows_small), packf(small_m, small_names, rows_small),
                       packf(small_v, small_names, rows_small), "adamw_replicated")
    small_out = [dict(zip(small_names, _unpack(o.reshape(-1), [small_w[k].shape for k in small_names]))) for o in small_out]

    two_d = lambda a: a.reshape(-1, a.shape[-1])
    res_w = _adamw(two_d(g_w_ada)[None], two_d(w_ada), two_d(m_w_ada), two_d(v_w_ada), "adamw_w_ada")
    res_b = _adamw(g_b_ada[None], b_ada, m_b_ada, v_b_ada, "adamw_b_ada")
    ada_out = [dict(w_ada=rw.reshape(w_ada.shape), b_ada=rb) for rw, rb in zip(res_w, res_b)]

    order = ["rel_table", "norm1_g", "norm2_g", "w_ada", "b_ada", "w_in", "mla_cq_g", "w_uq", "mla_ckv_g", "w_ukv",
             "mla_qn_g", "mla_kn_g", "sw_qn_g", "sw_kn_g", "sw_sinks", "w_out", "w_up", "conv_w", "conv_b", "w_down"]
    outs = [{**big_out[k], **small_out[k], **ada_out[k]} for k in range(4)]
    return (loss, grad_x, *[outs[0][n] for n in order], *[outs[1][n] for n in order],
            *[outs[2][n] for n in order], *[outs[3][n] for n in order])
```

```python
import math

import jax
import jax.numpy as jnp
from jax import lax
from jax.experimental import pallas as pl
from jax.experimental.pallas import tpu as pltpu

F32 = jnp.float32
MXU = jnp.bfloat16
EPS = 1e-6
NEG = -1e30
VMEM_LIMIT_BYTES = 56 * 1024 * 1024
N_DEV = 8
MESH = pl.DeviceIdType.MESH

D_MODEL = 1024
D_FF = 2816
HEAD = 64
LANES = 128
MLA_HEADS = 6
MLA_QK = 96
SW_HEADS = 6
REL_BUCKETS = 32
BLOCK = 128
SB_SCALE = HEAD ** -0.5
SB_DEAD = -105.0
SW_SCALE = HEAD ** -0.5
MLA_SCALE = MLA_QK ** -0.5
ROPE_THETA = 10000.0
D_IN_PAD = 2048
COL_SBQ, COL_SBK, COL_SBV, COL_CQ, COL_CKV, COL_SWQ, COL_SWK, COL_SWV, COL_KR = 0, 256, 512, 768, 1024, 1152, 1536, 1664, 1792

HALO = 16
ROW_TILE_BYTES = 1 << 20
ADAM_LR, ADAM_B1, ADAM_B2, ADAM_EPS, ADAM_WD, ADAM_STEP = 0.001, 0.9, 0.999, 1e-08, 0.01, 10


def _cp(*sem):
    return pltpu.CompilerParams(dimension_semantics=sem, vmem_limit_bytes=VMEM_LIMIT_BYTES)


def _iota(shape, dim):
    return lax.broadcasted_iota(jnp.int32, shape, dim)


def _dot(a, b):
    return jnp.dot(a, b, preferred_element_type=F32)


def _dot_nt(a, b):
    return lax.dot_general(a, b, (((1,), (1,)), ((), ())), preferred_element_type=F32)


def _dot_tn(a, b):
    return lax.dot_general(a, b, (((0,), (0,)), ((), ())), preferred_element_type=F32)


def _cumdot(x, u):
    hi = x.astype(MXU)
    mid = (x - hi.astype(F32)).astype(MXU)
    return _dot(hi, u) + _dot(mid, u)


def _sigmoid(x):
    return 1.0 / (1.0 + jnp.exp(-x))


def _rsum(x):
    return jnp.sum(x, axis=-1, keepdims=True)


def _csum(x):
    return jnp.sum(x, axis=0, keepdims=True)


def _all_gather(xs, name):
    na = len(xs)

    def body(*refs):
        start, finish = _gather_steps(refs[:na], refs[na:2 * na], *refs[2 * na:])
        start()
        finish()

    hbm = pl.BlockSpec(memory_space=pl.ANY)
    return pl.pallas_call(
        body, name=name, out_shape=_gather_out_shapes(xs), in_specs=[hbm] * na, out_specs=[hbm] * na,
        scratch_shapes=_gather_sems(na))(*xs)


def _gather_out_shapes(xs):
    return [jax.ShapeDtypeStruct((N_DEV,) + a.shape, a.dtype) for a in xs]


def _gather_sems(na):
    return [pltpu.SemaphoreType.DMA((7 * na,)), pltpu.SemaphoreType.DMA((7 * na,)), pltpu.SemaphoreType.DMA((na,))]


def _gather_steps(x_refs, out_refs, send_sems, recv_sems, local_sems):
    na = len(x_refs)
    x, y, c = lax.axis_index("x"), lax.axis_index("y"), lax.axis_index("c")
    me, sibling = (x, y, c), (x, y, 1 - c)
    chips = [(1 - x, y), (x, 1 - y), (1 - x, 1 - y)]

    def slot(a, px, py, pc):
        return out_refs[a].at[4 * px + 2 * py + pc]

    def copy(a, k, block, to, src=None):
        return pltpu.make_async_remote_copy(
            src_ref=slot(a, *block) if src is None else src, dst_ref=slot(a, *block),
            send_sem=send_sems.at[7 * a + k], recv_sem=recv_sems.at[7 * a + k], device_id=to, device_id_type=MESH)

    def own_copies(a):
        return ([copy(a, 0, me, sibling, src=x_refs[a])]
                + [copy(a, 1 + j, me, (*chip, c), src=x_refs[a]) for j, chip in enumerate(chips)])

    def local_copy(a):
        return pltpu.make_async_copy(x_refs[a], slot(a, *me), local_sems.at[a])

    def start():
        for a in range(na):
            local_copy(a).start()
            for cp in own_copies(a):
                cp.start()

    def finish():
        passed = []
        for j, chip in enumerate(chips):
            for a in range(na):
                copy(a, 1 + j, (*chip, c), me).wait_recv()
                passed.append(copy(a, 4 + j, (*chip, c), sibling))
                passed[-1].start()
        for a in range(na):
            copy(a, 0, sibling, me).wait_recv()
            for j, chip in enumerate(chips):
                copy(a, 4 + j, (*chip, 1 - c), me).wait_recv()
        for a in range(na):
            for cp in own_copies(a):
                cp.wait_send()
        for cp in passed:
            cp.wait_send()
        for a in range(na):
            local_copy(a).wait()

    return start, finish


def _with_gather(body, n_in, n_out, na, grid):
    if not na:
        return body

    def wrapped(*refs):
        ins, ride_in = refs[:n_in], refs[n_in:n_in + na]
        outs = refs[n_in + na:n_in + na + n_out]
        ride_out = refs[n_in + na + n_out:n_in + 2 * na + n_out]
        ids = [pl.program_id(k) for k in range(len(grid))]
        first, last = ids[0] == 0, ids[0] == grid[0] - 1
        for k in range(1, len(grid)):
            first, last = first & (ids[k] == 0), last & (ids[k] == grid[k] - 1)
        start, finish = _gather_steps(ride_in, ride_out, *refs[n_in + 2 * na + n_out:])
        pl.when(first)(start)
        body(*ins, *outs)
        pl.when(last)(finish)

    return wrapped


def _pair_exchange(xs, name):
    na = len(xs)

    def body(*refs):
        x_refs, out_refs = refs[:na], refs[na:2 * na]
        send_sems, recv_sems = refs[2 * na:]
        x, y, c = lax.axis_index("x"), lax.axis_index("y"), lax.axis_index("c")
        copies = []
        for a in range(na):
            for q in range(4):
                copies.append(pltpu.make_async_remote_copy(
                    src_ref=x_refs[a].at[2 * q + 1 - c], dst_ref=out_refs[a].at[q],
                    send_sem=send_sems.at[4 * a + q], recv_sem=recv_sems.at[4 * a + q],
                    device_id=(x, y, 1 - c), device_id_type=MESH))
                copies[-1].start()
        for cp in copies:
            cp.wait()

    hbm = pl.BlockSpec(memory_space=pl.ANY)
    return pl.pallas_call(
        body, name=name, out_shape=[jax.ShapeDtypeStruct((4,) + a.shape[1:], a.dtype) for a in xs],
        in_specs=[hbm] * na, out_specs=[hbm] * na,
        scratch_shapes=[pltpu.SemaphoreType.DMA((4 * na,)), pltpu.SemaphoreType.DMA((4 * na,))])(*xs)


def _row_tile(r, ncol):
    if r * ncol * 4 <= ROW_TILE_BYTES:
        return r
    return max(t for t in range(16, r, 16) if r % t == 0 and t * ncol * 4 <= ROW_TILE_BYTES)


def _pair_add(core, xs, sib, name):
    _, r, ncol = xs.shape
    tr = _row_tile(r, ncol)

    def body(c_ref, x_ref, s_ref, o_ref):
        o_ref[...] = (x_ref[...] + s_ref[...]).astype(MXU)

    return pl.pallas_call(
        body, name=name,
        grid_spec=pltpu.PrefetchScalarGridSpec(
            num_scalar_prefetch=1, grid=(4, r // tr),
            in_specs=[pl.BlockSpec((None, tr, ncol), lambda q, i, c_ref: (2 * q + c_ref[0], i, 0)),
                      pl.BlockSpec((None, tr, ncol), lambda q, i, c_ref: (q, i, 0))],
            out_specs=pl.BlockSpec((None, tr, ncol), lambda q, i, c_ref: (q, i, 0))),
        out_shape=jax.ShapeDtypeStruct((4, r, ncol), MXU),
        compiler_params=_cp("parallel", "parallel"))(core, xs, sib)


def _chip_exchange(xs, name):
    na = len(xs)

    def body(*refs):
        copies = _chip_exchange_copies(refs[:na], refs[na:2 * na], *refs[2 * na:])
        for cp in copies:
            cp.start()
        for cp in copies:
            cp.wait()

    hbm = pl.BlockSpec(memory_space=pl.ANY)
    return pl.pallas_call(
        body, name=name, out_shape=[jax.ShapeDtypeStruct(a.shape, a.dtype) for a in xs],
        in_specs=[hbm] * na, out_specs=[hbm] * na, scratch_shapes=_chip_exchange_sems(na))(*xs)


def _with_chip_exchange(body, n_in, n_out, na, grid):
    if not na:
        return body

    def wrapped(*refs):
        ins, ride_in = refs[:n_in], refs[n_in:n_in + na]
        outs = refs[n_in + na:n_in + na + n_out]
        ride_out = refs[n_in + na + n_out:n_in + 2 * na + n_out]
        ids = [pl.program_id(k) for k in range(len(grid))]
        first, last = ids[0] == 0, ids[0] == grid[0] - 1
        for k in range(1, len(grid)):
            first, last = first & (ids[k] == 0), last & (ids[k] == grid[k] - 1)
        copies = _chip_exchange_copies(ride_in, ride_out, *refs[n_in + 2 * na + n_out:])

        @pl.when(first)
        def _():
            for cp in copies:
                cp.start()

        body(*ins, *outs)

        @pl.when(last)
        def _():
            for cp in copies:
                cp.wait()

    return wrapped


def _chip_exchange_sems(na):
    return [pltpu.SemaphoreType.DMA((3 * na,)), pltpu.SemaphoreType.DMA((3 * na,)), pltpu.SemaphoreType.DMA((na,))]


def _chip_exchange_copies(x_refs, out_refs, send_sems, recv_sems, local_sems):
    x, y, c = lax.axis_index("x"), lax.axis_index("y"), lax.axis_index("c")
    me = 2 * x + y
    copies = [pltpu.make_async_copy(x_refs[a].at[me], out_refs[a].at[me], local_sems.at[a]) for a in range(len(x_refs))]
    for k, (dx, dy) in enumerate([(1, 0), (0, 1), (1, 1)]):
        px = 1 - x if dx else x
        py = 1 - y if dy else y
        for a in range(len(x_refs)):
            copies.append(pltpu.make_async_remote_copy(
                src_ref=x_refs[a].at[2 * px + py], dst_ref=out_refs[a].at[me],
                send_sem=send_sems.at[3 * a + k], recv_sem=recv_sems.at[3 * a + k],
                device_id=(px, py, c), device_id_type=MESH))
    return copies


def _ada_fwd(c_all, w_ada, b_my, name):
    nl, d, n = w_ada.shape
    nb = c_all.shape[0]

    def body(c_ref, w_ref, b_ref, o_ref):
        cv = c_ref[...]
        sc = (cv * _sigmoid(cv)).astype(MXU)
        o_ref[...] = _dot(sc, w_ref[...].astype(MXU)) + b_ref[...]

    return pl.pallas_call(
        body, name=name, grid=(nl,),
        in_specs=[pl.BlockSpec((nb, d), lambda l: (0, 0)),
                  pl.BlockSpec((None, d, n), lambda l: (l, 0, 0)),
                  pl.BlockSpec((None, 1, n), lambda l: (l, 0, 0))],
        out_specs=pl.BlockSpec((None, nb, n), lambda l: (l, 0, 0)),
        out_shape=jax.ShapeDtypeStruct((nl, nb, n), F32),
        compiler_params=_cp("parallel"))(c_all, w_ada, b_my)


def _ada_bwd(c_all, dmods_my, dmods_all, name):
    nl, nb, n = dmods_my.shape
    d = c_all.shape[1]
    nfull = dmods_all.shape[2]

    def body(c_ref, dm_ref, da_ref, dw_ref, db_ref):
        cv = c_ref[...]
        sc = (cv * _sigmoid(cv)).astype(MXU)
        dw_ref[...] = _dot_tn(sc, dm_ref[...].astype(MXU))
        db_ref[...] = _csum(da_ref[...])

    return pl.pallas_call(
        body, name=name, grid=(nl,),
        in_specs=[pl.BlockSpec((nb, d), lambda l: (0, 0)),
                  pl.BlockSpec((None, nb, n), lambda l: (l, 0, 0)),
                  pl.BlockSpec((None, nb, nfull), lambda l: (l, 0, 0))],
        out_specs=[pl.BlockSpec((None, d, n), lambda l: (l, 0, 0)),
                   pl.BlockSpec((None, 1, nfull), lambda l: (l, 0, 0))],
        out_shape=[jax.ShapeDtypeStruct((nl, d, n), F32), jax.ShapeDtypeStruct((nl, 1, nfull), F32)],
        compiler_params=_cp("parallel"))(c_all, dmods_my, dmods_all)


def _ln_mod_matmul(x, g, scale, shift, w, name):
    nb, s, d = x.shape
    n = w.shape[0]
    tm, tn = min(1024, s), (1408 if n % 1408 == 0 else 1024)

    def body(x_ref, g_ref, sc_ref, sh_ref, w_ref, y_ref, h_ref, h_s):
        @pl.when(pl.program_id(2) == 0)
        def _():
            xf = x_ref[...]
            rstd = lax.rsqrt(jnp.mean(xf * xf, axis=-1, keepdims=True) + EPS)
            hv = (xf * rstd * g_ref[...]) * (1.0 + sc_ref[...]) + sh_ref[...]
            h_s[...] = hv.astype(MXU)
            h_ref[...] = h_s[...]

        y_ref[...] = _dot_nt(h_s[...], w_ref[...]).astype(MXU)

    return pl.pallas_call(
        body, name=name, grid=(nb, s // tm, n // tn),
        in_specs=[pl.BlockSpec((None, tm, d), lambda b, i, j: (b, i, 0)),
                  pl.BlockSpec((1, d), lambda b, i, j: (0, 0)),
                  pl.BlockSpec((None, 1, d), lambda b, i, j: (b, 0, 0)),
                  pl.BlockSpec((None, 1, d), lambda b, i, j: (b, 0, 0)),
                  pl.BlockSpec((tn, d), lambda b, i, j: (j, 0))],
        out_specs=[pl.BlockSpec((None, tm, tn), lambda b, i, j: (b, i, j)),
                   pl.BlockSpec((None, tm, d), lambda b, i, j: (b, i, 0))],
        out_shape=[jax.ShapeDtypeStruct((nb, s, n), MXU), jax.ShapeDtypeStruct((nb, s, d), MXU)],
        scratch_shapes=[pltpu.VMEM((tm, d), MXU)],
        compiler_params=_cp("parallel", "parallel", "arbitrary"))(x, g, scale, shift, w)


def _ln_mod_matmul_bwd(dy, w, x, g, scale, dres, conv_w, name, riding=()):
    nb, s, n = dy.shape
    d = x.shape[-1]
    tm, tn = min(512, s), (1408 if n % 1408 == 0 else 1024)
    ni, nj = s // tm, n // tn
    hb = tm // HALO
    conv = conv_w is not None
    na = len(riding)

    def body(*refs):
        if conv:
            dy_ref, nx_ref, cw_ref = refs[:3]
            refs = refs[3:]
        else:
            dy_ref = refs[0]
            refs = refs[1:]
        w_ref, x_ref, g_ref, sc_ref, dr_ref = refs[:5]
        ride_in, refs = refs[5:5 + na], refs[5 + na:]
        dx_ref, dyp_ref, dsh_ref, dsc_ref, dg_ref = refs[:5]
        ride_out, refs = refs[5:5 + na], refs[5 + na:]
        acc = refs[0]
        b, i, j = pl.program_id(0), pl.program_id(1), pl.program_id(2)
        if na:
            copies = _chip_exchange_copies(ride_in, ride_out, *refs[1:])

            @pl.when((b == 0) & (i == 0) & (j == 0))
            def _():
                for cp in copies:
                    cp.start()

        @pl.when(j == 0)
        def _():
            acc[...] = jnp.zeros_like(acc)

        @pl.when((j == 0) & (i == 0))
        def _():
            dsh_ref[...] = jnp.zeros_like(dsh_ref)
            dsc_ref[...] = jnp.zeros_like(dsc_ref)

        @pl.when((j == 0) & (i == 0) & (b == 0))
        def _():
            dg_ref[...] = jnp.zeros_like(dg_ref)

        dv = dy_ref[...].astype(F32)
        if conv:
            rows = _iota((tm, 1), 0)
            nx = jnp.where(i == ni - 1, 0.0, nx_ref[...].astype(F32))
            n1 = jnp.where(rows == tm - 1, nx[0:1, :], pltpu.roll(dv, tm - 1, 0))
            n2 = jnp.where(rows == tm - 2, nx[0:1, :], jnp.where(rows == tm - 1, nx[1:2, :], pltpu.roll(dv, tm - 2, 0)))
            cw = cw_ref[...]
            dv = cw[2:3, :] * dv + cw[1:2, :] * n1 + cw[0:1, :] * n2
        dp = dv.astype(MXU)
        dyp_ref[...] = dp
        acc[...] += _dot(dp, w_ref[...])

        @pl.when(j == nj - 1)
        def _():
            dh = acc[...]
            xf = x_ref[...]
            rstd = lax.rsqrt(jnp.mean(xf * xf, axis=-1, keepdims=True) + EPS)
            xn = xf * rstd
            gg = g_ref[...]
            sc1 = 1.0 + sc_ref[...]
            dsh_ref[...] += _csum(dh)
            dsc_ref[...] += _csum(dh * xn * gg)
            dg_ref[...] += _csum(dh * xn * sc1)
            dn = dh * gg * sc1
            dx_ref[...] = dr_ref[...] + rstd * (dn - xn * jnp.mean(dn * xn, axis=-1, keepdims=True))

        if na:
            @pl.when((b == nb - 1) & (i == ni - 1) & (j == nj - 1))
            def _():
                for cp in copies:
                    cp.wait()

    hbm = pl.BlockSpec(memory_space=pl.ANY)
    in_specs = [pl.BlockSpec((None, tm, tn), lambda b, i, j: (b, i, j))]
    args = [dy]
    if conv:
        in_specs += [pl.BlockSpec((None, HALO, tn), lambda b, i, j: (b, jnp.minimum((i + 1) * hb, s // HALO - 1), j)),
                     pl.BlockSpec((3, tn), lambda b, i, j: (0, j))]
        args += [dy, conv_w]
    in_specs += [pl.BlockSpec((tn, d), lambda b, i, j: (j, 0)),
                 pl.BlockSpec((None, tm, d), lambda b, i, j: (b, i, 0)),
                 pl.BlockSpec((1, d), lambda b, i, j: (0, 0)),
                 pl.BlockSpec((None, 1, d), lambda b, i, j: (b, 0, 0)),
                 pl.BlockSpec((None, tm, d), lambda b, i, j: (b, i, 0))]
    in_specs += [hbm] * na
    args += [w, x, g, scale, dres, *riding]
    return pl.pallas_call(
        body, name=name, grid=(nb, ni, nj), in_specs=in_specs,
        out_specs=[pl.BlockSpec((None, tm, d), lambda b, i, j: (b, i, 0)),
                   pl.BlockSpec((None, tm, tn), lambda b, i, j: (b, i, j)),
                   pl.BlockSpec((None, 1, d), lambda b, i, j: (b, 0, 0)),
                   pl.BlockSpec((None, 1, d), lambda b, i, j: (b, 0, 0)),
                   pl.BlockSpec((1, d), lambda b, i, j: (0, 0))] + [hbm] * na,
        out_shape=[jax.ShapeDtypeStruct((nb, s, d), F32), jax.ShapeDtypeStruct((nb, s, n), MXU),
                   jax.ShapeDtypeStruct((nb, 1, d), F32), jax.ShapeDtypeStruct((nb, 1, d), F32),
                   jax.ShapeDtypeStruct((1, d), F32)] + [jax.ShapeDtypeStruct(a.shape, a.dtype) for a in riding],
        scratch_shapes=[pltpu.VMEM((tm, d), F32)] + (_chip_exchange_sems(na) if na else []),
        compiler_params=_cp("arbitrary", "arbitrary", "arbitrary"))(*args)


def _wgrad(xm, dym, name):
    t, k = xm.shape
    n = dym.shape[1]
    tk = 1408 if k % 1408 == 0 else 1024
    tt = min(1024, t)

    def body(x_ref, dy_ref, o_ref):
        @pl.when(pl.program_id(1) == 0)
        def _():
            o_ref[...] = jnp.zeros_like(o_ref)

        o_ref[...] += _dot_tn(x_ref[...], dy_ref[...])

    return pl.pallas_call(
        body, name=name, grid=(k // tk, t // tt),
        in_specs=[pl.BlockSpec((tt, tk), lambda a, c: (c, a)),
                  pl.BlockSpec((tt, n), lambda a, c: (c, 0))],
        out_specs=pl.BlockSpec((tk, n), lambda a, c: (a, 0)),
        out_shape=jax.ShapeDtypeStruct((k, n), F32),
        compiler_params=_cp("parallel", "arbitrary"))(xm, dym)


def _out_proj(parts, ws, gate, res, name):
    nb, s, d = res.shape
    tm = min(512, s)
    npart = len(parts)

    def body(*refs):
        p_refs, w_refs = refs[:npart], refs[npart:2 * npart]
        gt_ref, res_ref, xo_ref, y_ref = refs[2 * npart:]
        y = _dot(p_refs[0][...].astype(MXU), w_refs[0][...])
        for p_ref, w_ref in zip(p_refs[1:], w_refs[1:]):
            y = y + _dot(p_ref[...].astype(MXU), w_ref[...])
        y_ref[...] = y
        xo_ref[...] = res_ref[...] + gt_ref[...] * y

    in_specs = [pl.BlockSpec((None, tm, p.shape[-1]), lambda b, i: (b, i, 0)) for p in parts]
    in_specs += [pl.BlockSpec(w.shape, lambda b, i: (0, 0)) for w in ws]
    in_specs += [pl.BlockSpec((None, 1, d), lambda b, i: (b, 0, 0)),
                 pl.BlockSpec((None, tm, d), lambda b, i: (b, i, 0))]
    return pl.pallas_call(
        body, name=name, grid=(nb, s // tm), in_specs=in_specs,
        out_specs=[pl.BlockSpec((None, tm, d), lambda b, i: (b, i, 0))] * 2,
        out_shape=[jax.ShapeDtypeStruct((nb, s, d), F32)] * 2,
        compiler_params=_cp("parallel", "parallel"))(*parts, *ws, gate, res)


def _gate_bwd_nt(dx, y, gate, ws, name):
    nb, s, d = dx.shape
    tm = min(512, s)
    npart = len(ws)

    def body(*refs):
        dx_ref, y_ref, gt_ref = refs[:3]
        w_refs = refs[3:3 + npart]
        da_refs = refs[3 + npart:3 + 2 * npart]
        dy_ref, dgt_ref = refs[3 + 2 * npart:]

        @pl.when(pl.program_id(1) == 0)
        def _():
            dgt_ref[...] = jnp.zeros_like(dgt_ref)

        dxv = dx_ref[...]
        dyv = (dxv * gt_ref[...]).astype(MXU)
        dy_ref[...] = dyv
        dgt_ref[...] += _csum(dxv * y_ref[...])
        for w_ref, da_ref in zip(w_refs, da_refs):
            da_ref[...] = _dot_nt(dyv, w_ref[...])

    tile = pl.BlockSpec((None, tm, d), lambda b, i: (b, i, 0))
    row = pl.BlockSpec((None, 1, d), lambda b, i: (b, 0, 0))
    outs = pl.pallas_call(
        body, name=name, grid=(nb, s // tm),
        in_specs=[tile, tile, row] + [pl.BlockSpec(w.shape, lambda b, i: (0, 0)) for w in ws],
        out_specs=[pl.BlockSpec((None, tm, w.shape[0]), lambda b, i: (b, i, 0)) for w in ws] + [tile, row],
        out_shape=[jax.ShapeDtypeStruct((nb, s, w.shape[0]), F32) for w in ws]
        + [jax.ShapeDtypeStruct((nb, s, d), MXU), jax.ShapeDtypeStruct((nb, 1, d), F32)],
        compiler_params=_cp("arbitrary", "arbitrary"))(dx, y, gate, *ws)
    return outs[:npart], outs[npart], outs[npart + 1]


def _conv_shifts(xv, halo, rows):
    last, before = halo[HALO - 1:HALO, :], halo[HALO - 2:HALO - 1, :]
    p1 = jnp.where(rows == 0, last, pltpu.roll(xv, 1, 0))
    p2 = jnp.where(rows == 0, before, jnp.where(rows == 1, last, pltpu.roll(xv, 2, 0)))
    return p1, p2


def _conv_gate_matmul(u, cw, cb, wd, gate, res, name):
    nb, s, f2 = u.shape
    f = f2 // 2
    d = wd.shape[1]
    tm = min(512, s)
    tk = f // 2
    nk = f // tk
    hb = tm // HALO

    def body(ug_ref, uv_ref, hg_ref, hv_ref, cwg_ref, cwv_ref, cbg_ref, cbv_ref, wd_ref, gt_ref, res_ref,
             xo_ref, y_ref, acc):
        i, k = pl.program_id(1), pl.program_id(2)

        @pl.when(k == 0)
        def _():
            acc[...] = jnp.zeros_like(acc)

        rows = _iota((tm, 1), 0)

        def conv(x_ref, h_ref, w_ref, b_ref):
            xv = x_ref[...].astype(F32)
            halo = jnp.where(i == 0, 0.0, h_ref[...].astype(F32))
            p1, p2 = _conv_shifts(xv, halo, rows)
            wv = w_ref[...]
            return wv[2:3, :] * xv + wv[1:2, :] * p1 + wv[0:1, :] * p2 + b_ref[...]

        gv = conv(ug_ref, hg_ref, cwg_ref, cbg_ref)
        vv = conv(uv_ref, hv_ref, cwv_ref, cbv_ref)
        av = gv * _sigmoid(gv) * vv
        acc[...] += _dot(av.astype(MXU), wd_ref[...])

        @pl.when(k == nk - 1)
        def _():
            y = acc[...]
            y_ref[...] = y
            xo_ref[...] = res_ref[...] + gt_ref[...] * y

    def halo_idx(off):
        return lambda b, i, k: (b, jnp.maximum(i * hb - 1, 0), k + off)

    tile = pl.BlockSpec((None, tm, d), lambda b, i, k: (b, i, 0))
    return pl.pallas_call(
        body, name=name, grid=(nb, s // tm, nk),
        in_specs=[pl.BlockSpec((None, tm, tk), lambda b, i, k: (b, i, k)),
                  pl.BlockSpec((None, tm, tk), lambda b, i, k: (b, i, k + nk)),
                  pl.BlockSpec((None, HALO, tk), halo_idx(0)),
                  pl.BlockSpec((None, HALO, tk), halo_idx(nk)),
                  pl.BlockSpec((3, tk), lambda b, i, k: (0, k)),
                  pl.BlockSpec((3, tk), lambda b, i, k: (0, k + nk)),
                  pl.BlockSpec((1, tk), lambda b, i, k: (0, k)),
                  pl.BlockSpec((1, tk), lambda b, i, k: (0, k + nk)),
                  pl.BlockSpec((tk, d), lambda b, i, k: (k, 0)),
                  pl.BlockSpec((None, 1, d), lambda b, i, k: (b, 0, 0)),
                  tile],
        out_specs=[tile, tile],
        out_shape=[jax.ShapeDtypeStruct((nb, s, d), F32)] * 2,
        scratch_shapes=[pltpu.VMEM((tm, d), F32)],
        compiler_params=_cp("parallel", "parallel", "arbitrary"))(u, u, u, u, cw, cw, cb, cb, wd, gate, res)


def _conv_gate_bwd(da, u, cw, cb, name):
    nb, s, f2 = u.shape
    f = f2 // 2
    tm = min(128, s)
    hb = tm // HALO

    def body(da_ref, u_ref, h_ref, cw_ref, cb_ref, du_ref, a_ref, st_ref):
        b, i = pl.program_id(0), pl.program_id(1)

        @pl.when((b == 0) & (i == 0))
        def _():
            st_ref[...] = jnp.zeros_like(st_ref)

        rows = _iota((tm, 1), 0)
        first = i == 0

        def conv(cs):
            xv = u_ref[:, cs].astype(F32)
            halo = jnp.where(first, 0.0, h_ref[:, cs].astype(F32))
            p1, p2 = _conv_shifts(xv, halo, rows)
            wv = cw_ref[:, cs]
            return xv, p1, p2, wv[2:3, :] * xv + wv[1:2, :] * p1 + wv[0:1, :] * p2 + cb_ref[:, cs]

        def stats(cs, du, xv, p1, p2):
            du_ref[:, cs] = du.astype(MXU)
            st_ref[0:1, cs] += _csum(du)
            st_ref[1:2, cs] += _csum(du * p2)
            st_ref[2:3, cs] += _csum(du * p1)
            st_ref[3:4, cs] += _csum(du * xv)

        for k in range(f // LANES):
            cg = slice(k * LANES, (k + 1) * LANES)
            cv = slice(f + k * LANES, f + (k + 1) * LANES)
            xg, g1, g2, gv = conv(cg)
            xv, v1, v2, vv = conv(cv)
            sg = _sigmoid(gv)
            sl = gv * sg
            a_ref[:, cg] = (sl * vv).astype(MXU)
            dav = da_ref[:, cg]
            stats(cg, dav * vv * (sg * (1.0 + gv * (1.0 - sg))), xg, g1, g2)
            stats(cv, dav * sl, xv, v1, v2)

    return pl.pallas_call(
        body, name=name, grid=(nb, s // tm),
        in_specs=[pl.BlockSpec((None, tm, f), lambda b, i: (b, i, 0)),
                  pl.BlockSpec((None, tm, f2), lambda b, i: (b, i, 0)),
                  pl.BlockSpec((None, HALO, f2), lambda b, i: (b, jnp.maximum(i * hb - 1, 0), 0)),
                  pl.BlockSpec((3, f2), lambda b, i: (0, 0)),
                  pl.BlockSpec((1, f2), lambda b, i: (0, 0))],
        out_specs=[pl.BlockSpec((None, tm, f2), lambda b, i: (b, i, 0)),
                   pl.BlockSpec((None, tm, f), lambda b, i: (b, i, 0)),
                   pl.BlockSpec((8, f2), lambda b, i: (0, 0))],
        out_shape=[jax.ShapeDtypeStruct((nb, s, f2), MXU), jax.ShapeDtypeStruct((nb, s, f), MXU),
                   jax.ShapeDtypeStruct((8, f2), F32)],
        compiler_params=_cp("arbitrary", "arbitrary"))(da, u, u, cw, cb)


def _rot(xv, lane):
    return jnp.where((lane >= 64) & (lane < 80), -pltpu.roll(xv, 112, 1),
                     jnp.where((lane >= 80) & (lane < 96), pltpu.roll(xv, 16, 1), 0.0))


def _rot_t(dv, lane):
    return jnp.where((lane >= 80) & (lane < 96), -pltpu.roll(dv, 16, 1),
                     jnp.where((lane >= 64) & (lane < 80), pltpu.roll(dv, 112, 1), 0.0))


def _mla_prep_specs(s, tm):
    def blk(width, col):
        return pl.BlockSpec((None, tm, width), lambda b, i: (b, i, col // width))

    full = lambda shape: pl.BlockSpec(shape, lambda b, i: (0, 0))
    return [blk(256, COL_CQ), blk(128, COL_CKV), blk(128, COL_KR),
            pl.BlockSpec((None, tm, LANES), lambda b, i: (b, i, 0)),
            pl.BlockSpec((None, tm, LANES), lambda b, i: (b, i, 0)),
            full((1, 256)), full((1, 128)), full((1, 128)), full((1, 128)),
            full((768, 256)), full((768, 128))]


def _mla_prep(proj, cs, sn, gcq, gckv, gqn, gkn, wuq, wukv, name):
    nb, s, _ = proj.shape
    tm = min(256, s)

    def body(cq_ref, ckv_ref, kr_ref, c_ref, s_ref, gcq_ref, gckv_ref, gqn_ref, gkn_ref, wuq_ref, wukv_ref,
             q_ref, k_ref, v_ref):
        lane = _iota((tm, LANES), 1)
        cv, sv = c_ref[...], s_ref[...]
        cq = cq_ref[...].astype(F32)
        cqn = cq * lax.rsqrt(jnp.mean(cq * cq, axis=-1, keepdims=True) + EPS) * gcq_ref[...]
        qb = _dot_nt(cqn.astype(MXU), wuq_ref[...])
        ckv = ckv_ref[...].astype(F32)
        ckvn = ckv * lax.rsqrt(jnp.mean(ckv * ckv, axis=-1, keepdims=True) + EPS) * gckv_ref[...]
        kvb = _dot_nt(ckvn.astype(MXU), wukv_ref[...])
        kr = kr_ref[...].astype(F32)
        for h in range(MLA_HEADS):
            hs = slice(h * LANES, (h + 1) * LANES)
            qh = qb[:, hs]
            qn = qh * lax.rsqrt(_rsum(qh * qh) / MLA_QK + EPS) * gqn_ref[...]
            q_ref[:, hs] = (qn * cv + _rot(qn, lane) * sv).astype(MXU)
            kc = jnp.where(lane < HEAD, kvb[:, hs], kr)
            kn = kc * lax.rsqrt(_rsum(kc * kc) / MLA_QK + EPS) * gkn_ref[...]
            k_ref[:, hs] = (kn * cv + _rot(kn, lane) * sv).astype(MXU)
        for j in range(MLA_HEADS // 2):
            va = kvb[:, (2 * j) * LANES:(2 * j + 1) * LANES]
            vb = kvb[:, (2 * j + 1) * LANES:(2 * j + 2) * LANES]
            v_ref[:, j * LANES:(j + 1) * LANES] = jnp.where(lane < HEAD, pltpu.roll(va, HEAD, 1), vb).astype(MXU)

    return pl.pallas_call(
        body, name=name, grid=(nb, s // tm), in_specs=_mla_prep_specs(s, tm),
        out_specs=[pl.BlockSpec((None, tm, 768), lambda b, i: (b, i, 0)),
                   pl.BlockSpec((None, tm, 768), lambda b, i: (b, i, 0)),
                   pl.BlockSpec((None, tm, 384), lambda b, i: (b, i, 0))],
        out_shape=[jax.ShapeDtypeStruct((nb, s, 768), MXU), jax.ShapeDtypeStruct((nb, s, 768), MXU),
                   jax.ShapeDtypeStruct((nb, s, 384), MXU)],
        compiler_params=_cp("parallel", "parallel"))(proj, proj, proj, cs, sn, gcq, gckv, gqn, gkn, wuq, wukv)


def _mla_prep_bwd(proj, cs, sn, gcq, gckv, gqn, gkn, wuq, wukv, dq, dk, dv, name):
    nb, s, _ = proj.shape
    tm = min(256, s)

    def body(cq_ref, ckv_ref, kr_ref, c_ref, s_ref, gcq_ref, gckv_ref, gqn_ref, gkn_ref, wuq_ref, wukv_ref,
             dq_ref, dk_ref, dv_ref,
             dcq_ref, dckv_ref, dkr_ref, dwuq_ref, dwukv_ref, dgcq_ref, dgckv_ref, dgqn_ref, dgkn_ref,
             dqb_s, dkvb_s):
        @pl.when((pl.program_id(0) == 0) & (pl.program_id(1) == 0))
        def _():
            for r in (dwuq_ref, dwukv_ref, dgcq_ref, dgckv_ref, dgqn_ref, dgkn_ref):
                r[...] = jnp.zeros_like(r)

        lane = _iota((tm, LANES), 1)
        cv, sv = c_ref[...], s_ref[...]
        gqn, gkn = gqn_ref[...], gkn_ref[...]
        cq = cq_ref[...].astype(F32)
        rc = lax.rsqrt(jnp.mean(cq * cq, axis=-1, keepdims=True) + EPS)
        chat = cq * rc
        cqn = (chat * gcq_ref[...]).astype(MXU)
        qb = _dot_nt(cqn, wuq_ref[...])
        ckv = ckv_ref[...].astype(F32)
        rkv = lax.rsqrt(jnp.mean(ckv * ckv, axis=-1, keepdims=True) + EPS)
        kvhat = ckv * rkv
        ckvn = (kvhat * gckv_ref[...]).astype(MXU)
        kvb = _dot_nt(ckvn, wukv_ref[...])
        kr = kr_ref[...].astype(F32)
        dgq = jnp.zeros((1, LANES), F32)
        dgk = jnp.zeros((1, LANES), F32)
        dkr = jnp.zeros((tm, LANES), F32)
        for h in range(MLA_HEADS):
            hs = slice(h * LANES, (h + 1) * LANES)
            qh = qb[:, hs]
            rq = lax.rsqrt(_rsum(qh * qh) / MLA_QK + EPS)
            qhat = qh * rq
            dqr = dq_ref[:, hs]
            dqn = dqr * cv + _rot_t(dqr * sv, lane)
            dgq = dgq + _csum(dqn * qhat)
            dyq = dqn * gqn
            dqb_s[:, hs] = (rq * (dyq - qhat * (_rsum(dyq * qhat) / MLA_QK))).astype(MXU)

            kc = jnp.where(lane < HEAD, kvb[:, hs], kr)
            rk = lax.rsqrt(_rsum(kc * kc) / MLA_QK + EPS)
            khat = kc * rk
            dkr_h = dk_ref[:, hs]
            dkn = dkr_h * cv + _rot_t(dkr_h * sv, lane)
            dgk = dgk + _csum(dkn * khat)
            dyk = dkn * gkn
            dkc = rk * (dyk - khat * (_rsum(dyk * khat) / MLA_QK))
            dkr = dkr + jnp.where(lane >= HEAD, dkc, 0.0)
            dvb = dv_ref[:, (h // 2) * LANES:(h // 2 + 1) * LANES]
            dvp = dvb if h % 2 == 1 else pltpu.roll(dvb, HEAD, 1)
            dkvb_s[:, hs] = jnp.where(lane < HEAD, dkc, dvp).astype(MXU)
        dgqn_ref[...] += dgq
        dgkn_ref[...] += dgk
        dkr_ref[...] = dkr

        dqb = dqb_s[...]
        dwuq_ref[...] += _dot_tn(dqb, cqn)
        dcqn = _dot(dqb, wuq_ref[...])
        dgcq_ref[...] += _csum(dcqn * chat)
        dyc = dcqn * gcq_ref[...]
        dcq_ref[...] = rc * (dyc - chat * jnp.mean(dyc * chat, axis=-1, keepdims=True))

        dkvb = dkvb_s[...]
        dwukv_ref[...] += _dot_tn(dkvb, ckvn)
        dckvn = _dot(dkvb, wukv_ref[...])
        dgckv_ref[...] += _csum(dckvn * kvhat)
        dykv = dckvn * gckv_ref[...]
        dckv_ref[...] = rkv * (dykv - kvhat * jnp.mean(dykv * kvhat, axis=-1, keepdims=True))

    full = lambda shape: pl.BlockSpec(shape, lambda b, i: (0, 0))
    tile = lambda width: pl.BlockSpec((None, tm, width), lambda b, i: (b, i, 0))
    return pl.pallas_call(
        body, name=name, grid=(nb, s // tm),
        in_specs=_mla_prep_specs(s, tm) + [tile(768), tile(768), tile(384)],
        out_specs=[tile(256), tile(128), tile(128), full((768, 256)), full((768, 128)),
                   full((1, 256)), full((1, 128)), full((1, 128)), full((1, 128))],
        out_shape=[jax.ShapeDtypeStruct((nb, s, 256), F32), jax.ShapeDtypeStruct((nb, s, 128), F32),
                   jax.ShapeDtypeStruct((nb, s, 128), F32),
                   jax.ShapeDtypeStruct((768, 256), F32), jax.ShapeDtypeStruct((768, 128), F32),
                   jax.ShapeDtypeStruct((1, 256), F32), jax.ShapeDtypeStruct((1, 128), F32),
                   jax.ShapeDtypeStruct((1, 128), F32), jax.ShapeDtypeStruct((1, 128), F32)],
        scratch_shapes=[pltpu.VMEM((tm, 768), MXU), pltpu.VMEM((tm, 768), MXU)],
        compiler_params=_cp("arbitrary", "arbitrary"))(
            proj, proj, proj, cs, sn, gcq, gckv, gqn, gkn, wuq, wukv, dq, dk, dv)


def _softplus(z):
    return jnp.maximum(z, 0.0) + jnp.log(1.0 + jnp.exp(-jnp.abs(z)))


def _sb_fwd(proj, name, riding=()):
    nb, s, _ = proj.shape
    tq, tk = min(256, s), min(256, s)
    ratio = tq // tk
    na = len(riding)
    grid = (nb, 2, s // tq)

    def body(q_ref, k_ref, v_ref, o_ref, ct_ref, cnt_ref):
        i = pl.program_id(2)
        lo = _iota((tq, LANES), 1) < HEAD
        qv = q_ref[...]
        q0 = jnp.where(lo, qv, 0.0).astype(MXU)
        q1 = jnp.where(lo, 0.0, qv).astype(MXU)
        usuf = (_iota((tk, tk), 0) > _iota((tk, tk), 1)).astype(MXU)
        tpos = i * tq + _iota((tq, tk), 0)
        scol = _iota((tq, tk), 1)
        nch = (i + 1) * ratio

        def alive(st):
            return (st[0] < nch) & (st[5] > SB_DEAD)

        def step(st):
            t, c0, a0, c1, a1, _ = st
            j = nch - 1 - t
            off = pl.multiple_of(j * tk, tk)
            kc = k_ref[pl.ds(off, tk), :].astype(MXU)
            vc = v_ref[pl.ds(off, tk), :].astype(MXU)
            msk = (scol + j * tk) < tpos

            def head(qm, c, a):
                z = _dot_nt(qm, kc) * SB_SCALE
                sp = _softplus(z)
                lk = jnp.where(msk, -sp, 0.0)
                w = jnp.where(msk, jnp.exp(z - sp + _cumdot(lk, usuf) + c), 0.0)
                return c + _rsum(lk), a + _dot(w.astype(MXU), vc)

            c0, a0 = head(q0, c0, a0)
            c1, a1 = head(q1, c1, a1)
            return t + 1, c0, a0, c1, a1, jnp.maximum(jnp.max(c0), jnp.max(c1))

        z1 = jnp.zeros((tq, 1), F32)
        za = jnp.zeros((tq, LANES), F32)
        t, c0, a0, c1, a1, _ = lax.while_loop(alive, step, (jnp.int32(0), z1, za, z1, za, jnp.float32(0.0)))
        o_ref[...] = jnp.where(lo, a0, a1)
        ct_ref[...] = jnp.where(lo, c0, c1)
        cnt_ref[...] = jnp.zeros((8, LANES), F32) + t.astype(F32)

    kv = lambda col: pl.BlockSpec((None, s, LANES), lambda b, p, i: (b, 0, col // LANES + p))
    tile = pl.BlockSpec((None, tq, LANES), lambda b, p, i: (b, i, p))
    hbm = pl.BlockSpec(memory_space=pl.ANY)
    return pl.pallas_call(
        _with_gather(body, 3, 3, na, grid), name=name, grid=grid,
        in_specs=[pl.BlockSpec((None, tq, LANES), lambda b, p, i: (b, i, COL_SBQ // LANES + p)),
                  kv(COL_SBK), kv(COL_SBV)] + [hbm] * na,
        out_specs=[tile, tile, pl.BlockSpec((None, None, None, 8, LANES), lambda b, p, i: (b, p, i, 0, 0))] + [hbm] * na,
        out_shape=[jax.ShapeDtypeStruct((nb, s, 256), F32)] * 2
        + [jax.ShapeDtypeStruct((nb, 2, s // tq, 8, LANES), F32)] + _gather_out_shapes(riding),
        scratch_shapes=_gather_sems(na) if na else [],
        compiler_params=_cp("arbitrary", "arbitrary", "arbitrary"))(proj, proj, proj, *riding)


def _sb_bwd(proj, ct, cnt, do, name):
    nb, s, _ = proj.shape
    tq, tk = min(256, s), min(256, s)
    ratio = tq // tk

    def body(q_ref, k_ref, v_ref, ct_ref, cnt_ref, do_ref, dq_ref, dk_ref, dv_ref):
        i = pl.program_id(2)

        @pl.when(i == 0)
        def _():
            dk_ref[...] = jnp.zeros_like(dk_ref)
            dv_ref[...] = jnp.zeros_like(dv_ref)

        lane = _iota((tq, LANES), 1)
        lo = lane < HEAD
        lok = _iota((tk, LANES), 1) < HEAD
        qv, dov = q_ref[...], do_ref[...]
        qb, dob = qv.astype(MXU), dov.astype(MXU)
        q0 = jnp.where(lo, qv, 0.0).astype(MXU)
        q1 = jnp.where(lo, 0.0, qv).astype(MXU)
        do0 = jnp.where(lo, dov, 0.0).astype(MXU)
        do1 = jnp.where(lo, 0.0, dov).astype(MXU)
        ctv = ct_ref[...]
        ct0 = _rsum(jnp.where(lane == 0, ctv, 0.0))
        ct1 = _rsum(jnp.where(lane == LANES - 1, ctv, 0.0))
        uincl = (_iota((tk, tk), 0) <= _iota((tk, tk), 1)).astype(MXU)
        ustrict = (_iota((tk, tk), 0) < _iota((tk, tk), 1)).astype(MXU)
        tpos = i * tq + _iota((tq, tk), 0)
        scol = _iota((tq, tk), 1)
        nch = (i + 1) * ratio

        def step(j, carry):
            p0, g0, dq0, p1, g1, dq1 = carry
            off = pl.multiple_of(j * tk, tk)
            kc = k_ref[pl.ds(off, tk), :].astype(MXU)
            vc = v_ref[pl.ds(off, tk), :].astype(MXU)
            msk = (scol + j * tk) < tpos

            def head(qm, dom, ctot, pc, gc, dqa):
                z = _dot_nt(qm, kc) * SB_SCALE
                sp = _softplus(z)
                lk = jnp.where(msk, -sp, 0.0)
                lsig = z - sp
                w = jnp.where(msk, jnp.exp(lsig + (ctot - pc - _cumdot(lk, uincl))), 0.0)
                g = w * _dot_nt(dom, vc)
                gpre = gc + _cumdot(g, ustrict)
                sig = jnp.exp(lsig)
                dz = (jnp.where(msk, g * (1.0 - sig) - sig * gpre, 0.0) * SB_SCALE).astype(MXU)
                return (pc + _rsum(lk), gc + _rsum(g), dqa + _dot(dz, kc),
                        _dot_tn(dz, qb), _dot_tn(w.astype(MXU), dob))

            p0, g0, dq0, dk0, dv0 = head(q0, do0, ct0, p0, g0, dq0)
            p1, g1, dq1, dk1, dv1 = head(q1, do1, ct1, p1, g1, dq1)
            dk_ref[pl.ds(off, tk), :] += jnp.where(lok, dk0, dk1)
            dv_ref[pl.ds(off, tk), :] += jnp.where(lok, dv0, dv1)
            return p0, g0, dq0, p1, g1, dq1

        z1 = jnp.zeros((tq, 1), F32)
        za = jnp.zeros((tq, LANES), F32)
        first = nch - jnp.max(cnt_ref[...]).astype(jnp.int32)
        _, _, dq0, _, _, dq1 = lax.fori_loop(first, nch, step, (z1, z1, za, z1, z1, za))
        dq_ref[...] = jnp.where(lo, dq0, dq1)

    kv = lambda col: pl.BlockSpec((None, s, LANES), lambda b, p, i: (b, 0, col // LANES + p))
    tile = pl.BlockSpec((None, tq, LANES), lambda b, p, i: (b, i, p))
    acc = pl.BlockSpec((None, s, LANES), lambda b, p, i: (b, 0, p))
    return pl.pallas_call(
        body, name=name, grid=(nb, 2, s // tq),
        in_specs=[pl.BlockSpec((None, tq, LANES), lambda b, p, i: (b, i, COL_SBQ // LANES + p)),
                  kv(COL_SBK), kv(COL_SBV), tile,
                  pl.BlockSpec((None, None, None, 8, LANES), lambda b, p, i: (b, p, i, 0, 0)), tile],
        out_specs=[tile, acc, acc],
        out_shape=[jax.ShapeDtypeStruct((nb, s, 256), F32)] * 3,
        compiler_params=_cp("parallel", "parallel", "arbitrary"))(proj, proj, proj, ct, cnt, do)


def _mla_fwd(q, k, v, name, riding=()):
    nb, s, _ = q.shape
    tq = tk = min(1024, s)
    na = len(riding)
    grid = (nb, MLA_HEADS // 2, s // tq)

    def body(q_ref, k_ref, v_ref, o_ref, lse_ref):
        i = pl.program_id(2)
        q0, q1 = q_ref[:, :LANES], q_ref[:, LANES:]
        krow = _iota((tk, tq), 0)
        qcol = _iota((tk, tq), 1)

        def step(j, carry, diagonal):
            m0, l0, a0, m1, l1, a1 = carry
            off = pl.multiple_of(j * tk, tk)
            vc = v_ref[pl.ds(off, tk), :]

            def head(qh, kh, m, l, a):
                st = _dot_nt(kh, qh) * MLA_SCALE
                if diagonal:
                    st = jnp.where(krow <= qcol, st, NEG)
                mn = jnp.maximum(m, jnp.max(st, axis=0, keepdims=True))
                al = jnp.exp(m - mn)
                pt = jnp.exp(st - mn)
                return mn, al * l + _csum(pt), al * a + _dot_tn(vc, pt.astype(MXU))

            m0, l0, a0 = head(q0, k_ref[pl.ds(off, tk), :LANES], m0, l0, a0)
            m1, l1, a1 = head(q1, k_ref[pl.ds(off, tk), LANES:], m1, l1, a1)
            return m0, l0, a0, m1, l1, a1

        mi = jnp.full((1, tq), NEG, F32)
        z1 = jnp.zeros((1, tq), F32)
        za = jnp.zeros((LANES, tq), F32)
        carry = lax.fori_loop(0, i, lambda j, cr: step(j, cr, False), (mi, z1, za, mi, z1, za))
        m0, l0, a0, m1, l1, a1 = step(i, carry, True)
        lo_rows = _iota((LANES, tq), 0) < HEAD
        o_ref[...] = jnp.where(lo_rows, a0 / l0, a1 / l1).T
        lse_ref[...] = jnp.zeros_like(lse_ref)
        lse_ref[0:1, :] = m0 + jnp.log(l0)
        lse_ref[1:2, :] = m1 + jnp.log(l1)

    tile = pl.BlockSpec((None, tq, LANES), lambda b, p, i: (b, i, p))
    hbm = pl.BlockSpec(memory_space=pl.ANY)
    return pl.pallas_call(
        _with_gather(body, 3, 2, na, grid), name=name, grid=grid,
        in_specs=[pl.BlockSpec((None, tq, 2 * LANES), lambda b, p, i: (b, i, p)),
                  pl.BlockSpec((None, s, 2 * LANES), lambda b, p, i: (b, 0, p)),
                  pl.BlockSpec((None, s, LANES), lambda b, p, i: (b, 0, p))] + [hbm] * na,
        out_specs=[tile, pl.BlockSpec((None, None, 8, tq), lambda b, p, i: (b, p, 0, i))] + [hbm] * na,
        out_shape=[jax.ShapeDtypeStruct((nb, s, 384), F32), jax.ShapeDtypeStruct((nb, MLA_HEADS // 2, 8, s), F32)]
        + _gather_out_shapes(riding),
        scratch_shapes=_gather_sems(na) if na else [],
        compiler_params=_cp("arbitrary", "arbitrary", "arbitrary"))(q, k, v, *riding)


def _mla_bwd(q, k, v, o, lse, do, name, riding=()):
    nb, s, _ = q.shape
    tq = tk = min(1024, s)
    na = len(riding)
    grid = (nb, MLA_HEADS // 2, s // tq)

    def body(q_ref, k_ref, v_ref, o_ref, lse_ref, do_ref, dq_ref, dk_ref, dv_ref):
        i = pl.program_id(2)

        @pl.when(i == 0)
        def _():
            dk_ref[...] = jnp.zeros_like(dk_ref)
            dv_ref[...] = jnp.zeros_like(dv_ref)

        lo = _iota((tq, LANES), 1) < HEAD
        lok = _iota((tk, LANES), 1) < HEAD
        q0, q1 = q_ref[:, :LANES], q_ref[:, LANES:]
        dov = do_ref[...]
        dob = dov.astype(MXU)
        do0 = jnp.where(lo, dov, 0.0).astype(MXU)
        do1 = jnp.where(lo, 0.0, dov).astype(MXU)
        dd = dov * o_ref[...]
        hi = dd.astype(MXU)
        r1 = dd - hi.astype(F32)
        mid = r1.astype(MXU)
        low = (r1 - mid.astype(F32)).astype(MXU)
        sel_lane = _iota((8, LANES), 1) < HEAD
        sel0 = sel_lane.astype(MXU)
        sel1 = (~sel_lane).astype(MXU)
        dl0 = (_dot_nt(sel0, hi) + _dot_nt(sel0, mid) + _dot_nt(sel0, low))[0:1, :]
        dl1 = (_dot_nt(sel1, hi) + _dot_nt(sel1, mid) + _dot_nt(sel1, low))[0:1, :]
        ls0, ls1 = lse_ref[0:1, :], lse_ref[1:2, :]
        krow = _iota((tk, tq), 0)
        qcol = _iota((tk, tq), 1)

        def step(j, carry, diagonal):
            dq0, dq1 = carry
            off = pl.multiple_of(j * tk, tk)
            vc = v_ref[pl.ds(off, tk), :]

            def head(qh, kh, dom, ls, dl, dqa):
                st = _dot_nt(kh, qh) * MLA_SCALE
                if diagonal:
                    st = jnp.where(krow <= qcol, st, NEG)
                pt = jnp.exp(st - ls)
                dst = (pt * (_dot_nt(vc, dom) - dl) * MLA_SCALE).astype(MXU)
                return dqa + _dot_tn(kh, dst), _dot(dst, qh), _dot(pt.astype(MXU), dob)

            dq0, dk0, dv0 = head(q0, k_ref[pl.ds(off, tk), :LANES], do0, ls0, dl0, dq0)
            dq1, dk1, dv1 = head(q1, k_ref[pl.ds(off, tk), LANES:], do1, ls1, dl1, dq1)
            dk_ref[pl.ds(off, tk), :LANES] += dk0
            dk_ref[pl.ds(off, tk), LANES:] += dk1
            dv_ref[pl.ds(off, tk), :] += jnp.where(lok, dv0, dv1)
            return dq0, dq1

        za = jnp.zeros((LANES, tq), F32)
        carry = lax.fori_loop(0, i, lambda j, cr: step(j, cr, False), (za, za))
        dq0, dq1 = step(i, carry, True)
        dq_ref[:, :LANES] = dq0.T
        dq_ref[:, LANES:] = dq1.T

    tile = pl.BlockSpec((None, tq, LANES), lambda b, p, i: (b, i, p))
    tile2 = pl.BlockSpec((None, tq, 2 * LANES), lambda b, p, i: (b, i, p))
    hbm = pl.BlockSpec(memory_space=pl.ANY)
    return pl.pallas_call(
        _with_chip_exchange(body, 6, 3, na, grid), name=name, grid=grid,
        in_specs=[tile2,
                  pl.BlockSpec((None, s, 2 * LANES), lambda b, p, i: (b, 0, p)),
                  pl.BlockSpec((None, s, LANES), lambda b, p, i: (b, 0, p)),
                  tile, pl.BlockSpec((None, None, 8, tq), lambda b, p, i: (b, p, 0, i)), tile] + [hbm] * na,
        out_specs=[tile2,
                   pl.BlockSpec((None, s, 2 * LANES), lambda b, p, i: (b, 0, p)),
                   pl.BlockSpec((None, s, LANES), lambda b, p, i: (b, 0, p))] + [hbm] * na,
        out_shape=[jax.ShapeDtypeStruct((nb, s, 768), F32), jax.ShapeDtypeStruct((nb, s, 768), F32),
                   jax.ShapeDtypeStruct((nb, s, 384), F32)] + [jax.ShapeDtypeStruct(a.shape, a.dtype) for a in riding],
        scratch_shapes=_chip_exchange_sems(na) if na else [],
        compiler_params=_cp("arbitrary", "arbitrary", "arbitrary"))(q, k, v, o, lse, do, *riding)


def _half_stats(xv, lo):
    x2 = xv * xv
    s0 = _rsum(jnp.where(lo, x2, 0.0))
    s1 = _rsum(jnp.where(lo, 0.0, x2))
    return jnp.where(lo, lax.rsqrt(s0 / HEAD + EPS), lax.rsqrt(s1 / HEAD + EPS))


def _half_mean(xv, lo):
    s0 = _rsum(jnp.where(lo, xv, 0.0))
    s1 = _rsum(jnp.where(lo, 0.0, xv))
    return jnp.where(lo, s0, s1) / HEAD


def _swa_in_specs():
    def band(col, prev):
        if prev:
            return pl.BlockSpec((None, BLOCK, LANES), lambda b, n: (b, jnp.maximum(n - 1, 0), col // LANES))
        return pl.BlockSpec((None, BLOCK, LANES), lambda b, n: (b, n, col // LANES))

    full = lambda shape: pl.BlockSpec(shape, lambda b, n: tuple(0 for _ in shape))
    return [pl.BlockSpec((None, BLOCK, 384), lambda b, n: (b, n, COL_SWQ // 384)),
            band(COL_SWK, False), band(COL_SWK, True), band(COL_SWV, False), band(COL_SWV, True),
            full((1, LANES)), full((1, LANES)), full((8, LANES)), full((SW_HEADS, BLOCK, 2 * BLOCK))]


def _swa_valid(n):
    a = _iota((BLOCK, 2 * BLOCK), 0)
    bcol = _iota((BLOCK, 2 * BLOCK), 1)
    dist = BLOCK + a - bcol
    return (dist >= 0) & (dist < BLOCK) & ((n > 0) | (bcol >= BLOCK))


def _swa_fwd(proj, gq, gk, sinks, bias, name):
    nb, s, _ = proj.shape

    def body(q_ref, kc_ref, kp_ref, vc_ref, vp_ref, gq_ref, gk_ref, sk_ref, bias_ref, o_ref):
        n = pl.program_id(1)
        lo = _iota((BLOCK, LANES), 1) < HEAD
        lo2 = _iota((2 * BLOCK, LANES), 1) < HEAD
        kband = jnp.concatenate([kp_ref[...], kc_ref[...]], axis=0).astype(F32)
        vband = jnp.concatenate([vp_ref[...], vc_ref[...]], axis=0).astype(F32)
        kn = kband * _half_stats(kband, lo2) * gk_ref[...]
        ks = (kn.astype(MXU), pltpu.roll(kn, HEAD, 1).astype(MXU))
        vs = (vband.astype(MXU), pltpu.roll(vband, HEAD, 1).astype(MXU))
        valid = _swa_valid(n)
        for blk in range(SW_HEADS // 2):
            qv = q_ref[:, blk * LANES:(blk + 1) * LANES].astype(F32)
            qn = qv * _half_stats(qv, lo) * gq_ref[...]
            outs = []
            for half in range(2):
                h = 2 * blk + half
                swap = 0 if half == h // 3 else 1
                qm = jnp.where(lo if half == 0 else ~lo, qn, 0.0).astype(MXU)
                sc = jnp.where(valid, _dot_nt(qm, ks[swap]) * SW_SCALE + bias_ref[h], NEG)
                sk = jnp.max(sk_ref[h:h + 1, :], axis=-1, keepdims=True)
                m = jnp.maximum(jnp.max(sc, axis=-1, keepdims=True), sk)
                p = jnp.exp(sc - m)
                l = _rsum(p) + jnp.exp(sk - m)
                outs.append(_dot((p / l).astype(MXU), vs[swap]))
            o_ref[:, blk * LANES:(blk + 1) * LANES] = jnp.where(lo, outs[0], outs[1])

    return pl.pallas_call(
        body, name=name, grid=(nb, s // BLOCK), in_specs=_swa_in_specs(),
        out_specs=pl.BlockSpec((None, BLOCK, 384), lambda b, n: (b, n, 0)),
        out_shape=jax.ShapeDtypeStruct((nb, s, 384), F32),
        compiler_params=_cp("parallel", "parallel"))(proj, proj, proj, proj, proj, gq, gk, sinks, bias)


def _swa_bwd(proj, gq, gk, sinks, bias, do, name):
    nb, s, _ = proj.shape

    def body(q_ref, kc_ref, kp_ref, vc_ref, vp_ref, gq_ref, gk_ref, sk_ref, bias_ref, do_ref,
             dq_ref, dkc_ref, dkp_ref, dvc_ref, dvp_ref, dbias_ref, dsk_ref, dgq_ref, dgk_ref):
        n = pl.program_id(1)

        @pl.when((pl.program_id(0) == 0) & (n == 0))
        def _():
            for r in (dbias_ref, dsk_ref, dgq_ref, dgk_ref):
                r[...] = jnp.zeros_like(r)

        lo = _iota((BLOCK, LANES), 1) < HEAD
        lo2 = _iota((2 * BLOCK, LANES), 1) < HEAD
        kband = jnp.concatenate([kp_ref[...], kc_ref[...]], axis=0).astype(F32)
        vband = jnp.concatenate([vp_ref[...], vc_ref[...]], axis=0).astype(F32)
        rk = _half_stats(kband, lo2)
        khat = kband * rk
        gkv = gk_ref[...]
        kn = khat * gkv
        ks = (kn.astype(MXU), pltpu.roll(kn, HEAD, 1).astype(MXU))
        vs = (vband.astype(MXU), pltpu.roll(vband, HEAD, 1).astype(MXU))
        valid = _swa_valid(n)
        dkn = jnp.zeros((2 * BLOCK, LANES), F32)
        dvb = jnp.zeros((2 * BLOCK, LANES), F32)
        gqv = gq_ref[...]
        dgq = jnp.zeros((1, LANES), F32)
        for blk in range(SW_HEADS // 2):
            bs = slice(blk * LANES, (blk + 1) * LANES)
            qv = q_ref[:, bs].astype(F32)
            rq = _half_stats(qv, lo)
            qhat = qv * rq
            qn = qhat * gqv
            dov = do_ref[:, bs]
            dqn = jnp.zeros((BLOCK, LANES), F32)
            for half in range(2):
                h = 2 * blk + half
                swap = 0 if half == h // 3 else 1
                hm = lo if half == 0 else ~lo
                qm = jnp.where(hm, qn, 0.0).astype(MXU)
                dom = jnp.where(hm, dov, 0.0).astype(MXU)
                sc = jnp.where(valid, _dot_nt(qm, ks[swap]) * SW_SCALE + bias_ref[h], NEG)
                sk = jnp.max(sk_ref[h:h + 1, :], axis=-1, keepdims=True)
                m = jnp.maximum(jnp.max(sc, axis=-1, keepdims=True), sk)
                e = jnp.exp(sc - m)
                es = jnp.exp(sk - m)
                l = _rsum(e) + es
                p = e / l
                dp = _dot_nt(dom, vs[swap])
                delta = _rsum(p * dp)
                ds = p * (dp - delta)
                dsk_ref[h:h + 1, :] += jnp.broadcast_to(_csum(-(es / l) * delta), (1, LANES))
                dbias_ref[h] += ds
                dsb = (ds * SW_SCALE).astype(MXU)
                dqn = dqn + jnp.where(hm, _dot(dsb, ks[swap]), 0.0)
                rk_ = _dot_tn(dsb, qm)
                rv_ = _dot_tn(p.astype(MXU), dom)
                if swap:
                    rk_ = pltpu.roll(rk_, HEAD, 1)
                    rv_ = pltpu.roll(rv_, HEAD, 1)
                dkn = dkn + rk_
                dvb = dvb + rv_
            dgq = dgq + _csum(dqn * qhat)
            dyq = dqn * gqv
            dq_ref[:, bs] = rq * (dyq - qhat * _half_mean(dyq * qhat, lo))
        dgq_ref[...] += dgq
        dgk_ref[...] += _csum(dkn * khat)
        dyk = dkn * gkv
        dkb = rk * (dyk - khat * _half_mean(dyk * khat, lo2))
        dkp_ref[...] = dkb[:BLOCK]
        dkc_ref[...] = dkb[BLOCK:]
        dvp_ref[...] = dvb[:BLOCK]
        dvc_ref[...] = dvb[BLOCK:]

    full = lambda shape: pl.BlockSpec(shape, lambda b, n: tuple(0 for _ in shape))
    tile = pl.BlockSpec((None, BLOCK, LANES), lambda b, n: (b, n, 0))
    tile3 = pl.BlockSpec((None, BLOCK, 384), lambda b, n: (b, n, 0))
    kvs = jax.ShapeDtypeStruct((nb, s, LANES), F32)
    return pl.pallas_call(
        body, name=name, grid=(nb, s // BLOCK), in_specs=_swa_in_specs() + [tile3],
        out_specs=[tile3, tile, tile, tile, tile, full((SW_HEADS, BLOCK, 2 * BLOCK)), full((8, LANES)),
                   full((1, LANES)), full((1, LANES))],
        out_shape=[jax.ShapeDtypeStruct((nb, s, 384), F32), kvs, kvs, kvs, kvs,
                   jax.ShapeDtypeStruct((SW_HEADS, BLOCK, 2 * BLOCK), F32), jax.ShapeDtypeStruct((8, LANES), F32),
                   jax.ShapeDtypeStruct((1, LANES), F32), jax.ShapeDtypeStruct((1, LANES), F32)],
        compiler_params=_cp("arbitrary", "arbitrary"))(proj, proj, proj, proj, proj, gq, gk, sinks, bias, do)


def _bias_build(table, bucket, name):
    def body(tb_ref, bk_ref, o_ref):
        bk = bk_ref[...]
        tb = tb_ref[...]
        row = _iota((8, LANES), 0)
        col = _iota((8, LANES), 1)
        for h in range(SW_HEADS):
            acc = jnp.zeros((BLOCK, 2 * BLOCK), F32)
            for t in range(REL_BUCKETS):
                val = jnp.sum(jnp.where((row == h) & (col == t), tb, 0.0), keepdims=True)
                acc = jnp.where(bk == t, val, acc)
            o_ref[h] = acc

    return pl.pallas_call(
        body, name=name, out_shape=jax.ShapeDtypeStruct((SW_HEADS, BLOCK, 2 * BLOCK), F32))(table, bucket)


def _bias_grad(dbias, bucket, name):
    def body(db_ref, bk_ref, o_ref):
        bk = bk_ref[...]
        row = _iota((8, LANES), 0)
        col = _iota((8, LANES), 1)
        res = jnp.zeros((8, LANES), F32)
        for h in range(SW_HEADS):
            dbh = db_ref[h]
            for t in range(REL_BUCKETS):
                val = jnp.sum(jnp.where(bk == t, dbh, 0.0), keepdims=True)
                res = jnp.where((row == h) & (col == t), val, res)
        o_ref[...] = res

    return pl.pallas_call(body, name=name, out_shape=jax.ShapeDtypeStruct((8, LANES), F32))(dbias, bucket)


def _loss_grad(y, target, name):
    nb, s, d = y.shape
    tm = min(512, s)

    def body(y_ref, t_ref, loss_ref, dy_ref):
        @pl.when((pl.program_id(0) == 0) & (pl.program_id(1) == 0))
        def _():
            loss_ref[...] = jnp.zeros_like(loss_ref)

        e = y_ref[...] - t_ref[...]
        dy_ref[...] = e / d
        loss_ref[...] += 0.5 * jnp.sum(_rsum(e * e) / d, keepdims=True)

    tile = pl.BlockSpec((None, tm, d), lambda b, i: (b, i, 0))
    return pl.pallas_call(
        body, name=name, grid=(nb, s // tm), in_specs=[tile, tile],
        out_specs=[pl.BlockSpec((8, LANES), lambda b, i: (0, 0)), tile],
        out_shape=[jax.ShapeDtypeStruct((8, LANES), F32), jax.ShapeDtypeStruct((nb, s, d), F32)],
        compiler_params=_cp("arbitrary", "arbitrary"))(y, target)


def _adamw(parts, w, m, v, name):
    npart, r, ncol = parts.shape
    tr = _row_tile(r, ncol)
    bc1 = 1.0 - ADAM_B1 ** ADAM_STEP
    bc2 = 1.0 - ADAM_B2 ** ADAM_STEP

    def body(p_ref, w_ref, m_ref, v_ref, g_ref, d_ref, nm_ref, nv_ref):
        g = p_ref[0].astype(F32)
        for k in range(1, npart):
            g = g + p_ref[k].astype(F32)
        mn = ADAM_B1 * m_ref[...] + (1.0 - ADAM_B1) * g
        vn = ADAM_B2 * v_ref[...] + (1.0 - ADAM_B2) * (g * g)
        g_ref[...] = g
        nm_ref[...] = mn
        nv_ref[...] = vn
        d_ref[...] = -ADAM_LR * ((mn / bc1) / (jnp.sqrt(vn / bc2) + ADAM_EPS) + ADAM_WD * w_ref[...])

    tile = pl.BlockSpec((tr, ncol), lambda i: (i, 0))
    return pl.pallas_call(
        body, name=name, grid=(r // tr,),
        in_specs=[pl.BlockSpec((npart, tr, ncol), lambda i: (0, i, 0)), tile, tile, tile],
        out_specs=[tile] * 4, out_shape=[jax.ShapeDtypeStruct((r, ncol), F32)] * 4,
        compiler_params=_cp("parallel"))(parts, w, m, v)


def _unpack(flat, shapes, lead=()):
    out, off = [], 0
    for shp in shapes:
        size = 1
        for dim in shp:
            size *= dim
        out.append(flat[..., off:off + size].reshape(lead + tuple(shp)))
        off += size
    return out


def _t5_bucket():
    a = jnp.arange(BLOCK)[:, None]
    b = jnp.arange(2 * BLOCK)[None, :]
    dist = BLOCK + a - b
    max_exact = REL_BUCKETS // 2
    nn = jnp.maximum(dist, 0)
    nf = jnp.maximum(nn, 1).astype(F32)
    large = max_exact + (jnp.log(nf / max_exact) / math.log(BLOCK / max_exact)
                         * (REL_BUCKETS - max_exact)).astype(jnp.int32)
    large = jnp.minimum(large, REL_BUCKETS - 1)
    return jnp.where(nn < max_exact, nn, large).astype(jnp.int32)


def _pad_lanes(g, n):
    return jnp.pad(g, (0, n - g.shape[0])).reshape(1, n)


def kernel(x, c, positions, rel_table, norm1_g, norm2_g, w_ada, b_ada, w_in, mla_cq_g, w_uq, mla_ckv_g, w_ukv, mla_qn_g, mla_kn_g, sw_qn_g, sw_kn_g, sw_sinks, w_out, w_up, conv_w, conv_b, w_down, loss_target, m_rel_table, m_norm1_g, m_norm2_g, m_w_ada, m_b_ada, m_w_in, m_mla_cq_g, m_w_uq, m_mla_ckv_g, m_w_ukv, m_mla_qn_g, m_mla_kn_g, m_sw_qn_g, m_sw_kn_g, m_sw_sinks, m_w_out, m_w_up, m_conv_w, m_conv_b, m_w_down, v_rel_table, v_norm1_g, v_norm2_g, v_w_ada, v_b_ada, v_w_in, v_mla_cq_g, v_w_uq, v_mla_ckv_g, v_w_ukv, v_mla_qn_g, v_mla_kn_g, v_sw_qn_g, v_sw_kn_g, v_sw_sinks, v_w_out, v_w_up, v_conv_w, v_conv_b, v_w_down):
    nb, s, d = x.shape
    nl = norm1_g.shape[0]
    me = 4 * lax.axis_index("x") + 2 * lax.axis_index("y") + lax.axis_index("c")
    n_ada = w_ada.shape[2]

    shard = lambda w, l, transposed: (jnp.swapaxes(w[l], 0, 1) if transposed else w[l]).astype(MXU)
    attn_local = lambda l: [shard(w_in, l, True), shard(w_uq, l, True), shard(w_ukv, l, True), shard(w_out, l, False)]
    ffn_local = lambda l: [shard(w_up, l, True), shard(w_down, l, False)]
    full = lambda a: a.reshape(-1, a.shape[-1])
    zrows = lambda n: jnp.zeros((n, d), MXU)
    pad_in = lambda wt: jnp.concatenate([wt[:1152], wt[1184:1824], zrows(64), wt[1152:1184], zrows(160)], axis=0)
    pad_uq = lambda wt: jnp.pad(wt.reshape(MLA_HEADS, MLA_QK, 256), ((0, 0), (0, LANES - MLA_QK), (0, 0))).reshape(768, 256)
    got = _all_gather(attn_local(0) + [conv_w.reshape(-1, conv_w.shape[-1]), c], "gather_inputs")
    w_in_pt, w_uq_pt, w_ukv_t, w_out_f = [pad_in(full(got[0]))], [pad_uq(full(got[1]))], [full(got[2])], [full(got[3])]
    w_up_t, w_down_f = [], []
    conv_full = got[4].reshape(N_DEV, nl, 3, -1).transpose(1, 2, 0, 3).reshape(nl, 3, -1)
    c_all = got[5].reshape(N_DEV * nb, d)

    b_my = lax.dynamic_slice_in_dim(b_ada, me * n_ada, n_ada, axis=1).reshape(nl, 1, n_ada)
    mods_my = _ada_fwd(c_all, w_ada, b_my, "ada_fwd")
    mods, = _all_gather([mods_my.reshape(nl * N_DEV * nb, n_ada)], "gather_mods")
    mods = mods.reshape(N_DEV, nl, N_DEV * nb, n_ada).transpose(1, 2, 0, 3).reshape(nl, N_DEV * nb, N_DEV * n_ada)
    mods = lax.dynamic_slice_in_dim(mods, me * nb, nb, axis=1)
    shift1, scale1, gate1, shift2, scale2, gate2 = [mods[:, :, k * d:(k + 1) * d].reshape(nl, nb, 1, d) for k in range(6)]

    half = 16
    inv_freq = jnp.power(ROPE_THETA, -jnp.arange(half, dtype=F32) / half)
    ang = positions.astype(F32)[..., None] * inv_freq
    ones = lambda n: jnp.ones((nb, s, n), F32)
    zeros = lambda n: jnp.zeros((nb, s, n), F32)
    rope_c = jnp.concatenate([ones(64), jnp.cos(ang), jnp.cos(ang), ones(32)], axis=-1)
    rope_s = jnp.concatenate([zeros(64), jnp.sin(ang), jnp.sin(ang), zeros(32)], axis=-1)
    bucket = _t5_bucket()
    bias = _bias_build(jnp.pad(rel_table.T, ((0, 8 - SW_HEADS), (0, LANES - REL_BUCKETS))), bucket, "rel_bias")

    row = lambda g: g.reshape(1, -1)
    twice = lambda g: jnp.concatenate([g, g]).reshape(1, LANES)

    saved = []
    xl = x
    for l in range(nl):
        proj, h1 = _ln_mod_matmul(xl, row(norm1_g[l]), scale1[l], shift1[l], w_in_pt[l], f"l{l}_in_proj")
        prep_args = (proj, rope_c, rope_s, row(mla_cq_g[l]), row(mla_ckv_g[l]), _pad_lanes(mla_qn_g[l], LANES),
                     _pad_lanes(mla_kn_g[l], LANES), w_uq_pt[l], w_ukv_t[l])
        qm, km, vm = _mla_prep(*prep_args, f"l{l}_mla_prep")
        o_a, ct_a, cnt_a, up_g, down_g = _sb_fwd(proj, f"l{l}_sb_fwd", riding=ffn_local(l))
        w_up_t.append(full(up_g))
        w_down_f.append(full(down_g))
        o_b, lse_b, *nxt = _mla_fwd(qm, km, vm, f"l{l}_mla_fwd", riding=attn_local(l + 1) if l + 1 < nl else ())
        if nxt:
            w_in_pt.append(pad_in(full(nxt[0])))
            w_uq_pt.append(pad_uq(full(nxt[1])))
            w_ukv_t.append(full(nxt[2]))
            w_out_f.append(full(nxt[3]))
        sinks = jnp.broadcast_to(jnp.pad(sw_sinks[l], (0, 2))[:, None], (8, LANES))
        swa_args = (proj, twice(sw_qn_g[l]), twice(sw_kn_g[l]), sinks, bias)
        o_c = _swa_fwd(*swa_args, f"l{l}_swa_fwd")
        wo = [w_out_f[l][:256], w_out_f[l][256:640], w_out_f[l][640:]]
        x_mid, y1 = _out_proj([o_a, o_b, o_c], wo, gate1[l], xl, f"l{l}_out_proj")
        u_pre, h2 = _ln_mod_matmul(x_mid, row(norm2_g[l]), scale2[l], shift2[l], w_up_t[l], f"l{l}_up_proj")
        x_out, y2 = _conv_gate_matmul(u_pre, conv_full[l], row(conv_b[l]), w_down_f[l], gate2[l], x_mid, f"l{l}_ffn_down")
        saved.append(dict(x=xl, proj=proj, h1=h1, prep=prep_args, qkv=(qm, km, vm), o_a=o_a, ct_a=ct_a, cnt_a=cnt_a, o_b=o_b, lse_b=lse_b,
                          swa=swa_args, o_c=o_c, wo=wo, y1=y1, x_mid=x_mid, u_pre=u_pre, h2=h2, y2=y2))
        xl = x_out

    loss_blk, dx = _loss_grad(xl, loss_target, "loss")
    loss = lax.psum(loss_blk[0, 0], ("x", "y", "c"))

    t = nb * s
    flat = lambda a: a.reshape(t, a.shape[-1])
    grads = [None] * nl
    dmods = [None] * nl
    sharded_out = [None] * nl
    sharded_names = ["w_in", "w_uq", "w_ukv", "w_up", "w_out", "w_down", "conv_w"]
    sharded_wmv = dict(w_in=(w_in, m_w_in, v_w_in), w_uq=(w_uq, m_w_uq, v_w_uq), w_ukv=(w_ukv, m_w_ukv, v_w_ukv),
                       w_up=(w_up, m_w_up, v_w_up), w_out=(w_out, m_w_out, v_w_out), w_down=(w_down, m_w_down, v_w_down),
                       conv_w=(conv_w, m_conv_w, v_conv_w))
    n_in, n_up, n_out, n_dn = w_in.shape[2], w_up.shape[2], w_out.shape[1], w_down.shape[1]
    small_sizes = [w_uq[0].size, w_ukv[0].size, conv_w[0].size]
    n_small_rows = -(-sum(small_sizes) // d)
    rows_used = n_in + n_out + n_small_rows
    rows_grad = -(-rows_used // 16) * 16

    def pack_rows(mats, vecs):
        lead = mats[0].shape[:-2]
        flat_part = jnp.concatenate(vecs, axis=-1)
        flat_part = jnp.pad(flat_part, [(0, 0)] * len(lead) + [(0, n_small_rows * d - flat_part.shape[-1])])
        tail = jnp.zeros(lead + (rows_grad - rows_used, d), F32)
        return jnp.concatenate(list(mats) + [flat_part.reshape(lead + (n_small_rows, d)), tail], axis=-2)

    def unpack_rows(a):
        o1, o2 = n_in, n_in + n_out
        flat_part = a[o2:o2 + n_small_rows].reshape(-1)
        s1, s2, s3 = small_sizes[0], small_sizes[0] + small_sizes[1], sum(small_sizes)
        return dict(w_in=a[:o1].T, w_out=a[o1:o2],
                    w_uq=flat_part[:s1].reshape(w_uq.shape[2], -1).T, w_ukv=flat_part[s1:s2].reshape(w_ukv.shape[2], -1).T,
                    conv_w=flat_part[s2:s3].reshape(conv_w.shape[1:]))

    ffn_out = [None] * nl
    core = lax.axis_index("c").reshape(1).astype(jnp.int32)

    def update_ffn(l, recv):
        wmv = [{k: v[o][l] for k, v in sharded_wmv.items()} for o in range(3)]
        res_up = _adamw(recv[0], *[a["w_up"].T for a in wmv], f"l{l}_adamw_up")
        res_dn = _adamw(recv[1], *[a["w_down"] for a in wmv], f"l{l}_adamw_down")
        ffn_out[l] = [dict(w_up=ru.T, w_down=rd) for ru, rd in zip(res_up, res_dn)]

    def update_rest(l, recv):
        wmv = [{k: v[o][l] for k, v in sharded_wmv.items()} for o in range(3)]
        res_rest = _adamw(recv, *[pack_rows([a["w_in"].T, a["w_out"]], [a["w_uq"].T.reshape(-1), a["w_ukv"].T.reshape(-1),
                                                                         a["conv_w"].reshape(-1)]) for a in wmv],
                          f"l{l}_adamw_rest")
        sharded_out[l] = [dict(unpack_rows(rr), **ff) for rr, ff in zip(res_rest, ffn_out[l])]

    pending = None
    dbias = jnp.zeros((SW_HEADS, BLOCK, 2 * BLOCK), F32)
    for l in reversed(range(nl)):
        sv = saved[l]
        (da,), dy2, dgate2 = _gate_bwd_nt(dx, sv["y2"], gate2[l], [w_down_f[l]], f"l{l}_ffn_down_bwd")
        du, a_act, cstats = _conv_gate_bwd(da, sv["u_pre"], conv_full[l], row(conv_b[l]), f"l{l}_conv_gate_bwd")
        res = _ln_mod_matmul_bwd(du, w_up_t[l], sv["x_mid"], row(norm2_g[l]), scale2[l], dx, conv_full[l],
                                 f"l{l}_up_proj_bwd", riding=[pending[1]] if pending else ())
        dx_mid, du_pre, dshift2, dscale2, dg2 = res[:5]
        if pending:
            update_rest(pending[0], res[5])
            pending = None
        g_w_down = _wgrad(flat(a_act), flat(dy2), f"l{l}_w_down_grad")
        g_w_up_t = _wgrad(flat(du_pre), flat(sv["h2"]), f"l{l}_w_up_grad")
        per_dev = lambda g: g.reshape(N_DEV, -1, d)
        ffn_send = [per_dev(g_w_up_t), per_dev(g_w_down)]
        ffn_sib = _pair_exchange(ffn_send, f"l{l}_pair_exchange_ffn")
        ffn_pair = [_pair_add(core, a, b, f"l{l}_pair_add_{k}") for a, b, k in zip(ffn_send, ffn_sib, ("up", "down"))]

        (do_a, do_b, do_c), dy1, dgate1 = _gate_bwd_nt(dx_mid, sv["y1"], gate1[l], sv["wo"], f"l{l}_out_proj_bwd")
        mix = jnp.concatenate([sv["o_a"], sv["o_b"], sv["o_c"]], axis=-1).astype(MXU)
        g_w_out = _wgrad(flat(mix), flat(dy1), f"l{l}_w_out_grad")

        dsb_q, dsb_k, dsb_v = _sb_bwd(sv["proj"], sv["ct_a"], sv["cnt_a"], do_a, f"l{l}_sb_bwd")
        qm, km, vm = sv["qkv"]
        dqm, dkm, dvm, *ffn_recv = _mla_bwd(qm, km, vm, sv["o_b"], sv["lse_b"], do_b, f"l{l}_mla_bwd", riding=ffn_pair)
        update_ffn(l, ffn_recv)
        dsw_q, dkc, dkp, dvc, dvp, dbias_l, dsinks, dg_swq, dg_swk = _swa_bwd(*sv["swa"], do_c, f"l{l}_swa_bwd")
        dbias = dbias + dbias_l
        shift_up = lambda a: jnp.concatenate([a[:, BLOCK:], jnp.zeros((nb, BLOCK, LANES), F32)], axis=1)
        dsw_k = dkc + shift_up(dkp)
        dsw_v = dvc + shift_up(dvp)
        dcq, dckv, dkr, g_w_uq_pt, g_w_ukv_t, dg_cq, dg_ckv, dg_qn, dg_kn = _mla_prep_bwd(
            *sv["prep"], dqm, dkm, dvm, f"l{l}_mla_prep_bwd")
        dproj = jnp.concatenate([dsb_q, dsb_k, dsb_v, dcq, dckv, dsw_q, dsw_k, dsw_v, dkr, zeros(128)], axis=-1).astype(MXU)
        dx, dproj_m, dshift1, dscale1, dg1 = _ln_mod_matmul_bwd(
            dproj, w_in_pt[l], sv["x"], row(norm1_g[l]), scale1[l], dx_mid, None, f"l{l}_in_proj_bwd")
        g_w_in_pt = _wgrad(flat(dproj_m), flat(sv["h1"]), f"l{l}_w_in_grad")

        g_w_in_t = jnp.concatenate([g_w_in_pt[:1152], g_w_in_pt[1856:1888], g_w_in_pt[1152:1792]], axis=0)
        g_w_uq_t = g_w_uq_pt.reshape(MLA_HEADS, LANES, 256)[:, :MLA_QK].reshape(MLA_HEADS * MLA_QK, 256)
        dmods[l] = jnp.concatenate([dshift1, dscale1, dgate1, dshift2, dscale2, dgate2], axis=-1).reshape(nb, 6 * d)

        conv_dev = cstats[1:4].reshape(3, N_DEV, -1).transpose(1, 0, 2)
        rest = pack_rows([per_dev(g_w_in_t), per_dev(g_w_out)],
                         [g_w_uq_t.reshape(N_DEV, -1), g_w_ukv_t.reshape(N_DEV, -1), conv_dev.reshape(N_DEV, -1)])
        rest_sib, = _pair_exchange([rest], f"l{l}_pair_exchange_rest")
        rest_pair = _pair_add(core, rest, rest_sib, f"l{l}_pair_add_rest")
        if l > 0:
            pending = (l, rest_pair)
        else:
            update_rest(l, _chip_exchange([rest_pair], f"l{l}_chip_exchange")[0])
        grads[l] = dict(
            norm1_g=dg1[0], norm2_g=dg2[0], mla_cq_g=dg_cq[0], mla_ckv_g=dg_ckv[0], mla_qn_g=dg_qn[0, :MLA_QK],
            mla_kn_g=dg_kn[0, :MLA_QK], sw_qn_g=dg_swq[0, :HEAD] + dg_swq[0, HEAD:], sw_kn_g=dg_swk[0, :HEAD] + dg_swk[0, HEAD:],
            sw_sinks=dsinks[:SW_HEADS, 0], conv_b=cstats[0])
    grad_x = dx
    g_rel = _bias_grad(dbias, bucket, "rel_table_grad")[:SW_HEADS, :REL_BUCKETS].T
    stack = lambda k: jnp.stack([grads[l][k] for l in range(nl)])

    dm_all, = _all_gather([jnp.stack(dmods).reshape(nl * nb, 6 * d)], "gather_dmods")
    dm_all = dm_all.reshape(N_DEV, nl, nb, 6 * d).transpose(1, 0, 2, 3).reshape(nl, N_DEV * nb, 6 * d)
    dm_my = lax.dynamic_slice_in_dim(dm_all, me * n_ada, n_ada, axis=2)
    g_w_ada, g_b_ada = _ada_bwd(c_all, dm_my, dm_all, "ada_bwd")
    g_b_ada = g_b_ada.reshape(nl, 6 * d)

    big_out = [{k: jnp.stack([sharded_out[l][o][k] for l in range(nl)]) for k in sharded_names} for o in range(4)]
    packf = lambda dct, names, rows: jnp.pad(jnp.concatenate([dct[k].reshape(-1) for k in names]),
                                             (0, rows * LANES - sum(dct[k].size for k in names))).reshape(rows, LANES)

    small_names = ["rel_table", "norm1_g", "norm2_g", "mla_cq_g", "mla_ckv_g", "mla_qn_g", "mla_kn_g",
                   "sw_qn_g", "sw_kn_g", "sw_sinks", "conv_b"]
    small_w = dict(rel_table=rel_table, norm1_g=norm1_g, norm2_g=norm2_g, mla_cq_g=mla_cq_g, mla_ckv_g=mla_ckv_g,
                   mla_qn_g=mla_qn_g, mla_kn_g=mla_kn_g, sw_qn_g=sw_qn_g, sw_kn_g=sw_kn_g, sw_sinks=sw_sinks, conv_b=conv_b)
    small_m = dict(rel_table=m_rel_table, norm1_g=m_norm1_g, norm2_g=m_norm2_g, mla_cq_g=m_mla_cq_g, mla_ckv_g=m_mla_ckv_g,
                   mla_qn_g=m_mla_qn_g, mla_kn_g=m_mla_kn_g, sw_qn_g=m_sw_qn_g, sw_kn_g=m_sw_kn_g, sw_sinks=m_sw_sinks, conv_b=m_conv_b)
    small_v = dict(rel_table=v_rel_table, norm1_g=v_norm1_g, norm2_g=v_norm2_g, mla_cq_g=v_mla_cq_g, mla_ckv_g=v_mla_ckv_g,
                   mla_qn_g=v_mla_qn_g, mla_kn_g=v_mla_kn_g, sw_qn_g=v_sw_qn_g, sw_kn_g=v_sw_kn_g, sw_sinks=v_sw_sinks, conv_b=v_conv_b)
    small_g = {k: (g_rel if k == "rel_table" else stack(k)) for k in small_names}
    n_small = sum(small_w[k].size for k in small_names)
    rows_small = -(-n_small // (8 * LANES)) * 8
    small_parts, = _all_gather([packf(small_g, small_names, rows_small)], "gather_small_grads")
    small_out = _adamw(small_parts, packf(small_w, small_names, rows_small), packf(small_m, small_names, rows_small),
                       packf(small_v, small_names, rows_small), "adamw_replicated")
    small_out = [dict(zip(small_names, _unpack(o.reshape(-1), [small_w[k].shape for k in small_names]))) for o in small_out]

    two_d = lambda a: a.reshape(-1, a.shape[-1])
    res_w = _adamw(two_d(g_w_ada)[None], two_d(w_ada), two_d(m_w_ada), two_d(v_w_ada), "adamw_w_ada")
    res_b = _adamw(g_b_ada[None], b_ada, m_b_ada, v_b_ada, "adamw_b_ada")
    ada_out = [dict(w_ada=rw.reshape(w_ada.shape), b_ada=rb) for rw, rb in zip(res_w, res_b)]

    order = ["rel_table", "norm1_g", "norm2_g", "w_ada", "b_ada", "w_in", "mla_cq_g", "w_uq", "mla_ckv_g", "w_ukv",
             "mla_qn_g", "mla_kn_g", "sw_qn_g", "sw_kn_g", "sw_sinks", "w_out", "w_up", "conv_w", "conv_b", "w_down"]
    outs = [{**big_out[k], **small_out[k], **ada_out[k]} for k in range(4)]
    return (loss, grad_x, *[outs[0][n] for n in order], *[outs[1][n] for n in order],
            *[outs[2][n] for n in order], *[outs[3][n] for n in order])
```

```python
import math

import jax
import jax.numpy as jnp
from jax import lax
from jax.experimental import pallas as pl
from jax.experimental.pallas import tpu as pltpu

F32 = jnp.float32
MXU = jnp.bfloat16
EPS = 1e-6
NEG = -1e30
VMEM_LIMIT_BYTES = 56 * 1024 * 1024
N_DEV = 8
MESH = pl.DeviceIdType.MESH

D_MODEL = 1024
D_FF = 2816
HEAD = 64
LANES = 128
MLA_HEADS = 6
MLA_QK = 96
SW_HEADS = 6
REL_BUCKETS = 32
BLOCK = 128
SB_SCALE = HEAD ** -0.5
SB_DEAD = -105.0
SW_SCALE = HEAD ** -0.5
MLA_SCALE = MLA_QK ** -0.5
ROPE_THETA = 10000.0
D_IN_PAD = 2048
COL_SBQ, COL_SBK, COL_SBV, COL_CQ, COL_CKV, COL_SWQ, COL_SWK, COL_SWV, COL_KR = 0, 256, 512, 768, 1024, 1152, 1536, 1664, 1792

HALO = 16
ROW_TILE_BYTES = 1 << 20
ADAM_LR, ADAM_B1, ADAM_B2, ADAM_EPS, ADAM_WD, ADAM_STEP = 0.001, 0.9, 0.999, 1e-08, 0.01, 10


def _cp(*sem):
    return pltpu.CompilerParams(dimension_semantics=sem, vmem_limit_bytes=VMEM_LIMIT_BYTES)


def _iota(shape, dim):
    return lax.broadcasted_iota(jnp.int32, shape, dim)


def _dot(a, b):
    return jnp.dot(a, b, preferred_element_type=F32)


def _dot_nt(a, b):
    return lax.dot_general(a, b, (((1,), (1,)), ((), ())), preferred_element_type=F32)


def _dot_tn(a, b):
    return lax.dot_general(a, b, (((0,), (0,)), ((), ())), preferred_element_type=F32)


def _cumdot(x, u):
    hi = x.astype(MXU)
    mid = (x - hi.astype(F32)).astype(MXU)
    return _dot(hi, u) + _dot(mid, u)


def _sigmoid(x):
    return 1.0 / (1.0 + jnp.exp(-x))


def _rsum(x):
    return jnp.sum(x, axis=-1, keepdims=True)


def _csum(x):
    return jnp.sum(x, axis=0, keepdims=True)


def _all_gather(xs, name):
    na = len(xs)

    def body(*refs):
        start, finish = _gather_steps(refs[:na], refs[na:2 * na], *refs[2 * na:])
        start()
        finish()

    hbm = pl.BlockSpec(memory_space=pl.ANY)
    return pl.pallas_call(
        body, name=name, out_shape=_gather_out_shapes(xs), in_specs=[hbm] * na, out_specs=[hbm] * na,
        scratch_shapes=_gather_sems(na))(*xs)


def _gather_out_shapes(xs):
    return [jax.ShapeDtypeStruct((N_DEV,) + a.shape, a.dtype) for a in xs]


def _gather_sems(na):
    return [pltpu.SemaphoreType.DMA((7 * na,)), pltpu.SemaphoreType.DMA((7 * na,)), pltpu.SemaphoreType.DMA((na,))]


def _gather_steps(x_refs, out_refs, send_sems, recv_sems, local_sems):
    na = len(x_refs)
    x, y, c = lax.axis_index("x"), lax.axis_index("y"), lax.axis_index("c")
    me, sibling = (x, y, c), (x, y, 1 - c)
    chips = [(1 - x, y), (x, 1 - y), (1 - x, 1 - y)]

    def slot(a, px, py, pc):
        return out_refs[a].at[4 * px + 2 * py + pc]

    def copy(a, k, block, to, src=None):
        return pltpu.make_async_remote_copy(
            src_ref=slot(a, *block) if src is None else src, dst_ref=slot(a, *block),
            send_sem=send_sems.at[7 * a + k], recv_sem=recv_sems.at[7 * a + k], device_id=to, device_id_type=MESH)

    def own_copies(a):
        return ([copy(a, 0, me, sibling, src=x_refs[a])]
                + [copy(a, 1 + j, me, (*chip, c), src=x_refs[a]) for j, chip in enumerate(chips)])

    def local_copy(a):
        return pltpu.make_async_copy(x_refs[a], slot(a, *me), local_sems.at[a])

    def start():
        for a in range(na):
            local_copy(a).start()
            for cp in own_copies(a):
                cp.start()

    def finish():
        passed = []
        for j, chip in enumerate(chips):
            for a in range(na):
                copy(a, 1 + j, (*chip, c), me).wait_recv()
                passed.append(copy(a, 4 + j, (*chip, c), sibling))
                passed[-1].start()
        for a in range(na):
            copy(a, 0, sibling, me).wait_recv()
            for j, chip in enumerate(chips):
                copy(a, 4 + j, (*chip, 1 - c), me).wait_recv()
        for a in range(na):
            for cp in own_copies(a):
                cp.wait_send()
        for cp in passed:
            cp.wait_send()
        for a in range(na):
            local_copy(a).wait()

    return start, finish


def _with_gather(body, n_in, n_out, na, grid):
    if not na:
        return body

    def wrapped(*refs):
        ins, ride_in = refs[:n_in], refs[n_in:n_in + na]
        outs = refs[n_in + na:n_in + na + n_out]
        ride_out = refs[n_in + na + n_out:n_in + 2 * na + n_out]
        ids = [pl.program_id(k) for k in range(len(grid))]
        first, last = ids[0] == 0, ids[0] == grid[0] - 1
        for k in range(1, len(grid)):
            first, last = first & (ids[k] == 0), last & (ids[k] == grid[k] - 1)
        start, finish = _gather_steps(ride_in, ride_out, *refs[n_in + 2 * na + n_out:])
        pl.when(first)(start)
        body(*ins, *outs)
        pl.when(last)(finish)

    return wrapped


def _pair_exchange(xs, name):
    na = len(xs)

    def body(*refs):
        copies = _pair_exchange_copies(refs[:na], refs[na:2 * na], *refs[2 * na:])
        for cp in copies:
            cp.start()
        for cp in copies:
            cp.wait()

    hbm = pl.BlockSpec(memory_space=pl.ANY)
    return pl.pallas_call(
        body, name=name, out_shape=_pair_exchange_out_shapes(xs), in_specs=[hbm] * na, out_specs=[hbm] * na,
        scratch_shapes=_pair_exchange_sems(na))(*xs)


def _pair_exchange_out_shapes(xs):
    return [jax.ShapeDtypeStruct((4,) + a.shape[1:], a.dtype) for a in xs]


def _pair_exchange_sems(na):
    return [pltpu.SemaphoreType.DMA((4 * na,)), pltpu.SemaphoreType.DMA((4 * na,))]


def _pair_exchange_copies(x_refs, out_refs, send_sems, recv_sems):
    x, y, c = lax.axis_index("x"), lax.axis_index("y"), lax.axis_index("c")
    return [pltpu.make_async_remote_copy(
        src_ref=x_refs[a].at[2 * q + 1 - c], dst_ref=out_refs[a].at[q],
        send_sem=send_sems.at[4 * a + q], recv_sem=recv_sems.at[4 * a + q],
        device_id=(x, y, 1 - c), device_id_type=MESH) for a in range(len(x_refs)) for q in range(4)]


def _row_tile(r, ncol):
    if r * ncol * 4 <= ROW_TILE_BYTES:
        return r
    return max(t for t in range(16, r, 16) if r % t == 0 and t * ncol * 4 <= ROW_TILE_BYTES)


def _pair_add(core, xs, sib, name):
    _, r, ncol = xs.shape
    tr = _row_tile(r, ncol)

    def body(c_ref, x_ref, s_ref, o_ref):
        o_ref[...] = (x_ref[...] + s_ref[...]).astype(MXU)

    return pl.pallas_call(
        body, name=name,
        grid_spec=pltpu.PrefetchScalarGridSpec(
            num_scalar_prefetch=1, grid=(4, r // tr),
            in_specs=[pl.BlockSpec((None, tr, ncol), lambda q, i, c_ref: (2 * q + c_ref[0], i, 0)),
                      pl.BlockSpec((None, tr, ncol), lambda q, i, c_ref: (q, i, 0))],
            out_specs=pl.BlockSpec((None, tr, ncol), lambda q, i, c_ref: (q, i, 0))),
        out_shape=jax.ShapeDtypeStruct((4, r, ncol), MXU),
        compiler_params=_cp("parallel", "parallel"))(core, xs, sib)


def _chip_exchange(xs, name):
    na = len(xs)

    def body(*refs):
        copies = _chip_exchange_copies(refs[:na], refs[na:2 * na], *refs[2 * na:])
        for cp in copies:
            cp.start()
        for cp in copies:
            cp.wait()

    hbm = pl.BlockSpec(memory_space=pl.ANY)
    return pl.pallas_call(
        body, name=name, out_shape=[jax.ShapeDtypeStruct(a.shape, a.dtype) for a in xs],
        in_specs=[hbm] * na, out_specs=[hbm] * na, scratch_shapes=_chip_exchange_sems(na))(*xs)


def _with_chip_exchange(body, n_in, n_out, na, grid, make_copies=None):
    if not na:
        return body
    make_copies = make_copies or _chip_exchange_copies

    def wrapped(*refs):
        ins, ride_in = refs[:n_in], refs[n_in:n_in + na]
        outs = refs[n_in + na:n_in + na + n_out]
        ride_out = refs[n_in + na + n_out:n_in + 2 * na + n_out]
        ids = [pl.program_id(k) for k in range(len(grid))]
        first, last = ids[0] == 0, ids[0] == grid[0] - 1
        for k in range(1, len(grid)):
            first, last = first & (ids[k] == 0), last & (ids[k] == grid[k] - 1)
        copies = make_copies(ride_in, ride_out, *refs[n_in + 2 * na + n_out:])

        @pl.when(first)
        def _():
            for cp in copies:
                cp.start()

        body(*ins, *outs)

        @pl.when(last)
        def _():
            for cp in copies:
                cp.wait()

    return wrapped


def _chip_exchange_sems(na):
    return [pltpu.SemaphoreType.DMA((3 * na,)), pltpu.SemaphoreType.DMA((3 * na,)), pltpu.SemaphoreType.DMA((na,))]


def _chip_exchange_copies(x_refs, out_refs, send_sems, recv_sems, local_sems):
    x, y, c = lax.axis_index("x"), lax.axis_index("y"), lax.axis_index("c")
    me = 2 * x + y
    copies = [pltpu.make_async_copy(x_refs[a].at[me], out_refs[a].at[me], local_sems.at[a]) for a in range(len(x_refs))]
    for k, (dx, dy) in enumerate([(1, 0), (0, 1), (1, 1)]):
        px = 1 - x if dx else x
        py = 1 - y if dy else y
        for a in range(len(x_refs)):
            copies.append(pltpu.make_async_remote_copy(
                src_ref=x_refs[a].at[2 * px + py], dst_ref=out_refs[a].at[me],
                send_sem=send_sems.at[3 * a + k], recv_sem=recv_sems.at[3 * a + k],
                device_id=(px, py, c), device_id_type=MESH))
    return copies


def _ada_fwd(c_all, w_ada, b_my, name):
    nl, d, n = w_ada.shape
    nb = c_all.shape[0]

    def body(c_ref, w_ref, b_ref, o_ref):
        cv = c_ref[...]
        sc = (cv * _sigmoid(cv)).astype(MXU)
        o_ref[...] = _dot(sc, w_ref[...].astype(MXU)) + b_ref[...]

    return pl.pallas_call(
        body, name=name, grid=(nl,),
        in_specs=[pl.BlockSpec((nb, d), lambda l: (0, 0)),
                  pl.BlockSpec((None, d, n), lambda l: (l, 0, 0)),
                  pl.BlockSpec((None, 1, n), lambda l: (l, 0, 0))],
        out_specs=pl.BlockSpec((None, nb, n), lambda l: (l, 0, 0)),
        out_shape=jax.ShapeDtypeStruct((nl, nb, n), F32),
        compiler_params=_cp("parallel"))(c_all, w_ada, b_my)


def _ada_bwd(c_all, dmods_my, dmods_all, name):
    nl, nb, n = dmods_my.shape
    d = c_all.shape[1]
    nfull = dmods_all.shape[2]

    def body(c_ref, dm_ref, da_ref, dw_ref, db_ref):
        cv = c_ref[...]
        sc = (cv * _sigmoid(cv)).astype(MXU)
        dw_ref[...] = _dot_tn(sc, dm_ref[...].astype(MXU))
        db_ref[...] = _csum(da_ref[...])

    return pl.pallas_call(
        body, name=name, grid=(nl,),
        in_specs=[pl.BlockSpec((nb, d), lambda l: (0, 0)),
                  pl.BlockSpec((None, nb, n), lambda l: (l, 0, 0)),
                  pl.BlockSpec((None, nb, nfull), lambda l: (l, 0, 0))],
        out_specs=[pl.BlockSpec((None, d, n), lambda l: (l, 0, 0)),
                   pl.BlockSpec((None, 1, nfull), lambda l: (l, 0, 0))],
        out_shape=[jax.ShapeDtypeStruct((nl, d, n), F32), jax.ShapeDtypeStruct((nl, 1, nfull), F32)],
        compiler_params=_cp("parallel"))(c_all, dmods_my, dmods_all)


def _ln_mod_matmul(x, g, scale, shift, w, name):
    nb, s, d = x.shape
    n = w.shape[0]
    tm, tn = min(1024, s), (1408 if n % 1408 == 0 else 1024)

    def body(x_ref, g_ref, sc_ref, sh_ref, w_ref, y_ref, h_ref, h_s):
        @pl.when(pl.program_id(2) == 0)
        def _():
            xf = x_ref[...]
            rstd = lax.rsqrt(jnp.mean(xf * xf, axis=-1, keepdims=True) + EPS)
            hv = (xf * rstd * g_ref[...]) * (1.0 + sc_ref[...]) + sh_ref[...]
            h_s[...] = hv.astype(MXU)
            h_ref[...] = h_s[...]

        y_ref[...] = _dot_nt(h_s[...], w_ref[...]).astype(MXU)

    return pl.pallas_call(
        body, name=name, grid=(nb, s // tm, n // tn),
        in_specs=[pl.BlockSpec((None, tm, d), lambda b, i, j: (b, i, 0)),
                  pl.BlockSpec((1, d), lambda b, i, j: (0, 0)),
                  pl.BlockSpec((None, 1, d), lambda b, i, j: (b, 0, 0)),
                  pl.BlockSpec((None, 1, d), lambda b, i, j: (b, 0, 0)),
                  pl.BlockSpec((tn, d), lambda b, i, j: (j, 0))],
        out_specs=[pl.BlockSpec((None, tm, tn), lambda b, i, j: (b, i, j)),
                   pl.BlockSpec((None, tm, d), lambda b, i, j: (b, i, 0))],
        out_shape=[jax.ShapeDtypeStruct((nb, s, n), MXU), jax.ShapeDtypeStruct((nb, s, d), MXU)],
        scratch_shapes=[pltpu.VMEM((tm, d), MXU)],
        compiler_params=_cp("parallel", "parallel", "arbitrary"))(x, g, scale, shift, w)


def _ln_mod_matmul_bwd(dy, w, x, g, scale, dres, conv_w, name, riding=()):
    nb, s, n = dy.shape
    d = x.shape[-1]
    tm, tn = min(512, s), (1408 if n % 1408 == 0 else 1024)
    ni, nj = s // tm, n // tn
    hb = tm // HALO
    conv = conv_w is not None
    na = len(riding)

    def body(*refs):
        if conv:
            dy_ref, nx_ref, cw_ref = refs[:3]
            refs = refs[3:]
        else:
            dy_ref = refs[0]
            refs = refs[1:]
        w_ref, x_ref, g_ref, sc_ref, dr_ref = refs[:5]
        ride_in, refs = refs[5:5 + na], refs[5 + na:]
        dx_ref, dyp_ref, dsh_ref, dsc_ref, dg_ref = refs[:5]
        ride_out, refs = refs[5:5 + na], refs[5 + na:]
        acc = refs[0]
        b, i, j = pl.program_id(0), pl.program_id(1), pl.program_id(2)
        if na:
            copies = _chip_exchange_copies(ride_in, ride_out, *refs[1:])

            @pl.when((b == 0) & (i == 0) & (j == 0))
            def _():
                for cp in copies:
                    cp.start()

        @pl.when(j == 0)
        def _():
            acc[...] = jnp.zeros_like(acc)

        @pl.when((j == 0) & (i == 0))
        def _():
            dsh_ref[...] = jnp.zeros_like(dsh_ref)
            dsc_ref[...] = jnp.zeros_like(dsc_ref)

        @pl.when((j == 0) & (i == 0) & (b == 0))
        def _():
            dg_ref[...] = jnp.zeros_like(dg_ref)

        dv = dy_ref[...].astype(F32)
        if conv:
            rows = _iota((tm, 1), 0)
            nx = jnp.where(i == ni - 1, 0.0, nx_ref[...].astype(F32))
            n1 = jnp.where(rows == tm - 1, nx[0:1, :], pltpu.roll(dv, tm - 1, 0))
            n2 = jnp.where(rows == tm - 2, nx[0:1, :], jnp.where(rows == tm - 1, nx[1:2, :], pltpu.roll(dv, tm - 2, 0)))
            cw = cw_ref[...]
            dv = cw[2:3, :] * dv + cw[1:2, :] * n1 + cw[0:1, :] * n2
        dp = dv.astype(MXU)
        dyp_ref[...] = dp
        acc[...] += _dot(dp, w_ref[...])

        @pl.when(j == nj - 1)
        def _():
            dh = acc[...]
            xf = x_ref[...]
            rstd = lax.rsqrt(jnp.mean(xf * xf, axis=-1, keepdims=True) + EPS)
            xn = xf * rstd
            gg = g_ref[...]
            sc1 = 1.0 + sc_ref[...]
            dsh_ref[...] += _csum(dh)
            dsc_ref[...] += _csum(dh * xn * gg)
            dg_ref[...] += _csum(dh * xn * sc1)
            dn = dh * gg * sc1
            dx_ref[...] = dr_ref[...] + rstd * (dn - xn * jnp.mean(dn * xn, axis=-1, keepdims=True))

        if na:
            @pl.when((b == nb - 1) & (i == ni - 1) & (j == nj - 1))
            def _():
                for cp in copies:
                    cp.wait()

    hbm = pl.BlockSpec(memory_space=pl.ANY)
    in_specs = [pl.BlockSpec((None, tm, tn), lambda b, i, j: (b, i, j))]
    args = [dy]
    if conv:
        in_specs += [pl.BlockSpec((None, HALO, tn), lambda b, i, j: (b, jnp.minimum((i + 1) * hb, s // HALO - 1), j)),
                     pl.BlockSpec((3, tn), lambda b, i, j: (0, j))]
        args += [dy, conv_w]
    in_specs += [pl.BlockSpec((tn, d), lambda b, i, j: (j, 0)),
                 pl.BlockSpec((None, tm, d), lambda b, i, j: (b, i, 0)),
                 pl.BlockSpec((1, d), lambda b, i, j: (0, 0)),
                 pl.BlockSpec((None, 1, d), lambda b, i, j: (b, 0, 0)),
                 pl.BlockSpec((None, tm, d), lambda b, i, j: (b, i, 0))]
    in_specs += [hbm] * na
    args += [w, x, g, scale, dres, *riding]
    return pl.pallas_call(
        body, name=name, grid=(nb, ni, nj), in_specs=in_specs,
        out_specs=[pl.BlockSpec((None, tm, d), lambda b, i, j: (b, i, 0)),
                   pl.BlockSpec((None, tm, tn), lambda b, i, j: (b, i, j)),
                   pl.BlockSpec((None, 1, d), lambda b, i, j: (b, 0, 0)),
                   pl.BlockSpec((None, 1, d), lambda b, i, j: (b, 0, 0)),
                   pl.BlockSpec((1, d), lambda b, i, j: (0, 0))] + [hbm] * na,
        out_shape=[jax.ShapeDtypeStruct((nb, s, d), F32), jax.ShapeDtypeStruct((nb, s, n), MXU),
                   jax.ShapeDtypeStruct((nb, 1, d), F32), jax.ShapeDtypeStruct((nb, 1, d), F32),
                   jax.ShapeDtypeStruct((1, d), F32)] + [jax.ShapeDtypeStruct(a.shape, a.dtype) for a in riding],
        scratch_shapes=[pltpu.VMEM((tm, d), F32)] + (_chip_exchange_sems(na) if na else []),
        compiler_params=_cp("arbitrary", "arbitrary", "arbitrary"))(*args)


def _wgrad(xm, dym, name):
    t, k = xm.shape
    n = dym.shape[1]
    tk = 1408 if k % 1408 == 0 else 1024
    tt = min(1024, t)

    def body(x_ref, dy_ref, o_ref):
        @pl.when(pl.program_id(1) == 0)
        def _():
            o_ref[...] = jnp.zeros_like(o_ref)

        o_ref[...] += _dot_tn(x_ref[...], dy_ref[...])

    return pl.pallas_call(
        body, name=name, grid=(k // tk, t // tt),
        in_specs=[pl.BlockSpec((tt, tk), lambda a, c: (c, a)),
                  pl.BlockSpec((tt, n), lambda a, c: (c, 0))],
        out_specs=pl.BlockSpec((tk, n), lambda a, c: (a, 0)),
        out_shape=jax.ShapeDtypeStruct((k, n), F32),
        compiler_params=_cp("parallel", "arbitrary"))(xm, dym)


def _out_proj(parts, ws, gate, res, name):
    nb, s, d = res.shape
    tm = min(512, s)
    npart = len(parts)

    def body(*refs):
        p_refs, w_refs = refs[:npart], refs[npart:2 * npart]
        gt_ref, res_ref, xo_ref, y_ref = refs[2 * npart:]
        y = _dot(p_refs[0][...].astype(MXU), w_refs[0][...])
        for p_ref, w_ref in zip(p_refs[1:], w_refs[1:]):
            y = y + _dot(p_ref[...].astype(MXU), w_ref[...])
        y_ref[...] = y
        xo_ref[...] = res_ref[...] + gt_ref[...] * y

    in_specs = [pl.BlockSpec((None, tm, p.shape[-1]), lambda b, i: (b, i, 0)) for p in parts]
    in_specs += [pl.BlockSpec(w.shape, lambda b, i: (0, 0)) for w in ws]
    in_specs += [pl.BlockSpec((None, 1, d), lambda b, i: (b, 0, 0)),
                 pl.BlockSpec((None, tm, d), lambda b, i: (b, i, 0))]
    return pl.pallas_call(
        body, name=name, grid=(nb, s // tm), in_specs=in_specs,
        out_specs=[pl.BlockSpec((None, tm, d), lambda b, i: (b, i, 0))] * 2,
        out_shape=[jax.ShapeDtypeStruct((nb, s, d), F32)] * 2,
        compiler_params=_cp("parallel", "parallel"))(*parts, *ws, gate, res)


def _gate_bwd_nt(dx, y, gate, ws, name):
    nb, s, d = dx.shape
    tm = min(512, s)
    npart = len(ws)

    def body(*refs):
        dx_ref, y_ref, gt_ref = refs[:3]
        w_refs = refs[3:3 + npart]
        da_refs = refs[3 + npart:3 + 2 * npart]
        dy_ref, dgt_ref = refs[3 + 2 * npart:]

        @pl.when(pl.program_id(1) == 0)
        def _():
            dgt_ref[...] = jnp.zeros_like(dgt_ref)

        dxv = dx_ref[...]
        dyv = (dxv * gt_ref[...]).astype(MXU)
        dy_ref[...] = dyv
        dgt_ref[...] += _csum(dxv * y_ref[...])
        for w_ref, da_ref in zip(w_refs, da_refs):
            da_ref[...] = _dot_nt(dyv, w_ref[...])

    tile = pl.BlockSpec((None, tm, d), lambda b, i: (b, i, 0))
    row = pl.BlockSpec((None, 1, d), lambda b, i: (b, 0, 0))
    outs = pl.pallas_call(
        body, name=name, grid=(nb, s // tm),
        in_specs=[tile, tile, row] + [pl.BlockSpec(w.shape, lambda b, i: (0, 0)) for w in ws],
        out_specs=[pl.BlockSpec((None, tm, w.shape[0]), lambda b, i: (b, i, 0)) for w in ws] + [tile, row],
        out_shape=[jax.ShapeDtypeStruct((nb, s, w.shape[0]), F32) for w in ws]
        + [jax.ShapeDtypeStruct((nb, s, d), MXU), jax.ShapeDtypeStruct((nb, 1, d), F32)],
        compiler_params=_cp("arbitrary", "arbitrary"))(dx, y, gate, *ws)
    return outs[:npart], outs[npart], outs[npart + 1]


def _conv_shifts(xv, halo, rows):
    last, before = halo[HALO - 1:HALO, :], halo[HALO - 2:HALO - 1, :]
    p1 = jnp.where(rows == 0, last, pltpu.roll(xv, 1, 0))
    p2 = jnp.where(rows == 0, before, jnp.where(rows == 1, last, pltpu.roll(xv, 2, 0)))
    return p1, p2


def _conv_gate_matmul(u, cw, cb, wd, gate, res, name):
    nb, s, f2 = u.shape
    f = f2 // 2
    d = wd.shape[1]
    tm = min(512, s)
    tk = f // 2
    nk = f // tk
    hb = tm // HALO

    def body(ug_ref, uv_ref, hg_ref, hv_ref, cwg_ref, cwv_ref, cbg_ref, cbv_ref, wd_ref, gt_ref, res_ref,
             xo_ref, y_ref, acc):
        i, k = pl.program_id(1), pl.program_id(2)

        @pl.when(k == 0)
        def _():
            acc[...] = jnp.zeros_like(acc)

        rows = _iota((tm, 1), 0)

        def conv(x_ref, h_ref, w_ref, b_ref):
            xv = x_ref[...].astype(F32)
            halo = jnp.where(i == 0, 0.0, h_ref[...].astype(F32))
            p1, p2 = _conv_shifts(xv, halo, rows)
            wv = w_ref[...]
            return wv[2:3, :] * xv + wv[1:2, :] * p1 + wv[0:1, :] * p2 + b_ref[...]

        gv = conv(ug_ref, hg_ref, cwg_ref, cbg_ref)
        vv = conv(uv_ref, hv_ref, cwv_ref, cbv_ref)
        av = gv * _sigmoid(gv) * vv
        acc[...] += _dot(av.astype(MXU), wd_ref[...])

        @pl.when(k == nk - 1)
        def _():
            y = acc[...]
            y_ref[...] = y
            xo_ref[...] = res_ref[...] + gt_ref[...] * y

    def halo_idx(off):
        return lambda b, i, k: (b, jnp.maximum(i * hb - 1, 0), k + off)

    tile = pl.BlockSpec((None, tm, d), lambda b, i, k: (b, i, 0))
    return pl.pallas_call(
        body, name=name, grid=(nb, s // tm, nk),
        in_specs=[pl.BlockSpec((None, tm, tk), lambda b, i, k: (b, i, k)),
                  pl.BlockSpec((None, tm, tk), lambda b, i, k: (b, i, k + nk)),
                  pl.BlockSpec((None, HALO, tk), halo_idx(0)),
                  pl.BlockSpec((None, HALO, tk), halo_idx(nk)),
                  pl.BlockSpec((3, tk), lambda b, i, k: (0, k)),
                  pl.BlockSpec((3, tk), lambda b, i, k: (0, k + nk)),
                  pl.BlockSpec((1, tk), lambda b, i, k: (0, k)),
                  pl.BlockSpec((1, tk), lambda b, i, k: (0, k + nk)),
                  pl.BlockSpec((tk, d), lambda b, i, k: (k, 0)),
                  pl.BlockSpec((None, 1, d), lambda b, i, k: (b, 0, 0)),
                  tile],
        out_specs=[tile, tile],
        out_shape=[jax.ShapeDtypeStruct((nb, s, d), F32)] * 2,
        scratch_shapes=[pltpu.VMEM((tm, d), F32)],
        compiler_params=_cp("parallel", "parallel", "arbitrary"))(u, u, u, u, cw, cw, cb, cb, wd, gate, res)


def _conv_gate_bwd(da, u, cw, cb, name):
    nb, s, f2 = u.shape
    f = f2 // 2
    tm = min(128, s)
    hb = tm // HALO

    def body(da_ref, u_ref, h_ref, cw_ref, cb_ref, du_ref, a_ref, st_ref):
        b, i = pl.program_id(0), pl.program_id(1)

        @pl.when((b == 0) & (i == 0))
        def _():
            st_ref[...] = jnp.zeros_like(st_ref)

        rows = _iota((tm, 1), 0)
        first = i == 0

        def conv(cs):
            xv = u_ref[:, cs].astype(F32)
            halo = jnp.where(first, 0.0, h_ref[:, cs].astype(F32))
            p1, p2 = _conv_shifts(xv, halo, rows)
            wv = cw_ref[:, cs]
            return xv, p1, p2, wv[2:3, :] * xv + wv[1:2, :] * p1 + wv[0:1, :] * p2 + cb_ref[:, cs]

        def stats(cs, du, xv, p1, p2):
            du_ref[:, cs] = du.astype(MXU)
            st_ref[0:1, cs] += _csum(du)
            st_ref[1:2, cs] += _csum(du * p2)
            st_ref[2:3, cs] += _csum(du * p1)
            st_ref[3:4, cs] += _csum(du * xv)

        for k in range(f // LANES):
            cg = slice(k * LANES, (k + 1) * LANES)
            cv = slice(f + k * LANES, f + (k + 1) * LANES)
            xg, g1, g2, gv = conv(cg)
            xv, v1, v2, vv = conv(cv)
            sg = _sigmoid(gv)
            sl = gv * sg
            a_ref[:, cg] = (sl * vv).astype(MXU)
            dav = da_ref[:, cg]
            stats(cg, dav * vv * (sg * (1.0 + gv * (1.0 - sg))), xg, g1, g2)
            stats(cv, dav * sl, xv, v1, v2)

    return pl.pallas_call(
        body, name=name, grid=(nb, s // tm),
        in_specs=[pl.BlockSpec((None, tm, f), lambda b, i: (b, i, 0)),
                  pl.BlockSpec((None, tm, f2), lambda b, i: (b, i, 0)),
                  pl.BlockSpec((None, HALO, f2), lambda b, i: (b, jnp.maximum(i * hb - 1, 0), 0)),
                  pl.BlockSpec((3, f2), lambda b, i: (0, 0)),
                  pl.BlockSpec((1, f2), lambda b, i: (0, 0))],
        out_specs=[pl.BlockSpec((None, tm, f2), lambda b, i: (b, i, 0)),
                   pl.BlockSpec((None, tm, f), lambda b, i: (b, i, 0)),
                   pl.BlockSpec((8, f2), lambda b, i: (0, 0))],
        out_shape=[jax.ShapeDtypeStruct((nb, s, f2), MXU), jax.ShapeDtypeStruct((nb, s, f), MXU),
                   jax.ShapeDtypeStruct((8, f2), F32)],
        compiler_params=_cp("arbitrary", "arbitrary"))(da, u, u, cw, cb)


def _rot(xv, lane):
    return jnp.where((lane >= 64) & (lane < 80), -pltpu.roll(xv, 112, 1),
                     jnp.where((lane >= 80) & (lane < 96), pltpu.roll(xv, 16, 1), 0.0))


def _rot_t(dv, lane):
    return jnp.where((lane >= 80) & (lane < 96), -pltpu.roll(dv, 16, 1),
                     jnp.where((lane >= 64) & (lane < 80), pltpu.roll(dv, 112, 1), 0.0))


def _mla_prep_specs(s, tm):
    def blk(width, col):
        return pl.BlockSpec((None, tm, width), lambda b, i: (b, i, col // width))

    full = lambda shape: pl.BlockSpec(shape, lambda b, i: (0, 0))
    return [blk(256, COL_CQ), blk(128, COL_CKV), blk(128, COL_KR),
            pl.BlockSpec((None, tm, LANES), lambda b, i: (b, i, 0)),
            pl.BlockSpec((None, tm, LANES), lambda b, i: (b, i, 0)),
            full((1, 256)), full((1, 128)), full((1, 128)), full((1, 128)),
            full((768, 256)), full((768, 128))]


def _mla_prep(proj, cs, sn, gcq, gckv, gqn, gkn, wuq, wukv, name):
    nb, s, _ = proj.shape
    tm = min(256, s)

    def body(cq_ref, ckv_ref, kr_ref, c_ref, s_ref, gcq_ref, gckv_ref, gqn_ref, gkn_ref, wuq_ref, wukv_ref,
             q_ref, k_ref, v_ref):
        lane = _iota((tm, LANES), 1)
        cv, sv = c_ref[...], s_ref[...]
        cq = cq_ref[...].astype(F32)
        cqn = cq * lax.rsqrt(jnp.mean(cq * cq, axis=-1, keepdims=True) + EPS) * gcq_ref[...]
        qb = _dot_nt(cqn.astype(MXU), wuq_ref[...])
        ckv = ckv_ref[...].astype(F32)
        ckvn = ckv * lax.rsqrt(jnp.mean(ckv * ckv, axis=-1, keepdims=True) + EPS) * gckv_ref[...]
        kvb = _dot_nt(ckvn.astype(MXU), wukv_ref[...])
        kr = kr_ref[...].astype(F32)
        for h in range(MLA_HEADS):
            hs = slice(h * LANES, (h + 1) * LANES)
            qh = qb[:, hs]
            qn = qh * lax.rsqrt(_rsum(qh * qh) / MLA_QK + EPS) * gqn_ref[...]
            q_ref[:, hs] = (qn * cv + _rot(qn, lane) * sv).astype(MXU)
            kc = jnp.where(lane < HEAD, kvb[:, hs], kr)
            kn = kc * lax.rsqrt(_rsum(kc * kc) / MLA_QK + EPS) * gkn_ref[...]
            k_ref[:, hs] = (kn * cv + _rot(kn, lane) * sv).astype(MXU)
        for j in range(MLA_HEADS // 2):
            va = kvb[:, (2 * j) * LANES:(2 * j + 1) * LANES]
            vb = kvb[:, (2 * j + 1) * LANES:(2 * j + 2) * LANES]
            v_ref[:, j * LANES:(j + 1) * LANES] = jnp.where(lane < HEAD, pltpu.roll(va, HEAD, 1), vb).astype(MXU)

    return pl.pallas_call(
        body, name=name, grid=(nb, s // tm), in_specs=_mla_prep_specs(s, tm),
        out_specs=[pl.BlockSpec((None, tm, 768), lambda b, i: (b, i, 0)),
                   pl.BlockSpec((None, tm, 768), lambda b, i: (b, i, 0)),
                   pl.BlockSpec((None, tm, 384), lambda b, i: (b, i, 0))],
        out_shape=[jax.ShapeDtypeStruct((nb, s, 768), MXU), jax.ShapeDtypeStruct((nb, s, 768), MXU),
                   jax.ShapeDtypeStruct((nb, s, 384), MXU)],
        compiler_params=_cp("parallel", "parallel"))(proj, proj, proj, cs, sn, gcq, gckv, gqn, gkn, wuq, wukv)


def _mla_prep_bwd(proj, cs, sn, gcq, gckv, gqn, gkn, wuq, wukv, dq, dk, dv, name):
    nb, s, _ = proj.shape
    tm = min(256, s)

    def body(cq_ref, ckv_ref, kr_ref, c_ref, s_ref, gcq_ref, gckv_ref, gqn_ref, gkn_ref, wuq_ref, wukv_ref,
             dq_ref, dk_ref, dv_ref,
             dcq_ref, dckv_ref, dkr_ref, dwuq_ref, dwukv_ref, dgcq_ref, dgckv_ref, dgqn_ref, dgkn_ref,
             dqb_s, dkvb_s):
        @pl.when((pl.program_id(0) == 0) & (pl.program_id(1) == 0))
        def _():
            for r in (dwuq_ref, dwukv_ref, dgcq_ref, dgckv_ref, dgqn_ref, dgkn_ref):
                r[...] = jnp.zeros_like(r)

        lane = _iota((tm, LANES), 1)
        cv, sv = c_ref[...], s_ref[...]
        gqn, gkn = gqn_ref[...], gkn_ref[...]
        cq = cq_ref[...].astype(F32)
        rc = lax.rsqrt(jnp.mean(cq * cq, axis=-1, keepdims=True) + EPS)
        chat = cq * rc
        cqn = (chat * gcq_ref[...]).astype(MXU)
        qb = _dot_nt(cqn, wuq_ref[...])
        ckv = ckv_ref[...].astype(F32)
        rkv = lax.rsqrt(jnp.mean(ckv * ckv, axis=-1, keepdims=True) + EPS)
        kvhat = ckv * rkv
        ckvn = (kvhat * gckv_ref[...]).astype(MXU)
        kvb = _dot_nt(ckvn, wukv_ref[...])
        kr = kr_ref[...].astype(F32)
        dgq = jnp.zeros((1, LANES), F32)
        dgk = jnp.zeros((1, LANES), F32)
        dkr = jnp.zeros((tm, LANES), F32)
        for h in range(MLA_HEADS):
            hs = slice(h * LANES, (h + 1) * LANES)
            qh = qb[:, hs]
            rq = lax.rsqrt(_rsum(qh * qh) / MLA_QK + EPS)
            qhat = qh * rq
            dqr = dq_ref[:, hs]
            dqn = dqr * cv + _rot_t(dqr * sv, lane)
            dgq = dgq + _csum(dqn * qhat)
            dyq = dqn * gqn
            dqb_s[:, hs] = (rq * (dyq - qhat * (_rsum(dyq * qhat) / MLA_QK))).astype(MXU)

            kc = jnp.where(lane < HEAD, kvb[:, hs], kr)
            rk = lax.rsqrt(_rsum(kc * kc) / MLA_QK + EPS)
            khat = kc * rk
            dkr_h = dk_ref[:, hs]
            dkn = dkr_h * cv + _rot_t(dkr_h * sv, lane)
            dgk = dgk + _csum(dkn * khat)
            dyk = dkn * gkn
            dkc = rk * (dyk - khat * (_rsum(dyk * khat) / MLA_QK))
            dkr = dkr + jnp.where(lane >= HEAD, dkc, 0.0)
            dvb = dv_ref[:, (h // 2) * LANES:(h // 2 + 1) * LANES]
            dvp = dvb if h % 2 == 1 else pltpu.roll(dvb, HEAD, 1)
            dkvb_s[:, hs] = jnp.where(lane < HEAD, dkc, dvp).astype(MXU)
        dgqn_ref[...] += dgq
        dgkn_ref[...] += dgk
        dkr_ref[...] = dkr

        dqb = dqb_s[...]
        dwuq_ref[...] += _dot_tn(dqb, cqn)
        dcqn = _dot(dqb, wuq_ref[...])
        dgcq_ref[...] += _csum(dcqn * chat)
        dyc = dcqn * gcq_ref[...]
        dcq_ref[...] = rc * (dyc - chat * jnp.mean(dyc * chat, axis=-1, keepdims=True))

        dkvb = dkvb_s[...]
        dwukv_ref[...] += _dot_tn(dkvb, ckvn)
        dckvn = _dot(dkvb, wukv_ref[...])
        dgckv_ref[...] += _csum(dckvn * kvhat)
        dykv = dckvn * gckv_ref[...]
        dckv_ref[...] = rkv * (dykv - kvhat * jnp.mean(dykv * kvhat, axis=-1, keepdims=True))

    full = lambda shape: pl.BlockSpec(shape, lambda b, i: (0, 0))
    tile = lambda width: pl.BlockSpec((None, tm, width), lambda b, i: (b, i, 0))
    return pl.pallas_call(
        body, name=name, grid=(nb, s // tm),
        in_specs=_mla_prep_specs(s, tm) + [tile(768), tile(768), tile(384)],
        out_specs=[tile(256), tile(128), tile(128), full((768, 256)), full((768, 128)),
                   full((1, 256)), full((1, 128)), full((1, 128)), full((1, 128))],
        out_shape=[jax.ShapeDtypeStruct((nb, s, 256), F32), jax.ShapeDtypeStruct((nb, s, 128), F32),
                   jax.ShapeDtypeStruct((nb, s, 128), F32),
                   jax.ShapeDtypeStruct((768, 256), F32), jax.ShapeDtypeStruct((768, 128), F32),
                   jax.ShapeDtypeStruct((1, 256), F32), jax.ShapeDtypeStruct((1, 128), F32),
                   jax.ShapeDtypeStruct((1, 128), F32), jax.ShapeDtypeStruct((1, 128), F32)],
        scratch_shapes=[pltpu.VMEM((tm, 768), MXU), pltpu.VMEM((tm, 768), MXU)],
        compiler_params=_cp("arbitrary", "arbitrary"))(
            proj, proj, proj, cs, sn, gcq, gckv, gqn, gkn, wuq, wukv, dq, dk, dv)


def _softplus(z):
    return jnp.maximum(z, 0.0) + jnp.log(1.0 + jnp.exp(-jnp.abs(z)))


def _sb_fwd(proj, name, riding=()):
    nb, s, _ = proj.shape
    tq, tk = min(256, s), min(256, s)
    ratio = tq // tk
    na = len(riding)
    grid = (nb, 2, s // tq)

    def body(q_ref, k_ref, v_ref, o_ref, ct_ref, cnt_ref):
        i = pl.program_id(2)
        lo = _iota((tq, LANES), 1) < HEAD
        qv = q_ref[...]
        q0 = jnp.where(lo, qv, 0.0).astype(MXU)
        q1 = jnp.where(lo, 0.0, qv).astype(MXU)
        usuf = (_iota((tk, tk), 0) > _iota((tk, tk), 1)).astype(MXU)
        tpos = i * tq + _iota((tq, tk), 0)
        scol = _iota((tq, tk), 1)
        nch = (i + 1) * ratio

        def alive(st):
            return (st[0] < nch) & (st[5] > SB_DEAD)

        def step(st):
            t, c0, a0, c1, a1, _ = st
            j = nch - 1 - t
            off = pl.multiple_of(j * tk, tk)
            kc = k_ref[pl.ds(off, tk), :].astype(MXU)
            vc = v_ref[pl.ds(off, tk), :].astype(MXU)
            msk = (scol + j * tk) < tpos

            def head(qm, c, a):
                z = _dot_nt(qm, kc) * SB_SCALE
                sp = _softplus(z)
                lk = jnp.where(msk, -sp, 0.0)
                w = jnp.where(msk, jnp.exp(z - sp + _cumdot(lk, usuf) + c), 0.0)
                return c + _rsum(lk), a + _dot(w.astype(MXU), vc)

            c0, a0 = head(q0, c0, a0)
            c1, a1 = head(q1, c1, a1)
            return t + 1, c0, a0, c1, a1, jnp.maximum(jnp.max(c0), jnp.max(c1))

        z1 = jnp.zeros((tq, 1), F32)
        za = jnp.zeros((tq, LANES), F32)
        t, c0, a0, c1, a1, _ = lax.while_loop(alive, step, (jnp.int32(0), z1, za, z1, za, jnp.float32(0.0)))
        o_ref[...] = jnp.where(lo, a0, a1)
        ct_ref[...] = jnp.where(lo, c0, c1)
        cnt_ref[...] = jnp.zeros((8, LANES), F32) + t.astype(F32)

    kv = lambda col: pl.BlockSpec((None, s, LANES), lambda b, p, i: (b, 0, col // LANES + p))
    tile = pl.BlockSpec((None, tq, LANES), lambda b, p, i: (b, i, p))
    hbm = pl.BlockSpec(memory_space=pl.ANY)
    return pl.pallas_call(
        _with_gather(body, 3, 3, na, grid), name=name, grid=grid,
        in_specs=[pl.BlockSpec((None, tq, LANES), lambda b, p, i: (b, i, COL_SBQ // LANES + p)),
                  kv(COL_SBK), kv(COL_SBV)] + [hbm] * na,
        out_specs=[tile, tile, pl.BlockSpec((None, None, None, 8, LANES), lambda b, p, i: (b, p, i, 0, 0))] + [hbm] * na,
        out_shape=[jax.ShapeDtypeStruct((nb, s, 256), F32)] * 2
        + [jax.ShapeDtypeStruct((nb, 2, s // tq, 8, LANES), F32)] + _gather_out_shapes(riding),
        scratch_shapes=_gather_sems(na) if na else [],
        compiler_params=_cp("arbitrary", "arbitrary", "arbitrary"))(proj, proj, proj, *riding)


def _sb_bwd(proj, ct, cnt, do, name, riding=()):
    nb, s, _ = proj.shape
    tq, tk = min(256, s), min(256, s)
    ratio = tq // tk
    na = len(riding)
    grid = (nb, 2, s // tq)

    def body(q_ref, k_ref, v_ref, ct_ref, cnt_ref, do_ref, dq_ref, dk_ref, dv_ref):
        i = pl.program_id(2)

        @pl.when(i == 0)
        def _():
            dk_ref[...] = jnp.zeros_like(dk_ref)
            dv_ref[...] = jnp.zeros_like(dv_ref)

        lane = _iota((tq, LANES), 1)
        lo = lane < HEAD
        lok = _iota((tk, LANES), 1) < HEAD
        qv, dov = q_ref[...], do_ref[...]
        qb, dob = qv.astype(MXU), dov.astype(MXU)
        q0 = jnp.where(lo, qv, 0.0).astype(MXU)
        q1 = jnp.where(lo, 0.0, qv).astype(MXU)
        do0 = jnp.where(lo, dov, 0.0).astype(MXU)
        do1 = jnp.where(lo, 0.0, dov).astype(MXU)
        ctv = ct_ref[...]
        ct0 = _rsum(jnp.where(lane == 0, ctv, 0.0))
        ct1 = _rsum(jnp.where(lane == LANES - 1, ctv, 0.0))
        uincl = (_iota((tk, tk), 0) <= _iota((tk, tk), 1)).astype(MXU)
        ustrict = (_iota((tk, tk), 0) < _iota((tk, tk), 1)).astype(MXU)
        tpos = i * tq + _iota((tq, tk), 0)
        scol = _iota((tq, tk), 1)
        nch = (i + 1) * ratio

        def step(j, carry):
            p0, g0, dq0, p1, g1, dq1 = carry
            off = pl.multiple_of(j * tk, tk)
            kc = k_ref[pl.ds(off, tk), :].astype(MXU)
            vc = v_ref[pl.ds(off, tk), :].astype(MXU)
            msk = (scol + j * tk) < tpos

            def head(qm, dom, ctot, pc, gc, dqa):
                z = _dot_nt(qm, kc) * SB_SCALE
                sp = _softplus(z)
                lk = jnp.where(msk, -sp, 0.0)
                lsig = z - sp
                w = jnp.where(msk, jnp.exp(lsig + (ctot - pc - _cumdot(lk, uincl))), 0.0)
                g = w * _dot_nt(dom, vc)
                gpre = gc + _cumdot(g, ustrict)
                sig = jnp.exp(lsig)
                dz = (jnp.where(msk, g * (1.0 - sig) - sig * gpre, 0.0) * SB_SCALE).astype(MXU)
                return (pc + _rsum(lk), gc + _rsum(g), dqa + _dot(dz, kc),
                        _dot_tn(dz, qb), _dot_tn(w.astype(MXU), dob))

            p0, g0, dq0, dk0, dv0 = head(q0, do0, ct0, p0, g0, dq0)
            p1, g1, dq1, dk1, dv1 = head(q1, do1, ct1, p1, g1, dq1)
            dk_ref[pl.ds(off, tk), :] += jnp.where(lok, dk0, dk1)
            dv_ref[pl.ds(off, tk), :] += jnp.where(lok, dv0, dv1)
            return p0, g0, dq0, p1, g1, dq1

        z1 = jnp.zeros((tq, 1), F32)
        za = jnp.zeros((tq, LANES), F32)
        first = nch - jnp.max(cnt_ref[...]).astype(jnp.int32)
        _, _, dq0, _, _, dq1 = lax.fori_loop(first, nch, step, (z1, z1, za, z1, z1, za))
        dq_ref[...] = jnp.where(lo, dq0, dq1)

    kv = lambda col: pl.BlockSpec((None, s, LANES), lambda b, p, i: (b, 0, col // LANES + p))
    tile = pl.BlockSpec((None, tq, LANES), lambda b, p, i: (b, i, p))
    acc = pl.BlockSpec((None, s, LANES), lambda b, p, i: (b, 0, p))
    hbm = pl.BlockSpec(memory_space=pl.ANY)
    return pl.pallas_call(
        _with_chip_exchange(body, 6, 3, na, grid, _pair_exchange_copies), name=name, grid=grid,
        in_specs=[pl.BlockSpec((None, tq, LANES), lambda b, p, i: (b, i, COL_SBQ // LANES + p)),
                  kv(COL_SBK), kv(COL_SBV), tile,
                  pl.BlockSpec((None, None, None, 8, LANES), lambda b, p, i: (b, p, i, 0, 0)), tile] + [hbm] * na,
        out_specs=[tile, acc, acc] + [hbm] * na,
        out_shape=[jax.ShapeDtypeStruct((nb, s, 256), F32)] * 3 + _pair_exchange_out_shapes(riding),
        scratch_shapes=_pair_exchange_sems(na) if na else [],
        compiler_params=_cp("arbitrary", "arbitrary", "arbitrary"))(proj, proj, proj, ct, cnt, do, *riding)


def _mla_fwd(q, k, v, name, riding=()):
    nb, s, _ = q.shape
    tq = tk = min(1024, s)
    na = len(riding)
    grid = (nb, MLA_HEADS // 2, s // tq)

    def body(q_ref, k_ref, v_ref, o_ref, lse_ref):
        i = pl.program_id(2)
        q0, q1 = q_ref[:, :LANES], q_ref[:, LANES:]
        krow = _iota((tk, tq), 0)
        qcol = _iota((tk, tq), 1)

        def step(j, carry, diagonal):
            m0, l0, a0, m1, l1, a1 = carry
            off = pl.multiple_of(j * tk, tk)
            vc = v_ref[pl.ds(off, tk), :]

            def head(qh, kh, m, l, a):
                st = _dot_nt(kh, qh) * MLA_SCALE
                if diagonal:
                    st = jnp.where(krow <= qcol, st, NEG)
                mn = jnp.maximum(m, jnp.max(st, axis=0, keepdims=True))
                al = jnp.exp(m - mn)
                pt = jnp.exp(st - mn)
                return mn, al * l + _csum(pt), al * a + _dot_tn(vc, pt.astype(MXU))

            m0, l0, a0 = head(q0, k_ref[pl.ds(off, tk), :LANES], m0, l0, a0)
            m1, l1, a1 = head(q1, k_ref[pl.ds(off, tk), LANES:], m1, l1, a1)
            return m0, l0, a0, m1, l1, a1

        mi = jnp.full((1, tq), NEG, F32)
        z1 = jnp.zeros((1, tq), F32)
        za = jnp.zeros((LANES, tq), F32)
        carry = lax.fori_loop(0, i, lambda j, cr: step(j, cr, False), (mi, z1, za, mi, z1, za))
        m0, l0, a0, m1, l1, a1 = step(i, carry, True)
        lo_rows = _iota((LANES, tq), 0) < HEAD
        o_ref[...] = jnp.where(lo_rows, a0 / l0, a1 / l1).T
        lse_ref[...] = jnp.zeros_like(lse_ref)
        lse_ref[0:1, :] = m0 + jnp.log(l0)
        lse_ref[1:2, :] = m1 + jnp.log(l1)

    tile = pl.BlockSpec((None, tq, LANES), lambda b, p, i: (b, i, p))
    hbm = pl.BlockSpec(memory_space=pl.ANY)
    return pl.pallas_call(
        _with_gather(body, 3, 2, na, grid), name=name, grid=grid,
        in_specs=[pl.BlockSpec((None, tq, 2 * LANES), lambda b, p, i: (b, i, p)),
                  pl.BlockSpec((None, s, 2 * LANES), lambda b, p, i: (b, 0, p)),
                  pl.BlockSpec((None, s, LANES), lambda b, p, i: (b, 0, p))] + [hbm] * na,
        out_specs=[tile, pl.BlockSpec((None, None, 8, tq), lambda b, p, i: (b, p, 0, i))] + [hbm] * na,
        out_shape=[jax.ShapeDtypeStruct((nb, s, 384), F32), jax.ShapeDtypeStruct((nb, MLA_HEADS // 2, 8, s), F32)]
        + _gather_out_shapes(riding),
        scratch_shapes=_gather_sems(na) if na else [],
        compiler_params=_cp("arbitrary", "arbitrary", "arbitrary"))(q, k, v, *riding)


def _mla_bwd(q, k, v, o, lse, do, name, riding=()):
    nb, s, _ = q.shape
    tq = tk = min(1024, s)
    na = len(riding)
    grid = (nb, MLA_HEADS // 2, s // tq)

    def body(q_ref, k_ref, v_ref, o_ref, lse_ref, do_ref, dq_ref, dk_ref, dv_ref):
        i = pl.program_id(2)

        @pl.when(i == 0)
        def _():
            dk_ref[...] = jnp.zeros_like(dk_ref)
            dv_ref[...] = jnp.zeros_like(dv_ref)

        lo = _iota((tq, LANES), 1) < HEAD
        lok = _iota((tk, LANES), 1) < HEAD
        q0, q1 = q_ref[:, :LANES], q_ref[:, LANES:]
        dov = do_ref[...]
        dob = dov.astype(MXU)
        do0 = jnp.where(lo, dov, 0.0).astype(MXU)
        do1 = jnp.where(lo, 0.0, dov).astype(MXU)
        dd = dov * o_ref[...]
        hi = dd.astype(MXU)
        r1 = dd - hi.astype(F32)
        mid = r1.astype(MXU)
        low = (r1 - mid.astype(F32)).astype(MXU)
        sel_lane = _iota((8, LANES), 1) < HEAD
        sel0 = sel_lane.astype(MXU)
        sel1 = (~sel_lane).astype(MXU)
        dl0 = (_dot_nt(sel0, hi) + _dot_nt(sel0, mid) + _dot_nt(sel0, low))[0:1, :]
        dl1 = (_dot_nt(sel1, hi) + _dot_nt(sel1, mid) + _dot_nt(sel1, low))[0:1, :]
        ls0, ls1 = lse_ref[0:1, :], lse_ref[1:2, :]
        krow = _iota((tk, tq), 0)
        qcol = _iota((tk, tq), 1)

        def step(j, carry, diagonal):
            dq0, dq1 = carry
            off = pl.multiple_of(j * tk, tk)
            vc = v_ref[pl.ds(off, tk), :]

            def head(qh, kh, dom, ls, dl, dqa):
                st = _dot_nt(kh, qh) * MLA_SCALE
                if diagonal:
                    st = jnp.where(krow <= qcol, st, NEG)
                pt = jnp.exp(st - ls)
                dst = (pt * (_dot_nt(vc, dom) - dl) * MLA_SCALE).astype(MXU)
                return dqa + _dot_tn(kh, dst), _dot(dst, qh), _dot(pt.astype(MXU), dob)

            dq0, dk0, dv0 = head(q0, k_ref[pl.ds(off, tk), :LANES], do0, ls0, dl0, dq0)
            dq1, dk1, dv1 = head(q1, k_ref[pl.ds(off, tk), LANES:], do1, ls1, dl1, dq1)
            dk_ref[pl.ds(off, tk), :LANES] += dk0
            dk_ref[pl.ds(off, tk), LANES:] += dk1
            dv_ref[pl.ds(off, tk), :] += jnp.where(lok, dv0, dv1)
            return dq0, dq1

        za = jnp.zeros((LANES, tq), F32)
        carry = lax.fori_loop(0, i, lambda j, cr: step(j, cr, False), (za, za))
        dq0, dq1 = step(i, carry, True)
        dq_ref[:, :LANES] = dq0.T
        dq_ref[:, LANES:] = dq1.T

    tile = pl.BlockSpec((None, tq, LANES), lambda b, p, i: (b, i, p))
    tile2 = pl.BlockSpec((None, tq, 2 * LANES), lambda b, p, i: (b, i, p))
    hbm = pl.BlockSpec(memory_space=pl.ANY)
    return pl.pallas_call(
        _with_chip_exchange(body, 6, 3, na, grid), name=name, grid=grid,
        in_specs=[tile2,
                  pl.BlockSpec((None, s, 2 * LANES), lambda b, p, i: (b, 0, p)),
                  pl.BlockSpec((None, s, LANES), lambda b, p, i: (b, 0, p)),
                  tile, pl.BlockSpec((None, None, 8, tq), lambda b, p, i: (b, p, 0, i)), tile] + [hbm] * na,
        out_specs=[tile2,
                   pl.BlockSpec((None, s, 2 * LANES), lambda b, p, i: (b, 0, p)),
                   pl.BlockSpec((None, s, LANES), lambda b, p, i: (b, 0, p))] + [hbm] * na,
        out_shape=[jax.ShapeDtypeStruct((nb, s, 768), F32), jax.ShapeDtypeStruct((nb, s, 768), F32),
                   jax.ShapeDtypeStruct((nb, s, 384), F32)] + [jax.ShapeDtypeStruct(a.shape, a.dtype) for a in riding],
        scratch_shapes=_chip_exchange_sems(na) if na else [],
        compiler_params=_cp("arbitrary", "arbitrary", "arbitrary"))(q, k, v, o, lse, do, *riding)


def _half_stats(xv, lo):
    x2 = xv * xv
    s0 = _rsum(jnp.where(lo, x2, 0.0))
    s1 = _rsum(jnp.where(lo, 0.0, x2))
    return jnp.where(lo, lax.rsqrt(s0 / HEAD + EPS), lax.rsqrt(s1 / HEAD + EPS))


def _half_mean(xv, lo):
    s0 = _rsum(jnp.where(lo, xv, 0.0))
    s1 = _rsum(jnp.where(lo, 0.0, xv))
    return jnp.where(lo, s0, s1) / HEAD


def _swa_in_specs():
    def band(col, prev):
        if prev:
            return pl.BlockSpec((None, BLOCK, LANES), lambda b, n: (b, jnp.maximum(n - 1, 0), col // LANES))
        return pl.BlockSpec((None, BLOCK, LANES), lambda b, n: (b, n, col // LANES))

    full = lambda shape: pl.BlockSpec(shape, lambda b, n: tuple(0 for _ in shape))
    return [pl.BlockSpec((None, BLOCK, 384), lambda b, n: (b, n, COL_SWQ // 384)),
            band(COL_SWK, False), band(COL_SWK, True), band(COL_SWV, False), band(COL_SWV, True),
            full((1, LANES)), full((1, LANES)), full((8, LANES)), full((SW_HEADS, BLOCK, 2 * BLOCK))]


def _swa_valid(n):
    a = _iota((BLOCK, 2 * BLOCK), 0)
    bcol = _iota((BLOCK, 2 * BLOCK), 1)
    dist = BLOCK + a - bcol
    return (dist >= 0) & (dist < BLOCK) & ((n > 0) | (bcol >= BLOCK))


def _swa_fwd(proj, gq, gk, sinks, bias, name):
    nb, s, _ = proj.shape

    def body(q_ref, kc_ref, kp_ref, vc_ref, vp_ref, gq_ref, gk_ref, sk_ref, bias_ref, o_ref):
        n = pl.program_id(1)
        lo = _iota((BLOCK, LANES), 1) < HEAD
        lo2 = _iota((2 * BLOCK, LANES), 1) < HEAD
        kband = jnp.concatenate([kp_ref[...], kc_ref[...]], axis=0).astype(F32)
        vband = jnp.concatenate([vp_ref[...], vc_ref[...]], axis=0).astype(F32)
        kn = kband * _half_stats(kband, lo2) * gk_ref[...]
        ks = (kn.astype(MXU), pltpu.roll(kn, HEAD, 1).astype(MXU))
        vs = (vband.astype(MXU), pltpu.roll(vband, HEAD, 1).astype(MXU))
        valid = _swa_valid(n)
        for blk in range(SW_HEADS // 2):
            qv = q_ref[:, blk * LANES:(blk + 1) * LANES].astype(F32)
            qn = qv * _half_stats(qv, lo) * gq_ref[...]
            outs = []
            for half in range(2):
                h = 2 * blk + half
                swap = 0 if half == h // 3 else 1
                qm = jnp.where(lo if half == 0 else ~lo, qn, 0.0).astype(MXU)
                sc = jnp.where(valid, _dot_nt(qm, ks[swap]) * SW_SCALE + bias_ref[h], NEG)
                sk = jnp.max(sk_ref[h:h + 1, :], axis=-1, keepdims=True)
                m = jnp.maximum(jnp.max(sc, axis=-1, keepdims=True), sk)
                p = jnp.exp(sc - m)
                l = _rsum(p) + jnp.exp(sk - m)
                outs.append(_dot((p / l).astype(MXU), vs[swap]))
            o_ref[:, blk * LANES:(blk + 1) * LANES] = jnp.where(lo, outs[0], outs[1])

    return pl.pallas_call(
        body, name=name, grid=(nb, s // BLOCK), in_specs=_swa_in_specs(),
        out_specs=pl.BlockSpec((None, BLOCK, 384), lambda b, n: (b, n, 0)),
        out_shape=jax.ShapeDtypeStruct((nb, s, 384), F32),
        compiler_params=_cp("parallel", "parallel"))(proj, proj, proj, proj, proj, gq, gk, sinks, bias)


def _swa_bwd(proj, gq, gk, sinks, bias, do, name):
    nb, s, _ = proj.shape

    def body(q_ref, kc_ref, kp_ref, vc_ref, vp_ref, gq_ref, gk_ref, sk_ref, bias_ref, do_ref,
             dq_ref, dkc_ref, dkp_ref, dvc_ref, dvp_ref, dbias_ref, dsk_ref, dgq_ref, dgk_ref):
        n = pl.program_id(1)

        @pl.when((pl.program_id(0) == 0) & (n == 0))
        def _():
            for r in (dbias_ref, dsk_ref, dgq_ref, dgk_ref):
                r[...] = jnp.zeros_like(r)

        lo = _iota((BLOCK, LANES), 1) < HEAD
        lo2 = _iota((2 * BLOCK, LANES), 1) < HEAD
        kband = jnp.concatenate([kp_ref[...], kc_ref[...]], axis=0).astype(F32)
        vband = jnp.concatenate([vp_ref[...], vc_ref[...]], axis=0).astype(F32)
        rk = _half_stats(kband, lo2)
        khat = kband * rk
        gkv = gk_ref[...]
        kn = khat * gkv
        ks = (kn.astype(MXU), pltpu.roll(kn, HEAD, 1).astype(MXU))
        vs = (vband.astype(MXU), pltpu.roll(vband, HEAD, 1).astype(MXU))
        valid = _swa_valid(n)
        dkn = jnp.zeros((2 * BLOCK, LANES), F32)
        dvb = jnp.zeros((2 * BLOCK, LANES), F32)
        gqv = gq_ref[...]
        dgq = jnp.zeros((1, LANES), F32)
        for blk in range(SW_HEADS // 2):
            bs = slice(blk * LANES, (blk + 1) * LANES)
            qv = q_ref[:, bs].astype(F32)
            rq = _half_stats(qv, lo)
            qhat = qv * rq
            qn = qhat * gqv
            dov = do_ref[:, bs]
            dqn = jnp.zeros((BLOCK, LANES), F32)
            for half in range(2):
                h = 2 * blk + half
                swap = 0 if half == h // 3 else 1
                hm = lo if half == 0 else ~lo
                qm = jnp.where(hm, qn, 0.0).astype(MXU)
                dom = jnp.where(hm, dov, 0.0).astype(MXU)
                sc = jnp.where(valid, _dot_nt(qm, ks[swap]) * SW_SCALE + bias_ref[h], NEG)
                sk = jnp.max(sk_ref[h:h + 1, :], axis=-1, keepdims=True)
                m = jnp.maximum(jnp.max(sc, axis=-1, keepdims=True), sk)
                e = jnp.exp(sc - m)
                es = jnp.exp(sk - m)
                l = _rsum(e) + es
                p = e / l
                dp = _dot_nt(dom, vs[swap])
                delta = _rsum(p * dp)
                ds = p * (dp - delta)
                dsk_ref[h:h + 1, :] += jnp.broadcast_to(_csum(-(es / l) * delta), (1, LANES))
                dbias_ref[h] += ds
                dsb = (ds * SW_SCALE).astype(MXU)
                dqn = dqn + jnp.where(hm, _dot(dsb, ks[swap]), 0.0)
                rk_ = _dot_tn(dsb, qm)
                rv_ = _dot_tn(p.astype(MXU), dom)
                if swap:
                    rk_ = pltpu.roll(rk_, HEAD, 1)
                    rv_ = pltpu.roll(rv_, HEAD, 1)
                dkn = dkn + rk_
                dvb = dvb + rv_
            dgq = dgq + _csum(dqn * qhat)
            dyq = dqn * gqv
            dq_ref[:, bs] = rq * (dyq - qhat * _half_mean(dyq * qhat, lo))
        dgq_ref[...] += dgq
        dgk_ref[...] += _csum(dkn * khat)
        dyk = dkn * gkv
        dkb = rk * (dyk - khat * _half_mean(dyk * khat, lo2))
        dkp_ref[...] = dkb[:BLOCK]
        dkc_ref[...] = dkb[BLOCK:]
        dvp_ref[...] = dvb[:BLOCK]
        dvc_ref[...] = dvb[BLOCK:]

    full = lambda shape: pl.BlockSpec(shape, lambda b, n: tuple(0 for _ in shape))
    tile = pl.BlockSpec((None, BLOCK, LANES), lambda b, n: (b, n, 0))
    tile3 = pl.BlockSpec((None, BLOCK, 384), lambda b, n: (b, n, 0))
    kvs = jax.ShapeDtypeStruct((nb, s, LANES), F32)
    return pl.pallas_call(
        body, name=name, grid=(nb, s // BLOCK), in_specs=_swa_in_specs() + [tile3],
        out_specs=[tile3, tile, tile, tile, tile, full((SW_HEADS, BLOCK, 2 * BLOCK)), full((8, LANES)),
                   full((1, LANES)), full((1, LANES))],
        out_shape=[jax.ShapeDtypeStruct((nb, s, 384), F32), kvs, kvs, kvs, kvs,
                   jax.ShapeDtypeStruct((SW_HEADS, BLOCK, 2 * BLOCK), F32), jax.ShapeDtypeStruct((8, LANES), F32),
                   jax.ShapeDtypeStruct((1, LANES), F32), jax.ShapeDtypeStruct((1, LANES), F32)],
        compiler_params=_cp("arbitrary", "arbitrary"))(proj, proj, proj, proj, proj, gq, gk, sinks, bias, do)


def _bias_build(table, bucket, name):
    def body(tb_ref, bk_ref, o_ref):
        bk = bk_ref[...]
        tb = tb_ref[...]
        row = _iota((8, LANES), 0)
        col = _iota((8, LANES), 1)
        for h in range(SW_HEADS):
            acc = jnp.zeros((BLOCK, 2 * BLOCK), F32)
            for t in range(REL_BUCKETS):
                val = jnp.sum(jnp.where((row == h) & (col == t), tb, 0.0), keepdims=True)
                acc = jnp.where(bk == t, val, acc)
            o_ref[h] = acc

    return pl.pallas_call(
        body, name=name, out_shape=jax.ShapeDtypeStruct((SW_HEADS, BLOCK, 2 * BLOCK), F32))(table, bucket)


def _bias_grad(dbias, bucket, name):
    def body(db_ref, bk_ref, o_ref):
        bk = bk_ref[...]
        row = _iota((8, LANES), 0)
        col = _iota((8, LANES), 1)
        res = jnp.zeros((8, LANES), F32)
        for h in range(SW_HEADS):
            dbh = db_ref[h]
            for t in range(REL_BUCKETS):
                val = jnp.sum(jnp.where(bk == t, dbh, 0.0), keepdims=True)
                res = jnp.where((row == h) & (col == t), val, res)
        o_ref[...] = res

    return pl.pallas_call(body, name=name, out_shape=jax.ShapeDtypeStruct((8, LANES), F32))(dbias, bucket)


def _loss_grad(y, target, name):
    nb, s, d = y.shape
    tm = min(512, s)

    def body(y_ref, t_ref, loss_ref, dy_ref):
        @pl.when((pl.program_id(0) == 0) & (pl.program_id(1) == 0))
        def _():
            loss_ref[...] = jnp.zeros_like(loss_ref)

        e = y_ref[...] - t_ref[...]
        dy_ref[...] = e / d
        loss_ref[...] += 0.5 * jnp.sum(_rsum(e * e) / d, keepdims=True)

    tile = pl.BlockSpec((None, tm, d), lambda b, i: (b, i, 0))
    return pl.pallas_call(
        body, name=name, grid=(nb, s // tm), in_specs=[tile, tile],
        out_specs=[pl.BlockSpec((8, LANES), lambda b, i: (0, 0)), tile],
        out_shape=[jax.ShapeDtypeStruct((8, LANES), F32), jax.ShapeDtypeStruct((nb, s, d), F32)],
        compiler_params=_cp("arbitrary", "arbitrary"))(y, target)


def _adamw(parts, w, m, v, name):
    npart, r, ncol = parts.shape
    tr = _row_tile(r, ncol)
    bc1 = 1.0 - ADAM_B1 ** ADAM_STEP
    bc2 = 1.0 - ADAM_B2 ** ADAM_STEP

    def body(p_ref, w_ref, m_ref, v_ref, g_ref, d_ref, nm_ref, nv_ref):
        g = p_ref[0].astype(F32)
        for k in range(1, npart):
            g = g + p_ref[k].astype(F32)
        mn = ADAM_B1 * m_ref[...] + (1.0 - ADAM_B1) * g
        vn = ADAM_B2 * v_ref[...] + (1.0 - ADAM_B2) * (g * g)
        g_ref[...] = g
        nm_ref[...] = mn
        nv_ref[...] = vn
        d_ref[...] = -ADAM_LR * ((mn / bc1) / (jnp.sqrt(vn / bc2) + ADAM_EPS) + ADAM_WD * w_ref[...])

    tile = pl.BlockSpec((tr, ncol), lambda i: (i, 0))
    return pl.pallas_call(
        body, name=name, grid=(r // tr,),
        in_specs=[pl.BlockSpec((npart, tr, ncol), lambda i: (0, i, 0)), tile, tile, tile],
        out_specs=[tile] * 4, out_shape=[jax.ShapeDtypeStruct((r, ncol), F32)] * 4,
        compiler_params=_cp("parallel"))(parts, w, m, v)


def _unpack(flat, shapes, lead=()):
    out, off = [], 0
    for shp in shapes:
        size = 1
        for dim in shp:
            size *= dim
        out.append(flat[..., off:off + size].reshape(lead + tuple(shp)))
        off += size
    return out


def _t5_bucket():
    a = jnp.arange(BLOCK)[:, None]
    b = jnp.arange(2 * BLOCK)[None, :]
    dist = BLOCK + a - b
    max_exact = REL_BUCKETS // 2
    nn = jnp.maximum(dist, 0)
    nf = jnp.maximum(nn, 1).astype(F32)
    large = max_exact + (jnp.log(nf / max_exact) / math.log(BLOCK / max_exact)
                         * (REL_BUCKETS - max_exact)).astype(jnp.int32)
    large = jnp.minimum(large, REL_BUCKETS - 1)
    return jnp.where(nn < max_exact, nn, large).astype(jnp.int32)


def _pad_lanes(g, n):
    return jnp.pad(g, (0, n - g.shape[0])).reshape(1, n)


def kernel(x, c, positions, rel_table, norm1_g, norm2_g, w_ada, b_ada, w_in, mla_cq_g, w_uq, mla_ckv_g, w_ukv, mla_qn_g, mla_kn_g, sw_qn_g, sw_kn_g, sw_sinks, w_out, w_up, conv_w, conv_b, w_down, loss_target, m_rel_table, m_norm1_g, m_norm2_g, m_w_ada, m_b_ada, m_w_in, m_mla_cq_g, m_w_uq, m_mla_ckv_g, m_w_ukv, m_mla_qn_g, m_mla_kn_g, m_sw_qn_g, m_sw_kn_g, m_sw_sinks, m_w_out, m_w_up, m_conv_w, m_conv_b, m_w_down, v_rel_table, v_norm1_g, v_norm2_g, v_w_ada, v_b_ada, v_w_in, v_mla_cq_g, v_w_uq, v_mla_ckv_g, v_w_ukv, v_mla_qn_g, v_mla_kn_g, v_sw_qn_g, v_sw_kn_g, v_sw_sinks, v_w_out, v_w_up, v_conv_w, v_conv_b, v_w_down):
    nb, s, d = x.shape
    nl = norm1_g.shape[0]
    me = 4 * lax.axis_index("x") + 2 * lax.axis_index("y") + lax.axis_index("c")
    n_ada = w_ada.shape[2]

    shard = lambda w, l, transposed: (jnp.swapaxes(w[l], 0, 1) if transposed else w[l]).astype(MXU)
    attn_local = lambda l: [shard(w_in, l, True), shard(w_uq, l, True), shard(w_ukv, l, True), shard(w_out, l, False)]
    ffn_local = lambda l: [shard(w_up, l, True), shard(w_down, l, False)]
    full = lambda a: a.reshape(-1, a.shape[-1])
    zrows = lambda n: jnp.zeros((n, d), MXU)
    pad_in = lambda wt: jnp.concatenate([wt[:1152], wt[1184:1824], zrows(64), wt[1152:1184], zrows(160)], axis=0)
    pad_uq = lambda wt: jnp.pad(wt.reshape(MLA_HEADS, MLA_QK, 256), ((0, 0), (0, LANES - MLA_QK), (0, 0))).reshape(768, 256)
    got = _all_gather(attn_local(0) + [conv_w.reshape(-1, conv_w.shape[-1]), c], "gather_inputs")
    w_in_pt, w_uq_pt, w_ukv_t, w_out_f = [pad_in(full(got[0]))], [pad_uq(full(got[1]))], [full(got[2])], [full(got[3])]
    w_up_t, w_down_f = [], []
    conv_full = got[4].reshape(N_DEV, nl, 3, -1).transpose(1, 2, 0, 3).reshape(nl, 3, -1)
    c_all = got[5].reshape(N_DEV * nb, d)

    b_my = lax.dynamic_slice_in_dim(b_ada, me * n_ada, n_ada, axis=1).reshape(nl, 1, n_ada)
    mods_my = _ada_fwd(c_all, w_ada, b_my, "ada_fwd")
    mods, = _all_gather([mods_my.reshape(nl * N_DEV * nb, n_ada)], "gather_mods")
    mods = mods.reshape(N_DEV, nl, N_DEV * nb, n_ada).transpose(1, 2, 0, 3).reshape(nl, N_DEV * nb, N_DEV * n_ada)
    mods = lax.dynamic_slice_in_dim(mods, me * nb, nb, axis=1)
    shift1, scale1, gate1, shift2, scale2, gate2 = [mods[:, :, k * d:(k + 1) * d].reshape(nl, nb, 1, d) for k in range(6)]

    half = 16
    inv_freq = jnp.power(ROPE_THETA, -jnp.arange(half, dtype=F32) / half)
    ang = positions.astype(F32)[..., None] * inv_freq
    ones = lambda n: jnp.ones((nb, s, n), F32)
    zeros = lambda n: jnp.zeros((nb, s, n), F32)
    rope_c = jnp.concatenate([ones(64), jnp.cos(ang), jnp.cos(ang), ones(32)], axis=-1)
    rope_s = jnp.concatenate([zeros(64), jnp.sin(ang), jnp.sin(ang), zeros(32)], axis=-1)
    bucket = _t5_bucket()
    bias = _bias_build(jnp.pad(rel_table.T, ((0, 8 - SW_HEADS), (0, LANES - REL_BUCKETS))), bucket, "rel_bias")

    row = lambda g: g.reshape(1, -1)
    twice = lambda g: jnp.concatenate([g, g]).reshape(1, LANES)

    saved = []
    xl = x
    for l in range(nl):
        proj, h1 = _ln_mod_matmul(xl, row(norm1_g[l]), scale1[l], shift1[l], w_in_pt[l], f"l{l}_in_proj")
        prep_args = (proj, rope_c, rope_s, row(mla_cq_g[l]), row(mla_ckv_g[l]), _pad_lanes(mla_qn_g[l], LANES),
                     _pad_lanes(mla_kn_g[l], LANES), w_uq_pt[l], w_ukv_t[l])
        qm, km, vm = _mla_prep(*prep_args, f"l{l}_mla_prep")
        o_a, ct_a, cnt_a, up_g, down_g = _sb_fwd(proj, f"l{l}_sb_fwd", riding=ffn_local(l))
        w_up_t.append(full(up_g))
        w_down_f.append(full(down_g))
        o_b, lse_b, *nxt = _mla_fwd(qm, km, vm, f"l{l}_mla_fwd", riding=attn_local(l + 1) if l + 1 < nl else ())
        if nxt:
            w_in_pt.append(pad_in(full(nxt[0])))
            w_uq_pt.append(pad_uq(full(nxt[1])))
            w_ukv_t.append(full(nxt[2]))
            w_out_f.append(full(nxt[3]))
        sinks = jnp.broadcast_to(jnp.pad(sw_sinks[l], (0, 2))[:, None], (8, LANES))
        swa_args = (proj, twice(sw_qn_g[l]), twice(sw_kn_g[l]), sinks, bias)
        o_c = _swa_fwd(*swa_args, f"l{l}_swa_fwd")
        wo = [w_out_f[l][:256], w_out_f[l][256:640], w_out_f[l][640:]]
        x_mid, y1 = _out_proj([o_a, o_b, o_c], wo, gate1[l], xl, f"l{l}_out_proj")
        u_pre, h2 = _ln_mod_matmul(x_mid, row(norm2_g[l]), scale2[l], shift2[l], w_up_t[l], f"l{l}_up_proj")
        x_out, y2 = _conv_gate_matmul(u_pre, conv_full[l], row(conv_b[l]), w_down_f[l], gate2[l], x_mid, f"l{l}_ffn_down")
        saved.append(dict(x=xl, proj=proj, h1=h1, prep=prep_args, qkv=(qm, km, vm), o_a=o_a, ct_a=ct_a, cnt_a=cnt_a, o_b=o_b, lse_b=lse_b,
                          swa=swa_args, o_c=o_c, wo=wo, y1=y1, x_mid=x_mid, u_pre=u_pre, h2=h2, y2=y2))
        xl = x_out

    loss_blk, dx = _loss_grad(xl, loss_target, "loss")
    loss = lax.psum(loss_blk[0, 0], ("x", "y", "c"))

    t = nb * s
    flat = lambda a: a.reshape(t, a.shape[-1])
    grads = [None] * nl
    dmods = [None] * nl
    sharded_out = [None] * nl
    sharded_names = ["w_in", "w_uq", "w_ukv", "w_up", "w_out", "w_down", "conv_w"]
    sharded_wmv = dict(w_in=(w_in, m_w_in, v_w_in), w_uq=(w_uq, m_w_uq, v_w_uq), w_ukv=(w_ukv, m_w_ukv, v_w_ukv),
                       w_up=(w_up, m_w_up, v_w_up), w_out=(w_out, m_w_out, v_w_out), w_down=(w_down, m_w_down, v_w_down),
                       conv_w=(conv_w, m_conv_w, v_conv_w))
    n_in, n_up, n_out, n_dn = w_in.shape[2], w_up.shape[2], w_out.shape[1], w_down.shape[1]
    small_sizes = [w_uq[0].size, w_ukv[0].size, conv_w[0].size]
    n_small_rows = -(-sum(small_sizes) // d)
    rows_used = n_in + n_out + n_small_rows
    rows_grad = -(-rows_used // 16) * 16

    def pack_rows(mats, vecs):
        lead = mats[0].shape[:-2]
        flat_part = jnp.concatenate(vecs, axis=-1)
        flat_part = jnp.pad(flat_part, [(0, 0)] * len(lead) + [(0, n_small_rows * d - flat_part.shape[-1])])
        tail = jnp.zeros(lead + (rows_grad - rows_used, d), F32)
        return jnp.concatenate(list(mats) + [flat_part.reshape(lead + (n_small_rows, d)), tail], axis=-2)

    def unpack_rows(a):
        o1, o2 = n_in, n_in + n_out
        flat_part = a[o2:o2 + n_small_rows].reshape(-1)
        s1, s2, s3 = small_sizes[0], small_sizes[0] + small_sizes[1], sum(small_sizes)
        return dict(w_in=a[:o1].T, w_out=a[o1:o2],
                    w_uq=flat_part[:s1].reshape(w_uq.shape[2], -1).T, w_ukv=flat_part[s1:s2].reshape(w_ukv.shape[2], -1).T,
                    conv_w=flat_part[s2:s3].reshape(conv_w.shape[1:]))

    ffn_out = [None] * nl
    core = lax.axis_index("c").reshape(1).astype(jnp.int32)

    def update_ffn(l, recv):
        wmv = [{k: v[o][l] for k, v in sharded_wmv.items()} for o in range(3)]
        res_up = _adamw(recv[0], *[a["w_up"].T for a in wmv], f"l{l}_adamw_up")
        res_dn = _adamw(recv[1], *[a["w_down"] for a in wmv], f"l{l}_adamw_down")
        ffn_out[l] = [dict(w_up=ru.T, w_down=rd) for ru, rd in zip(res_up, res_dn)]

    def update_rest(l, recv):
        wmv = [{k: v[o][l] for k, v in sharded_wmv.items()} for o in range(3)]
        res_rest = _adamw(recv, *[pack_rows([a["w_in"].T, a["w_out"]], [a["w_uq"].T.reshape(-1), a["w_ukv"].T.reshape(-1),
                                                                         a["conv_w"].reshape(-1)]) for a in wmv],
                          f"l{l}_adamw_rest")
        sharded_out[l] = [dict(unpack_rows(rr), **ff) for rr, ff in zip(res_rest, ffn_out[l])]

    pending = None
    dbias = jnp.zeros((SW_HEADS, BLOCK, 2 * BLOCK), F32)
    for l in reversed(range(nl)):
        sv = saved[l]
        (da,), dy2, dgate2 = _gate_bwd_nt(dx, sv["y2"], gate2[l], [w_down_f[l]], f"l{l}_ffn_down_bwd")
        du, a_act, cstats = _conv_gate_bwd(da, sv["u_pre"], conv_full[l], row(conv_b[l]), f"l{l}_conv_gate_bwd")
        res = _ln_mod_matmul_bwd(du, w_up_t[l], sv["x_mid"], row(norm2_g[l]), scale2[l], dx, conv_full[l],
                                 f"l{l}_up_proj_bwd", riding=[pending[1]] if pending else ())
        dx_mid, du_pre, dshift2, dscale2, dg2 = res[:5]
        if pending:
            update_rest(pending[0], res[5])
            pending = None
        g_w_down = _wgrad(flat(a_act), flat(dy2), f"l{l}_w_down_grad")
        g_w_up_t = _wgrad(flat(du_pre), flat(sv["h2"]), f"l{l}_w_up_grad")
        per_dev = lambda g: g.reshape(N_DEV, -1, d)
        ffn_send = [per_dev(g_w_up_t), per_dev(g_w_down)]

        (do_a, do_b, do_c), dy1, dgate1 = _gate_bwd_nt(dx_mid, sv["y1"], gate1[l], sv["wo"], f"l{l}_out_proj_bwd")
        mix = jnp.concatenate([sv["o_a"], sv["o_b"], sv["o_c"]], axis=-1).astype(MXU)
        g_w_out = _wgrad(flat(mix), flat(dy1), f"l{l}_w_out_grad")

        dsb_q, dsb_k, dsb_v, *ffn_sib = _sb_bwd(sv["proj"], sv["ct_a"], sv["cnt_a"], do_a, f"l{l}_sb_bwd", riding=ffn_send)
        ffn_pair = [_pair_add(core, a, b, f"l{l}_pair_add_{k}") for a, b, k in zip(ffn_send, ffn_sib, ("up", "down"))]
        qm, km, vm = sv["qkv"]
        dqm, dkm, dvm, *ffn_recv = _mla_bwd(qm, km, vm, sv["o_b"], sv["lse_b"], do_b, f"l{l}_mla_bwd", riding=ffn_pair)
        update_ffn(l, ffn_recv)
        dsw_q, dkc, dkp, dvc, dvp, dbias_l, dsinks, dg_swq, dg_swk = _swa_bwd(*sv["swa"], do_c, f"l{l}_swa_bwd")
        dbias = dbias + dbias_l
        shift_up = lambda a: jnp.concatenate([a[:, BLOCK:], jnp.zeros((nb, BLOCK, LANES), F32)], axis=1)
        dsw_k = dkc + shift_up(dkp)
        dsw_v = dvc + shift_up(dvp)
        dcq, dckv, dkr, g_w_uq_pt, g_w_ukv_t, dg_cq, dg_ckv, dg_qn, dg_kn = _mla_prep_bwd(
            *sv["prep"], dqm, dkm, dvm, f"l{l}_mla_prep_bwd")
        dproj = jnp.concatenate([dsb_q, dsb_k, dsb_v, dcq, dckv, dsw_q, dsw_k, dsw_v, dkr, zeros(128)], axis=-1).astype(MXU)
        dx, dproj_m, dshift1, dscale1, dg1 = _ln_mod_matmul_bwd(
            dproj, w_in_pt[l], sv["x"], row(norm1_g[l]), scale1[l], dx_mid, None, f"l{l}_in_proj_bwd")
        g_w_in_pt = _wgrad(flat(dproj_m), flat(sv["h1"]), f"l{l}_w_in_grad")

        g_w_in_t = jnp.concatenate([g_w_in_pt[:1152], g_w_in_pt[1856:1888], g_w_in_pt[1152:1792]], axis=0)
        g_w_uq_t = g_w_uq_pt.reshape(MLA_HEADS, LANES, 256)[:, :MLA_QK].reshape(MLA_HEADS * MLA_QK, 256)
        dmods[l] = jnp.concatenate([dshift1, dscale1, dgate1, dshift2, dscale2, dgate2], axis=-1).reshape(nb, 6 * d)

        conv_dev = cstats[1:4].reshape(3, N_DEV, -1).transpose(1, 0, 2)
        rest = pack_rows([per_dev(g_w_in_t), per_dev(g_w_out)],
                         [g_w_uq_t.reshape(N_DEV, -1), g_w_ukv_t.reshape(N_DEV, -1), conv_dev.reshape(N_DEV, -1)])
        rest_sib, = _pair_exchange([rest], f"l{l}_pair_exchange_rest")
        rest_pair = _pair_add(core, rest, rest_sib, f"l{l}_pair_add_rest")
        if l > 0:
            pending = (l, rest_pair)
        else:
            update_rest(l, _chip_exchange([rest_pair], f"l{l}_chip_exchange")[0])
        grads[l] = dict(
            norm1_g=dg1[0], norm2_g=dg2[0], mla_cq_g=dg_cq[0], mla_ckv_g=dg_ckv[0], mla_qn_g=dg_qn[0, :MLA_QK],
            mla_kn_g=dg_kn[0, :MLA_QK], sw_qn_g=dg_swq[0, :HEAD] + dg_swq[0, HEAD:], sw_kn_g=dg_swk[0, :HEAD] + dg_swk[0, HEAD:],
            sw_sinks=dsinks[:SW_HEADS, 0], conv_b=cstats[0])
    grad_x = dx
    g_rel = _bias_grad(dbias, bucket, "rel_table_grad")[:SW_HEADS, :REL_BUCKETS].T
    stack = lambda k: jnp.stack([grads[l][k] for l in range(nl)])

    dm_all, = _all_gather([jnp.stack(dmods).reshape(nl * nb, 6 * d)], "gather_dmods")
    dm_all = dm_all.reshape(N_DEV, nl, nb, 6 * d).transpose(1, 0, 2, 3).reshape(nl, N_DEV * nb, 6 * d)
    dm_my = lax.dynamic_slice_in_dim(dm_all, me * n_ada, n_ada, axis=2)
    g_w_ada, g_b_ada = _ada_bwd(c_all, dm_my, dm_all, "ada_bwd")
    g_b_ada = g_b_ada.reshape(nl, 6 * d)

    big_out = [{k: jnp.stack([sharded_out[l][o][k] for l in range(nl)]) for k in sharded_names} for o in range(4)]
    packf = lambda dct, names, rows: jnp.pad(jnp.concatenate([dct[k].reshape(-1) for k in names]),
                                             (0, rows * LANES - sum(dct[k].size for k in names))).reshape(rows, LANES)

    small_names = ["rel_table", "norm1_g", "norm2_g", "mla_cq_g", "mla_ckv_g", "mla_qn_g", "mla_kn_g",
                   "sw_qn_g", "sw_kn_g", "sw_sinks", "conv_b"]
    small_w = dict(rel_table=rel_table, norm1_g=norm1_g, norm2_g=norm2_g, mla_cq_g=mla_cq_g, mla_ckv_g=mla_ckv_g,
                   mla_qn_g=mla_qn_g, mla_kn_g=mla_kn_g, sw_qn_g=sw_qn_g, sw_kn_g=sw_kn_g, sw_sinks=sw_sinks, conv_b=conv_b)
    small_m = dict(rel_table=m_rel_table, norm1_g=m_norm1_g, norm2_g=m_norm2_g, mla_cq_g=m_mla_cq_g, mla_ckv_g=m_mla_ckv_g,
                   mla_qn_g=m_mla_qn_g, mla_kn_g=m_mla_kn_g, sw_qn_g=m_sw_qn_g, sw_kn_g=m_sw_kn_g, sw_sinks=m_sw_sinks, conv_b=m_conv_b)
    small_v = dict(rel_table=v_rel_table, norm1_g=v_norm1_g, norm2_g=v_norm2_g, mla_cq_g=v_mla_cq_g, mla_ckv_g=v_mla_ckv_g,
                   mla_qn_g=v_mla_qn_g, mla_kn_g=v_mla_kn_g, sw_qn_g=v_sw_qn_g, sw_kn_g=v_sw_kn_g, sw_sinks=v_sw_sinks, conv_b=v_conv_b)
    small_g = {k: (g_rel if k == "rel_table" else stack(k)) for k in small_names}
    n_small = sum(small_w[k].size for k in small_names)
    rows_small = -(-n_small // (8 * LANES)) * 8
    small_parts, = _all_gather([packf(small_g, small_names, rows_small)], "gather_small_grads")
    small_out = _adamw(small_parts, packf(small_w, small_names, rows_small), packf(small_m, small_names, rows_small),
                       packf(small_v, small_names, rows_small), "adamw_replicated")
    small_out = [dict(zip(small_names, _unpack(o.reshape(-1), [small_w[k].shape for k in small_names]))) for o in small_out]

    two_d = lambda a: a.reshape(-1, a.shape[-1])
    res_w = _adamw(two_d(g_w_ada)[None], two_d(w_ada), two_d(m_w_ada), two_d(v_w_ada), "adamw_w_ada")
    res_b = _adamw(g_b_ada[None], b_ada, m_b_ada, v_b_ada, "adamw_b_ada")
    ada_out = [dict(w_ada=rw.reshape(w_ada.shape), b_ada=rb) for rw, rb in zip(res_w, res_b)]

    order = ["rel_table", "norm1_g", "norm2_g", "w_ada", "b_ada", "w_in", "mla_cq_g", "w_uq", "mla_ckv_g", "w_ukv",
             "mla_qn_g", "mla_kn_g", "sw_qn_g", "sw_kn_g", "sw_sinks", "w_out", "w_up", "conv_w", "conv_b", "w_down"]
    outs = [{**big_out[k], **small_out[k], **ada_out[k]} for k in range(4)]
    return (loss, grad_x, *[outs[0][n] for n in order], *[outs[1][n] for n in order],
            *[outs[2][n] for n in order], *[outs[3][n] for n in order])
```

```python
import math

import jax
import jax.numpy as jnp
from jax import lax
from jax.experimental import pallas as pl
from jax.experimental.pallas import tpu as pltpu

F32 = jnp.float32
MXU = jnp.bfloat16
EPS = 1e-6
NEG = -1e30
VMEM_LIMIT_BYTES = 56 * 1024 * 1024
N_DEV = 8
MESH = pl.DeviceIdType.MESH

D_MODEL = 1024
D_FF = 2816
HEAD = 64
LANES = 128
MLA_HEADS = 6
MLA_QK = 96
SW_HEADS = 6
REL_BUCKETS = 32
BLOCK = 128
SB_SCALE = HEAD ** -0.5
SB_DEAD = -105.0
SW_SCALE = HEAD ** -0.5
MLA_SCALE = MLA_QK ** -0.5
ROPE_THETA = 10000.0
D_IN_PAD = 2048
COL_SBQ, COL_SBK, COL_SBV, COL_CQ, COL_CKV, COL_SWQ, COL_SWK, COL_SWV, COL_KR = 0, 256, 512, 768, 1024, 1152, 1536, 1664, 1792

HALO = 16
ROW_TILE_BYTES = 1 << 21
ADAM_LR, ADAM_B1, ADAM_B2, ADAM_EPS, ADAM_WD, ADAM_STEP = 0.001, 0.9, 0.999, 1e-08, 0.01, 10


def _cp(*sem):
    return pltpu.CompilerParams(dimension_semantics=sem, vmem_limit_bytes=VMEM_LIMIT_BYTES)


def _iota(shape, dim):
    return lax.broadcasted_iota(jnp.int32, shape, dim)


def _dot(a, b):
    return jnp.dot(a, b, preferred_element_type=F32)


def _dot_nt(a, b):
    return lax.dot_general(a, b, (((1,), (1,)), ((), ())), preferred_element_type=F32)


def _dot_tn(a, b):
    return lax.dot_general(a, b, (((0,), (0,)), ((), ())), preferred_element_type=F32)


def _cumdot(x, u):
    hi = x.astype(MXU)
    mid = (x - hi.astype(F32)).astype(MXU)
    return _dot(hi, u) + _dot(mid, u)


def _sigmoid(x):
    return 1.0 / (1.0 + jnp.exp(-x))


def _rsum(x):
    return jnp.sum(x, axis=-1, keepdims=True)


def _csum(x):
    return jnp.sum(x, axis=0, keepdims=True)


def _all_gather(xs, name):
    na = len(xs)

    def body(*refs):
        start, finish = _gather_steps(refs[:na], refs[na:2 * na], *refs[2 * na:])
        start()
        finish()

    hbm = pl.BlockSpec(memory_space=pl.ANY)
    return pl.pallas_call(
        body, name=name, out_shape=_gather_out_shapes(xs), in_specs=[hbm] * na, out_specs=[hbm] * na,
        scratch_shapes=_gather_sems(na))(*xs)


def _gather_out_shapes(xs):
    return [jax.ShapeDtypeStruct((N_DEV,) + a.shape, a.dtype) for a in xs]


def _gather_sems(na):
    return [pltpu.SemaphoreType.DMA((7 * na,)), pltpu.SemaphoreType.DMA((7 * na,)), pltpu.SemaphoreType.DMA((na,))]


def _gather_steps(x_refs, out_refs, send_sems, recv_sems, local_sems):
    na = len(x_refs)
    x, y, c = lax.axis_index("x"), lax.axis_index("y"), lax.axis_index("c")
    me, sibling = (x, y, c), (x, y, 1 - c)
    chips = [(1 - x, y), (x, 1 - y), (1 - x, 1 - y)]

    def slot(a, px, py, pc):
        return out_refs[a].at[4 * px + 2 * py + pc]

    def copy(a, k, block, to, src=None):
        return pltpu.make_async_remote_copy(
            src_ref=slot(a, *block) if src is None else src, dst_ref=slot(a, *block),
            send_sem=send_sems.at[7 * a + k], recv_sem=recv_sems.at[7 * a + k], device_id=to, device_id_type=MESH)

    def own_copies(a):
        return ([copy(a, 0, me, sibling, src=x_refs[a])]
                + [copy(a, 1 + j, me, (*chip, c), src=x_refs[a]) for j, chip in enumerate(chips)])

    def local_copy(a):
        return pltpu.make_async_copy(x_refs[a], slot(a, *me), local_sems.at[a])

    def start():
        for a in range(na):
            local_copy(a).start()
            for cp in own_copies(a):
                cp.start()

    def finish():
        passed = []
        for j, chip in enumerate(chips):
            for a in range(na):
                copy(a, 1 + j, (*chip, c), me).wait_recv()
                passed.append(copy(a, 4 + j, (*chip, c), sibling))
                passed[-1].start()
        for a in range(na):
            copy(a, 0, sibling, me).wait_recv()
            for j, chip in enumerate(chips):
                copy(a, 4 + j, (*chip, 1 - c), me).wait_recv()
        for a in range(na):
            for cp in own_copies(a):
                cp.wait_send()
        for cp in passed:
            cp.wait_send()
        for a in range(na):
            local_copy(a).wait()

    return start, finish


def _with_gather(body, n_in, n_out, na, grid):
    if not na:
        return body

    def wrapped(*refs):
        ins, ride_in = refs[:n_in], refs[n_in:n_in + na]
        outs = refs[n_in + na:n_in + na + n_out]
        ride_out = refs[n_in + na + n_out:n_in + 2 * na + n_out]
        ids = [pl.program_id(k) for k in range(len(grid))]
        first, last = ids[0] == 0, ids[0] == grid[0] - 1
        for k in range(1, len(grid)):
            first, last = first & (ids[k] == 0), last & (ids[k] == grid[k] - 1)
        start, finish = _gather_steps(ride_in, ride_out, *refs[n_in + 2 * na + n_out:])
        pl.when(first)(start)
        body(*ins, *outs)
        pl.when(last)(finish)

    return wrapped


def _pair_exchange(xs, name):
    na = len(xs)

    def body(*refs):
        copies = _pair_exchange_copies(refs[:na], refs[na:2 * na], *refs[2 * na:])
        for cp in copies:
            cp.start()
        for cp in copies:
            cp.wait()

    hbm = pl.BlockSpec(memory_space=pl.ANY)
    return pl.pallas_call(
        body, name=name, out_shape=_pair_exchange_out_shapes(xs), in_specs=[hbm] * na, out_specs=[hbm] * na,
        scratch_shapes=_pair_exchange_sems(na))(*xs)


def _pair_exchange_out_shapes(xs):
    return [jax.ShapeDtypeStruct((4,) + a.shape[1:], a.dtype) for a in xs]


def _pair_exchange_sems(na):
    return [pltpu.SemaphoreType.DMA((4 * na,)), pltpu.SemaphoreType.DMA((4 * na,))]


def _pair_exchange_copies(x_refs, out_refs, send_sems, recv_sems):
    x, y, c = lax.axis_index("x"), lax.axis_index("y"), lax.axis_index("c")
    return [pltpu.make_async_remote_copy(
        src_ref=x_refs[a].at[2 * q + 1 - c], dst_ref=out_refs[a].at[q],
        send_sem=send_sems.at[4 * a + q], recv_sem=recv_sems.at[4 * a + q],
        device_id=(x, y, 1 - c), device_id_type=MESH) for a in range(len(x_refs)) for q in range(4)]


def _row_tile(r, ncol):
    if r * ncol * 4 <= ROW_TILE_BYTES:
        return r
    return max(t for t in range(16, r, 16) if r % t == 0 and t * ncol * 4 <= ROW_TILE_BYTES)


def _pair_add(core, xs, sib, name):
    _, r, ncol = xs.shape
    tr = _row_tile(r, ncol)

    def body(c_ref, x_ref, s_ref, o_ref):
        o_ref[...] = (x_ref[...] + s_ref[...]).astype(MXU)

    return pl.pallas_call(
        body, name=name,
        grid_spec=pltpu.PrefetchScalarGridSpec(
            num_scalar_prefetch=1, grid=(4, r // tr),
            in_specs=[pl.BlockSpec((None, tr, ncol), lambda q, i, c_ref: (2 * q + c_ref[0], i, 0)),
                      pl.BlockSpec((None, tr, ncol), lambda q, i, c_ref: (q, i, 0))],
            out_specs=pl.BlockSpec((None, tr, ncol), lambda q, i, c_ref: (q, i, 0))),
        out_shape=jax.ShapeDtypeStruct((4, r, ncol), MXU),
        compiler_params=_cp("parallel", "parallel"))(core, xs, sib)


def _chip_exchange(xs, name):
    na = len(xs)

    def body(*refs):
        copies = _chip_exchange_copies(refs[:na], refs[na:2 * na], *refs[2 * na:])
        for cp in copies:
            cp.start()
        for cp in copies:
            cp.wait()

    hbm = pl.BlockSpec(memory_space=pl.ANY)
    return pl.pallas_call(
        body, name=name, out_shape=[jax.ShapeDtypeStruct(a.shape, a.dtype) for a in xs],
        in_specs=[hbm] * na, out_specs=[hbm] * na, scratch_shapes=_chip_exchange_sems(na))(*xs)


def _with_chip_exchange(body, n_in, n_out, na, grid, make_copies=None):
    if not na:
        return body
    make_copies = make_copies or _chip_exchange_copies

    def wrapped(*refs):
        ins, ride_in = refs[:n_in], refs[n_in:n_in + na]
        outs = refs[n_in + na:n_in + na + n_out]
        ride_out = refs[n_in + na + n_out:n_in + 2 * na + n_out]
        ids = [pl.program_id(k) for k in range(len(grid))]
        first, last = ids[0] == 0, ids[0] == grid[0] - 1
        for k in range(1, len(grid)):
            first, last = first & (ids[k] == 0), last & (ids[k] == grid[k] - 1)
        copies = make_copies(ride_in, ride_out, *refs[n_in + 2 * na + n_out:])

        @pl.when(first)
        def _():
            for cp in copies:
                cp.start()

        body(*ins, *outs)

        @pl.when(last)
        def _():
            for cp in copies:
                cp.wait()

    return wrapped


def _chip_exchange_sems(na):
    return [pltpu.SemaphoreType.DMA((3 * na,)), pltpu.SemaphoreType.DMA((3 * na,)), pltpu.SemaphoreType.DMA((na,))]


def _chip_exchange_copies(x_refs, out_refs, send_sems, recv_sems, local_sems):
    x, y, c = lax.axis_index("x"), lax.axis_index("y"), lax.axis_index("c")
    me = 2 * x + y
    copies = [pltpu.make_async_copy(x_refs[a].at[me], out_refs[a].at[me], local_sems.at[a]) for a in range(len(x_refs))]
    for k, (dx, dy) in enumerate([(1, 0), (0, 1), (1, 1)]):
        px = 1 - x if dx else x
        py = 1 - y if dy else y
        for a in range(len(x_refs)):
            copies.append(pltpu.make_async_remote_copy(
                src_ref=x_refs[a].at[2 * px + py], dst_ref=out_refs[a].at[me],
                send_sem=send_sems.at[3 * a + k], recv_sem=recv_sems.at[3 * a + k],
                device_id=(px, py, c), device_id_type=MESH))
    return copies


def _ada_fwd(c_all, w_ada, b_my, name):
    nl, d, n = w_ada.shape
    nb = c_all.shape[0]

    def body(c_ref, w_ref, b_ref, o_ref):
        cv = c_ref[...]
        sc = (cv * _sigmoid(cv)).astype(MXU)
        o_ref[...] = _dot(sc, w_ref[...].astype(MXU)) + b_ref[...]

    return pl.pallas_call(
        body, name=name, grid=(nl,),
        in_specs=[pl.BlockSpec((nb, d), lambda l: (0, 0)),
                  pl.BlockSpec((None, d, n), lambda l: (l, 0, 0)),
                  pl.BlockSpec((None, 1, n), lambda l: (l, 0, 0))],
        out_specs=pl.BlockSpec((None, nb, n), lambda l: (l, 0, 0)),
        out_shape=jax.ShapeDtypeStruct((nl, nb, n), F32),
        compiler_params=_cp("parallel"))(c_all, w_ada, b_my)


def _ada_bwd(c_all, dmods_my, dmods_all, name):
    nl, nb, n = dmods_my.shape
    d = c_all.shape[1]
    nfull = dmods_all.shape[2]

    def body(c_ref, dm_ref, da_ref, dw_ref, db_ref):
        cv = c_ref[...]
        sc = (cv * _sigmoid(cv)).astype(MXU)
        dw_ref[...] = _dot_tn(sc, dm_ref[...].astype(MXU))
        db_ref[...] = _csum(da_ref[...])

    return pl.pallas_call(
        body, name=name, grid=(nl,),
        in_specs=[pl.BlockSpec((nb, d), lambda l: (0, 0)),
                  pl.BlockSpec((None, nb, n), lambda l: (l, 0, 0)),
                  pl.BlockSpec((None, nb, nfull), lambda l: (l, 0, 0))],
        out_specs=[pl.BlockSpec((None, d, n), lambda l: (l, 0, 0)),
                   pl.BlockSpec((None, 1, nfull), lambda l: (l, 0, 0))],
        out_shape=[jax.ShapeDtypeStruct((nl, d, n), F32), jax.ShapeDtypeStruct((nl, 1, nfull), F32)],
        compiler_params=_cp("parallel"))(c_all, dmods_my, dmods_all)


def _ln_mod_matmul(x, g, scale, shift, w, name):
    nb, s, d = x.shape
    n = w.shape[0]
    tm, tn = min(1024, s), (1408 if n % 1408 == 0 else 1024)

    def body(x_ref, g_ref, sc_ref, sh_ref, w_ref, y_ref, h_ref, h_s):
        @pl.when(pl.program_id(2) == 0)
        def _():
            xf = x_ref[...]
            rstd = lax.rsqrt(jnp.mean(xf * xf, axis=-1, keepdims=True) + EPS)
            hv = (xf * rstd * g_ref[...]) * (1.0 + sc_ref[...]) + sh_ref[...]
            h_s[...] = hv.astype(MXU)
            h_ref[...] = h_s[...]

        y_ref[...] = _dot_nt(h_s[...], w_ref[...]).astype(MXU)

    return pl.pallas_call(
        body, name=name, grid=(nb, s // tm, n // tn),
        in_specs=[pl.BlockSpec((None, tm, d), lambda b, i, j: (b, i, 0)),
                  pl.BlockSpec((1, d), lambda b, i, j: (0, 0)),
                  pl.BlockSpec((None, 1, d), lambda b, i, j: (b, 0, 0)),
                  pl.BlockSpec((None, 1, d), lambda b, i, j: (b, 0, 0)),
                  pl.BlockSpec((tn, d), lambda b, i, j: (j, 0))],
        out_specs=[pl.BlockSpec((None, tm, tn), lambda b, i, j: (b, i, j)),
                   pl.BlockSpec((None, tm, d), lambda b, i, j: (b, i, 0))],
        out_shape=[jax.ShapeDtypeStruct((nb, s, n), MXU), jax.ShapeDtypeStruct((nb, s, d), MXU)],
        scratch_shapes=[pltpu.VMEM((tm, d), MXU)],
        compiler_params=_cp("parallel", "parallel", "arbitrary"))(x, g, scale, shift, w)


def _ln_mod_matmul_bwd(dy, w, x, g, scale, dres, conv_w, name, riding=()):
    nb, s, n = dy.shape
    d = x.shape[-1]
    tm, tn = min(512, s), (1408 if n % 1408 == 0 else 1024)
    ni, nj = s // tm, n // tn
    hb = tm // HALO
    conv = conv_w is not None
    na = len(riding)

    def body(*refs):
        if conv:
            dy_ref, nx_ref, cw_ref = refs[:3]
            refs = refs[3:]
        else:
            dy_ref = refs[0]
            refs = refs[1:]
        w_ref, x_ref, g_ref, sc_ref, dr_ref = refs[:5]
        ride_in, refs = refs[5:5 + na], refs[5 + na:]
        dx_ref, dyp_ref, dsh_ref, dsc_ref, dg_ref = refs[:5]
        ride_out, refs = refs[5:5 + na], refs[5 + na:]
        acc = refs[0]
        b, i, j = pl.program_id(0), pl.program_id(1), pl.program_id(2)
        if na:
            copies = _chip_exchange_copies(ride_in, ride_out, *refs[1:])

            @pl.when((b == 0) & (i == 0) & (j == 0))
            def _():
                for cp in copies:
                    cp.start()

        @pl.when(j == 0)
        def _():
            acc[...] = jnp.zeros_like(acc)

        @pl.when((j == 0) & (i == 0))
        def _():
            dsh_ref[...] = jnp.zeros_like(dsh_ref)
            dsc_ref[...] = jnp.zeros_like(dsc_ref)

        @pl.when((j == 0) & (i == 0) & (b == 0))
        def _():
            dg_ref[...] = jnp.zeros_like(dg_ref)

        dv = dy_ref[...].astype(F32)
        if conv:
            rows = _iota((tm, 1), 0)
            nx = jnp.where(i == ni - 1, 0.0, nx_ref[...].astype(F32))
            n1 = jnp.where(rows == tm - 1, nx[0:1, :], pltpu.roll(dv, tm - 1, 0))
            n2 = jnp.where(rows == tm - 2, nx[0:1, :], jnp.where(rows == tm - 1, nx[1:2, :], pltpu.roll(dv, tm - 2, 0)))
            cw = cw_ref[...]
            dv = cw[2:3, :] * dv + cw[1:2, :] * n1 + cw[0:1, :] * n2
        dp = dv.astype(MXU)
        dyp_ref[...] = dp
        acc[...] += _dot(dp, w_ref[...])

        @pl.when(j == nj - 1)
        def _():
            dh = acc[...]
            xf = x_ref[...]
            rstd = lax.rsqrt(jnp.mean(xf * xf, axis=-1, keepdims=True) + EPS)
            xn = xf * rstd
            gg = g_ref[...]
            sc1 = 1.0 + sc_ref[...]
            dsh_ref[...] += _csum(dh)
            dsc_ref[...] += _csum(dh * xn * gg)
            dg_ref[...] += _csum(dh * xn * sc1)
            dn = dh * gg * sc1
            dx_ref[...] = dr_ref[...] + rstd * (dn - xn * jnp.mean(dn * xn, axis=-1, keepdims=True))

        if na:
            @pl.when((b == nb - 1) & (i == ni - 1) & (j == nj - 1))
            def _():
                for cp in copies:
                    cp.wait()

    hbm = pl.BlockSpec(memory_space=pl.ANY)
    in_specs = [pl.BlockSpec((None, tm, tn), lambda b, i, j: (b, i, j))]
    args = [dy]
    if conv:
        in_specs += [pl.BlockSpec((None, HALO, tn), lambda b, i, j: (b, jnp.minimum((i + 1) * hb, s // HALO - 1), j)),
                     pl.BlockSpec((3, tn), lambda b, i, j: (0, j))]
        args += [dy, conv_w]
    in_specs += [pl.BlockSpec((tn, d), lambda b, i, j: (j, 0)),
                 pl.BlockSpec((None, tm, d), lambda b, i, j: (b, i, 0)),
                 pl.BlockSpec((1, d), lambda b, i, j: (0, 0)),
                 pl.BlockSpec((None, 1, d), lambda b, i, j: (b, 0, 0)),
                 pl.BlockSpec((None, tm, d), lambda b, i, j: (b, i, 0))]
    in_specs += [hbm] * na
    args += [w, x, g, scale, dres, *riding]
    return pl.pallas_call(
        body, name=name, grid=(nb, ni, nj), in_specs=in_specs,
        out_specs=[pl.BlockSpec((None, tm, d), lambda b, i, j: (b, i, 0)),
                   pl.BlockSpec((None, tm, tn), lambda b, i, j: (b, i, j)),
                   pl.BlockSpec((None, 1, d), lambda b, i, j: (b, 0, 0)),
                   pl.BlockSpec((None, 1, d), lambda b, i, j: (b, 0, 0)),
                   pl.BlockSpec((1, d), lambda b, i, j: (0, 0))] + [hbm] * na,
        out_shape=[jax.ShapeDtypeStruct((nb, s, d), F32), jax.ShapeDtypeStruct((nb, s, n), MXU),
                   jax.ShapeDtypeStruct((nb, 1, d), F32), jax.ShapeDtypeStruct((nb, 1, d), F32),
                   jax.ShapeDtypeStruct((1, d), F32)] + [jax.ShapeDtypeStruct(a.shape, a.dtype) for a in riding],
        scratch_shapes=[pltpu.VMEM((tm, d), F32)] + (_chip_exchange_sems(na) if na else []),
        compiler_params=_cp("arbitrary", "arbitrary", "arbitrary"))(*args)


def _wgrad(xm, dym, name):
    t, k = xm.shape
    n = dym.shape[1]
    tk = 1408 if k % 1408 == 0 else 1024
    tt = min(1024, t)

    def body(x_ref, dy_ref, o_ref):
        @pl.when(pl.program_id(1) == 0)
        def _():
            o_ref[...] = jnp.zeros_like(o_ref)

        o_ref[...] += _dot_tn(x_ref[...], dy_ref[...])

    return pl.pallas_call(
        body, name=name, grid=(k // tk, t // tt),
        in_specs=[pl.BlockSpec((tt, tk), lambda a, c: (c, a)),
                  pl.BlockSpec((tt, n), lambda a, c: (c, 0))],
        out_specs=pl.BlockSpec((tk, n), lambda a, c: (a, 0)),
        out_shape=jax.ShapeDtypeStruct((k, n), F32),
        compiler_params=_cp("parallel", "arbitrary"))(xm, dym)


def _out_proj(parts, ws, gate, res, name):
    nb, s, d = res.shape
    tm = min(512, s)
    npart = len(parts)

    def body(*refs):
        p_refs, w_refs = refs[:npart], refs[npart:2 * npart]
        gt_ref, res_ref, xo_ref, y_ref = refs[2 * npart:]
        y = _dot(p_refs[0][...].astype(MXU), w_refs[0][...])
        for p_ref, w_ref in zip(p_refs[1:], w_refs[1:]):
            y = y + _dot(p_ref[...].astype(MXU), w_ref[...])
        y_ref[...] = y
        xo_ref[...] = res_ref[...] + gt_ref[...] * y

    in_specs = [pl.BlockSpec((None, tm, p.shape[-1]), lambda b, i: (b, i, 0)) for p in parts]
    in_specs += [pl.BlockSpec(w.shape, lambda b, i: (0, 0)) for w in ws]
    in_specs += [pl.BlockSpec((None, 1, d), lambda b, i: (b, 0, 0)),
                 pl.BlockSpec((None, tm, d), lambda b, i: (b, i, 0))]
    return pl.pallas_call(
        body, name=name, grid=(nb, s // tm), in_specs=in_specs,
        out_specs=[pl.BlockSpec((None, tm, d), lambda b, i: (b, i, 0))] * 2,
        out_shape=[jax.ShapeDtypeStruct((nb, s, d), F32)] * 2,
        compiler_params=_cp("parallel", "parallel"))(*parts, *ws, gate, res)


def _gate_bwd_nt(dx, y, gate, ws, name):
    nb, s, d = dx.shape
    tm = min(512, s)
    npart = len(ws)

    def body(*refs):
        dx_ref, y_ref, gt_ref = refs[:3]
        w_refs = refs[3:3 + npart]
        da_refs = refs[3 + npart:3 + 2 * npart]
        dy_ref, dgt_ref = refs[3 + 2 * npart:]

        @pl.when(pl.program_id(1) == 0)
        def _():
            dgt_ref[...] = jnp.zeros_like(dgt_ref)

        dxv = dx_ref[...]
        dyv = (dxv * gt_ref[...]).astype(MXU)
        dy_ref[...] = dyv
        dgt_ref[...] += _csum(dxv * y_ref[...])
        for w_ref, da_ref in zip(w_refs, da_refs):
            da_ref[...] = _dot_nt(dyv, w_ref[...])

    tile = pl.BlockSpec((None, tm, d), lambda b, i: (b, i, 0))
    row = pl.BlockSpec((None, 1, d), lambda b, i: (b, 0, 0))
    outs = pl.pallas_call(
        body, name=name, grid=(nb, s // tm),
        in_specs=[tile, tile, row] + [pl.BlockSpec(w.shape, lambda b, i: (0, 0)) for w in ws],
        out_specs=[pl.BlockSpec((None, tm, w.shape[0]), lambda b, i: (b, i, 0)) for w in ws] + [tile, row],
        out_shape=[jax.ShapeDtypeStruct((nb, s, w.shape[0]), F32) for w in ws]
        + [jax.ShapeDtypeStruct((nb, s, d), MXU), jax.ShapeDtypeStruct((nb, 1, d), F32)],
        compiler_params=_cp("arbitrary", "arbitrary"))(dx, y, gate, *ws)
    return outs[:npart], outs[npart], outs[npart + 1]


def _conv_shifts(xv, halo, rows):
    last, before = halo[HALO - 1:HALO, :], halo[HALO - 2:HALO - 1, :]
    p1 = jnp.where(rows == 0, last, pltpu.roll(xv, 1, 0))
    p2 = jnp.where(rows == 0, before, jnp.where(rows == 1, last, pltpu.roll(xv, 2, 0)))
    return p1, p2


def _conv_gate_matmul(u, cw, cb, wd, gate, res, name):
    nb, s, f2 = u.shape
    f = f2 // 2
    d = wd.shape[1]
    tm = min(512, s)
    tk = f // 2
    nk = f // tk
    hb = tm // HALO

    def body(ug_ref, uv_ref, hg_ref, hv_ref, cwg_ref, cwv_ref, cbg_ref, cbv_ref, wd_ref, gt_ref, res_ref,
             xo_ref, y_ref, acc):
        i, k = pl.program_id(1), pl.program_id(2)

        @pl.when(k == 0)
        def _():
            acc[...] = jnp.zeros_like(acc)

        rows = _iota((tm, 1), 0)

        def conv(x_ref, h_ref, w_ref, b_ref):
            xv = x_ref[...].astype(F32)
            halo = jnp.where(i == 0, 0.0, h_ref[...].astype(F32))
            p1, p2 = _conv_shifts(xv, halo, rows)
            wv = w_ref[...]
            return wv[2:3, :] * xv + wv[1:2, :] * p1 + wv[0:1, :] * p2 + b_ref[...]

        gv = conv(ug_ref, hg_ref, cwg_ref, cbg_ref)
        vv = conv(uv_ref, hv_ref, cwv_ref, cbv_ref)
        av = gv * _sigmoid(gv) * vv
        acc[...] += _dot(av.astype(MXU), wd_ref[...])

        @pl.when(k == nk - 1)
        def _():
            y = acc[...]
            y_ref[...] = y
            xo_ref[...] = res_ref[...] + gt_ref[...] * y

    def halo_idx(off):
        return lambda b, i, k: (b, jnp.maximum(i * hb - 1, 0), k + off)

    tile = pl.BlockSpec((None, tm, d), lambda b, i, k: (b, i, 0))
    return pl.pallas_call(
        body, name=name, grid=(nb, s // tm, nk),
        in_specs=[pl.BlockSpec((None, tm, tk), lambda b, i, k: (b, i, k)),
                  pl.BlockSpec((None, tm, tk), lambda b, i, k: (b, i, k + nk)),
                  pl.BlockSpec((None, HALO, tk), halo_idx(0)),
                  pl.BlockSpec((None, HALO, tk), halo_idx(nk)),
                  pl.BlockSpec((3, tk), lambda b, i, k: (0, k)),
                  pl.BlockSpec((3, tk), lambda b, i, k: (0, k + nk)),
                  pl.BlockSpec((1, tk), lambda b, i, k: (0, k)),
                  pl.BlockSpec((1, tk), lambda b, i, k: (0, k + nk)),
                  pl.BlockSpec((tk, d), lambda b, i, k: (k, 0)),
                  pl.BlockSpec((None, 1, d), lambda b, i, k: (b, 0, 0)),
                  tile],
        out_specs=[tile, tile],
        out_shape=[jax.ShapeDtypeStruct((nb, s, d), F32)] * 2,
        scratch_shapes=[pltpu.VMEM((tm, d), F32)],
        compiler_params=_cp("parallel", "parallel", "arbitrary"))(u, u, u, u, cw, cw, cb, cb, wd, gate, res)


def _conv_gate_bwd(da, u, cw, cb, name):
    nb, s, f2 = u.shape
    f = f2 // 2
    tm = min(128, s)
    hb = tm // HALO

    def body(da_ref, u_ref, h_ref, cw_ref, cb_ref, du_ref, a_ref, st_ref):
        b, i = pl.program_id(0), pl.program_id(1)

        @pl.when((b == 0) & (i == 0))
        def _():
            st_ref[...] = jnp.zeros_like(st_ref)

        rows = _iota((tm, 1), 0)
        first = i == 0

        def conv(cs):
            xv = u_ref[:, cs].astype(F32)
            halo = jnp.where(first, 0.0, h_ref[:, cs].astype(F32))
            p1, p2 = _conv_shifts(xv, halo, rows)
            wv = cw_ref[:, cs]
            return xv, p1, p2, wv[2:3, :] * xv + wv[1:2, :] * p1 + wv[0:1, :] * p2 + cb_ref[:, cs]

        def stats(cs, du, xv, p1, p2):
            du_ref[:, cs] = du.astype(MXU)
            st_ref[0:1, cs] += _csum(du)
            st_ref[1:2, cs] += _csum(du * p2)
            st_ref[2:3, cs] += _csum(du * p1)
            st_ref[3:4, cs] += _csum(du * xv)

        for k in range(f // LANES):
            cg = slice(k * LANES, (k + 1) * LANES)
            cv = slice(f + k * LANES, f + (k + 1) * LANES)
            xg, g1, g2, gv = conv(cg)
            xv, v1, v2, vv = conv(cv)
            sg = _sigmoid(gv)
            sl = gv * sg
            a_ref[:, cg] = (sl * vv).astype(MXU)
            dav = da_ref[:, cg]
            stats(cg, dav * vv * (sg * (1.0 + gv * (1.0 - sg))), xg, g1, g2)
            stats(cv, dav * sl, xv, v1, v2)

    return pl.pallas_call(
        body, name=name, grid=(nb, s // tm),
        in_specs=[pl.BlockSpec((None, tm, f), lambda b, i: (b, i, 0)),
                  pl.BlockSpec((None, tm, f2), lambda b, i: (b, i, 0)),
                  pl.BlockSpec((None, HALO, f2), lambda b, i: (b, jnp.maximum(i * hb - 1, 0), 0)),
                  pl.BlockSpec((3, f2), lambda b, i: (0, 0)),
                  pl.BlockSpec((1, f2), lambda b, i: (0, 0))],
        out_specs=[pl.BlockSpec((None, tm, f2), lambda b, i: (b, i, 0)),
                   pl.BlockSpec((None, tm, f), lambda b, i: (b, i, 0)),
                   pl.BlockSpec((8, f2), lambda b, i: (0, 0))],
        out_shape=[jax.ShapeDtypeStruct((nb, s, f2), MXU), jax.ShapeDtypeStruct((nb, s, f), MXU),
                   jax.ShapeDtypeStruct((8, f2), F32)],
        compiler_params=_cp("arbitrary", "arbitrary"))(da, u, u, cw, cb)


def _rot(xv, lane):
    return jnp.where((lane >= 64) & (lane < 80), -pltpu.roll(xv, 112, 1),
                     jnp.where((lane >= 80) & (lane < 96), pltpu.roll(xv, 16, 1), 0.0))


def _rot_t(dv, lane):
    return jnp.where((lane >= 80) & (lane < 96), -pltpu.roll(dv, 16, 1),
                     jnp.where((lane >= 64) & (lane < 80), pltpu.roll(dv, 112, 1), 0.0))


def _mla_prep_specs(s, tm):
    def blk(width, col):
        return pl.BlockSpec((None, tm, width), lambda b, i: (b, i, col // width))

    full = lambda shape: pl.BlockSpec(shape, lambda b, i: (0, 0))
    return [blk(256, COL_CQ), blk(128, COL_CKV), blk(128, COL_KR),
            pl.BlockSpec((None, tm, LANES), lambda b, i: (b, i, 0)),
            pl.BlockSpec((None, tm, LANES), lambda b, i: (b, i, 0)),
            full((1, 256)), full((1, 128)), full((1, 128)), full((1, 128)),
            full((768, 256)), full((768, 128))]


def _mla_prep(proj, cs, sn, gcq, gckv, gqn, gkn, wuq, wukv, name):
    nb, s, _ = proj.shape
    tm = min(256, s)

    def body(cq_ref, ckv_ref, kr_ref, c_ref, s_ref, gcq_ref, gckv_ref, gqn_ref, gkn_ref, wuq_ref, wukv_ref,
             q_ref, k_ref, v_ref):
        lane = _iota((tm, LANES), 1)
        cv, sv = c_ref[...], s_ref[...]
        cq = cq_ref[...].astype(F32)
        cqn = cq * lax.rsqrt(jnp.mean(cq * cq, axis=-1, keepdims=True) + EPS) * gcq_ref[...]
        qb = _dot_nt(cqn.astype(MXU), wuq_ref[...])
        ckv = ckv_ref[...].astype(F32)
        ckvn = ckv * lax.rsqrt(jnp.mean(ckv * ckv, axis=-1, keepdims=True) + EPS) * gckv_ref[...]
        kvb = _dot_nt(ckvn.astype(MXU), wukv_ref[...])
        kr = kr_ref[...].astype(F32)
        for h in range(MLA_HEADS):
            hs = slice(h * LANES, (h + 1) * LANES)
            qh = qb[:, hs]
            qn = qh * lax.rsqrt(_rsum(qh * qh) / MLA_QK + EPS) * gqn_ref[...]
            q_ref[:, hs] = (qn * cv + _rot(qn, lane) * sv).astype(MXU)
            kc = jnp.where(lane < HEAD, kvb[:, hs], kr)
            kn = kc * lax.rsqrt(_rsum(kc * kc) / MLA_QK + EPS) * gkn_ref[...]
            k_ref[:, hs] = (kn * cv + _rot(kn, lane) * sv).astype(MXU)
        for j in range(MLA_HEADS // 2):
            va = kvb[:, (2 * j) * LANES:(2 * j + 1) * LANES]
            vb = kvb[:, (2 * j + 1) * LANES:(2 * j + 2) * LANES]
            v_ref[:, j * LANES:(j + 1) * LANES] = jnp.where(lane < HEAD, pltpu.roll(va, HEAD, 1), vb).astype(MXU)

    return pl.pallas_call(
        body, name=name, grid=(nb, s // tm), in_specs=_mla_prep_specs(s, tm),
        out_specs=[pl.BlockSpec((None, tm, 768), lambda b, i: (b, i, 0)),
                   pl.BlockSpec((None, tm, 768), lambda b, i: (b, i, 0)),
                   pl.BlockSpec((None, tm, 384), lambda b, i: (b, i, 0))],
        out_shape=[jax.ShapeDtypeStruct((nb, s, 768), MXU), jax.ShapeDtypeStruct((nb, s, 768), MXU),
                   jax.ShapeDtypeStruct((nb, s, 384), MXU)],
        compiler_params=_cp("parallel", "parallel"))(proj, proj, proj, cs, sn, gcq, gckv, gqn, gkn, wuq, wukv)


def _mla_prep_bwd(proj, cs, sn, gcq, gckv, gqn, gkn, wuq, wukv, dq, dk, dv, name):
    nb, s, _ = proj.shape
    tm = min(256, s)

    def body(cq_ref, ckv_ref, kr_ref, c_ref, s_ref, gcq_ref, gckv_ref, gqn_ref, gkn_ref, wuq_ref, wukv_ref,
             dq_ref, dk_ref, dv_ref,
             dcq_ref, dckv_ref, dkr_ref, dwuq_ref, dwukv_ref, dgcq_ref, dgckv_ref, dgqn_ref, dgkn_ref,
             dqb_s, dkvb_s):
        @pl.when((pl.program_id(0) == 0) & (pl.program_id(1) == 0))
        def _():
            for r in (dwuq_ref, dwukv_ref, dgcq_ref, dgckv_ref, dgqn_ref, dgkn_ref):
                r[...] = jnp.zeros_like(r)

        lane = _iota((tm, LANES), 1)
        cv, sv = c_ref[...], s_ref[...]
        gqn, gkn = gqn_ref[...], gkn_ref[...]
        cq = cq_ref[...].astype(F32)
        rc = lax.rsqrt(jnp.mean(cq * cq, axis=-1, keepdims=True) + EPS)
        chat = cq * rc
        cqn = (chat * gcq_ref[...]).astype(MXU)
        qb = _dot_nt(cqn, wuq_ref[...])
        ckv = ckv_ref[...].astype(F32)
        rkv = lax.rsqrt(jnp.mean(ckv * ckv, axis=-1, keepdims=True) + EPS)
        kvhat = ckv * rkv
        ckvn = (kvhat * gckv_ref[...]).astype(MXU)
        kvb = _dot_nt(ckvn, wukv_ref[...])
        kr = kr_ref[...].astype(F32)
        dgq = jnp.zeros((1, LANES), F32)
        dgk = jnp.zeros((1, LANES), F32)
        dkr = jnp.zeros((tm, LANES), F32)
        for h in range(MLA_HEADS):
            hs = slice(h * LANES, (h + 1) * LANES)
            qh = qb[:, hs]
            rq = lax.rsqrt(_rsum(qh * qh) / MLA_QK + EPS)
            qhat = qh * rq
            dqr = dq_ref[:, hs]
            dqn = dqr * cv + _rot_t(dqr * sv, lane)
            dgq = dgq + _csum(dqn * qhat)
            dyq = dqn * gqn
            dqb_s[:, hs] = (rq * (dyq - qhat * (_rsum(dyq * qhat) / MLA_QK))).astype(MXU)

            kc = jnp.where(lane < HEAD, kvb[:, hs], kr)
            rk = lax.rsqrt(_rsum(kc * kc) / MLA_QK + EPS)
            khat = kc * rk
            dkr_h = dk_ref[:, hs]
            dkn = dkr_h * cv + _rot_t(dkr_h * sv, lane)
            dgk = dgk + _csum(dkn * khat)
            dyk = dkn * gkn
            dkc = rk * (dyk - khat * (_rsum(dyk * khat) / MLA_QK))
            dkr = dkr + jnp.where(lane >= HEAD, dkc, 0.0)
            dvb = dv_ref[:, (h // 2) * LANES:(h // 2 + 1) * LANES]
            dvp = dvb if h % 2 == 1 else pltpu.roll(dvb, HEAD, 1)
            dkvb_s[:, hs] = jnp.where(lane < HEAD, dkc, dvp).astype(MXU)
        dgqn_ref[...] += dgq
        dgkn_ref[...] += dgk
        dkr_ref[...] = dkr

        dqb = dqb_s[...]
        dwuq_ref[...] += _dot_tn(dqb, cqn)
        dcqn = _dot(dqb, wuq_ref[...])
        dgcq_ref[...] += _csum(dcqn * chat)
        dyc = dcqn * gcq_ref[...]
        dcq_ref[...] = rc * (dyc - chat * jnp.mean(dyc * chat, axis=-1, keepdims=True))

        dkvb = dkvb_s[...]
        dwukv_ref[...] += _dot_tn(dkvb, ckvn)
        dckvn = _dot(dkvb, wukv_ref[...])
        dgckv_ref[...] += _csum(dckvn * kvhat)
        dykv = dckvn * gckv_ref[...]
        dckv_ref[...] = rkv * (dykv - kvhat * jnp.mean(dykv * kvhat, axis=-1, keepdims=True))

    full = lambda shape: pl.BlockSpec(shape, lambda b, i: (0, 0))
    tile = lambda width: pl.BlockSpec((None, tm, width), lambda b, i: (b, i, 0))
    return pl.pallas_call(
        body, name=name, grid=(nb, s // tm),
        in_specs=_mla_prep_specs(s, tm) + [tile(768), tile(768), tile(384)],
        out_specs=[tile(256), tile(128), tile(128), full((768, 256)), full((768, 128)),
                   full((1, 256)), full((1, 128)), full((1, 128)), full((1, 128))],
        out_shape=[jax.ShapeDtypeStruct((nb, s, 256), F32), jax.ShapeDtypeStruct((nb, s, 128), F32),
                   jax.ShapeDtypeStruct((nb, s, 128), F32),
                   jax.ShapeDtypeStruct((768, 256), F32), jax.ShapeDtypeStruct((768, 128), F32),
                   jax.ShapeDtypeStruct((1, 256), F32), jax.ShapeDtypeStruct((1, 128), F32),
                   jax.ShapeDtypeStruct((1, 128), F32), jax.ShapeDtypeStruct((1, 128), F32)],
        scratch_shapes=[pltpu.VMEM((tm, 768), MXU), pltpu.VMEM((tm, 768), MXU)],
        compiler_params=_cp("arbitrary", "arbitrary"))(
            proj, proj, proj, cs, sn, gcq, gckv, gqn, gkn, wuq, wukv, dq, dk, dv)


def _softplus(z):
    return jnp.maximum(z, 0.0) + jnp.log(1.0 + jnp.exp(-jnp.abs(z)))


def _sb_fwd(proj, name, riding=()):
    nb, s, _ = proj.shape
    tq, tk = min(256, s), min(256, s)
    ratio = tq // tk
    na = len(riding)
    grid = (nb, 2, s // tq)

    def body(q_ref, k_ref, v_ref, o_ref, ct_ref, cnt_ref):
        i = pl.program_id(2)
        lo = _iota((tq, LANES), 1) < HEAD
        qv = q_ref[...]
        q0 = jnp.where(lo, qv, 0.0).astype(MXU)
        q1 = jnp.where(lo, 0.0, qv).astype(MXU)
        usuf = (_iota((tk, tk), 0) > _iota((tk, tk), 1)).astype(MXU)
        tpos = i * tq + _iota((tq, tk), 0)
        scol = _iota((tq, tk), 1)
        nch = (i + 1) * ratio

        def alive(st):
            return (st[0] < nch) & (st[5] > SB_DEAD)

        def step(st):
            t, c0, a0, c1, a1, _ = st
            j = nch - 1 - t
            off = pl.multiple_of(j * tk, tk)
            kc = k_ref[pl.ds(off, tk), :].astype(MXU)
            vc = v_ref[pl.ds(off, tk), :].astype(MXU)
            msk = (scol + j * tk) < tpos

            def head(qm, c, a):
                z = _dot_nt(qm, kc) * SB_SCALE
                sp = _softplus(z)
                lk = jnp.where(msk, -sp, 0.0)
                w = jnp.where(msk, jnp.exp(z - sp + _cumdot(lk, usuf) + c), 0.0)
                return c + _rsum(lk), a + _dot(w.astype(MXU), vc)

            c0, a0 = head(q0, c0, a0)
            c1, a1 = head(q1, c1, a1)
            return t + 1, c0, a0, c1, a1, jnp.maximum(jnp.max(c0), jnp.max(c1))

        z1 = jnp.zeros((tq, 1), F32)
        za = jnp.zeros((tq, LANES), F32)
        t, c0, a0, c1, a1, _ = lax.while_loop(alive, step, (jnp.int32(0), z1, za, z1, za, jnp.float32(0.0)))
        o_ref[...] = jnp.where(lo, a0, a1)
        ct_ref[...] = jnp.where(lo, c0, c1)
        cnt_ref[...] = jnp.zeros((8, LANES), F32) + t.astype(F32)

    kv = lambda col: pl.BlockSpec((None, s, LANES), lambda b, p, i: (b, 0, col // LANES + p))
    tile = pl.BlockSpec((None, tq, LANES), lambda b, p, i: (b, i, p))
    hbm = pl.BlockSpec(memory_space=pl.ANY)
    return pl.pallas_call(
        _with_gather(body, 3, 3, na, grid), name=name, grid=grid,
        in_specs=[pl.BlockSpec((None, tq, LANES), lambda b, p, i: (b, i, COL_SBQ // LANES + p)),
                  kv(COL_SBK), kv(COL_SBV)] + [hbm] * na,
        out_specs=[tile, tile, pl.BlockSpec((None, None, None, 8, LANES), lambda b, p, i: (b, p, i, 0, 0))] + [hbm] * na,
        out_shape=[jax.ShapeDtypeStruct((nb, s, 256), F32)] * 2
        + [jax.ShapeDtypeStruct((nb, 2, s // tq, 8, LANES), F32)] + _gather_out_shapes(riding),
        scratch_shapes=_gather_sems(na) if na else [],
        compiler_params=_cp("arbitrary", "arbitrary", "arbitrary"))(proj, proj, proj, *riding)


def _sb_bwd(proj, ct, cnt, do, name, riding=()):
    nb, s, _ = proj.shape
    tq, tk = min(256, s), min(256, s)
    ratio = tq // tk
    na = len(riding)
    grid = (nb, 2, s // tq)

    def body(q_ref, k_ref, v_ref, ct_ref, cnt_ref, do_ref, dq_ref, dk_ref, dv_ref):
        i = pl.program_id(2)

        @pl.when(i == 0)
        def _():
            dk_ref[...] = jnp.zeros_like(dk_ref)
            dv_ref[...] = jnp.zeros_like(dv_ref)

        lane = _iota((tq, LANES), 1)
        lo = lane < HEAD
        lok = _iota((tk, LANES), 1) < HEAD
        qv, dov = q_ref[...], do_ref[...]
        qb, dob = qv.astype(MXU), dov.astype(MXU)
        q0 = jnp.where(lo, qv, 0.0).astype(MXU)
        q1 = jnp.where(lo, 0.0, qv).astype(MXU)
        do0 = jnp.where(lo, dov, 0.0).astype(MXU)
        do1 = jnp.where(lo, 0.0, dov).astype(MXU)
        ctv = ct_ref[...]
        ct0 = _rsum(jnp.where(lane == 0, ctv, 0.0))
        ct1 = _rsum(jnp.where(lane == LANES - 1, ctv, 0.0))
        uincl = (_iota((tk, tk), 0) <= _iota((tk, tk), 1)).astype(MXU)
        ustrict = (_iota((tk, tk), 0) < _iota((tk, tk), 1)).astype(MXU)
        tpos = i * tq + _iota((tq, tk), 0)
        scol = _iota((tq, tk), 1)
        nch = (i + 1) * ratio

        def step(j, carry):
            p0, g0, dq0, p1, g1, dq1 = carry
            off = pl.multiple_of(j * tk, tk)
            kc = k_ref[pl.ds(off, tk), :].astype(MXU)
            vc = v_ref[pl.ds(off, tk), :].astype(MXU)
            msk = (scol + j * tk) < tpos

            def head(qm, dom, ctot, pc, gc, dqa):
                z = _dot_nt(qm, kc) * SB_SCALE
                sp = _softplus(z)
                lk = jnp.where(msk, -sp, 0.0)
                lsig = z - sp
                w = jnp.where(msk, jnp.exp(lsig + (ctot - pc - _cumdot(lk, uincl))), 0.0)
                g = w * _dot_nt(dom, vc)
                gpre = gc + _cumdot(g, ustrict)
                sig = jnp.exp(lsig)
                dz = (jnp.where(msk, g * (1.0 - sig) - sig * gpre, 0.0) * SB_SCALE).astype(MXU)
                return (pc + _rsum(lk), gc + _rsum(g), dqa + _dot(dz, kc),
                        _dot_tn(dz, qb), _dot_tn(w.astype(MXU), dob))

            p0, g0, dq0, dk0, dv0 = head(q0, do0, ct0, p0, g0, dq0)
            p1, g1, dq1, dk1, dv1 = head(q1, do1, ct1, p1, g1, dq1)
            dk_ref[pl.ds(off, tk), :] += jnp.where(lok, dk0, dk1)
            dv_ref[pl.ds(off, tk), :] += jnp.where(lok, dv0, dv1)
            return p0, g0, dq0, p1, g1, dq1

        z1 = jnp.zeros((tq, 1), F32)
        za = jnp.zeros((tq, LANES), F32)
        first = nch - jnp.max(cnt_ref[...]).astype(jnp.int32)
        _, _, dq0, _, _, dq1 = lax.fori_loop(first, nch, step, (z1, z1, za, z1, z1, za))
        dq_ref[...] = jnp.where(lo, dq0, dq1)

    kv = lambda col: pl.BlockSpec((None, s, LANES), lambda b, p, i: (b, 0, col // LANES + p))
    tile = pl.BlockSpec((None, tq, LANES), lambda b, p, i: (b, i, p))
    acc = pl.BlockSpec((None, s, LANES), lambda b, p, i: (b, 0, p))
    hbm = pl.BlockSpec(memory_space=pl.ANY)
    return pl.pallas_call(
        _with_chip_exchange(body, 6, 3, na, grid, _pair_exchange_copies), name=name, grid=grid,
        in_specs=[pl.BlockSpec((None, tq, LANES), lambda b, p, i: (b, i, COL_SBQ // LANES + p)),
                  kv(COL_SBK), kv(COL_SBV), tile,
                  pl.BlockSpec((None, None, None, 8, LANES), lambda b, p, i: (b, p, i, 0, 0)), tile] + [hbm] * na,
        out_specs=[tile, acc, acc] + [hbm] * na,
        out_shape=[jax.ShapeDtypeStruct((nb, s, 256), F32)] * 3 + _pair_exchange_out_shapes(riding),
        scratch_shapes=_pair_exchange_sems(na) if na else [],
        compiler_params=_cp("arbitrary", "arbitrary", "arbitrary"))(proj, proj, proj, ct, cnt, do, *riding)


def _mla_fwd(q, k, v, name, riding=()):
    nb, s, _ = q.shape
    tq = tk = min(1024, s)
    na = len(riding)
    grid = (nb, MLA_HEADS // 2, s // tq)

    def body(q_ref, k_ref, v_ref, o_ref, lse_ref):
        i = pl.program_id(2)
        q0, q1 = q_ref[:, :LANES], q_ref[:, LANES:]
        krow = _iota((tk, tq), 0)
        qcol = _iota((tk, tq), 1)

        def step(j, carry, diagonal):
            m0, l0, a0, m1, l1, a1 = carry
            off = pl.multiple_of(j * tk, tk)
            vc = v_ref[pl.ds(off, tk), :]

            def head(qh, kh, m, l, a):
                st = _dot_nt(kh, qh) * MLA_SCALE
                if diagonal:
                    st = jnp.where(krow <= qcol, st, NEG)
                mn = jnp.maximum(m, jnp.max(st, axis=0, keepdims=True))
                al = jnp.exp(m - mn)
                pt = jnp.exp(st - mn)
                return mn, al * l + _csum(pt), al * a + _dot_tn(vc, pt.astype(MXU))

            m0, l0, a0 = head(q0, k_ref[pl.ds(off, tk), :LANES], m0, l0, a0)
            m1, l1, a1 = head(q1, k_ref[pl.ds(off, tk), LANES:], m1, l1, a1)
            return m0, l0, a0, m1, l1, a1

        mi = jnp.full((1, tq), NEG, F32)
        z1 = jnp.zeros((1, tq), F32)
        za = jnp.zeros((LANES, tq), F32)
        carry = lax.fori_loop(0, i, lambda j, cr: step(j, cr, False), (mi, z1, za, mi, z1, za))
        m0, l0, a0, m1, l1, a1 = step(i, carry, True)
        lo_rows = _iota((LANES, tq), 0) < HEAD
        o_ref[...] = jnp.where(lo_rows, a0 / l0, a1 / l1).T
        lse_ref[...] = jnp.zeros_like(lse_ref)
        lse_ref[0:1, :] = m0 + jnp.log(l0)
        lse_ref[1:2, :] = m1 + jnp.log(l1)

    tile = pl.BlockSpec((None, tq, LANES), lambda b, p, i: (b, i, p))
    hbm = pl.BlockSpec(memory_space=pl.ANY)
    return pl.pallas_call(
        _with_gather(body, 3, 2, na, grid), name=name, grid=grid,
        in_specs=[pl.BlockSpec((None, tq, 2 * LANES), lambda b, p, i: (b, i, p)),
                  pl.BlockSpec((None, s, 2 * LANES), lambda b, p, i: (b, 0, p)),
                  pl.BlockSpec((None, s, LANES), lambda b, p, i: (b, 0, p))] + [hbm] * na,
        out_specs=[tile, pl.BlockSpec((None, None, 8, tq), lambda b, p, i: (b, p, 0, i))] + [hbm] * na,
        out_shape=[jax.ShapeDtypeStruct((nb, s, 384), F32), jax.ShapeDtypeStruct((nb, MLA_HEADS // 2, 8, s), F32)]
        + _gather_out_shapes(riding),
        scratch_shapes=_gather_sems(na) if na else [],
        compiler_params=_cp("arbitrary", "arbitrary", "arbitrary"))(q, k, v, *riding)


def _mla_bwd(q, k, v, o, lse, do, name, riding=()):
    nb, s, _ = q.shape
    tq = tk = min(1024, s)
    na = len(riding)
    grid = (nb, MLA_HEADS // 2, s // tq)

    def body(q_ref, k_ref, v_ref, o_ref, lse_ref, do_ref, dq_ref, dk_ref, dv_ref):
        i = pl.program_id(2)

        @pl.when(i == 0)
        def _():
            dk_ref[...] = jnp.zeros_like(dk_ref)
            dv_ref[...] = jnp.zeros_like(dv_ref)

        lo = _iota((tq, LANES), 1) < HEAD
        lok = _iota((tk, LANES), 1) < HEAD
        q0, q1 = q_ref[:, :LANES], q_ref[:, LANES:]
        dov = do_ref[...]
        dob = dov.astype(MXU)
        do0 = jnp.where(lo, dov, 0.0).astype(MXU)
        do1 = jnp.where(lo, 0.0, dov).astype(MXU)
        dd = dov * o_ref[...]
        hi = dd.astype(MXU)
        r1 = dd - hi.astype(F32)
        mid = r1.astype(MXU)
        low = (r1 - mid.astype(F32)).astype(MXU)
        sel_lane = _iota((8, LANES), 1) < HEAD
        sel0 = sel_lane.astype(MXU)
        sel1 = (~sel_lane).astype(MXU)
        dl0 = (_dot_nt(sel0, hi) + _dot_nt(sel0, mid) + _dot_nt(sel0, low))[0:1, :]
        dl1 = (_dot_nt(sel1, hi) + _dot_nt(sel1, mid) + _dot_nt(sel1, low))[0:1, :]
        ls0, ls1 = lse_ref[0:1, :], lse_ref[1:2, :]
        krow = _iota((tk, tq), 0)
        qcol = _iota((tk, tq), 1)

        def step(j, carry, diagonal):
            dq0, dq1 = carry
            off = pl.multiple_of(j * tk, tk)
            vc = v_ref[pl.ds(off, tk), :]

            def head(qh, kh, dom, ls, dl, dqa):
                st = _dot_nt(kh, qh) * MLA_SCALE
                if diagonal:
                    st = jnp.where(krow <= qcol, st, NEG)
                pt = jnp.exp(st - ls)
                dst = (pt * (_dot_nt(vc, dom) - dl) * MLA_SCALE).astype(MXU)
                return dqa + _dot_tn(kh, dst), _dot(dst, qh), _dot(pt.astype(MXU), dob)

            dq0, dk0, dv0 = head(q0, k_ref[pl.ds(off, tk), :LANES], do0, ls0, dl0, dq0)
            dq1, dk1, dv1 = head(q1, k_ref[pl.ds(off, tk), LANES:], do1, ls1, dl1, dq1)
            dk_ref[pl.ds(off, tk), :LANES] += dk0
            dk_ref[pl.ds(off, tk), LANES:] += dk1
            dv_ref[pl.ds(off, tk), :] += jnp.where(lok, dv0, dv1)
            return dq0, dq1

        za = jnp.zeros((LANES, tq), F32)
        carry = lax.fori_loop(0, i, lambda j, cr: step(j, cr, False), (za, za))
        dq0, dq1 = step(i, carry, True)
        dq_ref[:, :LANES] = dq0.T
        dq_ref[:, LANES:] = dq1.T

    tile = pl.BlockSpec((None, tq, LANES), lambda b, p, i: (b, i, p))
    tile2 = pl.BlockSpec((None, tq, 2 * LANES), lambda b, p, i: (b, i, p))
    hbm = pl.BlockSpec(memory_space=pl.ANY)
    return pl.pallas_call(
        _with_chip_exchange(body, 6, 3, na, grid), name=name, grid=grid,
        in_specs=[tile2,
                  pl.BlockSpec((None, s, 2 * LANES), lambda b, p, i: (b, 0, p)),
                  pl.BlockSpec((None, s, LANES), lambda b, p, i: (b, 0, p)),
                  tile, pl.BlockSpec((None, None, 8, tq), lambda b, p, i: (b, p, 0, i)), tile] + [hbm] * na,
        out_specs=[tile2,
                   pl.BlockSpec((None, s, 2 * LANES), lambda b, p, i: (b, 0, p)),
                   pl.BlockSpec((None, s, LANES), lambda b, p, i: (b, 0, p))] + [hbm] * na,
        out_shape=[jax.ShapeDtypeStruct((nb, s, 768), F32), jax.ShapeDtypeStruct((nb, s, 768), F32),
                   jax.ShapeDtypeStruct((nb, s, 384), F32)] + [jax.ShapeDtypeStruct(a.shape, a.dtype) for a in riding],
        scratch_shapes=_chip_exchange_sems(na) if na else [],
        compiler_params=_cp("arbitrary", "arbitrary", "arbitrary"))(q, k, v, o, lse, do, *riding)


def _half_stats(xv, lo):
    x2 = xv * xv
    s0 = _rsum(jnp.where(lo, x2, 0.0))
    s1 = _rsum(jnp.where(lo, 0.0, x2))
    return jnp.where(lo, lax.rsqrt(s0 / HEAD + EPS), lax.rsqrt(s1 / HEAD + EPS))


def _half_mean(xv, lo):
    s0 = _rsum(jnp.where(lo, xv, 0.0))
    s1 = _rsum(jnp.where(lo, 0.0, xv))
    return jnp.where(lo, s0, s1) / HEAD


def _swa_in_specs():
    def band(col, prev):
        if prev:
            return pl.BlockSpec((None, BLOCK, LANES), lambda b, n: (b, jnp.maximum(n - 1, 0), col // LANES))
        return pl.BlockSpec((None, BLOCK, LANES), lambda b, n: (b, n, col // LANES))

    full = lambda shape: pl.BlockSpec(shape, lambda b, n: tuple(0 for _ in shape))
    return [pl.BlockSpec((None, BLOCK, 384), lambda b, n: (b, n, COL_SWQ // 384)),
            band(COL_SWK, False), band(COL_SWK, True), band(COL_SWV, False), band(COL_SWV, True),
            full((1, LANES)), full((1, LANES)), full((8, LANES)), full((SW_HEADS, BLOCK, 2 * BLOCK))]


def _swa_valid(n):
    a = _iota((BLOCK, 2 * BLOCK), 0)
    bcol = _iota((BLOCK, 2 * BLOCK), 1)
    dist = BLOCK + a - bcol
    return (dist >= 0) & (dist < BLOCK) & ((n > 0) | (bcol >= BLOCK))


def _swa_fwd(proj, gq, gk, sinks, bias, name):
    nb, s, _ = proj.shape

    def body(q_ref, kc_ref, kp_ref, vc_ref, vp_ref, gq_ref, gk_ref, sk_ref, bias_ref, o_ref):
        n = pl.program_id(1)
        lo = _iota((BLOCK, LANES), 1) < HEAD
        lo2 = _iota((2 * BLOCK, LANES), 1) < HEAD
        kband = jnp.concatenate([kp_ref[...], kc_ref[...]], axis=0).astype(F32)
        vband = jnp.concatenate([vp_ref[...], vc_ref[...]], axis=0).astype(F32)
        kn = kband * _half_stats(kband, lo2) * gk_ref[...]
        ks = (kn.astype(MXU), pltpu.roll(kn, HEAD, 1).astype(MXU))
        vs = (vband.astype(MXU), pltpu.roll(vband, HEAD, 1).astype(MXU))
        valid = _swa_valid(n)
        for blk in range(SW_HEADS // 2):
            qv = q_ref[:, blk * LANES:(blk + 1) * LANES].astype(F32)
            qn = qv * _half_stats(qv, lo) * gq_ref[...]
            outs = []
            for half in range(2):
                h = 2 * blk + half
                swap = 0 if half == h // 3 else 1
                qm = jnp.where(lo if half == 0 else ~lo, qn, 0.0).astype(MXU)
                sc = jnp.where(valid, _dot_nt(qm, ks[swap]) * SW_SCALE + bias_ref[h], NEG)
                sk = jnp.max(sk_ref[h:h + 1, :], axis=-1, keepdims=True)
                m = jnp.maximum(jnp.max(sc, axis=-1, keepdims=True), sk)
                p = jnp.exp(sc - m)
                l = _rsum(p) + jnp.exp(sk - m)
                outs.append(_dot((p / l).astype(MXU), vs[swap]))
            o_ref[:, blk * LANES:(blk + 1) * LANES] = jnp.where(lo, outs[0], outs[1])

    return pl.pallas_call(
        body, name=name, grid=(nb, s // BLOCK), in_specs=_swa_in_specs(),
        out_specs=pl.BlockSpec((None, BLOCK, 384), lambda b, n: (b, n, 0)),
        out_shape=jax.ShapeDtypeStruct((nb, s, 384), F32),
        compiler_params=_cp("parallel", "parallel"))(proj, proj, proj, proj, proj, gq, gk, sinks, bias)


def _swa_bwd(proj, gq, gk, sinks, bias, do, name):
    nb, s, _ = proj.shape

    def body(q_ref, kc_ref, kp_ref, vc_ref, vp_ref, gq_ref, gk_ref, sk_ref, bias_ref, do_ref,
             dq_ref, dkc_ref, dkp_ref, dvc_ref, dvp_ref, dbias_ref, dsk_ref, dgq_ref, dgk_ref):
        n = pl.program_id(1)

        @pl.when((pl.program_id(0) == 0) & (n == 0))
        def _():
            for r in (dbias_ref, dsk_ref, dgq_ref, dgk_ref):
                r[...] = jnp.zeros_like(r)

        lo = _iota((BLOCK, LANES), 1) < HEAD
        lo2 = _iota((2 * BLOCK, LANES), 1) < HEAD
        kband = jnp.concatenate([kp_ref[...], kc_ref[...]], axis=0).astype(F32)
        vband = jnp.concatenate([vp_ref[...], vc_ref[...]], axis=0).astype(F32)
        rk = _half_stats(kband, lo2)
        khat = kband * rk
        gkv = gk_ref[...]
        kn = khat * gkv
        ks = (kn.astype(MXU), pltpu.roll(kn, HEAD, 1).astype(MXU))
        vs = (vband.astype(MXU), pltpu.roll(vband, HEAD, 1).astype(MXU))
        valid = _swa_valid(n)
        dkn = jnp.zeros((2 * BLOCK, LANES), F32)
        dvb = jnp.zeros((2 * BLOCK, LANES), F32)
        gqv = gq_ref[...]
        dgq = jnp.zeros((1, LANES), F32)
        for blk in range(SW_HEADS // 2):
            bs = slice(blk * LANES, (blk + 1) * LANES)
            qv = q_ref[:, bs].astype(F32)
            rq = _half_stats(qv, lo)
            qhat = qv * rq
            qn = qhat * gqv
            dov = do_ref[:, bs]
            dqn = jnp.zeros((BLOCK, LANES), F32)
            for half in range(2):
                h = 2 * blk + half
                swap = 0 if half == h // 3 else 1
                hm = lo if half == 0 else ~lo
                qm = jnp.where(hm, qn, 0.0).astype(MXU)
                dom = jnp.where(hm, dov, 0.0).astype(MXU)
                sc = jnp.where(valid, _dot_nt(qm, ks[swap]) * SW_SCALE + bias_ref[h], NEG)
                sk = jnp.max(sk_ref[h:h + 1, :], axis=-1, keepdims=True)
                m = jnp.maximum(jnp.max(sc, axis=-1, keepdims=True), sk)
                e = jnp.exp(sc - m)
                es = jnp.exp(sk - m)
                l = _rsum(e) + es
                p = e / l
                dp = _dot_nt(dom, vs[swap])
                delta = _rsum(p * dp)
                ds = p * (dp - delta)
                dsk_ref[h:h + 1, :] += jnp.broadcast_to(_csum(-(es / l) * delta), (1, LANES))
                dbias_ref[h] += ds
                dsb = (ds * SW_SCALE).astype(MXU)
                dqn = dqn + jnp.where(hm, _dot(dsb, ks[swap]), 0.0)
                rk_ = _dot_tn(dsb, qm)
                rv_ = _dot_tn(p.astype(MXU), dom)
                if swap:
                    rk_ = pltpu.roll(rk_, HEAD, 1)
                    rv_ = pltpu.roll(rv_, HEAD, 1)
                dkn = dkn + rk_
                dvb = dvb + rv_
            dgq = dgq + _csum(dqn * qhat)
            dyq = dqn * gqv
            dq_ref[:, bs] = rq * (dyq - qhat * _half_mean(dyq * qhat, lo))
        dgq_ref[...] += dgq
        dgk_ref[...] += _csum(dkn * khat)
        dyk = dkn * gkv
        dkb = rk * (dyk - khat * _half_mean(dyk * khat, lo2))
        dkp_ref[...] = dkb[:BLOCK]
        dkc_ref[...] = dkb[BLOCK:]
        dvp_ref[...] = dvb[:BLOCK]
        dvc_ref[...] = dvb[BLOCK:]

    full = lambda shape: pl.BlockSpec(shape, lambda b, n: tuple(0 for _ in shape))
    tile = pl.BlockSpec((None, BLOCK, LANES), lambda b, n: (b, n, 0))
    tile3 = pl.BlockSpec((None, BLOCK, 384), lambda b, n: (b, n, 0))
    kvs = jax.ShapeDtypeStruct((nb, s, LANES), F32)
    return pl.pallas_call(
        body, name=name, grid=(nb, s // BLOCK), in_specs=_swa_in_specs() + [tile3],
        out_specs=[tile3, tile, tile, tile, tile, full((SW_HEADS, BLOCK, 2 * BLOCK)), full((8, LANES)),
                   full((1, LANES)), full((1, LANES))],
        out_shape=[jax.ShapeDtypeStruct((nb, s, 384), F32), kvs, kvs, kvs, kvs,
                   jax.ShapeDtypeStruct((SW_HEADS, BLOCK, 2 * BLOCK), F32), jax.ShapeDtypeStruct((8, LANES), F32),
                   jax.ShapeDtypeStruct((1, LANES), F32), jax.ShapeDtypeStruct((1, LANES), F32)],
        compiler_params=_cp("arbitrary", "arbitrary"))(proj, proj, proj, proj, proj, gq, gk, sinks, bias, do)


def _bias_build(table, bucket, name):
    def body(tb_ref, bk_ref, o_ref):
        bk = bk_ref[...]
        tb = tb_ref[...]
        row = _iota((8, LANES), 0)
        col = _iota((8, LANES), 1)
        for h in range(SW_HEADS):
            acc = jnp.zeros((BLOCK, 2 * BLOCK), F32)
            for t in range(REL_BUCKETS):
                val = jnp.sum(jnp.where((row == h) & (col == t), tb, 0.0), keepdims=True)
                acc = jnp.where(bk == t, val, acc)
            o_ref[h] = acc

    return pl.pallas_call(
        body, name=name, out_shape=jax.ShapeDtypeStruct((SW_HEADS, BLOCK, 2 * BLOCK), F32))(table, bucket)


def _bias_grad(dbias, bucket, name):
    def body(db_ref, bk_ref, o_ref):
        bk = bk_ref[...]
        row = _iota((8, LANES), 0)
        col = _iota((8, LANES), 1)
        res = jnp.zeros((8, LANES), F32)
        for h in range(SW_HEADS):
            dbh = db_ref[h]
            for t in range(REL_BUCKETS):
                val = jnp.sum(jnp.where(bk == t, dbh, 0.0), keepdims=True)
                res = jnp.where((row == h) & (col == t), val, res)
        o_ref[...] = res

    return pl.pallas_call(body, name=name, out_shape=jax.ShapeDtypeStruct((8, LANES), F32))(dbias, bucket)


def _loss_grad(y, target, name):
    nb, s, d = y.shape
    tm = min(512, s)

    def body(y_ref, t_ref, loss_ref, dy_ref):
        @pl.when((pl.program_id(0) == 0) & (pl.program_id(1) == 0))
        def _():
            loss_ref[...] = jnp.zeros_like(loss_ref)

        e = y_ref[...] - t_ref[...]
        dy_ref[...] = e / d
        loss_ref[...] += 0.5 * jnp.sum(_rsum(e * e) / d, keepdims=True)

    tile = pl.BlockSpec((None, tm, d), lambda b, i: (b, i, 0))
    return pl.pallas_call(
        body, name=name, grid=(nb, s // tm), in_specs=[tile, tile],
        out_specs=[pl.BlockSpec((8, LANES), lambda b, i: (0, 0)), tile],
        out_shape=[jax.ShapeDtypeStruct((8, LANES), F32), jax.ShapeDtypeStruct((nb, s, d), F32)],
        compiler_params=_cp("arbitrary", "arbitrary"))(y, target)


def _adamw(parts, w, m, v, name):
    npart, r, ncol = parts.shape
    tr = _row_tile(r, ncol)
    bc1 = 1.0 - ADAM_B1 ** ADAM_STEP
    bc2 = 1.0 - ADAM_B2 ** ADAM_STEP

    def body(p_ref, w_ref, m_ref, v_ref, g_ref, d_ref, nm_ref, nv_ref):
        g = p_ref[0].astype(F32)
        for k in range(1, npart):
            g = g + p_ref[k].astype(F32)
        mn = ADAM_B1 * m_ref[...] + (1.0 - ADAM_B1) * g
        vn = ADAM_B2 * v_ref[...] + (1.0 - ADAM_B2) * (g * g)
        g_ref[...] = g
        nm_ref[...] = mn
        nv_ref[...] = vn
        d_ref[...] = -ADAM_LR * ((mn / bc1) / (jnp.sqrt(vn / bc2) + ADAM_EPS) + ADAM_WD * w_ref[...])

    tile = pl.BlockSpec((tr, ncol), lambda i: (i, 0))
    return pl.pallas_call(
        body, name=name, grid=(r // tr,),
        in_specs=[pl.BlockSpec((npart, tr, ncol), lambda i: (0, i, 0)), tile, tile, tile],
        out_specs=[tile] * 4, out_shape=[jax.ShapeDtypeStruct((r, ncol), F32)] * 4,
        compiler_params=_cp("parallel"))(parts, w, m, v)


def _unpack(flat, shapes, lead=()):
    out, off = [], 0
    for shp in shapes:
        size = 1
        for dim in shp:
            size *= dim
        out.append(flat[..., off:off + size].reshape(lead + tuple(shp)))
        off += size
    return out


def _t5_bucket():
    a = jnp.arange(BLOCK)[:, None]
    b = jnp.arange(2 * BLOCK)[None, :]
    dist = BLOCK + a - b
    max_exact = REL_BUCKETS // 2
    nn = jnp.maximum(dist, 0)
    nf = jnp.maximum(nn, 1).astype(F32)
    large = max_exact + (jnp.log(nf / max_exact) / math.log(BLOCK / max_exact)
                         * (REL_BUCKETS - max_exact)).astype(jnp.int32)
    large = jnp.minimum(large, REL_BUCKETS - 1)
    return jnp.where(nn < max_exact, nn, large).astype(jnp.int32)


def _pad_lanes(g, n):
    return jnp.pad(g, (0, n - g.shape[0])).reshape(1, n)


def kernel(x, c, positions, rel_table, norm1_g, norm2_g, w_ada, b_ada, w_in, mla_cq_g, w_uq, mla_ckv_g, w_ukv, mla_qn_g, mla_kn_g, sw_qn_g, sw_kn_g, sw_sinks, w_out, w_up, conv_w, conv_b, w_down, loss_target, m_rel_table, m_norm1_g, m_norm2_g, m_w_ada, m_b_ada, m_w_in, m_mla_cq_g, m_w_uq, m_mla_ckv_g, m_w_ukv, m_mla_qn_g, m_mla_kn_g, m_sw_qn_g, m_sw_kn_g, m_sw_sinks, m_w_out, m_w_up, m_conv_w, m_conv_b, m_w_down, v_rel_table, v_norm1_g, v_norm2_g, v_w_ada, v_b_ada, v_w_in, v_mla_cq_g, v_w_uq, v_mla_ckv_g, v_w_ukv, v_mla_qn_g, v_mla_kn_g, v_sw_qn_g, v_sw_kn_g, v_sw_sinks, v_w_out, v_w_up, v_conv_w, v_conv_b, v_w_down):
    nb, s, d = x.shape
    nl = norm1_g.shape[0]
    me = 4 * lax.axis_index("x") + 2 * lax.axis_index("y") + lax.axis_index("c")
    n_ada = w_ada.shape[2]

    shard = lambda w, l, transposed: (jnp.swapaxes(w[l], 0, 1) if transposed else w[l]).astype(MXU)
    attn_local = lambda l: [shard(w_in, l, True), shard(w_uq, l, True), shard(w_ukv, l, True), shard(w_out, l, False)]
    ffn_local = lambda l: [shard(w_up, l, True), shard(w_down, l, False)]
    full = lambda a: a.reshape(-1, a.shape[-1])
    zrows = lambda n: jnp.zeros((n, d), MXU)
    pad_in = lambda wt: jnp.concatenate([wt[:1152], wt[1184:1824], zrows(64), wt[1152:1184], zrows(160)], axis=0)
    pad_uq = lambda wt: jnp.pad(wt.reshape(MLA_HEADS, MLA_QK, 256), ((0, 0), (0, LANES - MLA_QK), (0, 0))).reshape(768, 256)
    got = _all_gather(attn_local(0) + [conv_w.reshape(-1, conv_w.shape[-1]), c], "gather_inputs")
    w_in_pt, w_uq_pt, w_ukv_t, w_out_f = [pad_in(full(got[0]))], [pad_uq(full(got[1]))], [full(got[2])], [full(got[3])]
    w_up_t, w_down_f = [], []
    conv_full = got[4].reshape(N_DEV, nl, 3, -1).transpose(1, 2, 0, 3).reshape(nl, 3, -1)
    c_all = got[5].reshape(N_DEV * nb, d)

    b_my = lax.dynamic_slice_in_dim(b_ada, me * n_ada, n_ada, axis=1).reshape(nl, 1, n_ada)
    mods_my = _ada_fwd(c_all, w_ada, b_my, "ada_fwd")
    mods, = _all_gather([mods_my.reshape(nl * N_DEV * nb, n_ada)], "gather_mods")
    mods = mods.reshape(N_DEV, nl, N_DEV * nb, n_ada).transpose(1, 2, 0, 3).reshape(nl, N_DEV * nb, N_DEV * n_ada)
    mods = lax.dynamic_slice_in_dim(mods, me * nb, nb, axis=1)
    shift1, scale1, gate1, shift2, scale2, gate2 = [mods[:, :, k * d:(k + 1) * d].reshape(nl, nb, 1, d) for k in range(6)]

    half = 16
    inv_freq = jnp.power(ROPE_THETA, -jnp.arange(half, dtype=F32) / half)
    ang = positions.astype(F32)[..., None] * inv_freq
    ones = lambda n: jnp.ones((nb, s, n), F32)
    zeros = lambda n: jnp.zeros((nb, s, n), F32)
    rope_c = jnp.concatenate([ones(64), jnp.cos(ang), jnp.cos(ang), ones(32)], axis=-1)
    rope_s = jnp.concatenate([zeros(64), jnp.sin(ang), jnp.sin(ang), zeros(32)], axis=-1)
    bucket = _t5_bucket()
    bias = _bias_build(jnp.pad(rel_table.T, ((0, 8 - SW_HEADS), (0, LANES - REL_BUCKETS))), bucket, "rel_bias")

    row = lambda g: g.reshape(1, -1)
    twice = lambda g: jnp.concatenate([g, g]).reshape(1, LANES)

    saved = []
    xl = x
    for l in range(nl):
        proj, h1 = _ln_mod_matmul(xl, row(norm1_g[l]), scale1[l], shift1[l], w_in_pt[l], f"l{l}_in_proj")
        prep_args = (proj, rope_c, rope_s, row(mla_cq_g[l]), row(mla_ckv_g[l]), _pad_lanes(mla_qn_g[l], LANES),
                     _pad_lanes(mla_kn_g[l], LANES), w_uq_pt[l], w_ukv_t[l])
        qm, km, vm = _mla_prep(*prep_args, f"l{l}_mla_prep")
        o_a, ct_a, cnt_a, up_g, down_g = _sb_fwd(proj, f"l{l}_sb_fwd", riding=ffn_local(l))
        w_up_t.append(full(up_g))
        w_down_f.append(full(down_g))
        o_b, lse_b, *nxt = _mla_fwd(qm, km, vm, f"l{l}_mla_fwd", riding=attn_local(l + 1) if l + 1 < nl else ())
        if nxt:
            w_in_pt.append(pad_in(full(nxt[0])))
            w_uq_pt.append(pad_uq(full(nxt[1])))
            w_ukv_t.append(full(nxt[2]))
            w_out_f.append(full(nxt[3]))
        sinks = jnp.broadcast_to(jnp.pad(sw_sinks[l], (0, 2))[:, None], (8, LANES))
        swa_args = (proj, twice(sw_qn_g[l]), twice(sw_kn_g[l]), sinks, bias)
        o_c = _swa_fwd(*swa_args, f"l{l}_swa_fwd")
        wo = [w_out_f[l][:256], w_out_f[l][256:640], w_out_f[l][640:]]
        x_mid, y1 = _out_proj([o_a, o_b, o_c], wo, gate1[l], xl, f"l{l}_out_proj")
        u_pre, h2 = _ln_mod_matmul(x_mid, row(norm2_g[l]), scale2[l], shift2[l], w_up_t[l], f"l{l}_up_proj")
        x_out, y2 = _conv_gate_matmul(u_pre, conv_full[l], row(conv_b[l]), w_down_f[l], gate2[l], x_mid, f"l{l}_ffn_down")
        saved.append(dict(x=xl, proj=proj, h1=h1, prep=prep_args, qkv=(qm, km, vm), o_a=o_a, ct_a=ct_a, cnt_a=cnt_a, o_b=o_b, lse_b=lse_b,
                          swa=swa_args, o_c=o_c, wo=wo, y1=y1, x_mid=x_mid, u_pre=u_pre, h2=h2, y2=y2))
        xl = x_out

    loss_blk, dx = _loss_grad(xl, loss_target, "loss")
    loss = lax.psum(loss_blk[0, 0], ("x", "y", "c"))

    t = nb * s
    flat = lambda a: a.reshape(t, a.shape[-1])
    grads = [None] * nl
    dmods = [None] * nl
    sharded_out = [None] * nl
    sharded_names = ["w_in", "w_uq", "w_ukv", "w_up", "w_out", "w_down", "conv_w"]
    sharded_wmv = dict(w_in=(w_in, m_w_in, v_w_in), w_uq=(w_uq, m_w_uq, v_w_uq), w_ukv=(w_ukv, m_w_ukv, v_w_ukv),
                       w_up=(w_up, m_w_up, v_w_up), w_out=(w_out, m_w_out, v_w_out), w_down=(w_down, m_w_down, v_w_down),
                       conv_w=(conv_w, m_conv_w, v_conv_w))
    n_in, n_up, n_out, n_dn = w_in.shape[2], w_up.shape[2], w_out.shape[1], w_down.shape[1]
    small_sizes = [w_uq[0].size, w_ukv[0].size, conv_w[0].size]
    n_small_rows = -(-sum(small_sizes) // d)
    rows_used = n_in + n_out + n_small_rows
    rows_grad = -(-rows_used // 16) * 16

    def pack_rows(mats, vecs):
        lead = mats[0].shape[:-2]
        flat_part = jnp.concatenate(vecs, axis=-1)
        flat_part = jnp.pad(flat_part, [(0, 0)] * len(lead) + [(0, n_small_rows * d - flat_part.shape[-1])])
        tail = jnp.zeros(lead + (rows_grad - rows_used, d), F32)
        return jnp.concatenate(list(mats) + [flat_part.reshape(lead + (n_small_rows, d)), tail], axis=-2)

    def unpack_rows(a):
        o1, o2 = n_in, n_in + n_out
        flat_part = a[o2:o2 + n_small_rows].reshape(-1)
        s1, s2, s3 = small_sizes[0], small_sizes[0] + small_sizes[1], sum(small_sizes)
        return dict(w_in=a[:o1].T, w_out=a[o1:o2],
                    w_uq=flat_part[:s1].reshape(w_uq.shape[2], -1).T, w_ukv=flat_part[s1:s2].reshape(w_ukv.shape[2], -1).T,
                    conv_w=flat_part[s2:s3].reshape(conv_w.shape[1:]))

    ffn_out = [None] * nl
    core = lax.axis_index("c").reshape(1).astype(jnp.int32)

    def update_ffn(l, recv):
        wmv = [{k: v[o][l] for k, v in sharded_wmv.items()} for o in range(3)]
        res_up = _adamw(recv[0], *[a["w_up"].T for a in wmv], f"l{l}_adamw_up")
        res_dn = _adamw(recv[1], *[a["w_down"] for a in wmv], f"l{l}_adamw_down")
        ffn_out[l] = [dict(w_up=ru.T, w_down=rd) for ru, rd in zip(res_up, res_dn)]

    def update_rest(l, recv):
        wmv = [{k: v[o][l] for k, v in sharded_wmv.items()} for o in range(3)]
        res_rest = _adamw(recv, *[pack_rows([a["w_in"].T, a["w_out"]], [a["w_uq"].T.reshape(-1), a["w_ukv"].T.reshape(-1),
                                                                         a["conv_w"].reshape(-1)]) for a in wmv],
                          f"l{l}_adamw_rest")
        sharded_out[l] = [dict(unpack_rows(rr), **ff) for rr, ff in zip(res_rest, ffn_out[l])]

    pending = None
    dbias = jnp.zeros((SW_HEADS, BLOCK, 2 * BLOCK), F32)
    for l in reversed(range(nl)):
        sv = saved[l]
        (da,), dy2, dgate2 = _gate_bwd_nt(dx, sv["y2"], gate2[l], [w_down_f[l]], f"l{l}_ffn_down_bwd")
        du, a_act, cstats = _conv_gate_bwd(da, sv["u_pre"], conv_full[l], row(conv_b[l]), f"l{l}_conv_gate_bwd")
        res = _ln_mod_matmul_bwd(du, w_up_t[l], sv["x_mid"], row(norm2_g[l]), scale2[l], dx, conv_full[l],
                                 f"l{l}_up_proj_bwd", riding=[pending[1]] if pending else ())
        dx_mid, du_pre, dshift2, dscale2, dg2 = res[:5]
        if pending:
            update_rest(pending[0], res[5])
            pending = None
        g_w_down = _wgrad(flat(a_act), flat(dy2), f"l{l}_w_down_grad")
        g_w_up_t = _wgrad(flat(du_pre), flat(sv["h2"]), f"l{l}_w_up_grad")
        per_dev = lambda g: g.reshape(N_DEV, -1, d)
        ffn_send = [per_dev(g_w_up_t), per_dev(g_w_down)]

        (do_a, do_b, do_c), dy1, dgate1 = _gate_bwd_nt(dx_mid, sv["y1"], gate1[l], sv["wo"], f"l{l}_out_proj_bwd")
        mix = jnp.concatenate([sv["o_a"], sv["o_b"], sv["o_c"]], axis=-1).astype(MXU)
        g_w_out = _wgrad(flat(mix), flat(dy1), f"l{l}_w_out_grad")

        dsb_q, dsb_k, dsb_v, *ffn_sib = _sb_bwd(sv["proj"], sv["ct_a"], sv["cnt_a"], do_a, f"l{l}_sb_bwd", riding=ffn_send)
        ffn_pair = [_pair_add(core, a, b, f"l{l}_pair_add_{k}") for a, b, k in zip(ffn_send, ffn_sib, ("up", "down"))]
        qm, km, vm = sv["qkv"]
        dqm, dkm, dvm, *ffn_recv = _mla_bwd(qm, km, vm, sv["o_b"], sv["lse_b"], do_b, f"l{l}_mla_bwd", riding=ffn_pair)
        update_ffn(l, ffn_recv)
        dsw_q, dkc, dkp, dvc, dvp, dbias_l, dsinks, dg_swq, dg_swk = _swa_bwd(*sv["swa"], do_c, f"l{l}_swa_bwd")
        dbias = dbias + dbias_l
        shift_up = lambda a: jnp.concatenate([a[:, BLOCK:], jnp.zeros((nb, BLOCK, LANES), F32)], axis=1)
        dsw_k = dkc + shift_up(dkp)
        dsw_v = dvc + shift_up(dvp)
        dcq, dckv, dkr, g_w_uq_pt, g_w_ukv_t, dg_cq, dg_ckv, dg_qn, dg_kn = _mla_prep_bwd(
            *sv["prep"], dqm, dkm, dvm, f"l{l}_mla_prep_bwd")
        dproj = jnp.concatenate([dsb_q, dsb_k, dsb_v, dcq, dckv, dsw_q, dsw_k, dsw_v, dkr, zeros(128)], axis=-1).astype(MXU)
        dx, dproj_m, dshift1, dscale1, dg1 = _ln_mod_matmul_bwd(
            dproj, w_in_pt[l], sv["x"], row(norm1_g[l]), scale1[l], dx_mid, None, f"l{l}_in_proj_bwd")
        g_w_in_pt = _wgrad(flat(dproj_m), flat(sv["h1"]), f"l{l}_w_in_grad")

        g_w_in_t = jnp.concatenate([g_w_in_pt[:1152], g_w_in_pt[1856:1888], g_w_in_pt[1152:1792]], axis=0)
        g_w_uq_t = g_w_uq_pt.reshape(MLA_HEADS, LANES, 256)[:, :MLA_QK].reshape(MLA_HEADS * MLA_QK, 256)
        dmods[l] = jnp.concatenate([dshift1, dscale1, dgate1, dshift2, dscale2, dgate2], axis=-1).reshape(nb, 6 * d)

        conv_dev = cstats[1:4].reshape(3, N_DEV, -1).transpose(1, 0, 2)
        rest = pack_rows([per_dev(g_w_in_t), per_dev(g_w_out)],
                         [g_w_uq_t.reshape(N_DEV, -1), g_w_ukv_t.reshape(N_DEV, -1), conv_dev.reshape(N_DEV, -1)])
        rest_sib, = _pair_exchange([rest], f"l{l}_pair_exchange_rest")
        rest_pair = _pair_add(core, rest, rest_sib, f"l{l}_pair_add_rest")
        if l > 0:
            pending = (l, rest_pair)
        else:
            update_rest(l, _chip_exchange([rest_pair], f"l{l}_chip_exchange")[0])
        grads[l] = dict(
            norm1_g=dg1[0], norm2_g=dg2[0], mla_cq_g=dg_cq[0], mla_ckv_g=dg_ckv[0], mla_qn_g=dg_qn[0, :MLA_QK],
            mla_kn_g=dg_kn[0, :MLA_QK], sw_qn_g=dg_swq[0, :HEAD] + dg_swq[0, HEAD:], sw_kn_g=dg_swk[0, :HEAD] + dg_swk[0, HEAD:],
            sw_sinks=dsinks[:SW_HEADS, 0], conv_b=cstats[0])
    grad_x = dx
    g_rel = _bias_grad(dbias, bucket, "rel_table_grad")[:SW_HEADS, :REL_BUCKETS].T
    stack = lambda k: jnp.stack([grads[l][k] for l in range(nl)])

    dm_all, = _all_gather([jnp.stack(dmods).reshape(nl * nb, 6 * d)], "gather_dmods")
    dm_all = dm_all.reshape(N_DEV, nl, nb, 6 * d).transpose(1, 0, 2, 3).reshape(nl, N_DEV * nb, 6 * d)
    dm_my = lax.dynamic_slice_in_dim(dm_all, me * n_ada, n_ada, axis=2)
    g_w_ada, g_b_ada = _ada_bwd(c_all, dm_my, dm_all, "ada_bwd")
    g_b_ada = g_b_ada.reshape(nl, 6 * d)

    big_out = [{k: jnp.stack([sharded_out[l][o][k] for l in range(nl)]) for k in sharded_names} for o in range(4)]
    packf = lambda dct, names, rows: jnp.pad(jnp.concatenate([dct[k].reshape(-1) for k in names]),
                                             (0, rows * LANES - sum(dct[k].size for k in names))).reshape(rows, LANES)

    small_names = ["rel_table", "norm1_g", "norm2_g", "mla_cq_g", "mla_ckv_g", "mla_qn_g", "mla_kn_g",
                   "sw_qn_g", "sw_kn_g", "sw_sinks", "conv_b"]
    small_w = dict(rel_table=rel_table, norm1_g=norm1_g, norm2_g=norm2_g, mla_cq_g=mla_cq_g, mla_ckv_g=mla_ckv_g,
                   mla_qn_g=mla_qn_g, mla_kn_g=mla_kn_g, sw_qn_g=sw_qn_g, sw_kn_g=sw_kn_g, sw_sinks=sw_sinks, conv_b=conv_b)
    small_m = dict(rel_table=m_rel_table, norm1_g=m_norm1_g, norm2_g=m_norm2_g, mla_cq_g=m_mla_cq_g, mla_ckv_g=m_mla_ckv_g,
                   mla_qn_g=m_mla_qn_g, mla_kn_g=m_mla_kn_g, sw_qn_g=m_sw_qn_g, sw_kn_g=m_sw_kn_g, sw_sinks=m_sw_sinks, conv_b=m_conv_b)
    small_v = dict(rel_table=v_rel_table, norm1_g=v_norm1_g, norm2_g=v_norm2_g, mla_cq_g=v_mla_cq_g, mla_ckv_g=v_mla_ckv_g,
                   mla_qn_g=v_mla_qn_g, mla_kn_g=v_mla_kn_g, sw_qn_g=v_sw_qn_g, sw_kn_g=v_sw_kn_g, sw_sinks=v_sw_sinks, conv_b=v_conv_b)
    small_g = {k: (g_rel if k == "rel_table" else stack(k)) for k in small_names}
    n_small = sum(small_w[k].size for k in small_names)
    rows_small = -(-n_small // (8 * LANES)) * 8
    small_parts, = _all_gather([packf(small_g, small_names, rows_small)], "gather_small_grads")
    small_out = _adamw(small_parts, packf(small_w, small_names, rows_small), packf(small_m, small_names, rows_small),
                       packf(small_v, small_names, rows_small), "adamw_replicated")
    small_out = [dict(zip(small_names, _unpack(o.reshape(-1), [small_w[k].shape for k in small_names]))) for o in small_out]

    two_d = lambda a: a.reshape(-1, a.shape[-1])
    res_w = _adamw(two_d(g_w_ada)[None], two_d(w_ada), two_d(m_w_ada), two_d(v_w_ada), "adamw_w_ada")
    res_b = _adamw(g_b_ada[None], b_ada, m_b_ada, v_b_ada, "adamw_b_ada")
    ada_out = [dict(w_ada=rw.reshape(w_ada.shape), b_ada=rb) for rw, rb in zip(res_w, res_b)]

    order = ["rel_table", "norm1_g", "norm2_g", "w_ada", "b_ada", "w_in", "mla_cq_g", "w_uq", "mla_ckv_g", "w_ukv",
             "mla_qn_g", "mla_kn_g", "sw_qn_g", "sw_kn_g", "sw_sinks", "w_out", "w_up", "conv_w", "conv_b", "w_down"]
    outs = [{**big_out[k], **small_out[k], **ada_out[k]} for k in range(4)]
    return (loss, grad_x, *[outs[0][n] for n in order], *[outs[1][n] for n in order],
            *[outs[2][n] for n in order], *[outs[3][n] for n in order])
```

```python
import math

import jax
import jax.numpy as jnp
from jax import lax
from jax.experimental import pallas as pl
from jax.experimental.pallas import tpu as pltpu

F32 = jnp.float32
MXU = jnp.bfloat16
EPS = 1e-6
NEG = -1e30
VMEM_LIMIT_BYTES = 56 * 1024 * 1024
N_DEV = 8
MESH = pl.DeviceIdType.MESH

D_MODEL = 1024
D_FF = 2816
HEAD = 64
LANES = 128
MLA_HEADS = 6
MLA_QK = 96
SW_HEADS = 6
REL_BUCKETS = 32
BLOCK = 128
SB_SCALE = HEAD ** -0.5
SB_DEAD = -105.0
SW_SCALE = HEAD ** -0.5
MLA_SCALE = MLA_QK ** -0.5
ROPE_THETA = 10000.0
D_IN_PAD = 2048
COL_SBQ, COL_SBK, COL_SBV, COL_CQ, COL_CKV, COL_SWQ, COL_SWK, COL_SWV, COL_KR = 0, 256, 512, 768, 1024, 1152, 1536, 1664, 1792

HALO = 16
ROW_TILE_BYTES = 1 << 21
ADAM_LR, ADAM_B1, ADAM_B2, ADAM_EPS, ADAM_WD, ADAM_STEP = 0.001, 0.9, 0.999, 1e-08, 0.01, 10


def _cp(*sem):
    return pltpu.CompilerParams(dimension_semantics=sem, vmem_limit_bytes=VMEM_LIMIT_BYTES)


def _iota(shape, dim):
    return lax.broadcasted_iota(jnp.int32, shape, dim)


def _dot(a, b):
    return jnp.dot(a, b, preferred_element_type=F32)


def _dot_nt(a, b):
    return lax.dot_general(a, b, (((1,), (1,)), ((), ())), preferred_element_type=F32)


def _dot_tn(a, b):
    return lax.dot_general(a, b, (((0,), (0,)), ((), ())), preferred_element_type=F32)


def _cumdot(x, u):
    hi = x.astype(MXU)
    mid = (x - hi.astype(F32)).astype(MXU)
    return _dot(hi, u) + _dot(mid, u)


def _sigmoid(x):
    return 1.0 / (1.0 + jnp.exp(-x))


def _rsum(x):
    return jnp.sum(x, axis=-1, keepdims=True)


def _csum(x):
    return jnp.sum(x, axis=0, keepdims=True)


def _all_gather(xs, name):
    na = len(xs)

    def body(*refs):
        start, finish = _gather_steps(refs[:na], refs[na:2 * na], *refs[2 * na:])
        start()
        finish()

    hbm = pl.BlockSpec(memory_space=pl.ANY)
    return pl.pallas_call(
        body, name=name, out_shape=_gather_out_shapes(xs), in_specs=[hbm] * na, out_specs=[hbm] * na,
        scratch_shapes=_gather_sems(na))(*xs)


def _gather_out_shapes(xs):
    return [jax.ShapeDtypeStruct((N_DEV,) + a.shape, a.dtype) for a in xs]


def _gather_sems(na):
    return [pltpu.SemaphoreType.DMA((7 * na,)), pltpu.SemaphoreType.DMA((7 * na,)), pltpu.SemaphoreType.DMA((na,))]


def _gather_steps(x_refs, out_refs, send_sems, recv_sems, local_sems):
    na = len(x_refs)
    x, y, c = lax.axis_index("x"), lax.axis_index("y"), lax.axis_index("c")
    me, sibling = (x, y, c), (x, y, 1 - c)
    chips = [(1 - x, y), (x, 1 - y), (1 - x, 1 - y)]

    def slot(a, px, py, pc):
        return out_refs[a].at[4 * px + 2 * py + pc]

    def copy(a, k, block, to, src=None):
        return pltpu.make_async_remote_copy(
            src_ref=slot(a, *block) if src is None else src, dst_ref=slot(a, *block),
            send_sem=send_sems.at[7 * a + k], recv_sem=recv_sems.at[7 * a + k], device_id=to, device_id_type=MESH)

    def own_copies(a):
        return ([copy(a, 0, me, sibling, src=x_refs[a])]
                + [copy(a, 1 + j, me, (*chip, c), src=x_refs[a]) for j, chip in enumerate(chips)])

    def local_copy(a):
        return pltpu.make_async_copy(x_refs[a], slot(a, *me), local_sems.at[a])

    def start():
        for a in range(na):
            local_copy(a).start()
            for cp in own_copies(a):
                cp.start()

    def finish():
        passed = []
        for j, chip in enumerate(chips):
            for a in range(na):
                copy(a, 1 + j, (*chip, c), me).wait_recv()
                passed.append(copy(a, 4 + j, (*chip, c), sibling))
                passed[-1].start()
        for a in range(na):
            copy(a, 0, sibling, me).wait_recv()
            for j, chip in enumerate(chips):
                copy(a, 4 + j, (*chip, 1 - c), me).wait_recv()
        for a in range(na):
            for cp in own_copies(a):
                cp.wait_send()
        for cp in passed:
            cp.wait_send()
        for a in range(na):
            local_copy(a).wait()

    return start, finish


def _with_gather(body, n_in, n_out, na, grid):
    if not na:
        return body

    def wrapped(*refs):
        ins, ride_in = refs[:n_in], refs[n_in:n_in + na]
        outs = refs[n_in + na:n_in + na + n_out]
        ride_out = refs[n_in + na + n_out:n_in + 2 * na + n_out]
        ids = [pl.program_id(k) for k in range(len(grid))]
        first, last = ids[0] == 0, ids[0] == grid[0] - 1
        for k in range(1, len(grid)):
            first, last = first & (ids[k] == 0), last & (ids[k] == grid[k] - 1)
        start, finish = _gather_steps(ride_in, ride_out, *refs[n_in + 2 * na + n_out:])
        pl.when(first)(start)
        body(*ins, *outs)
        pl.when(last)(finish)

    return wrapped


def _pair_exchange(xs, name):
    na = len(xs)

    def body(*refs):
        copies = _pair_exchange_copies(refs[:na], refs[na:2 * na], *refs[2 * na:])
        for cp in copies:
            cp.start()
        for cp in copies:
            cp.wait()

    hbm = pl.BlockSpec(memory_space=pl.ANY)
    return pl.pallas_call(
        body, name=name, out_shape=_pair_exchange_out_shapes(xs), in_specs=[hbm] * na, out_specs=[hbm] * na,
        scratch_shapes=_pair_exchange_sems(na))(*xs)


def _pair_exchange_out_shapes(xs):
    return [jax.ShapeDtypeStruct((4,) + a.shape[1:], a.dtype) for a in xs]


def _pair_exchange_sems(na):
    return [pltpu.SemaphoreType.DMA((4 * na,)), pltpu.SemaphoreType.DMA((4 * na,))]


def _pair_exchange_copies(x_refs, out_refs, send_sems, recv_sems):
    x, y, c = lax.axis_index("x"), lax.axis_index("y"), lax.axis_index("c")
    return [pltpu.make_async_remote_copy(
        src_ref=x_refs[a].at[2 * q + 1 - c], dst_ref=out_refs[a].at[q],
        send_sem=send_sems.at[4 * a + q], recv_sem=recv_sems.at[4 * a + q],
        device_id=(x, y, 1 - c), device_id_type=MESH) for a in range(len(x_refs)) for q in range(4)]


def _row_tile(r, ncol):
    if r * ncol * 4 <= ROW_TILE_BYTES:
        return r
    return max(t for t in range(16, r, 16) if r % t == 0 and t * ncol * 4 <= ROW_TILE_BYTES)


def _pair_add(core, xs, sib, name):
    _, r, ncol = xs.shape
    tr = _row_tile(r, ncol)

    def body(c_ref, x_ref, s_ref, o_ref):
        o_ref[...] = (x_ref[...] + s_ref[...]).astype(MXU)

    return pl.pallas_call(
        body, name=name,
        grid_spec=pltpu.PrefetchScalarGridSpec(
            num_scalar_prefetch=1, grid=(4, r // tr),
            in_specs=[pl.BlockSpec((None, tr, ncol), lambda q, i, c_ref: (2 * q + c_ref[0], i, 0)),
                      pl.BlockSpec((None, tr, ncol), lambda q, i, c_ref: (q, i, 0))],
            out_specs=pl.BlockSpec((None, tr, ncol), lambda q, i, c_ref: (q, i, 0))),
        out_shape=jax.ShapeDtypeStruct((4, r, ncol), MXU),
        compiler_params=_cp("parallel", "parallel"))(core, xs, sib)


def _chip_exchange(xs, name):
    na = len(xs)

    def body(*refs):
        copies = _chip_exchange_copies(refs[:na], refs[na:2 * na], *refs[2 * na:])
        for cp in copies:
            cp.start()
        for cp in copies:
            cp.wait()

    hbm = pl.BlockSpec(memory_space=pl.ANY)
    return pl.pallas_call(
        body, name=name, out_shape=[jax.ShapeDtypeStruct(a.shape, a.dtype) for a in xs],
        in_specs=[hbm] * na, out_specs=[hbm] * na, scratch_shapes=_chip_exchange_sems(na))(*xs)


def _with_chip_exchange(body, n_in, n_out, na, grid, make_copies=None):
    if not na:
        return body
    make_copies = make_copies or _chip_exchange_copies

    def wrapped(*refs):
        ins, ride_in = refs[:n_in], refs[n_in:n_in + na]
        outs = refs[n_in + na:n_in + na + n_out]
        ride_out = refs[n_in + na + n_out:n_in + 2 * na + n_out]
        ids = [pl.program_id(k) for k in range(len(grid))]
        first, last = ids[0] == 0, ids[0] == grid[0] - 1
        for k in range(1, len(grid)):
            first, last = first & (ids[k] == 0), last & (ids[k] == grid[k] - 1)
        copies = make_copies(ride_in, ride_out, *refs[n_in + 2 * na + n_out:])

        @pl.when(first)
        def _():
            for cp in copies:
                cp.start()

        body(*ins, *outs)

        @pl.when(last)
        def _():
            for cp in copies:
                cp.wait()

    return wrapped


def _chip_exchange_sems(na):
    return [pltpu.SemaphoreType.DMA((3 * na,)), pltpu.SemaphoreType.DMA((3 * na,)), pltpu.SemaphoreType.DMA((na,))]


def _chip_exchange_copies(x_refs, out_refs, send_sems, recv_sems, local_sems):
    x, y, c = lax.axis_index("x"), lax.axis_index("y"), lax.axis_index("c")
    me = 2 * x + y
    copies = [pltpu.make_async_copy(x_refs[a].at[me], out_refs[a].at[me], local_sems.at[a]) for a in range(len(x_refs))]
    for k, (dx, dy) in enumerate([(1, 0), (0, 1), (1, 1)]):
        px = 1 - x if dx else x
        py = 1 - y if dy else y
        for a in range(len(x_refs)):
            copies.append(pltpu.make_async_remote_copy(
                src_ref=x_refs[a].at[2 * px + py], dst_ref=out_refs[a].at[me],
                send_sem=send_sems.at[3 * a + k], recv_sem=recv_sems.at[3 * a + k],
                device_id=(px, py, c), device_id_type=MESH))
    return copies


def _ada_fwd(c_all, w_ada, b_my, name):
    nl, d, n = w_ada.shape
    nb = c_all.shape[0]

    def body(c_ref, w_ref, b_ref, o_ref):
        cv = c_ref[...]
        sc = (cv * _sigmoid(cv)).astype(MXU)
        o_ref[...] = _dot(sc, w_ref[...].astype(MXU)) + b_ref[...]

    return pl.pallas_call(
        body, name=name, grid=(nl,),
        in_specs=[pl.BlockSpec((nb, d), lambda l: (0, 0)),
                  pl.BlockSpec((None, d, n), lambda l: (l, 0, 0)),
                  pl.BlockSpec((None, 1, n), lambda l: (l, 0, 0))],
        out_specs=pl.BlockSpec((None, nb, n), lambda l: (l, 0, 0)),
        out_shape=jax.ShapeDtypeStruct((nl, nb, n), F32),
        compiler_params=_cp("parallel"))(c_all, w_ada, b_my)


def _ada_bwd(c_all, dmods_my, dmods_all, name):
    nl, nb, n = dmods_my.shape
    d = c_all.shape[1]
    nfull = dmods_all.shape[2]

    def body(c_ref, dm_ref, da_ref, dw_ref, db_ref):
        cv = c_ref[...]
        sc = (cv * _sigmoid(cv)).astype(MXU)
        dw_ref[...] = _dot_tn(sc, dm_ref[...].astype(MXU))
        db_ref[...] = _csum(da_ref[...])

    return pl.pallas_call(
        body, name=name, grid=(nl,),
        in_specs=[pl.BlockSpec((nb, d), lambda l: (0, 0)),
                  pl.BlockSpec((None, nb, n), lambda l: (l, 0, 0)),
                  pl.BlockSpec((None, nb, nfull), lambda l: (l, 0, 0))],
        out_specs=[pl.BlockSpec((None, d, n), lambda l: (l, 0, 0)),
                   pl.BlockSpec((None, 1, nfull), lambda l: (l, 0, 0))],
        out_shape=[jax.ShapeDtypeStruct((nl, d, n), F32), jax.ShapeDtypeStruct((nl, 1, nfull), F32)],
        compiler_params=_cp("parallel"))(c_all, dmods_my, dmods_all)


def _ln_mod_matmul(x, g, scale, shift, w, name):
    nb, s, d = x.shape
    n = w.shape[0]
    tm, tn = min(1024, s), (1408 if n % 1408 == 0 else 1024)

    def body(x_ref, g_ref, sc_ref, sh_ref, w_ref, y_ref, h_ref, h_s):
        @pl.when(pl.program_id(2) == 0)
        def _():
            xf = x_ref[...]
            rstd = lax.rsqrt(jnp.mean(xf * xf, axis=-1, keepdims=True) + EPS)
            hv = (xf * rstd * g_ref[...]) * (1.0 + sc_ref[...]) + sh_ref[...]
            h_s[...] = hv.astype(MXU)
            h_ref[...] = h_s[...]

        y_ref[...] = _dot_nt(h_s[...], w_ref[...]).astype(MXU)

    return pl.pallas_call(
        body, name=name, grid=(nb, s // tm, n // tn),
        in_specs=[pl.BlockSpec((None, tm, d), lambda b, i, j: (b, i, 0)),
                  pl.BlockSpec((1, d), lambda b, i, j: (0, 0)),
                  pl.BlockSpec((None, 1, d), lambda b, i, j: (b, 0, 0)),
                  pl.BlockSpec((None, 1, d), lambda b, i, j: (b, 0, 0)),
                  pl.BlockSpec((tn, d), lambda b, i, j: (j, 0))],
        out_specs=[pl.BlockSpec((None, tm, tn), lambda b, i, j: (b, i, j)),
                   pl.BlockSpec((None, tm, d), lambda b, i, j: (b, i, 0))],
        out_shape=[jax.ShapeDtypeStruct((nb, s, n), MXU), jax.ShapeDtypeStruct((nb, s, d), MXU)],
        scratch_shapes=[pltpu.VMEM((tm, d), MXU)],
        compiler_params=_cp("parallel", "parallel", "arbitrary"))(x, g, scale, shift, w)


def _ln_mod_matmul_bwd(dy, w, x, g, scale, dres, conv_w, name, riding=()):
    nb, s, n = dy.shape
    d = x.shape[-1]
    tm, tn = min(512, s), (1408 if n % 1408 == 0 else 1024)
    ni, nj = s // tm, n // tn
    hb = tm // HALO
    conv = conv_w is not None
    na = len(riding)

    def body(*refs):
        if conv:
            dy_ref, nx_ref, cw_ref = refs[:3]
            refs = refs[3:]
        else:
            dy_ref = refs[0]
            refs = refs[1:]
        w_ref, x_ref, g_ref, sc_ref, dr_ref = refs[:5]
        ride_in, refs = refs[5:5 + na], refs[5 + na:]
        dx_ref, dyp_ref, dsh_ref, dsc_ref, dg_ref = refs[:5]
        ride_out, refs = refs[5:5 + na], refs[5 + na:]
        acc = refs[0]
        b, i, j = pl.program_id(0), pl.program_id(1), pl.program_id(2)
        if na:
            copies = _chip_exchange_copies(ride_in, ride_out, *refs[1:])

            @pl.when((b == 0) & (i == 0) & (j == 0))
            def _():
                for cp in copies:
                    cp.start()

        @pl.when(j == 0)
        def _():
            acc[...] = jnp.zeros_like(acc)

        @pl.when((j == 0) & (i == 0))
        def _():
            dsh_ref[...] = jnp.zeros_like(dsh_ref)
            dsc_ref[...] = jnp.zeros_like(dsc_ref)

        @pl.when((j == 0) & (i == 0) & (b == 0))
        def _():
            dg_ref[...] = jnp.zeros_like(dg_ref)

        dv = dy_ref[...].astype(F32)
        if conv:
            rows = _iota((tm, 1), 0)
            nx = jnp.where(i == ni - 1, 0.0, nx_ref[...].astype(F32))
            n1 = jnp.where(rows == tm - 1, nx[0:1, :], pltpu.roll(dv, tm - 1, 0))
            n2 = jnp.where(rows == tm - 2, nx[0:1, :], jnp.where(rows == tm - 1, nx[1:2, :], pltpu.roll(dv, tm - 2, 0)))
            cw = cw_ref[...]
            dv = cw[2:3, :] * dv + cw[1:2, :] * n1 + cw[0:1, :] * n2
        dp = dv.astype(MXU)
        dyp_ref[...] = dp
        acc[...] += _dot(dp, w_ref[...])

        @pl.when(j == nj - 1)
        def _():
            dh = acc[...]
            xf = x_ref[...]
            rstd = lax.rsqrt(jnp.mean(xf * xf, axis=-1, keepdims=True) + EPS)
            xn = xf * rstd
            gg = g_ref[...]
            sc1 = 1.0 + sc_ref[...]
            dsh_ref[...] += _csum(dh)
            dsc_ref[...] += _csum(dh * xn * gg)
            dg_ref[...] += _csum(dh * xn * sc1)
            dn = dh * gg * sc1
            dx_ref[...] = dr_ref[...] + rstd * (dn - xn * jnp.mean(dn * xn, axis=-1, keepdims=True))

        if na:
            @pl.when((b == nb - 1) & (i == ni - 1) & (j == nj - 1))
            def _():
                for cp in copies:
                    cp.wait()

    hbm = pl.BlockSpec(memory_space=pl.ANY)
    in_specs = [pl.BlockSpec((None, tm, tn), lambda b, i, j: (b, i, j))]
    args = [dy]
    if conv:
        in_specs += [pl.BlockSpec((None, HALO, tn), lambda b, i, j: (b, jnp.minimum((i + 1) * hb, s // HALO - 1), j)),
                     pl.BlockSpec((3, tn), lambda b, i, j: (0, j))]
        args += [dy, conv_w]
    in_specs += [pl.BlockSpec((tn, d), lambda b, i, j: (j, 0)),
                 pl.BlockSpec((None, tm, d), lambda b, i, j: (b, i, 0)),
                 pl.BlockSpec((1, d), lambda b, i, j: (0, 0)),
                 pl.BlockSpec((None, 1, d), lambda b, i, j: (b, 0, 0)),
                 pl.BlockSpec((None, tm, d), lambda b, i, j: (b, i, 0))]
    in_specs += [hbm] * na
    args += [w, x, g, scale, dres, *riding]
    return pl.pallas_call(
        body, name=name, grid=(nb, ni, nj), in_specs=in_specs,
        out_specs=[pl.BlockSpec((None, tm, d), lambda b, i, j: (b, i, 0)),
                   pl.BlockSpec((None, tm, tn), lambda b, i, j: (b, i, j)),
                   pl.BlockSpec((None, 1, d), lambda b, i, j: (b, 0, 0)),
                   pl.BlockSpec((None, 1, d), lambda b, i, j: (b, 0, 0)),
                   pl.BlockSpec((1, d), lambda b, i, j: (0, 0))] + [hbm] * na,
        out_shape=[jax.ShapeDtypeStruct((nb, s, d), F32), jax.ShapeDtypeStruct((nb, s, n), MXU),
                   jax.ShapeDtypeStruct((nb, 1, d), F32), jax.ShapeDtypeStruct((nb, 1, d), F32),
                   jax.ShapeDtypeStruct((1, d), F32)] + [jax.ShapeDtypeStruct(a.shape, a.dtype) for a in riding],
        scratch_shapes=[pltpu.VMEM((tm, d), F32)] + (_chip_exchange_sems(na) if na else []),
        compiler_params=_cp("arbitrary", "arbitrary", "arbitrary"))(*args)


def _wgrad(xm, dym, name):
    t, k = xm.shape
    n = dym.shape[1]
    tk = 1408 if k % 1408 == 0 else 1024
    tt = min(1024, t)

    def body(x_ref, dy_ref, o_ref):
        @pl.when(pl.program_id(1) == 0)
        def _():
            o_ref[...] = jnp.zeros_like(o_ref)

        o_ref[...] += _dot_tn(x_ref[...], dy_ref[...])

    return pl.pallas_call(
        body, name=name, grid=(k // tk, t // tt),
        in_specs=[pl.BlockSpec((tt, tk), lambda a, c: (c, a)),
                  pl.BlockSpec((tt, n), lambda a, c: (c, 0))],
        out_specs=pl.BlockSpec((tk, n), lambda a, c: (a, 0)),
        out_shape=jax.ShapeDtypeStruct((k, n), F32),
        compiler_params=_cp("parallel", "arbitrary"))(xm, dym)


def _out_proj(parts, ws, gate, res, name):
    nb, s, d = res.shape
    tm = min(512, s)
    npart = len(parts)

    def body(*refs):
        p_refs, w_refs = refs[:npart], refs[npart:2 * npart]
        gt_ref, res_ref, xo_ref, y_ref = refs[2 * npart:]
        y = _dot(p_refs[0][...].astype(MXU), w_refs[0][...])
        for p_ref, w_ref in zip(p_refs[1:], w_refs[1:]):
            y = y + _dot(p_ref[...].astype(MXU), w_ref[...])
        y_ref[...] = y
        xo_ref[...] = res_ref[...] + gt_ref[...] * y

    in_specs = [pl.BlockSpec((None, tm, p.shape[-1]), lambda b, i: (b, i, 0)) for p in parts]
    in_specs += [pl.BlockSpec(w.shape, lambda b, i: (0, 0)) for w in ws]
    in_specs += [pl.BlockSpec((None, 1, d), lambda b, i: (b, 0, 0)),
                 pl.BlockSpec((None, tm, d), lambda b, i: (b, i, 0))]
    return pl.pallas_call(
        body, name=name, grid=(nb, s // tm), in_specs=in_specs,
        out_specs=[pl.BlockSpec((None, tm, d), lambda b, i: (b, i, 0))] * 2,
        out_shape=[jax.ShapeDtypeStruct((nb, s, d), F32)] * 2,
        compiler_params=_cp("parallel", "parallel"))(*parts, *ws, gate, res)


def _gate_bwd_nt(dx, y, gate, ws, name):
    nb, s, d = dx.shape
    tm = min(512, s)
    npart = len(ws)

    def body(*refs):
        dx_ref, y_ref, gt_ref = refs[:3]
        w_refs = refs[3:3 + npart]
        da_refs = refs[3 + npart:3 + 2 * npart]
        dy_ref, dgt_ref = refs[3 + 2 * npart:]

        @pl.when(pl.program_id(1) == 0)
        def _():
            dgt_ref[...] = jnp.zeros_like(dgt_ref)

        dxv = dx_ref[...]
        dyv = (dxv * gt_ref[...]).astype(MXU)
        dy_ref[...] = dyv
        dgt_ref[...] += _csum(dxv * y_ref[...])
        for w_ref, da_ref in zip(w_refs, da_refs):
            da_ref[...] = _dot_nt(dyv, w_ref[...])

    tile = pl.BlockSpec((None, tm, d), lambda b, i: (b, i, 0))
    row = pl.BlockSpec((None, 1, d), lambda b, i: (b, 0, 0))
    outs = pl.pallas_call(
        body, name=name, grid=(nb, s // tm),
        in_specs=[tile, tile, row] + [pl.BlockSpec(w.shape, lambda b, i: (0, 0)) for w in ws],
        out_specs=[pl.BlockSpec((None, tm, w.shape[0]), lambda b, i: (b, i, 0)) for w in ws] + [tile, row],
        out_shape=[jax.ShapeDtypeStruct((nb, s, w.shape[0]), F32) for w in ws]
        + [jax.ShapeDtypeStruct((nb, s, d), MXU), jax.ShapeDtypeStruct((nb, 1, d), F32)],
        compiler_params=_cp("arbitrary", "arbitrary"))(dx, y, gate, *ws)
    return outs[:npart], outs[npart], outs[npart + 1]


def _conv_shifts(xv, halo, rows):
    last, before = halo[HALO - 1:HALO, :], halo[HALO - 2:HALO - 1, :]
    p1 = jnp.where(rows == 0, last, pltpu.roll(xv, 1, 0))
    p2 = jnp.where(rows == 0, before, jnp.where(rows == 1, last, pltpu.roll(xv, 2, 0)))
    return p1, p2


def _conv_gate_matmul(u, cw, cb, wd, gate, res, name):
    nb, s, f2 = u.shape
    f = f2 // 2
    d = wd.shape[1]
    tm = min(512, s)
    tk = f // 2
    nk = f // tk
    hb = tm // HALO

    def body(ug_ref, uv_ref, hg_ref, hv_ref, cwg_ref, cwv_ref, cbg_ref, cbv_ref, wd_ref, gt_ref, res_ref,
             xo_ref, y_ref, acc):
        i, k = pl.program_id(1), pl.program_id(2)

        @pl.when(k == 0)
        def _():
            acc[...] = jnp.zeros_like(acc)

        rows = _iota((tm, 1), 0)

        def conv(x_ref, h_ref, w_ref, b_ref):
            xv = x_ref[...].astype(F32)
            halo = jnp.where(i == 0, 0.0, h_ref[...].astype(F32))
            p1, p2 = _conv_shifts(xv, halo, rows)
            wv = w_ref[...]
            return wv[2:3, :] * xv + wv[1:2, :] * p1 + wv[0:1, :] * p2 + b_ref[...]

        gv = conv(ug_ref, hg_ref, cwg_ref, cbg_ref)
        vv = conv(uv_ref, hv_ref, cwv_ref, cbv_ref)
        av = gv * _sigmoid(gv) * vv
        acc[...] += _dot(av.astype(MXU), wd_ref[...])

        @pl.when(k == nk - 1)
        def _():
            y = acc[...]
            y_ref[...] = y
            xo_ref[...] = res_ref[...] + gt_ref[...] * y

    def halo_idx(off):
        return lambda b, i, k: (b, jnp.maximum(i * hb - 1, 0), k + off)

    tile = pl.BlockSpec((None, tm, d), lambda b, i, k: (b, i, 0))
    return pl.pallas_call(
        body, name=name, grid=(nb, s // tm, nk),
        in_specs=[pl.BlockSpec((None, tm, tk), lambda b, i, k: (b, i, k)),
                  pl.BlockSpec((None, tm, tk), lambda b, i, k: (b, i, k + nk)),
                  pl.BlockSpec((None, HALO, tk), halo_idx(0)),
                  pl.BlockSpec((None, HALO, tk), halo_idx(nk)),
                  pl.BlockSpec((3, tk), lambda b, i, k: (0, k)),
                  pl.BlockSpec((3, tk), lambda b, i, k: (0, k + nk)),
                  pl.BlockSpec((1, tk), lambda b, i, k: (0, k)),
                  pl.BlockSpec((1, tk), lambda b, i, k: (0, k + nk)),
                  pl.BlockSpec((tk, d), lambda b, i, k: (k, 0)),
                  pl.BlockSpec((None, 1, d), lambda b, i, k: (b, 0, 0)),
                  tile],
        out_specs=[tile, tile],
        out_shape=[jax.ShapeDtypeStruct((nb, s, d), F32)] * 2,
        scratch_shapes=[pltpu.VMEM((tm, d), F32)],
        compiler_params=_cp("parallel", "parallel", "arbitrary"))(u, u, u, u, cw, cw, cb, cb, wd, gate, res)


def _conv_gate_bwd(da, u, cw, cb, name):
    nb, s, f2 = u.shape
    f = f2 // 2
    tm = min(128, s)
    hb = tm // HALO

    def body(da_ref, u_ref, h_ref, cw_ref, cb_ref, du_ref, a_ref, st_ref):
        b, i = pl.program_id(0), pl.program_id(1)

        @pl.when((b == 0) & (i == 0))
        def _():
            st_ref[...] = jnp.zeros_like(st_ref)

        rows = _iota((tm, 1), 0)
        first = i == 0

        def conv(cs):
            xv = u_ref[:, cs].astype(F32)
            halo = jnp.where(first, 0.0, h_ref[:, cs].astype(F32))
            p1, p2 = _conv_shifts(xv, halo, rows)
            wv = cw_ref[:, cs]
            return xv, p1, p2, wv[2:3, :] * xv + wv[1:2, :] * p1 + wv[0:1, :] * p2 + cb_ref[:, cs]

        def stats(cs, du, xv, p1, p2):
            du_ref[:, cs] = du.astype(MXU)
            st_ref[0:1, cs] += _csum(du)
            st_ref[1:2, cs] += _csum(du * p2)
            st_ref[2:3, cs] += _csum(du * p1)
            st_ref[3:4, cs] += _csum(du * xv)

        for k in range(f // LANES):
            cg = slice(k * LANES, (k + 1) * LANES)
            cv = slice(f + k * LANES, f + (k + 1) * LANES)
            xg, g1, g2, gv = conv(cg)
            xv, v1, v2, vv = conv(cv)
            sg = _sigmoid(gv)
            sl = gv * sg
            a_ref[:, cg] = (sl * vv).astype(MXU)
            dav = da_ref[:, cg]
            stats(cg, dav * vv * (sg * (1.0 + gv * (1.0 - sg))), xg, g1, g2)
            stats(cv, dav * sl, xv, v1, v2)

    return pl.pallas_call(
        body, name=name, grid=(nb, s // tm),
        in_specs=[pl.BlockSpec((None, tm, f), lambda b, i: (b, i, 0)),
                  pl.BlockSpec((None, tm, f2), lambda b, i: (b, i, 0)),
                  pl.BlockSpec((None, HALO, f2), lambda b, i: (b, jnp.maximum(i * hb - 1, 0), 0)),
                  pl.BlockSpec((3, f2), lambda b, i: (0, 0)),
                  pl.BlockSpec((1, f2), lambda b, i: (0, 0))],
        out_specs=[pl.BlockSpec((None, tm, f2), lambda b, i: (b, i, 0)),
                   pl.BlockSpec((None, tm, f), lambda b, i: (b, i, 0)),
                   pl.BlockSpec((8, f2), lambda b, i: (0, 0))],
        out_shape=[jax.ShapeDtypeStruct((nb, s, f2), MXU), jax.ShapeDtypeStruct((nb, s, f), MXU),
                   jax.ShapeDtypeStruct((8, f2), F32)],
        compiler_params=_cp("arbitrary", "arbitrary"))(da, u, u, cw, cb)


def _rot(xv, lane):
    return jnp.where((lane >= 64) & (lane < 80), -pltpu.roll(xv, 112, 1),
                     jnp.where((lane >= 80) & (lane < 96), pltpu.roll(xv, 16, 1), 0.0))


def _rot_t(dv, lane):
    return jnp.where((lane >= 80) & (lane < 96), -pltpu.roll(dv, 16, 1),
                     jnp.where((lane >= 64) & (lane < 80), pltpu.roll(dv, 112, 1), 0.0))


def _mla_prep_specs(s, tm):
    def blk(width, col):
        return pl.BlockSpec((None, tm, width), lambda b, i: (b, i, col // width))

    full = lambda shape: pl.BlockSpec(shape, lambda b, i: (0, 0))
    return [blk(256, COL_CQ), blk(128, COL_CKV), blk(128, COL_KR),
            pl.BlockSpec((None, tm, LANES), lambda b, i: (b, i, 0)),
            pl.BlockSpec((None, tm, LANES), lambda b, i: (b, i, 0)),
            full((1, 256)), full((1, 128)), full((1, 128)), full((1, 128)),
            full((768, 256)), full((768, 128))]


def _mla_prep(proj, cs, sn, gcq, gckv, gqn, gkn, wuq, wukv, name):
    nb, s, _ = proj.shape
    tm = min(512, s)

    def body(cq_ref, ckv_ref, kr_ref, c_ref, s_ref, gcq_ref, gckv_ref, gqn_ref, gkn_ref, wuq_ref, wukv_ref,
             q_ref, k_ref, v_ref):
        lane = _iota((tm, LANES), 1)
        cv, sv = c_ref[...], s_ref[...]
        cq = cq_ref[...].astype(F32)
        cqn = cq * lax.rsqrt(jnp.mean(cq * cq, axis=-1, keepdims=True) + EPS) * gcq_ref[...]
        qb = _dot_nt(cqn.astype(MXU), wuq_ref[...])
        ckv = ckv_ref[...].astype(F32)
        ckvn = ckv * lax.rsqrt(jnp.mean(ckv * ckv, axis=-1, keepdims=True) + EPS) * gckv_ref[...]
        kvb = _dot_nt(ckvn.astype(MXU), wukv_ref[...])
        kr = kr_ref[...].astype(F32)
        for h in range(MLA_HEADS):
            hs = slice(h * LANES, (h + 1) * LANES)
            qh = qb[:, hs]
            qn = qh * lax.rsqrt(_rsum(qh * qh) / MLA_QK + EPS) * gqn_ref[...]
            q_ref[:, hs] = (qn * cv + _rot(qn, lane) * sv).astype(MXU)
            kc = jnp.where(lane < HEAD, kvb[:, hs], kr)
            kn = kc * lax.rsqrt(_rsum(kc * kc) / MLA_QK + EPS) * gkn_ref[...]
            k_ref[:, hs] = (kn * cv + _rot(kn, lane) * sv).astype(MXU)
        for j in range(MLA_HEADS // 2):
            va = kvb[:, (2 * j) * LANES:(2 * j + 1) * LANES]
            vb = kvb[:, (2 * j + 1) * LANES:(2 * j + 2) * LANES]
            v_ref[:, j * LANES:(j + 1) * LANES] = jnp.where(lane < HEAD, pltpu.roll(va, HEAD, 1), vb).astype(MXU)

    return pl.pallas_call(
        body, name=name, grid=(nb, s // tm), in_specs=_mla_prep_specs(s, tm),
        out_specs=[pl.BlockSpec((None, tm, 768), lambda b, i: (b, i, 0)),
                   pl.BlockSpec((None, tm, 768), lambda b, i: (b, i, 0)),
                   pl.BlockSpec((None, tm, 384), lambda b, i: (b, i, 0))],
        out_shape=[jax.ShapeDtypeStruct((nb, s, 768), MXU), jax.ShapeDtypeStruct((nb, s, 768), MXU),
                   jax.ShapeDtypeStruct((nb, s, 384), MXU)],
        compiler_params=_cp("parallel", "parallel"))(proj, proj, proj, cs, sn, gcq, gckv, gqn, gkn, wuq, wukv)


def _mla_prep_bwd(proj, cs, sn, gcq, gckv, gqn, gkn, wuq, wukv, dq, dk, dv, name):
    nb, s, _ = proj.shape
    tm = min(512, s)

    def body(cq_ref, ckv_ref, kr_ref, c_ref, s_ref, gcq_ref, gckv_ref, gqn_ref, gkn_ref, wuq_ref, wukv_ref,
             dq_ref, dk_ref, dv_ref,
             dcq_ref, dckv_ref, dkr_ref, dwuq_ref, dwukv_ref, dgcq_ref, dgckv_ref, dgqn_ref, dgkn_ref,
             dqb_s, dkvb_s):
        @pl.when((pl.program_id(0) == 0) & (pl.program_id(1) == 0))
        def _():
            for r in (dwuq_ref, dwukv_ref, dgcq_ref, dgckv_ref, dgqn_ref, dgkn_ref):
                r[...] = jnp.zeros_like(r)

        lane = _iota((tm, LANES), 1)
        cv, sv = c_ref[...], s_ref[...]
        gqn, gkn = gqn_ref[...], gkn_ref[...]
        cq = cq_ref[...].astype(F32)
        rc = lax.rsqrt(jnp.mean(cq * cq, axis=-1, keepdims=True) + EPS)
        chat = cq * rc
        cqn = (chat * gcq_ref[...]).astype(MXU)
        qb = _dot_nt(cqn, wuq_ref[...])
        ckv = ckv_ref[...].astype(F32)
        rkv = lax.rsqrt(jnp.mean(ckv * ckv, axis=-1, keepdims=True) + EPS)
        kvhat = ckv * rkv
        ckvn = (kvhat * gckv_ref[...]).astype(MXU)
        kvb = _dot_nt(ckvn, wukv_ref[...])
        kr = kr_ref[...].astype(F32)
        dgq = jnp.zeros((1, LANES), F32)
        dgk = jnp.zeros((1, LANES), F32)
        dkr = jnp.zeros((tm, LANES), F32)
        for h in range(MLA_HEADS):
            hs = slice(h * LANES, (h + 1) * LANES)
            qh = qb[:, hs]
            rq = lax.rsqrt(_rsum(qh * qh) / MLA_QK + EPS)
            qhat = qh * rq
            dqr = dq_ref[:, hs]
            dqn = dqr * cv + _rot_t(dqr * sv, lane)
            dgq = dgq + _csum(dqn * qhat)
            dyq = dqn * gqn
            dqb_s[:, hs] = (rq * (dyq - qhat * (_rsum(dyq * qhat) / MLA_QK))).astype(MXU)

            kc = jnp.where(lane < HEAD, kvb[:, hs], kr)
            rk = lax.rsqrt(_rsum(kc * kc) / MLA_QK + EPS)
            khat = kc * rk
            dkr_h = dk_ref[:, hs]
            dkn = dkr_h * cv + _rot_t(dkr_h * sv, lane)
            dgk = dgk + _csum(dkn * khat)
            dyk = dkn * gkn
            dkc = rk * (dyk - khat * (_rsum(dyk * khat) / MLA_QK))
            dkr = dkr + jnp.where(lane >= HEAD, dkc, 0.0)
            dvb = dv_ref[:, (h // 2) * LANES:(h // 2 + 1) * LANES]
            dvp = dvb if h % 2 == 1 else pltpu.roll(dvb, HEAD, 1)
            dkvb_s[:, hs] = jnp.where(lane < HEAD, dkc, dvp).astype(MXU)
        dgqn_ref[...] += dgq
        dgkn_ref[...] += dgk
        dkr_ref[...] = dkr

        dqb = dqb_s[...]
        dwuq_ref[...] += _dot_tn(dqb, cqn)
        dcqn = _dot(dqb, wuq_ref[...])
        dgcq_ref[...] += _csum(dcqn * chat)
        dyc = dcqn * gcq_ref[...]
        dcq_ref[...] = rc * (dyc - chat * jnp.mean(dyc * chat, axis=-1, keepdims=True))

        dkvb = dkvb_s[...]
        dwukv_ref[...] += _dot_tn(dkvb, ckvn)
        dckvn = _dot(dkvb, wukv_ref[...])
        dgckv_ref[...] += _csum(dckvn * kvhat)
        dykv = dckvn * gckv_ref[...]
        dckv_ref[...] = rkv * (dykv - kvhat * jnp.mean(dykv * kvhat, axis=-1, keepdims=True))

    full = lambda shape: pl.BlockSpec(shape, lambda b, i: (0, 0))
    tile = lambda width: pl.BlockSpec((None, tm, width), lambda b, i: (b, i, 0))
    return pl.pallas_call(
        body, name=name, grid=(nb, s // tm),
        in_specs=_mla_prep_specs(s, tm) + [tile(768), tile(768), tile(384)],
        out_specs=[tile(256), tile(128), tile(128), full((768, 256)), full((768, 128)),
                   full((1, 256)), full((1, 128)), full((1, 128)), full((1, 128))],
        out_shape=[jax.ShapeDtypeStruct((nb, s, 256), F32), jax.ShapeDtypeStruct((nb, s, 128), F32),
                   jax.ShapeDtypeStruct((nb, s, 128), F32),
                   jax.ShapeDtypeStruct((768, 256), F32), jax.ShapeDtypeStruct((768, 128), F32),
                   jax.ShapeDtypeStruct((1, 256), F32), jax.ShapeDtypeStruct((1, 128), F32),
                   jax.ShapeDtypeStruct((1, 128), F32), jax.ShapeDtypeStruct((1, 128), F32)],
        scratch_shapes=[pltpu.VMEM((tm, 768), MXU), pltpu.VMEM((tm, 768), MXU)],
        compiler_params=_cp("arbitrary", "arbitrary"))(
            proj, proj, proj, cs, sn, gcq, gckv, gqn, gkn, wuq, wukv, dq, dk, dv)


def _softplus(z):
    return jnp.maximum(z, 0.0) + jnp.log(1.0 + jnp.exp(-jnp.abs(z)))


def _sb_fwd(proj, name, riding=()):
    nb, s, _ = proj.shape
    tq, tk = min(256, s), min(256, s)
    ratio = tq // tk
    na = len(riding)
    grid = (nb, 2, s // tq)

    def body(q_ref, k_ref, v_ref, o_ref, ct_ref, cnt_ref):
        i = pl.program_id(2)
        lo = _iota((tq, LANES), 1) < HEAD
        qv = q_ref[...]
        q0 = jnp.where(lo, qv, 0.0).astype(MXU)
        q1 = jnp.where(lo, 0.0, qv).astype(MXU)
        usuf = (_iota((tk, tk), 0) > _iota((tk, tk), 1)).astype(MXU)
        tpos = i * tq + _iota((tq, tk), 0)
        scol = _iota((tq, tk), 1)
        nch = (i + 1) * ratio

        def alive(st):
            return (st[0] < nch) & (st[5] > SB_DEAD)

        def step(st):
            t, c0, a0, c1, a1, _ = st
            j = nch - 1 - t
            off = pl.multiple_of(j * tk, tk)
            kc = k_ref[pl.ds(off, tk), :].astype(MXU)
            vc = v_ref[pl.ds(off, tk), :].astype(MXU)
            msk = (scol + j * tk) < tpos

            def head(qm, c, a):
                z = _dot_nt(qm, kc) * SB_SCALE
                sp = _softplus(z)
                lk = jnp.where(msk, -sp, 0.0)
                w = jnp.where(msk, jnp.exp(z - sp + _cumdot(lk, usuf) + c), 0.0)
                return c + _rsum(lk), a + _dot(w.astype(MXU), vc)

            c0, a0 = head(q0, c0, a0)
            c1, a1 = head(q1, c1, a1)
            return t + 1, c0, a0, c1, a1, jnp.maximum(jnp.max(c0), jnp.max(c1))

        z1 = jnp.zeros((tq, 1), F32)
        za = jnp.zeros((tq, LANES), F32)
        t, c0, a0, c1, a1, _ = lax.while_loop(alive, step, (jnp.int32(0), z1, za, z1, za, jnp.float32(0.0)))
        o_ref[...] = jnp.where(lo, a0, a1)
        ct_ref[...] = jnp.where(lo, c0, c1)
        cnt_ref[...] = jnp.zeros((8, LANES), F32) + t.astype(F32)

    kv = lambda col: pl.BlockSpec((None, s, LANES), lambda b, p, i: (b, 0, col // LANES + p))
    tile = pl.BlockSpec((None, tq, LANES), lambda b, p, i: (b, i, p))
    hbm = pl.BlockSpec(memory_space=pl.ANY)
    return pl.pallas_call(
        _with_gather(body, 3, 3, na, grid), name=name, grid=grid,
        in_specs=[pl.BlockSpec((None, tq, LANES), lambda b, p, i: (b, i, COL_SBQ // LANES + p)),
                  kv(COL_SBK), kv(COL_SBV)] + [hbm] * na,
        out_specs=[tile, tile, pl.BlockSpec((None, None, None, 8, LANES), lambda b, p, i: (b, p, i, 0, 0))] + [hbm] * na,
        out_shape=[jax.ShapeDtypeStruct((nb, s, 256), F32)] * 2
        + [jax.ShapeDtypeStruct((nb, 2, s // tq, 8, LANES), F32)] + _gather_out_shapes(riding),
        scratch_shapes=_gather_sems(na) if na else [],
        compiler_params=_cp("arbitrary", "arbitrary", "arbitrary"))(proj, proj, proj, *riding)


def _sb_bwd(proj, ct, cnt, do, name, riding=()):
    nb, s, _ = proj.shape
    tq, tk = min(256, s), min(256, s)
    ratio = tq // tk
    na = len(riding)
    grid = (nb, 2, s // tq)

    def body(q_ref, k_ref, v_ref, ct_ref, cnt_ref, do_ref, dq_ref, dk_ref, dv_ref):
        i = pl.program_id(2)

        @pl.when(i == 0)
        def _():
            dk_ref[...] = jnp.zeros_like(dk_ref)
            dv_ref[...] = jnp.zeros_like(dv_ref)

        lane = _iota((tq, LANES), 1)
        lo = lane < HEAD
        lok = _iota((tk, LANES), 1) < HEAD
        qv, dov = q_ref[...], do_ref[...]
        qb, dob = qv.astype(MXU), dov.astype(MXU)
        q0 = jnp.where(lo, qv, 0.0).astype(MXU)
        q1 = jnp.where(lo, 0.0, qv).astype(MXU)
        do0 = jnp.where(lo, dov, 0.0).astype(MXU)
        do1 = jnp.where(lo, 0.0, dov).astype(MXU)
        ctv = ct_ref[...]
        ct0 = _rsum(jnp.where(lane == 0, ctv, 0.0))
        ct1 = _rsum(jnp.where(lane == LANES - 1, ctv, 0.0))
        uincl = (_iota((tk, tk), 0) <= _iota((tk, tk), 1)).astype(MXU)
        ustrict = (_iota((tk, tk), 0) < _iota((tk, tk), 1)).astype(MXU)
        tpos = i * tq + _iota((tq, tk), 0)
        scol = _iota((tq, tk), 1)
        nch = (i + 1) * ratio

        def step(j, carry):
            p0, g0, dq0, p1, g1, dq1 = carry
            off = pl.multiple_of(j * tk, tk)
            kc = k_ref[pl.ds(off, tk), :].astype(MXU)
            vc = v_ref[pl.ds(off, tk), :].astype(MXU)
            msk = (scol + j * tk) < tpos

            def head(qm, dom, ctot, pc, gc, dqa):
                z = _dot_nt(qm, kc) * SB_SCALE
                sp = _softplus(z)
                lk = jnp.where(msk, -sp, 0.0)
                lsig = z - sp
                w = jnp.where(msk, jnp.exp(lsig + (ctot - pc - _cumdot(lk, uincl))), 0.0)
                g = w * _dot_nt(dom, vc)
                gpre = gc + _cumdot(g, ustrict)
                sig = jnp.exp(lsig)
                dz = (jnp.where(msk, g * (1.0 - sig) - sig * gpre, 0.0) * SB_SCALE).astype(MXU)
                return (pc + _rsum(lk), gc + _rsum(g), dqa + _dot(dz, kc),
                        _dot_tn(dz, qb), _dot_tn(w.astype(MXU), dob))

            p0, g0, dq0, dk0, dv0 = head(q0, do0, ct0, p0, g0, dq0)
            p1, g1, dq1, dk1, dv1 = head(q1, do1, ct1, p1, g1, dq1)
            dk_ref[pl.ds(off, tk), :] += jnp.where(lok, dk0, dk1)
            dv_ref[pl.ds(off, tk), :] += jnp.where(lok, dv0, dv1)
            return p0, g0, dq0, p1, g1, dq1

        z1 = jnp.zeros((tq, 1), F32)
        za = jnp.zeros((tq, LANES), F32)
        first = nch - jnp.max(cnt_ref[...]).astype(jnp.int32)
        _, _, dq0, _, _, dq1 = lax.fori_loop(first, nch, step, (z1, z1, za, z1, z1, za))
        dq_ref[...] = jnp.where(lo, dq0, dq1)

    kv = lambda col: pl.BlockSpec((None, s, LANES), lambda b, p, i: (b, 0, col // LANES + p))
    tile = pl.BlockSpec((None, tq, LANES), lambda b, p, i: (b, i, p))
    acc = pl.BlockSpec((None, s, LANES), lambda b, p, i: (b, 0, p))
    hbm = pl.BlockSpec(memory_space=pl.ANY)
    return pl.pallas_call(
        _with_chip_exchange(body, 6, 3, na, grid, _pair_exchange_copies), name=name, grid=grid,
        in_specs=[pl.BlockSpec((None, tq, LANES), lambda b, p, i: (b, i, COL_SBQ // LANES + p)),
                  kv(COL_SBK), kv(COL_SBV), tile,
                  pl.BlockSpec((None, None, None, 8, LANES), lambda b, p, i: (b, p, i, 0, 0)), tile] + [hbm] * na,
        out_specs=[tile, acc, acc] + [hbm] * na,
        out_shape=[jax.ShapeDtypeStruct((nb, s, 256), F32)] * 3 + _pair_exchange_out_shapes(riding),
        scratch_shapes=_pair_exchange_sems(na) if na else [],
        compiler_params=_cp("arbitrary", "arbitrary", "arbitrary"))(proj, proj, proj, ct, cnt, do, *riding)


def _mla_fwd(q, k, v, name, riding=()):
    nb, s, _ = q.shape
    tq = tk = min(1024, s)
    na = len(riding)
    grid = (nb, MLA_HEADS // 2, s // tq)

    def body(q_ref, k_ref, v_ref, o_ref, lse_ref):
        i = pl.program_id(2)
        q0, q1 = q_ref[:, :LANES], q_ref[:, LANES:]
        krow = _iota((tk, tq), 0)
        qcol = _iota((tk, tq), 1)

        def step(j, carry, diagonal):
            m0, l0, a0, m1, l1, a1 = carry
            off = pl.multiple_of(j * tk, tk)
            vc = v_ref[pl.ds(off, tk), :]

            def head(qh, kh, m, l, a):
                st = _dot_nt(kh, qh) * MLA_SCALE
                if diagonal:
                    st = jnp.where(krow <= qcol, st, NEG)
                mn = jnp.maximum(m, jnp.max(st, axis=0, keepdims=True))
                al = jnp.exp(m - mn)
                pt = jnp.exp(st - mn)
                return mn, al * l + _csum(pt), al * a + _dot_tn(vc, pt.astype(MXU))

            m0, l0, a0 = head(q0, k_ref[pl.ds(off, tk), :LANES], m0, l0, a0)
            m1, l1, a1 = head(q1, k_ref[pl.ds(off, tk), LANES:], m1, l1, a1)
            return m0, l0, a0, m1, l1, a1

        mi = jnp.full((1, tq), NEG, F32)
        z1 = jnp.zeros((1, tq), F32)
        za = jnp.zeros((LANES, tq), F32)
        carry = lax.fori_loop(0, i, lambda j, cr: step(j, cr, False), (mi, z1, za, mi, z1, za))
        m0, l0, a0, m1, l1, a1 = step(i, carry, True)
        lo_rows = _iota((LANES, tq), 0) < HEAD
        o_ref[...] = jnp.where(lo_rows, a0 / l0, a1 / l1).T
        lse_ref[...] = jnp.zeros_like(lse_ref)
        lse_ref[0:1, :] = m0 + jnp.log(l0)
        lse_ref[1:2, :] = m1 + jnp.log(l1)

    tile = pl.BlockSpec((None, tq, LANES), lambda b, p, i: (b, i, p))
    hbm = pl.BlockSpec(memory_space=pl.ANY)
    return pl.pallas_call(
        _with_gather(body, 3, 2, na, grid), name=name, grid=grid,
        in_specs=[pl.BlockSpec((None, tq, 2 * LANES), lambda b, p, i: (b, i, p)),
                  pl.BlockSpec((None, s, 2 * LANES), lambda b, p, i: (b, 0, p)),
                  pl.BlockSpec((None, s, LANES), lambda b, p, i: (b, 0, p))] + [hbm] * na,
        out_specs=[tile, pl.BlockSpec((None, None, 8, tq), lambda b, p, i: (b, p, 0, i))] + [hbm] * na,
        out_shape=[jax.ShapeDtypeStruct((nb, s, 384), F32), jax.ShapeDtypeStruct((nb, MLA_HEADS // 2, 8, s), F32)]
        + _gather_out_shapes(riding),
        scratch_shapes=_gather_sems(na) if na else [],
        compiler_params=_cp("arbitrary", "arbitrary", "arbitrary"))(q, k, v, *riding)


def _mla_bwd(q, k, v, o, lse, do, name, riding=()):
    nb, s, _ = q.shape
    tq = tk = min(1024, s)
    na = len(riding)
    grid = (nb, MLA_HEADS // 2, s // tq)

    def body(q_ref, k_ref, v_ref, o_ref, lse_ref, do_ref, dq_ref, dk_ref, dv_ref):
        i = pl.program_id(2)

        @pl.when(i == 0)
        def _():
            dk_ref[...] = jnp.zeros_like(dk_ref)
            dv_ref[...] = jnp.zeros_like(dv_ref)

        lo = _iota((tq, LANES), 1) < HEAD
        lok = _iota((tk, LANES), 1) < HEAD
        q0, q1 = q_ref[:, :LANES], q_ref[:, LANES:]
        dov = do_ref[...]
        dob = dov.astype(MXU)
        do0 = jnp.where(lo, dov, 0.0).astype(MXU)
        do1 = jnp.where(lo, 0.0, dov).astype(MXU)
        dd = dov * o_ref[...]
        hi = dd.astype(MXU)
        r1 = dd - hi.astype(F32)
        mid = r1.astype(MXU)
        low = (r1 - mid.astype(F32)).astype(MXU)
        sel_lane = _iota((8, LANES), 1) < HEAD
        sel0 = sel_lane.astype(MXU)
        sel1 = (~sel_lane).astype(MXU)
        dl0 = (_dot_nt(sel0, hi) + _dot_nt(sel0, mid) + _dot_nt(sel0, low))[0:1, :]
        dl1 = (_dot_nt(sel1, hi) + _dot_nt(sel1, mid) + _dot_nt(sel1, low))[0:1, :]
        ls0, ls1 = lse_ref[0:1, :], lse_ref[1:2, :]
        krow = _iota((tk, tq), 0)
        qcol = _iota((tk, tq), 1)

        def step(j, carry, diagonal):
            dq0, dq1 = carry
            off = pl.multiple_of(j * tk, tk)
            vc = v_ref[pl.ds(off, tk), :]

            def head(qh, kh, dom, ls, dl, dqa):
                st = _dot_nt(kh, qh) * MLA_SCALE
                if diagonal:
                    st = jnp.where(krow <= qcol, st, NEG)
                pt = jnp.exp(st - ls)
                dst = (pt * (_dot_nt(vc, dom) - dl) * MLA_SCALE).astype(MXU)
                return dqa + _dot_tn(kh, dst), _dot(dst, qh), _dot(pt.astype(MXU), dob)

            dq0, dk0, dv0 = head(q0, k_ref[pl.ds(off, tk), :LANES], do0, ls0, dl0, dq0)
            dq1, dk1, dv1 = head(q1, k_ref[pl.ds(off, tk), LANES:], do1, ls1, dl1, dq1)
            dk_ref[pl.ds(off, tk), :LANES] += dk0
            dk_ref[pl.ds(off, tk), LANES:] += dk1
            dv_ref[pl.ds(off, tk), :] += jnp.where(lok, dv0, dv1)
            return dq0, dq1

        za = jnp.zeros((LANES, tq), F32)
        carry = lax.fori_loop(0, i, lambda j, cr: step(j, cr, False), (za, za))
        dq0, dq1 = step(i, carry, True)
        dq_ref[:, :LANES] = dq0.T
        dq_ref[:, LANES:] = dq1.T

    tile = pl.BlockSpec((None, tq, LANES), lambda b, p, i: (b, i, p))
    tile2 = pl.BlockSpec((None, tq, 2 * LANES), lambda b, p, i: (b, i, p))
    hbm = pl.BlockSpec(memory_space=pl.ANY)
    return pl.pallas_call(
        _with_chip_exchange(body, 6, 3, na, grid), name=name, grid=grid,
        in_specs=[tile2,
                  pl.BlockSpec((None, s, 2 * LANES), lambda b, p, i: (b, 0, p)),
                  pl.BlockSpec((None, s, LANES), lambda b, p, i: (b, 0, p)),
                  tile, pl.BlockSpec((None, None, 8, tq), lambda b, p, i: (b, p, 0, i)), tile] + [hbm] * na,
        out_specs=[tile2,
                   pl.BlockSpec((None, s, 2 * LANES), lambda b, p, i: (b, 0, p)),
                   pl.BlockSpec((None, s, LANES), lambda b, p, i: (b, 0, p))] + [hbm] * na,
        out_shape=[jax.ShapeDtypeStruct((nb, s, 768), F32), jax.ShapeDtypeStruct((nb, s, 768), F32),
                   jax.ShapeDtypeStruct((nb, s, 384), F32)] + [jax.ShapeDtypeStruct(a.shape, a.dtype) for a in riding],
        scratch_shapes=_chip_exchange_sems(na) if na else [],
        compiler_params=_cp("arbitrary", "arbitrary", "arbitrary"))(q, k, v, o, lse, do, *riding)


def _half_stats(xv, lo):
    x2 = xv * xv
    s0 = _rsum(jnp.where(lo, x2, 0.0))
    s1 = _rsum(jnp.where(lo, 0.0, x2))
    return jnp.where(lo, lax.rsqrt(s0 / HEAD + EPS), lax.rsqrt(s1 / HEAD + EPS))


def _half_mean(xv, lo):
    s0 = _rsum(jnp.where(lo, xv, 0.0))
    s1 = _rsum(jnp.where(lo, 0.0, xv))
    return jnp.where(lo, s0, s1) / HEAD


def _swa_in_specs():
    def band(col, prev):
        if prev:
            return pl.BlockSpec((None, BLOCK, LANES), lambda b, n: (b, jnp.maximum(n - 1, 0), col // LANES))
        return pl.BlockSpec((None, BLOCK, LANES), lambda b, n: (b, n, col // LANES))

    full = lambda shape: pl.BlockSpec(shape, lambda b, n: tuple(0 for _ in shape))
    return [pl.BlockSpec((None, BLOCK, 384), lambda b, n: (b, n, COL_SWQ // 384)),
            band(COL_SWK, False), band(COL_SWK, True), band(COL_SWV, False), band(COL_SWV, True),
            full((1, LANES)), full((1, LANES)), full((8, LANES)), full((SW_HEADS, BLOCK, 2 * BLOCK))]


def _swa_valid(n):
    a = _iota((BLOCK, 2 * BLOCK), 0)
    bcol = _iota((BLOCK, 2 * BLOCK), 1)
    dist = BLOCK + a - bcol
    return (dist >= 0) & (dist < BLOCK) & ((n > 0) | (bcol >= BLOCK))


def _swa_fwd(proj, gq, gk, sinks, bias, name):
    nb, s, _ = proj.shape

    def body(q_ref, kc_ref, kp_ref, vc_ref, vp_ref, gq_ref, gk_ref, sk_ref, bias_ref, o_ref):
        n = pl.program_id(1)
        lo = _iota((BLOCK, LANES), 1) < HEAD
        lo2 = _iota((2 * BLOCK, LANES), 1) < HEAD
        kband = jnp.concatenate([kp_ref[...], kc_ref[...]], axis=0).astype(F32)
        vband = jnp.concatenate([vp_ref[...], vc_ref[...]], axis=0).astype(F32)
        kn = kband * _half_stats(kband, lo2) * gk_ref[...]
        ks = (kn.astype(MXU), pltpu.roll(kn, HEAD, 1).astype(MXU))
        vs = (vband.astype(MXU), pltpu.roll(vband, HEAD, 1).astype(MXU))
        valid = _swa_valid(n)
        for blk in range(SW_HEADS // 2):
            qv = q_ref[:, blk * LANES:(blk + 1) * LANES].astype(F32)
            qn = qv * _half_stats(qv, lo) * gq_ref[...]
            outs = []
            for half in range(2):
                h = 2 * blk + half
                swap = 0 if half == h // 3 else 1
                qm = jnp.where(lo if half == 0 else ~lo, qn, 0.0).astype(MXU)
                sc = jnp.where(valid, _dot_nt(qm, ks[swap]) * SW_SCALE + bias_ref[h], NEG)
                sk = jnp.max(sk_ref[h:h + 1, :], axis=-1, keepdims=True)
                m = jnp.maximum(jnp.max(sc, axis=-1, keepdims=True), sk)
                p = jnp.exp(sc - m)
                l = _rsum(p) + jnp.exp(sk - m)
                outs.append(_dot((p / l).astype(MXU), vs[swap]))
            o_ref[:, blk * LANES:(blk + 1) * LANES] = jnp.where(lo, outs[0], outs[1])

    return pl.pallas_call(
        body, name=name, grid=(nb, s // BLOCK), in_specs=_swa_in_specs(),
        out_specs=pl.BlockSpec((None, BLOCK, 384), lambda b, n: (b, n, 0)),
        out_shape=jax.ShapeDtypeStruct((nb, s, 384), F32),
        compiler_params=_cp("parallel", "parallel"))(proj, proj, proj, proj, proj, gq, gk, sinks, bias)


def _swa_bwd(proj, gq, gk, sinks, bias, do, name):
    nb, s, _ = proj.shape

    def body(q_ref, kc_ref, kp_ref, vc_ref, vp_ref, gq_ref, gk_ref, sk_ref, bias_ref, do_ref,
             dq_ref, dkc_ref, dkp_ref, dvc_ref, dvp_ref, dbias_ref, dsk_ref, dgq_ref, dgk_ref):
        n = pl.program_id(1)

        @pl.when((pl.program_id(0) == 0) & (n == 0))
        def _():
            for r in (dbias_ref, dsk_ref, dgq_ref, dgk_ref):
                r[...] = jnp.zeros_like(r)

        lo = _iota((BLOCK, LANES), 1) < HEAD
        lo2 = _iota((2 * BLOCK, LANES), 1) < HEAD
        kband = jnp.concatenate([kp_ref[...], kc_ref[...]], axis=0).astype(F32)
        vband = jnp.concatenate([vp_ref[...], vc_ref[...]], axis=0).astype(F32)
        rk = _half_stats(kband, lo2)
        khat = kband * rk
        gkv = gk_ref[...]
        kn = khat * gkv
        ks = (kn.astype(MXU), pltpu.roll(kn, HEAD, 1).astype(MXU))
        vs = (vband.astype(MXU), pltpu.roll(vband, HEAD, 1).astype(MXU))
        valid = _swa_valid(n)
        dkn = jnp.zeros((2 * BLOCK, LANES), F32)
        dvb = jnp.zeros((2 * BLOCK, LANES), F32)
        gqv = gq_ref[...]
        dgq = jnp.zeros((1, LANES), F32)
        for blk in range(SW_HEADS // 2):
            bs = slice(blk * LANES, (blk + 1) * LANES)
            qv = q_ref[:, bs].astype(F32)
            rq = _half_stats(qv, lo)
            qhat = qv * rq
            qn = qhat * gqv
            dov = do_ref[:, bs]
            dqn = jnp.zeros((BLOCK, LANES), F32)
            for half in range(2):
                h = 2 * blk + half
                swap = 0 if half == h // 3 else 1
                hm = lo if half == 0 else ~lo
                qm = jnp.where(hm, qn, 0.0).astype(MXU)
                dom = jnp.where(hm, dov, 0.0).astype(MXU)
                sc = jnp.where(valid, _dot_nt(qm, ks[swap]) * SW_SCALE + bias_ref[h], NEG)
                sk = jnp.max(sk_ref[h:h + 1, :], axis=-1, keepdims=True)
                m = jnp.maximum(jnp.max(sc, axis=-1, keepdims=True), sk)
                e = jnp.exp(sc - m)
                es = jnp.exp(sk - m)
                l = _rsum(e) + es
                p = e / l
                dp = _dot_nt(dom, vs[swap])
                delta = _rsum(p * dp)
                ds = p * (dp - delta)
                dsk_ref[h:h + 1, :] += jnp.broadcast_to(_csum(-(es / l) * delta), (1, LANES))
                dbias_ref[h] += ds
                dsb = (ds * SW_SCALE).astype(MXU)
                dqn = dqn + jnp.where(hm, _dot(dsb, ks[swap]), 0.0)
                rk_ = _dot_tn(dsb, qm)
                rv_ = _dot_tn(p.astype(MXU), dom)
                if swap:
                    rk_ = pltpu.roll(rk_, HEAD, 1)
                    rv_ = pltpu.roll(rv_, HEAD, 1)
                dkn = dkn + rk_
                dvb = dvb + rv_
            dgq = dgq + _csum(dqn * qhat)
            dyq = dqn * gqv
            dq_ref[:, bs] = rq * (dyq - qhat * _half_mean(dyq * qhat, lo))
        dgq_ref[...] += dgq
        dgk_ref[...] += _csum(dkn * khat)
        dyk = dkn * gkv
        dkb = rk * (dyk - khat * _half_mean(dyk * khat, lo2))
        dkp_ref[...] = dkb[:BLOCK]
        dkc_ref[...] = dkb[BLOCK:]
        dvp_ref[...] = dvb[:BLOCK]
        dvc_ref[...] = dvb[BLOCK:]

    full = lambda shape: pl.BlockSpec(shape, lambda b, n: tuple(0 for _ in shape))
    tile = pl.BlockSpec((None, BLOCK, LANES), lambda b, n: (b, n, 0))
    tile3 = pl.BlockSpec((None, BLOCK, 384), lambda b, n: (b, n, 0))
    kvs = jax.ShapeDtypeStruct((nb, s, LANES), F32)
    return pl.pallas_call(
        body, name=name, grid=(nb, s // BLOCK), in_specs=_swa_in_specs() + [tile3],
        out_specs=[tile3, tile, tile, tile, tile, full((SW_HEADS, BLOCK, 2 * BLOCK)), full((8, LANES)),
                   full((1, LANES)), full((1, LANES))],
        out_shape=[jax.ShapeDtypeStruct((nb, s, 384), F32), kvs, kvs, kvs, kvs,
                   jax.ShapeDtypeStruct((SW_HEADS, BLOCK, 2 * BLOCK), F32), jax.ShapeDtypeStruct((8, LANES), F32),
                   jax.ShapeDtypeStruct((1, LANES), F32), jax.ShapeDtypeStruct((1, LANES), F32)],
        compiler_params=_cp("arbitrary", "arbitrary"))(proj, proj, proj, proj, proj, gq, gk, sinks, bias, do)


def _bias_build(table, bucket, name):
    def body(tb_ref, bk_ref, o_ref):
        bk = bk_ref[...]
        tb = tb_ref[...]
        row = _iota((8, LANES), 0)
        col = _iota((8, LANES), 1)
        for h in range(SW_HEADS):
            acc = jnp.zeros((BLOCK, 2 * BLOCK), F32)
            for t in range(REL_BUCKETS):
                val = jnp.sum(jnp.where((row == h) & (col == t), tb, 0.0), keepdims=True)
                acc = jnp.where(bk == t, val, acc)
            o_ref[h] = acc

    return pl.pallas_call(
        body, name=name, out_shape=jax.ShapeDtypeStruct((SW_HEADS, BLOCK, 2 * BLOCK), F32))(table, bucket)


def _bias_grad(dbias, bucket, name):
    def body(db_ref, bk_ref, o_ref):
        bk = bk_ref[...]
        row = _iota((8, LANES), 0)
        col = _iota((8, LANES), 1)
        res = jnp.zeros((8, LANES), F32)
        for h in range(SW_HEADS):
            dbh = db_ref[h]
            for t in range(REL_BUCKETS):
                val = jnp.sum(jnp.where(bk == t, dbh, 0.0), keepdims=True)
                res = jnp.where((row == h) & (col == t), val, res)
        o_ref[...] = res

    return pl.pallas_call(body, name=name, out_shape=jax.ShapeDtypeStruct((8, LANES), F32))(dbias, bucket)


def _loss_grad(y, target, name):
    nb, s, d = y.shape
    tm = min(512, s)

    def body(y_ref, t_ref, loss_ref, dy_ref):
        @pl.when((pl.program_id(0) == 0) & (pl.program_id(1) == 0))
        def _():
            loss_ref[...] = jnp.zeros_like(loss_ref)

        e = y_ref[...] - t_ref[...]
        dy_ref[...] = e / d
        loss_ref[...] += 0.5 * jnp.sum(_rsum(e * e) / d, keepdims=True)

    tile = pl.BlockSpec((None, tm, d), lambda b, i: (b, i, 0))
    return pl.pallas_call(
        body, name=name, grid=(nb, s // tm), in_specs=[tile, tile],
        out_specs=[pl.BlockSpec((8, LANES), lambda b, i: (0, 0)), tile],
        out_shape=[jax.ShapeDtypeStruct((8, LANES), F32), jax.ShapeDtypeStruct((nb, s, d), F32)],
        compiler_params=_cp("arbitrary", "arbitrary"))(y, target)


def _adamw(parts, w, m, v, name):
    npart, r, ncol = parts.shape
    tr = _row_tile(r, ncol)
    bc1 = 1.0 - ADAM_B1 ** ADAM_STEP
    bc2 = 1.0 - ADAM_B2 ** ADAM_STEP

    def body(p_ref, w_ref, m_ref, v_ref, g_ref, d_ref, nm_ref, nv_ref):
        g = p_ref[0].astype(F32)
        for k in range(1, npart):
            g = g + p_ref[k].astype(F32)
        mn = ADAM_B1 * m_ref[...] + (1.0 - ADAM_B1) * g
        vn = ADAM_B2 * v_ref[...] + (1.0 - ADAM_B2) * (g * g)
        g_ref[...] = g
        nm_ref[...] = mn
        nv_ref[...] = vn
        d_ref[...] = -ADAM_LR * ((mn / bc1) / (jnp.sqrt(vn / bc2) + ADAM_EPS) + ADAM_WD * w_ref[...])

    tile = pl.BlockSpec((tr, ncol), lambda i: (i, 0))
    return pl.pallas_call(
        body, name=name, grid=(r // tr,),
        in_specs=[pl.BlockSpec((npart, tr, ncol), lambda i: (0, i, 0)), tile, tile, tile],
        out_specs=[tile] * 4, out_shape=[jax.ShapeDtypeStruct((r, ncol), F32)] * 4,
        compiler_params=_cp("parallel"))(parts, w, m, v)


def _unpack(flat, shapes, lead=()):
    out, off = [], 0
    for shp in shapes:
        size = 1
        for dim in shp:
            size *= dim
        out.append(flat[..., off:off + size].reshape(lead + tuple(shp)))
        off += size
    return out


def _t5_bucket():
    a = jnp.arange(BLOCK)[:, None]
    b = jnp.arange(2 * BLOCK)[None, :]
    dist = BLOCK + a - b
    max_exact = REL_BUCKETS // 2
    nn = jnp.maximum(dist, 0)
    nf = jnp.maximum(nn, 1).astype(F32)
    large = max_exact + (jnp.log(nf / max_exact) / math.log(BLOCK / max_exact)
                         * (REL_BUCKETS - max_exact)).astype(jnp.int32)
    large = jnp.minimum(large, REL_BUCKETS - 1)
    return jnp.where(nn < max_exact, nn, large).astype(jnp.int32)


def _pad_lanes(g, n):
    return jnp.pad(g, (0, n - g.shape[0])).reshape(1, n)


def kernel(x, c, positions, rel_table, norm1_g, norm2_g, w_ada, b_ada, w_in, mla_cq_g, w_uq, mla_ckv_g, w_ukv, mla_qn_g, mla_kn_g, sw_qn_g, sw_kn_g, sw_sinks, w_out, w_up, conv_w, conv_b, w_down, loss_target, m_rel_table, m_norm1_g, m_norm2_g, m_w_ada, m_b_ada, m_w_in, m_mla_cq_g, m_w_uq, m_mla_ckv_g, m_w_ukv, m_mla_qn_g, m_mla_kn_g, m_sw_qn_g, m_sw_kn_g, m_sw_sinks, m_w_out, m_w_up, m_conv_w, m_conv_b, m_w_down, v_rel_table, v_norm1_g, v_norm2_g, v_w_ada, v_b_ada, v_w_in, v_mla_cq_g, v_w_uq, v_mla_ckv_g, v_w_ukv, v_mla_qn_g, v_mla_kn_g, v_sw_qn_g, v_sw_kn_g, v_sw_sinks, v_w_out, v_w_up, v_conv_w, v_conv_b, v_w_down):
    nb, s, d = x.shape
    nl = norm1_g.shape[0]
    me = 4 * lax.axis_index("x") + 2 * lax.axis_index("y") + lax.axis_index("c")
    n_ada = w_ada.shape[2]

    shard = lambda w, l, transposed: (jnp.swapaxes(w[l], 0, 1) if transposed else w[l]).astype(MXU)
    attn_local = lambda l: [shard(w_in, l, True), shard(w_uq, l, True), shard(w_ukv, l, True), shard(w_out, l, False)]
    ffn_local = lambda l: [shard(w_up, l, True), shard(w_down, l, False)]
    full = lambda a: a.reshape(-1, a.shape[-1])
    zrows = lambda n: jnp.zeros((n, d), MXU)
    pad_in = lambda wt: jnp.concatenate([wt[:1152], wt[1184:1824], zrows(64), wt[1152:1184], zrows(160)], axis=0)
    pad_uq = lambda wt: jnp.pad(wt.reshape(MLA_HEADS, MLA_QK, 256), ((0, 0), (0, LANES - MLA_QK), (0, 0))).reshape(768, 256)
    got = _all_gather(attn_local(0) + [conv_w.reshape(-1, conv_w.shape[-1]), c], "gather_inputs")
    w_in_pt, w_uq_pt, w_ukv_t, w_out_f = [pad_in(full(got[0]))], [pad_uq(full(got[1]))], [full(got[2])], [full(got[3])]
    w_up_t, w_down_f = [], []
    conv_full = got[4].reshape(N_DEV, nl, 3, -1).transpose(1, 2, 0, 3).reshape(nl, 3, -1)
    c_all = got[5].reshape(N_DEV * nb, d)

    b_my = lax.dynamic_slice_in_dim(b_ada, me * n_ada, n_ada, axis=1).reshape(nl, 1, n_ada)
    mods_my = _ada_fwd(c_all, w_ada, b_my, "ada_fwd")
    mods, = _all_gather([mods_my.reshape(nl * N_DEV * nb, n_ada)], "gather_mods")
    mods = mods.reshape(N_DEV, nl, N_DEV * nb, n_ada).transpose(1, 2, 0, 3).reshape(nl, N_DEV * nb, N_DEV * n_ada)
    mods = lax.dynamic_slice_in_dim(mods, me * nb, nb, axis=1)
    shift1, scale1, gate1, shift2, scale2, gate2 = [mods[:, :, k * d:(k + 1) * d].reshape(nl, nb, 1, d) for k in range(6)]

    half = 16
    inv_freq = jnp.power(ROPE_THETA, -jnp.arange(half, dtype=F32) / half)
    ang = positions.astype(F32)[..., None] * inv_freq
    ones = lambda n: jnp.ones((nb, s, n), F32)
    zeros = lambda n: jnp.zeros((nb, s, n), F32)
    rope_c = jnp.concatenate([ones(64), jnp.cos(ang), jnp.cos(ang), ones(32)], axis=-1)
    rope_s = jnp.concatenate([zeros(64), jnp.sin(ang), jnp.sin(ang), zeros(32)], axis=-1)
    bucket = _t5_bucket()
    bias = _bias_build(jnp.pad(rel_table.T, ((0, 8 - SW_HEADS), (0, LANES - REL_BUCKETS))), bucket, "rel_bias")

    row = lambda g: g.reshape(1, -1)
    twice = lambda g: jnp.concatenate([g, g]).reshape(1, LANES)

    saved = []
    xl = x
    for l in range(nl):
        proj, h1 = _ln_mod_matmul(xl, row(norm1_g[l]), scale1[l], shift1[l], w_in_pt[l], f"l{l}_in_proj")
        prep_args = (proj, rope_c, rope_s, row(mla_cq_g[l]), row(mla_ckv_g[l]), _pad_lanes(mla_qn_g[l], LANES),
                     _pad_lanes(mla_kn_g[l], LANES), w_uq_pt[l], w_ukv_t[l])
        qm, km, vm = _mla_prep(*prep_args, f"l{l}_mla_prep")
        o_a, ct_a, cnt_a, up_g, down_g = _sb_fwd(proj, f"l{l}_sb_fwd", riding=ffn_local(l))
        w_up_t.append(full(up_g))
        w_down_f.append(full(down_g))
        o_b, lse_b, *nxt = _mla_fwd(qm, km, vm, f"l{l}_mla_fwd", riding=attn_local(l + 1) if l + 1 < nl else ())
        if nxt:
            w_in_pt.append(pad_in(full(nxt[0])))
            w_uq_pt.append(pad_uq(full(nxt[1])))
            w_ukv_t.append(full(nxt[2]))
            w_out_f.append(full(nxt[3]))
        sinks = jnp.broadcast_to(jnp.pad(sw_sinks[l], (0, 2))[:, None], (8, LANES))
        swa_args = (proj, twice(sw_qn_g[l]), twice(sw_kn_g[l]), sinks, bias)
        o_c = _swa_fwd(*swa_args, f"l{l}_swa_fwd")
        wo = [w_out_f[l][:256], w_out_f[l][256:640], w_out_f[l][640:]]
        x_mid, y1 = _out_proj([o_a, o_b, o_c], wo, gate1[l], xl, f"l{l}_out_proj")
        u_pre, h2 = _ln_mod_matmul(x_mid, row(norm2_g[l]), scale2[l], shift2[l], w_up_t[l], f"l{l}_up_proj")
        x_out, y2 = _conv_gate_matmul(u_pre, conv_full[l], row(conv_b[l]), w_down_f[l], gate2[l], x_mid, f"l{l}_ffn_down")
        saved.append(dict(x=xl, proj=proj, h1=h1, prep=prep_args, qkv=(qm, km, vm), o_a=o_a, ct_a=ct_a, cnt_a=cnt_a, o_b=o_b, lse_b=lse_b,
                          swa=swa_args, o_c=o_c, wo=wo, y1=y1, x_mid=x_mid, u_pre=u_pre, h2=h2, y2=y2))
        xl = x_out

    loss_blk, dx = _loss_grad(xl, loss_target, "loss")
    loss = lax.psum(loss_blk[0, 0], ("x", "y", "c"))

    t = nb * s
    flat = lambda a: a.reshape(t, a.shape[-1])
    grads = [None] * nl
    dmods = [None] * nl
    sharded_out = [None] * nl
    sharded_names = ["w_in", "w_uq", "w_ukv", "w_up", "w_out", "w_down", "conv_w"]
    sharded_wmv = dict(w_in=(w_in, m_w_in, v_w_in), w_uq=(w_uq, m_w_uq, v_w_uq), w_ukv=(w_ukv, m_w_ukv, v_w_ukv),
                       w_up=(w_up, m_w_up, v_w_up), w_out=(w_out, m_w_out, v_w_out), w_down=(w_down, m_w_down, v_w_down),
                       conv_w=(conv_w, m_conv_w, v_conv_w))
    n_in, n_up, n_out, n_dn = w_in.shape[2], w_up.shape[2], w_out.shape[1], w_down.shape[1]
    small_sizes = [w_uq[0].size, w_ukv[0].size, conv_w[0].size]
    n_small_rows = -(-sum(small_sizes) // d)
    rows_used = n_in + n_out + n_small_rows
    rows_grad = -(-rows_used // 16) * 16

    def pack_rows(mats, vecs):
        lead = mats[0].shape[:-2]
        flat_part = jnp.concatenate(vecs, axis=-1)
        flat_part = jnp.pad(flat_part, [(0, 0)] * len(lead) + [(0, n_small_rows * d - flat_part.shape[-1])])
        tail = jnp.zeros(lead + (rows_grad - rows_used, d), F32)
        return jnp.concatenate(list(mats) + [flat_part.reshape(lead + (n_small_rows, d)), tail], axis=-2)

    def unpack_rows(a):
        o1, o2 = n_in, n_in + n_out
        flat_part = a[o2:o2 + n_small_rows].reshape(-1)
        s1, s2, s3 = small_sizes[0], small_sizes[0] + small_sizes[1], sum(small_sizes)
        return dict(w_in=a[:o1].T, w_out=a[o1:o2],
                    w_uq=flat_part[:s1].reshape(w_uq.shape[2], -1).T, w_ukv=flat_part[s1:s2].reshape(w_ukv.shape[2], -1).T,
                    conv_w=flat_part[s2:s3].reshape(conv_w.shape[1:]))

    ffn_out = [None] * nl
    core = lax.axis_index("c").reshape(1).astype(jnp.int32)

    def update_ffn(l, recv):
        wmv = [{k: v[o][l] for k, v in sharded_wmv.items()} for o in range(3)]
        res_up = _adamw(recv[0], *[a["w_up"].T for a in wmv], f"l{l}_adamw_up")
        res_dn = _adamw(recv[1], *[a["w_down"] for a in wmv], f"l{l}_adamw_down")
        ffn_out[l] = [dict(w_up=ru.T, w_down=rd) for ru, rd in zip(res_up, res_dn)]

    def update_rest(l, recv):
        wmv = [{k: v[o][l] for k, v in sharded_wmv.items()} for o in range(3)]
        res_rest = _adamw(recv, *[pack_rows([a["w_in"].T, a["w_out"]], [a["w_uq"].T.reshape(-1), a["w_ukv"].T.reshape(-1),
                                                                         a["conv_w"].reshape(-1)]) for a in wmv],
                          f"l{l}_adamw_rest")
        sharded_out[l] = [dict(unpack_rows(rr), **ff) for rr, ff in zip(res_rest, ffn_out[l])]

    pending = None
    dbias = jnp.zeros((SW_HEADS, BLOCK, 2 * BLOCK), F32)
    for l in reversed(range(nl)):
        sv = saved[l]
        (da,), dy2, dgate2 = _gate_bwd_nt(dx, sv["y2"], gate2[l], [w_down_f[l]], f"l{l}_ffn_down_bwd")
        du, a_act, cstats = _conv_gate_bwd(da, sv["u_pre"], conv_full[l], row(conv_b[l]), f"l{l}_conv_gate_bwd")
        res = _ln_mod_matmul_bwd(du, w_up_t[l], sv["x_mid"], row(norm2_g[l]), scale2[l], dx, conv_full[l],
                                 f"l{l}_up_proj_bwd", riding=[pending[1]] if pending else ())
        dx_mid, du_pre, dshift2, dscale2, dg2 = res[:5]
        if pending:
            update_rest(pending[0], res[5])
            pending = None
        g_w_down = _wgrad(flat(a_act), flat(dy2), f"l{l}_w_down_grad")
        g_w_up_t = _wgrad(flat(du_pre), flat(sv["h2"]), f"l{l}_w_up_grad")
        per_dev = lambda g: g.reshape(N_DEV, -1, d)
        ffn_send = [per_dev(g_w_up_t), per_dev(g_w_down)]

        (do_a, do_b, do_c), dy1, dgate1 = _gate_bwd_nt(dx_mid, sv["y1"], gate1[l], sv["wo"], f"l{l}_out_proj_bwd")
        mix = jnp.concatenate([sv["o_a"], sv["o_b"], sv["o_c"]], axis=-1).astype(MXU)
        g_w_out = _wgrad(flat(mix), flat(dy1), f"l{l}_w_out_grad")

        dsb_q, dsb_k, dsb_v, *ffn_sib = _sb_bwd(sv["proj"], sv["ct_a"], sv["cnt_a"], do_a, f"l{l}_sb_bwd", riding=ffn_send)
        ffn_pair = [_pair_add(core, a, b, f"l{l}_pair_add_{k}") for a, b, k in zip(ffn_send, ffn_sib, ("up", "down"))]
        qm, km, vm = sv["qkv"]
        dqm, dkm, dvm, *ffn_recv = _mla_bwd(qm, km, vm, sv["o_b"], sv["lse_b"], do_b, f"l{l}_mla_bwd", riding=ffn_pair)
        update_ffn(l, ffn_recv)
        dsw_q, dkc, dkp, dvc, dvp, dbias_l, dsinks, dg_swq, dg_swk = _swa_bwd(*sv["swa"], do_c, f"l{l}_swa_bwd")
        dbias = dbias + dbias_l
        shift_up = lambda a: jnp.concatenate([a[:, BLOCK:], jnp.zeros((nb, BLOCK, LANES), F32)], axis=1)
        dsw_k = dkc + shift_up(dkp)
        dsw_v = dvc + shift_up(dvp)
        dcq, dckv, dkr, g_w_uq_pt, g_w_ukv_t, dg_cq, dg_ckv, dg_qn, dg_kn = _mla_prep_bwd(
            *sv["prep"], dqm, dkm, dvm, f"l{l}_mla_prep_bwd")
        dproj = jnp.concatenate([dsb_q, dsb_k, dsb_v, dcq, dckv, dsw_q, dsw_k, dsw_v, dkr, zeros(128)], axis=-1).astype(MXU)
        dx, dproj_m, dshift1, dscale1, dg1 = _ln_mod_matmul_bwd(
            dproj, w_in_pt[l], sv["x"], row(norm1_g[l]), scale1[l], dx_mid, None, f"l{l}_in_proj_bwd")
        g_w_in_pt = _wgrad(flat(dproj_m), flat(sv["h1"]), f"l{l}_w_in_grad")

        g_w_in_t = jnp.concatenate([g_w_in_pt[:1152], g_w_in_pt[1856:1888], g_w_in_pt[1152:1792]], axis=0)
        g_w_uq_t = g_w_uq_pt.reshape(MLA_HEADS, LANES, 256)[:, :MLA_QK].reshape(MLA_HEADS * MLA_QK, 256)
        dmods[l] = jnp.concatenate([dshift1, dscale1, dgate1, dshift2, dscale2, dgate2], axis=-1).reshape(nb, 6 * d)

        conv_dev = cstats[1:4].reshape(3, N_DEV, -1).transpose(1, 0, 2)
        rest = pack_rows([per_dev(g_w_in_t), per_dev(g_w_out)],
                         [g_w_uq_t.reshape(N_DEV, -1), g_w_ukv_t.reshape(N_DEV, -1), conv_dev.reshape(N_DEV, -1)])
        rest_sib, = _pair_exchange([rest], f"l{l}_pair_exchange_rest")
        rest_pair = _pair_add(core, rest, rest_sib, f"l{l}_pair_add_rest")
        if l > 0:
            pending = (l, rest_pair)
        else:
            update_rest(l, _chip_exchange([rest_pair], f"l{l}_chip_exchange")[0])
        grads[l] = dict(
            norm1_g=dg1[0], norm2_g=dg2[0], mla_cq_g=dg_cq[0], mla_ckv_g=dg_ckv[0], mla_qn_g=dg_qn[0, :MLA_QK],
            mla_kn_g=dg_kn[0, :MLA_QK], sw_qn_g=dg_swq[0, :HEAD] + dg_swq[0, HEAD:], sw_kn_g=dg_swk[0, :HEAD] + dg_swk[0, HEAD:],
            sw_sinks=dsinks[:SW_HEADS, 0], conv_b=cstats[0])
    grad_x = dx
    g_rel = _bias_grad(dbias, bucket, "rel_table_grad")[:SW_HEADS, :REL_BUCKETS].T
    stack = lambda k: jnp.stack([grads[l][k] for l in range(nl)])

    dm_all, = _all_gather([jnp.stack(dmods).reshape(nl * nb, 6 * d)], "gather_dmods")
    dm_all = dm_all.reshape(N_DEV, nl, nb, 6 * d).transpose(1, 0, 2, 3).reshape(nl, N_DEV * nb, 6 * d)
    dm_my = lax.dynamic_slice_in_dim(dm_all, me * n_ada, n_ada, axis=2)
    g_w_ada, g_b_ada = _ada_bwd(c_all, dm_my, dm_all, "ada_bwd")
    g_b_ada = g_b_ada.reshape(nl, 6 * d)

    big_out = [{k: jnp.stack([sharded_out[l][o][k] for l in range(nl)]) for k in sharded_names} for o in range(4)]
    packf = lambda dct, names, rows: jnp.pad(jnp.concatenate([dct[k].reshape(-1) for k in names]),
                                             (0, rows * LANES - sum(dct[k].size for k in names))).reshape(rows, LANES)

    small_names = ["rel_table", "norm1_g", "norm2_g", "mla_cq_g", "mla_ckv_g", "mla_qn_g", "mla_kn_g",
                   "sw_qn_g", "sw_kn_g", "sw_sinks", "conv_b"]
    small_w = dict(rel_table=rel_table, norm1_g=norm1_g, norm2_g=norm2_g, mla_cq_g=mla_cq_g, mla_ckv_g=mla_ckv_g,
                   mla_qn_g=mla_qn_g, mla_kn_g=mla_kn_g, sw_qn_g=sw_qn_g, sw_kn_g=sw_kn_g, sw_sinks=sw_sinks, conv_b=conv_b)
    small_m = dict(rel_table=m_rel_table, norm1_g=m_norm1_g, norm2_g=m_norm2_g, mla_cq_g=m_mla_cq_g, mla_ckv_g=m_mla_ckv_g,
                   mla_qn_g=m_mla_qn_g, mla_kn_g=m_mla_kn_g, sw_qn_g=m_sw_qn_g, sw_kn_g=m_sw_kn_g, sw_sinks=m_sw_sinks, conv_b=m_conv_b)
    small_v = dict(rel_table=v_rel_table, norm1_g=v_norm1_g, norm2_g=v_norm2_g, mla_cq_g=v_mla_cq_g, mla_ckv_g=v_mla_ckv_g,
                   mla_qn_g=v_mla_qn_g, mla_kn_g=v_mla_kn_g, sw_qn_g=v_sw_qn_g, sw_kn_g=v_sw_kn_g, sw_sinks=v_sw_sinks, conv_b=v_conv_b)
    small_g = {k: (g_rel if k == "rel_table" else stack(k)) for k in small_names}
    n_small = sum(small_w[k].size for k in small_names)
    rows_small = -(-n_small // (8 * LANES)) * 8
    small_parts, = _all_gather([packf(small_g, small_names, rows_small)], "gather_small_grads")
    small_out = _adamw(small_parts, packf(small_w, small_names, rows_small), packf(small_m, small_names, rows_small),
                       packf(small_v, small_names, rows_small), "adamw_replicated")
    small_out = [dict(zip(small_names, _unpack(o.reshape(-1), [small_w[k].shape for k in small_names]))) for o in small_out]

    two_d = lambda a: a.reshape(-1, a.shape[-1])
    res_w = _adamw(two_d(g_w_ada)[None], two_d(w_ada), two_d(m_w_ada), two_d(v_w_ada), "adamw_w_ada")
    res_b = _adamw(g_b_ada[None], b_ada, m_b_ada, v_b_ada, "adamw_b_ada")
    ada_out = [dict(w_ada=rw.reshape(w_ada.shape), b_ada=rb) for rw, rb in zip(res_w, res_b)]

    order = ["rel_table", "norm1_g", "norm2_g", "w_ada", "b_ada", "w_in", "mla_cq_g", "w_uq", "mla_ckv_g", "w_ukv",
             "mla_qn_g", "mla_kn_g", "sw_qn_g", "sw_kn_g", "sw_sinks", "w_out", "w_up", "conv_w", "conv_b", "w_down"]
    outs = [{**big_out[k], **small_out[k], **ada_out[k]} for k in range(4)]
    return (loss, grad_x, *[outs[0][n] for n in order], *[outs[1][n] for n in order],
            *[outs[2][n] for n in order], *[outs[3][n] for n in order])
```
